```python
import math
import jax, jax.numpy as jnp
from jax import lax
import numpy as np

D_MODEL = 1024
BATCH = 8
SEQ = 8192
DEPTH = 1

D_MIX = D_MODEL
D_SSM = D_MIX // 2
D_CONV = D_MIX - D_SSM
SSM_GROUP = 16
N_SSM_GROUPS = D_SSM // SSM_GROUP
SSM_STATE = 64
CONV_HEAD_DIM = 64
N_CONV_HEADS = D_CONV // CONV_HEAD_DIM
CONV_WIDTH = 3
D_IN_PROJ = D_SSM + 3 * D_CONV
D_FF = 4 * D_MODEL
RMS_EPS = 1e-6
DT_MIN = 1e-3
DT_MAX = 1e-1

kernel_name = "hymba_s5_shortconv_sandwich_block"


def rms_norm(x, g):
    xf = x.astype(jnp.float32)
    y = xf * lax.rsqrt(jnp.mean(xf * xf, axis=-1, keepdims=True) + RMS_EPS)
    return (y * g.astype(jnp.float32)).astype(x.dtype)


def _scan_combine(e1, e2):
    a1r, a1i, b1r, b1i = e1
    a2r, a2i, b2r, b2i = e2
    ar = a2r * a1r - a2i * a1i
    ai = a2r * a1i + a2i * a1r
    br = a2r * b1r - a2i * b1i + b2r
    bi = a2r * b1i + a2i * b1r + b2i
    return (ar, ai, br, bi)


def s5_group_mixer(u, lam_re, lam_im, log_dt, b_re, b_im, c_re, c_im, d_skip, w_glu):
    bsz, seq, _ = u.shape
    uf = u.astype(jnp.float32).reshape(bsz, seq, N_SSM_GROUPS, SSM_GROUP)
    lr = lam_re.astype(jnp.float32)
    li = lam_im.astype(jnp.float32)
    dt = jnp.exp(log_dt.astype(jnp.float32))[:, None]
    mag = jnp.exp(lr * dt)
    abr = mag * jnp.cos(li * dt)
    abi = mag * jnp.sin(li * dt)
    nr, ni = abr - 1.0, abi
    den = lr * lr + li * li
    coef_r = (nr * lr + ni * li) / den
    coef_i = (ni * lr - nr * li) / den
    br_ = b_re.astype(jnp.float32)
    bi_ = b_im.astype(jnp.float32)
    bbar_r = coef_r[..., None] * br_ - coef_i[..., None] * bi_
    bbar_i = coef_r[..., None] * bi_ + coef_i[..., None] * br_
    bu_r = jnp.einsum('blgh,gph->blgp', uf, bbar_r)
    bu_i = jnp.einsum('blgh,gph->blgp', uf, bbar_i)
    a_r = jnp.broadcast_to(abr, bu_r.shape)
    a_i = jnp.broadcast_to(abi, bu_i.shape)
    _, _, xr, xi = lax.associative_scan(_scan_combine, (a_r, a_i, bu_r, bu_i), axis=1)
    y = (jnp.einsum('blgp,ghp->blgh', xr, c_re.astype(jnp.float32))
         - jnp.einsum('blgp,ghp->blgh', xi, c_im.astype(jnp.float32)))
    y = y + d_skip.astype(jnp.float32) * uf
    y = jax.nn.gelu(y.reshape(bsz, seq, D_SSM))
    y = y * jax.nn.sigmoid(y @ w_glu.astype(jnp.float32))
    return y.astype(u.dtype)


def short_conv_mixer(h, b_gate, c_gate, conv_w):
    z = c_gate * h
    zp = jnp.pad(z, ((0, 0), (CONV_WIDTH - 1, 0), (0, 0)))
    conv = (conv_w[0] * zp[:, :-2] + conv_w[1] * zp[:, 1:-1] + conv_w[2] * zp[:, 2:])
    return b_gate * conv


def _fwd_setup_inputs(seed: int = 0) -> dict:
    key = jax.random.key(seed)
    ks = jax.random.split(key, 24)
    f32 = jnp.float32
    L = DEPTH
    x = jax.random.normal(ks[0], (BATCH, SEQ, D_MODEL), f32)

    def gain(k, n):
        return 1.0 + 0.01 * jax.random.normal(k, (L, n), f32)

    n_idx = jnp.arange(SSM_STATE, dtype=f32)
    lam_re = -0.5 + 0.01 * jax.random.normal(ks[1], (L, N_SSM_GROUPS, SSM_STATE), f32)
    lam_im = math.pi * n_idx + 0.01 * jax.random.normal(ks[2], (L, N_SSM_GROUPS, SSM_STATE), f32)
    log_dt = jax.random.uniform(ks[3], (L, N_SSM_GROUPS), f32, math.log(DT_MIN), math.log(DT_MAX))
    b_scale = (2.0 * SSM_GROUP) ** -0.5
    c_scale = (2.0 * SSM_STATE) ** -0.5
    return {
        "x": x,
        "g_pre_mix": gain(ks[4], D_MODEL),
        "w_in": jax.random.normal(ks[5], (L, D_MODEL, D_IN_PROJ), f32) * D_MODEL ** -0.5,
        "lam_re": lam_re,
        "lam_im": lam_im,
        "log_dt": log_dt,
        "b_re": jax.random.normal(ks[6], (L, N_SSM_GROUPS, SSM_STATE, SSM_GROUP), f32) * b_scale,
        "b_im": jax.random.normal(ks[7], (L, N_SSM_GROUPS, SSM_STATE, SSM_GROUP), f32) * b_scale,
        "c_re": jax.random.normal(ks[8], (L, N_SSM_GROUPS, SSM_GROUP, SSM_STATE), f32) * c_scale,
        "c_im": jax.random.normal(ks[9], (L, N_SSM_GROUPS, SSM_GROUP, SSM_STATE), f32) * c_scale,
        "d_skip": jax.random.normal(ks[10], (L, N_SSM_GROUPS, SSM_GROUP), f32),
        "w_glu": jax.random.normal(ks[11], (L, D_SSM, D_SSM), f32) * D_SSM ** -0.5,
        "conv_w": jax.random.normal(ks[12], (L, CONV_WIDTH, D_CONV), f32) * CONV_WIDTH ** -0.5,
        "g_ssm_out": gain(ks[13], D_SSM),
        "g_conv_out": gain(ks[14], D_CONV),
        "w_out": jax.random.normal(ks[15], (L, D_MIX, D_MODEL), f32) * D_MIX ** -0.5,
        "g_post_mix": gain(ks[16], D_MODEL),
        "g_pre_mlp": gain(ks[17], D_MODEL),
        "w_up": jax.random.normal(ks[18], (L, D_MODEL, D_FF), f32) * D_MODEL ** -0.5,
        "w_down": jax.random.normal(ks[19], (L, D_FF, D_MODEL), f32) * D_FF ** -0.5,
        "g_post_mlp": gain(ks[20], D_MODEL),
    }


def _fwd_reference(x, g_pre_mix, w_in, lam_re, lam_im, log_dt, b_re, b_im, c_re, c_im, d_skip,
              w_glu, conv_w, g_ssm_out, g_conv_out, w_out, g_post_mix, g_pre_mlp, w_up,
              w_down, g_post_mlp):
    for i in range(DEPTH):
        hn = rms_norm(x, g_pre_mix[i])
        proj = hn @ w_in[i]
        u_ssm = proj[..., :D_SSM]
        h_conv = proj[..., D_SSM:D_SSM + D_CONV]
        b_gate = proj[..., D_SSM + D_CONV:D_SSM + 2 * D_CONV]
        c_gate = proj[..., D_SSM + 2 * D_CONV:]
        y_ssm = s5_group_mixer(u_ssm, lam_re[i], lam_im[i], log_dt[i], b_re[i], b_im[i],
                               c_re[i], c_im[i], d_skip[i], w_glu[i])
        y_conv = short_conv_mixer(h_conv, b_gate, c_gate, conv_w[i])
        y = jnp.concatenate([rms_norm(y_ssm, g_ssm_out[i]),
                             rms_norm(y_conv, g_conv_out[i])], axis=-1)
        x = x + rms_norm(y @ w_out[i], g_post_mix[i])
        hn = rms_norm(x, g_pre_mlp[i])
        m = jnp.square(jax.nn.relu(hn @ w_up[i])) @ w_down[i]
        x = x + rms_norm(m, g_post_mlp[i])
    return x


import jax as _jax
import jax.numpy as _jnp

TWIN_FORMAT = 'train_step'
FWD_PARAMS = ['x', 'g_pre_mix', 'w_in', 'lam_re', 'lam_im', 'log_dt', 'b_re', 'b_im', 'c_re', 'c_im', 'd_skip', 'w_glu', 'conv_w', 'g_ssm_out', 'g_conv_out', 'w_out', 'g_post_mix', 'g_pre_mlp', 'w_up', 'w_down', 'g_post_mlp']
TWIN_WEIGHTS = ['g_pre_mix', 'w_in', 'lam_re', 'lam_im', 'log_dt', 'b_re', 'b_im', 'c_re', 'c_im', 'd_skip', 'w_glu', 'conv_w', 'g_ssm_out', 'g_conv_out', 'w_out', 'g_post_mix', 'g_pre_mlp', 'w_up', 'w_down', 'g_post_mlp']
TWIN_DIFF_INPUT = 'x'
TWIN_INPUTS = ['x', 'g_pre_mix', 'w_in', 'lam_re', 'lam_im', 'log_dt', 'b_re', 'b_im', 'c_re', 'c_im', 'd_skip', 'w_glu', 'conv_w', 'g_ssm_out', 'g_conv_out', 'w_out', 'g_post_mix', 'g_pre_mlp', 'w_up', 'w_down', 'g_post_mlp', 'loss_target', 'm_g_pre_mix', 'm_w_in', 'm_lam_re', 'm_lam_im', 'm_log_dt', 'm_b_re', 'm_b_im', 'm_c_re', 'm_c_im', 'm_d_skip', 'm_w_glu', 'm_conv_w', 'm_g_ssm_out', 'm_g_conv_out', 'm_w_out', 'm_g_post_mix', 'm_g_pre_mlp', 'm_w_up', 'm_w_down', 'm_g_post_mlp', 'v_g_pre_mix', 'v_w_in', 'v_lam_re', 'v_lam_im', 'v_log_dt', 'v_b_re', 'v_b_im', 'v_c_re', 'v_c_im', 'v_d_skip', 'v_w_glu', 'v_conv_w', 'v_g_ssm_out', 'v_g_conv_out', 'v_w_out', 'v_g_post_mix', 'v_g_pre_mlp', 'v_w_up', 'v_w_down', 'v_g_post_mlp']
TWIN_OUTPUTS = ['loss', 'grad_x', 'grad_g_pre_mix', 'grad_w_in', 'grad_lam_re', 'grad_lam_im', 'grad_log_dt', 'grad_b_re', 'grad_b_im', 'grad_c_re', 'grad_c_im', 'grad_d_skip', 'grad_w_glu', 'grad_conv_w', 'grad_g_ssm_out', 'grad_g_conv_out', 'grad_w_out', 'grad_g_post_mix', 'grad_g_pre_mlp', 'grad_w_up', 'grad_w_down', 'grad_g_post_mlp', 'delta_g_pre_mix', 'delta_w_in', 'delta_lam_re', 'delta_lam_im', 'delta_log_dt', 'delta_b_re', 'delta_b_im', 'delta_c_re', 'delta_c_im', 'delta_d_skip', 'delta_w_glu', 'delta_conv_w', 'delta_g_ssm_out', 'delta_g_conv_out', 'delta_w_out', 'delta_g_post_mix', 'delta_g_pre_mlp', 'delta_w_up', 'delta_w_down', 'delta_g_post_mlp', 'new_m_g_pre_mix', 'new_m_w_in', 'new_m_lam_re', 'new_m_lam_im', 'new_m_log_dt', 'new_m_b_re', 'new_m_b_im', 'new_m_c_re', 'new_m_c_im', 'new_m_d_skip', 'new_m_w_glu', 'new_m_conv_w', 'new_m_g_ssm_out', 'new_m_g_conv_out', 'new_m_w_out', 'new_m_g_post_mix', 'new_m_g_pre_mlp', 'new_m_w_up', 'new_m_w_down', 'new_m_g_post_mlp', 'new_v_g_pre_mix', 'new_v_w_in', 'new_v_lam_re', 'new_v_lam_im', 'new_v_log_dt', 'new_v_b_re', 'new_v_b_im', 'new_v_c_re', 'new_v_c_im', 'new_v_d_skip', 'new_v_w_glu', 'new_v_conv_w', 'new_v_g_ssm_out', 'new_v_g_conv_out', 'new_v_w_out', 'new_v_g_post_mix', 'new_v_g_pre_mlp', 'new_v_w_up', 'new_v_w_down', 'new_v_g_post_mlp']
TWIN_LEAF_KINDS = {'loss': 'loss', 'grad_x': 'grad_x', 'grad_g_pre_mix': 'grad_w', 'grad_w_in': 'grad_w', 'grad_lam_re': 'grad_w', 'grad_lam_im': 'grad_w', 'grad_log_dt': 'grad_w', 'grad_b_re': 'grad_w', 'grad_b_im': 'grad_w', 'grad_c_re': 'grad_w', 'grad_c_im': 'grad_w', 'grad_d_skip': 'grad_w', 'grad_w_glu': 'grad_w', 'grad_conv_w': 'grad_w', 'grad_g_ssm_out': 'grad_w', 'grad_g_conv_out': 'grad_w', 'grad_w_out': 'grad_w', 'grad_g_post_mix': 'grad_w', 'grad_g_pre_mlp': 'grad_w', 'grad_w_up': 'grad_w', 'grad_w_down': 'grad_w', 'grad_g_post_mlp': 'grad_w', 'delta_g_pre_mix': 'delta_w', 'delta_w_in': 'delta_w', 'delta_lam_re': 'delta_w', 'delta_lam_im': 'delta_w', 'delta_log_dt': 'delta_w', 'delta_b_re': 'delta_w', 'delta_b_im': 'delta_w', 'delta_c_re': 'delta_w', 'delta_c_im': 'delta_w', 'delta_d_skip': 'delta_w', 'delta_w_glu': 'delta_w', 'delta_conv_w': 'delta_w', 'delta_g_ssm_out': 'delta_w', 'delta_g_conv_out': 'delta_w', 'delta_w_out': 'delta_w', 'delta_g_post_mix': 'delta_w', 'delta_g_pre_mlp': 'delta_w', 'delta_w_up': 'delta_w', 'delta_w_down': 'delta_w', 'delta_g_post_mlp': 'delta_w', 'new_m_g_pre_mix': 'new_m', 'new_m_w_in': 'new_m', 'new_m_lam_re': 'new_m', 'new_m_lam_im': 'new_m', 'new_m_log_dt': 'new_m', 'new_m_b_re': 'new_m', 'new_m_b_im': 'new_m', 'new_m_c_re': 'new_m', 'new_m_c_im': 'new_m', 'new_m_d_skip': 'new_m', 'new_m_w_glu': 'new_m', 'new_m_conv_w': 'new_m', 'new_m_g_ssm_out': 'new_m', 'new_m_g_conv_out': 'new_m', 'new_m_w_out': 'new_m', 'new_m_g_post_mix': 'new_m', 'new_m_g_pre_mlp': 'new_m', 'new_m_w_up': 'new_m', 'new_m_w_down': 'new_m', 'new_m_g_post_mlp': 'new_m', 'new_v_g_pre_mix': 'new_v', 'new_v_w_in': 'new_v', 'new_v_lam_re': 'new_v', 'new_v_lam_im': 'new_v', 'new_v_log_dt': 'new_v', 'new_v_b_re': 'new_v', 'new_v_b_im': 'new_v', 'new_v_c_re': 'new_v', 'new_v_c_im': 'new_v', 'new_v_d_skip': 'new_v', 'new_v_w_glu': 'new_v', 'new_v_conv_w': 'new_v', 'new_v_g_ssm_out': 'new_v', 'new_v_g_conv_out': 'new_v', 'new_v_w_out': 'new_v', 'new_v_g_post_mix': 'new_v', 'new_v_g_pre_mlp': 'new_v', 'new_v_w_up': 'new_v', 'new_v_w_down': 'new_v', 'new_v_g_post_mlp': 'new_v'}


def _forward(args):
    return _fwd_reference(*[args[k] for k in FWD_PARAMS])


def _output_shape():
    out = _jax.eval_shape(lambda: _forward(_fwd_setup_inputs(0)))
    return out.shape, out.dtype

N_MICROBATCH = 1
ADAM_LR = 0.001
ADAM_B1 = 0.9
ADAM_B2 = 0.999
ADAM_EPS = 1e-08
ADAM_WD = 0.01
ADAM_STEP = 10
PER_EXAMPLE_BATCH_AXIS = {'x': 0, 'loss_target': 0}
SHARED_INPUTS = []
_WEIGHT_DTYPES = {'g_pre_mix': _jnp.float32, 'w_in': _jnp.float32, 'lam_re': _jnp.float32, 'lam_im': _jnp.float32, 'log_dt': _jnp.float32, 'b_re': _jnp.float32, 'b_im': _jnp.float32, 'c_re': _jnp.float32, 'c_im': _jnp.float32, 'd_skip': _jnp.float32, 'w_glu': _jnp.float32, 'conv_w': _jnp.float32, 'g_ssm_out': _jnp.float32, 'g_conv_out': _jnp.float32, 'w_out': _jnp.float32, 'g_post_mix': _jnp.float32, 'g_pre_mlp': _jnp.float32, 'w_up': _jnp.float32, 'w_down': _jnp.float32, 'g_post_mlp': _jnp.float32}
MOMENT_SCALE = {'g_pre_mix': 1.001387e+00, 'w_in': 7.015189e-01, 'lam_re': 4.365095e-02, 'lam_im': 2.871669e-02, 'log_dt': 1.723440e+01, 'b_re': 2.409645e-02, 'b_im': 2.344466e-02, 'c_re': 5.239331e-02, 'c_im': 4.394648e-02, 'd_skip': 1.415501e+01, 'w_glu': 1.803417e+00, 'conv_w': 6.828414e-01, 'g_ssm_out': 1.355022e+01, 'g_conv_out': 1.334087e+00, 'w_out': 9.196352e+00, 'g_post_mix': 6.456340e+01, 'g_pre_mlp': 2.967974e+00, 'w_up': 1.510582e+00, 'w_down': 8.896963e+00, 'g_post_mlp': 6.625473e+01}


def _to_microbatches(a, axis):
    t = _jnp.moveaxis(a, axis, 0)
    t = t.reshape((N_MICROBATCH, t.shape[0] // N_MICROBATCH) + t.shape[1:])
    return _jnp.moveaxis(t, 1, axis + 1)


def setup_inputs(seed: int = 0) -> dict:
    inp = _fwd_setup_inputs(seed)
    key = _jax.random.fold_in(_jax.random.key(seed), 7919)
    shape, _ = _output_shape()
    out = dict(inp)
    out["loss_target"] = _jax.random.normal(_jax.random.fold_in(key, 0), shape, _jnp.float32)
    for i, name in enumerate(TWIN_WEIGHTS):
        w = inp[name].astype(_jnp.float32)
        if MOMENT_SCALE is None:
            s = _jnp.sqrt(_jnp.mean(_jnp.square(w)) + 1e-30)
        else:
            s = MOMENT_SCALE[name]
        km, kv = _jax.random.split(_jax.random.fold_in(key, i + 1))
        out[name] = w
        out["m_" + name] = s * _jax.random.normal(km, w.shape, _jnp.float32)
        out["v_" + name] = (s * s) * _jax.random.uniform(kv, w.shape, _jnp.float32, 0.5, 1.5)
    if N_MICROBATCH > 1:
        for name, axis in PER_EXAMPLE_BATCH_AXIS.items():
            out[name] = _to_microbatches(out[name], axis)
    return {'x': out['x'], 'g_pre_mix': out['g_pre_mix'], 'w_in': out['w_in'], 'lam_re': out['lam_re'], 'lam_im': out['lam_im'], 'log_dt': out['log_dt'], 'b_re': out['b_re'], 'b_im': out['b_im'], 'c_re': out['c_re'], 'c_im': out['c_im'], 'd_skip': out['d_skip'], 'w_glu': out['w_glu'], 'conv_w': out['conv_w'], 'g_ssm_out': out['g_ssm_out'], 'g_conv_out': out['g_conv_out'], 'w_out': out['w_out'], 'g_post_mix': out['g_post_mix'], 'g_pre_mlp': out['g_pre_mlp'], 'w_up': out['w_up'], 'w_down': out['w_down'], 'g_post_mlp': out['g_post_mlp'], 'loss_target': out['loss_target'], 'm_g_pre_mix': out['m_g_pre_mix'], 'm_w_in': out['m_w_in'], 'm_lam_re': out['m_lam_re'], 'm_lam_im': out['m_lam_im'], 'm_log_dt': out['m_log_dt'], 'm_b_re': out['m_b_re'], 'm_b_im': out['m_b_im'], 'm_c_re': out['m_c_re'], 'm_c_im': out['m_c_im'], 'm_d_skip': out['m_d_skip'], 'm_w_glu': out['m_w_glu'], 'm_conv_w': out['m_conv_w'], 'm_g_ssm_out': out['m_g_ssm_out'], 'm_g_conv_out': out['m_g_conv_out'], 'm_w_out': out['m_w_out'], 'm_g_post_mix': out['m_g_post_mix'], 'm_g_pre_mlp': out['m_g_pre_mlp'], 'm_w_up': out['m_w_up'], 'm_w_down': out['m_w_down'], 'm_g_post_mlp': out['m_g_post_mlp'], 'v_g_pre_mix': out['v_g_pre_mix'], 'v_w_in': out['v_w_in'], 'v_lam_re': out['v_lam_re'], 'v_lam_im': out['v_lam_im'], 'v_log_dt': out['v_log_dt'], 'v_b_re': out['v_b_re'], 'v_b_im': out['v_b_im'], 'v_c_re': out['v_c_re'], 'v_c_im': out['v_c_im'], 'v_d_skip': out['v_d_skip'], 'v_w_glu': out['v_w_glu'], 'v_conv_w': out['v_conv_w'], 'v_g_ssm_out': out['v_g_ssm_out'], 'v_g_conv_out': out['v_g_conv_out'], 'v_w_out': out['v_w_out'], 'v_g_post_mix': out['v_g_post_mix'], 'v_g_pre_mlp': out['v_g_pre_mlp'], 'v_w_up': out['v_w_up'], 'v_w_down': out['v_w_down'], 'v_g_post_mlp': out['v_g_post_mlp']}


def _loss(weights, diff, rest, loss_target):
    with _jax.named_scope("forward"):
        args = {**rest, TWIN_DIFF_INPUT: diff, **{k: w.astype(_WEIGHT_DTYPES[k]) for k, w in weights.items()}}
        y = _forward(args)
    with _jax.named_scope("loss_head"):
        err = _jnp.square(y.astype(_jnp.float32) - loss_target)
        return 0.5 * _jnp.sum(_jnp.mean(err, axis=-1)) if err.ndim else 0.5 * err


def _adamw(w, g, m, v):
    m = ADAM_B1 * m + (1.0 - ADAM_B1) * g
    v = ADAM_B2 * v + (1.0 - ADAM_B2) * _jnp.square(g)
    m_hat = m / (1.0 - ADAM_B1 ** ADAM_STEP)
    v_hat = v / (1.0 - ADAM_B2 ** ADAM_STEP)
    delta = -ADAM_LR * (m_hat / (_jnp.sqrt(v_hat) + ADAM_EPS) + ADAM_WD * w)
    return delta, m, v


def reference(x, g_pre_mix, w_in, lam_re, lam_im, log_dt, b_re, b_im, c_re, c_im, d_skip, w_glu, conv_w, g_ssm_out, g_conv_out, w_out, g_post_mix, g_pre_mlp, w_up, w_down, g_post_mlp, loss_target, m_g_pre_mix, m_w_in, m_lam_re, m_lam_im, m_log_dt, m_b_re, m_b_im, m_c_re, m_c_im, m_d_skip, m_w_glu, m_conv_w, m_g_ssm_out, m_g_conv_out, m_w_out, m_g_post_mix, m_g_pre_mlp, m_w_up, m_w_down, m_g_post_mlp, v_g_pre_mix, v_w_in, v_lam_re, v_lam_im, v_log_dt, v_b_re, v_b_im, v_c_re, v_c_im, v_d_skip, v_w_glu, v_conv_w, v_g_ssm_out, v_g_conv_out, v_w_out, v_g_post_mix, v_g_pre_mlp, v_w_up, v_w_down, v_g_post_mlp):
    given = dict(x=x, g_pre_mix=g_pre_mix, w_in=w_in, lam_re=lam_re, lam_im=lam_im, log_dt=log_dt, b_re=b_re, b_im=b_im, c_re=c_re, c_im=c_im, d_skip=d_skip, w_glu=w_glu, conv_w=conv_w, g_ssm_out=g_ssm_out, g_conv_out=g_conv_out, w_out=w_out, g_post_mix=g_post_mix, g_pre_mlp=g_pre_mlp, w_up=w_up, w_down=w_down, g_post_mlp=g_post_mlp, loss_target=loss_target, m_g_pre_mix=m_g_pre_mix, m_w_in=m_w_in, m_lam_re=m_lam_re, m_lam_im=m_lam_im, m_log_dt=m_log_dt, m_b_re=m_b_re, m_b_im=m_b_im, m_c_re=m_c_re, m_c_im=m_c_im, m_d_skip=m_d_skip, m_w_glu=m_w_glu, m_conv_w=m_conv_w, m_g_ssm_out=m_g_ssm_out, m_g_conv_out=m_g_conv_out, m_w_out=m_w_out, m_g_post_mix=m_g_post_mix, m_g_pre_mlp=m_g_pre_mlp, m_w_up=m_w_up, m_w_down=m_w_down, m_g_post_mlp=m_g_post_mlp, v_g_pre_mix=v_g_pre_mix, v_w_in=v_w_in, v_lam_re=v_lam_re, v_lam_im=v_lam_im, v_log_dt=v_log_dt, v_b_re=v_b_re, v_b_im=v_b_im, v_c_re=v_c_re, v_c_im=v_c_im, v_d_skip=v_d_skip, v_w_glu=v_w_glu, v_conv_w=v_conv_w, v_g_ssm_out=v_g_ssm_out, v_g_conv_out=v_g_conv_out, v_w_out=v_w_out, v_g_post_mix=v_g_post_mix, v_g_pre_mlp=v_g_pre_mlp, v_w_up=v_w_up, v_w_down=v_w_down, v_g_post_mlp=v_g_post_mlp)
    weights = {n: given[n] for n in TWIN_WEIGHTS}
    shared = {n: given[n] for n in SHARED_INPUTS}
    per_example = {n: given[n] for n in ['x']}
    grad_fn = _jax.value_and_grad(_loss, argnums=(0, 1))

    def one_microbatch(ex, loss_target):
        ex = dict(ex)
        diff = ex.pop(TWIN_DIFF_INPUT)
        return grad_fn(weights, diff, {**shared, **ex}, loss_target)

    if N_MICROBATCH == 1:
        loss, (grad_w, grad_x) = one_microbatch(per_example, given["loss_target"])
    else:
        def body(carry, xs):
            loss_sum, grad_sum = carry
            l_k, (gw_k, gx_k) = one_microbatch(xs[0], xs[1])
            with _jax.named_scope("update"):
                return (loss_sum + l_k, _jax.tree.map(_jnp.add, grad_sum, gw_k)), gx_k

        init = (_jnp.zeros((), _jnp.float32), _jax.tree.map(_jnp.zeros_like, weights))
        (loss, grad_w), grad_x = _jax.lax.scan(body, init, (per_example, given["loss_target"]))
    with _jax.named_scope("update"):
        delta_w, new_m, new_v = {}, {}, {}
        for n in TWIN_WEIGHTS:
            delta_w[n], new_m[n], new_v[n] = _adamw(weights[n], grad_w[n], given["m_" + n], given["v_" + n])
    return (loss, grad_x, *[grad_w[n] for n in TWIN_WEIGHTS], *[delta_w[n] for n in TWIN_WEIGHTS],
            *[new_m[n] for n in TWIN_WEIGHTS], *[new_v[n] for n in TWIN_WEIGHTS])
```

```python
import functools
import math

import jax
import jax.numpy as jnp
from jax import lax
from jax.experimental import pallas as pl
from jax.experimental.pallas import tpu as pltpu

F32 = jnp.float32
MXU_DTYPE = jnp.bfloat16

D_MODEL = 1024
D_SSM = 512
D_CONV = 512
N_GROUPS = 32
GROUP = 16
STATE = 64
D_FF = 4096
RMS_EPS = 1e-6
N_CHIPS = 4
N_DEV = 8

ADAM_LR = 0.001
ADAM_B1 = 0.9
ADAM_B2 = 0.999
ADAM_EPS = 1e-08
ADAM_WD = 0.01
ADAM_STEP = 10

N_GBLK = 2
G_PER_BLK = N_GROUPS // N_GBLK
UB = G_PER_BLK * GROUP
WB = G_PER_BLK * STATE
LANE_CHUNK = 512
SUBLANES = 8

TM_PROJ = 512
TM_S5 = 512
TM_TAIL = 256
TM_MLP = 256
TL_TN = 512
VMEM_LIMIT = 56 * 1024 * 1024

MESH = pl.DeviceIdType.MESH


def _params(sem, vmem=VMEM_LIMIT):
    return pltpu.CompilerParams(dimension_semantics=sem, vmem_limit_bytes=vmem)


def _resident(shape):
    nd = len(shape)
    return pl.BlockSpec(shape, lambda *_: (0,) * nd, pipeline_mode=pl.Buffered(1))


def _dot(a, b):
    return jnp.dot(a, b, preferred_element_type=F32)


def _dot_nt(a, b):
    return lax.dot_general(a, b, (((1,), (1,)), ((), ())), preferred_element_type=F32)


def _dot_tn(a, b):
    return lax.dot_general(a, b, (((0,), (0,)), ((), ())), preferred_element_type=F32)


def _rms_fwd(x, g):
    r = lax.rsqrt(jnp.mean(x * x, axis=-1, keepdims=True) + RMS_EPS)
    return x * r * g


def _rms_bwd(x, g, dy):
    r = lax.rsqrt(jnp.mean(x * x, axis=-1, keepdims=True) + RMS_EPS)
    xn = x * r
    q = dy * g
    dx = r * (q - xn * jnp.mean(q * xn, axis=-1, keepdims=True))
    return dx, jnp.sum(dy * xn, axis=0, keepdims=True)


_GELU_C = math.sqrt(2.0 / math.pi)


def _gelu(x):
    t = jnp.tanh(_GELU_C * (x + 0.044715 * (x * x * x)))
    y = x * (0.5 * (1.0 + t))
    dy = 0.5 * (1.0 + t) + 0.5 * x * (1.0 - t * t) * (_GELU_C * (1.0 + 3 * 0.044715 * (x * x)))
    return y, dy


def _tile(n, pref):
    t = min(n, pref)
    assert n % t == 0, (n, t)
    return t


def _inproj_fwd(x, g1, w_in_all):
    L, D = x.shape
    ns, _, nc = w_in_all.shape
    tm = _tile(L, TM_PROJ)

    def body(x_ref, g_ref, w_ref, hn_ref, proj_ref):
        hn = _rms_fwd(x_ref[...], g_ref[...]).astype(MXU_DTYPE)
        hn_ref[...] = hn
        for j in range(ns):
            proj_ref[:, j * nc:(j + 1) * nc] = _dot(hn, w_ref[j])

    return pl.pallas_call(
        body, name="inproj_fwd", grid=(L // tm,),
        in_specs=[pl.BlockSpec((tm, D), lambda i: (i, 0)), _resident((1, D)), _resident(w_in_all.shape)],
        out_specs=[pl.BlockSpec((tm, D), lambda i: (i, 0)), pl.BlockSpec((tm, ns * nc), lambda i: (i, 0))],
        out_shape=[jax.ShapeDtypeStruct((L, D), MXU_DTYPE), jax.ShapeDtypeStruct((L, ns * nc), F32)],
        compiler_params=_params(("arbitrary",)),
    )(x, g1, w_in_all)


def _scan_tile(xr, xi, hr, hi, coef_ref, lanes, reverse):
    for k, j in ((1, 0), (2, 2), (4, 4)):
        ar = coef_ref[0, j, :, lanes]
        ai = coef_ref[0, j + 1, :, lanes]
        shift = SUBLANES - k if reverse else k
        sr = pltpu.roll(xr, shift, 0)
        si = pltpu.roll(xi, shift, 0)
        xr, xi = xr + (ar * sr - ai * si), xi + (ar * si + ai * sr)
    pr = coef_ref[0, 6, :, lanes]
    pi = coef_ref[0, 7, :, lanes]
    return xr + (pr * hr - pi * hi), xi + (pr * hi + pi * hr)


def _s5_fwd(proj, bmat, cmat, coef, dskip):
    L = proj.shape[0]
    tm = _tile(L, TM_S5)
    nrt = tm // SUBLANES
    lc = min(LANE_CHUNK, WB)

    def body(u_ref, bm_ref, cm_ref, coef_ref, d_ref, sre_ref, sim_ref, ys_ref, hr_ref, hi_ref):
        @pl.when(pl.program_id(1) == 0)
        def _():
            hr_ref[...] = jnp.zeros_like(hr_ref)
            hi_ref[...] = jnp.zeros_like(hi_ref)

        u = u_ref[...]
        bu = _dot(u.astype(MXU_DTYPE), bm_ref[0])
        sre_ref[...] = bu[:, :WB]
        sim_ref[...] = bu[:, WB:]
        for c in range(WB // lc):
            lanes = slice(c * lc, (c + 1) * lc)

            def row_body(r, carry, lanes=lanes):
                hr, hi = carry
                rows = pl.ds(pl.multiple_of(r * SUBLANES, SUBLANES), SUBLANES)
                xr, xi = _scan_tile(sre_ref[rows, lanes], sim_ref[rows, lanes], hr, hi, coef_ref, lanes, False)
                sre_ref[rows, lanes] = xr
                sim_ref[rows, lanes] = xi
                last = SUBLANES - 1
                return (jnp.broadcast_to(xr[last:last + 1, :], xr.shape), jnp.broadcast_to(xi[last:last + 1, :], xi.shape))

            hr, hi = lax.fori_loop(0, nrt, row_body, (hr_ref[:, lanes], hi_ref[:, lanes]))
            hr_ref[:, lanes] = hr
            hi_ref[:, lanes] = hi
        ys = _dot(sre_ref[...].astype(MXU_DTYPE), cm_ref[0, :WB, :]) + _dot(sim_ref[...].astype(MXU_DTYPE), cm_ref[0, WB:, :])
        ys_ref[...] = ys + d_ref[0] * u

    return pl.pallas_call(
        body, name="s5_fwd", grid=(N_GBLK, L // tm),
        in_specs=[
            pl.BlockSpec((tm, UB), lambda b, i: (i, b)),
            pl.BlockSpec((1, UB, 2 * WB), lambda b, i: (b, 0, 0)),
            pl.BlockSpec((1, 2 * WB, UB), lambda b, i: (b, 0, 0)),
            pl.BlockSpec((1, 8, SUBLANES, WB), lambda b, i: (b, 0, 0, 0)),
            pl.BlockSpec((1, 1, UB), lambda b, i: (b, 0, 0)),
        ],
        out_specs=[
            pl.BlockSpec((tm, WB), lambda b, i: (i, b)),
            pl.BlockSpec((tm, WB), lambda b, i: (i, b)),
            pl.BlockSpec((tm, UB), lambda b, i: (i, b)),
        ],
        out_shape=[
            jax.ShapeDtypeStruct((L, N_GBLK * WB), F32),
            jax.ShapeDtypeStruct((L, N_GBLK * WB), F32),
            jax.ShapeDtypeStruct((L, D_SSM), F32),
        ],
        scratch_shapes=[pltpu.VMEM((SUBLANES, WB), F32), pltpu.VMEM((SUBLANES, WB), F32)],
        compiler_params=_params(("arbitrary", "arbitrary")),
    )(proj, bmat, cmat, coef, dskip)


def _tail_fwd(x, ys, proj, w_glu, conv_w, g_ssm, g_conv, w_out, g_post):
    L, D = x.shape
    tm = _tile(L, TM_TAIL)

    def body(x_ref, ys_ref, h_ref, bg_ref, cg_ref, wglu_ref, cw_ref, gs_ref, gc_ref, wout_ref, gp_ref,
             ycat_ref, o_ref, x1_ref, zbuf):
        @pl.when(pl.program_id(0) == 0)
        def _():
            zbuf[0:SUBLANES, :] = jnp.zeros((SUBLANES, D_CONV), F32)

        y1, _ = _gelu(ys_ref[...])
        y2 = y1 * jax.nn.sigmoid(_dot(y1.astype(MXU_DTYPE), wglu_ref[...]))
        ycat_ref[:, :D_SSM] = _rms_fwd(y2, gs_ref[...]).astype(MXU_DTYPE)
        z = cg_ref[...] * h_ref[...]
        zbuf[SUBLANES:, :] = z
        conv = cw_ref[0:1, :] * zbuf[SUBLANES - 2:SUBLANES - 2 + tm, :] + cw_ref[1:2, :] * zbuf[SUBLANES - 1:SUBLANES - 1 + tm, :] + cw_ref[2:3, :] * z
        zbuf[0:SUBLANES, :] = zbuf[tm:tm + SUBLANES, :]
        ycat_ref[:, D_SSM:] = _rms_fwd(bg_ref[...] * conv, gc_ref[...]).astype(MXU_DTYPE)
        o = _dot(ycat_ref[...], wout_ref[...])
        o_ref[...] = o
        x1_ref[...] = x_ref[...] + _rms_fwd(o, gp_ref[...])

    row = lambda i: (i, 0)
    return pl.pallas_call(
        body, name="tail_fwd", grid=(L // tm,),
        in_specs=[
            pl.BlockSpec((tm, D), row), pl.BlockSpec((tm, D_SSM), row),
            pl.BlockSpec((tm, D_CONV), lambda i: (i, 1)), pl.BlockSpec((tm, D_CONV), lambda i: (i, 2)),
            pl.BlockSpec((tm, D_CONV), lambda i: (i, 3)),
            _resident(w_glu.shape), _resident(conv_w.shape), _resident(g_ssm.shape), _resident(g_conv.shape),
            _resident(w_out.shape), _resident(g_post.shape),
        ],
        out_specs=[pl.BlockSpec((tm, D), row), pl.BlockSpec((tm, D), row), pl.BlockSpec((tm, D), row)],
        out_shape=[jax.ShapeDtypeStruct((L, D), MXU_DTYPE), jax.ShapeDtypeStruct((L, D), F32), jax.ShapeDtypeStruct((L, D), F32)],
        scratch_shapes=[pltpu.VMEM((tm + SUBLANES, D_CONV), F32)],
        compiler_params=_params(("arbitrary",)),
    )(x, ys, proj, proj, proj, w_glu, conv_w, g_ssm, g_conv, w_out, g_post)


def _mlp_fwd(x1, target, w_up_all, w_down, g_pre, g_post):
    L, D = x1.shape
    ns, _, fc = w_up_all.shape
    tm = _tile(L, TM_MLP)

    def body(x1_ref, t_ref, wup_ref, wdn_ref, gpre_ref, gpost_ref, hn2_ref, up_ref, m_ref, dx2_ref, loss_ref):
        @pl.when(pl.program_id(0) == 0)
        def _():
            loss_ref[...] = jnp.zeros_like(loss_ref)

        x1v = x1_ref[...]
        hn2 = _rms_fwd(x1v, gpre_ref[...]).astype(MXU_DTYPE)
        hn2_ref[...] = hn2
        m = jnp.zeros((tm, D), F32)
        for j in range(ns):
            up = _dot(hn2, wup_ref[j])
            up_ref[:, j * fc:(j + 1) * fc] = up
            act = jnp.square(jnp.maximum(up, 0.0)).astype(MXU_DTYPE)
            m = m + _dot(act, wdn_ref[j * fc:(j + 1) * fc, :])
        m_ref[...] = m
        err = x1v + _rms_fwd(m, gpost_ref[...]) - t_ref[...]
        loss_ref[...] += 0.5 * jnp.sum(jnp.mean(err * err, axis=-1, keepdims=True))
        dx2_ref[...] = err * (1.0 / D)

    row = lambda i: (i, 0)
    return pl.pallas_call(
        body, name="mlp_fwd", grid=(L // tm,),
        in_specs=[pl.BlockSpec((tm, D), row), pl.BlockSpec((tm, D), row), _resident(w_up_all.shape), _resident(w_down.shape),
                  _resident(g_pre.shape), _resident(g_post.shape)],
        out_specs=[pl.BlockSpec((tm, D), row), pl.BlockSpec((tm, ns * fc), row), pl.BlockSpec((tm, D), row),
                   pl.BlockSpec((tm, D), row), pl.BlockSpec((SUBLANES, 128), lambda i: (0, 0))],
        out_shape=[jax.ShapeDtypeStruct((L, D), MXU_DTYPE), jax.ShapeDtypeStruct((L, ns * fc), F32), jax.ShapeDtypeStruct((L, D), F32),
                   jax.ShapeDtypeStruct((L, D), F32), jax.ShapeDtypeStruct((SUBLANES, 128), F32)],
        compiler_params=_params(("arbitrary",)),
    )(x1, target, w_up_all, w_down, g_pre, g_post)


def _mlp_bwd(dx2, m, up, x1, w_up_all, w_down, g_pre, g_post):
    L, D = x1.shape
    ns, _, fc = w_up_all.shape
    tm = _tile(L, TM_MLP)

    def body(dx2_ref, m_ref, up_ref, x1_ref, wup_ref, wdn_ref, gpre_ref, gpost_ref,
             dm_ref, dup_ref, act_ref, dx1_ref, dgpost_ref, dgpre_ref):
        @pl.when(pl.program_id(0) == 0)
        def _():
            dgpost_ref[...] = jnp.zeros_like(dgpost_ref)
            dgpre_ref[...] = jnp.zeros_like(dgpre_ref)

        dx2v = dx2_ref[...]
        dm, dg = _rms_bwd(m_ref[...], gpost_ref[...], dx2v)
        dgpost_ref[...] += dg
        dm_b = dm.astype(MXU_DTYPE)
        dm_ref[...] = dm_b
        dhn2 = jnp.zeros((tm, D), F32)
        for j in range(ns):
            cols = slice(j * fc, (j + 1) * fc)
            relu = jnp.maximum(up_ref[:, cols], 0.0)
            act_ref[:, cols] = jnp.square(relu).astype(MXU_DTYPE)
            dup = (_dot_nt(dm_b, wdn_ref[cols, :]) * (2.0 * relu)).astype(MXU_DTYPE)
            dup_ref[:, cols] = dup
            dhn2 = dhn2 + _dot_nt(dup, wup_ref[j])
        dx, dg = _rms_bwd(x1_ref[...], gpre_ref[...], dhn2)
        dgpre_ref[...] += dg
        dx1_ref[...] = dx2v + dx

    row = lambda i: (i, 0)
    vec = pl.BlockSpec((1, D), lambda i: (0, 0))
    return pl.pallas_call(
        body, name="mlp_bwd", grid=(L // tm,),
        in_specs=[pl.BlockSpec((tm, D), row), pl.BlockSpec((tm, D), row), pl.BlockSpec((tm, ns * fc), row), pl.BlockSpec((tm, D), row),
                  _resident(w_up_all.shape), _resident(w_down.shape), _resident(g_pre.shape), _resident(g_post.shape)],
        out_specs=[pl.BlockSpec((tm, D), row), pl.BlockSpec((tm, ns * fc), row), pl.BlockSpec((tm, ns * fc), row),
                   pl.BlockSpec((tm, D), row), vec, vec],
        out_shape=[jax.ShapeDtypeStruct((L, D), MXU_DTYPE), jax.ShapeDtypeStruct((L, ns * fc), MXU_DTYPE),
                   jax.ShapeDtypeStruct((L, ns * fc), MXU_DTYPE), jax.ShapeDtypeStruct((L, D), F32),
                   jax.ShapeDtypeStruct((1, D), F32), jax.ShapeDtypeStruct((1, D), F32)],
        compiler_params=_params(("arbitrary",)),
    )(dx2, m, up, x1, w_up_all, w_down, g_pre, g_post)


def _tail_bwd(dx1, o, ys, proj, w_glu, conv_w, g_ssm, g_conv, w_out, g_post):
    L, D = dx1.shape
    tm = _tile(L, TM_TAIL)
    nt = L // tm
    hb = tm // SUBLANES

    def body(dx1_ref, o_ref, ys_ref, h_ref, bg_ref, cg_ref, hh_ref, hcg_ref, wglu_ref, cw_ref, gs_ref, gc_ref, wout_ref, gp_ref,
             do_ref, da_ref, y1_ref, dys_ref, dhbc_ref, dgp_ref, dgs_ref, dgc_ref, dcw_ref, zbuf, dcbuf):
        step = pl.program_id(0)

        @pl.when(step == 0)
        def _():
            dcbuf[tm:, :] = jnp.zeros((SUBLANES, D_CONV), F32)
            dgp_ref[...] = jnp.zeros_like(dgp_ref)
            dgs_ref[...] = jnp.zeros_like(dgs_ref)
            dgc_ref[...] = jnp.zeros_like(dgc_ref)
            dcw_ref[...] = jnp.zeros_like(dcw_ref)

        do, dg = _rms_bwd(o_ref[...], gp_ref[...], dx1_ref[...])
        dgp_ref[...] += dg
        do_b = do.astype(MXU_DTYPE)
        do_ref[...] = do_b
        dycat = _dot_nt(do_b, wout_ref[...])
        y1, dgelu = _gelu(ys_ref[...])
        y1_b = y1.astype(MXU_DTYPE)
        y1_ref[...] = y1_b
        s = jax.nn.sigmoid(_dot(y1_b, wglu_ref[...]))
        dy2, dg = _rms_bwd(y1 * s, gs_ref[...], dycat[:, :D_SSM])
        dgs_ref[...] += dg
        da_b = (dy2 * y1 * s * (1.0 - s)).astype(MXU_DTYPE)
        da_ref[...] = da_b
        dys_ref[...] = (dy2 * s + _dot_nt(da_b, wglu_ref[...])) * dgelu
        h = h_ref[...]
        cg = cg_ref[...]
        bg = bg_ref[...]
        z = cg * h
        first = step == nt - 1
        zbuf[0:SUBLANES, :] = jnp.where(first, 0.0, hcg_ref[...] * hh_ref[...])
        zbuf[SUBLANES:, :] = z
        z1 = zbuf[SUBLANES - 1:SUBLANES - 1 + tm, :]
        z2 = zbuf[SUBLANES - 2:SUBLANES - 2 + tm, :]
        conv = cw_ref[0:1, :] * z2 + cw_ref[1:2, :] * z1 + cw_ref[2:3, :] * z
        dyc, dg = _rms_bwd(bg * conv, gc_ref[...], dycat[:, D_SSM:])
        dgc_ref[...] += dg
        dconv = dyc * bg
        dcw_ref[0:1, :] += jnp.sum(dconv * z2, axis=0, keepdims=True)
        dcw_ref[1:2, :] += jnp.sum(dconv * z1, axis=0, keepdims=True)
        dcw_ref[2:3, :] += jnp.sum(dconv * z, axis=0, keepdims=True)
        dcbuf[0:tm, :] = dconv
        dz = cw_ref[2:3, :] * dconv + cw_ref[1:2, :] * dcbuf[1:1 + tm, :] + cw_ref[0:1, :] * dcbuf[2:2 + tm, :]
        dcbuf[tm:, :] = dcbuf[0:SUBLANES, :]
        dhbc_ref[:, 0:D_CONV] = (dz * cg).astype(MXU_DTYPE)
        dhbc_ref[:, D_CONV:2 * D_CONV] = (dyc * conv).astype(MXU_DTYPE)
        dhbc_ref[:, 2 * D_CONV:] = (dz * h).astype(MXU_DTYPE)

    rev = lambda i: (nt - 1 - i, 0)
    col = lambda c: (lambda i: (nt - 1 - i, c))
    halo = lambda c: (lambda i: (jnp.maximum((nt - 1 - i) * hb - 1, 0), c))
    vec = lambda n: pl.BlockSpec((1, n), lambda i: (0, 0))
    return pl.pallas_call(
        body, name="tail_bwd", grid=(nt,),
        in_specs=[
            pl.BlockSpec((tm, D), rev), pl.BlockSpec((tm, D), rev), pl.BlockSpec((tm, D_SSM), rev),
            pl.BlockSpec((tm, D_CONV), col(1)), pl.BlockSpec((tm, D_CONV), col(2)), pl.BlockSpec((tm, D_CONV), col(3)),
            pl.BlockSpec((SUBLANES, D_CONV), halo(1)), pl.BlockSpec((SUBLANES, D_CONV), halo(3)),
            _resident(w_glu.shape), _resident(conv_w.shape), _resident(g_ssm.shape), _resident(g_conv.shape),
            _resident(w_out.shape), _resident(g_post.shape),
        ],
        out_specs=[
            pl.BlockSpec((tm, D), rev), pl.BlockSpec((tm, D_SSM), rev), pl.BlockSpec((tm, D_SSM), rev), pl.BlockSpec((tm, D_SSM), rev),
            pl.BlockSpec((tm, 3 * D_CONV), rev), vec(D), vec(D_SSM), vec(D_CONV),
            pl.BlockSpec((SUBLANES, D_CONV), lambda i: (0, 0)),
        ],
        out_shape=[
            jax.ShapeDtypeStruct((L, D), MXU_DTYPE), jax.ShapeDtypeStruct((L, D_SSM), MXU_DTYPE), jax.ShapeDtypeStruct((L, D_SSM), MXU_DTYPE),
            jax.ShapeDtypeStruct((L, D_SSM), F32), jax.ShapeDtypeStruct((L, 3 * D_CONV), MXU_DTYPE),
            jax.ShapeDtypeStruct((1, D), F32), jax.ShapeDtypeStruct((1, D_SSM), F32), jax.ShapeDtypeStruct((1, D_CONV), F32),
            jax.ShapeDtypeStruct((SUBLANES, D_CONV), F32),
        ],
        scratch_shapes=[pltpu.VMEM((tm + SUBLANES, D_CONV), F32), pltpu.VMEM((tm + SUBLANES, D_CONV), F32)],
        compiler_params=_params(("arbitrary",)),
    )(dx1, o, ys, proj, proj, proj, proj, proj, w_glu, conv_w, g_ssm, g_conv, w_out, g_post)


def _s5_bwd(dys, proj, s_re, s_im, bmat, cmat, coef_rev, dskip):
    L = dys.shape[0]
    tm = _tile(L, TM_S5)
    nt = L // tm
    nrt = tm // SUBLANES
    lc = min(LANE_CHUNK, WB)

    def body(dys_ref, u_ref, sre_ref, sim_ref, bm_ref, cm_ref, coef_ref, d_ref,
             du_ref, gb_ref, gc_ref, q_ref, gd_ref, lr_ref, li_ref, hr_ref, hi_ref, qr_acc, qi_acc):
        step = pl.program_id(1)

        @pl.when(step == 0)
        def _():
            for ref in (hr_ref, hi_ref, qr_acc, qi_acc, gb_ref, gc_ref, gd_ref):
                ref[...] = jnp.zeros_like(ref)

        dys_v = dys_ref[...]
        u = u_ref[...]
        dys_b = dys_v.astype(MXU_DTYPE)
        u_b = u.astype(MXU_DTYPE)
        d = _dot_nt(dys_b, cm_ref[0])
        lr_ref[...] = d[:, :WB]
        li_ref[...] = d[:, WB:]
        for c in range(WB // lc):
            lanes = slice(c * lc, (c + 1) * lc)

            def row_body(k, carry, lanes=lanes):
                hr, hi = carry
                rows = pl.ds(pl.multiple_of((nrt - 1 - k) * SUBLANES, SUBLANES), SUBLANES)
                dr = lr_ref[rows, lanes]
                di = li_ref[rows, lanes]
                xr, xi = _scan_tile(dr, di, hr, hi, coef_ref, lanes, True)
                lr_ref[rows, lanes] = xr
                li_ref[rows, lanes] = xi
                er = xr - dr
                ei = xi - di
                sr = sre_ref[rows, lanes]
                si = sim_ref[rows, lanes]
                qr_acc[:, lanes] += er * sr + ei * si
                qi_acc[:, lanes] += ei * sr - er * si
                return (jnp.broadcast_to(xr[0:1, :], xr.shape), jnp.broadcast_to(xi[0:1, :], xi.shape))

            hr, hi = lax.fori_loop(0, nrt, row_body, (hr_ref[:, lanes], hi_ref[:, lanes]))
            hr_ref[:, lanes] = hr
            hi_ref[:, lanes] = hi
        lr_b = lr_ref[...].astype(MXU_DTYPE)
        li_b = li_ref[...].astype(MXU_DTYPE)
        du_ref[...] = _dot_nt(lr_b, bm_ref[0, :, :WB]) + _dot_nt(li_b, bm_ref[0, :, WB:]) + d_ref[0] * dys_v
        gb_ref[0, :WB, :] += _dot_tn(lr_b, u_b)
        gb_ref[0, WB:, :] += _dot_tn(li_b, u_b)
        gc_ref[0, :, :WB] += _dot_tn(dys_b, sre_ref[...].astype(MXU_DTYPE))
        gc_ref[0, :, WB:] += _dot_tn(dys_b, sim_ref[...].astype(MXU_DTYPE))
        gd_ref[0] += jnp.sum(dys_v * u, axis=0, keepdims=True)

        @pl.when(step == nt - 1)
        def _():
            q_ref[0, 0:1, :] = jnp.sum(qr_acc[...], axis=0, keepdims=True)
            q_ref[0, 1:2, :] = jnp.sum(qi_acc[...], axis=0, keepdims=True)

    rev = lambda b, i: (nt - 1 - i, b)
    blk = lambda b, i: (b, 0, 0)
    return pl.pallas_call(
        body, name="s5_bwd", grid=(N_GBLK, nt),
        in_specs=[
            pl.BlockSpec((tm, UB), rev), pl.BlockSpec((tm, UB), rev), pl.BlockSpec((tm, WB), rev), pl.BlockSpec((tm, WB), rev),
            pl.BlockSpec((1, UB, 2 * WB), blk), pl.BlockSpec((1, 2 * WB, UB), blk),
            pl.BlockSpec((1, 8, SUBLANES, WB), lambda b, i: (b, 0, 0, 0)), pl.BlockSpec((1, 1, UB), blk),
        ],
        out_specs=[
            pl.BlockSpec((tm, UB), rev), pl.BlockSpec((1, 2 * WB, UB), blk), pl.BlockSpec((1, UB, 2 * WB), blk),
            pl.BlockSpec((1, 2, WB), blk), pl.BlockSpec((1, 1, UB), blk),
        ],
        out_shape=[
            jax.ShapeDtypeStruct((L, D_SSM), F32), jax.ShapeDtypeStruct((N_GBLK, 2 * WB, UB), F32),
            jax.ShapeDtypeStruct((N_GBLK, UB, 2 * WB), F32), jax.ShapeDtypeStruct((N_GBLK, 2, WB), F32),
            jax.ShapeDtypeStruct((N_GBLK, 1, UB), F32),
        ],
        scratch_shapes=[pltpu.VMEM((tm, WB), F32), pltpu.VMEM((tm, WB), F32), pltpu.VMEM((SUBLANES, WB), F32),
                        pltpu.VMEM((SUBLANES, WB), F32), pltpu.VMEM((SUBLANES, WB), F32), pltpu.VMEM((SUBLANES, WB), F32)],
        compiler_params=_params(("arbitrary", "arbitrary")),
    )(dys, proj, s_re, s_im, bmat, cmat, coef_rev, dskip)


def _inproj_bwd(du, dhbc, x, dx1, w_in_all, g1):
    L, D = x.shape
    ns, _, nc = w_in_all.shape
    tm = _tile(L, TM_PROJ)

    def body(du_ref, dhbc_ref, x_ref, dx1_ref, w_ref, g_ref, gx_ref, dproj_ref, dg_ref):
        @pl.when(pl.program_id(0) == 0)
        def _():
            dg_ref[...] = jnp.zeros_like(dg_ref)

        du_b = du_ref[...].astype(MXU_DTYPE)
        dproj_ref[:, :nc] = du_b
        dproj_ref[:, nc:] = dhbc_ref[...]
        dhn = _dot_nt(du_b, w_ref[0])
        for j in range(1, ns):
            dhn = dhn + _dot_nt(dhbc_ref[:, (j - 1) * nc:j * nc], w_ref[j])
        dx, dg = _rms_bwd(x_ref[...], g_ref[...], dhn)
        dg_ref[...] += dg
        gx_ref[...] = dx1_ref[...] + dx

    row = lambda i: (i, 0)
    return pl.pallas_call(
        body, name="inproj_bwd", grid=(L // tm,),
        in_specs=[pl.BlockSpec((tm, nc), row), pl.BlockSpec((tm, (ns - 1) * nc), row), pl.BlockSpec((tm, D), row), pl.BlockSpec((tm, D), row),
                  _resident(w_in_all.shape), _resident(g1.shape)],
        out_specs=[pl.BlockSpec((tm, D), row), pl.BlockSpec((tm, ns * nc), row), pl.BlockSpec((1, D), lambda i: (0, 0))],
        out_shape=[jax.ShapeDtypeStruct((L, D), F32), jax.ShapeDtypeStruct((L, ns * nc), MXU_DTYPE), jax.ShapeDtypeStruct((1, D), F32)],
        compiler_params=_params(("arbitrary",)),
    )(du, dhbc, x, dx1, w_in_all, g1)


def _matmul_tn(a, b, name, col_shards=1):
    L, K = a.shape
    N = b.shape[1]
    tl = _tile(L, TL_TN)
    tk = _tile(K, 1024)
    nw = N // col_shards
    tn = _tile(nw, 1024)
    npb = nw // tn

    def body(a_ref, b_ref, o_ref):
        @pl.when(pl.program_id(2) == 0)
        def _():
            o_ref[...] = jnp.zeros_like(o_ref)

        o_ref[0] += _dot_tn(a_ref[...], b_ref[...])

    return pl.pallas_call(
        body, name=name, grid=(K // tk, N // tn, L // tl),
        in_specs=[pl.BlockSpec((tl, tk), lambda k, n, l: (l, k)), pl.BlockSpec((tl, tn), lambda k, n, l: (l, n))],
        out_specs=pl.BlockSpec((1, tk, tn), lambda k, n, l: (n // npb, k, n % npb)),
        out_shape=jax.ShapeDtypeStruct((col_shards, K, nw), F32),
        compiler_params=_params(("arbitrary", "arbitrary", "arbitrary")),
    )(a, b)


def _ssm_discretize(lam_re, lam_im, log_dt, b_re, b_im):
    dt = jnp.exp(log_dt)[:, None]
    zr = lam_re * dt
    zi = lam_im * dt
    mag = jnp.exp(zr)
    abr = mag * jnp.cos(zi)
    abi = mag * jnp.sin(zi)
    nr, ni = abr - 1.0, abi
    den = lam_re * lam_re + lam_im * lam_im
    coef_r = (nr * lam_re + ni * lam_im) / den
    coef_i = (ni * lam_re - nr * lam_im) / den
    bbar_r = coef_r[..., None] * b_re - coef_i[..., None] * b_im
    bbar_i = coef_r[..., None] * b_im + coef_i[..., None] * b_re
    return zr, zi, bbar_r, bbar_i


def _block_diag(t):
    nb, g, r, c = t.shape
    eye = jnp.eye(g, dtype=t.dtype)
    return (t[:, :, :, None, :] * eye[None, :, None, :, None]).reshape(nb, g * r, g * c)


def _diag_blocks(m, r, c):
    nb = m.shape[0]
    g = m.shape[1] // r
    eye = jnp.eye(g, dtype=m.dtype)
    t = (m.reshape(nb, g, r, g, c) * eye[None, :, None, :, None]).sum(axis=3)
    return t.reshape(nb * g, r, c)


def _scan_tables(ar, ai, reverse):
    def mul(p, q):
        return (p[0] * q[0] - p[1] * q[1], p[0] * q[1] + p[1] * q[0])

    a1 = (ar, -ai if reverse else ai)
    a2 = mul(a1, a1)
    a3 = mul(a2, a1)
    a4 = mul(a2, a2)
    pw = [a1, a2, a3, a4, mul(a4, a1), mul(a4, a2), mul(a4, a3), mul(a4, a4)]
    t = jnp.arange(SUBLANES)[:, None]
    tabs = []
    for k, ak in ((1, a1), (2, a2), (4, a4)):
        mask = (t + k <= SUBLANES - 1) if reverse else (t >= k)
        tabs += [jnp.where(mask, ak[0][None, :], 0.0), jnp.where(mask, ak[1][None, :], 0.0)]
    order = range(SUBLANES - 1, -1, -1) if reverse else range(SUBLANES)
    tabs += [jnp.stack([pw[j][0] for j in order]), jnp.stack([pw[j][1] for j in order])]
    coef = jnp.stack(tabs)
    return coef.reshape(8, SUBLANES, N_GBLK, WB).transpose(2, 0, 1, 3)


def _ssm_matrices(lam_re, lam_im, log_dt, b_re, b_im, c_re, c_im):
    zr, zi, bbar_r, bbar_i = _ssm_discretize(lam_re, lam_im, log_dt, b_re, b_im)
    mag = jnp.exp(zr)
    ar = (mag * jnp.cos(zi)).reshape(-1)
    ai = (mag * jnp.sin(zi)).reshape(-1)
    blk = lambda t: t.reshape(N_GBLK, G_PER_BLK, *t.shape[1:])
    bmat = jnp.concatenate([_block_diag(blk(bbar_r).transpose(0, 1, 3, 2)), _block_diag(blk(bbar_i).transpose(0, 1, 3, 2))], axis=2)
    cmat = jnp.concatenate([_block_diag(blk(c_re).transpose(0, 1, 3, 2)), _block_diag(blk(-c_im).transpose(0, 1, 3, 2))], axis=1)
    return bmat.astype(MXU_DTYPE), cmat.astype(MXU_DTYPE), _scan_tables(ar, ai, False), _scan_tables(ar, ai, True)


def _ssm_param_grads(lam_re, lam_im, log_dt, b_re, b_im, gb, gc, q, gd):
    gbr = _diag_blocks(gb[:, :WB, :], STATE, GROUP)
    gbi = _diag_blocks(gb[:, WB:, :], STATE, GROUP)
    d_c_re = _diag_blocks(gc[:, :, :WB], GROUP, STATE)
    d_c_im = -_diag_blocks(gc[:, :, WB:], GROUP, STATE)
    qr = q[:, 0, :].reshape(N_GROUPS, STATE)
    qi = q[:, 1, :].reshape(N_GROUPS, STATE)
    _, vjp = jax.vjp(_ssm_discretize, lam_re, lam_im, log_dt, b_re, b_im)
    d_lam_re, d_lam_im, d_log_dt, d_b_re, d_b_im = vjp((qr, qi, gbr, gbi))
    return d_lam_re, d_lam_im, d_log_dt, d_b_re, d_b_im, d_c_re, d_c_im, gd.reshape(N_GROUPS, GROUP)


def _local_step(x, target, p, w_in_all, w_glu, conv_w, w_out, w_up_all, w_down):
    g1 = p["g_pre_mix"][None]
    bmat, cmat, coef_f, coef_r = _ssm_matrices(p["lam_re"], p["lam_im"], p["log_dt"], p["b_re"], p["b_im"], p["c_re"], p["c_im"])
    dskip = p["d_skip"].reshape(N_GBLK, 1, UB)
    g_ssm, g_conv = p["g_ssm_out"][None], p["g_conv_out"][None]
    g_post_mix, g_pre_mlp, g_post_mlp = p["g_post_mix"][None], p["g_pre_mlp"][None], p["g_post_mlp"][None]

    hn, proj = _inproj_fwd(x, g1, w_in_all)
    s_re, s_im, ys = _s5_fwd(proj, bmat, cmat, coef_f, dskip)
    ycat, o, x1 = _tail_fwd(x, ys, proj, w_glu, conv_w, g_ssm, g_conv, w_out, g_post_mix)
    hn2, up, m, dx2, loss = _mlp_fwd(x1, target, w_up_all, w_down, g_pre_mlp, g_post_mlp)

    dm, dup, act, dx1, dg_post_mlp, dg_pre_mlp = _mlp_bwd(dx2, m, up, x1, w_up_all, w_down, g_pre_mlp, g_post_mlp)
    gw_down = _matmul_tn(act, dm, "dw_down").reshape(N_CHIPS, D_FF // N_CHIPS, D_MODEL)
    gw_up = _matmul_tn(hn2, dup, "dw_up", col_shards=N_CHIPS)
    do, da, y1, dys, dhbc, dg_post_mix, dg_ssm, dg_conv, dconv_w = _tail_bwd(dx1, o, ys, proj, w_glu, conv_w, g_ssm, g_conv, w_out, g_post_mix)
    gw_out = _matmul_tn(ycat, do, "dw_out").reshape(N_CHIPS, D_MODEL // N_CHIPS, D_MODEL)
    gw_glu = _matmul_tn(y1, da, "dw_glu").reshape(N_CHIPS, D_SSM // N_CHIPS, D_SSM)
    du, gb, gc, q, gd = _s5_bwd(dys, proj, s_re, s_im, bmat, cmat, coef_r, dskip)
    grad_x, dproj, dg_pre_mix = _inproj_bwd(du, dhbc, x, dx1, w_in_all, g1)
    gw_in = _matmul_tn(hn, dproj, "dw_in", col_shards=N_CHIPS)
    d_lam_re, d_lam_im, d_log_dt, d_b_re, d_b_im, d_c_re, d_c_im, d_d_skip = _ssm_param_grads(
        p["lam_re"], p["lam_im"], p["log_dt"], p["b_re"], p["b_im"], gb, gc, q, gd)
    small = {
        "g_pre_mix": dg_pre_mix[0], "lam_re": d_lam_re, "lam_im": d_lam_im, "log_dt": d_log_dt, "b_re": d_b_re, "b_im": d_b_im,
        "c_re": d_c_re, "c_im": d_c_im, "d_skip": d_d_skip, "conv_w": dconv_w[:3], "g_ssm_out": dg_ssm[0], "g_conv_out": dg_conv[0],
        "g_post_mix": dg_post_mix[0], "g_pre_mlp": dg_pre_mlp[0], "g_post_mlp": dg_post_mlp[0],
    }
    big = {"w_in": gw_in, "w_glu": gw_glu, "w_out": gw_out, "w_up": gw_up, "w_down": gw_down}
    return loss[0, 0], grad_x, big, small


HBM = pl.BlockSpec(memory_space=pltpu.HBM)
VMEM = pl.BlockSpec(memory_space=pltpu.VMEM)
SMEM = pl.BlockSpec(memory_space=pltpu.SMEM)


def _place():
    x, y, c = lax.axis_index("x"), lax.axis_index("y"), lax.axis_index("c")
    chips = [(1 - x, y), (x, 1 - y), (1 - x, 1 - y)]
    return (x, y, c), 2 * x + y, (x, y, 1 - c), chips


def _remote(src, dst, send_sem, recv_sem, device):
    return pltpu.make_async_remote_copy(src_ref=src, dst_ref=dst, send_sem=send_sem, recv_sem=recv_sem,
                                        device_id=device, device_id_type=MESH)


def _half(ref_rows, c):
    h = ref_rows // 2
    return pl.ds(c * h, h)


def _all_gather_weights(shards, conv_shard):
    nt = len(shards)

    def body(*refs):
        ins, outs = refs[:nt + 1], refs[nt + 1:2 * nt + 2]
        isend, irecv, fsend, frecv, lsem = refs[2 * nt + 2:]
        (x, y, c), me, sibling, chips = _place()
        ids = [2 * px + py for px, py in chips]
        local = [pltpu.make_async_copy(ins[t], outs[t].at[me], lsem.at[t]) for t in range(nt + 1)]
        for cp in local:
            cp.start()

        def ici(t, k, origin):
            if t == nt:
                return _remote(ins[t], outs[t].at[origin], isend.at[t, k], irecv.at[t, k], (*chips[k], c))
            rows = _half(shards[t].shape[0], c)
            src = ins[t].at[rows, :] if origin is me else outs[t].at[origin, rows, :]
            return _remote(src, outs[t].at[origin, rows, :], isend.at[t, k], irecv.at[t, k], (*chips[k], c))

        def fwd(t, k, core):
            rows = _half(shards[t].shape[0], core)
            blk = outs[t].at[ids[k], rows, :]
            return _remote(blk, blk, fsend.at[t, k], frecv.at[t, k], sibling)

        sends = [ici(t, k, me) for t in range(nt + 1) for k in range(3)]
        for cp in sends:
            cp.start()
        for t in range(nt):
            for k in range(3):
                ici(t, k, ids[k]).wait_recv()
                cp = fwd(t, k, c)
                cp.start()
                sends.append(cp)
        for k in range(3):
            ici(nt, k, ids[k]).wait_recv()
        for t in range(nt):
            for k in range(3):
                fwd(t, k, 1 - c).wait_recv()
        for cp in sends:
            cp.wait_send()
        for cp in local:
            cp.wait()

    arrays = list(shards) + [conv_shard]
    return pl.pallas_call(
        body, name="ag_weights",
        in_specs=[HBM] * (nt + 1), out_specs=[HBM] * (nt + 1),
        out_shape=[jax.ShapeDtypeStruct((N_CHIPS, *a.shape), a.dtype) for a in arrays],
        scratch_shapes=[pltpu.SemaphoreType.DMA((nt + 1, 3)), pltpu.SemaphoreType.DMA((nt + 1, 3)),
                        pltpu.SemaphoreType.DMA((nt, 3)), pltpu.SemaphoreType.DMA((nt, 3)), pltpu.SemaphoreType.DMA((nt + 1,))],
    )(*arrays)


def _pair_exchange(grads):
    nt = len(grads)

    def body(*refs):
        ins, outs = refs[:nt], refs[nt:2 * nt]
        send, recv = refs[2 * nt:]
        (x, y, c), me, sibling, chips = _place()
        cps = [_remote(ins[t].at[:, _half(grads[t].shape[1], 1 - c), :], outs[t], send.at[t], recv.at[t], sibling) for t in range(nt)]
        for cp in cps:
            cp.start()
        for cp in cps:
            cp.wait()

    return pl.pallas_call(
        body, name="rs_pair", in_specs=[HBM] * nt, out_specs=[HBM] * nt,
        out_shape=[jax.ShapeDtypeStruct((g.shape[0], g.shape[1] // 2, g.shape[2]), g.dtype) for g in grads],
        scratch_shapes=[pltpu.SemaphoreType.DMA((nt,)), pltpu.SemaphoreType.DMA((nt,))],
    )(*grads)


def _chip_exchange(parts):
    nt = len(parts)

    def body(*refs):
        ins, outs = refs[:nt], refs[nt:2 * nt]
        send, recv, lsem = refs[2 * nt:]
        (x, y, c), me, sibling, chips = _place()
        ids = [2 * px + py for px, py in chips]
        local = [pltpu.make_async_copy(ins[t].at[me], outs[t].at[me], lsem.at[t]) for t in range(nt)]
        for cp in local:
            cp.start()
        cps = [_remote(ins[t].at[ids[k]], outs[t].at[me], send.at[t, k], recv.at[t, k], (*chips[k], c)) for t in range(nt) for k in range(3)]
        for cp in cps:
            cp.start()
        for t in range(nt):
            for k in range(3):
                _remote(ins[t].at[ids[k]], outs[t].at[ids[k]], send.at[t, k], recv.at[t, k], (*chips[k], c)).wait_recv()
        for cp in cps:
            cp.wait_send()
        for cp in local:
            cp.wait()

    return pl.pallas_call(
        body, name="rs_ici", in_specs=[HBM] * nt, out_specs=[HBM] * nt,
        out_shape=[jax.ShapeDtypeStruct(p.shape, p.dtype) for p in parts],
        scratch_shapes=[pltpu.SemaphoreType.DMA((nt, 3)), pltpu.SemaphoreType.DMA((nt, 3)), pltpu.SemaphoreType.DMA((nt,))],
    )(*parts)


def _share_halves(grads):
    nt = len(grads)

    def body(*refs):
        outs = refs[nt:2 * nt]
        send, recv = refs[2 * nt:]
        (x, y, c), me, sibling, chips = _place()
        cps = []
        for t in range(nt):
            mine = outs[t].at[_half(grads[t].shape[0], c), :]
            cps.append(_remote(mine, mine, send.at[t], recv.at[t], sibling))
        for cp in cps:
            cp.start()
        for t in range(nt):
            theirs = outs[t].at[_half(grads[t].shape[0], 1 - c), :]
            _remote(theirs, theirs, send.at[t], recv.at[t], sibling).wait_recv()
        for cp in cps:
            cp.wait_send()

    return pl.pallas_call(
        body, name="rs_share", in_specs=[HBM] * nt, out_specs=[HBM] * nt,
        out_shape=[jax.ShapeDtypeStruct(g.shape, g.dtype) for g in grads],
        input_output_aliases={t: t for t in range(nt)},
        scratch_shapes=[pltpu.SemaphoreType.DMA((nt,)), pltpu.SemaphoreType.DMA((nt,))],
    )(*grads)


def _all_reduce_small(pack):
    m, n = pack.shape

    def body(x_ref, sum_ref, all_ref, send, recv, lsem):
        (x, y, c), me, sibling, chips = _place()

        def rows(px, py, pc):
            return all_ref.at[pl.ds((4 * px + 2 * py + pc) * m, m), :]

        def copy(k, block, to, src=None):
            return _remote(rows(*block) if src is None else src, rows(*block), send.at[k], recv.at[k], to)

        mine = pltpu.make_async_copy(x_ref, rows(x, y, c), lsem)
        mine.start()
        first = [copy(0, (x, y, c), sibling, src=x_ref)]
        first += [copy(1 + j, (x, y, c), (*chip, c), src=x_ref) for j, chip in enumerate(chips)]
        for cp in first:
            cp.start()
        passed = [copy(4 + j, (*chip, c), sibling) for j, chip in enumerate(chips)]
        for j, chip in enumerate(chips):
            copy(1 + j, (*chip, c), (x, y, c)).wait_recv()
            passed[j].start()
        copy(0, (x, y, 1 - c), (x, y, c)).wait_recv()
        for j, chip in enumerate(chips):
            copy(4 + j, (*chip, 1 - c), (x, y, c)).wait_recv()
        for cp in first + passed:
            cp.wait_send()
        mine.wait()
        total = all_ref[0:m, :]
        for d in range(1, N_DEV):
            total = total + all_ref[d * m:(d + 1) * m, :]
        sum_ref[...] = total

    return pl.pallas_call(
        body, name="ar_small", in_specs=[VMEM], out_specs=[VMEM, VMEM],
        out_shape=[jax.ShapeDtypeStruct((m, n), pack.dtype), jax.ShapeDtypeStruct((N_DEV * m, n), pack.dtype)],
        scratch_shapes=[pltpu.SemaphoreType.DMA((7,)), pltpu.SemaphoreType.DMA((7,)), pltpu.SemaphoreType.DMA],
        compiler_params=pltpu.CompilerParams(vmem_limit_bytes=VMEM_LIMIT),
    )(pack)[0]


def _row_tile(rows, n):
    return _tile(rows, max(SUBLANES, (2 * 1024 * 1024) // (4 * n)))


def _pair_add(grad, other, core, name):
    ns, h, n = other.shape
    tr = _row_tile(h, n)
    nb = h // tr

    def body(c_ref, g_ref, o_ref, out_ref):
        out_ref[...] = g_ref[...] + o_ref[...]

    return pl.pallas_call(
        body, name=name,
        grid_spec=pltpu.PrefetchScalarGridSpec(
            num_scalar_prefetch=1, grid=(ns, nb),
            in_specs=[pl.BlockSpec((1, tr, n), lambda s, i, c: (s, c[0] * nb + i, 0)), pl.BlockSpec((1, tr, n), lambda s, i, c: (s, i, 0))],
            out_specs=pl.BlockSpec((1, tr, n), lambda s, i, c: (s, i, 0))),
        out_shape=jax.ShapeDtypeStruct(other.shape, F32),
        compiler_params=_params(("arbitrary", "arbitrary")),
    )(core, grad, other)


def _quad_sum(parts, core, name):
    ns, h, n = parts.shape
    tr = _row_tile(h, n)
    nb = h // tr

    def body(c_ref, p_ref, out_ref):
        out_ref[...] = ((p_ref[0] + p_ref[1]) + p_ref[2]) + p_ref[3]

    return pl.pallas_call(
        body, name=name,
        grid_spec=pltpu.PrefetchScalarGridSpec(
            num_scalar_prefetch=1, grid=(nb,),
            in_specs=[pl.BlockSpec((ns, tr, n), lambda i, c: (0, i, 0))],
            out_specs=pl.BlockSpec((tr, n), lambda i, c: (c[0] * nb + i, 0))),
        out_shape=jax.ShapeDtypeStruct((2 * h, n), F32),
        compiler_params=_params(("arbitrary",)),
    )(core, parts)


def _adamw_math(w, g, m, v):
    m = ADAM_B1 * m + (1.0 - ADAM_B1) * g
    v = ADAM_B2 * v + (1.0 - ADAM_B2) * jnp.square(g)
    m_hat = m / (1.0 - ADAM_B1 ** ADAM_STEP)
    v_hat = v / (1.0 - ADAM_B2 ** ADAM_STEP)
    delta = -ADAM_LR * (m_hat / (jnp.sqrt(v_hat) + ADAM_EPS) + ADAM_WD * w)
    return delta, m, v


def _adamw(w, g, m, v, name):
    r, n = w.shape
    tr = _row_tile(r, n)

    def body(w_ref, g_ref, m_ref, v_ref, d_ref, nm_ref, nv_ref):
        d_ref[...], nm_ref[...], nv_ref[...] = _adamw_math(w_ref[...], g_ref[...], m_ref[...], v_ref[...])

    spec = pl.BlockSpec((tr, n), lambda i: (i, 0))
    return pl.pallas_call(
        body, name=name, grid=(r // tr,), in_specs=[spec] * 4, out_specs=[spec] * 3,
        out_shape=[jax.ShapeDtypeStruct((r, n), F32)] * 3,
        compiler_params=_params(("arbitrary",)),
    )(w, g, m, v)


SMALL_SHAPES = {
    "g_pre_mix": (D_MODEL,), "lam_re": (N_GROUPS, STATE), "lam_im": (N_GROUPS, STATE), "log_dt": (N_GROUPS,),
    "b_re": (N_GROUPS, STATE, GROUP), "b_im": (N_GROUPS, STATE, GROUP), "c_re": (N_GROUPS, GROUP, STATE), "c_im": (N_GROUPS, GROUP, STATE),
    "d_skip": (N_GROUPS, GROUP), "conv_w": (3, D_CONV), "g_ssm_out": (D_SSM,), "g_conv_out": (D_CONV,),
    "g_post_mix": (D_MODEL,), "g_pre_mlp": (D_MODEL,), "g_post_mlp": (D_MODEL,),
}
LANES = 128
PACK_TILE = SUBLANES * LANES


def _pack_rows(name):
    n = math.prod(SMALL_SHAPES[name])
    return SUBLANES * (-(-n // PACK_TILE))


def _pack_offsets():
    offs, row = {}, 0
    for name in SMALL_SHAPES:
        offs[name] = row
        row += _pack_rows(name)
    return offs, row


def _pack_small(grads):
    parts = []
    for name in SMALL_SHAPES:
        flat = grads[name].reshape(-1)
        parts.append(jnp.pad(flat, (0, _pack_rows(name) * LANES - flat.shape[0])).reshape(-1, LANES))
    return jnp.concatenate(parts, axis=0)


def _lane_shape(name):
    n = math.prod(SMALL_SHAPES[name])
    return (n // LANES, LANES) if n % LANES == 0 else (1, n)


def _adamw_small(pack, conv_grad, w, m, v):
    names = list(SMALL_SHAPES)
    offs, _ = _pack_offsets()
    nn = len(names)

    def body(*refs):
        pack_ref, cg_ref = refs[0], refs[1]
        w_refs, m_refs, v_refs = refs[2:2 + nn], refs[2 + nn:2 + 2 * nn], refs[2 + 2 * nn:2 + 3 * nn]
        outs = refs[2 + 3 * nn:]
        for j, name in enumerate(names):
            r, n = w_refs[j].shape
            g = cg_ref[...] if name == "conv_w" else pack_ref[offs[name]:offs[name] + r, 0:n]
            delta, nm, nv = _adamw_math(w_refs[j][...], g, m_refs[j][...], v_refs[j][...])
            outs[j][...] = g
            outs[nn + j][...] = delta
            outs[2 * nn + j][...] = nm
            outs[3 * nn + j][...] = nv

    args = [pack, conv_grad] + [w[k] for k in names] + [m[k] for k in names] + [v[k] for k in names]
    res = pl.pallas_call(
        body, name="adamw_small", in_specs=[VMEM] * len(args), out_specs=[VMEM] * (4 * nn),
        out_shape=[jax.ShapeDtypeStruct(w[k].shape, F32) for k in names] * 4,
    )(*args)
    return [dict(zip(names, res[q * nn:(q + 1) * nn])) for q in range(4)]


WEIGHTS = ["g_pre_mix", "w_in", "lam_re", "lam_im", "log_dt", "b_re", "b_im", "c_re", "c_im", "d_skip", "w_glu", "conv_w",
           "g_ssm_out", "g_conv_out", "w_out", "g_post_mix", "g_pre_mlp", "w_up", "w_down", "g_post_mlp"]
BIG = ["w_in", "w_glu", "w_out", "w_up", "w_down"]


def kernel(x, g_pre_mix, w_in, lam_re, lam_im, log_dt, b_re, b_im, c_re, c_im, d_skip, w_glu, conv_w, g_ssm_out, g_conv_out, w_out, g_post_mix, g_pre_mlp, w_up, w_down, g_post_mlp, loss_target, m_g_pre_mix, m_w_in, m_lam_re, m_lam_im, m_log_dt, m_b_re, m_b_im, m_c_re, m_c_im, m_d_skip, m_w_glu, m_conv_w, m_g_ssm_out, m_g_conv_out, m_w_out, m_g_post_mix, m_g_pre_mlp, m_w_up, m_w_down, m_g_post_mlp, v_g_pre_mix, v_w_in, v_lam_re, v_lam_im, v_log_dt, v_b_re, v_b_im, v_c_re, v_c_im, v_d_skip, v_w_glu, v_conv_w, v_g_ssm_out, v_g_conv_out, v_w_out, v_g_post_mix, v_g_pre_mlp, v_w_up, v_w_down, v_g_post_mlp):
    w = dict(g_pre_mix=g_pre_mix, w_in=w_in, lam_re=lam_re, lam_im=lam_im, log_dt=log_dt, b_re=b_re, b_im=b_im, c_re=c_re, c_im=c_im,
             d_skip=d_skip, w_glu=w_glu, conv_w=conv_w, g_ssm_out=g_ssm_out, g_conv_out=g_conv_out, w_out=w_out, g_post_mix=g_post_mix,
             g_pre_mlp=g_pre_mlp, w_up=w_up, w_down=w_down, g_post_mlp=g_post_mlp)
    m = dict(g_pre_mix=m_g_pre_mix, w_in=m_w_in, lam_re=m_lam_re, lam_im=m_lam_im, log_dt=m_log_dt, b_re=m_b_re, b_im=m_b_im, c_re=m_c_re,
             c_im=m_c_im, d_skip=m_d_skip, w_glu=m_w_glu, conv_w=m_conv_w, g_ssm_out=m_g_ssm_out, g_conv_out=m_g_conv_out, w_out=m_w_out,
             g_post_mix=m_g_post_mix, g_pre_mlp=m_g_pre_mlp, w_up=m_w_up, w_down=m_w_down, g_post_mlp=m_g_post_mlp)
    v = dict(g_pre_mix=v_g_pre_mix, w_in=v_w_in, lam_re=v_lam_re, lam_im=v_lam_im, log_dt=v_log_dt, b_re=v_b_re, b_im=v_b_im, c_re=v_c_re,
             c_im=v_c_im, d_skip=v_d_skip, w_glu=v_w_glu, conv_w=v_conv_w, g_ssm_out=v_g_ssm_out, g_conv_out=v_g_conv_out, w_out=v_w_out,
             g_post_mix=v_g_post_mix, g_pre_mlp=v_g_pre_mlp, w_up=v_w_up, w_down=v_w_down, g_post_mlp=v_g_post_mlp)
    shapes = {k: a.shape for k, a in w.items()}
    w, m, v = ({k: a[0] for k, a in d.items()} for d in (w, m, v))
    chip = 2 * lax.axis_index("x") + lax.axis_index("y")
    core = lax.axis_index("c").astype(jnp.int32).reshape(1)

    conv_pad = jnp.pad(w["conv_w"], ((0, SUBLANES - 3), (0, 0)))
    gathered = _all_gather_weights([w[k].astype(MXU_DTYPE) for k in BIG], conv_pad)
    w_in_all, w_glu_all, w_out_all, w_up_all, w_down_all, conv_all = gathered
    conv_full = jnp.transpose(conv_all, (1, 0, 2)).reshape(SUBLANES, D_CONV)
    small_w = {k: w[k] for k in SMALL_SHAPES if k != "conv_w"}
    loss, grad_x, big, small = _local_step(
        x[0], loss_target[0], small_w, w_in_all, w_glu_all.reshape(D_SSM, D_SSM), conv_full,
        w_out_all.reshape(D_MODEL, D_MODEL), w_up_all, w_down_all.reshape(D_FF, D_MODEL))
    loss = lax.psum(loss, ("x", "y", "c"))

    grads = [big[k] for k in BIG]
    others = _pair_exchange(grads)
    parts = [_pair_add(g, o, core, "pair_add_" + k) for k, g, o in zip(BIG, grads, others)]
    quads = _chip_exchange(parts)
    halves = [_quad_sum(q, core, "quad_sum_" + k) for k, q in zip(BIG, quads)]
    shard_grads = dict(zip(BIG, _share_halves(halves)))
    pack = _all_reduce_small(_pack_small(small))
    offs, _ = _pack_offsets()
    conv_grad = pack[offs["conv_w"]:offs["conv_w"] + 3 * D_CONV // LANES].reshape(3, D_CONV)
    conv_grad = lax.dynamic_slice(conv_grad, (0, chip * (D_CONV // N_CHIPS)), (3, D_CONV // N_CHIPS))

    out = {q: {} for q in ("grad", "delta", "new_m", "new_v")}
    for k in BIG:
        out["grad"][k] = shard_grads[k]
        out["delta"][k], out["new_m"][k], out["new_v"][k] = _adamw(w[k], shard_grads[k], m[k], v[k], "adamw_" + k)
    lane = lambda d: {k: (d[k] if k == "conv_w" else d[k].reshape(_lane_shape(k))) for k in SMALL_SHAPES}
    res = _adamw_small(pack, conv_grad, lane(w), lane(m), lane(v))
    for q, d in zip(("grad", "delta", "new_m", "new_v"), res):
        out[q].update(d)
    flat = [loss, grad_x[None]]
    for q in ("grad", "delta", "new_m", "new_v"):
        flat += [out[q][k].reshape(shapes[k]) for k in WEIGHTS]
    return tuple(flat)
```

```python
import functools
import math

import jax
import jax.numpy as jnp
from jax import lax
from jax.experimental import pallas as pl
from jax.experimental.pallas import tpu as pltpu

F32 = jnp.float32
MXU_DTYPE = jnp.bfloat16
WIRE_DTYPE = jnp.bfloat16

D_MODEL = 1024
D_SSM = 512
D_CONV = 512
N_GROUPS = 32
GROUP = 16
STATE = 64
D_FF = 4096
RMS_EPS = 1e-6
N_CHIPS = 4
N_DEV = 8

ADAM_LR = 0.001
ADAM_B1 = 0.9
ADAM_B2 = 0.999
ADAM_EPS = 1e-08
ADAM_WD = 0.01
ADAM_STEP = 10

N_GBLK = 2
G_PER_BLK = N_GROUPS // N_GBLK
UB = G_PER_BLK * GROUP
WB = G_PER_BLK * STATE
LANE_CHUNK = 512
SUBLANES = 8

TM_PROJ = 512
TM_S5 = 512
TM_TAIL = 256
TM_MLP = 256
TL_TN = 512
VMEM_LIMIT = 56 * 1024 * 1024

MESH = pl.DeviceIdType.MESH


def _params(sem, vmem=VMEM_LIMIT):
    return pltpu.CompilerParams(dimension_semantics=sem, vmem_limit_bytes=vmem)


def _resident(shape):
    nd = len(shape)
    return pl.BlockSpec(shape, lambda *_: (0,) * nd, pipeline_mode=pl.Buffered(1))


def _dot(a, b):
    return jnp.dot(a, b, preferred_element_type=F32)


def _dot_nt(a, b):
    return lax.dot_general(a, b, (((1,), (1,)), ((), ())), preferred_element_type=F32)


def _dot_tn(a, b):
    return lax.dot_general(a, b, (((0,), (0,)), ((), ())), preferred_element_type=F32)


def _rms_fwd(x, g):
    r = lax.rsqrt(jnp.mean(x * x, axis=-1, keepdims=True) + RMS_EPS)
    return x * r * g


def _rms_bwd(x, g, dy):
    r = lax.rsqrt(jnp.mean(x * x, axis=-1, keepdims=True) + RMS_EPS)
    xn = x * r
    q = dy * g
    dx = r * (q - xn * jnp.mean(q * xn, axis=-1, keepdims=True))
    return dx, jnp.sum(dy * xn, axis=0, keepdims=True)


_GELU_C = math.sqrt(2.0 / math.pi)


def _gelu(x):
    t = jnp.tanh(_GELU_C * (x + 0.044715 * (x * x * x)))
    y = x * (0.5 * (1.0 + t))
    dy = 0.5 * (1.0 + t) + 0.5 * x * (1.0 - t * t) * (_GELU_C * (1.0 + 3 * 0.044715 * (x * x)))
    return y, dy


def _tile(n, pref):
    t = min(n, pref)
    assert n % t == 0, (n, t)
    return t


HBM = pl.BlockSpec(memory_space=pltpu.HBM)
VMEM = pl.BlockSpec(memory_space=pltpu.VMEM)
DMA_SEMS = pltpu.SemaphoreType.DMA


def _place():
    x, y, c = lax.axis_index("x"), lax.axis_index("y"), lax.axis_index("c")
    chips = [(1 - x, y), (x, 1 - y), (1 - x, 1 - y)]
    return (x, y, c), 2 * x + y, (x, y, 1 - c), chips, [2 * px + py for px, py in chips]


def _remote(src, dst, send_sem, recv_sem, device):
    return pltpu.make_async_remote_copy(src_ref=src, dst_ref=dst, send_sem=send_sem, recv_sem=recv_sem,
                                        device_id=device, device_id_type=MESH)


def _half(rows, c):
    return pl.ds(c * (rows // 2), rows // 2)


class _Exchange:
    aliases = {}

    def start(self, ins, outs, sems):
        local, outgoing, _ = self._copies(ins, outs, sems)
        for cp in local + outgoing:
            cp.start()

    def finish(self, ins, outs, sems):
        local, outgoing, incoming = self._copies(ins, outs, sems)
        for cp in incoming:
            cp.wait_recv()
        for cp in outgoing:
            cp.wait_send()
        for cp in local:
            cp.wait()


class _Gather(_Exchange):
    def __init__(self, shards, split):
        self.inputs, self.split = list(shards), split
        self.out_shape = [jax.ShapeDtypeStruct((N_CHIPS, *a.shape), a.dtype) for a in shards]
        self.sems = [DMA_SEMS((len(shards), 3)), DMA_SEMS((len(shards), 3)), DMA_SEMS((len(shards),))]

    def _copies(self, ins, outs, sems):
        send, recv, lsem = sems
        (x, y, c), me, sibling, chips, ids = _place()
        local = [pltpu.make_async_copy(ins[t], outs[t].at[me], lsem.at[t]) for t in range(len(ins))]
        outgoing, incoming = [], []
        for t, a in enumerate(self.inputs):
            rows = _half(a.shape[0], c) if self.split[t] else pl.ds(0, a.shape[0])
            for k in range(3):
                to = (*chips[k], c)
                outgoing.append(_remote(ins[t].at[rows, :], outs[t].at[me, rows, :], send.at[t, k], recv.at[t, k], to))
                incoming.append(_remote(ins[t].at[rows, :], outs[t].at[ids[k], rows, :], send.at[t, k], recv.at[t, k], to))
        return local, outgoing, incoming


class _Forward(_Exchange):
    def __init__(self, arrays):
        self.inputs = list(arrays)
        self.out_shape = [jax.ShapeDtypeStruct(a.shape, a.dtype) for a in arrays]
        self.aliases = {t: t for t in range(len(arrays))}
        self.sems = [DMA_SEMS((len(arrays), 3)), DMA_SEMS((len(arrays), 3))]

    def _copies(self, ins, outs, sems):
        send, recv = sems
        (x, y, c), me, sibling, chips, ids = _place()
        outgoing, incoming = [], []
        for t, a in enumerate(self.inputs):
            for k in range(3):
                mine = outs[t].at[ids[k], _half(a.shape[1], c), :]
                theirs = outs[t].at[ids[k], _half(a.shape[1], 1 - c), :]
                outgoing.append(_remote(mine, mine, send.at[t, k], recv.at[t, k], sibling))
                incoming.append(_remote(theirs, theirs, send.at[t, k], recv.at[t, k], sibling))
        return [], outgoing, incoming


class _Pair(_Exchange):
    def __init__(self, grads):
        self.inputs = list(grads)
        self.out_shape = [jax.ShapeDtypeStruct((g.shape[0], g.shape[1] // 2, g.shape[2]), g.dtype) for g in grads]
        self.sems = [DMA_SEMS((len(grads),)), DMA_SEMS((len(grads),))]

    def _copies(self, ins, outs, sems):
        send, recv = sems
        (x, y, c), me, sibling, chips, ids = _place()
        cps = [_remote(ins[t].at[:, _half(g.shape[1], 1 - c), :], outs[t], send.at[t], recv.at[t], sibling)
               for t, g in enumerate(self.inputs)]
        return [], cps, cps


class _Chip(_Exchange):
    def __init__(self, parts):
        self.inputs = list(parts)
        self.out_shape = [jax.ShapeDtypeStruct(p.shape, p.dtype) for p in parts]
        self.sems = [DMA_SEMS((len(parts), 3)), DMA_SEMS((len(parts), 3)), DMA_SEMS((len(parts),))]

    def _copies(self, ins, outs, sems):
        send, recv, lsem = sems
        (x, y, c), me, sibling, chips, ids = _place()
        local = [pltpu.make_async_copy(ins[t].at[me], outs[t].at[me], lsem.at[t]) for t in range(len(ins))]
        outgoing, incoming = [], []
        for t in range(len(ins)):
            for k in range(3):
                to = (*chips[k], c)
                outgoing.append(_remote(ins[t].at[ids[k]], outs[t].at[me], send.at[t, k], recv.at[t, k], to))
                incoming.append(_remote(ins[t].at[ids[k]], outs[t].at[ids[k]], send.at[t, k], recv.at[t, k], to))
        return local, outgoing, incoming


class _Share(_Exchange):
    def __init__(self, grads):
        self.inputs = list(grads)
        self.out_shape = [jax.ShapeDtypeStruct(g.shape, g.dtype) for g in grads]
        self.aliases = {t: t for t in range(len(grads))}
        self.sems = [DMA_SEMS((len(grads),)), DMA_SEMS((len(grads),))]

    def _copies(self, ins, outs, sems):
        send, recv = sems
        (x, y, c), me, sibling, chips, ids = _place()
        outgoing, incoming = [], []
        for t, g in enumerate(self.inputs):
            mine = outs[t].at[_half(g.shape[0], c), :]
            theirs = outs[t].at[_half(g.shape[0], 1 - c), :]
            outgoing.append(_remote(mine, mine, send.at[t], recv.at[t], sibling))
            incoming.append(_remote(theirs, theirs, send.at[t], recv.at[t], sibling))
        return [], outgoing, incoming


class _GatherSmall(_Exchange):
    def __init__(self, block):
        self.inputs = [block]
        self.out_shape = [jax.ShapeDtypeStruct((N_DEV, *block.shape), block.dtype)]
        self.sems = [DMA_SEMS((7,)), DMA_SEMS((7,)), DMA_SEMS(())]

    def _copies(self, ins, outs, sems):
        send, recv, lsem = sems
        (x, y, c), me, sibling, chips, ids = _place()
        slot = lambda px, py, pc: outs[0].at[4 * px + 2 * py + pc]

        def copy(k, block, to, src=None):
            return _remote(slot(*block) if src is None else src, slot(*block), send.at[k], recv.at[k], to)

        local = [pltpu.make_async_copy(ins[0], slot(x, y, c), lsem)]
        first = [copy(0, (x, y, c), sibling, src=ins[0])] + [copy(1 + j, (x, y, c), (*chip, c), src=ins[0]) for j, chip in enumerate(chips)]
        passed = [copy(4 + j, (*chip, c), sibling) for j, chip in enumerate(chips)]
        landed = [copy(1 + j, (*chip, c), (x, y, c)) for j, chip in enumerate(chips)]
        from_sibling = [copy(0, (x, y, 1 - c), (x, y, c))] + [copy(4 + j, (*chip, 1 - c), (x, y, c)) for j, chip in enumerate(chips)]
        return local, first, (passed, landed, from_sibling)

    def finish(self, ins, outs, sems):
        local, first, (passed, landed, from_sibling) = self._copies(ins, outs, sems)
        for j in range(3):
            landed[j].wait_recv()
            passed[j].start()
        for cp in from_sibling:
            cp.wait_recv()
        for cp in first + passed:
            cp.wait_send()
        for cp in local:
            cp.wait()


def _split_refs(refs, counts):
    out = []
    for n in counts:
        out.append(refs[:n])
        refs = refs[n:]
    return out


def _each_exchange(exchanges, method, x_in, x_out, x_sem):
    for ex in exchanges:
        ni, no, ns = len(ex.inputs), len(ex.out_shape), len(ex.sems)
        getattr(ex, method)(x_in[:ni], x_out[:no], x_sem[:ns])
        x_in, x_out, x_sem = x_in[ni:], x_out[no:], x_sem[ns:]


def _call(body, *, name, grid, in_specs, out_specs, out_shape, operands, semantics, scratch_shapes=(), exchanges=()):
    x_in = [a for ex in exchanges for a in ex.inputs]
    x_out = [s for ex in exchanges for s in ex.out_shape]
    x_sem = [s for ex in exchanges for s in ex.sems]
    counts = (len(in_specs), len(x_in), len(out_specs), len(x_out), len(scratch_shapes), len(x_sem))
    aliases, i0, o0 = {}, len(in_specs), len(out_specs)
    for ex in exchanges:
        aliases.update({i0 + i: o0 + o for i, o in ex.aliases.items()})
        i0, o0 = i0 + len(ex.inputs), o0 + len(ex.out_shape)

    def full_body(*refs):
        ins, xi, outs, xo, scr, xs = _split_refs(list(refs), counts)
        if exchanges:
            @pl.when(functools.reduce(jnp.logical_and, [pl.program_id(a) == 0 for a in range(len(grid))]))
            def _():
                _each_exchange(exchanges, "start", xi, xo, xs)

        body(*ins, *outs, *scr)
        if exchanges:
            @pl.when(functools.reduce(jnp.logical_and, [pl.program_id(a) == grid[a] - 1 for a in range(len(grid))]))
            def _():
                _each_exchange(exchanges, "finish", xi, xo, xs)

    return pl.pallas_call(
        full_body, name=name, grid=grid,
        in_specs=list(in_specs) + [HBM] * len(x_in), out_specs=list(out_specs) + [HBM] * len(x_out),
        out_shape=list(out_shape) + x_out, scratch_shapes=list(scratch_shapes) + x_sem,
        input_output_aliases=aliases, compiler_params=_params(semantics),
    )(*operands, *x_in)


def _run_exchanges(exchanges, name):
    x_in = [a for ex in exchanges for a in ex.inputs]
    x_out = [s for ex in exchanges for s in ex.out_shape]
    x_sem = [s for ex in exchanges for s in ex.sems]
    aliases, i0, o0 = {}, 0, 0
    for ex in exchanges:
        aliases.update({i0 + i: o0 + o for i, o in ex.aliases.items()})
        i0, o0 = i0 + len(ex.inputs), o0 + len(ex.out_shape)

    def body(*refs):
        xi, xo, xs = _split_refs(list(refs), (len(x_in), len(x_out), len(x_sem)))
        _each_exchange(exchanges, "start", xi, xo, xs)
        _each_exchange(exchanges, "finish", xi, xo, xs)

    return pl.pallas_call(
        body, name=name, in_specs=[HBM] * len(x_in), out_specs=[HBM] * len(x_out), out_shape=x_out,
        scratch_shapes=x_sem, input_output_aliases=aliases,
    )(*x_in)


def _inproj_fwd(x, g1, w_in_all):
    L, D = x.shape
    ns, _, nc = w_in_all.shape
    tm = _tile(L, TM_PROJ)

    def body(x_ref, g_ref, w_ref, hn_ref, proj_ref):
        hn = _rms_fwd(x_ref[...], g_ref[...]).astype(MXU_DTYPE)
        hn_ref[...] = hn
        for j in range(ns):
            proj_ref[:, j * nc:(j + 1) * nc] = _dot(hn, w_ref[j])

    return pl.pallas_call(
        body, name="inproj_fwd", grid=(L // tm,),
        in_specs=[pl.BlockSpec((tm, D), lambda i: (i, 0)), _resident((1, D)), _resident(w_in_all.shape)],
        out_specs=[pl.BlockSpec((tm, D), lambda i: (i, 0)), pl.BlockSpec((tm, ns * nc), lambda i: (i, 0))],
        out_shape=[jax.ShapeDtypeStruct((L, D), MXU_DTYPE), jax.ShapeDtypeStruct((L, ns * nc), F32)],
        compiler_params=_params(("arbitrary",)),
    )(x, g1, w_in_all)


def _scan_tile(xr, xi, hr, hi, coef_ref, lanes, reverse):
    for k, j in ((1, 0), (2, 2), (4, 4)):
        ar = coef_ref[0, j, :, lanes]
        ai = coef_ref[0, j + 1, :, lanes]
        shift = SUBLANES - k if reverse else k
        sr = pltpu.roll(xr, shift, 0)
        si = pltpu.roll(xi, shift, 0)
        xr, xi = xr + (ar * sr - ai * si), xi + (ar * si + ai * sr)
    pr = coef_ref[0, 6, :, lanes]
    pi = coef_ref[0, 7, :, lanes]
    return xr + (pr * hr - pi * hi), xi + (pr * hi + pi * hr)


def _s5_fwd(proj, bmat, cmat, coef, dskip, exchanges=()):
    L = proj.shape[0]
    tm = _tile(L, TM_S5)
    nrt = tm // SUBLANES
    lc = min(LANE_CHUNK, WB)

    def body(u_ref, bm_ref, cm_ref, coef_ref, d_ref, sre_ref, sim_ref, ys_ref, hr_ref, hi_ref):
        @pl.when(pl.program_id(1) == 0)
        def _():
            hr_ref[...] = jnp.zeros_like(hr_ref)
            hi_ref[...] = jnp.zeros_like(hi_ref)

        u = u_ref[...]
        bu = _dot(u.astype(MXU_DTYPE), bm_ref[0])
        sre_ref[...] = bu[:, :WB]
        sim_ref[...] = bu[:, WB:]
        for c in range(WB // lc):
            lanes = slice(c * lc, (c + 1) * lc)

            def row_body(r, carry, lanes=lanes):
                hr, hi = carry
                rows = pl.ds(pl.multiple_of(r * SUBLANES, SUBLANES), SUBLANES)
                xr, xi = _scan_tile(sre_ref[rows, lanes], sim_ref[rows, lanes], hr, hi, coef_ref, lanes, False)
                sre_ref[rows, lanes] = xr
                sim_ref[rows, lanes] = xi
                last = SUBLANES - 1
                return (jnp.broadcast_to(xr[last:last + 1, :], xr.shape), jnp.broadcast_to(xi[last:last + 1, :], xi.shape))

            hr, hi = lax.fori_loop(0, nrt, row_body, (hr_ref[:, lanes], hi_ref[:, lanes]))
            hr_ref[:, lanes] = hr
            hi_ref[:, lanes] = hi
        ys = _dot(sre_ref[...].astype(MXU_DTYPE), cm_ref[0, :WB, :]) + _dot(sim_ref[...].astype(MXU_DTYPE), cm_ref[0, WB:, :])
        ys_ref[...] = ys + d_ref[0] * u

    return _call(
        body, name="s5_fwd", grid=(N_GBLK, L // tm), exchanges=exchanges, semantics=("arbitrary", "arbitrary"),
        operands=(proj, bmat, cmat, coef, dskip),
        in_specs=[
            pl.BlockSpec((tm, UB), lambda b, i: (i, b)),
            pl.BlockSpec((1, UB, 2 * WB), lambda b, i: (b, 0, 0)),
            pl.BlockSpec((1, 2 * WB, UB), lambda b, i: (b, 0, 0)),
            pl.BlockSpec((1, 8, SUBLANES, WB), lambda b, i: (b, 0, 0, 0)),
            pl.BlockSpec((1, 1, UB), lambda b, i: (b, 0, 0)),
        ],
        out_specs=[
            pl.BlockSpec((tm, WB), lambda b, i: (i, b)),
            pl.BlockSpec((tm, WB), lambda b, i: (i, b)),
            pl.BlockSpec((tm, UB), lambda b, i: (i, b)),
        ],
        out_shape=[
            jax.ShapeDtypeStruct((L, N_GBLK * WB), F32),
            jax.ShapeDtypeStruct((L, N_GBLK * WB), F32),
            jax.ShapeDtypeStruct((L, D_SSM), F32),
        ],
        scratch_shapes=[pltpu.VMEM((SUBLANES, WB), F32), pltpu.VMEM((SUBLANES, WB), F32)],
    )


def _tail_fwd(x, ys, proj, w_glu, conv_w, g_ssm, g_conv, w_out, g_post, exchanges=()):
    L, D = x.shape
    tm = _tile(L, TM_TAIL)

    def body(x_ref, ys_ref, h_ref, bg_ref, cg_ref, wglu_ref, cw_ref, gs_ref, gc_ref, wout_ref, gp_ref,
             ycat_ref, o_ref, x1_ref, zbuf):
        @pl.when(pl.program_id(0) == 0)
        def _():
            zbuf[0:SUBLANES, :] = jnp.zeros((SUBLANES, D_CONV), F32)

        y1, _ = _gelu(ys_ref[...])
        y2 = y1 * jax.nn.sigmoid(_dot(y1.astype(MXU_DTYPE), wglu_ref[...]))
        ycat_ref[:, :D_SSM] = _rms_fwd(y2, gs_ref[...]).astype(MXU_DTYPE)
        z = cg_ref[...] * h_ref[...]
        zbuf[SUBLANES:, :] = z
        conv = cw_ref[0:1, :] * zbuf[SUBLANES - 2:SUBLANES - 2 + tm, :] + cw_ref[1:2, :] * zbuf[SUBLANES - 1:SUBLANES - 1 + tm, :] + cw_ref[2:3, :] * z
        zbuf[0:SUBLANES, :] = zbuf[tm:tm + SUBLANES, :]
        ycat_ref[:, D_SSM:] = _rms_fwd(bg_ref[...] * conv, gc_ref[...]).astype(MXU_DTYPE)
        o = _dot(ycat_ref[...], wout_ref[...])
        o_ref[...] = o
        x1_ref[...] = x_ref[...] + _rms_fwd(o, gp_ref[...])

    row = lambda i: (i, 0)
    return _call(
        body, name="tail_fwd", grid=(L // tm,), exchanges=exchanges, semantics=("arbitrary",),
        operands=(x, ys, proj, proj, proj, w_glu, conv_w, g_ssm, g_conv, w_out, g_post),
        in_specs=[
            pl.BlockSpec((tm, D), row), pl.BlockSpec((tm, D_SSM), row),
            pl.BlockSpec((tm, D_CONV), lambda i: (i, 1)), pl.BlockSpec((tm, D_CONV), lambda i: (i, 2)),
            pl.BlockSpec((tm, D_CONV), lambda i: (i, 3)),
            _resident(w_glu.shape), _resident(conv_w.shape), _resident(g_ssm.shape), _resident(g_conv.shape),
            _resident(w_out.shape), _resident(g_post.shape),
        ],
        out_specs=[pl.BlockSpec((tm, D), row), pl.BlockSpec((tm, D), row), pl.BlockSpec((tm, D), row)],
        out_shape=[jax.ShapeDtypeStruct((L, D), MXU_DTYPE), jax.ShapeDtypeStruct((L, D), F32), jax.ShapeDtypeStruct((L, D), F32)],
        scratch_shapes=[pltpu.VMEM((tm + SUBLANES, D_CONV), F32)],
    )


def _mlp_fwd(x1, target, w_up_all, w_down, g_pre, g_post):
    L, D = x1.shape
    ns, _, fc = w_up_all.shape
    tm = _tile(L, TM_MLP)

    def body(x1_ref, t_ref, wup_ref, wdn_ref, gpre_ref, gpost_ref, hn2_ref, up_ref, m_ref, dx2_ref, loss_ref):
        @pl.when(pl.program_id(0) == 0)
        def _():
            loss_ref[...] = jnp.zeros_like(loss_ref)

        x1v = x1_ref[...]
        hn2 = _rms_fwd(x1v, gpre_ref[...]).astype(MXU_DTYPE)
        hn2_ref[...] = hn2
        m = jnp.zeros((tm, D), F32)
        for j in range(ns):
            up = _dot(hn2, wup_ref[j])
            up_ref[:, j * fc:(j + 1) * fc] = up
            act = jnp.square(jnp.maximum(up, 0.0)).astype(MXU_DTYPE)
            m = m + _dot(act, wdn_ref[j * fc:(j + 1) * fc, :])
        m_ref[...] = m
        err = x1v + _rms_fwd(m, gpost_ref[...]) - t_ref[...]
        loss_ref[...] += 0.5 * jnp.sum(jnp.mean(err * err, axis=-1, keepdims=True))
        dx2_ref[...] = err * (1.0 / D)

    row = lambda i: (i, 0)
    return pl.pallas_call(
        body, name="mlp_fwd", grid=(L // tm,),
        in_specs=[pl.BlockSpec((tm, D), row), pl.BlockSpec((tm, D), row), _resident(w_up_all.shape), _resident(w_down.shape),
                  _resident(g_pre.shape), _resident(g_post.shape)],
        out_specs=[pl.BlockSpec((tm, D), row), pl.BlockSpec((tm, ns * fc), row), pl.BlockSpec((tm, D), row),
                   pl.BlockSpec((tm, D), row), pl.BlockSpec((SUBLANES, 128), lambda i: (0, 0))],
        out_shape=[jax.ShapeDtypeStruct((L, D), MXU_DTYPE), jax.ShapeDtypeStruct((L, ns * fc), F32), jax.ShapeDtypeStruct((L, D), F32),
                   jax.ShapeDtypeStruct((L, D), F32), jax.ShapeDtypeStruct((SUBLANES, 128), F32)],
        compiler_params=_params(("arbitrary",)),
    )(x1, target, w_up_all, w_down, g_pre, g_post)


def _mlp_bwd(dx2, m, up, x1, w_up_all, w_down, g_pre, g_post):
    L, D = x1.shape
    ns, _, fc = w_up_all.shape
    tm = _tile(L, TM_MLP)

    def body(dx2_ref, m_ref, up_ref, x1_ref, wup_ref, wdn_ref, gpre_ref, gpost_ref,
             dm_ref, dup_ref, act_ref, dx1_ref, dgpost_ref, dgpre_ref):
        @pl.when(pl.program_id(0) == 0)
        def _():
            dgpost_ref[...] = jnp.zeros_like(dgpost_ref)
            dgpre_ref[...] = jnp.zeros_like(dgpre_ref)

        dx2v = dx2_ref[...]
        dm, dg = _rms_bwd(m_ref[...], gpost_ref[...], dx2v)
        dgpost_ref[...] += dg
        dm_b = dm.astype(MXU_DTYPE)
        dm_ref[...] = dm_b
        dhn2 = jnp.zeros((tm, D), F32)
        for j in range(ns):
            cols = slice(j * fc, (j + 1) * fc)
            relu = jnp.maximum(up_ref[:, cols], 0.0)
            act_ref[:, cols] = jnp.square(relu).astype(MXU_DTYPE)
            dup = (_dot_nt(dm_b, wdn_ref[cols, :]) * (2.0 * relu)).astype(MXU_DTYPE)
            dup_ref[:, cols] = dup
            dhn2 = dhn2 + _dot_nt(dup, wup_ref[j])
        dx, dg = _rms_bwd(x1_ref[...], gpre_ref[...], dhn2)
        dgpre_ref[...] += dg
        dx1_ref[...] = dx2v + dx

    row = lambda i: (i, 0)
    vec = pl.BlockSpec((1, D), lambda i: (0, 0))
    return pl.pallas_call(
        body, name="mlp_bwd", grid=(L // tm,),
        in_specs=[pl.BlockSpec((tm, D), row), pl.BlockSpec((tm, D), row), pl.BlockSpec((tm, ns * fc), row), pl.BlockSpec((tm, D), row),
                  _resident(w_up_all.shape), _resident(w_down.shape), _resident(g_pre.shape), _resident(g_post.shape)],
        out_specs=[pl.BlockSpec((tm, D), row), pl.BlockSpec((tm, ns * fc), row), pl.BlockSpec((tm, ns * fc), row),
                   pl.BlockSpec((tm, D), row), vec, vec],
        out_shape=[jax.ShapeDtypeStruct((L, D), MXU_DTYPE), jax.ShapeDtypeStruct((L, ns * fc), MXU_DTYPE),
                   jax.ShapeDtypeStruct((L, ns * fc), MXU_DTYPE), jax.ShapeDtypeStruct((L, D), F32),
                   jax.ShapeDtypeStruct((1, D), F32), jax.ShapeDtypeStruct((1, D), F32)],
        compiler_params=_params(("arbitrary",)),
    )(dx2, m, up, x1, w_up_all, w_down, g_pre, g_post)


def _tail_bwd(dx1, o, ys, proj, w_glu, conv_w, g_ssm, g_conv, w_out, g_post, exchanges=()):
    L, D = dx1.shape
    tm = _tile(L, TM_TAIL)
    nt = L // tm
    hb = tm // SUBLANES

    def body(dx1_ref, o_ref, ys_ref, h_ref, bg_ref, cg_ref, hh_ref, hcg_ref, wglu_ref, cw_ref, gs_ref, gc_ref, wout_ref, gp_ref,
             do_ref, da_ref, y1_ref, dys_ref, dhbc_ref, dgp_ref, dgs_ref, dgc_ref, dcw_ref, zbuf, dcbuf):
        step = pl.program_id(0)

        @pl.when(step == 0)
        def _():
            dcbuf[tm:, :] = jnp.zeros((SUBLANES, D_CONV), F32)
            dgp_ref[...] = jnp.zeros_like(dgp_ref)
            dgs_ref[...] = jnp.zeros_like(dgs_ref)
            dgc_ref[...] = jnp.zeros_like(dgc_ref)
            dcw_ref[...] = jnp.zeros_like(dcw_ref)

        do, dg = _rms_bwd(o_ref[...], gp_ref[...], dx1_ref[...])
        dgp_ref[...] += dg
        do_b = do.astype(MXU_DTYPE)
        do_ref[...] = do_b
        dycat = _dot_nt(do_b, wout_ref[...])
        y1, dgelu = _gelu(ys_ref[...])
        y1_b = y1.astype(MXU_DTYPE)
        y1_ref[...] = y1_b
        s = jax.nn.sigmoid(_dot(y1_b, wglu_ref[...]))
        dy2, dg = _rms_bwd(y1 * s, gs_ref[...], dycat[:, :D_SSM])
        dgs_ref[...] += dg
        da_b = (dy2 * y1 * s * (1.0 - s)).astype(MXU_DTYPE)
        da_ref[...] = da_b
        dys_ref[...] = (dy2 * s + _dot_nt(da_b, wglu_ref[...])) * dgelu
        h = h_ref[...]
        cg = cg_ref[...]
        bg = bg_ref[...]
        z = cg * h
        first = step == nt - 1
        zbuf[0:SUBLANES, :] = jnp.where(first, 0.0, hcg_ref[...] * hh_ref[...])
        zbuf[SUBLANES:, :] = z
        z1 = zbuf[SUBLANES - 1:SUBLANES - 1 + tm, :]
        z2 = zbuf[SUBLANES - 2:SUBLANES - 2 + tm, :]
        conv = cw_ref[0:1, :] * z2 + cw_ref[1:2, :] * z1 + cw_ref[2:3, :] * z
        dyc, dg = _rms_bwd(bg * conv, gc_ref[...], dycat[:, D_SSM:])
        dgc_ref[...] += dg
        dconv = dyc * bg
        dcw_ref[0:1, :] += jnp.sum(dconv * z2, axis=0, keepdims=True)
        dcw_ref[1:2, :] += jnp.sum(dconv * z1, axis=0, keepdims=True)
        dcw_ref[2:3, :] += jnp.sum(dconv * z, axis=0, keepdims=True)
        dcbuf[0:tm, :] = dconv
        dz = cw_ref[2:3, :] * dconv + cw_ref[1:2, :] * dcbuf[1:1 + tm, :] + cw_ref[0:1, :] * dcbuf[2:2 + tm, :]
        dcbuf[tm:, :] = dcbuf[0:SUBLANES, :]
        dhbc_ref[:, 0:D_CONV] = (dz * cg).astype(MXU_DTYPE)
        dhbc_ref[:, D_CONV:2 * D_CONV] = (dyc * conv).astype(MXU_DTYPE)
        dhbc_ref[:, 2 * D_CONV:] = (dz * h).astype(MXU_DTYPE)

    rev = lambda i: (nt - 1 - i, 0)
    col = lambda c: (lambda i: (nt - 1 - i, c))
    halo = lambda c: (lambda i: (jnp.maximum((nt - 1 - i) * hb - 1, 0), c))
    vec = lambda n: pl.BlockSpec((1, n), lambda i: (0, 0))
    return _call(
        body, name="tail_bwd", grid=(nt,), exchanges=exchanges, semantics=("arbitrary",),
        operands=(dx1, o, ys, proj, proj, proj, proj, proj, w_glu, conv_w, g_ssm, g_conv, w_out, g_post),
        in_specs=[
            pl.BlockSpec((tm, D), rev), pl.BlockSpec((tm, D), rev), pl.BlockSpec((tm, D_SSM), rev),
            pl.BlockSpec((tm, D_CONV), col(1)), pl.BlockSpec((tm, D_CONV), col(2)), pl.BlockSpec((tm, D_CONV), col(3)),
            pl.BlockSpec((SUBLANES, D_CONV), halo(1)), pl.BlockSpec((SUBLANES, D_CONV), halo(3)),
            _resident(w_glu.shape), _resident(conv_w.shape), _resident(g_ssm.shape), _resident(g_conv.shape),
            _resident(w_out.shape), _resident(g_post.shape),
        ],
        out_specs=[
            pl.BlockSpec((tm, D), rev), pl.BlockSpec((tm, D_SSM), rev), pl.BlockSpec((tm, D_SSM), rev), pl.BlockSpec((tm, D_SSM), rev),
            pl.BlockSpec((tm, 3 * D_CONV), rev), vec(D), vec(D_SSM), vec(D_CONV),
            pl.BlockSpec((SUBLANES, D_CONV), lambda i: (0, 0)),
        ],
        out_shape=[
            jax.ShapeDtypeStruct((L, D), MXU_DTYPE), jax.ShapeDtypeStruct((L, D_SSM), MXU_DTYPE), jax.ShapeDtypeStruct((L, D_SSM), MXU_DTYPE),
            jax.ShapeDtypeStruct((L, D_SSM), F32), jax.ShapeDtypeStruct((L, 3 * D_CONV), MXU_DTYPE),
            jax.ShapeDtypeStruct((1, D), F32), jax.ShapeDtypeStruct((1, D_SSM), F32), jax.ShapeDtypeStruct((1, D_CONV), F32),
            jax.ShapeDtypeStruct((SUBLANES, D_CONV), F32),
        ],
        scratch_shapes=[pltpu.VMEM((tm + SUBLANES, D_CONV), F32), pltpu.VMEM((tm + SUBLANES, D_CONV), F32)],
    )


def _s5_bwd(dys, proj, s_re, s_im, bmat, cmat, coef_rev, dskip, exchanges=()):
    L = dys.shape[0]
    tm = _tile(L, TM_S5)
    nt = L // tm
    nrt = tm // SUBLANES
    lc = min(LANE_CHUNK, WB)

    def body(dys_ref, u_ref, sre_ref, sim_ref, bm_ref, cm_ref, coef_ref, d_ref,
             du_ref, gb_ref, gc_ref, q_ref, gd_ref, lr_ref, li_ref, hr_ref, hi_ref, qr_acc, qi_acc):
        step = pl.program_id(1)

        @pl.when(step == 0)
        def _():
            for ref in (hr_ref, hi_ref, qr_acc, qi_acc, gb_ref, gc_ref, gd_ref):
                ref[...] = jnp.zeros_like(ref)

        dys_v = dys_ref[...]
        u = u_ref[...]
        dys_b = dys_v.astype(MXU_DTYPE)
        u_b = u.astype(MXU_DTYPE)
        d = _dot_nt(dys_b, cm_ref[0])
        lr_ref[...] = d[:, :WB]
        li_ref[...] = d[:, WB:]
        for c in range(WB // lc):
            lanes = slice(c * lc, (c + 1) * lc)

            def row_body(k, carry, lanes=lanes):
                hr, hi = carry
                rows = pl.ds(pl.multiple_of((nrt - 1 - k) * SUBLANES, SUBLANES), SUBLANES)
                dr = lr_ref[rows, lanes]
                di = li_ref[rows, lanes]
                xr, xi = _scan_tile(dr, di, hr, hi, coef_ref, lanes, True)
                lr_ref[rows, lanes] = xr
                li_ref[rows, lanes] = xi
                er = xr - dr
                ei = xi - di
                sr = sre_ref[rows, lanes]
                si = sim_ref[rows, lanes]
                qr_acc[:, lanes] += er * sr + ei * si
                qi_acc[:, lanes] += ei * sr - er * si
                return (jnp.broadcast_to(xr[0:1, :], xr.shape), jnp.broadcast_to(xi[0:1, :], xi.shape))

            hr, hi = lax.fori_loop(0, nrt, row_body, (hr_ref[:, lanes], hi_ref[:, lanes]))
            hr_ref[:, lanes] = hr
            hi_ref[:, lanes] = hi
        lr_b = lr_ref[...].astype(MXU_DTYPE)
        li_b = li_ref[...].astype(MXU_DTYPE)
        du_ref[...] = _dot_nt(lr_b, bm_ref[0, :, :WB]) + _dot_nt(li_b, bm_ref[0, :, WB:]) + d_ref[0] * dys_v
        gb_ref[0, :WB, :] += _dot_tn(lr_b, u_b)
        gb_ref[0, WB:, :] += _dot_tn(li_b, u_b)
        gc_ref[0, :, :WB] += _dot_tn(dys_b, sre_ref[...].astype(MXU_DTYPE))
        gc_ref[0, :, WB:] += _dot_tn(dys_b, sim_ref[...].astype(MXU_DTYPE))
        gd_ref[0] += jnp.sum(dys_v * u, axis=0, keepdims=True)

        @pl.when(step == nt - 1)
        def _():
            q_ref[0, 0:1, :] = jnp.sum(qr_acc[...], axis=0, keepdims=True)
            q_ref[0, 1:2, :] = jnp.sum(qi_acc[...], axis=0, keepdims=True)

    rev = lambda b, i: (nt - 1 - i, b)
    blk = lambda b, i: (b, 0, 0)
    return _call(
        body, name="s5_bwd", grid=(N_GBLK, nt), exchanges=exchanges, semantics=("arbitrary", "arbitrary"),
        operands=(dys, proj, s_re, s_im, bmat, cmat, coef_rev, dskip),
        in_specs=[
            pl.BlockSpec((tm, UB), rev), pl.BlockSpec((tm, UB), rev), pl.BlockSpec((tm, WB), rev), pl.BlockSpec((tm, WB), rev),
            pl.BlockSpec((1, UB, 2 * WB), blk), pl.BlockSpec((1, 2 * WB, UB), blk),
            pl.BlockSpec((1, 8, SUBLANES, WB), lambda b, i: (b, 0, 0, 0)), pl.BlockSpec((1, 1, UB), blk),
        ],
        out_specs=[
            pl.BlockSpec((tm, UB), rev), pl.BlockSpec((1, 2 * WB, UB), blk), pl.BlockSpec((1, UB, 2 * WB), blk),
            pl.BlockSpec((1, 2, WB), blk), pl.BlockSpec((1, 1, UB), blk),
        ],
        out_shape=[
            jax.ShapeDtypeStruct((L, D_SSM), F32), jax.ShapeDtypeStruct((N_GBLK, 2 * WB, UB), F32),
            jax.ShapeDtypeStruct((N_GBLK, UB, 2 * WB), F32), jax.ShapeDtypeStruct((N_GBLK, 2, WB), F32),
            jax.ShapeDtypeStruct((N_GBLK, 1, UB), F32),
        ],
        scratch_shapes=[pltpu.VMEM((tm, WB), F32), pltpu.VMEM((tm, WB), F32), pltpu.VMEM((SUBLANES, WB), F32),
                        pltpu.VMEM((SUBLANES, WB), F32), pltpu.VMEM((SUBLANES, WB), F32), pltpu.VMEM((SUBLANES, WB), F32)],
    )


def _inproj_bwd(du, dhbc, x, dx1, w_in_all, g1):
    L, D = x.shape
    ns, _, nc = w_in_all.shape
    tm = _tile(L, TM_PROJ)

    def body(du_ref, dhbc_ref, x_ref, dx1_ref, w_ref, g_ref, gx_ref, dproj_ref, dg_ref):
        @pl.when(pl.program_id(0) == 0)
        def _():
            dg_ref[...] = jnp.zeros_like(dg_ref)

        du_b = du_ref[...].astype(MXU_DTYPE)
        dproj_ref[:, :nc] = du_b
        dproj_ref[:, nc:] = dhbc_ref[...]
        dhn = _dot_nt(du_b, w_ref[0])
        for j in range(1, ns):
            dhn = dhn + _dot_nt(dhbc_ref[:, (j - 1) * nc:j * nc], w_ref[j])
        dx, dg = _rms_bwd(x_ref[...], g_ref[...], dhn)
        dg_ref[...] += dg
        gx_ref[...] = dx1_ref[...] + dx

    row = lambda i: (i, 0)
    return pl.pallas_call(
        body, name="inproj_bwd", grid=(L // tm,),
        in_specs=[pl.BlockSpec((tm, nc), row), pl.BlockSpec((tm, (ns - 1) * nc), row), pl.BlockSpec((tm, D), row), pl.BlockSpec((tm, D), row),
                  _resident(w_in_all.shape), _resident(g1.shape)],
        out_specs=[pl.BlockSpec((tm, D), row), pl.BlockSpec((tm, ns * nc), row), pl.BlockSpec((1, D), lambda i: (0, 0))],
        out_shape=[jax.ShapeDtypeStruct((L, D), F32), jax.ShapeDtypeStruct((L, ns * nc), MXU_DTYPE), jax.ShapeDtypeStruct((1, D), F32)],
        compiler_params=_params(("arbitrary",)),
    )(du, dhbc, x, dx1, w_in_all, g1)


def _matmul_tn(a, b, name, col_shards=1, exchanges=()):
    L, K = a.shape
    N = b.shape[1]
    tl = _tile(L, TL_TN)
    tk = _tile(K, 1024)
    nw = N // col_shards
    tn = _tile(nw, 1024)
    npb = nw // tn

    def body(a_ref, b_ref, o_ref):
        @pl.when(pl.program_id(2) == 0)
        def _():
            o_ref[...] = jnp.zeros_like(o_ref)

        o_ref[0] += _dot_tn(a_ref[...], b_ref[...])

    return _call(
        body, name=name, grid=(K // tk, N // tn, L // tl), exchanges=exchanges, semantics=("arbitrary", "arbitrary", "arbitrary"),
        operands=(a, b),
        in_specs=[pl.BlockSpec((tl, tk), lambda k, n, l: (l, k)), pl.BlockSpec((tl, tn), lambda k, n, l: (l, n))],
        out_specs=[pl.BlockSpec((1, tk, tn), lambda k, n, l: (n // npb, k, n % npb))],
        out_shape=[jax.ShapeDtypeStruct((col_shards, K, nw), F32)],
    )


def _ssm_discretize(lam_re, lam_im, log_dt, b_re, b_im):
    dt = jnp.exp(log_dt)[:, None]
    zr = lam_re * dt
    zi = lam_im * dt
    mag = jnp.exp(zr)
    abr = mag * jnp.cos(zi)
    abi = mag * jnp.sin(zi)
    nr, ni = abr - 1.0, abi
    den = lam_re * lam_re + lam_im * lam_im
    coef_r = (nr * lam_re + ni * lam_im) / den
    coef_i = (ni * lam_re - nr * lam_im) / den
    bbar_r = coef_r[..., None] * b_re - coef_i[..., None] * b_im
    bbar_i = coef_r[..., None] * b_im + coef_i[..., None] * b_re
    return zr, zi, bbar_r, bbar_i


def _block_diag(t):
    nb, g, r, c = t.shape
    eye = jnp.eye(g, dtype=t.dtype)
    return (t[:, :, :, None, :] * eye[None, :, None, :, None]).reshape(nb, g * r, g * c)


def _diag_blocks(m, r, c):
    nb = m.shape[0]
    g = m.shape[1] // r
    eye = jnp.eye(g, dtype=m.dtype)
    t = (m.reshape(nb, g, r, g, c) * eye[None, :, None, :, None]).sum(axis=3)
    return t.reshape(nb * g, r, c)


def _scan_tables(ar, ai, reverse):
    def mul(p, q):
        return (p[0] * q[0] - p[1] * q[1], p[0] * q[1] + p[1] * q[0])

    a1 = (ar, -ai if reverse else ai)
    a2 = mul(a1, a1)
    a3 = mul(a2, a1)
    a4 = mul(a2, a2)
    pw = [a1, a2, a3, a4, mul(a4, a1), mul(a4, a2), mul(a4, a3), mul(a4, a4)]
    t = jnp.arange(SUBLANES)[:, None]
    tabs = []
    for k, ak in ((1, a1), (2, a2), (4, a4)):
        mask = (t + k <= SUBLANES - 1) if reverse else (t >= k)
        tabs += [jnp.where(mask, ak[0][None, :], 0.0), jnp.where(mask, ak[1][None, :], 0.0)]
    order = range(SUBLANES - 1, -1, -1) if reverse else range(SUBLANES)
    tabs += [jnp.stack([pw[j][0] for j in order]), jnp.stack([pw[j][1] for j in order])]
    coef = jnp.stack(tabs)
    return coef.reshape(8, SUBLANES, N_GBLK, WB).transpose(2, 0, 1, 3)


def _ssm_matrices(lam_re, lam_im, log_dt, b_re, b_im, c_re, c_im):
    zr, zi, bbar_r, bbar_i = _ssm_discretize(lam_re, lam_im, log_dt, b_re, b_im)
    mag = jnp.exp(zr)
    ar = (mag * jnp.cos(zi)).reshape(-1)
    ai = (mag * jnp.sin(zi)).reshape(-1)
    blk = lambda t: t.reshape(N_GBLK, G_PER_BLK, *t.shape[1:])
    bmat = jnp.concatenate([_block_diag(blk(bbar_r).transpose(0, 1, 3, 2)), _block_diag(blk(bbar_i).transpose(0, 1, 3, 2))], axis=2)
    cmat = jnp.concatenate([_block_diag(blk(c_re).transpose(0, 1, 3, 2)), _block_diag(blk(-c_im).transpose(0, 1, 3, 2))], axis=1)
    return bmat.astype(MXU_DTYPE), cmat.astype(MXU_DTYPE), _scan_tables(ar, ai, False), _scan_tables(ar, ai, True)


def _ssm_param_grads(lam_re, lam_im, log_dt, b_re, b_im, gb, gc, q, gd):
    gbr = _diag_blocks(gb[:, :WB, :], STATE, GROUP)
    gbi = _diag_blocks(gb[:, WB:, :], STATE, GROUP)
    d_c_re = _diag_blocks(gc[:, :, :WB], GROUP, STATE)
    d_c_im = -_diag_blocks(gc[:, :, WB:], GROUP, STATE)
    qr = q[:, 0, :].reshape(N_GROUPS, STATE)
    qi = q[:, 1, :].reshape(N_GROUPS, STATE)
    _, vjp = jax.vjp(_ssm_discretize, lam_re, lam_im, log_dt, b_re, b_im)
    d_lam_re, d_lam_im, d_log_dt, d_b_re, d_b_im = vjp((qr, qi, gbr, gbi))
    return d_lam_re, d_lam_im, d_log_dt, d_b_re, d_b_im, d_c_re, d_c_im, gd.reshape(N_GROUPS, GROUP)


def _row_tile(rows, n):
    return _tile(rows, max(SUBLANES, (2 * 1024 * 1024) // (4 * n)))


def _pair_add(grad, other, core, name):
    ns, h, n = other.shape
    tr = _row_tile(h, n)
    nb = h // tr

    def body(c_ref, g_ref, o_ref, out_ref):
        out_ref[...] = (g_ref[...] + o_ref[...]).astype(WIRE_DTYPE)

    return pl.pallas_call(
        body, name=name,
        grid_spec=pltpu.PrefetchScalarGridSpec(
            num_scalar_prefetch=1, grid=(ns, nb),
            in_specs=[pl.BlockSpec((1, tr, n), lambda s, i, c: (s, c[0] * nb + i, 0)), pl.BlockSpec((1, tr, n), lambda s, i, c: (s, i, 0))],
            out_specs=pl.BlockSpec((1, tr, n), lambda s, i, c: (s, i, 0))),
        out_shape=jax.ShapeDtypeStruct(other.shape, WIRE_DTYPE),
        compiler_params=_params(("arbitrary", "arbitrary")),
    )(core, grad, other)


def _quad_sum(parts, core, name):
    ns, h, n = parts.shape
    tr = _row_tile(h, n)
    nb = h // tr

    def body(c_ref, p_ref, out_ref):
        p = [p_ref[k].astype(F32) for k in range(ns)]
        out_ref[...] = ((p[0] + p[1]) + p[2]) + p[3]

    return pl.pallas_call(
        body, name=name,
        grid_spec=pltpu.PrefetchScalarGridSpec(
            num_scalar_prefetch=1, grid=(nb,),
            in_specs=[pl.BlockSpec((ns, tr, n), lambda i, c: (0, i, 0))],
            out_specs=pl.BlockSpec((tr, n), lambda i, c: (c[0] * nb + i, 0))),
        out_shape=jax.ShapeDtypeStruct((2 * h, n), F32),
        compiler_params=_params(("arbitrary",)),
    )(core, parts)


def _sum_devices(blocks):
    nd, m, n = blocks.shape

    def body(b_ref, out_ref):
        total = b_ref[0]
        for d in range(1, nd):
            total = total + b_ref[d]
        out_ref[...] = total

    return pl.pallas_call(body, name="sum_devices", in_specs=[VMEM], out_specs=VMEM,
                          out_shape=jax.ShapeDtypeStruct((m, n), blocks.dtype))(blocks)


def _adamw_math(w, g, m, v):
    m = ADAM_B1 * m + (1.0 - ADAM_B1) * g
    v = ADAM_B2 * v + (1.0 - ADAM_B2) * jnp.square(g)
    m_hat = m / (1.0 - ADAM_B1 ** ADAM_STEP)
    v_hat = v / (1.0 - ADAM_B2 ** ADAM_STEP)
    delta = -ADAM_LR * (m_hat / (jnp.sqrt(v_hat) + ADAM_EPS) + ADAM_WD * w)
    return delta, m, v


def _adamw(w, g, m, v, name):
    r, n = w.shape
    tr = _row_tile(r, n)

    def body(w_ref, g_ref, m_ref, v_ref, d_ref, nm_ref, nv_ref):
        d_ref[...], nm_ref[...], nv_ref[...] = _adamw_math(w_ref[...], g_ref[...], m_ref[...], v_ref[...])

    spec = pl.BlockSpec((tr, n), lambda i: (i, 0))
    return pl.pallas_call(
        body, name=name, grid=(r // tr,), in_specs=[spec] * 4, out_specs=[spec] * 3,
        out_shape=[jax.ShapeDtypeStruct((r, n), F32)] * 3,
        compiler_params=_params(("arbitrary",)),
    )(w, g, m, v)


SMALL_SHAPES = {
    "g_pre_mix": (D_MODEL,), "lam_re": (N_GROUPS, STATE), "lam_im": (N_GROUPS, STATE), "log_dt": (N_GROUPS,),
    "b_re": (N_GROUPS, STATE, GROUP), "b_im": (N_GROUPS, STATE, GROUP), "c_re": (N_GROUPS, GROUP, STATE), "c_im": (N_GROUPS, GROUP, STATE),
    "d_skip": (N_GROUPS, GROUP), "conv_w": (3, D_CONV), "g_ssm_out": (D_SSM,), "g_conv_out": (D_CONV,),
    "g_post_mix": (D_MODEL,), "g_pre_mlp": (D_MODEL,), "g_post_mlp": (D_MODEL,),
}
LANES = 128
PACK_TILE = SUBLANES * LANES


def _pack_rows(name):
    n = math.prod(SMALL_SHAPES[name])
    return SUBLANES * (-(-n // PACK_TILE))


def _pack_offsets():
    offs, row = {}, 0
    for name in SMALL_SHAPES:
        offs[name] = row
        row += _pack_rows(name)
    return offs, row


def _pack_small(grads):
    parts = []
    for name in SMALL_SHAPES:
        flat = grads[name].reshape(-1)
        parts.append(jnp.pad(flat, (0, _pack_rows(name) * LANES - flat.shape[0])).reshape(-1, LANES))
    return jnp.concatenate(parts, axis=0)


def _lane_shape(name):
    n = math.prod(SMALL_SHAPES[name])
    return (n // LANES, LANES) if n % LANES == 0 else (1, n)


def _adamw_small(pack, conv_grad, w, m, v):
    names = list(SMALL_SHAPES)
    offs, _ = _pack_offsets()
    nn = len(names)

    def body(*refs):
        pack_ref, cg_ref = refs[0], refs[1]
        w_refs, m_refs, v_refs = refs[2:2 + nn], refs[2 + nn:2 + 2 * nn], refs[2 + 2 * nn:2 + 3 * nn]
        outs = refs[2 + 3 * nn:]
        for j, name in enumerate(names):
            r, n = w_refs[j].shape
            g = cg_ref[...] if name == "conv_w" else pack_ref[offs[name]:offs[name] + r, 0:n]
            delta, nm, nv = _adamw_math(w_refs[j][...], g, m_refs[j][...], v_refs[j][...])
            outs[j][...] = g
            outs[nn + j][...] = delta
            outs[2 * nn + j][...] = nm
            outs[3 * nn + j][...] = nv

    args = [pack, conv_grad] + [w[k] for k in names] + [m[k] for k in names] + [v[k] for k in names]
    res = pl.pallas_call(
        body, name="adamw_small", in_specs=[VMEM] * len(args), out_specs=[VMEM] * (4 * nn),
        out_shape=[jax.ShapeDtypeStruct(w[k].shape, F32) for k in names] * 4,
    )(*args)
    return [dict(zip(names, res[q * nn:(q + 1) * nn])) for q in range(4)]


WEIGHTS = ["g_pre_mix", "w_in", "lam_re", "lam_im", "log_dt", "b_re", "b_im", "c_re", "c_im", "d_skip", "w_glu", "conv_w",
           "g_ssm_out", "g_conv_out", "w_out", "g_post_mix", "g_pre_mlp", "w_up", "w_down", "g_post_mlp"]
BIG = ["w_in", "w_glu", "w_out", "w_up", "w_down"]


def kernel(x, g_pre_mix, w_in, lam_re, lam_im, log_dt, b_re, b_im, c_re, c_im, d_skip, w_glu, conv_w, g_ssm_out, g_conv_out, w_out, g_post_mix, g_pre_mlp, w_up, w_down, g_post_mlp, loss_target, m_g_pre_mix, m_w_in, m_lam_re, m_lam_im, m_log_dt, m_b_re, m_b_im, m_c_re, m_c_im, m_d_skip, m_w_glu, m_conv_w, m_g_ssm_out, m_g_conv_out, m_w_out, m_g_post_mix, m_g_pre_mlp, m_w_up, m_w_down, m_g_post_mlp, v_g_pre_mix, v_w_in, v_lam_re, v_lam_im, v_log_dt, v_b_re, v_b_im, v_c_re, v_c_im, v_d_skip, v_w_glu, v_conv_w, v_g_ssm_out, v_g_conv_out, v_w_out, v_g_post_mix, v_g_pre_mlp, v_w_up, v_w_down, v_g_post_mlp):
    w = dict(g_pre_mix=g_pre_mix, w_in=w_in, lam_re=lam_re, lam_im=lam_im, log_dt=log_dt, b_re=b_re, b_im=b_im, c_re=c_re, c_im=c_im,
             d_skip=d_skip, w_glu=w_glu, conv_w=conv_w, g_ssm_out=g_ssm_out, g_conv_out=g_conv_out, w_out=w_out, g_post_mix=g_post_mix,
             g_pre_mlp=g_pre_mlp, w_up=w_up, w_down=w_down, g_post_mlp=g_post_mlp)
    m = dict(g_pre_mix=m_g_pre_mix, w_in=m_w_in, lam_re=m_lam_re, lam_im=m_lam_im, log_dt=m_log_dt, b_re=m_b_re, b_im=m_b_im, c_re=m_c_re,
             c_im=m_c_im, d_skip=m_d_skip, w_glu=m_w_glu, conv_w=m_conv_w, g_ssm_out=m_g_ssm_out, g_conv_out=m_g_conv_out, w_out=m_w_out,
             g_post_mix=m_g_post_mix, g_pre_mlp=m_g_pre_mlp, w_up=m_w_up, w_down=m_w_down, g_post_mlp=m_g_post_mlp)
    v = dict(g_pre_mix=v_g_pre_mix, w_in=v_w_in, lam_re=v_lam_re, lam_im=v_lam_im, log_dt=v_log_dt, b_re=v_b_re, b_im=v_b_im, c_re=v_c_re,
             c_im=v_c_im, d_skip=v_d_skip, w_glu=v_w_glu, conv_w=v_conv_w, g_ssm_out=v_g_ssm_out, g_conv_out=v_g_conv_out, w_out=v_w_out,
             g_post_mix=v_g_post_mix, g_pre_mlp=v_g_pre_mlp, w_up=v_w_up, w_down=v_w_down, g_post_mlp=v_g_post_mlp)
    shapes = {k: a.shape for k, a in w.items()}
    w, m, v = ({k: a[0] for k, a in d.items()} for d in (w, m, v))
    chip = 2 * lax.axis_index("x") + lax.axis_index("y")
    core = lax.axis_index("c").astype(jnp.int32).reshape(1)

    xs, target = x[0], loss_target[0]
    g1 = w["g_pre_mix"][None]
    g_ssm, g_conv = w["g_ssm_out"][None], w["g_conv_out"][None]
    g_post_mix, g_pre_mlp, g_post_mlp = w["g_post_mix"][None], w["g_pre_mlp"][None], w["g_post_mlp"][None]
    bmat, cmat, coef_f, coef_r = _ssm_matrices(w["lam_re"], w["lam_im"], w["log_dt"], w["b_re"], w["b_im"], w["c_re"], w["c_im"])
    dskip = w["d_skip"].reshape(N_GBLK, 1, UB)
    shard = {k: w[k].astype(MXU_DTYPE) for k in BIG}
    conv_pad = jnp.pad(w["conv_w"], ((0, SUBLANES - 3), (0, 0)))

    (w_in_all,) = _run_exchanges([_Gather([shard["w_in"]], [False])], "ag_w_in")
    hn, proj = _inproj_fwd(xs, g1, w_in_all)
    rest = _Gather([shard["w_glu"], shard["w_out"], conv_pad, shard["w_up"], shard["w_down"]], [False, False, False, True, True])
    s_re, s_im, ys, w_glu_all, w_out_all, conv_all, w_up_all, w_down_all = _s5_fwd(proj, bmat, cmat, coef_f, dskip, exchanges=[rest])
    w_glu_f, w_out_f = w_glu_all.reshape(D_SSM, D_SSM), w_out_all.reshape(D_MODEL, D_MODEL)
    conv_f = jnp.transpose(conv_all, (1, 0, 2)).reshape(SUBLANES, D_CONV)
    ycat, o, x1, w_up_all, w_down_all = _tail_fwd(xs, ys, proj, w_glu_f, conv_f, g_ssm, g_conv, w_out_f, g_post_mix,
                                                  exchanges=[_Forward([w_up_all, w_down_all])])
    w_down_f = w_down_all.reshape(D_FF, D_MODEL)
    hn2, up, m_act, dx2, loss = _mlp_fwd(x1, target, w_up_all, w_down_f, g_pre_mlp, g_post_mlp)
    loss = lax.psum(loss[0, 0], ("x", "y", "c"))

    dm, dup, act, dx1, dg_post_mlp, dg_pre_mlp = _mlp_bwd(dx2, m_act, up, x1, w_up_all, w_down_f, g_pre_mlp, g_post_mlp)
    gw_down = _matmul_tn(act, dm, "dw_down")[0].reshape(N_CHIPS, D_FF // N_CHIPS, D_MODEL)
    gw_up = _matmul_tn(hn2, dup, "dw_up", col_shards=N_CHIPS)[0]
    do, da, y1, dys, dhbc, dg_post_mix, dg_ssm, dg_conv, dconv_w, o_down, o_up = _tail_bwd(
        dx1, o, ys, proj, w_glu_f, conv_f, g_ssm, g_conv, w_out_f, g_post_mix, exchanges=[_Pair([gw_down, gw_up])])
    p_down = _pair_add(gw_down, o_down, core, "pair_add_w_down")
    p_up = _pair_add(gw_up, o_up, core, "pair_add_w_up")
    gw_out = _matmul_tn(ycat, do, "dw_out")[0].reshape(N_CHIPS, D_MODEL // N_CHIPS, D_MODEL)
    gw_glu = _matmul_tn(y1, da, "dw_glu")[0].reshape(N_CHIPS, D_SSM // N_CHIPS, D_SSM)
    du, gb, gc, q, gd, q_down, q_up, o_out, o_glu = _s5_bwd(
        dys, proj, s_re, s_im, bmat, cmat, coef_r, dskip, exchanges=[_Chip([p_down, p_up]), _Pair([gw_out, gw_glu])])
    h_down = _quad_sum(q_down, core, "quad_sum_w_down")
    h_up = _quad_sum(q_up, core, "quad_sum_w_up")
    p_out = _pair_add(gw_out, o_out, core, "pair_add_w_out")
    p_glu = _pair_add(gw_glu, o_glu, core, "pair_add_w_glu")
    grad_x, dproj, dg_pre_mix = _inproj_bwd(du, dhbc, xs, dx1, w_in_all, g1)
    d_lam_re, d_lam_im, d_log_dt, d_b_re, d_b_im, d_c_re, d_c_im, d_d_skip = _ssm_param_grads(
        w["lam_re"], w["lam_im"], w["log_dt"], w["b_re"], w["b_im"], gb, gc, q, gd)
    small = {
        "g_pre_mix": dg_pre_mix[0], "lam_re": d_lam_re, "lam_im": d_lam_im, "log_dt": d_log_dt, "b_re": d_b_re, "b_im": d_b_im,
        "c_re": d_c_re, "c_im": d_c_im, "d_skip": d_d_skip, "conv_w": dconv_w[:3], "g_ssm_out": dg_ssm[0], "g_conv_out": dg_conv[0],
        "g_post_mix": dg_post_mix[0], "g_pre_mlp": dg_pre_mlp[0], "g_post_mlp": dg_post_mlp[0],
    }
    gw_in, g_down, g_up, q_out, q_glu, packs = _matmul_tn(
        hn, dproj, "dw_in", col_shards=N_CHIPS,
        exchanges=[_Share([h_down, h_up]), _Chip([p_out, p_glu]), _GatherSmall(_pack_small(small))])
    h_out = _quad_sum(q_out, core, "quad_sum_w_out")
    h_glu = _quad_sum(q_glu, core, "quad_sum_w_glu")
    pack = _sum_devices(packs)
    (o_in,) = _run_exchanges([_Pair([gw_in])], "rs_pair_w_in")
    p_in = _pair_add(gw_in, o_in, core, "pair_add_w_in")
    q_in, g_out, g_glu = _run_exchanges([_Chip([p_in]), _Share([h_out, h_glu])], "rs_chip_w_in")
    h_in = _quad_sum(q_in, core, "quad_sum_w_in")
    (g_in,) = _run_exchanges([_Share([h_in])], "rs_share_w_in")
    shard_grads = {"w_in": g_in, "w_glu": g_glu, "w_out": g_out, "w_up": g_up, "w_down": g_down}
    offs, _ = _pack_offsets()
    conv_grad = pack[offs["conv_w"]:offs["conv_w"] + 3 * D_CONV // LANES].reshape(3, D_CONV)
    conv_grad = lax.dynamic_slice(conv_grad, (0, chip * (D_CONV // N_CHIPS)), (3, D_CONV // N_CHIPS))

    out = {q: {} for q in ("grad", "delta", "new_m", "new_v")}
    for k in BIG:
        out["grad"][k] = shard_grads[k]
        out["delta"][k], out["new_m"][k], out["new_v"][k] = _adamw(w[k], shard_grads[k], m[k], v[k], "adamw_" + k)
    lane = lambda d: {k: (d[k] if k == "conv_w" else d[k].reshape(_lane_shape(k))) for k in SMALL_SHAPES}
    res = _adamw_small(pack, conv_grad, lane(w), lane(m), lane(v))
    for q, d in zip(("grad", "delta", "new_m", "new_v"), res):
        out[q].update(d)
    flat = [loss, grad_x[None]]
    for q in ("grad", "delta", "new_m", "new_v"):
        flat += [out[q][k].reshape(shapes[k]) for k in WEIGHTS]
    return tuple(flat)
```

```python
import functools
import math

import jax
import jax.numpy as jnp
from jax import lax
from jax.experimental import pallas as pl
from jax.experimental.pallas import tpu as pltpu

F32 = jnp.float32
MXU_DTYPE = jnp.bfloat16
WIRE_DTYPE = jnp.bfloat16

D_MODEL = 1024
D_SSM = 512
D_CONV = 512
N_GROUPS = 32
GROUP = 16
STATE = 64
D_FF = 4096
RMS_EPS = 1e-6
N_CHIPS = 4
N_DEV = 8

ADAM_LR = 0.001
ADAM_B1 = 0.9
ADAM_B2 = 0.999
ADAM_EPS = 1e-08
ADAM_WD = 0.01
ADAM_STEP = 10

N_GBLK = 2
G_PER_BLK = N_GROUPS // N_GBLK
UB = G_PER_BLK * GROUP
WB = G_PER_BLK * STATE
LANE_CHUNK = 512
SUBLANES = 8

TM_PROJ = 512
TM_S5 = 512
TM_TAIL = 256
TM_MLP = 256
TL_TN = 2048
VMEM_LIMIT = 56 * 1024 * 1024

MESH = pl.DeviceIdType.MESH


def _params(sem, vmem=VMEM_LIMIT):
    return pltpu.CompilerParams(dimension_semantics=sem, vmem_limit_bytes=vmem)


def _resident(shape):
    nd = len(shape)
    return pl.BlockSpec(shape, lambda *_: (0,) * nd, pipeline_mode=pl.Buffered(1))


def _dot(a, b):
    return jnp.dot(a, b, preferred_element_type=F32)


def _dot_nt(a, b):
    return lax.dot_general(a, b, (((1,), (1,)), ((), ())), preferred_element_type=F32)


def _dot_tn(a, b):
    return lax.dot_general(a, b, (((0,), (0,)), ((), ())), preferred_element_type=F32)


def _rms_fwd(x, g):
    r = lax.rsqrt(jnp.mean(x * x, axis=-1, keepdims=True) + RMS_EPS)
    return x * r * g


def _rms_bwd(x, g, dy):
    r = lax.rsqrt(jnp.mean(x * x, axis=-1, keepdims=True) + RMS_EPS)
    xn = x * r
    q = dy * g
    dx = r * (q - xn * jnp.mean(q * xn, axis=-1, keepdims=True))
    return dx, jnp.sum(dy * xn, axis=0, keepdims=True)


_GELU_C = math.sqrt(2.0 / math.pi)


def _gelu(x):
    t = jnp.tanh(_GELU_C * (x + 0.044715 * (x * x * x)))
    y = x * (0.5 * (1.0 + t))
    dy = 0.5 * (1.0 + t) + 0.5 * x * (1.0 - t * t) * (_GELU_C * (1.0 + 3 * 0.044715 * (x * x)))
    return y, dy


def _tile(n, pref):
    t = min(n, pref)
    assert n % t == 0, (n, t)
    return t


HBM = pl.BlockSpec(memory_space=pltpu.HBM)
VMEM = pl.BlockSpec(memory_space=pltpu.VMEM)
DMA_SEMS = pltpu.SemaphoreType.DMA


def _place():
    x, y, c = lax.axis_index("x"), lax.axis_index("y"), lax.axis_index("c")
    chips = [(1 - x, y), (x, 1 - y), (1 - x, 1 - y)]
    return (x, y, c), 2 * x + y, (x, y, 1 - c), chips, [2 * px + py for px, py in chips]


def _remote(src, dst, send_sem, recv_sem, device):
    return pltpu.make_async_remote_copy(src_ref=src, dst_ref=dst, send_sem=send_sem, recv_sem=recv_sem,
                                        device_id=device, device_id_type=MESH)


def _half(rows, c):
    return pl.ds(c * (rows // 2), rows // 2)


class _Exchange:
    aliases = {}

    def start(self, ins, outs, sems):
        local, outgoing, _ = self._copies(ins, outs, sems)
        for cp in local + outgoing:
            cp.start()

    def finish(self, ins, outs, sems):
        local, outgoing, incoming = self._copies(ins, outs, sems)
        for cp in incoming:
            cp.wait_recv()
        for cp in outgoing:
            cp.wait_send()
        for cp in local:
            cp.wait()


class _Gather(_Exchange):
    def __init__(self, shards, split):
        self.inputs, self.split = list(shards), split
        self.out_shape = [jax.ShapeDtypeStruct((N_CHIPS, *a.shape), a.dtype) for a in shards]
        self.sems = [DMA_SEMS((len(shards), 3)), DMA_SEMS((len(shards), 3)), DMA_SEMS((len(shards),))]

    def _copies(self, ins, outs, sems):
        send, recv, lsem = sems
        (x, y, c), me, sibling, chips, ids = _place()
        local = [pltpu.make_async_copy(ins[t], outs[t].at[me], lsem.at[t]) for t in range(len(ins))]
        outgoing, incoming = [], []
        for t, a in enumerate(self.inputs):
            rows = _half(a.shape[0], c) if self.split[t] else pl.ds(0, a.shape[0])
            for k in range(3):
                to = (*chips[k], c)
                outgoing.append(_remote(ins[t].at[rows, :], outs[t].at[me, rows, :], send.at[t, k], recv.at[t, k], to))
                incoming.append(_remote(ins[t].at[rows, :], outs[t].at[ids[k], rows, :], send.at[t, k], recv.at[t, k], to))
        return local, outgoing, incoming


class _Forward(_Exchange):
    def __init__(self, arrays):
        self.inputs = list(arrays)
        self.out_shape = [jax.ShapeDtypeStruct(a.shape, a.dtype) for a in arrays]
        self.aliases = {t: t for t in range(len(arrays))}
        self.sems = [DMA_SEMS((len(arrays), 3)), DMA_SEMS((len(arrays), 3))]

    def _copies(self, ins, outs, sems):
        send, recv = sems
        (x, y, c), me, sibling, chips, ids = _place()
        outgoing, incoming = [], []
        for t, a in enumerate(self.inputs):
            for k in range(3):
                mine = outs[t].at[ids[k], _half(a.shape[1], c), :]
                theirs = outs[t].at[ids[k], _half(a.shape[1], 1 - c), :]
                outgoing.append(_remote(mine, mine, send.at[t, k], recv.at[t, k], sibling))
                incoming.append(_remote(theirs, theirs, send.at[t, k], recv.at[t, k], sibling))
        return [], outgoing, incoming


class _GatherForward(_Exchange):
    def __init__(self, shards):
        self.gather = _Gather(shards, [True] * len(shards))
        self.forward = _Forward(self.gather.out_shape)
        self.inputs, self.out_shape = self.gather.inputs, self.gather.out_shape
        self.sems = self.gather.sems + self.forward.sems

    def start(self, ins, outs, sems):
        self.gather.start(ins, outs, sems[:3])

    def finish(self, ins, outs, sems):
        local, outgoing, incoming = self.gather._copies(ins, outs, sems[:3])
        _, passed, from_sibling = self.forward._copies(outs, outs, sems[3:])
        for landed, onward in zip(incoming, passed):
            landed.wait_recv()
            onward.start()
        for cp in from_sibling:
            cp.wait_recv()
        for cp in outgoing + passed:
            cp.wait_send()
        for cp in local:
            cp.wait()


class _Pair(_Exchange):
    def __init__(self, grads):
        self.inputs = list(grads)
        self.out_shape = [jax.ShapeDtypeStruct((g.shape[0], g.shape[1] // 2, g.shape[2]), g.dtype) for g in grads]
        self.sems = [DMA_SEMS((len(grads),)), DMA_SEMS((len(grads),))]

    def _copies(self, ins, outs, sems):
        send, recv = sems
        (x, y, c), me, sibling, chips, ids = _place()
        cps = [_remote(ins[t].at[:, _half(g.shape[1], 1 - c), :], outs[t], send.at[t], recv.at[t], sibling)
               for t, g in enumerate(self.inputs)]
        return [], cps, cps


class _Chip(_Exchange):
    def __init__(self, parts):
        self.inputs = list(parts)
        self.out_shape = [jax.ShapeDtypeStruct(p.shape, p.dtype) for p in parts]
        self.sems = [DMA_SEMS((len(parts), 3)), DMA_SEMS((len(parts), 3)), DMA_SEMS((len(parts),))]

    def _copies(self, ins, outs, sems):
        send, recv, lsem = sems
        (x, y, c), me, sibling, chips, ids = _place()
        local = [pltpu.make_async_copy(ins[t].at[me], outs[t].at[me], lsem.at[t]) for t in range(len(ins))]
        outgoing, incoming = [], []
        for t in range(len(ins)):
            for k in range(3):
                to = (*chips[k], c)
                outgoing.append(_remote(ins[t].at[ids[k]], outs[t].at[me], send.at[t, k], recv.at[t, k], to))
                incoming.append(_remote(ins[t].at[ids[k]], outs[t].at[ids[k]], send.at[t, k], recv.at[t, k], to))
        return local, outgoing, incoming


class _Share(_Exchange):
    def __init__(self, grads):
        self.inputs = list(grads)
        self.out_shape = [jax.ShapeDtypeStruct(g.shape, g.dtype) for g in grads]
        self.aliases = {t: t for t in range(len(grads))}
        self.sems = [DMA_SEMS((len(grads),)), DMA_SEMS((len(grads),))]

    def _copies(self, ins, outs, sems):
        send, recv = sems
        (x, y, c), me, sibling, chips, ids = _place()
        outgoing, incoming = [], []
        for t, g in enumerate(self.inputs):
            mine = outs[t].at[_half(g.shape[0], c), :]
            theirs = outs[t].at[_half(g.shape[0], 1 - c), :]
            outgoing.append(_remote(mine, mine, send.at[t], recv.at[t], sibling))
            incoming.append(_remote(theirs, theirs, send.at[t], recv.at[t], sibling))
        return [], outgoing, incoming


class _GatherSmall(_Exchange):
    def __init__(self, block):
        self.inputs = [block]
        self.out_shape = [jax.ShapeDtypeStruct((N_DEV, *block.shape), block.dtype)]
        self.sems = [DMA_SEMS((7,)), DMA_SEMS((7,)), DMA_SEMS(())]

    def _copies(self, ins, outs, sems):
        send, recv, lsem = sems
        (x, y, c), me, sibling, chips, ids = _place()
        slot = lambda px, py, pc: outs[0].at[4 * px + 2 * py + pc]

        def copy(k, block, to, src=None):
            return _remote(slot(*block) if src is None else src, slot(*block), send.at[k], recv.at[k], to)

        local = [pltpu.make_async_copy(ins[0], slot(x, y, c), lsem)]
        first = [copy(0, (x, y, c), sibling, src=ins[0])] + [copy(1 + j, (x, y, c), (*chip, c), src=ins[0]) for j, chip in enumerate(chips)]
        passed = [copy(4 + j, (*chip, c), sibling) for j, chip in enumerate(chips)]
        landed = [copy(1 + j, (*chip, c), (x, y, c)) for j, chip in enumerate(chips)]
        from_sibling = [copy(0, (x, y, 1 - c), (x, y, c))] + [copy(4 + j, (*chip, 1 - c), (x, y, c)) for j, chip in enumerate(chips)]
        return local, first, (passed, landed, from_sibling)

    def finish(self, ins, outs, sems):
        local, first, (passed, landed, from_sibling) = self._copies(ins, outs, sems)
        for j in range(3):
            landed[j].wait_recv()
            passed[j].start()
        for cp in from_sibling:
            cp.wait_recv()
        for cp in first + passed:
            cp.wait_send()
        for cp in local:
            cp.wait()


def _split_refs(refs, counts):
    out = []
    for n in counts:
        out.append(refs[:n])
        refs = refs[n:]
    return out


def _each_exchange(exchanges, method, x_in, x_out, x_sem):
    for ex in exchanges:
        ni, no, ns = len(ex.inputs), len(ex.out_shape), len(ex.sems)
        getattr(ex, method)(x_in[:ni], x_out[:no], x_sem[:ns])
        x_in, x_out, x_sem = x_in[ni:], x_out[no:], x_sem[ns:]


def _call(body, *, name, grid, in_specs, out_specs, out_shape, operands, semantics, scratch_shapes=(), exchanges=()):
    x_in = [a for ex in exchanges for a in ex.inputs]
    x_out = [s for ex in exchanges for s in ex.out_shape]
    x_sem = [s for ex in exchanges for s in ex.sems]
    counts = (len(in_specs), len(x_in), len(out_specs), len(x_out), len(scratch_shapes), len(x_sem))
    aliases, i0, o0 = {}, len(in_specs), len(out_specs)
    for ex in exchanges:
        aliases.update({i0 + i: o0 + o for i, o in ex.aliases.items()})
        i0, o0 = i0 + len(ex.inputs), o0 + len(ex.out_shape)

    def full_body(*refs):
        ins, xi, outs, xo, scr, xs = _split_refs(list(refs), counts)
        if exchanges:
            @pl.when(functools.reduce(jnp.logical_and, [pl.program_id(a) == 0 for a in range(len(grid))]))
            def _():
                _each_exchange(exchanges, "start", xi, xo, xs)

        body(*ins, *outs, *scr)
        if exchanges:
            @pl.when(functools.reduce(jnp.logical_and, [pl.program_id(a) == grid[a] - 1 for a in range(len(grid))]))
            def _():
                _each_exchange(exchanges, "finish", xi, xo, xs)

    return pl.pallas_call(
        full_body, name=name, grid=grid,
        in_specs=list(in_specs) + [HBM] * len(x_in), out_specs=list(out_specs) + [HBM] * len(x_out),
        out_shape=list(out_shape) + x_out, scratch_shapes=list(scratch_shapes) + x_sem,
        input_output_aliases=aliases, compiler_params=_params(semantics),
    )(*operands, *x_in)


def _run_exchanges(exchanges, name):
    x_in = [a for ex in exchanges for a in ex.inputs]
    x_out = [s for ex in exchanges for s in ex.out_shape]
    x_sem = [s for ex in exchanges for s in ex.sems]
    aliases, i0, o0 = {}, 0, 0
    for ex in exchanges:
        aliases.update({i0 + i: o0 + o for i, o in ex.aliases.items()})
        i0, o0 = i0 + len(ex.inputs), o0 + len(ex.out_shape)

    def body(*refs):
        xi, xo, xs = _split_refs(list(refs), (len(x_in), len(x_out), len(x_sem)))
        _each_exchange(exchanges, "start", xi, xo, xs)
        _each_exchange(exchanges, "finish", xi, xo, xs)

    return pl.pallas_call(
        body, name=name, in_specs=[HBM] * len(x_in), out_specs=[HBM] * len(x_out), out_shape=x_out,
        scratch_shapes=x_sem, input_output_aliases=aliases,
    )(*x_in)


def _inproj_fwd(x, g1, w_in_all):
    L, D = x.shape
    ns, _, nc = w_in_all.shape
    tm = _tile(L, TM_PROJ)

    def body(x_ref, g_ref, w_ref, hn_ref, proj_ref):
        hn = _rms_fwd(x_ref[...], g_ref[...]).astype(MXU_DTYPE)
        hn_ref[...] = hn
        for j in range(ns):
            proj_ref[:, j * nc:(j + 1) * nc] = _dot(hn, w_ref[j])

    return pl.pallas_call(
        body, name="inproj_fwd", grid=(L // tm,),
        in_specs=[pl.BlockSpec((tm, D), lambda i: (i, 0)), _resident((1, D)), _resident(w_in_all.shape)],
        out_specs=[pl.BlockSpec((tm, D), lambda i: (i, 0)), pl.BlockSpec((tm, ns * nc), lambda i: (i, 0))],
        out_shape=[jax.ShapeDtypeStruct((L, D), MXU_DTYPE), jax.ShapeDtypeStruct((L, ns * nc), F32)],
        compiler_params=_params(("arbitrary",)),
    )(x, g1, w_in_all)


def _scan_tile(xr, xi, hr, hi, coef_ref, lanes, reverse):
    for k, j in ((1, 0), (2, 2), (4, 4)):
        ar = coef_ref[0, j, :, lanes]
        ai = coef_ref[0, j + 1, :, lanes]
        shift = SUBLANES - k if reverse else k
        sr = pltpu.roll(xr, shift, 0)
        si = pltpu.roll(xi, shift, 0)
        xr, xi = xr + (ar * sr - ai * si), xi + (ar * si + ai * sr)
    pr = coef_ref[0, 6, :, lanes]
    pi = coef_ref[0, 7, :, lanes]
    return xr + (pr * hr - pi * hi), xi + (pr * hi + pi * hr)


def _s5_fwd(proj, bmat, cmat, coef, dskip, exchanges=()):
    L = proj.shape[0]
    tm = _tile(L, TM_S5)
    nrt = tm // SUBLANES
    lc = min(LANE_CHUNK, WB)

    def body(u_ref, bm_ref, cm_ref, coef_ref, d_ref, sre_ref, sim_ref, ys_ref, hr_ref, hi_ref):
        @pl.when(pl.program_id(1) == 0)
        def _():
            hr_ref[...] = jnp.zeros_like(hr_ref)
            hi_ref[...] = jnp.zeros_like(hi_ref)

        u = u_ref[...]
        bu = _dot(u.astype(MXU_DTYPE), bm_ref[0])
        sre_ref[...] = bu[:, :WB]
        sim_ref[...] = bu[:, WB:]
        for c in range(WB // lc):
            lanes = slice(c * lc, (c + 1) * lc)

            def row_body(r, carry, lanes=lanes):
                hr, hi = carry
                rows = pl.ds(pl.multiple_of(r * SUBLANES, SUBLANES), SUBLANES)
                xr, xi = _scan_tile(sre_ref[rows, lanes], sim_ref[rows, lanes], hr, hi, coef_ref, lanes, False)
                sre_ref[rows, lanes] = xr
                sim_ref[rows, lanes] = xi
                last = SUBLANES - 1
                return (jnp.broadcast_to(xr[last:last + 1, :], xr.shape), jnp.broadcast_to(xi[last:last + 1, :], xi.shape))

            hr, hi = lax.fori_loop(0, nrt, row_body, (hr_ref[:, lanes], hi_ref[:, lanes]))
            hr_ref[:, lanes] = hr
            hi_ref[:, lanes] = hi
        ys = _dot(sre_ref[...].astype(MXU_DTYPE), cm_ref[0, :WB, :]) + _dot(sim_ref[...].astype(MXU_DTYPE), cm_ref[0, WB:, :])
        ys_ref[...] = ys + d_ref[0] * u

    return _call(
        body, name="s5_fwd", grid=(N_GBLK, L // tm), exchanges=exchanges, semantics=("arbitrary", "arbitrary"),
        operands=(proj, bmat, cmat, coef, dskip),
        in_specs=[
            pl.BlockSpec((tm, UB), lambda b, i: (i, b)),
            pl.BlockSpec((1, UB, 2 * WB), lambda b, i: (b, 0, 0)),
            pl.BlockSpec((1, 2 * WB, UB), lambda b, i: (b, 0, 0)),
            pl.BlockSpec((1, 8, SUBLANES, WB), lambda b, i: (b, 0, 0, 0)),
            pl.BlockSpec((1, 1, UB), lambda b, i: (b, 0, 0)),
        ],
        out_specs=[
            pl.BlockSpec((tm, WB), lambda b, i: (i, b)),
            pl.BlockSpec((tm, WB), lambda b, i: (i, b)),
            pl.BlockSpec((tm, UB), lambda b, i: (i, b)),
        ],
        out_shape=[
            jax.ShapeDtypeStruct((L, N_GBLK * WB), F32),
            jax.ShapeDtypeStruct((L, N_GBLK * WB), F32),
            jax.ShapeDtypeStruct((L, D_SSM), F32),
        ],
        scratch_shapes=[pltpu.VMEM((SUBLANES, WB), F32), pltpu.VMEM((SUBLANES, WB), F32)],
    )


def _tail_fwd(x, ys, proj, w_glu, conv_w, g_ssm, g_conv, w_out, g_post, exchanges=()):
    L, D = x.shape
    tm = _tile(L, TM_TAIL)

    def body(x_ref, ys_ref, h_ref, bg_ref, cg_ref, wglu_ref, cw_ref, gs_ref, gc_ref, wout_ref, gp_ref,
             ycat_ref, o_ref, x1_ref, zbuf):
        @pl.when(pl.program_id(0) == 0)
        def _():
            zbuf[0:SUBLANES, :] = jnp.zeros((SUBLANES, D_CONV), F32)

        y1, _ = _gelu(ys_ref[...])
        y2 = y1 * jax.nn.sigmoid(_dot(y1.astype(MXU_DTYPE), wglu_ref[...]))
        ycat_ref[:, :D_SSM] = _rms_fwd(y2, gs_ref[...]).astype(MXU_DTYPE)
        z = cg_ref[...] * h_ref[...]
        zbuf[SUBLANES:, :] = z
        conv = cw_ref[0:1, :] * zbuf[SUBLANES - 2:SUBLANES - 2 + tm, :] + cw_ref[1:2, :] * zbuf[SUBLANES - 1:SUBLANES - 1 + tm, :] + cw_ref[2:3, :] * z
        zbuf[0:SUBLANES, :] = zbuf[tm:tm + SUBLANES, :]
        ycat_ref[:, D_SSM:] = _rms_fwd(bg_ref[...] * conv, gc_ref[...]).astype(MXU_DTYPE)
        o = _dot(ycat_ref[...], wout_ref[...])
        o_ref[...] = o
        x1_ref[...] = x_ref[...] + _rms_fwd(o, gp_ref[...])

    row = lambda i: (i, 0)
    return _call(
        body, name="tail_fwd", grid=(L // tm,), exchanges=exchanges, semantics=("arbitrary",),
        operands=(x, ys, proj, proj, proj, w_glu, conv_w, g_ssm, g_conv, w_out, g_post),
        in_specs=[
            pl.BlockSpec((tm, D), row), pl.BlockSpec((tm, D_SSM), row),
            pl.BlockSpec((tm, D_CONV), lambda i: (i, 1)), pl.BlockSpec((tm, D_CONV), lambda i: (i, 2)),
            pl.BlockSpec((tm, D_CONV), lambda i: (i, 3)),
            _resident(w_glu.shape), _resident(conv_w.shape), _resident(g_ssm.shape), _resident(g_conv.shape),
            _resident(w_out.shape), _resident(g_post.shape),
        ],
        out_specs=[pl.BlockSpec((tm, D), row), pl.BlockSpec((tm, D), row), pl.BlockSpec((tm, D), row)],
        out_shape=[jax.ShapeDtypeStruct((L, D), MXU_DTYPE), jax.ShapeDtypeStruct((L, D), F32), jax.ShapeDtypeStruct((L, D), F32)],
        scratch_shapes=[pltpu.VMEM((tm + SUBLANES, D_CONV), F32)],
    )


def _mlp_fwd(x1, target, w_up_all, w_down, g_pre, g_post):
    L, D = x1.shape
    ns, _, fc = w_up_all.shape
    tm = _tile(L, TM_MLP)

    def body(x1_ref, t_ref, wup_ref, wdn_ref, gpre_ref, gpost_ref, hn2_ref, up_ref, m_ref, dx2_ref, loss_ref):
        @pl.when(pl.program_id(0) == 0)
        def _():
            loss_ref[...] = jnp.zeros_like(loss_ref)

        x1v = x1_ref[...]
        hn2 = _rms_fwd(x1v, gpre_ref[...]).astype(MXU_DTYPE)
        hn2_ref[...] = hn2
        m = jnp.zeros((tm, D), F32)
        for j in range(ns):
            up = _dot(hn2, wup_ref[j])
            up_ref[:, j * fc:(j + 1) * fc] = up
            act = jnp.square(jnp.maximum(up, 0.0)).astype(MXU_DTYPE)
            m = m + _dot(act, wdn_ref[j * fc:(j + 1) * fc, :])
        m_ref[...] = m
        err = x1v + _rms_fwd(m, gpost_ref[...]) - t_ref[...]
        loss_ref[...] += 0.5 * jnp.sum(jnp.mean(err * err, axis=-1, keepdims=True))
        dx2_ref[...] = err * (1.0 / D)

    row = lambda i: (i, 0)
    return pl.pallas_call(
        body, name="mlp_fwd", grid=(L // tm,),
        in_specs=[pl.BlockSpec((tm, D), row), pl.BlockSpec((tm, D), row), _resident(w_up_all.shape), _resident(w_down.shape),
                  _resident(g_pre.shape), _resident(g_post.shape)],
        out_specs=[pl.BlockSpec((tm, D), row), pl.BlockSpec((tm, ns * fc), row), pl.BlockSpec((tm, D), row),
                   pl.BlockSpec((tm, D), row), pl.BlockSpec((SUBLANES, 128), lambda i: (0, 0))],
        out_shape=[jax.ShapeDtypeStruct((L, D), MXU_DTYPE), jax.ShapeDtypeStruct((L, ns * fc), F32), jax.ShapeDtypeStruct((L, D), F32),
                   jax.ShapeDtypeStruct((L, D), F32), jax.ShapeDtypeStruct((SUBLANES, 128), F32)],
        compiler_params=_params(("arbitrary",)),
    )(x1, target, w_up_all, w_down, g_pre, g_post)


def _mlp_bwd(dx2, m, up, x1, w_up_all, w_down, g_pre, g_post):
    L, D = x1.shape
    ns, _, fc = w_up_all.shape
    tm = _tile(L, TM_MLP)

    def body(dx2_ref, m_ref, up_ref, x1_ref, wup_ref, wdn_ref, gpre_ref, gpost_ref,
             dm_ref, dup_ref, act_ref, dx1_ref, dgpost_ref, dgpre_ref):
        @pl.when(pl.program_id(0) == 0)
        def _():
            dgpost_ref[...] = jnp.zeros_like(dgpost_ref)
            dgpre_ref[...] = jnp.zeros_like(dgpre_ref)

        dx2v = dx2_ref[...]
        dm, dg = _rms_bwd(m_ref[...], gpost_ref[...], dx2v)
        dgpost_ref[...] += dg
        dm_b = dm.astype(MXU_DTYPE)
        dm_ref[...] = dm_b
        dhn2 = jnp.zeros((tm, D), F32)
        for j in range(ns):
            cols = slice(j * fc, (j + 1) * fc)
            relu = jnp.maximum(up_ref[:, cols], 0.0)
            act_ref[:, cols] = jnp.square(relu).astype(MXU_DTYPE)
            dup = (_dot_nt(dm_b, wdn_ref[cols, :]) * (2.0 * relu)).astype(MXU_DTYPE)
            dup_ref[:, cols] = dup
            dhn2 = dhn2 + _dot_nt(dup, wup_ref[j])
        dx, dg = _rms_bwd(x1_ref[...], gpre_ref[...], dhn2)
        dgpre_ref[...] += dg
        dx1_ref[...] = dx2v + dx

    row = lambda i: (i, 0)
    vec = pl.BlockSpec((1, D), lambda i: (0, 0))
    return pl.pallas_call(
        body, name="mlp_bwd", grid=(L // tm,),
        in_specs=[pl.BlockSpec((tm, D), row), pl.BlockSpec((tm, D), row), pl.BlockSpec((tm, ns * fc), row), pl.BlockSpec((tm, D), row),
                  _resident(w_up_all.shape), _resident(w_down.shape), _resident(g_pre.shape), _resident(g_post.shape)],
        out_specs=[pl.BlockSpec((tm, D), row), pl.BlockSpec((tm, ns * fc), row), pl.BlockSpec((tm, ns * fc), row),
                   pl.BlockSpec((tm, D), row), vec, vec],
        out_shape=[jax.ShapeDtypeStruct((L, D), MXU_DTYPE), jax.ShapeDtypeStruct((L, ns * fc), MXU_DTYPE),
                   jax.ShapeDtypeStruct((L, ns * fc), MXU_DTYPE), jax.ShapeDtypeStruct((L, D), F32),
                   jax.ShapeDtypeStruct((1, D), F32), jax.ShapeDtypeStruct((1, D), F32)],
        compiler_params=_params(("arbitrary",)),
    )(dx2, m, up, x1, w_up_all, w_down, g_pre, g_post)


def _tail_bwd(dx1, o, ys, proj, w_glu, conv_w, g_ssm, g_conv, w_out, g_post, exchanges=()):
    L, D = dx1.shape
    tm = _tile(L, TM_TAIL)
    nt = L // tm
    hb = tm // SUBLANES

    def body(dx1_ref, o_ref, ys_ref, h_ref, bg_ref, cg_ref, hh_ref, hcg_ref, wglu_ref, cw_ref, gs_ref, gc_ref, wout_ref, gp_ref,
             do_ref, da_ref, y1_ref, dys_ref, dhbc_ref, dgp_ref, dgs_ref, dgc_ref, dcw_ref, zbuf, dcbuf):
        step = pl.program_id(0)

        @pl.when(step == 0)
        def _():
            dcbuf[tm:, :] = jnp.zeros((SUBLANES, D_CONV), F32)
            dgp_ref[...] = jnp.zeros_like(dgp_ref)
            dgs_ref[...] = jnp.zeros_like(dgs_ref)
            dgc_ref[...] = jnp.zeros_like(dgc_ref)
            dcw_ref[...] = jnp.zeros_like(dcw_ref)

        do, dg = _rms_bwd(o_ref[...], gp_ref[...], dx1_ref[...])
        dgp_ref[...] += dg
        do_b = do.astype(MXU_DTYPE)
        do_ref[...] = do_b
        dycat = _dot_nt(do_b, wout_ref[...])
        y1, dgelu = _gelu(ys_ref[...])
        y1_b = y1.astype(MXU_DTYPE)
        y1_ref[...] = y1_b
        s = jax.nn.sigmoid(_dot(y1_b, wglu_ref[...]))
        dy2, dg = _rms_bwd(y1 * s, gs_ref[...], dycat[:, :D_SSM])
        dgs_ref[...] += dg
        da_b = (dy2 * y1 * s * (1.0 - s)).astype(MXU_DTYPE)
        da_ref[...] = da_b
        dys_ref[...] = (dy2 * s + _dot_nt(da_b, wglu_ref[...])) * dgelu
        h = h_ref[...]
        cg = cg_ref[...]
        bg = bg_ref[...]
        z = cg * h
        first = step == nt - 1
        zbuf[0:SUBLANES, :] = jnp.where(first, 0.0, hcg_ref[...] * hh_ref[...])
        zbuf[SUBLANES:, :] = z
        z1 = zbuf[SUBLANES - 1:SUBLANES - 1 + tm, :]
        z2 = zbuf[SUBLANES - 2:SUBLANES - 2 + tm, :]
        conv = cw_ref[0:1, :] * z2 + cw_ref[1:2, :] * z1 + cw_ref[2:3, :] * z
        dyc, dg = _rms_bwd(bg * conv, gc_ref[...], dycat[:, D_SSM:])
        dgc_ref[...] += dg
        dconv = dyc * bg
        dcw_ref[0:1, :] += jnp.sum(dconv * z2, axis=0, keepdims=True)
        dcw_ref[1:2, :] += jnp.sum(dconv * z1, axis=0, keepdims=True)
        dcw_ref[2:3, :] += jnp.sum(dconv * z, axis=0, keepdims=True)
        dcbuf[0:tm, :] = dconv
        dz = cw_ref[2:3, :] * dconv + cw_ref[1:2, :] * dcbuf[1:1 + tm, :] + cw_ref[0:1, :] * dcbuf[2:2 + tm, :]
        dcbuf[tm:, :] = dcbuf[0:SUBLANES, :]
        dhbc_ref[:, 0:D_CONV] = (dz * cg).astype(MXU_DTYPE)
        dhbc_ref[:, D_CONV:2 * D_CONV] = (dyc * conv).astype(MXU_DTYPE)
        dhbc_ref[:, 2 * D_CONV:] = (dz * h).astype(MXU_DTYPE)

    rev = lambda i: (nt - 1 - i, 0)
    col = lambda c: (lambda i: (nt - 1 - i, c))
    halo = lambda c: (lambda i: (jnp.maximum((nt - 1 - i) * hb - 1, 0), c))
    vec = lambda n: pl.BlockSpec((1, n), lambda i: (0, 0))
    return _call(
        body, name="tail_bwd", grid=(nt,), exchanges=exchanges, semantics=("arbitrary",),
        operands=(dx1, o, ys, proj, proj, proj, proj, proj, w_glu, conv_w, g_ssm, g_conv, w_out, g_post),
        in_specs=[
            pl.BlockSpec((tm, D), rev), pl.BlockSpec((tm, D), rev), pl.BlockSpec((tm, D_SSM), rev),
            pl.BlockSpec((tm, D_CONV), col(1)), pl.BlockSpec((tm, D_CONV), col(2)), pl.BlockSpec((tm, D_CONV), col(3)),
            pl.BlockSpec((SUBLANES, D_CONV), halo(1)), pl.BlockSpec((SUBLANES, D_CONV), halo(3)),
            _resident(w_glu.shape), _resident(conv_w.shape), _resident(g_ssm.shape), _resident(g_conv.shape),
            _resident(w_out.shape), _resident(g_post.shape),
        ],
        out_specs=[
            pl.BlockSpec((tm, D), rev), pl.BlockSpec((tm, D_SSM), rev), pl.BlockSpec((tm, D_SSM), rev), pl.BlockSpec((tm, D_SSM), rev),
            pl.BlockSpec((tm, 3 * D_CONV), rev), vec(D), vec(D_SSM), vec(D_CONV),
            pl.BlockSpec((SUBLANES, D_CONV), lambda i: (0, 0)),
        ],
        out_shape=[
            jax.ShapeDtypeStruct((L, D), MXU_DTYPE), jax.ShapeDtypeStruct((L, D_SSM), MXU_DTYPE), jax.ShapeDtypeStruct((L, D_SSM), MXU_DTYPE),
            jax.ShapeDtypeStruct((L, D_SSM), F32), jax.ShapeDtypeStruct((L, 3 * D_CONV), MXU_DTYPE),
            jax.ShapeDtypeStruct((1, D), F32), jax.ShapeDtypeStruct((1, D_SSM), F32), jax.ShapeDtypeStruct((1, D_CONV), F32),
            jax.ShapeDtypeStruct((SUBLANES, D_CONV), F32),
        ],
        scratch_shapes=[pltpu.VMEM((tm + SUBLANES, D_CONV), F32), pltpu.VMEM((tm + SUBLANES, D_CONV), F32)],
    )


def _s5_bwd(dys, proj, s_re, s_im, bmat, cmat, coef_rev, dskip, exchanges=()):
    L = dys.shape[0]
    tm = _tile(L, TM_S5)
    nt = L // tm
    nrt = tm // SUBLANES
    lc = min(LANE_CHUNK, WB)

    def body(dys_ref, u_ref, sre_ref, sim_ref, bm_ref, cm_ref, coef_ref, d_ref,
             du_ref, gb_ref, gc_ref, q_ref, gd_ref, lr_ref, li_ref, hr_ref, hi_ref, qr_acc, qi_acc):
        step = pl.program_id(1)

        @pl.when(step == 0)
        def _():
            for ref in (hr_ref, hi_ref, qr_acc, qi_acc, gb_ref, gc_ref, gd_ref):
                ref[...] = jnp.zeros_like(ref)

        dys_v = dys_ref[...]
        u = u_ref[...]
        dys_b = dys_v.astype(MXU_DTYPE)
        u_b = u.astype(MXU_DTYPE)
        d = _dot_nt(dys_b, cm_ref[0])
        lr_ref[...] = d[:, :WB]
        li_ref[...] = d[:, WB:]
        for c in range(WB // lc):
            lanes = slice(c * lc, (c + 1) * lc)

            def row_body(k, carry, lanes=lanes):
                hr, hi = carry
                rows = pl.ds(pl.multiple_of((nrt - 1 - k) * SUBLANES, SUBLANES), SUBLANES)
                dr = lr_ref[rows, lanes]
                di = li_ref[rows, lanes]
                xr, xi = _scan_tile(dr, di, hr, hi, coef_ref, lanes, True)
                lr_ref[rows, lanes] = xr
                li_ref[rows, lanes] = xi
                er = xr - dr
                ei = xi - di
                sr = sre_ref[rows, lanes]
                si = sim_ref[rows, lanes]
                qr_acc[:, lanes] += er * sr + ei * si
                qi_acc[:, lanes] += ei * sr - er * si
                return (jnp.broadcast_to(xr[0:1, :], xr.shape), jnp.broadcast_to(xi[0:1, :], xi.shape))

            hr, hi = lax.fori_loop(0, nrt, row_body, (hr_ref[:, lanes], hi_ref[:, lanes]))
            hr_ref[:, lanes] = hr
            hi_ref[:, lanes] = hi
        lr_b = lr_ref[...].astype(MXU_DTYPE)
        li_b = li_ref[...].astype(MXU_DTYPE)
        du_ref[...] = _dot_nt(lr_b, bm_ref[0, :, :WB]) + _dot_nt(li_b, bm_ref[0, :, WB:]) + d_ref[0] * dys_v
        gb_ref[0, :WB, :] += _dot_tn(lr_b, u_b)
        gb_ref[0, WB:, :] += _dot_tn(li_b, u_b)
        gc_ref[0, :, :WB] += _dot_tn(dys_b, sre_ref[...].astype(MXU_DTYPE))
        gc_ref[0, :, WB:] += _dot_tn(dys_b, sim_ref[...].astype(MXU_DTYPE))
        gd_ref[0] += jnp.sum(dys_v * u, axis=0, keepdims=True)

        @pl.when(step == nt - 1)
        def _():
            q_ref[0, 0:1, :] = jnp.sum(qr_acc[...], axis=0, keepdims=True)
            q_ref[0, 1:2, :] = jnp.sum(qi_acc[...], axis=0, keepdims=True)

    rev = lambda b, i: (nt - 1 - i, b)
    blk = lambda b, i: (b, 0, 0)
    return _call(
        body, name="s5_bwd", grid=(N_GBLK, nt), exchanges=exchanges, semantics=("arbitrary", "arbitrary"),
        operands=(dys, proj, s_re, s_im, bmat, cmat, coef_rev, dskip),
        in_specs=[
            pl.BlockSpec((tm, UB), rev), pl.BlockSpec((tm, UB), rev), pl.BlockSpec((tm, WB), rev), pl.BlockSpec((tm, WB), rev),
            pl.BlockSpec((1, UB, 2 * WB), blk), pl.BlockSpec((1, 2 * WB, UB), blk),
            pl.BlockSpec((1, 8, SUBLANES, WB), lambda b, i: (b, 0, 0, 0)), pl.BlockSpec((1, 1, UB), blk),
        ],
        out_specs=[
            pl.BlockSpec((tm, UB), rev), pl.BlockSpec((1, 2 * WB, UB), blk), pl.BlockSpec((1, UB, 2 * WB), blk),
            pl.BlockSpec((1, 2, WB), blk), pl.BlockSpec((1, 1, UB), blk),
        ],
        out_shape=[
            jax.ShapeDtypeStruct((L, D_SSM), F32), jax.ShapeDtypeStruct((N_GBLK, 2 * WB, UB), F32),
            jax.ShapeDtypeStruct((N_GBLK, UB, 2 * WB), F32), jax.ShapeDtypeStruct((N_GBLK, 2, WB), F32),
            jax.ShapeDtypeStruct((N_GBLK, 1, UB), F32),
        ],
        scratch_shapes=[pltpu.VMEM((tm, WB), F32), pltpu.VMEM((tm, WB), F32), pltpu.VMEM((SUBLANES, WB), F32),
                        pltpu.VMEM((SUBLANES, WB), F32), pltpu.VMEM((SUBLANES, WB), F32), pltpu.VMEM((SUBLANES, WB), F32)],
    )


def _inproj_bwd(du, dhbc, x, dx1, w_in_all, g1):
    L, D = x.shape
    ns, _, nc = w_in_all.shape
    tm = _tile(L, TM_PROJ)

    def body(du_ref, dhbc_ref, x_ref, dx1_ref, w_ref, g_ref, gx_ref, dproj_ref, dg_ref):
        @pl.when(pl.program_id(0) == 0)
        def _():
            dg_ref[...] = jnp.zeros_like(dg_ref)

        du_b = du_ref[...].astype(MXU_DTYPE)
        dproj_ref[:, :nc] = du_b
        dproj_ref[:, nc:] = dhbc_ref[...]
        dhn = _dot_nt(du_b, w_ref[0])
        for j in range(1, ns):
            dhn = dhn + _dot_nt(dhbc_ref[:, (j - 1) * nc:j * nc], w_ref[j])
        dx, dg = _rms_bwd(x_ref[...], g_ref[...], dhn)
        dg_ref[...] += dg
        gx_ref[...] = dx1_ref[...] + dx

    row = lambda i: (i, 0)
    return pl.pallas_call(
        body, name="inproj_bwd", grid=(L // tm,),
        in_specs=[pl.BlockSpec((tm, nc), row), pl.BlockSpec((tm, (ns - 1) * nc), row), pl.BlockSpec((tm, D), row), pl.BlockSpec((tm, D), row),
                  _resident(w_in_all.shape), _resident(g1.shape)],
        out_specs=[pl.BlockSpec((tm, D), row), pl.BlockSpec((tm, ns * nc), row), pl.BlockSpec((1, D), lambda i: (0, 0))],
        out_shape=[jax.ShapeDtypeStruct((L, D), F32), jax.ShapeDtypeStruct((L, ns * nc), MXU_DTYPE), jax.ShapeDtypeStruct((1, D), F32)],
        compiler_params=_params(("arbitrary",)),
    )(du, dhbc, x, dx1, w_in_all, g1)


def _matmul_tn(a, b, name, col_shards=1, exchanges=()):
    L, K = a.shape
    N = b.shape[1]
    tl = _tile(L, TL_TN)
    tk = _tile(K, 1024)
    nw = N // col_shards
    tn = _tile(nw, 1024)
    npb = nw // tn

    def body(a_ref, b_ref, o_ref):
        @pl.when(pl.program_id(2) == 0)
        def _():
            o_ref[...] = jnp.zeros_like(o_ref)

        o_ref[0] += _dot_tn(a_ref[...], b_ref[...])

    return _call(
        body, name=name, grid=(K // tk, N // tn, L // tl), exchanges=exchanges, semantics=("arbitrary", "arbitrary", "arbitrary"),
        operands=(a, b),
        in_specs=[pl.BlockSpec((tl, tk), lambda k, n, l: (l, k)), pl.BlockSpec((tl, tn), lambda k, n, l: (l, n))],
        out_specs=[pl.BlockSpec((1, tk, tn), lambda k, n, l: (n // npb, k, n % npb))],
        out_shape=[jax.ShapeDtypeStruct((col_shards, K, nw), F32)],
    )


def _ssm_discretize(lam_re, lam_im, log_dt, b_re, b_im):
    dt = jnp.exp(log_dt)[:, None]
    zr = lam_re * dt
    zi = lam_im * dt
    mag = jnp.exp(zr)
    abr = mag * jnp.cos(zi)
    abi = mag * jnp.sin(zi)
    nr, ni = abr - 1.0, abi
    den = lam_re * lam_re + lam_im * lam_im
    coef_r = (nr * lam_re + ni * lam_im) / den
    coef_i = (ni * lam_re - nr * lam_im) / den
    bbar_r = coef_r[..., None] * b_re - coef_i[..., None] * b_im
    bbar_i = coef_r[..., None] * b_im + coef_i[..., None] * b_re
    return zr, zi, bbar_r, bbar_i


def _block_diag(t):
    nb, g, r, c = t.shape
    eye = jnp.eye(g, dtype=t.dtype)
    return (t[:, :, :, None, :] * eye[None, :, None, :, None]).reshape(nb, g * r, g * c)


def _diag_blocks(m, r, c):
    nb = m.shape[0]
    g = m.shape[1] // r
    eye = jnp.eye(g, dtype=m.dtype)
    t = (m.reshape(nb, g, r, g, c) * eye[None, :, None, :, None]).sum(axis=3)
    return t.reshape(nb * g, r, c)


def _scan_tables(ar, ai, reverse):
    def mul(p, q):
        return (p[0] * q[0] - p[1] * q[1], p[0] * q[1] + p[1] * q[0])

    a1 = (ar, -ai if reverse else ai)
    a2 = mul(a1, a1)
    a3 = mul(a2, a1)
    a4 = mul(a2, a2)
    pw = [a1, a2, a3, a4, mul(a4, a1), mul(a4, a2), mul(a4, a3), mul(a4, a4)]
    t = jnp.arange(SUBLANES)[:, None]
    tabs = []
    for k, ak in ((1, a1), (2, a2), (4, a4)):
        mask = (t + k <= SUBLANES - 1) if reverse else (t >= k)
        tabs += [jnp.where(mask, ak[0][None, :], 0.0), jnp.where(mask, ak[1][None, :], 0.0)]
    order = range(SUBLANES - 1, -1, -1) if reverse else range(SUBLANES)
    tabs += [jnp.stack([pw[j][0] for j in order]), jnp.stack([pw[j][1] for j in order])]
    coef = jnp.stack(tabs)
    return coef.reshape(8, SUBLANES, N_GBLK, WB).transpose(2, 0, 1, 3)


def _ssm_matrices(lam_re, lam_im, log_dt, b_re, b_im, c_re, c_im):
    zr, zi, bbar_r, bbar_i = _ssm_discretize(lam_re, lam_im, log_dt, b_re, b_im)
    mag = jnp.exp(zr)
    ar = (mag * jnp.cos(zi)).reshape(-1)
    ai = (mag * jnp.sin(zi)).reshape(-1)
    blk = lambda t: t.reshape(N_GBLK, G_PER_BLK, *t.shape[1:])
    bmat = jnp.concatenate([_block_diag(blk(bbar_r).transpose(0, 1, 3, 2)), _block_diag(blk(bbar_i).transpose(0, 1, 3, 2))], axis=2)
    cmat = jnp.concatenate([_block_diag(blk(c_re).transpose(0, 1, 3, 2)), _block_diag(blk(-c_im).transpose(0, 1, 3, 2))], axis=1)
    return bmat.astype(MXU_DTYPE), cmat.astype(MXU_DTYPE), _scan_tables(ar, ai, False), _scan_tables(ar, ai, True)


def _ssm_param_grads(lam_re, lam_im, log_dt, b_re, b_im, gb, gc, q, gd):
    gbr = _diag_blocks(gb[:, :WB, :], STATE, GROUP)
    gbi = _diag_blocks(gb[:, WB:, :], STATE, GROUP)
    d_c_re = _diag_blocks(gc[:, :, :WB], GROUP, STATE)
    d_c_im = -_diag_blocks(gc[:, :, WB:], GROUP, STATE)
    qr = q[:, 0, :].reshape(N_GROUPS, STATE)
    qi = q[:, 1, :].reshape(N_GROUPS, STATE)
    _, vjp = jax.vjp(_ssm_discretize, lam_re, lam_im, log_dt, b_re, b_im)
    d_lam_re, d_lam_im, d_log_dt, d_b_re, d_b_im = vjp((qr, qi, gbr, gbi))
    return d_lam_re, d_lam_im, d_log_dt, d_b_re, d_b_im, d_c_re, d_c_im, gd.reshape(N_GROUPS, GROUP)


def _row_tile(rows, n):
    return _tile(rows, max(SUBLANES, (2 * 1024 * 1024) // (4 * n)))


def _pair_add(grad, other, core, name):
    ns, h, n = other.shape
    tr = _row_tile(h, n)
    nb = h // tr

    def body(c_ref, g_ref, o_ref, out_ref):
        out_ref[...] = (g_ref[...] + o_ref[...]).astype(WIRE_DTYPE)

    return pl.pallas_call(
        body, name=name,
        grid_spec=pltpu.PrefetchScalarGridSpec(
            num_scalar_prefetch=1, grid=(ns, nb),
            in_specs=[pl.BlockSpec((1, tr, n), lambda s, i, c: (s, c[0] * nb + i, 0)), pl.BlockSpec((1, tr, n), lambda s, i, c: (s, i, 0))],
            out_specs=pl.BlockSpec((1, tr, n), lambda s, i, c: (s, i, 0))),
        out_shape=jax.ShapeDtypeStruct(other.shape, WIRE_DTYPE),
        compiler_params=_params(("arbitrary", "arbitrary")),
    )(core, grad, other)


def _quad_sum(parts, core, name):
    ns, h, n = parts.shape
    tr = _row_tile(h, n)
    nb = h // tr

    def body(c_ref, p_ref, out_ref):
        p = [p_ref[k].astype(F32) for k in range(ns)]
        out_ref[...] = ((p[0] + p[1]) + p[2]) + p[3]

    return pl.pallas_call(
        body, name=name,
        grid_spec=pltpu.PrefetchScalarGridSpec(
            num_scalar_prefetch=1, grid=(nb,),
            in_specs=[pl.BlockSpec((ns, tr, n), lambda i, c: (0, i, 0))],
            out_specs=pl.BlockSpec((tr, n), lambda i, c: (c[0] * nb + i, 0))),
        out_shape=jax.ShapeDtypeStruct((2 * h, n), F32),
        compiler_params=_params(("arbitrary",)),
    )(core, parts)


def _sum_devices(blocks):
    nd, m, n = blocks.shape

    def body(b_ref, out_ref):
        total = b_ref[0]
        for d in range(1, nd):
            total = total + b_ref[d]
        out_ref[...] = total

    return pl.pallas_call(body, name="sum_devices", in_specs=[VMEM], out_specs=VMEM,
                          out_shape=jax.ShapeDtypeStruct((m, n), blocks.dtype))(blocks)


def _adamw_math(w, g, m, v):
    m = ADAM_B1 * m + (1.0 - ADAM_B1) * g
    v = ADAM_B2 * v + (1.0 - ADAM_B2) * jnp.square(g)
    m_hat = m / (1.0 - ADAM_B1 ** ADAM_STEP)
    v_hat = v / (1.0 - ADAM_B2 ** ADAM_STEP)
    delta = -ADAM_LR * (m_hat / (jnp.sqrt(v_hat) + ADAM_EPS) + ADAM_WD * w)
    return delta, m, v


def _adamw(w, g, m, v, name):
    r, n = w.shape
    tr = _row_tile(r, n)

    def body(w_ref, g_ref, m_ref, v_ref, d_ref, nm_ref, nv_ref):
        d_ref[...], nm_ref[...], nv_ref[...] = _adamw_math(w_ref[...], g_ref[...], m_ref[...], v_ref[...])

    spec = pl.BlockSpec((tr, n), lambda i: (i, 0))
    return pl.pallas_call(
        body, name=name, grid=(r // tr,), in_specs=[spec] * 4, out_specs=[spec] * 3,
        out_shape=[jax.ShapeDtypeStruct((r, n), F32)] * 3,
        compiler_params=_params(("arbitrary",)),
    )(w, g, m, v)


SMALL_SHAPES = {
    "g_pre_mix": (D_MODEL,), "lam_re": (N_GROUPS, STATE), "lam_im": (N_GROUPS, STATE), "log_dt": (N_GROUPS,),
    "b_re": (N_GROUPS, STATE, GROUP), "b_im": (N_GROUPS, STATE, GROUP), "c_re": (N_GROUPS, GROUP, STATE), "c_im": (N_GROUPS, GROUP, STATE),
    "d_skip": (N_GROUPS, GROUP), "conv_w": (3, D_CONV), "g_ssm_out": (D_SSM,), "g_conv_out": (D_CONV,),
    "g_post_mix": (D_MODEL,), "g_pre_mlp": (D_MODEL,), "g_post_mlp": (D_MODEL,),
}
LANES = 128
PACK_TILE = SUBLANES * LANES


def _pack_rows(name):
    n = math.prod(SMALL_SHAPES[name])
    return SUBLANES * (-(-n // PACK_TILE))


def _pack_offsets():
    offs, row = {}, 0
    for name in SMALL_SHAPES:
        offs[name] = row
        row += _pack_rows(name)
    return offs, row


def _pack_small(grads):
    parts = []
    for name in SMALL_SHAPES:
        flat = grads[name].reshape(-1)
        parts.append(jnp.pad(flat, (0, _pack_rows(name) * LANES - flat.shape[0])).reshape(-1, LANES))
    return jnp.concatenate(parts, axis=0)


def _lane_shape(name):
    n = math.prod(SMALL_SHAPES[name])
    return (n // LANES, LANES) if n % LANES == 0 else (1, n)


def _adamw_small(pack, conv_grad, w, m, v):
    names = list(SMALL_SHAPES)
    offs, _ = _pack_offsets()
    nn = len(names)

    def body(*refs):
        pack_ref, cg_ref = refs[0], refs[1]
        w_refs, m_refs, v_refs = refs[2:2 + nn], refs[2 + nn:2 + 2 * nn], refs[2 + 2 * nn:2 + 3 * nn]
        outs = refs[2 + 3 * nn:]
        for j, name in enumerate(names):
            r, n = w_refs[j].shape
            g = cg_ref[...] if name == "conv_w" else pack_ref[offs[name]:offs[name] + r, 0:n]
            delta, nm, nv = _adamw_math(w_refs[j][...], g, m_refs[j][...], v_refs[j][...])
            outs[j][...] = g
            outs[nn + j][...] = delta
            outs[2 * nn + j][...] = nm
            outs[3 * nn + j][...] = nv

    args = [pack, conv_grad] + [w[k] for k in names] + [m[k] for k in names] + [v[k] for k in names]
    res = pl.pallas_call(
        body, name="adamw_small", in_specs=[VMEM] * len(args), out_specs=[VMEM] * (4 * nn),
        out_shape=[jax.ShapeDtypeStruct(w[k].shape, F32) for k in names] * 4,
    )(*args)
    return [dict(zip(names, res[q * nn:(q + 1) * nn])) for q in range(4)]


WEIGHTS = ["g_pre_mix", "w_in", "lam_re", "lam_im", "log_dt", "b_re", "b_im", "c_re", "c_im", "d_skip", "w_glu", "conv_w",
           "g_ssm_out", "g_conv_out", "w_out", "g_post_mix", "g_pre_mlp", "w_up", "w_down", "g_post_mlp"]
BIG = ["w_in", "w_glu", "w_out", "w_up", "w_down"]


def kernel(x, g_pre_mix, w_in, lam_re, lam_im, log_dt, b_re, b_im, c_re, c_im, d_skip, w_glu, conv_w, g_ssm_out, g_conv_out, w_out, g_post_mix, g_pre_mlp, w_up, w_down, g_post_mlp, loss_target, m_g_pre_mix, m_w_in, m_lam_re, m_lam_im, m_log_dt, m_b_re, m_b_im, m_c_re, m_c_im, m_d_skip, m_w_glu, m_conv_w, m_g_ssm_out, m_g_conv_out, m_w_out, m_g_post_mix, m_g_pre_mlp, m_w_up, m_w_down, m_g_post_mlp, v_g_pre_mix, v_w_in, v_lam_re, v_lam_im, v_log_dt, v_b_re, v_b_im, v_c_re, v_c_im, v_d_skip, v_w_glu, v_conv_w, v_g_ssm_out, v_g_conv_out, v_w_out, v_g_post_mix, v_g_pre_mlp, v_w_up, v_w_down, v_g_post_mlp):
    w = dict(g_pre_mix=g_pre_mix, w_in=w_in, lam_re=lam_re, lam_im=lam_im, log_dt=log_dt, b_re=b_re, b_im=b_im, c_re=c_re, c_im=c_im,
             d_skip=d_skip, w_glu=w_glu, conv_w=conv_w, g_ssm_out=g_ssm_out, g_conv_out=g_conv_out, w_out=w_out, g_post_mix=g_post_mix,
             g_pre_mlp=g_pre_mlp, w_up=w_up, w_down=w_down, g_post_mlp=g_post_mlp)
    m = dict(g_pre_mix=m_g_pre_mix, w_in=m_w_in, lam_re=m_lam_re, lam_im=m_lam_im, log_dt=m_log_dt, b_re=m_b_re, b_im=m_b_im, c_re=m_c_re,
             c_im=m_c_im, d_skip=m_d_skip, w_glu=m_w_glu, conv_w=m_conv_w, g_ssm_out=m_g_ssm_out, g_conv_out=m_g_conv_out, w_out=m_w_out,
             g_post_mix=m_g_post_mix, g_pre_mlp=m_g_pre_mlp, w_up=m_w_up, w_down=m_w_down, g_post_mlp=m_g_post_mlp)
    v = dict(g_pre_mix=v_g_pre_mix, w_in=v_w_in, lam_re=v_lam_re, lam_im=v_lam_im, log_dt=v_log_dt, b_re=v_b_re, b_im=v_b_im, c_re=v_c_re,
             c_im=v_c_im, d_skip=v_d_skip, w_glu=v_w_glu, conv_w=v_conv_w, g_ssm_out=v_g_ssm_out, g_conv_out=v_g_conv_out, w_out=v_w_out,
             g_post_mix=v_g_post_mix, g_pre_mlp=v_g_pre_mlp, w_up=v_w_up, w_down=v_w_down, g_post_mlp=v_g_post_mlp)
    shapes = {k: a.shape for k, a in w.items()}
    w, m, v = ({k: a[0] for k, a in d.items()} for d in (w, m, v))
    chip = 2 * lax.axis_index("x") + lax.axis_index("y")
    core = lax.axis_index("c").astype(jnp.int32).reshape(1)

    xs, target = x[0], loss_target[0]
    g1 = w["g_pre_mix"][None]
    g_ssm, g_conv = w["g_ssm_out"][None], w["g_conv_out"][None]
    g_post_mix, g_pre_mlp, g_post_mlp = w["g_post_mix"][None], w["g_pre_mlp"][None], w["g_post_mlp"][None]
    bmat, cmat, coef_f, coef_r = _ssm_matrices(w["lam_re"], w["lam_im"], w["log_dt"], w["b_re"], w["b_im"], w["c_re"], w["c_im"])
    dskip = w["d_skip"].reshape(N_GBLK, 1, UB)
    shard = {k: w[k].astype(MXU_DTYPE) for k in BIG}
    conv_pad = jnp.pad(w["conv_w"], ((0, SUBLANES - 3), (0, 0)))

    (w_in_all,) = _run_exchanges([_GatherForward([shard["w_in"]])], "ag_w_in")
    hn, proj = _inproj_fwd(xs, g1, w_in_all)
    rest = _Gather([shard["w_glu"], shard["w_out"], conv_pad, shard["w_up"], shard["w_down"]], [False, False, False, True, True])
    s_re, s_im, ys, w_glu_all, w_out_all, conv_all, w_up_all, w_down_all = _s5_fwd(proj, bmat, cmat, coef_f, dskip, exchanges=[rest])
    w_glu_f, w_out_f = w_glu_all.reshape(D_SSM, D_SSM), w_out_all.reshape(D_MODEL, D_MODEL)
    conv_f = jnp.transpose(conv_all, (1, 0, 2)).reshape(SUBLANES, D_CONV)
    ycat, o, x1, w_up_all, w_down_all = _tail_fwd(xs, ys, proj, w_glu_f, conv_f, g_ssm, g_conv, w_out_f, g_post_mix,
                                                  exchanges=[_Forward([w_up_all, w_down_all])])
    w_down_f = w_down_all.reshape(D_FF, D_MODEL)
    hn2, up, m_act, dx2, loss = _mlp_fwd(x1, target, w_up_all, w_down_f, g_pre_mlp, g_post_mlp)

    dm, dup, act, dx1, dg_post_mlp, dg_pre_mlp = _mlp_bwd(dx2, m_act, up, x1, w_up_all, w_down_f, g_pre_mlp, g_post_mlp)
    gw_down = _matmul_tn(act, dm, "dw_down")[0].reshape(N_CHIPS, D_FF // N_CHIPS, D_MODEL)
    gw_up = _matmul_tn(hn2, dup, "dw_up", col_shards=N_CHIPS)[0]
    do, da, y1, dys, dhbc, dg_post_mix, dg_ssm, dg_conv, dconv_w, o_down, o_up = _tail_bwd(
        dx1, o, ys, proj, w_glu_f, conv_f, g_ssm, g_conv, w_out_f, g_post_mix, exchanges=[_Pair([gw_down, gw_up])])
    p_down = _pair_add(gw_down, o_down, core, "pair_add_w_down")
    p_up = _pair_add(gw_up, o_up, core, "pair_add_w_up")
    gw_out = _matmul_tn(ycat, do, "dw_out")[0].reshape(N_CHIPS, D_MODEL // N_CHIPS, D_MODEL)
    gw_glu = _matmul_tn(y1, da, "dw_glu")[0].reshape(N_CHIPS, D_SSM // N_CHIPS, D_SSM)
    du, gb, gc, q, gd, q_down, q_up, o_out, o_glu = _s5_bwd(
        dys, proj, s_re, s_im, bmat, cmat, coef_r, dskip, exchanges=[_Chip([p_down, p_up]), _Pair([gw_out, gw_glu])])
    h_down = _quad_sum(q_down, core, "quad_sum_w_down")
    h_up = _quad_sum(q_up, core, "quad_sum_w_up")
    p_out = _pair_add(gw_out, o_out, core, "pair_add_w_out")
    p_glu = _pair_add(gw_glu, o_glu, core, "pair_add_w_glu")
    grad_x, dproj, dg_pre_mix = _inproj_bwd(du, dhbc, xs, dx1, w_in_all, g1)
    d_lam_re, d_lam_im, d_log_dt, d_b_re, d_b_im, d_c_re, d_c_im, d_d_skip = _ssm_param_grads(
        w["lam_re"], w["lam_im"], w["log_dt"], w["b_re"], w["b_im"], gb, gc, q, gd)
    small = {
        "g_pre_mix": dg_pre_mix[0], "lam_re": d_lam_re, "lam_im": d_lam_im, "log_dt": d_log_dt, "b_re": d_b_re, "b_im": d_b_im,
        "c_re": d_c_re, "c_im": d_c_im, "d_skip": d_d_skip, "conv_w": dconv_w[:3], "g_ssm_out": dg_ssm[0], "g_conv_out": dg_conv[0],
        "g_post_mix": dg_post_mix[0], "g_pre_mlp": dg_pre_mlp[0], "g_post_mlp": dg_post_mlp[0],
    }
    gw_in, g_down, g_up, q_out, q_glu, packs = _matmul_tn(
        hn, dproj, "dw_in", col_shards=N_CHIPS,
        exchanges=[_Share([h_down, h_up]), _Chip([p_out, p_glu]), _GatherSmall(jnp.concatenate([_pack_small(small), loss], axis=0))])
    h_out = _quad_sum(q_out, core, "quad_sum_w_out")
    h_glu = _quad_sum(q_glu, core, "quad_sum_w_glu")
    pack = _sum_devices(packs)
    loss = pack[pack.shape[0] - SUBLANES, 0]
    (o_in,) = _run_exchanges([_Pair([gw_in])], "rs_pair_w_in")
    p_in = _pair_add(gw_in, o_in, core, "pair_add_w_in")
    q_in, g_out, g_glu = _run_exchanges([_Chip([p_in]), _Share([h_out, h_glu])], "rs_chip_w_in")
    h_in = _quad_sum(q_in, core, "quad_sum_w_in")
    (g_in,) = _run_exchanges([_Share([h_in])], "rs_share_w_in")
    shard_grads = {"w_in": g_in, "w_glu": g_glu, "w_out": g_out, "w_up": g_up, "w_down": g_down}
    offs, _ = _pack_offsets()
    conv_grad = pack[offs["conv_w"]:offs["conv_w"] + 3 * D_CONV // LANES].reshape(3, D_CONV)
    conv_grad = lax.dynamic_slice(conv_grad, (0, chip * (D_CONV // N_CHIPS)), (3, D_CONV // N_CHIPS))

    out = {q: {} for q in ("grad", "delta", "new_m", "new_v")}
    for k in BIG:
        out["grad"][k] = shard_grads[k]
        out["delta"][k], out["new_m"][k], out["new_v"][k] = _adamw(w[k], shard_grads[k], m[k], v[k], "adamw_" + k)
    lane = lambda d: {k: (d[k] if k == "conv_w" else d[k].reshape(_lane_shape(k))) for k in SMALL_SHAPES}
    res = _adamw_small(pack, conv_grad, lane(w), lane(m), lane(v))
    for q, d in zip(("grad", "delta", "new_m", "new_v"), res):
        out[q].update(d)
    flat = [loss, grad_x[None]]
    for q in ("grad", "delta", "new_m", "new_v"):
        flat += [out[q][k].reshape(shapes[k]) for k in WEIGHTS]
    return tuple(flat)
```

```python
import functools
import math

import jax
import jax.numpy as jnp
from jax import lax
from jax.experimental import pallas as pl
from jax.experimental.pallas import tpu as pltpu

F32 = jnp.float32
MXU_DTYPE = jnp.bfloat16
WIRE_DTYPE = jnp.bfloat16

D_MODEL = 1024
D_SSM = 512
D_CONV = 512
N_GROUPS = 32
GROUP = 16
STATE = 64
D_FF = 4096
RMS_EPS = 1e-6
N_CHIPS = 4
N_DEV = 8

ADAM_LR = 0.001
ADAM_B1 = 0.9
ADAM_B2 = 0.999
ADAM_EPS = 1e-08
ADAM_WD = 0.01
ADAM_STEP = 10

N_GBLK = 2
G_PER_BLK = N_GROUPS // N_GBLK
UB = G_PER_BLK * GROUP
WB = G_PER_BLK * STATE
LANE_CHUNK = 1024
SUBLANES = 8

TM_PROJ = 512
TM_S5 = 512
TM_TAIL = 256
TM_MLP = 256
TL_TN = 2048
VMEM_LIMIT = 56 * 1024 * 1024

MESH = pl.DeviceIdType.MESH


def _params(sem, vmem=VMEM_LIMIT):
    return pltpu.CompilerParams(dimension_semantics=sem, vmem_limit_bytes=vmem)


def _resident(shape):
    nd = len(shape)
    return pl.BlockSpec(shape, lambda *_: (0,) * nd, pipeline_mode=pl.Buffered(1))


def _dot(a, b):
    return jnp.dot(a, b, preferred_element_type=F32)


def _dot_nt(a, b):
    return lax.dot_general(a, b, (((1,), (1,)), ((), ())), preferred_element_type=F32)


def _dot_tn(a, b):
    return lax.dot_general(a, b, (((0,), (0,)), ((), ())), preferred_element_type=F32)


def _rms_fwd(x, g):
    r = lax.rsqrt(jnp.mean(x * x, axis=-1, keepdims=True) + RMS_EPS)
    return x * r * g


def _rms_bwd(x, g, dy):
    r = lax.rsqrt(jnp.mean(x * x, axis=-1, keepdims=True) + RMS_EPS)
    xn = x * r
    q = dy * g
    dx = r * (q - xn * jnp.mean(q * xn, axis=-1, keepdims=True))
    return dx, jnp.sum(dy * xn, axis=0, keepdims=True)


_GELU_C = math.sqrt(2.0 / math.pi)


def _gelu(x):
    t = jnp.tanh(_GELU_C * (x + 0.044715 * (x * x * x)))
    y = x * (0.5 * (1.0 + t))
    dy = 0.5 * (1.0 + t) + 0.5 * x * (1.0 - t * t) * (_GELU_C * (1.0 + 3 * 0.044715 * (x * x)))
    return y, dy


def _tile(n, pref):
    t = min(n, pref)
    assert n % t == 0, (n, t)
    return t


HBM = pl.BlockSpec(memory_space=pltpu.HBM)
VMEM = pl.BlockSpec(memory_space=pltpu.VMEM)
DMA_SEMS = pltpu.SemaphoreType.DMA


def _place():
    x, y, c = lax.axis_index("x"), lax.axis_index("y"), lax.axis_index("c")
    chips = [(1 - x, y), (x, 1 - y), (1 - x, 1 - y)]
    return (x, y, c), 2 * x + y, (x, y, 1 - c), chips, [2 * px + py for px, py in chips]


def _remote(src, dst, send_sem, recv_sem, device):
    return pltpu.make_async_remote_copy(src_ref=src, dst_ref=dst, send_sem=send_sem, recv_sem=recv_sem,
                                        device_id=device, device_id_type=MESH)


def _half(rows, c):
    return pl.ds(c * (rows // 2), rows // 2)


class _Exchange:
    aliases = {}

    def start(self, ins, outs, sems):
        local, outgoing, _ = self._copies(ins, outs, sems)
        for cp in local + outgoing:
            cp.start()

    def finish(self, ins, outs, sems):
        local, outgoing, incoming = self._copies(ins, outs, sems)
        for cp in incoming:
            cp.wait_recv()
        for cp in outgoing:
            cp.wait_send()
        for cp in local:
            cp.wait()


class _Gather(_Exchange):
    def __init__(self, shards, split):
        self.inputs, self.split = list(shards), split
        self.out_shape = [jax.ShapeDtypeStruct((N_CHIPS, *a.shape), a.dtype) for a in shards]
        self.sems = [DMA_SEMS((len(shards), 3)), DMA_SEMS((len(shards), 3)), DMA_SEMS((len(shards),))]

    def _copies(self, ins, outs, sems):
        send, recv, lsem = sems
        (x, y, c), me, sibling, chips, ids = _place()
        local = [pltpu.make_async_copy(ins[t], outs[t].at[me], lsem.at[t]) for t in range(len(ins))]
        outgoing, incoming = [], []
        for t, a in enumerate(self.inputs):
            rows = _half(a.shape[0], c) if self.split[t] else pl.ds(0, a.shape[0])
            for k in range(3):
                to = (*chips[k], c)
                outgoing.append(_remote(ins[t].at[rows, :], outs[t].at[me, rows, :], send.at[t, k], recv.at[t, k], to))
                incoming.append(_remote(ins[t].at[rows, :], outs[t].at[ids[k], rows, :], send.at[t, k], recv.at[t, k], to))
        return local, outgoing, incoming


class _Forward(_Exchange):
    def __init__(self, arrays):
        self.inputs = list(arrays)
        self.out_shape = [jax.ShapeDtypeStruct(a.shape, a.dtype) for a in arrays]
        self.aliases = {t: t for t in range(len(arrays))}
        self.sems = [DMA_SEMS((len(arrays), 3)), DMA_SEMS((len(arrays), 3))]

    def _copies(self, ins, outs, sems):
        send, recv = sems
        (x, y, c), me, sibling, chips, ids = _place()
        outgoing, incoming = [], []
        for t, a in enumerate(self.inputs):
            for k in range(3):
                mine = outs[t].at[ids[k], _half(a.shape[1], c), :]
                theirs = outs[t].at[ids[k], _half(a.shape[1], 1 - c), :]
                outgoing.append(_remote(mine, mine, send.at[t, k], recv.at[t, k], sibling))
                incoming.append(_remote(theirs, theirs, send.at[t, k], recv.at[t, k], sibling))
        return [], outgoing, incoming


class _GatherForward(_Exchange):
    def __init__(self, shards):
        self.gather = _Gather(shards, [True] * len(shards))
        self.forward = _Forward(self.gather.out_shape)
        self.inputs, self.out_shape = self.gather.inputs, self.gather.out_shape
        self.sems = self.gather.sems + self.forward.sems

    def start(self, ins, outs, sems):
        self.gather.start(ins, outs, sems[:3])

    def finish(self, ins, outs, sems):
        local, outgoing, incoming = self.gather._copies(ins, outs, sems[:3])
        _, passed, from_sibling = self.forward._copies(outs, outs, sems[3:])
        for landed, onward in zip(incoming, passed):
            landed.wait_recv()
            onward.start()
        for cp in from_sibling:
            cp.wait_recv()
        for cp in outgoing + passed:
            cp.wait_send()
        for cp in local:
            cp.wait()


class _Pair(_Exchange):
    def __init__(self, grads):
        self.inputs = list(grads)
        self.out_shape = [jax.ShapeDtypeStruct((g.shape[0], g.shape[1] // 2, g.shape[2]), g.dtype) for g in grads]
        self.sems = [DMA_SEMS((len(grads),)), DMA_SEMS((len(grads),))]

    def _copies(self, ins, outs, sems):
        send, recv = sems
        (x, y, c), me, sibling, chips, ids = _place()
        cps = [_remote(ins[t].at[:, _half(g.shape[1], 1 - c), :], outs[t], send.at[t], recv.at[t], sibling)
               for t, g in enumerate(self.inputs)]
        return [], cps, cps


class _Chip(_Exchange):
    def __init__(self, parts):
        self.inputs = list(parts)
        self.out_shape = [jax.ShapeDtypeStruct(p.shape, p.dtype) for p in parts]
        self.sems = [DMA_SEMS((len(parts), 3)), DMA_SEMS((len(parts), 3)), DMA_SEMS((len(parts),))]

    def _copies(self, ins, outs, sems):
        send, recv, lsem = sems
        (x, y, c), me, sibling, chips, ids = _place()
        local = [pltpu.make_async_copy(ins[t].at[me], outs[t].at[me], lsem.at[t]) for t in range(len(ins))]
        outgoing, incoming = [], []
        for t in range(len(ins)):
            for k in range(3):
                to = (*chips[k], c)
                outgoing.append(_remote(ins[t].at[ids[k]], outs[t].at[me], send.at[t, k], recv.at[t, k], to))
                incoming.append(_remote(ins[t].at[ids[k]], outs[t].at[ids[k]], send.at[t, k], recv.at[t, k], to))
        return local, outgoing, incoming


class _Share(_Exchange):
    def __init__(self, grads):
        self.inputs = list(grads)
        self.out_shape = [jax.ShapeDtypeStruct(g.shape, g.dtype) for g in grads]
        self.aliases = {t: t for t in range(len(grads))}
        self.sems = [DMA_SEMS((len(grads),)), DMA_SEMS((len(grads),))]

    def _copies(self, ins, outs, sems):
        send, recv = sems
        (x, y, c), me, sibling, chips, ids = _place()
        outgoing, incoming = [], []
        for t, g in enumerate(self.inputs):
            mine = outs[t].at[_half(g.shape[0], c), :]
            theirs = outs[t].at[_half(g.shape[0], 1 - c), :]
            outgoing.append(_remote(mine, mine, send.at[t], recv.at[t], sibling))
            incoming.append(_remote(theirs, theirs, send.at[t], recv.at[t], sibling))
        return [], outgoing, incoming


class _GatherSmall(_Exchange):
    def __init__(self, block):
        self.inputs = [block]
        self.out_shape = [jax.ShapeDtypeStruct((N_DEV, *block.shape), block.dtype)]
        self.sems = [DMA_SEMS((7,)), DMA_SEMS((7,)), DMA_SEMS(())]

    def _copies(self, ins, outs, sems):
        send, recv, lsem = sems
        (x, y, c), me, sibling, chips, ids = _place()
        slot = lambda px, py, pc: outs[0].at[4 * px + 2 * py + pc]

        def copy(k, block, to, src=None):
            return _remote(slot(*block) if src is None else src, slot(*block), send.at[k], recv.at[k], to)

        local = [pltpu.make_async_copy(ins[0], slot(x, y, c), lsem)]
        first = [copy(0, (x, y, c), sibling, src=ins[0])] + [copy(1 + j, (x, y, c), (*chip, c), src=ins[0]) for j, chip in enumerate(chips)]
        passed = [copy(4 + j, (*chip, c), sibling) for j, chip in enumerate(chips)]
        landed = [copy(1 + j, (*chip, c), (x, y, c)) for j, chip in enumerate(chips)]
        from_sibling = [copy(0, (x, y, 1 - c), (x, y, c))] + [copy(4 + j, (*chip, 1 - c), (x, y, c)) for j, chip in enumerate(chips)]
        return local, first, (passed, landed, from_sibling)

    def finish(self, ins, outs, sems):
        local, first, (passed, landed, from_sibling) = self._copies(ins, outs, sems)
        for j in range(3):
            landed[j].wait_recv()
            passed[j].start()
        for cp in from_sibling:
            cp.wait_recv()
        for cp in first + passed:
            cp.wait_send()
        for cp in local:
            cp.wait()


def _split_refs(refs, counts):
    out = []
    for n in counts:
        out.append(refs[:n])
        refs = refs[n:]
    return out


def _each_exchange(exchanges, method, x_in, x_out, x_sem):
    for ex in exchanges:
        ni, no, ns = len(ex.inputs), len(ex.out_shape), len(ex.sems)
        getattr(ex, method)(x_in[:ni], x_out[:no], x_sem[:ns])
        x_in, x_out, x_sem = x_in[ni:], x_out[no:], x_sem[ns:]


def _call(body, *, name, grid, in_specs, out_specs, out_shape, operands, semantics, scratch_shapes=(), exchanges=()):
    x_in = [a for ex in exchanges for a in ex.inputs]
    x_out = [s for ex in exchanges for s in ex.out_shape]
    x_sem = [s for ex in exchanges for s in ex.sems]
    counts = (len(in_specs), len(x_in), len(out_specs), len(x_out), len(scratch_shapes), len(x_sem))
    aliases, i0, o0 = {}, len(in_specs), len(out_specs)
    for ex in exchanges:
        aliases.update({i0 + i: o0 + o for i, o in ex.aliases.items()})
        i0, o0 = i0 + len(ex.inputs), o0 + len(ex.out_shape)

    def full_body(*refs):
        ins, xi, outs, xo, scr, xs = _split_refs(list(refs), counts)
        if exchanges:
            @pl.when(functools.reduce(jnp.logical_and, [pl.program_id(a) == 0 for a in range(len(grid))]))
            def _():
                _each_exchange(exchanges, "start", xi, xo, xs)

        body(*ins, *outs, *scr)
        if exchanges:
            @pl.when(functools.reduce(jnp.logical_and, [pl.program_id(a) == grid[a] - 1 for a in range(len(grid))]))
            def _():
                _each_exchange(exchanges, "finish", xi, xo, xs)

    return pl.pallas_call(
        full_body, name=name, grid=grid,
        in_specs=list(in_specs) + [HBM] * len(x_in), out_specs=list(out_specs) + [HBM] * len(x_out),
        out_shape=list(out_shape) + x_out, scratch_shapes=list(scratch_shapes) + x_sem,
        input_output_aliases=aliases, compiler_params=_params(semantics),
    )(*operands, *x_in)


def _run_exchanges(exchanges, name):
    x_in = [a for ex in exchanges for a in ex.inputs]
    x_out = [s for ex in exchanges for s in ex.out_shape]
    x_sem = [s for ex in exchanges for s in ex.sems]
    aliases, i0, o0 = {}, 0, 0
    for ex in exchanges:
        aliases.update({i0 + i: o0 + o for i, o in ex.aliases.items()})
        i0, o0 = i0 + len(ex.inputs), o0 + len(ex.out_shape)

    def body(*refs):
        xi, xo, xs = _split_refs(list(refs), (len(x_in), len(x_out), len(x_sem)))
        _each_exchange(exchanges, "start", xi, xo, xs)
        _each_exchange(exchanges, "finish", xi, xo, xs)

    return pl.pallas_call(
        body, name=name, in_specs=[HBM] * len(x_in), out_specs=[HBM] * len(x_out), out_shape=x_out,
        scratch_shapes=x_sem, input_output_aliases=aliases,
    )(*x_in)


def _inproj_fwd(x, g1, w_in_all):
    L, D = x.shape
    ns, _, nc = w_in_all.shape
    tm = _tile(L, TM_PROJ)

    def body(x_ref, g_ref, w_ref, hn_ref, proj_ref):
        hn = _rms_fwd(x_ref[...], g_ref[...]).astype(MXU_DTYPE)
        hn_ref[...] = hn
        for j in range(ns):
            proj_ref[:, j * nc:(j + 1) * nc] = _dot(hn, w_ref[j])

    return pl.pallas_call(
        body, name="inproj_fwd", grid=(L // tm,),
        in_specs=[pl.BlockSpec((tm, D), lambda i: (i, 0)), _resident((1, D)), _resident(w_in_all.shape)],
        out_specs=[pl.BlockSpec((tm, D), lambda i: (i, 0)), pl.BlockSpec((tm, ns * nc), lambda i: (i, 0))],
        out_shape=[jax.ShapeDtypeStruct((L, D), MXU_DTYPE), jax.ShapeDtypeStruct((L, ns * nc), F32)],
        compiler_params=_params(("arbitrary",)),
    )(x, g1, w_in_all)


def _scan_tile(xr, xi, hr, hi, coef_ref, lanes, reverse):
    for k, j in ((1, 0), (2, 2), (4, 4)):
        ar = coef_ref[0, j, :, lanes]
        ai = coef_ref[0, j + 1, :, lanes]
        shift = SUBLANES - k if reverse else k
        sr = pltpu.roll(xr, shift, 0)
        si = pltpu.roll(xi, shift, 0)
        xr, xi = xr + (ar * sr - ai * si), xi + (ar * si + ai * sr)
    pr = coef_ref[0, 6, :, lanes]
    pi = coef_ref[0, 7, :, lanes]
    return xr + (pr * hr - pi * hi), xi + (pr * hi + pi * hr)


def _s5_fwd(proj, bmat, cmat, coef, dskip, exchanges=()):
    L = proj.shape[0]
    tm = _tile(L, TM_S5)
    nrt = tm // SUBLANES
    lc = min(LANE_CHUNK, WB)

    def body(u_ref, bm_ref, cm_ref, coef_ref, d_ref, sre_ref, sim_ref, ys_ref, hr_ref, hi_ref):
        @pl.when(pl.program_id(1) == 0)
        def _():
            hr_ref[...] = jnp.zeros_like(hr_ref)
            hi_ref[...] = jnp.zeros_like(hi_ref)

        u = u_ref[...]
        bu = _dot(u.astype(MXU_DTYPE), bm_ref[0])
        sre_ref[...] = bu[:, :WB]
        sim_ref[...] = bu[:, WB:]
        for c in range(WB // lc):
            lanes = slice(c * lc, (c + 1) * lc)

            def row_body(r, carry, lanes=lanes):
                hr, hi = carry
                rows = pl.ds(pl.multiple_of(r * SUBLANES, SUBLANES), SUBLANES)
                xr, xi = _scan_tile(sre_ref[rows, lanes], sim_ref[rows, lanes], hr, hi, coef_ref, lanes, False)
                sre_ref[rows, lanes] = xr
                sim_ref[rows, lanes] = xi
                last = SUBLANES - 1
                return (jnp.broadcast_to(xr[last:last + 1, :], xr.shape), jnp.broadcast_to(xi[last:last + 1, :], xi.shape))

            hr, hi = lax.fori_loop(0, nrt, row_body, (hr_ref[:, lanes], hi_ref[:, lanes]), unroll=True)
            hr_ref[:, lanes] = hr
            hi_ref[:, lanes] = hi
        ys = _dot(sre_ref[...].astype(MXU_DTYPE), cm_ref[0, :WB, :]) + _dot(sim_ref[...].astype(MXU_DTYPE), cm_ref[0, WB:, :])
        ys_ref[...] = ys + d_ref[0] * u

    return _call(
        body, name="s5_fwd", grid=(N_GBLK, L // tm), exchanges=exchanges, semantics=("arbitrary", "arbitrary"),
        operands=(proj, bmat, cmat, coef, dskip),
        in_specs=[
            pl.BlockSpec((tm, UB), lambda b, i: (i, b)),
            pl.BlockSpec((1, UB, 2 * WB), lambda b, i: (b, 0, 0)),
            pl.BlockSpec((1, 2 * WB, UB), lambda b, i: (b, 0, 0)),
            pl.BlockSpec((1, 8, SUBLANES, WB), lambda b, i: (b, 0, 0, 0)),
            pl.BlockSpec((1, 1, UB), lambda b, i: (b, 0, 0)),
        ],
        out_specs=[
            pl.BlockSpec((tm, WB), lambda b, i: (i, b)),
            pl.BlockSpec((tm, WB), lambda b, i: (i, b)),
            pl.BlockSpec((tm, UB), lambda b, i: (i, b)),
        ],
        out_shape=[
            jax.ShapeDtypeStruct((L, N_GBLK * WB), F32),
            jax.ShapeDtypeStruct((L, N_GBLK * WB), F32),
            jax.ShapeDtypeStruct((L, D_SSM), F32),
        ],
        scratch_shapes=[pltpu.VMEM((SUBLANES, WB), F32), pltpu.VMEM((SUBLANES, WB), F32)],
    )


def _tail_fwd(x, ys, proj, w_glu, conv_w, g_ssm, g_conv, w_out, g_post, exchanges=()):
    L, D = x.shape
    tm = _tile(L, TM_TAIL)

    def body(x_ref, ys_ref, h_ref, bg_ref, cg_ref, wglu_ref, cw_ref, gs_ref, gc_ref, wout_ref, gp_ref,
             ycat_ref, o_ref, x1_ref, zbuf):
        @pl.when(pl.program_id(0) == 0)
        def _():
            zbuf[0:SUBLANES, :] = jnp.zeros((SUBLANES, D_CONV), F32)

        y1, _ = _gelu(ys_ref[...])
        y2 = y1 * jax.nn.sigmoid(_dot(y1.astype(MXU_DTYPE), wglu_ref[...]))
        ycat_ref[:, :D_SSM] = _rms_fwd(y2, gs_ref[...]).astype(MXU_DTYPE)
        z = cg_ref[...] * h_ref[...]
        zbuf[SUBLANES:, :] = z
        conv = cw_ref[0:1, :] * zbuf[SUBLANES - 2:SUBLANES - 2 + tm, :] + cw_ref[1:2, :] * zbuf[SUBLANES - 1:SUBLANES - 1 + tm, :] + cw_ref[2:3, :] * z
        zbuf[0:SUBLANES, :] = zbuf[tm:tm + SUBLANES, :]
        ycat_ref[:, D_SSM:] = _rms_fwd(bg_ref[...] * conv, gc_ref[...]).astype(MXU_DTYPE)
        o = _dot(ycat_ref[...], wout_ref[...])
        o_ref[...] = o
        x1_ref[...] = x_ref[...] + _rms_fwd(o, gp_ref[...])

    row = lambda i: (i, 0)
    return _call(
        body, name="tail_fwd", grid=(L // tm,), exchanges=exchanges, semantics=("arbitrary",),
        operands=(x, ys, proj, proj, proj, w_glu, conv_w, g_ssm, g_conv, w_out, g_post),
        in_specs=[
            pl.BlockSpec((tm, D), row), pl.BlockSpec((tm, D_SSM), row),
            pl.BlockSpec((tm, D_CONV), lambda i: (i, 1)), pl.BlockSpec((tm, D_CONV), lambda i: (i, 2)),
            pl.BlockSpec((tm, D_CONV), lambda i: (i, 3)),
            _resident(w_glu.shape), _resident(conv_w.shape), _resident(g_ssm.shape), _resident(g_conv.shape),
            _resident(w_out.shape), _resident(g_post.shape),
        ],
        out_specs=[pl.BlockSpec((tm, D), row), pl.BlockSpec((tm, D), row), pl.BlockSpec((tm, D), row)],
        out_shape=[jax.ShapeDtypeStruct((L, D), MXU_DTYPE), jax.ShapeDtypeStruct((L, D), F32), jax.ShapeDtypeStruct((L, D), F32)],
        scratch_shapes=[pltpu.VMEM((tm + SUBLANES, D_CONV), F32)],
    )


def _mlp_fwd(x1, target, w_up_all, w_down, g_pre, g_post):
    L, D = x1.shape
    ns, _, fc = w_up_all.shape
    tm = _tile(L, TM_MLP)

    def body(x1_ref, t_ref, wup_ref, wdn_ref, gpre_ref, gpost_ref, hn2_ref, up_ref, m_ref, dx2_ref, loss_ref):
        @pl.when(pl.program_id(0) == 0)
        def _():
            loss_ref[...] = jnp.zeros_like(loss_ref)

        x1v = x1_ref[...]
        hn2 = _rms_fwd(x1v, gpre_ref[...]).astype(MXU_DTYPE)
        hn2_ref[...] = hn2
        m = jnp.zeros((tm, D), F32)
        for j in range(ns):
            up = _dot(hn2, wup_ref[j])
            up_ref[:, j * fc:(j + 1) * fc] = up
            act = jnp.square(jnp.maximum(up, 0.0)).astype(MXU_DTYPE)
            m = m + _dot(act, wdn_ref[j * fc:(j + 1) * fc, :])
        m_ref[...] = m
        err = x1v + _rms_fwd(m, gpost_ref[...]) - t_ref[...]
        loss_ref[...] += 0.5 * jnp.sum(jnp.mean(err * err, axis=-1, keepdims=True))
        dx2_ref[...] = err * (1.0 / D)

    row = lambda i: (i, 0)
    return pl.pallas_call(
        body, name="mlp_fwd", grid=(L // tm,),
        in_specs=[pl.BlockSpec((tm, D), row), pl.BlockSpec((tm, D), row), _resident(w_up_all.shape), _resident(w_down.shape),
                  _resident(g_pre.shape), _resident(g_post.shape)],
        out_specs=[pl.BlockSpec((tm, D), row), pl.BlockSpec((tm, ns * fc), row), pl.BlockSpec((tm, D), row),
                   pl.BlockSpec((tm, D), row), pl.BlockSpec((SUBLANES, 128), lambda i: (0, 0))],
        out_shape=[jax.ShapeDtypeStruct((L, D), MXU_DTYPE), jax.ShapeDtypeStruct((L, ns * fc), F32), jax.ShapeDtypeStruct((L, D), F32),
                   jax.ShapeDtypeStruct((L, D), F32), jax.ShapeDtypeStruct((SUBLANES, 128), F32)],
        compiler_params=_params(("arbitrary",)),
    )(x1, target, w_up_all, w_down, g_pre, g_post)


def _mlp_bwd(dx2, m, up, x1, w_up_all, w_down, g_pre, g_post):
    L, D = x1.shape
    ns, _, fc = w_up_all.shape
    tm = _tile(L, TM_MLP)

    def body(dx2_ref, m_ref, up_ref, x1_ref, wup_ref, wdn_ref, gpre_ref, gpost_ref,
             dm_ref, dup_ref, act_ref, dx1_ref, dgpost_ref, dgpre_ref):
        @pl.when(pl.program_id(0) == 0)
        def _():
            dgpost_ref[...] = jnp.zeros_like(dgpost_ref)
            dgpre_ref[...] = jnp.zeros_like(dgpre_ref)

        dx2v = dx2_ref[...]
        dm, dg = _rms_bwd(m_ref[...], gpost_ref[...], dx2v)
        dgpost_ref[...] += dg
        dm_b = dm.astype(MXU_DTYPE)
        dm_ref[...] = dm_b
        dhn2 = jnp.zeros((tm, D), F32)
        for j in range(ns):
            cols = slice(j * fc, (j + 1) * fc)
            relu = jnp.maximum(up_ref[:, cols], 0.0)
            act_ref[:, cols] = jnp.square(relu).astype(MXU_DTYPE)
            dup = (_dot_nt(dm_b, wdn_ref[cols, :]) * (2.0 * relu)).astype(MXU_DTYPE)
            dup_ref[:, cols] = dup
            dhn2 = dhn2 + _dot_nt(dup, wup_ref[j])
        dx, dg = _rms_bwd(x1_ref[...], gpre_ref[...], dhn2)
        dgpre_ref[...] += dg
        dx1_ref[...] = dx2v + dx

    row = lambda i: (i, 0)
    vec = pl.BlockSpec((1, D), lambda i: (0, 0))
    return pl.pallas_call(
        body, name="mlp_bwd", grid=(L // tm,),
        in_specs=[pl.BlockSpec((tm, D), row), pl.BlockSpec((tm, D), row), pl.BlockSpec((tm, ns * fc), row), pl.BlockSpec((tm, D), row),
                  _resident(w_up_all.shape), _resident(w_down.shape), _resident(g_pre.shape), _resident(g_post.shape)],
        out_specs=[pl.BlockSpec((tm, D), row), pl.BlockSpec((tm, ns * fc), row), pl.BlockSpec((tm, ns * fc), row),
                   pl.BlockSpec((tm, D), row), vec, vec],
        out_shape=[jax.ShapeDtypeStruct((L, D), MXU_DTYPE), jax.ShapeDtypeStruct((L, ns * fc), MXU_DTYPE),
                   jax.ShapeDtypeStruct((L, ns * fc), MXU_DTYPE), jax.ShapeDtypeStruct((L, D), F32),
                   jax.ShapeDtypeStruct((1, D), F32), jax.ShapeDtypeStruct((1, D), F32)],
        compiler_params=_params(("arbitrary",)),
    )(dx2, m, up, x1, w_up_all, w_down, g_pre, g_post)


def _tail_bwd(dx1, o, ys, proj, w_glu, conv_w, g_ssm, g_conv, w_out, g_post, exchanges=()):
    L, D = dx1.shape
    tm = _tile(L, TM_TAIL)
    nt = L // tm
    hb = tm // SUBLANES

    def body(dx1_ref, o_ref, ys_ref, h_ref, bg_ref, cg_ref, hh_ref, hcg_ref, wglu_ref, cw_ref, gs_ref, gc_ref, wout_ref, gp_ref,
             do_ref, da_ref, y1_ref, dys_ref, dhbc_ref, dgp_ref, dgs_ref, dgc_ref, dcw_ref, zbuf, dcbuf):
        step = pl.program_id(0)

        @pl.when(step == 0)
        def _():
            dcbuf[tm:, :] = jnp.zeros((SUBLANES, D_CONV), F32)
            dgp_ref[...] = jnp.zeros_like(dgp_ref)
            dgs_ref[...] = jnp.zeros_like(dgs_ref)
            dgc_ref[...] = jnp.zeros_like(dgc_ref)
            dcw_ref[...] = jnp.zeros_like(dcw_ref)

        do, dg = _rms_bwd(o_ref[...], gp_ref[...], dx1_ref[...])
        dgp_ref[...] += dg
        do_b = do.astype(MXU_DTYPE)
        do_ref[...] = do_b
        dycat = _dot_nt(do_b, wout_ref[...])
        y1, dgelu = _gelu(ys_ref[...])
        y1_b = y1.astype(MXU_DTYPE)
        y1_ref[...] = y1_b
        s = jax.nn.sigmoid(_dot(y1_b, wglu_ref[...]))
        dy2, dg = _rms_bwd(y1 * s, gs_ref[...], dycat[:, :D_SSM])
        dgs_ref[...] += dg
        da_b = (dy2 * y1 * s * (1.0 - s)).astype(MXU_DTYPE)
        da_ref[...] = da_b
        dys_ref[...] = (dy2 * s + _dot_nt(da_b, wglu_ref[...])) * dgelu
        h = h_ref[...]
        cg = cg_ref[...]
        bg = bg_ref[...]
        z = cg * h
        first = step == nt - 1
        zbuf[0:SUBLANES, :] = jnp.where(first, 0.0, hcg_ref[...] * hh_ref[...])
        zbuf[SUBLANES:, :] = z
        z1 = zbuf[SUBLANES - 1:SUBLANES - 1 + tm, :]
        z2 = zbuf[SUBLANES - 2:SUBLANES - 2 + tm, :]
        conv = cw_ref[0:1, :] * z2 + cw_ref[1:2, :] * z1 + cw_ref[2:3, :] * z
        dyc, dg = _rms_bwd(bg * conv, gc_ref[...], dycat[:, D_SSM:])
        dgc_ref[...] += dg
        dconv = dyc * bg
        dcw_ref[0:1, :] += jnp.sum(dconv * z2, axis=0, keepdims=True)
        dcw_ref[1:2, :] += jnp.sum(dconv * z1, axis=0, keepdims=True)
        dcw_ref[2:3, :] += jnp.sum(dconv * z, axis=0, keepdims=True)
        dcbuf[0:tm, :] = dconv
        dz = cw_ref[2:3, :] * dconv + cw_ref[1:2, :] * dcbuf[1:1 + tm, :] + cw_ref[0:1, :] * dcbuf[2:2 + tm, :]
        dcbuf[tm:, :] = dcbuf[0:SUBLANES, :]
        dhbc_ref[:, 0:D_CONV] = (dz * cg).astype(MXU_DTYPE)
        dhbc_ref[:, D_CONV:2 * D_CONV] = (dyc * conv).astype(MXU_DTYPE)
        dhbc_ref[:, 2 * D_CONV:] = (dz * h).astype(MXU_DTYPE)

    rev = lambda i: (nt - 1 - i, 0)
    col = lambda c: (lambda i: (nt - 1 - i, c))
    halo = lambda c: (lambda i: (jnp.maximum((nt - 1 - i) * hb - 1, 0), c))
    vec = lambda n: pl.BlockSpec((1, n), lambda i: (0, 0))
    return _call(
        body, name="tail_bwd", grid=(nt,), exchanges=exchanges, semantics=("arbitrary",),
        operands=(dx1, o, ys, proj, proj, proj, proj, proj, w_glu, conv_w, g_ssm, g_conv, w_out, g_post),
        in_specs=[
            pl.BlockSpec((tm, D), rev), pl.BlockSpec((tm, D), rev), pl.BlockSpec((tm, D_SSM), rev),
            pl.BlockSpec((tm, D_CONV), col(1)), pl.BlockSpec((tm, D_CONV), col(2)), pl.BlockSpec((tm, D_CONV), col(3)),
            pl.BlockSpec((SUBLANES, D_CONV), halo(1)), pl.BlockSpec((SUBLANES, D_CONV), halo(3)),
            _resident(w_glu.shape), _resident(conv_w.shape), _resident(g_ssm.shape), _resident(g_conv.shape),
            _resident(w_out.shape), _resident(g_post.shape),
        ],
        out_specs=[
            pl.BlockSpec((tm, D), rev), pl.BlockSpec((tm, D_SSM), rev), pl.BlockSpec((tm, D_SSM), rev), pl.BlockSpec((tm, D_SSM), rev),
            pl.BlockSpec((tm, 3 * D_CONV), rev), vec(D), vec(D_SSM), vec(D_CONV),
            pl.BlockSpec((SUBLANES, D_CONV), lambda i: (0, 0)),
        ],
        out_shape=[
            jax.ShapeDtypeStruct((L, D), MXU_DTYPE), jax.ShapeDtypeStruct((L, D_SSM), MXU_DTYPE), jax.ShapeDtypeStruct((L, D_SSM), MXU_DTYPE),
            jax.ShapeDtypeStruct((L, D_SSM), F32), jax.ShapeDtypeStruct((L, 3 * D_CONV), MXU_DTYPE),
            jax.ShapeDtypeStruct((1, D), F32), jax.ShapeDtypeStruct((1, D_SSM), F32), jax.ShapeDtypeStruct((1, D_CONV), F32),
            jax.ShapeDtypeStruct((SUBLANES, D_CONV), F32),
        ],
        scratch_shapes=[pltpu.VMEM((tm + SUBLANES, D_CONV), F32), pltpu.VMEM((tm + SUBLANES, D_CONV), F32)],
    )


def _s5_bwd(dys, proj, s_re, s_im, bmat, cmat, coef_rev, dskip, exchanges=()):
    L = dys.shape[0]
    tm = _tile(L, TM_S5)
    nt = L // tm
    nrt = tm // SUBLANES
    lc = min(LANE_CHUNK, WB)

    def body(dys_ref, u_ref, sre_ref, sim_ref, bm_ref, cm_ref, coef_ref, d_ref,
             du_ref, gb_ref, gc_ref, q_ref, gd_ref, lr_ref, li_ref, hr_ref, hi_ref, qr_acc, qi_acc):
        step = pl.program_id(1)

        @pl.when(step == 0)
        def _():
            for ref in (hr_ref, hi_ref, qr_acc, qi_acc, gb_ref, gc_ref, gd_ref):
                ref[...] = jnp.zeros_like(ref)

        dys_v = dys_ref[...]
        u = u_ref[...]
        dys_b = dys_v.astype(MXU_DTYPE)
        u_b = u.astype(MXU_DTYPE)
        d = _dot_nt(dys_b, cm_ref[0])
        lr_ref[...] = d[:, :WB]
        li_ref[...] = d[:, WB:]
        for c in range(WB // lc):
            lanes = slice(c * lc, (c + 1) * lc)

            def row_body(k, carry, lanes=lanes):
                hr, hi = carry
                rows = pl.ds(pl.multiple_of((nrt - 1 - k) * SUBLANES, SUBLANES), SUBLANES)
                dr = lr_ref[rows, lanes]
                di = li_ref[rows, lanes]
                xr, xi = _scan_tile(dr, di, hr, hi, coef_ref, lanes, True)
                lr_ref[rows, lanes] = xr
                li_ref[rows, lanes] = xi
                er = xr - dr
                ei = xi - di
                sr = sre_ref[rows, lanes]
                si = sim_ref[rows, lanes]
                qr_acc[:, lanes] += er * sr + ei * si
                qi_acc[:, lanes] += ei * sr - er * si
                return (jnp.broadcast_to(xr[0:1, :], xr.shape), jnp.broadcast_to(xi[0:1, :], xi.shape))

            hr, hi = lax.fori_loop(0, nrt, row_body, (hr_ref[:, lanes], hi_ref[:, lanes]), unroll=True)
            hr_ref[:, lanes] = hr
            hi_ref[:, lanes] = hi
        lr_b = lr_ref[...].astype(MXU_DTYPE)
        li_b = li_ref[...].astype(MXU_DTYPE)
        du_ref[...] = _dot_nt(lr_b, bm_ref[0, :, :WB]) + _dot_nt(li_b, bm_ref[0, :, WB:]) + d_ref[0] * dys_v
        gb_ref[0, :WB, :] += _dot_tn(lr_b, u_b)
        gb_ref[0, WB:, :] += _dot_tn(li_b, u_b)
        gc_ref[0, :, :WB] += _dot_tn(dys_b, sre_ref[...].astype(MXU_DTYPE))
        gc_ref[0, :, WB:] += _dot_tn(dys_b, sim_ref[...].astype(MXU_DTYPE))
        gd_ref[0] += jnp.sum(dys_v * u, axis=0, keepdims=True)

        @pl.when(step == nt - 1)
        def _():
            q_ref[0, 0:1, :] = jnp.sum(qr_acc[...], axis=0, keepdims=True)
            q_ref[0, 1:2, :] = jnp.sum(qi_acc[...], axis=0, keepdims=True)

    rev = lambda b, i: (nt - 1 - i, b)
    blk = lambda b, i: (b, 0, 0)
    return _call(
        body, name="s5_bwd", grid=(N_GBLK, nt), exchanges=exchanges, semantics=("arbitrary", "arbitrary"),
        operands=(dys, proj, s_re, s_im, bmat, cmat, coef_rev, dskip),
        in_specs=[
            pl.BlockSpec((tm, UB), rev), pl.BlockSpec((tm, UB), rev), pl.BlockSpec((tm, WB), rev), pl.BlockSpec((tm, WB), rev),
            pl.BlockSpec((1, UB, 2 * WB), blk), pl.BlockSpec((1, 2 * WB, UB), blk),
            pl.BlockSpec((1, 8, SUBLANES, WB), lambda b, i: (b, 0, 0, 0)), pl.BlockSpec((1, 1, UB), blk),
        ],
        out_specs=[
            pl.BlockSpec((tm, UB), rev), pl.BlockSpec((1, 2 * WB, UB), blk), pl.BlockSpec((1, UB, 2 * WB), blk),
            pl.BlockSpec((1, 2, WB), blk), pl.BlockSpec((1, 1, UB), blk),
        ],
        out_shape=[
            jax.ShapeDtypeStruct((L, D_SSM), F32), jax.ShapeDtypeStruct((N_GBLK, 2 * WB, UB), F32),
            jax.ShapeDtypeStruct((N_GBLK, UB, 2 * WB), F32), jax.ShapeDtypeStruct((N_GBLK, 2, WB), F32),
            jax.ShapeDtypeStruct((N_GBLK, 1, UB), F32),
        ],
        scratch_shapes=[pltpu.VMEM((tm, WB), F32), pltpu.VMEM((tm, WB), F32), pltpu.VMEM((SUBLANES, WB), F32),
                        pltpu.VMEM((SUBLANES, WB), F32), pltpu.VMEM((SUBLANES, WB), F32), pltpu.VMEM((SUBLANES, WB), F32)],
    )


def _inproj_bwd(du, dhbc, x, dx1, w_in_all, g1):
    L, D = x.shape
    ns, _, nc = w_in_all.shape
    tm = _tile(L, TM_PROJ)

    def body(du_ref, dhbc_ref, x_ref, dx1_ref, w_ref, g_ref, gx_ref, dproj_ref, dg_ref):
        @pl.when(pl.program_id(0) == 0)
        def _():
            dg_ref[...] = jnp.zeros_like(dg_ref)

        du_b = du_ref[...].astype(MXU_DTYPE)
        dproj_ref[:, :nc] = du_b
        dproj_ref[:, nc:] = dhbc_ref[...]
        dhn = _dot_nt(du_b, w_ref[0])
        for j in range(1, ns):
            dhn = dhn + _dot_nt(dhbc_ref[:, (j - 1) * nc:j * nc], w_ref[j])
        dx, dg = _rms_bwd(x_ref[...], g_ref[...], dhn)
        dg_ref[...] += dg
        gx_ref[...] = dx1_ref[...] + dx

    row = lambda i: (i, 0)
    return pl.pallas_call(
        body, name="inproj_bwd", grid=(L // tm,),
        in_specs=[pl.BlockSpec((tm, nc), row), pl.BlockSpec((tm, (ns - 1) * nc), row), pl.BlockSpec((tm, D), row), pl.BlockSpec((tm, D), row),
                  _resident(w_in_all.shape), _resident(g1.shape)],
        out_specs=[pl.BlockSpec((tm, D), row), pl.BlockSpec((tm, ns * nc), row), pl.BlockSpec((1, D), lambda i: (0, 0))],
        out_shape=[jax.ShapeDtypeStruct((L, D), F32), jax.ShapeDtypeStruct((L, ns * nc), MXU_DTYPE), jax.ShapeDtypeStruct((1, D), F32)],
        compiler_params=_params(("arbitrary",)),
    )(du, dhbc, x, dx1, w_in_all, g1)


def _matmul_tn(a, b, name, col_shards=1, exchanges=()):
    L, K = a.shape
    N = b.shape[1]
    tl = _tile(L, TL_TN)
    tk = _tile(K, 1024)
    nw = N // col_shards
    tn = _tile(nw, 1024)
    npb = nw // tn

    def body(a_ref, b_ref, o_ref):
        @pl.when(pl.program_id(2) == 0)
        def _():
            o_ref[...] = jnp.zeros_like(o_ref)

        o_ref[0] += _dot_tn(a_ref[...], b_ref[...])

    return _call(
        body, name=name, grid=(K // tk, N // tn, L // tl), exchanges=exchanges, semantics=("arbitrary", "arbitrary", "arbitrary"),
        operands=(a, b),
        in_specs=[pl.BlockSpec((tl, tk), lambda k, n, l: (l, k)), pl.BlockSpec((tl, tn), lambda k, n, l: (l, n))],
        out_specs=[pl.BlockSpec((1, tk, tn), lambda k, n, l: (n // npb, k, n % npb))],
        out_shape=[jax.ShapeDtypeStruct((col_shards, K, nw), F32)],
    )


def _ssm_discretize(lam_re, lam_im, log_dt, b_re, b_im):
    dt = jnp.exp(log_dt)[:, None]
    zr = lam_re * dt
    zi = lam_im * dt
    mag = jnp.exp(zr)
    abr = mag * jnp.cos(zi)
    abi = mag * jnp.sin(zi)
    nr, ni = abr - 1.0, abi
    den = lam_re * lam_re + lam_im * lam_im
    coef_r = (nr * lam_re + ni * lam_im) / den
    coef_i = (ni * lam_re - nr * lam_im) / den
    bbar_r = coef_r[..., None] * b_re - coef_i[..., None] * b_im
    bbar_i = coef_r[..., None] * b_im + coef_i[..., None] * b_re
    return zr, zi, bbar_r, bbar_i


def _block_diag(t):
    nb, g, r, c = t.shape
    eye = jnp.eye(g, dtype=t.dtype)
    return (t[:, :, :, None, :] * eye[None, :, None, :, None]).reshape(nb, g * r, g * c)


def _diag_blocks(m, r, c):
    nb = m.shape[0]
    g = m.shape[1] // r
    eye = jnp.eye(g, dtype=m.dtype)
    t = (m.reshape(nb, g, r, g, c) * eye[None, :, None, :, None]).sum(axis=3)
    return t.reshape(nb * g, r, c)


def _scan_tables(ar, ai, reverse):
    def mul(p, q):
        return (p[0] * q[0] - p[1] * q[1], p[0] * q[1] + p[1] * q[0])

    a1 = (ar, -ai if reverse else ai)
    a2 = mul(a1, a1)
    a3 = mul(a2, a1)
    a4 = mul(a2, a2)
    pw = [a1, a2, a3, a4, mul(a4, a1), mul(a4, a2), mul(a4, a3), mul(a4, a4)]
    t = jnp.arange(SUBLANES)[:, None]
    tabs = []
    for k, ak in ((1, a1), (2, a2), (4, a4)):
        mask = (t + k <= SUBLANES - 1) if reverse else (t >= k)
        tabs += [jnp.where(mask, ak[0][None, :], 0.0), jnp.where(mask, ak[1][None, :], 0.0)]
    order = range(SUBLANES - 1, -1, -1) if reverse else range(SUBLANES)
    tabs += [jnp.stack([pw[j][0] for j in order]), jnp.stack([pw[j][1] for j in order])]
    coef = jnp.stack(tabs)
    return coef.reshape(8, SUBLANES, N_GBLK, WB).transpose(2, 0, 1, 3)


def _ssm_matrices(lam_re, lam_im, log_dt, b_re, b_im, c_re, c_im):
    zr, zi, bbar_r, bbar_i = _ssm_discretize(lam_re, lam_im, log_dt, b_re, b_im)
    mag = jnp.exp(zr)
    ar = (mag * jnp.cos(zi)).reshape(-1)
    ai = (mag * jnp.sin(zi)).reshape(-1)
    blk = lambda t: t.reshape(N_GBLK, G_PER_BLK, *t.shape[1:])
    bmat = jnp.concatenate([_block_diag(blk(bbar_r).transpose(0, 1, 3, 2)), _block_diag(blk(bbar_i).transpose(0, 1, 3, 2))], axis=2)
    cmat = jnp.concatenate([_block_diag(blk(c_re).transpose(0, 1, 3, 2)), _block_diag(blk(-c_im).transpose(0, 1, 3, 2))], axis=1)
    return bmat.astype(MXU_DTYPE), cmat.astype(MXU_DTYPE), _scan_tables(ar, ai, False), _scan_tables(ar, ai, True)


def _ssm_param_grads(lam_re, lam_im, log_dt, b_re, b_im, gb, gc, q, gd):
    gbr = _diag_blocks(gb[:, :WB, :], STATE, GROUP)
    gbi = _diag_blocks(gb[:, WB:, :], STATE, GROUP)
    d_c_re = _diag_blocks(gc[:, :, :WB], GROUP, STATE)
    d_c_im = -_diag_blocks(gc[:, :, WB:], GROUP, STATE)
    qr = q[:, 0, :].reshape(N_GROUPS, STATE)
    qi = q[:, 1, :].reshape(N_GROUPS, STATE)
    _, vjp = jax.vjp(_ssm_discretize, lam_re, lam_im, log_dt, b_re, b_im)
    d_lam_re, d_lam_im, d_log_dt, d_b_re, d_b_im = vjp((qr, qi, gbr, gbi))
    return d_lam_re, d_lam_im, d_log_dt, d_b_re, d_b_im, d_c_re, d_c_im, gd.reshape(N_GROUPS, GROUP)


def _row_tile(rows, n):
    return _tile(rows, max(SUBLANES, (2 * 1024 * 1024) // (4 * n)))


def _pair_add(grad, other, core, name):
    ns, h, n = other.shape
    tr = _row_tile(h, n)
    nb = h // tr

    def body(c_ref, g_ref, o_ref, out_ref):
        out_ref[...] = (g_ref[...] + o_ref[...]).astype(WIRE_DTYPE)

    return pl.pallas_call(
        body, name=name,
        grid_spec=pltpu.PrefetchScalarGridSpec(
            num_scalar_prefetch=1, grid=(ns, nb),
            in_specs=[pl.BlockSpec((1, tr, n), lambda s, i, c: (s, c[0] * nb + i, 0)), pl.BlockSpec((1, tr, n), lambda s, i, c: (s, i, 0))],
            out_specs=pl.BlockSpec((1, tr, n), lambda s, i, c: (s, i, 0))),
        out_shape=jax.ShapeDtypeStruct(other.shape, WIRE_DTYPE),
        compiler_params=_params(("arbitrary", "arbitrary")),
    )(core, grad, other)


def _quad_sum(parts, core, name):
    ns, h, n = parts.shape
    tr = _row_tile(h, n)
    nb = h // tr

    def body(c_ref, p_ref, out_ref):
        p = [p_ref[k].astype(F32) for k in range(ns)]
        out_ref[...] = ((p[0] + p[1]) + p[2]) + p[3]

    return pl.pallas_call(
        body, name=name,
        grid_spec=pltpu.PrefetchScalarGridSpec(
            num_scalar_prefetch=1, grid=(nb,),
            in_specs=[pl.BlockSpec((ns, tr, n), lambda i, c: (0, i, 0))],
            out_specs=pl.BlockSpec((tr, n), lambda i, c: (c[0] * nb + i, 0))),
        out_shape=jax.ShapeDtypeStruct((2 * h, n), F32),
        compiler_params=_params(("arbitrary",)),
    )(core, parts)


def _sum_devices(blocks):
    nd, m, n = blocks.shape

    def body(b_ref, out_ref):
        total = b_ref[0]
        for d in range(1, nd):
            total = total + b_ref[d]
        out_ref[...] = total

    return pl.pallas_call(body, name="sum_devices", in_specs=[VMEM], out_specs=VMEM,
                          out_shape=jax.ShapeDtypeStruct((m, n), blocks.dtype))(blocks)


def _adamw_math(w, g, m, v):
    m = ADAM_B1 * m + (1.0 - ADAM_B1) * g
    v = ADAM_B2 * v + (1.0 - ADAM_B2) * jnp.square(g)
    m_hat = m / (1.0 - ADAM_B1 ** ADAM_STEP)
    v_hat = v / (1.0 - ADAM_B2 ** ADAM_STEP)
    delta = -ADAM_LR * (m_hat / (jnp.sqrt(v_hat) + ADAM_EPS) + ADAM_WD * w)
    return delta, m, v


def _adamw(w, g, m, v, name):
    r, n = w.shape
    tr = _row_tile(r, n)

    def body(w_ref, g_ref, m_ref, v_ref, d_ref, nm_ref, nv_ref):
        d_ref[...], nm_ref[...], nv_ref[...] = _adamw_math(w_ref[...], g_ref[...], m_ref[...], v_ref[...])

    spec = pl.BlockSpec((tr, n), lambda i: (i, 0))
    return pl.pallas_call(
        body, name=name, grid=(r // tr,), in_specs=[spec] * 4, out_specs=[spec] * 3,
        out_shape=[jax.ShapeDtypeStruct((r, n), F32)] * 3,
        compiler_params=_params(("arbitrary",)),
    )(w, g, m, v)


SMALL_SHAPES = {
    "g_pre_mix": (D_MODEL,), "lam_re": (N_GROUPS, STATE), "lam_im": (N_GROUPS, STATE), "log_dt": (N_GROUPS,),
    "b_re": (N_GROUPS, STATE, GROUP), "b_im": (N_GROUPS, STATE, GROUP), "c_re": (N_GROUPS, GROUP, STATE), "c_im": (N_GROUPS, GROUP, STATE),
    "d_skip": (N_GROUPS, GROUP), "conv_w": (3, D_CONV), "g_ssm_out": (D_SSM,), "g_conv_out": (D_CONV,),
    "g_post_mix": (D_MODEL,), "g_pre_mlp": (D_MODEL,), "g_post_mlp": (D_MODEL,),
}
LANES = 128
PACK_TILE = SUBLANES * LANES


def _pack_rows(name):
    n = math.prod(SMALL_SHAPES[name])
    return SUBLANES * (-(-n // PACK_TILE))


def _pack_offsets():
    offs, row = {}, 0
    for name in SMALL_SHAPES:
        offs[name] = row
        row += _pack_rows(name)
    return offs, row


def _pack_small(grads):
    parts = []
    for name in SMALL_SHAPES:
        flat = grads[name].reshape(-1)
        parts.append(jnp.pad(flat, (0, _pack_rows(name) * LANES - flat.shape[0])).reshape(-1, LANES))
    return jnp.concatenate(parts, axis=0)


def _lane_shape(name):
    n = math.prod(SMALL_SHAPES[name])
    return (n // LANES, LANES) if n % LANES == 0 else (1, n)


def _adamw_small(pack, conv_grad, w, m, v):
    names = list(SMALL_SHAPES)
    offs, _ = _pack_offsets()
    nn = len(names)

    def body(*refs):
        pack_ref, cg_ref = refs[0], refs[1]
        w_refs, m_refs, v_refs = refs[2:2 + nn], refs[2 + nn:2 + 2 * nn], refs[2 + 2 * nn:2 + 3 * nn]
        outs = refs[2 + 3 * nn:]
        for j, name in enumerate(names):
            r, n = w_refs[j].shape
            g = cg_ref[...] if name == "conv_w" else pack_ref[offs[name]:offs[name] + r, 0:n]
            delta, nm, nv = _adamw_math(w_refs[j][...], g, m_refs[j][...], v_refs[j][...])
            outs[j][...] = g
            outs[nn + j][...] = delta
            outs[2 * nn + j][...] = nm
            outs[3 * nn + j][...] = nv

    args = [pack, conv_grad] + [w[k] for k in names] + [m[k] for k in names] + [v[k] for k in names]
    res = pl.pallas_call(
        body, name="adamw_small", in_specs=[VMEM] * len(args), out_specs=[VMEM] * (4 * nn),
        out_shape=[jax.ShapeDtypeStruct(w[k].shape, F32) for k in names] * 4,
    )(*args)
    return [dict(zip(names, res[q * nn:(q + 1) * nn])) for q in range(4)]


WEIGHTS = ["g_pre_mix", "w_in", "lam_re", "lam_im", "log_dt", "b_re", "b_im", "c_re", "c_im", "d_skip", "w_glu", "conv_w",
           "g_ssm_out", "g_conv_out", "w_out", "g_post_mix", "g_pre_mlp", "w_up", "w_down", "g_post_mlp"]
BIG = ["w_in", "w_glu", "w_out", "w_up", "w_down"]


def kernel(x, g_pre_mix, w_in, lam_re, lam_im, log_dt, b_re, b_im, c_re, c_im, d_skip, w_glu, conv_w, g_ssm_out, g_conv_out, w_out, g_post_mix, g_pre_mlp, w_up, w_down, g_post_mlp, loss_target, m_g_pre_mix, m_w_in, m_lam_re, m_lam_im, m_log_dt, m_b_re, m_b_im, m_c_re, m_c_im, m_d_skip, m_w_glu, m_conv_w, m_g_ssm_out, m_g_conv_out, m_w_out, m_g_post_mix, m_g_pre_mlp, m_w_up, m_w_down, m_g_post_mlp, v_g_pre_mix, v_w_in, v_lam_re, v_lam_im, v_log_dt, v_b_re, v_b_im, v_c_re, v_c_im, v_d_skip, v_w_glu, v_conv_w, v_g_ssm_out, v_g_conv_out, v_w_out, v_g_post_mix, v_g_pre_mlp, v_w_up, v_w_down, v_g_post_mlp):
    w = dict(g_pre_mix=g_pre_mix, w_in=w_in, lam_re=lam_re, lam_im=lam_im, log_dt=log_dt, b_re=b_re, b_im=b_im, c_re=c_re, c_im=c_im,
             d_skip=d_skip, w_glu=w_glu, conv_w=conv_w, g_ssm_out=g_ssm_out, g_conv_out=g_conv_out, w_out=w_out, g_post_mix=g_post_mix,
             g_pre_mlp=g_pre_mlp, w_up=w_up, w_down=w_down, g_post_mlp=g_post_mlp)
    m = dict(g_pre_mix=m_g_pre_mix, w_in=m_w_in, lam_re=m_lam_re, lam_im=m_lam_im, log_dt=m_log_dt, b_re=m_b_re, b_im=m_b_im, c_re=m_c_re,
             c_im=m_c_im, d_skip=m_d_skip, w_glu=m_w_glu, conv_w=m_conv_w, g_ssm_out=m_g_ssm_out, g_conv_out=m_g_conv_out, w_out=m_w_out,
             g_post_mix=m_g_post_mix, g_pre_mlp=m_g_pre_mlp, w_up=m_w_up, w_down=m_w_down, g_post_mlp=m_g_post_mlp)
    v = dict(g_pre_mix=v_g_pre_mix, w_in=v_w_in, lam_re=v_lam_re, lam_im=v_lam_im, log_dt=v_log_dt, b_re=v_b_re, b_im=v_b_im, c_re=v_c_re,
             c_im=v_c_im, d_skip=v_d_skip, w_glu=v_w_glu, conv_w=v_conv_w, g_ssm_out=v_g_ssm_out, g_conv_out=v_g_conv_out, w_out=v_w_out,
             g_post_mix=v_g_post_mix, g_pre_mlp=v_g_pre_mlp, w_up=v_w_up, w_down=v_w_down, g_post_mlp=v_g_post_mlp)
    shapes = {k: a.shape for k, a in w.items()}
    w, m, v = ({k: a[0] for k, a in d.items()} for d in (w, m, v))
    chip = 2 * lax.axis_index("x") + lax.axis_index("y")
    core = lax.axis_index("c").astype(jnp.int32).reshape(1)

    xs, target = x[0], loss_target[0]
    g1 = w["g_pre_mix"][None]
    g_ssm, g_conv = w["g_ssm_out"][None], w["g_conv_out"][None]
    g_post_mix, g_pre_mlp, g_post_mlp = w["g_post_mix"][None], w["g_pre_mlp"][None], w["g_post_mlp"][None]
    bmat, cmat, coef_f, coef_r = _ssm_matrices(w["lam_re"], w["lam_im"], w["log_dt"], w["b_re"], w["b_im"], w["c_re"], w["c_im"])
    dskip = w["d_skip"].reshape(N_GBLK, 1, UB)
    shard = {k: w[k].astype(MXU_DTYPE) for k in BIG}
    conv_pad = jnp.pad(w["conv_w"], ((0, SUBLANES - 3), (0, 0)))

    (w_in_all,) = _run_exchanges([_GatherForward([shard["w_in"]])], "ag_w_in")
    hn, proj = _inproj_fwd(xs, g1, w_in_all)
    rest = _Gather([shard["w_glu"], shard["w_out"], conv_pad, shard["w_up"], shard["w_down"]], [False, False, False, True, True])
    s_re, s_im, ys, w_glu_all, w_out_all, conv_all, w_up_all, w_down_all = _s5_fwd(proj, bmat, cmat, coef_f, dskip, exchanges=[rest])
    w_glu_f, w_out_f = w_glu_all.reshape(D_SSM, D_SSM), w_out_all.reshape(D_MODEL, D_MODEL)
    conv_f = jnp.transpose(conv_all, (1, 0, 2)).reshape(SUBLANES, D_CONV)
    ycat, o, x1, w_up_all, w_down_all = _tail_fwd(xs, ys, proj, w_glu_f, conv_f, g_ssm, g_conv, w_out_f, g_post_mix,
                                                  exchanges=[_Forward([w_up_all, w_down_all])])
    w_down_f = w_down_all.reshape(D_FF, D_MODEL)
    hn2, up, m_act, dx2, loss = _mlp_fwd(x1, target, w_up_all, w_down_f, g_pre_mlp, g_post_mlp)

    dm, dup, act, dx1, dg_post_mlp, dg_pre_mlp = _mlp_bwd(dx2, m_act, up, x1, w_up_all, w_down_f, g_pre_mlp, g_post_mlp)
    gw_down = _matmul_tn(act, dm, "dw_down")[0].reshape(N_CHIPS, D_FF // N_CHIPS, D_MODEL)
    gw_up = _matmul_tn(hn2, dup, "dw_up", col_shards=N_CHIPS)[0]
    do, da, y1, dys, dhbc, dg_post_mix, dg_ssm, dg_conv, dconv_w, o_down, o_up = _tail_bwd(
        dx1, o, ys, proj, w_glu_f, conv_f, g_ssm, g_conv, w_out_f, g_post_mix, exchanges=[_Pair([gw_down, gw_up])])
    p_down = _pair_add(gw_down, o_down, core, "pair_add_w_down")
    p_up = _pair_add(gw_up, o_up, core, "pair_add_w_up")
    gw_out = _matmul_tn(ycat, do, "dw_out")[0].reshape(N_CHIPS, D_MODEL // N_CHIPS, D_MODEL)
    gw_glu = _matmul_tn(y1, da, "dw_glu")[0].reshape(N_CHIPS, D_SSM // N_CHIPS, D_SSM)
    du, gb, gc, q, gd, q_down, q_up, o_out, o_glu = _s5_bwd(
        dys, proj, s_re, s_im, bmat, cmat, coef_r, dskip, exchanges=[_Chip([p_down, p_up]), _Pair([gw_out, gw_glu])])
    h_down = _quad_sum(q_down, core, "quad_sum_w_down")
    h_up = _quad_sum(q_up, core, "quad_sum_w_up")
    p_out = _pair_add(gw_out, o_out, core, "pair_add_w_out")
    p_glu = _pair_add(gw_glu, o_glu, core, "pair_add_w_glu")
    grad_x, dproj, dg_pre_mix = _inproj_bwd(du, dhbc, xs, dx1, w_in_all, g1)
    d_lam_re, d_lam_im, d_log_dt, d_b_re, d_b_im, d_c_re, d_c_im, d_d_skip = _ssm_param_grads(
        w["lam_re"], w["lam_im"], w["log_dt"], w["b_re"], w["b_im"], gb, gc, q, gd)
    small = {
        "g_pre_mix": dg_pre_mix[0], "lam_re": d_lam_re, "lam_im": d_lam_im, "log_dt": d_log_dt, "b_re": d_b_re, "b_im": d_b_im,
        "c_re": d_c_re, "c_im": d_c_im, "d_skip": d_d_skip, "conv_w": dconv_w[:3], "g_ssm_out": dg_ssm[0], "g_conv_out": dg_conv[0],
        "g_post_mix": dg_post_mix[0], "g_pre_mlp": dg_pre_mlp[0], "g_post_mlp": dg_post_mlp[0],
    }
    gw_in, g_down, g_up, q_out, q_glu, packs = _matmul_tn(
        hn, dproj, "dw_in", col_shards=N_CHIPS,
        exchanges=[_Share([h_down, h_up]), _Chip([p_out, p_glu]), _GatherSmall(jnp.concatenate([_pack_small(small), loss], axis=0))])
    h_out = _quad_sum(q_out, core, "quad_sum_w_out")
    h_glu = _quad_sum(q_glu, core, "quad_sum_w_glu")
    pack = _sum_devices(packs)
    loss = pack[pack.shape[0] - SUBLANES, 0]
    (o_in,) = _run_exchanges([_Pair([gw_in])], "rs_pair_w_in")
    p_in = _pair_add(gw_in, o_in, core, "pair_add_w_in")
    q_in, g_out, g_glu = _run_exchanges([_Chip([p_in]), _Share([h_out, h_glu])], "rs_chip_w_in")
    h_in = _quad_sum(q_in, core, "quad_sum_w_in")
    (g_in,) = _run_exchanges([_Share([h_in])], "rs_share_w_in")
    shard_grads = {"w_in": g_in, "w_glu": g_glu, "w_out": g_out, "w_up": g_up, "w_down": g_down}
    offs, _ = _pack_offsets()
    conv_grad = pack[offs["conv_w"]:offs["conv_w"] + 3 * D_CONV // LANES].reshape(3, D_CONV)
    conv_grad = lax.dynamic_slice(conv_grad, (0, chip * (D_CONV // N_CHIPS)), (3, D_CONV // N_CHIPS))

    out = {q: {} for q in ("grad", "delta", "new_m", "new_v")}
    for k in BIG:
        out["grad"][k] = shard_grads[k]
        out["delta"][k], out["new_m"][k], out["new_v"][k] = _adamw(w[k], shard_grads[k], m[k], v[k], "adamw_" + k)
    lane = lambda d: {k: (d[k] if k == "conv_w" else d[k].reshape(_lane_shape(k))) for k in SMALL_SHAPES}
    res = _adamw_small(pack, conv_grad, lane(w), lane(m), lane(v))
    for q, d in zip(("grad", "delta", "new_m", "new_v"), res):
        out[q].update(d)
    flat = [loss, grad_x[None]]
    for q in ("grad", "delta", "new_m", "new_v"):
        flat += [out[q][k].reshape(shapes[k]) for k in WEIGHTS]
    return tuple(flat)
```

```python
import functools
import math

import jax
import jax.numpy as jnp
from jax import lax
from jax.experimental import pallas as pl
from jax.experimental.pallas import tpu as pltpu

F32 = jnp.float32
MXU_DTYPE = jnp.bfloat16
WIRE_DTYPE = jnp.bfloat16

D_MODEL = 1024
D_SSM = 512
D_CONV = 512
N_GROUPS = 32
GROUP = 16
STATE = 64
D_FF = 4096
RMS_EPS = 1e-6
N_CHIPS = 4
N_DEV = 8

ADAM_LR = 0.001
ADAM_B1 = 0.9
ADAM_B2 = 0.999
ADAM_EPS = 1e-08
ADAM_WD = 0.01
ADAM_STEP = 10

N_GBLK = 2
G_PER_BLK = N_GROUPS // N_GBLK
UB = G_PER_BLK * GROUP
WB = G_PER_BLK * STATE
LANE_CHUNK = 256
SUBLANES = 8
N_TABLES = 24

TM_PROJ = 512
TM_S5 = 512
TM_TAIL = 256
TM_MLP = 256
TL_TN = 2048
VMEM_LIMIT = 56 * 1024 * 1024

MESH = pl.DeviceIdType.MESH


def _params(sem, vmem=VMEM_LIMIT):
    return pltpu.CompilerParams(dimension_semantics=sem, vmem_limit_bytes=vmem)


def _resident(shape):
    nd = len(shape)
    return pl.BlockSpec(shape, lambda *_: (0,) * nd, pipeline_mode=pl.Buffered(1))


def _dot(a, b):
    return jnp.dot(a, b, preferred_element_type=F32)


def _dot_nt(a, b):
    return lax.dot_general(a, b, (((1,), (1,)), ((), ())), preferred_element_type=F32)


def _dot_tn(a, b):
    return lax.dot_general(a, b, (((0,), (0,)), ((), ())), preferred_element_type=F32)


def _rms_fwd(x, g):
    r = lax.rsqrt(jnp.mean(x * x, axis=-1, keepdims=True) + RMS_EPS)
    return x * r * g


def _rms_bwd(x, g, dy):
    r = lax.rsqrt(jnp.mean(x * x, axis=-1, keepdims=True) + RMS_EPS)
    xn = x * r
    q = dy * g
    dx = r * (q - xn * jnp.mean(q * xn, axis=-1, keepdims=True))
    return dx, jnp.sum(dy * xn, axis=0, keepdims=True)


_GELU_C = math.sqrt(2.0 / math.pi)


def _gelu(x):
    t = jnp.tanh(_GELU_C * (x + 0.044715 * (x * x * x)))
    y = x * (0.5 * (1.0 + t))
    dy = 0.5 * (1.0 + t) + 0.5 * x * (1.0 - t * t) * (_GELU_C * (1.0 + 3 * 0.044715 * (x * x)))
    return y, dy


def _tile(n, pref):
    t = min(n, pref)
    assert n % t == 0, (n, t)
    return t


HBM = pl.BlockSpec(memory_space=pltpu.HBM)
VMEM = pl.BlockSpec(memory_space=pltpu.VMEM)
DMA_SEMS = pltpu.SemaphoreType.DMA


def _place():
    x, y, c = lax.axis_index("x"), lax.axis_index("y"), lax.axis_index("c")
    chips = [(1 - x, y), (x, 1 - y), (1 - x, 1 - y)]
    return (x, y, c), 2 * x + y, (x, y, 1 - c), chips, [2 * px + py for px, py in chips]


def _remote(src, dst, send_sem, recv_sem, device):
    return pltpu.make_async_remote_copy(src_ref=src, dst_ref=dst, send_sem=send_sem, recv_sem=recv_sem,
                                        device_id=device, device_id_type=MESH)


def _half(rows, c):
    return pl.ds(c * (rows // 2), rows // 2)


class _Exchange:
    aliases = {}

    def start(self, ins, outs, sems):
        local, outgoing, _ = self._copies(ins, outs, sems)
        for cp in local + outgoing:
            cp.start()

    def finish(self, ins, outs, sems):
        local, outgoing, incoming = self._copies(ins, outs, sems)
        for cp in incoming:
            cp.wait_recv()
        for cp in outgoing:
            cp.wait_send()
        for cp in local:
            cp.wait()


class _Gather(_Exchange):
    def __init__(self, shards, split):
        self.inputs, self.split = list(shards), split
        self.out_shape = [jax.ShapeDtypeStruct((N_CHIPS, *a.shape), a.dtype) for a in shards]
        self.sems = [DMA_SEMS((len(shards), 3)), DMA_SEMS((len(shards), 3)), DMA_SEMS((len(shards),))]

    def _copies(self, ins, outs, sems):
        send, recv, lsem = sems
        (x, y, c), me, sibling, chips, ids = _place()
        local = [pltpu.make_async_copy(ins[t], outs[t].at[me], lsem.at[t]) for t in range(len(ins))]
        outgoing, incoming = [], []
        for t, a in enumerate(self.inputs):
            rows = _half(a.shape[0], c) if self.split[t] else pl.ds(0, a.shape[0])
            for k in range(3):
                to = (*chips[k], c)
                outgoing.append(_remote(ins[t].at[rows, :], outs[t].at[me, rows, :], send.at[t, k], recv.at[t, k], to))
                incoming.append(_remote(ins[t].at[rows, :], outs[t].at[ids[k], rows, :], send.at[t, k], recv.at[t, k], to))
        return local, outgoing, incoming


class _Forward(_Exchange):
    def __init__(self, arrays):
        self.inputs = list(arrays)
        self.out_shape = [jax.ShapeDtypeStruct(a.shape, a.dtype) for a in arrays]
        self.aliases = {t: t for t in range(len(arrays))}
        self.sems = [DMA_SEMS((len(arrays), 3)), DMA_SEMS((len(arrays), 3))]

    def _copies(self, ins, outs, sems):
        send, recv = sems
        (x, y, c), me, sibling, chips, ids = _place()
        outgoing, incoming = [], []
        for t, a in enumerate(self.inputs):
            for k in range(3):
                mine = outs[t].at[ids[k], _half(a.shape[1], c), :]
                theirs = outs[t].at[ids[k], _half(a.shape[1], 1 - c), :]
                outgoing.append(_remote(mine, mine, send.at[t, k], recv.at[t, k], sibling))
                incoming.append(_remote(theirs, theirs, send.at[t, k], recv.at[t, k], sibling))
        return [], outgoing, incoming


class _GatherForward(_Exchange):
    def __init__(self, shards):
        self.gather = _Gather(shards, [True] * len(shards))
        self.forward = _Forward(self.gather.out_shape)
        self.inputs, self.out_shape = self.gather.inputs, self.gather.out_shape
        self.sems = self.gather.sems + self.forward.sems

    def start(self, ins, outs, sems):
        self.gather.start(ins, outs, sems[:3])

    def finish(self, ins, outs, sems):
        local, outgoing, incoming = self.gather._copies(ins, outs, sems[:3])
        _, passed, from_sibling = self.forward._copies(outs, outs, sems[3:])
        for landed, onward in zip(incoming, passed):
            landed.wait_recv()
            onward.start()
        for cp in from_sibling:
            cp.wait_recv()
        for cp in outgoing + passed:
            cp.wait_send()
        for cp in local:
            cp.wait()


class _Pair(_Exchange):
    def __init__(self, grads):
        self.inputs = list(grads)
        self.out_shape = [jax.ShapeDtypeStruct((g.shape[0], g.shape[1] // 2, g.shape[2]), g.dtype) for g in grads]
        self.sems = [DMA_SEMS((len(grads),)), DMA_SEMS((len(grads),))]

    def _copies(self, ins, outs, sems):
        send, recv = sems
        (x, y, c), me, sibling, chips, ids = _place()
        cps = [_remote(ins[t].at[:, _half(g.shape[1], 1 - c), :], outs[t], send.at[t], recv.at[t], sibling)
               for t, g in enumerate(self.inputs)]
        return [], cps, cps


class _Chip(_Exchange):
    def __init__(self, parts):
        self.inputs = list(parts)
        self.out_shape = [jax.ShapeDtypeStruct(p.shape, p.dtype) for p in parts]
        self.sems = [DMA_SEMS((len(parts), 3)), DMA_SEMS((len(parts), 3)), DMA_SEMS((len(parts),))]

    def _copies(self, ins, outs, sems):
        send, recv, lsem = sems
        (x, y, c), me, sibling, chips, ids = _place()
        local = [pltpu.make_async_copy(ins[t].at[me], outs[t].at[me], lsem.at[t]) for t in range(len(ins))]
        outgoing, incoming = [], []
        for t in range(len(ins)):
            for k in range(3):
                to = (*chips[k], c)
                outgoing.append(_remote(ins[t].at[ids[k]], outs[t].at[me], send.at[t, k], recv.at[t, k], to))
                incoming.append(_remote(ins[t].at[ids[k]], outs[t].at[ids[k]], send.at[t, k], recv.at[t, k], to))
        return local, outgoing, incoming


class _Share(_Exchange):
    def __init__(self, grads):
        self.inputs = list(grads)
        self.out_shape = [jax.ShapeDtypeStruct(g.shape, g.dtype) for g in grads]
        self.aliases = {t: t for t in range(len(grads))}
        self.sems = [DMA_SEMS((len(grads),)), DMA_SEMS((len(grads),))]

    def _copies(self, ins, outs, sems):
        send, recv = sems
        (x, y, c), me, sibling, chips, ids = _place()
        outgoing, incoming = [], []
        for t, g in enumerate(self.inputs):
            mine = outs[t].at[_half(g.shape[0], c), :]
            theirs = outs[t].at[_half(g.shape[0], 1 - c), :]
            outgoing.append(_remote(mine, mine, send.at[t], recv.at[t], sibling))
            incoming.append(_remote(theirs, theirs, send.at[t], recv.at[t], sibling))
        return [], outgoing, incoming


class _GatherSmall(_Exchange):
    def __init__(self, block):
        self.inputs = [block]
        self.out_shape = [jax.ShapeDtypeStruct((N_DEV, *block.shape), block.dtype)]
        self.sems = [DMA_SEMS((7,)), DMA_SEMS((7,)), DMA_SEMS(())]

    def _copies(self, ins, outs, sems):
        send, recv, lsem = sems
        (x, y, c), me, sibling, chips, ids = _place()
        slot = lambda px, py, pc: outs[0].at[4 * px + 2 * py + pc]

        def copy(k, block, to, src=None):
            return _remote(slot(*block) if src is None else src, slot(*block), send.at[k], recv.at[k], to)

        local = [pltpu.make_async_copy(ins[0], slot(x, y, c), lsem)]
        first = [copy(0, (x, y, c), sibling, src=ins[0])] + [copy(1 + j, (x, y, c), (*chip, c), src=ins[0]) for j, chip in enumerate(chips)]
        passed = [copy(4 + j, (*chip, c), sibling) for j, chip in enumerate(chips)]
        landed = [copy(1 + j, (*chip, c), (x, y, c)) for j, chip in enumerate(chips)]
        from_sibling = [copy(0, (x, y, 1 - c), (x, y, c))] + [copy(4 + j, (*chip, 1 - c), (x, y, c)) for j, chip in enumerate(chips)]
        return local, first, (passed, landed, from_sibling)

    def finish(self, ins, outs, sems):
        local, first, (passed, landed, from_sibling) = self._copies(ins, outs, sems)
        for j in range(3):
            landed[j].wait_recv()
            passed[j].start()
        for cp in from_sibling:
            cp.wait_recv()
        for cp in first + passed:
            cp.wait_send()
        for cp in local:
            cp.wait()


def _split_refs(refs, counts):
    out = []
    for n in counts:
        out.append(refs[:n])
        refs = refs[n:]
    return out


def _each_exchange(exchanges, method, x_in, x_out, x_sem):
    for ex in exchanges:
        ni, no, ns = len(ex.inputs), len(ex.out_shape), len(ex.sems)
        getattr(ex, method)(x_in[:ni], x_out[:no], x_sem[:ns])
        x_in, x_out, x_sem = x_in[ni:], x_out[no:], x_sem[ns:]


def _call(body, *, name, grid, in_specs, out_specs, out_shape, operands, semantics, scratch_shapes=(), exchanges=()):
    x_in = [a for ex in exchanges for a in ex.inputs]
    x_out = [s for ex in exchanges for s in ex.out_shape]
    x_sem = [s for ex in exchanges for s in ex.sems]
    counts = (len(in_specs), len(x_in), len(out_specs), len(x_out), len(scratch_shapes), len(x_sem))
    aliases, i0, o0 = {}, len(in_specs), len(out_specs)
    for ex in exchanges:
        aliases.update({i0 + i: o0 + o for i, o in ex.aliases.items()})
        i0, o0 = i0 + len(ex.inputs), o0 + len(ex.out_shape)

    def full_body(*refs):
        ins, xi, outs, xo, scr, xs = _split_refs(list(refs), counts)
        if exchanges:
            @pl.when(functools.reduce(jnp.logical_and, [pl.program_id(a) == 0 for a in range(len(grid))]))
            def _():
                _each_exchange(exchanges, "start", xi, xo, xs)

        body(*ins, *outs, *scr)
        if exchanges:
            @pl.when(functools.reduce(jnp.logical_and, [pl.program_id(a) == grid[a] - 1 for a in range(len(grid))]))
            def _():
                _each_exchange(exchanges, "finish", xi, xo, xs)

    return pl.pallas_call(
        full_body, name=name, grid=grid,
        in_specs=list(in_specs) + [HBM] * len(x_in), out_specs=list(out_specs) + [HBM] * len(x_out),
        out_shape=list(out_shape) + x_out, scratch_shapes=list(scratch_shapes) + x_sem,
        input_output_aliases=aliases, compiler_params=_params(semantics),
    )(*operands, *x_in)


def _run_exchanges(exchanges, name):
    x_in = [a for ex in exchanges for a in ex.inputs]
    x_out = [s for ex in exchanges for s in ex.out_shape]
    x_sem = [s for ex in exchanges for s in ex.sems]
    aliases, i0, o0 = {}, 0, 0
    for ex in exchanges:
        aliases.update({i0 + i: o0 + o for i, o in ex.aliases.items()})
        i0, o0 = i0 + len(ex.inputs), o0 + len(ex.out_shape)

    def body(*refs):
        xi, xo, xs = _split_refs(list(refs), (len(x_in), len(x_out), len(x_sem)))
        _each_exchange(exchanges, "start", xi, xo, xs)
        _each_exchange(exchanges, "finish", xi, xo, xs)

    return pl.pallas_call(
        body, name=name, in_specs=[HBM] * len(x_in), out_specs=[HBM] * len(x_out), out_shape=x_out,
        scratch_shapes=x_sem, input_output_aliases=aliases,
    )(*x_in)


def _inproj_fwd(x, g1, w_in_all):
    L, D = x.shape
    ns, _, nc = w_in_all.shape
    tm = _tile(L, TM_PROJ)

    def body(x_ref, g_ref, w_ref, hn_ref, proj_ref, u_ref):
        hn = _rms_fwd(x_ref[...], g_ref[...]).astype(MXU_DTYPE)
        hn_ref[...] = hn
        for j in range(ns):
            proj_ref[:, j * nc:(j + 1) * nc] = _dot(hn, w_ref[j])
        _store_slabs(u_ref, proj_ref[:, 0:nc])

    return pl.pallas_call(
        body, name="inproj_fwd", grid=(L // tm,),
        in_specs=[pl.BlockSpec((tm, D), lambda i: (i, 0)), _resident((1, D)), _resident(w_in_all.shape)],
        out_specs=[pl.BlockSpec((tm, D), lambda i: (i, 0)), pl.BlockSpec((tm, ns * nc), lambda i: (i, 0)), _slab_spec(nc, tm)],
        out_shape=[jax.ShapeDtypeStruct((L, D), MXU_DTYPE), jax.ShapeDtypeStruct((L, ns * nc), F32), _slab_shape(L, nc)],
        compiler_params=_params(("arbitrary",)),
    )(x, g1, w_in_all)


def _slab_shape(L, n):
    return jax.ShapeDtypeStruct((n // LANES, L, LANES), F32)


def _slab_spec(n, tm, index=lambda i: (0, i, 0)):
    return pl.BlockSpec((n // LANES, tm, LANES), index)


def _store_slabs(ref, value):
    for k in range(ref.shape[0]):
        ref[k] = value[:, k * LANES:(k + 1) * LANES]


def _load_slabs(ref):
    return jnp.concatenate([ref[k] for k in range(ref.shape[0])], axis=1)


SEG_ROWS = SUBLANES * SUBLANES


def _load_permuted(ref):
    tm = ref.shape[1]
    slabs = []
    for k in range(ref.shape[0]):
        tiles = [ref.at[k][pl.ds(b * SEG_ROWS + j, SUBLANES, stride=SUBLANES), :] for b in range(tm // SEG_ROWS) for j in range(SUBLANES)]
        slabs.append(jnp.concatenate(tiles, axis=0))
    return jnp.concatenate(slabs, axis=1)


def _store_permuted(ref, value):
    tm = ref.shape[1]
    for k in range(ref.shape[0]):
        for b in range(tm // SEG_ROWS):
            for j in range(SUBLANES):
                r = b * SEG_ROWS + j * SUBLANES
                ref.at[k][pl.ds(b * SEG_ROWS + j, SUBLANES, stride=SUBLANES), :] = value[r:r + SUBLANES, k * LANES:(k + 1) * LANES]


def _scan_tile(xr, xi, hr, hi, coef_ref, lanes, reverse):
    for k, j in ((1, 0), (2, 2), (4, 4)):
        ar = coef_ref[0, j, :, lanes]
        ai = coef_ref[0, j + 1, :, lanes]
        shift = SUBLANES - k if reverse else k
        sr = pltpu.roll(xr, shift, 0)
        si = pltpu.roll(xi, shift, 0)
        xr, xi = xr + (ar * sr - ai * si), xi + (ar * si + ai * sr)
    pr = coef_ref[0, 6, :, lanes]
    pi = coef_ref[0, 7, :, lanes]
    return xr + (pr * hr - pi * hi), xi + (pr * hi + pi * hr)


def _scan_block(read, write, hr, hi, coef_ref, lanes, reverse):
    order = list(range(SUBLANES - 1, -1, -1) if reverse else range(SUBLANES))
    near = 8 + 2 * order[0]
    ar = coef_ref[0, near, :, lanes]
    ai = coef_ref[0, near + 1, :, lanes]
    xr, xi = read(order[0])
    local = {order[0]: (xr, xi)}
    for j in order[1:]:
        br, bi = read(j)
        xr, xi = br + (ar * xr - ai * xi), bi + (ar * xi + ai * xr)
        local[j] = (xr, xi)
    er, ei = _scan_tile(xr, xi, hr, hi, coef_ref, lanes, reverse)
    edge = lax.broadcasted_iota(jnp.int32, er.shape, 0) == (SUBLANES - 1 if reverse else 0)
    shift = SUBLANES - 1 if reverse else 1
    pr = jnp.where(edge, hr, pltpu.roll(er, shift, 0))
    pi = jnp.where(edge, hi, pltpu.roll(ei, shift, 0))
    for j in range(SUBLANES):
        cr = coef_ref[0, 8 + 2 * j, :, lanes]
        ci = coef_ref[0, 9 + 2 * j, :, lanes]
        xr, xi = local[j]
        write(j, xr + (cr * pr - ci * pi), xi + (cr * pi + ci * pr))
    end = 0 if reverse else SUBLANES - 1
    return jnp.broadcast_to(er[end:end + 1, :], er.shape), jnp.broadcast_to(ei[end:end + 1, :], ei.shape)


def _s5_fwd(u4, bmat, cmat, coef, dskip, exchanges=()):
    L = u4.shape[1]
    tm = _tile(L, TM_S5)
    lc = min(LANE_CHUNK, WB)

    def body(u_ref, bm_ref, cm_ref, coef_ref, d_ref, sre_ref, sim_ref, ys_ref, hr_ref, hi_ref):
        @pl.when(pl.program_id(1) == 0)
        def _():
            hr_ref[...] = jnp.zeros_like(hr_ref)
            hi_ref[...] = jnp.zeros_like(hi_ref)

        u = _load_permuted(u_ref)
        bu = _dot(u.astype(MXU_DTYPE), bm_ref[0])
        sre_ref[...] = bu[:, :WB]
        sim_ref[...] = bu[:, WB:]
        for c in range(WB // lc):
            lanes = slice(c * lc, (c + 1) * lc)
            hr, hi = hr_ref[:, lanes], hi_ref[:, lanes]
            for b in range(tm // SEG_ROWS):
                rows = lambda j, b=b: slice(b * SEG_ROWS + j * SUBLANES, b * SEG_ROWS + (j + 1) * SUBLANES)

                def read(j, rows=rows, lanes=lanes):
                    return sre_ref[rows(j), lanes], sim_ref[rows(j), lanes]

                def write(j, xr, xi, rows=rows, lanes=lanes):
                    sre_ref[rows(j), lanes] = xr
                    sim_ref[rows(j), lanes] = xi

                hr, hi = _scan_block(read, write, hr, hi, coef_ref, lanes, False)
            hr_ref[:, lanes] = hr
            hi_ref[:, lanes] = hi
        ys = _dot(sre_ref[...].astype(MXU_DTYPE), cm_ref[0, :WB, :]) + _dot(sim_ref[...].astype(MXU_DTYPE), cm_ref[0, WB:, :])
        _store_permuted(ys_ref, ys + d_ref[0] * u)

    return _call(
        body, name="s5_fwd", grid=(N_GBLK, L // tm), exchanges=exchanges, semantics=("arbitrary", "arbitrary"),
        operands=(u4, bmat, cmat, coef, dskip),
        in_specs=[
            _slab_spec(UB, tm, lambda b, i: (b, i, 0)),
            pl.BlockSpec((1, UB, 2 * WB), lambda b, i: (b, 0, 0)),
            pl.BlockSpec((1, 2 * WB, UB), lambda b, i: (b, 0, 0)),
            pl.BlockSpec((1, N_TABLES, SUBLANES, WB), lambda b, i: (b, 0, 0, 0)),
            pl.BlockSpec((1, 1, UB), lambda b, i: (b, 0, 0)),
        ],
        out_specs=[
            pl.BlockSpec((tm, WB), lambda b, i: (i, b)),
            pl.BlockSpec((tm, WB), lambda b, i: (i, b)),
            _slab_spec(UB, tm, lambda b, i: (b, i, 0)),
        ],
        out_shape=[
            jax.ShapeDtypeStruct((L, N_GBLK * WB), F32),
            jax.ShapeDtypeStruct((L, N_GBLK * WB), F32),
            _slab_shape(L, D_SSM),
        ],
        scratch_shapes=[pltpu.VMEM((SUBLANES, WB), F32), pltpu.VMEM((SUBLANES, WB), F32)],
    )


def _tail_fwd(x, ys, proj, w_glu, conv_w, g_ssm, g_conv, w_out, g_post, exchanges=()):
    L, D = x.shape
    tm = _tile(L, TM_TAIL)

    def body(x_ref, ys_ref, h_ref, bg_ref, cg_ref, wglu_ref, cw_ref, gs_ref, gc_ref, wout_ref, gp_ref,
             ycat_ref, o_ref, x1_ref, zbuf):
        @pl.when(pl.program_id(0) == 0)
        def _():
            zbuf[0:SUBLANES, :] = jnp.zeros((SUBLANES, D_CONV), F32)

        y1, _ = _gelu(_load_slabs(ys_ref))
        y2 = y1 * jax.nn.sigmoid(_dot(y1.astype(MXU_DTYPE), wglu_ref[...]))
        ycat_ref[:, :D_SSM] = _rms_fwd(y2, gs_ref[...]).astype(MXU_DTYPE)
        z = cg_ref[...] * h_ref[...]
        zbuf[SUBLANES:, :] = z
        conv = cw_ref[0:1, :] * zbuf[SUBLANES - 2:SUBLANES - 2 + tm, :] + cw_ref[1:2, :] * zbuf[SUBLANES - 1:SUBLANES - 1 + tm, :] + cw_ref[2:3, :] * z
        zbuf[0:SUBLANES, :] = zbuf[tm:tm + SUBLANES, :]
        ycat_ref[:, D_SSM:] = _rms_fwd(bg_ref[...] * conv, gc_ref[...]).astype(MXU_DTYPE)
        o = _dot(ycat_ref[...], wout_ref[...])
        o_ref[...] = o
        x1_ref[...] = x_ref[...] + _rms_fwd(o, gp_ref[...])

    row = lambda i: (i, 0)
    return _call(
        body, name="tail_fwd", grid=(L // tm,), exchanges=exchanges, semantics=("arbitrary",),
        operands=(x, ys, proj, proj, proj, w_glu, conv_w, g_ssm, g_conv, w_out, g_post),
        in_specs=[
            pl.BlockSpec((tm, D), row), _slab_spec(D_SSM, tm),
            pl.BlockSpec((tm, D_CONV), lambda i: (i, 1)), pl.BlockSpec((tm, D_CONV), lambda i: (i, 2)),
            pl.BlockSpec((tm, D_CONV), lambda i: (i, 3)),
            _resident(w_glu.shape), _resident(conv_w.shape), _resident(g_ssm.shape), _resident(g_conv.shape),
            _resident(w_out.shape), _resident(g_post.shape),
        ],
        out_specs=[pl.BlockSpec((tm, D), row), pl.BlockSpec((tm, D), row), pl.BlockSpec((tm, D), row)],
        out_shape=[jax.ShapeDtypeStruct((L, D), MXU_DTYPE), jax.ShapeDtypeStruct((L, D), F32), jax.ShapeDtypeStruct((L, D), F32)],
        scratch_shapes=[pltpu.VMEM((tm + SUBLANES, D_CONV), F32)],
    )


def _mlp_fwd(x1, target, w_up_all, w_down, g_pre, g_post):
    L, D = x1.shape
    ns, _, fc = w_up_all.shape
    tm = _tile(L, TM_MLP)

    def body(x1_ref, t_ref, wup_ref, wdn_ref, gpre_ref, gpost_ref, hn2_ref, up_ref, m_ref, dx2_ref, loss_ref):
        @pl.when(pl.program_id(0) == 0)
        def _():
            loss_ref[...] = jnp.zeros_like(loss_ref)

        x1v = x1_ref[...]
        hn2 = _rms_fwd(x1v, gpre_ref[...]).astype(MXU_DTYPE)
        hn2_ref[...] = hn2
        m = jnp.zeros((tm, D), F32)
        for j in range(ns):
            up = _dot(hn2, wup_ref[j])
            up_ref[:, j * fc:(j + 1) * fc] = up
            act = jnp.square(jnp.maximum(up, 0.0)).astype(MXU_DTYPE)
            m = m + _dot(act, wdn_ref[j * fc:(j + 1) * fc, :])
        m_ref[...] = m
        err = x1v + _rms_fwd(m, gpost_ref[...]) - t_ref[...]
        loss_ref[...] += 0.5 * jnp.sum(jnp.mean(err * err, axis=-1, keepdims=True))
        dx2_ref[...] = err * (1.0 / D)

    row = lambda i: (i, 0)
    return pl.pallas_call(
        body, name="mlp_fwd", grid=(L // tm,),
        in_specs=[pl.BlockSpec((tm, D), row), pl.BlockSpec((tm, D), row), _resident(w_up_all.shape), _resident(w_down.shape),
                  _resident(g_pre.shape), _resident(g_post.shape)],
        out_specs=[pl.BlockSpec((tm, D), row), pl.BlockSpec((tm, ns * fc), row), pl.BlockSpec((tm, D), row),
                   pl.BlockSpec((tm, D), row), pl.BlockSpec((SUBLANES, 128), lambda i: (0, 0))],
        out_shape=[jax.ShapeDtypeStruct((L, D), MXU_DTYPE), jax.ShapeDtypeStruct((L, ns * fc), F32), jax.ShapeDtypeStruct((L, D), F32),
                   jax.ShapeDtypeStruct((L, D), F32), jax.ShapeDtypeStruct((SUBLANES, 128), F32)],
        compiler_params=_params(("arbitrary",)),
    )(x1, target, w_up_all, w_down, g_pre, g_post)


def _mlp_bwd(dx2, m, up, x1, w_up_all, w_down, g_pre, g_post):
    L, D = x1.shape
    ns, _, fc = w_up_all.shape
    tm = _tile(L, TM_MLP)

    def body(dx2_ref, m_ref, up_ref, x1_ref, wup_ref, wdn_ref, gpre_ref, gpost_ref,
             dm_ref, dup_ref, act_ref, dx1_ref, dgpost_ref, dgpre_ref):
        @pl.when(pl.program_id(0) == 0)
        def _():
            dgpost_ref[...] = jnp.zeros_like(dgpost_ref)
            dgpre_ref[...] = jnp.zeros_like(dgpre_ref)

        dx2v = dx2_ref[...]
        dm, dg = _rms_bwd(m_ref[...], gpost_ref[...], dx2v)
        dgpost_ref[...] += dg
        dm_b = dm.astype(MXU_DTYPE)
        dm_ref[...] = dm_b
        dhn2 = jnp.zeros((tm, D), F32)
        for j in range(ns):
            cols = slice(j * fc, (j + 1) * fc)
            relu = jnp.maximum(up_ref[:, cols], 0.0)
            act_ref[:, cols] = jnp.square(relu).astype(MXU_DTYPE)
            dup = (_dot_nt(dm_b, wdn_ref[cols, :]) * (2.0 * relu)).astype(MXU_DTYPE)
            dup_ref[:, cols] = dup
            dhn2 = dhn2 + _dot_nt(dup, wup_ref[j])
        dx, dg = _rms_bwd(x1_ref[...], gpre_ref[...], dhn2)
        dgpre_ref[...] += dg
        dx1_ref[...] = dx2v + dx

    row = lambda i: (i, 0)
    vec = pl.BlockSpec((1, D), lambda i: (0, 0))
    return pl.pallas_call(
        body, name="mlp_bwd", grid=(L // tm,),
        in_specs=[pl.BlockSpec((tm, D), row), pl.BlockSpec((tm, D), row), pl.BlockSpec((tm, ns * fc), row), pl.BlockSpec((tm, D), row),
                  _resident(w_up_all.shape), _resident(w_down.shape), _resident(g_pre.shape), _resident(g_post.shape)],
        out_specs=[pl.BlockSpec((tm, D), row), pl.BlockSpec((tm, ns * fc), row), pl.BlockSpec((tm, ns * fc), row),
                   pl.BlockSpec((tm, D), row), vec, vec],
        out_shape=[jax.ShapeDtypeStruct((L, D), MXU_DTYPE), jax.ShapeDtypeStruct((L, ns * fc), MXU_DTYPE),
                   jax.ShapeDtypeStruct((L, ns * fc), MXU_DTYPE), jax.ShapeDtypeStruct((L, D), F32),
                   jax.ShapeDtypeStruct((1, D), F32), jax.ShapeDtypeStruct((1, D), F32)],
        compiler_params=_params(("arbitrary",)),
    )(dx2, m, up, x1, w_up_all, w_down, g_pre, g_post)


def _tail_bwd(dx1, o, ys, proj, w_glu, conv_w, g_ssm, g_conv, w_out, g_post, exchanges=()):
    L, D = dx1.shape
    tm = _tile(L, TM_TAIL)
    nt = L // tm
    hb = tm // SUBLANES

    def body(dx1_ref, o_ref, ys_ref, h_ref, bg_ref, cg_ref, hh_ref, hcg_ref, wglu_ref, cw_ref, gs_ref, gc_ref, wout_ref, gp_ref,
             do_ref, da_ref, y1_ref, dys_ref, dhbc_ref, dgp_ref, dgs_ref, dgc_ref, dcw_ref, zbuf, dcbuf):
        step = pl.program_id(0)

        @pl.when(step == 0)
        def _():
            dcbuf[tm:, :] = jnp.zeros((SUBLANES, D_CONV), F32)
            dgp_ref[...] = jnp.zeros_like(dgp_ref)
            dgs_ref[...] = jnp.zeros_like(dgs_ref)
            dgc_ref[...] = jnp.zeros_like(dgc_ref)
            dcw_ref[...] = jnp.zeros_like(dcw_ref)

        do, dg = _rms_bwd(o_ref[...], gp_ref[...], dx1_ref[...])
        dgp_ref[...] += dg
        do_b = do.astype(MXU_DTYPE)
        do_ref[...] = do_b
        dycat = _dot_nt(do_b, wout_ref[...])
        y1, dgelu = _gelu(_load_slabs(ys_ref))
        y1_b = y1.astype(MXU_DTYPE)
        y1_ref[...] = y1_b
        s = jax.nn.sigmoid(_dot(y1_b, wglu_ref[...]))
        dy2, dg = _rms_bwd(y1 * s, gs_ref[...], dycat[:, :D_SSM])
        dgs_ref[...] += dg
        da_b = (dy2 * y1 * s * (1.0 - s)).astype(MXU_DTYPE)
        da_ref[...] = da_b
        _store_slabs(dys_ref, (dy2 * s + _dot_nt(da_b, wglu_ref[...])) * dgelu)
        h = h_ref[...]
        cg = cg_ref[...]
        bg = bg_ref[...]
        z = cg * h
        first = step == nt - 1
        zbuf[0:SUBLANES, :] = jnp.where(first, 0.0, hcg_ref[...] * hh_ref[...])
        zbuf[SUBLANES:, :] = z
        z1 = zbuf[SUBLANES - 1:SUBLANES - 1 + tm, :]
        z2 = zbuf[SUBLANES - 2:SUBLANES - 2 + tm, :]
        conv = cw_ref[0:1, :] * z2 + cw_ref[1:2, :] * z1 + cw_ref[2:3, :] * z
        dyc, dg = _rms_bwd(bg * conv, gc_ref[...], dycat[:, D_SSM:])
        dgc_ref[...] += dg
        dconv = dyc * bg
        dcw_ref[0:1, :] += jnp.sum(dconv * z2, axis=0, keepdims=True)
        dcw_ref[1:2, :] += jnp.sum(dconv * z1, axis=0, keepdims=True)
        dcw_ref[2:3, :] += jnp.sum(dconv * z, axis=0, keepdims=True)
        dcbuf[0:tm, :] = dconv
        dz = cw_ref[2:3, :] * dconv + cw_ref[1:2, :] * dcbuf[1:1 + tm, :] + cw_ref[0:1, :] * dcbuf[2:2 + tm, :]
        dcbuf[tm:, :] = dcbuf[0:SUBLANES, :]
        dhbc_ref[:, 0:D_CONV] = (dz * cg).astype(MXU_DTYPE)
        dhbc_ref[:, D_CONV:2 * D_CONV] = (dyc * conv).astype(MXU_DTYPE)
        dhbc_ref[:, 2 * D_CONV:] = (dz * h).astype(MXU_DTYPE)

    rev = lambda i: (nt - 1 - i, 0)
    slab = lambda i: (0, nt - 1 - i, 0)
    col = lambda c: (lambda i: (nt - 1 - i, c))
    halo = lambda c: (lambda i: (jnp.maximum((nt - 1 - i) * hb - 1, 0), c))
    vec = lambda n: pl.BlockSpec((1, n), lambda i: (0, 0))
    return _call(
        body, name="tail_bwd", grid=(nt,), exchanges=exchanges, semantics=("arbitrary",),
        operands=(dx1, o, ys, proj, proj, proj, proj, proj, w_glu, conv_w, g_ssm, g_conv, w_out, g_post),
        in_specs=[
            pl.BlockSpec((tm, D), rev), pl.BlockSpec((tm, D), rev), _slab_spec(D_SSM, tm, slab),
            pl.BlockSpec((tm, D_CONV), col(1)), pl.BlockSpec((tm, D_CONV), col(2)), pl.BlockSpec((tm, D_CONV), col(3)),
            pl.BlockSpec((SUBLANES, D_CONV), halo(1)), pl.BlockSpec((SUBLANES, D_CONV), halo(3)),
            _resident(w_glu.shape), _resident(conv_w.shape), _resident(g_ssm.shape), _resident(g_conv.shape),
            _resident(w_out.shape), _resident(g_post.shape),
        ],
        out_specs=[
            pl.BlockSpec((tm, D), rev), pl.BlockSpec((tm, D_SSM), rev), pl.BlockSpec((tm, D_SSM), rev), _slab_spec(D_SSM, tm, slab),
            pl.BlockSpec((tm, 3 * D_CONV), rev), vec(D), vec(D_SSM), vec(D_CONV),
            pl.BlockSpec((SUBLANES, D_CONV), lambda i: (0, 0)),
        ],
        out_shape=[
            jax.ShapeDtypeStruct((L, D), MXU_DTYPE), jax.ShapeDtypeStruct((L, D_SSM), MXU_DTYPE), jax.ShapeDtypeStruct((L, D_SSM), MXU_DTYPE),
            _slab_shape(L, D_SSM), jax.ShapeDtypeStruct((L, 3 * D_CONV), MXU_DTYPE),
            jax.ShapeDtypeStruct((1, D), F32), jax.ShapeDtypeStruct((1, D_SSM), F32), jax.ShapeDtypeStruct((1, D_CONV), F32),
            jax.ShapeDtypeStruct((SUBLANES, D_CONV), F32),
        ],
        scratch_shapes=[pltpu.VMEM((tm + SUBLANES, D_CONV), F32), pltpu.VMEM((tm + SUBLANES, D_CONV), F32)],
    )


def _s5_bwd(dys, u4, s_re, s_im, bmat, cmat, coef_rev, dskip, exchanges=()):
    L = dys.shape[1]
    tm = _tile(L, TM_S5)
    nt = L // tm
    lc = min(LANE_CHUNK, WB)

    def body(dys_ref, u_ref, sre_ref, sim_ref, bm_ref, cm_ref, coef_ref, d_ref,
             du_ref, gb_ref, gc_ref, q_ref, gd_ref, dr_ref, di_ref, lr_ref, li_ref, hr_ref, hi_ref, qr_acc, qi_acc):
        step = pl.program_id(1)

        @pl.when(step == 0)
        def _():
            for ref in (hr_ref, hi_ref, qr_acc, qi_acc, gb_ref, gc_ref, gd_ref):
                ref[...] = jnp.zeros_like(ref)

        dys_v = _load_permuted(dys_ref)
        u = _load_permuted(u_ref)
        dys_b = dys_v.astype(MXU_DTYPE)
        u_b = u.astype(MXU_DTYPE)
        d = _dot_nt(dys_b, cm_ref[0])
        dr_ref[...] = d[:, :WB]
        di_ref[...] = d[:, WB:]
        for c in range(WB // lc):
            lanes = slice(c * lc, (c + 1) * lc)
            hr, hi = hr_ref[:, lanes], hi_ref[:, lanes]
            q = [qr_acc[:, lanes], qi_acc[:, lanes]]
            for b in range(tm // SEG_ROWS - 1, -1, -1):
                rows = lambda j, b=b: slice(b * SEG_ROWS + j * SUBLANES, b * SEG_ROWS + (j + 1) * SUBLANES)

                def read(j, rows=rows, lanes=lanes):
                    return dr_ref[rows(j), lanes], di_ref[rows(j), lanes]

                def write(j, xr, xi, rows=rows, lanes=lanes, q=q):
                    lr_ref[rows(j), lanes] = xr
                    li_ref[rows(j), lanes] = xi
                    er = xr - dr_ref[rows(j), lanes]
                    ei = xi - di_ref[rows(j), lanes]
                    sr = sre_ref[rows(j), lanes]
                    si = sim_ref[rows(j), lanes]
                    q[0] = q[0] + (er * sr + ei * si)
                    q[1] = q[1] + (ei * sr - er * si)

                hr, hi = _scan_block(read, write, hr, hi, coef_ref, lanes, True)
            hr_ref[:, lanes] = hr
            hi_ref[:, lanes] = hi
            qr_acc[:, lanes] = q[0]
            qi_acc[:, lanes] = q[1]
        lr_b = lr_ref[...].astype(MXU_DTYPE)
        li_b = li_ref[...].astype(MXU_DTYPE)
        _store_permuted(du_ref, _dot_nt(lr_b, bm_ref[0, :, :WB]) + _dot_nt(li_b, bm_ref[0, :, WB:]) + d_ref[0] * dys_v)
        gb_ref[0, :WB, :] += _dot_tn(lr_b, u_b)
        gb_ref[0, WB:, :] += _dot_tn(li_b, u_b)
        gc_ref[0, :, :WB] += _dot_tn(dys_b, sre_ref[...].astype(MXU_DTYPE))
        gc_ref[0, :, WB:] += _dot_tn(dys_b, sim_ref[...].astype(MXU_DTYPE))
        gd_ref[0] += jnp.sum(dys_v * u, axis=0, keepdims=True)

        @pl.when(step == nt - 1)
        def _():
            q_ref[0, 0:1, :] = jnp.sum(qr_acc[...], axis=0, keepdims=True)
            q_ref[0, 1:2, :] = jnp.sum(qi_acc[...], axis=0, keepdims=True)

    rev = lambda b, i: (nt - 1 - i, b)
    slab = lambda b, i: (b, nt - 1 - i, 0)
    blk = lambda b, i: (b, 0, 0)
    return _call(
        body, name="s5_bwd", grid=(N_GBLK, nt), exchanges=exchanges, semantics=("arbitrary", "arbitrary"),
        operands=(dys, u4, s_re, s_im, bmat, cmat, coef_rev, dskip),
        in_specs=[
            _slab_spec(UB, tm, slab), _slab_spec(UB, tm, slab), pl.BlockSpec((tm, WB), rev), pl.BlockSpec((tm, WB), rev),
            pl.BlockSpec((1, UB, 2 * WB), blk), pl.BlockSpec((1, 2 * WB, UB), blk),
            pl.BlockSpec((1, N_TABLES, SUBLANES, WB), lambda b, i: (b, 0, 0, 0)), pl.BlockSpec((1, 1, UB), blk),
        ],
        out_specs=[
            _slab_spec(UB, tm, slab), pl.BlockSpec((1, 2 * WB, UB), blk), pl.BlockSpec((1, UB, 2 * WB), blk),
            pl.BlockSpec((1, 2, WB), blk), pl.BlockSpec((1, 1, UB), blk),
        ],
        out_shape=[
            _slab_shape(L, D_SSM), jax.ShapeDtypeStruct((N_GBLK, 2 * WB, UB), F32),
            jax.ShapeDtypeStruct((N_GBLK, UB, 2 * WB), F32), jax.ShapeDtypeStruct((N_GBLK, 2, WB), F32),
            jax.ShapeDtypeStruct((N_GBLK, 1, UB), F32),
        ],
        scratch_shapes=[pltpu.VMEM((tm, WB), F32)] * 4 + [pltpu.VMEM((SUBLANES, WB), F32)] * 4,
    )


def _inproj_bwd(du, dhbc, x, dx1, w_in_all, g1):
    L, D = x.shape
    ns, _, nc = w_in_all.shape
    tm = _tile(L, TM_PROJ)

    def body(du_ref, dhbc_ref, x_ref, dx1_ref, w_ref, g_ref, gx_ref, dproj_ref, dg_ref):
        @pl.when(pl.program_id(0) == 0)
        def _():
            dg_ref[...] = jnp.zeros_like(dg_ref)

        du_b = _load_slabs(du_ref).astype(MXU_DTYPE)
        dproj_ref[:, :nc] = du_b
        dproj_ref[:, nc:] = dhbc_ref[...]
        dhn = _dot_nt(du_b, w_ref[0])
        for j in range(1, ns):
            dhn = dhn + _dot_nt(dhbc_ref[:, (j - 1) * nc:j * nc], w_ref[j])
        dx, dg = _rms_bwd(x_ref[...], g_ref[...], dhn)
        dg_ref[...] += dg
        gx_ref[...] = dx1_ref[...] + dx

    row = lambda i: (i, 0)
    return pl.pallas_call(
        body, name="inproj_bwd", grid=(L // tm,),
        in_specs=[_slab_spec(nc, tm), pl.BlockSpec((tm, (ns - 1) * nc), row), pl.BlockSpec((tm, D), row), pl.BlockSpec((tm, D), row),
                  _resident(w_in_all.shape), _resident(g1.shape)],
        out_specs=[pl.BlockSpec((tm, D), row), pl.BlockSpec((tm, ns * nc), row), pl.BlockSpec((1, D), lambda i: (0, 0))],
        out_shape=[jax.ShapeDtypeStruct((L, D), F32), jax.ShapeDtypeStruct((L, ns * nc), MXU_DTYPE), jax.ShapeDtypeStruct((1, D), F32)],
        compiler_params=_params(("arbitrary",)),
    )(du, dhbc, x, dx1, w_in_all, g1)


def _matmul_tn(a, b, name, col_shards=1, exchanges=()):
    L, K = a.shape
    N = b.shape[1]
    tl = _tile(L, TL_TN)
    tk = _tile(K, 1024)
    nw = N // col_shards
    tn = _tile(nw, 1024)
    npb = nw // tn

    def body(a_ref, b_ref, o_ref):
        @pl.when(pl.program_id(2) == 0)
        def _():
            o_ref[...] = jnp.zeros_like(o_ref)

        o_ref[0] += _dot_tn(a_ref[...], b_ref[...])

    return _call(
        body, name=name, grid=(K // tk, N // tn, L // tl), exchanges=exchanges, semantics=("arbitrary", "arbitrary", "arbitrary"),
        operands=(a, b),
        in_specs=[pl.BlockSpec((tl, tk), lambda k, n, l: (l, k)), pl.BlockSpec((tl, tn), lambda k, n, l: (l, n))],
        out_specs=[pl.BlockSpec((1, tk, tn), lambda k, n, l: (n // npb, k, n % npb))],
        out_shape=[jax.ShapeDtypeStruct((col_shards, K, nw), F32)],
    )


def _ssm_discretize(lam_re, lam_im, log_dt, b_re, b_im):
    dt = jnp.exp(log_dt)[:, None]
    zr = lam_re * dt
    zi = lam_im * dt
    mag = jnp.exp(zr)
    abr = mag * jnp.cos(zi)
    abi = mag * jnp.sin(zi)
    nr, ni = abr - 1.0, abi
    den = lam_re * lam_re + lam_im * lam_im
    coef_r = (nr * lam_re + ni * lam_im) / den
    coef_i = (ni * lam_re - nr * lam_im) / den
    bbar_r = coef_r[..., None] * b_re - coef_i[..., None] * b_im
    bbar_i = coef_r[..., None] * b_im + coef_i[..., None] * b_re
    return zr, zi, bbar_r, bbar_i


def _block_diag(t):
    nb, g, r, c = t.shape
    eye = jnp.eye(g, dtype=t.dtype)
    return (t[:, :, :, None, :] * eye[None, :, None, :, None]).reshape(nb, g * r, g * c)


def _diag_blocks(m, r, c):
    nb = m.shape[0]
    g = m.shape[1] // r
    eye = jnp.eye(g, dtype=m.dtype)
    t = (m.reshape(nb, g, r, g, c) * eye[None, :, None, :, None]).sum(axis=3)
    return t.reshape(nb * g, r, c)


def _powers(a):
    def mul(p, q):
        return (p[0] * q[0] - p[1] * q[1], p[0] * q[1] + p[1] * q[0])

    a2 = mul(a, a)
    a3 = mul(a2, a)
    a4 = mul(a2, a2)
    return [a, a2, a3, a4, mul(a4, a), mul(a4, a2), mul(a4, a3), mul(a4, a4)]


def _scan_tables(ar, ai, reverse):
    pw = _powers((ar, -ai if reverse else ai))
    seg = _powers(pw[SUBLANES - 1])
    t = jnp.arange(SUBLANES)[:, None]
    tabs = []
    for k in (1, 2, 4):
        mask = (t + k <= SUBLANES - 1) if reverse else (t >= k)
        tabs += [jnp.where(mask, seg[k - 1][0][None, :], 0.0), jnp.where(mask, seg[k - 1][1][None, :], 0.0)]
    order = range(SUBLANES - 1, -1, -1) if reverse else range(SUBLANES)
    tabs += [jnp.stack([seg[j][0] for j in order]), jnp.stack([seg[j][1] for j in order])]
    for j in range(SUBLANES):
        p = pw[SUBLANES - 1 - j] if reverse else pw[j]
        tabs += [jnp.broadcast_to(p[0][None, :], (SUBLANES, p[0].shape[0])), jnp.broadcast_to(p[1][None, :], (SUBLANES, p[1].shape[0]))]
    coef = jnp.stack(tabs)
    return coef.reshape(N_TABLES, SUBLANES, N_GBLK, WB).transpose(2, 0, 1, 3)


def _ssm_matrices(lam_re, lam_im, log_dt, b_re, b_im, c_re, c_im):
    zr, zi, bbar_r, bbar_i = _ssm_discretize(lam_re, lam_im, log_dt, b_re, b_im)
    mag = jnp.exp(zr)
    ar = (mag * jnp.cos(zi)).reshape(-1)
    ai = (mag * jnp.sin(zi)).reshape(-1)
    blk = lambda t: t.reshape(N_GBLK, G_PER_BLK, *t.shape[1:])
    bmat = jnp.concatenate([_block_diag(blk(bbar_r).transpose(0, 1, 3, 2)), _block_diag(blk(bbar_i).transpose(0, 1, 3, 2))], axis=2)
    cmat = jnp.concatenate([_block_diag(blk(c_re).transpose(0, 1, 3, 2)), _block_diag(blk(-c_im).transpose(0, 1, 3, 2))], axis=1)
    return bmat.astype(MXU_DTYPE), cmat.astype(MXU_DTYPE), _scan_tables(ar, ai, False), _scan_tables(ar, ai, True)


def _ssm_param_grads(lam_re, lam_im, log_dt, b_re, b_im, gb, gc, q, gd):
    gbr = _diag_blocks(gb[:, :WB, :], STATE, GROUP)
    gbi = _diag_blocks(gb[:, WB:, :], STATE, GROUP)
    d_c_re = _diag_blocks(gc[:, :, :WB], GROUP, STATE)
    d_c_im = -_diag_blocks(gc[:, :, WB:], GROUP, STATE)
    qr = q[:, 0, :].reshape(N_GROUPS, STATE)
    qi = q[:, 1, :].reshape(N_GROUPS, STATE)
    _, vjp = jax.vjp(_ssm_discretize, lam_re, lam_im, log_dt, b_re, b_im)
    d_lam_re, d_lam_im, d_log_dt, d_b_re, d_b_im = vjp((qr, qi, gbr, gbi))
    return d_lam_re, d_lam_im, d_log_dt, d_b_re, d_b_im, d_c_re, d_c_im, gd.reshape(N_GROUPS, GROUP)


def _row_tile(rows, n):
    return _tile(rows, max(SUBLANES, (2 * 1024 * 1024) // (4 * n)))


def _pair_add(grad, other, core, name):
    ns, h, n = other.shape
    tr = _row_tile(h, n)
    nb = h // tr

    def body(c_ref, g_ref, o_ref, out_ref):
        out_ref[...] = (g_ref[...] + o_ref[...]).astype(WIRE_DTYPE)

    return pl.pallas_call(
        body, name=name,
        grid_spec=pltpu.PrefetchScalarGridSpec(
            num_scalar_prefetch=1, grid=(ns, nb),
            in_specs=[pl.BlockSpec((1, tr, n), lambda s, i, c: (s, c[0] * nb + i, 0)), pl.BlockSpec((1, tr, n), lambda s, i, c: (s, i, 0))],
            out_specs=pl.BlockSpec((1, tr, n), lambda s, i, c: (s, i, 0))),
        out_shape=jax.ShapeDtypeStruct(other.shape, WIRE_DTYPE),
        compiler_params=_params(("arbitrary", "arbitrary")),
    )(core, grad, other)


def _quad_sum(parts, core, name):
    ns, h, n = parts.shape
    tr = _row_tile(h, n)
    nb = h // tr

    def body(c_ref, p_ref, out_ref):
        p = [p_ref[k].astype(F32) for k in range(ns)]
        out_ref[...] = ((p[0] + p[1]) + p[2]) + p[3]

    return pl.pallas_call(
        body, name=name,
        grid_spec=pltpu.PrefetchScalarGridSpec(
            num_scalar_prefetch=1, grid=(nb,),
            in_specs=[pl.BlockSpec((ns, tr, n), lambda i, c: (0, i, 0))],
            out_specs=pl.BlockSpec((tr, n), lambda i, c: (c[0] * nb + i, 0))),
        out_shape=jax.ShapeDtypeStruct((2 * h, n), F32),
        compiler_params=_params(("arbitrary",)),
    )(core, parts)


def _sum_devices(blocks):
    nd, m, n = blocks.shape

    def body(b_ref, out_ref):
        total = b_ref[0]
        for d in range(1, nd):
            total = total + b_ref[d]
        out_ref[...] = total

    return pl.pallas_call(body, name="sum_devices", in_specs=[VMEM], out_specs=VMEM,
                          out_shape=jax.ShapeDtypeStruct((m, n), blocks.dtype))(blocks)


def _adamw_math(w, g, m, v):
    m = ADAM_B1 * m + (1.0 - ADAM_B1) * g
    v = ADAM_B2 * v + (1.0 - ADAM_B2) * jnp.square(g)
    m_hat = m / (1.0 - ADAM_B1 ** ADAM_STEP)
    v_hat = v / (1.0 - ADAM_B2 ** ADAM_STEP)
    delta = -ADAM_LR * (m_hat / (jnp.sqrt(v_hat) + ADAM_EPS) + ADAM_WD * w)
    return delta, m, v


def _adamw(w, g, m, v, name):
    r, n = w.shape
    tr = _row_tile(r, n)

    def body(w_ref, g_ref, m_ref, v_ref, d_ref, nm_ref, nv_ref):
        d_ref[...], nm_ref[...], nv_ref[...] = _adamw_math(w_ref[...], g_ref[...], m_ref[...], v_ref[...])

    spec = pl.BlockSpec((tr, n), lambda i: (i, 0))
    return pl.pallas_call(
        body, name=name, grid=(r // tr,), in_specs=[spec] * 4, out_specs=[spec] * 3,
        out_shape=[jax.ShapeDtypeStruct((r, n), F32)] * 3,
        compiler_params=_params(("arbitrary",)),
    )(w, g, m, v)


SMALL_SHAPES = {
    "g_pre_mix": (D_MODEL,), "lam_re": (N_GROUPS, STATE), "lam_im": (N_GROUPS, STATE), "log_dt": (N_GROUPS,),
    "b_re": (N_GROUPS, STATE, GROUP), "b_im": (N_GROUPS, STATE, GROUP), "c_re": (N_GROUPS, GROUP, STATE), "c_im": (N_GROUPS, GROUP, STATE),
    "d_skip": (N_GROUPS, GROUP), "conv_w": (3, D_CONV), "g_ssm_out": (D_SSM,), "g_conv_out": (D_CONV,),
    "g_post_mix": (D_MODEL,), "g_pre_mlp": (D_MODEL,), "g_post_mlp": (D_MODEL,),
}
LANES = 128
PACK_TILE = SUBLANES * LANES


def _pack_rows(name):
    n = math.prod(SMALL_SHAPES[name])
    return SUBLANES * (-(-n // PACK_TILE))


def _pack_offsets():
    offs, row = {}, 0
    for name in SMALL_SHAPES:
        offs[name] = row
        row += _pack_rows(name)
    return offs, row


def _pack_small(grads):
    parts = []
    for name in SMALL_SHAPES:
        flat = grads[name].reshape(-1)
        parts.append(jnp.pad(flat, (0, _pack_rows(name) * LANES - flat.shape[0])).reshape(-1, LANES))
    return jnp.concatenate(parts, axis=0)


def _lane_shape(name):
    n = math.prod(SMALL_SHAPES[name])
    return (n // LANES, LANES) if n % LANES == 0 else (1, n)


def _adamw_small(pack, conv_grad, w, m, v):
    names = list(SMALL_SHAPES)
    offs, _ = _pack_offsets()
    nn = len(names)

    def body(*refs):
        pack_ref, cg_ref = refs[0], refs[1]
        w_refs, m_refs, v_refs = refs[2:2 + nn], refs[2 + nn:2 + 2 * nn], refs[2 + 2 * nn:2 + 3 * nn]
        outs = refs[2 + 3 * nn:]
        for j, name in enumerate(names):
            r, n = w_refs[j].shape
            g = cg_ref[...] if name == "conv_w" else pack_ref[offs[name]:offs[name] + r, 0:n]
            delta, nm, nv = _adamw_math(w_refs[j][...], g, m_refs[j][...], v_refs[j][...])
            outs[j][...] = g
            outs[nn + j][...] = delta
            outs[2 * nn + j][...] = nm
            outs[3 * nn + j][...] = nv

    args = [pack, conv_grad] + [w[k] for k in names] + [m[k] for k in names] + [v[k] for k in names]
    res = pl.pallas_call(
        body, name="adamw_small", in_specs=[VMEM] * len(args), out_specs=[VMEM] * (4 * nn),
        out_shape=[jax.ShapeDtypeStruct(w[k].shape, F32) for k in names] * 4,
    )(*args)
    return [dict(zip(names, res[q * nn:(q + 1) * nn])) for q in range(4)]


WEIGHTS = ["g_pre_mix", "w_in", "lam_re", "lam_im", "log_dt", "b_re", "b_im", "c_re", "c_im", "d_skip", "w_glu", "conv_w",
           "g_ssm_out", "g_conv_out", "w_out", "g_post_mix", "g_pre_mlp", "w_up", "w_down", "g_post_mlp"]
BIG = ["w_in", "w_glu", "w_out", "w_up", "w_down"]


def kernel(x, g_pre_mix, w_in, lam_re, lam_im, log_dt, b_re, b_im, c_re, c_im, d_skip, w_glu, conv_w, g_ssm_out, g_conv_out, w_out, g_post_mix, g_pre_mlp, w_up, w_down, g_post_mlp, loss_target, m_g_pre_mix, m_w_in, m_lam_re, m_lam_im, m_log_dt, m_b_re, m_b_im, m_c_re, m_c_im, m_d_skip, m_w_glu, m_conv_w, m_g_ssm_out, m_g_conv_out, m_w_out, m_g_post_mix, m_g_pre_mlp, m_w_up, m_w_down, m_g_post_mlp, v_g_pre_mix, v_w_in, v_lam_re, v_lam_im, v_log_dt, v_b_re, v_b_im, v_c_re, v_c_im, v_d_skip, v_w_glu, v_conv_w, v_g_ssm_out, v_g_conv_out, v_w_out, v_g_post_mix, v_g_pre_mlp, v_w_up, v_w_down, v_g_post_mlp):
    w = dict(g_pre_mix=g_pre_mix, w_in=w_in, lam_re=lam_re, lam_im=lam_im, log_dt=log_dt, b_re=b_re, b_im=b_im, c_re=c_re, c_im=c_im,
             d_skip=d_skip, w_glu=w_glu, conv_w=conv_w, g_ssm_out=g_ssm_out, g_conv_out=g_conv_out, w_out=w_out, g_post_mix=g_post_mix,
             g_pre_mlp=g_pre_mlp, w_up=w_up, w_down=w_down, g_post_mlp=g_post_mlp)
    m = dict(g_pre_mix=m_g_pre_mix, w_in=m_w_in, lam_re=m_lam_re, lam_im=m_lam_im, log_dt=m_log_dt, b_re=m_b_re, b_im=m_b_im, c_re=m_c_re,
             c_im=m_c_im, d_skip=m_d_skip, w_glu=m_w_glu, conv_w=m_conv_w, g_ssm_out=m_g_ssm_out, g_conv_out=m_g_conv_out, w_out=m_w_out,
             g_post_mix=m_g_post_mix, g_pre_mlp=m_g_pre_mlp, w_up=m_w_up, w_down=m_w_down, g_post_mlp=m_g_post_mlp)
    v = dict(g_pre_mix=v_g_pre_mix, w_in=v_w_in, lam_re=v_lam_re, lam_im=v_lam_im, log_dt=v_log_dt, b_re=v_b_re, b_im=v_b_im, c_re=v_c_re,
             c_im=v_c_im, d_skip=v_d_skip, w_glu=v_w_glu, conv_w=v_conv_w, g_ssm_out=v_g_ssm_out, g_conv_out=v_g_conv_out, w_out=v_w_out,
             g_post_mix=v_g_post_mix, g_pre_mlp=v_g_pre_mlp, w_up=v_w_up, w_down=v_w_down, g_post_mlp=v_g_post_mlp)
    shapes = {k: a.shape for k, a in w.items()}
    w, m, v = ({k: a[0] for k, a in d.items()} for d in (w, m, v))
    chip = 2 * lax.axis_index("x") + lax.axis_index("y")
    core = lax.axis_index("c").astype(jnp.int32).reshape(1)

    xs, target = x[0], loss_target[0]
    g1 = w["g_pre_mix"][None]
    g_ssm, g_conv = w["g_ssm_out"][None], w["g_conv_out"][None]
    g_post_mix, g_pre_mlp, g_post_mlp = w["g_post_mix"][None], w["g_pre_mlp"][None], w["g_post_mlp"][None]
    bmat, cmat, coef_f, coef_r = _ssm_matrices(w["lam_re"], w["lam_im"], w["log_dt"], w["b_re"], w["b_im"], w["c_re"], w["c_im"])
    dskip = w["d_skip"].reshape(N_GBLK, 1, UB)
    shard = {k: w[k].astype(MXU_DTYPE) for k in BIG}
    conv_pad = jnp.pad(w["conv_w"], ((0, SUBLANES - 3), (0, 0)))

    (w_in_all,) = _run_exchanges([_GatherForward([shard["w_in"]])], "ag_w_in")
    hn, proj, u4 = _inproj_fwd(xs, g1, w_in_all)
    rest = _Gather([shard["w_glu"], shard["w_out"], conv_pad, shard["w_up"], shard["w_down"]], [False, False, False, True, True])
    s_re, s_im, ys, w_glu_all, w_out_all, conv_all, w_up_all, w_down_all = _s5_fwd(u4, bmat, cmat, coef_f, dskip, exchanges=[rest])
    w_glu_f, w_out_f = w_glu_all.reshape(D_SSM, D_SSM), w_out_all.reshape(D_MODEL, D_MODEL)
    conv_f = jnp.transpose(conv_all, (1, 0, 2)).reshape(SUBLANES, D_CONV)
    ycat, o, x1, w_up_all, w_down_all = _tail_fwd(xs, ys, proj, w_glu_f, conv_f, g_ssm, g_conv, w_out_f, g_post_mix,
                                                  exchanges=[_Forward([w_up_all, w_down_all])])
    w_down_f = w_down_all.reshape(D_FF, D_MODEL)
    hn2, up, m_act, dx2, loss = _mlp_fwd(x1, target, w_up_all, w_down_f, g_pre_mlp, g_post_mlp)

    dm, dup, act, dx1, dg_post_mlp, dg_pre_mlp = _mlp_bwd(dx2, m_act, up, x1, w_up_all, w_down_f, g_pre_mlp, g_post_mlp)
    gw_down = _matmul_tn(act, dm, "dw_down")[0].reshape(N_CHIPS, D_FF // N_CHIPS, D_MODEL)
    gw_up = _matmul_tn(hn2, dup, "dw_up", col_shards=N_CHIPS)[0]
    do, da, y1, dys, dhbc, dg_post_mix, dg_ssm, dg_conv, dconv_w, o_down, o_up = _tail_bwd(
        dx1, o, ys, proj, w_glu_f, conv_f, g_ssm, g_conv, w_out_f, g_post_mix, exchanges=[_Pair([gw_down, gw_up])])
    p_down = _pair_add(gw_down, o_down, core, "pair_add_w_down")
    p_up = _pair_add(gw_up, o_up, core, "pair_add_w_up")
    gw_out = _matmul_tn(ycat, do, "dw_out")[0].reshape(N_CHIPS, D_MODEL // N_CHIPS, D_MODEL)
    gw_glu = _matmul_tn(y1, da, "dw_glu")[0].reshape(N_CHIPS, D_SSM // N_CHIPS, D_SSM)
    du, gb, gc, q, gd, q_down, q_up, o_out, o_glu = _s5_bwd(
        dys, u4, s_re, s_im, bmat, cmat, coef_r, dskip, exchanges=[_Chip([p_down, p_up]), _Pair([gw_out, gw_glu])])
    h_down = _quad_sum(q_down, core, "quad_sum_w_down")
    h_up = _quad_sum(q_up, core, "quad_sum_w_up")
    p_out = _pair_add(gw_out, o_out, core, "pair_add_w_out")
    p_glu = _pair_add(gw_glu, o_glu, core, "pair_add_w_glu")
    grad_x, dproj, dg_pre_mix = _inproj_bwd(du, dhbc, xs, dx1, w_in_all, g1)
    d_lam_re, d_lam_im, d_log_dt, d_b_re, d_b_im, d_c_re, d_c_im, d_d_skip = _ssm_param_grads(
        w["lam_re"], w["lam_im"], w["log_dt"], w["b_re"], w["b_im"], gb, gc, q, gd)
    small = {
        "g_pre_mix": dg_pre_mix[0], "lam_re": d_lam_re, "lam_im": d_lam_im, "log_dt": d_log_dt, "b_re": d_b_re, "b_im": d_b_im,
        "c_re": d_c_re, "c_im": d_c_im, "d_skip": d_d_skip, "conv_w": dconv_w[:3], "g_ssm_out": dg_ssm[0], "g_conv_out": dg_conv[0],
        "g_post_mix": dg_post_mix[0], "g_pre_mlp": dg_pre_mlp[0], "g_post_mlp": dg_post_mlp[0],
    }
    gw_in, g_down, g_up, q_out, q_glu, packs = _matmul_tn(
        hn, dproj, "dw_in", col_shards=N_CHIPS,
        exchanges=[_Share([h_down, h_up]), _Chip([p_out, p_glu]), _GatherSmall(jnp.concatenate([_pack_small(small), loss], axis=0))])
    h_out = _quad_sum(q_out, core, "quad_sum_w_out")
    h_glu = _quad_sum(q_glu, core, "quad_sum_w_glu")
    pack = _sum_devices(packs)
    loss = pack[pack.shape[0] - SUBLANES, 0]
    (o_in,) = _run_exchanges([_Pair([gw_in])], "rs_pair_w_in")
    p_in = _pair_add(gw_in, o_in, core, "pair_add_w_in")
    q_in, g_out, g_glu = _run_exchanges([_Chip([p_in]), _Share([h_out, h_glu])], "rs_chip_w_in")
    h_in = _quad_sum(q_in, core, "quad_sum_w_in")
    (g_in,) = _run_exchanges([_Share([h_in])], "rs_share_w_in")
    shard_grads = {"w_in": g_in, "w_glu": g_glu, "w_out": g_out, "w_up": g_up, "w_down": g_down}
    offs, _ = _pack_offsets()
    conv_grad = pack[offs["conv_w"]:offs["conv_w"] + 3 * D_CONV // LANES].reshape(3, D_CONV)
    conv_grad = lax.dynamic_slice(conv_grad, (0, chip * (D_CONV // N_CHIPS)), (3, D_CONV // N_CHIPS))

    out = {q: {} for q in ("grad", "delta", "new_m", "new_v")}
    for k in BIG:
        out["grad"][k] = shard_grads[k]
        out["delta"][k], out["new_m"][k], out["new_v"][k] = _adamw(w[k], shard_grads[k], m[k], v[k], "adamw_" + k)
    lane = lambda d: {k: (d[k] if k == "conv_w" else d[k].reshape(_lane_shape(k))) for k in SMALL_SHAPES}
    res = _adamw_small(pack, conv_grad, lane(w), lane(m), lane(v))
    for q, d in zip(("grad", "delta", "new_m", "new_v"), res):
        out[q].update(d)
    flat = [loss, grad_x[None]]
    for q in ("grad", "delta", "new_m", "new_v"):
        flat += [out[q][k].reshape(shapes[k]) for k in WEIGHTS]
    return tuple(flat)
```

```python
import functools
import math

import jax
import jax.numpy as jnp
from jax import lax
from jax.experimental import pallas as pl
from jax.experimental.pallas import tpu as pltpu

F32 = jnp.float32
MXU_DTYPE = jnp.bfloat16
WIRE_DTYPE = jnp.bfloat16

D_MODEL = 1024
D_SSM = 512
D_CONV = 512
N_GROUPS = 32
GROUP = 16
STATE = 64
D_FF = 4096
RMS_EPS = 1e-6
N_CHIPS = 4
N_DEV = 8

ADAM_LR = 0.001
ADAM_B1 = 0.9
ADAM_B2 = 0.999
ADAM_EPS = 1e-08
ADAM_WD = 0.01
ADAM_STEP = 10

N_GBLK = 2
G_PER_BLK = N_GROUPS // N_GBLK
UB = G_PER_BLK * GROUP
WB = G_PER_BLK * STATE
LANE_CHUNK = 256
SUBLANES = 8
N_TABLES = 24

TM_PROJ = 512
TM_S5 = 512
TM_TAIL = 256
TM_MLP = 256
TL_TN = 2048
VMEM_LIMIT = 56 * 1024 * 1024

MESH = pl.DeviceIdType.MESH


def _params(sem, vmem=VMEM_LIMIT):
    return pltpu.CompilerParams(dimension_semantics=sem, vmem_limit_bytes=vmem)


def _resident(shape):
    nd = len(shape)
    return pl.BlockSpec(shape, lambda *_: (0,) * nd, pipeline_mode=pl.Buffered(1))


def _dot(a, b):
    return jnp.dot(a, b, preferred_element_type=F32)


def _dot_nt(a, b):
    return lax.dot_general(a, b, (((1,), (1,)), ((), ())), preferred_element_type=F32)


def _dot_tn(a, b):
    return lax.dot_general(a, b, (((0,), (0,)), ((), ())), preferred_element_type=F32)


def _rms_fwd(x, g):
    r = lax.rsqrt(jnp.mean(x * x, axis=-1, keepdims=True) + RMS_EPS)
    return x * r * g


def _rms_bwd(x, g, dy):
    r = lax.rsqrt(jnp.mean(x * x, axis=-1, keepdims=True) + RMS_EPS)
    xn = x * r
    q = dy * g
    dx = r * (q - xn * jnp.mean(q * xn, axis=-1, keepdims=True))
    return dx, jnp.sum(dy * xn, axis=0, keepdims=True)


_GELU_C = math.sqrt(2.0 / math.pi)


def _gelu(x):
    t = jnp.tanh(_GELU_C * (x + 0.044715 * (x * x * x)))
    y = x * (0.5 * (1.0 + t))
    dy = 0.5 * (1.0 + t) + 0.5 * x * (1.0 - t * t) * (_GELU_C * (1.0 + 3 * 0.044715 * (x * x)))
    return y, dy


def _tile(n, pref):
    t = min(n, pref)
    assert n % t == 0, (n, t)
    return t


HBM = pl.BlockSpec(memory_space=pltpu.HBM)
VMEM = pl.BlockSpec(memory_space=pltpu.VMEM)
DMA_SEMS = pltpu.SemaphoreType.DMA


def _place():
    x, y, c = lax.axis_index("x"), lax.axis_index("y"), lax.axis_index("c")
    chips = [(1 - x, y), (x, 1 - y), (1 - x, 1 - y)]
    return (x, y, c), 2 * x + y, (x, y, 1 - c), chips, [2 * px + py for px, py in chips]


def _remote(src, dst, send_sem, recv_sem, device):
    return pltpu.make_async_remote_copy(src_ref=src, dst_ref=dst, send_sem=send_sem, recv_sem=recv_sem,
                                        device_id=device, device_id_type=MESH)


def _half(rows, c):
    return pl.ds(c * (rows // 2), rows // 2)


class _Exchange:
    aliases = {}

    def start(self, ins, outs, sems):
        local, outgoing, _ = self._copies(ins, outs, sems)
        for cp in local + outgoing:
            cp.start()

    def finish(self, ins, outs, sems):
        local, outgoing, incoming = self._copies(ins, outs, sems)
        for cp in incoming:
            cp.wait_recv()
        for cp in outgoing:
            cp.wait_send()
        for cp in local:
            cp.wait()


class _Gather(_Exchange):
    def __init__(self, shards, split):
        self.inputs, self.split = list(shards), split
        self.out_shape = [jax.ShapeDtypeStruct((N_CHIPS, *a.shape), a.dtype) for a in shards]
        self.sems = [DMA_SEMS((len(shards), 3)), DMA_SEMS((len(shards), 3)), DMA_SEMS((len(shards),))]

    def _copies(self, ins, outs, sems):
        send, recv, lsem = sems
        (x, y, c), me, sibling, chips, ids = _place()
        local = [pltpu.make_async_copy(ins[t], outs[t].at[me], lsem.at[t]) for t in range(len(ins))]
        outgoing, incoming = [], []
        for t, a in enumerate(self.inputs):
            rows = _half(a.shape[0], c) if self.split[t] else pl.ds(0, a.shape[0])
            for k in range(3):
                to = (*chips[k], c)
                outgoing.append(_remote(ins[t].at[rows, :], outs[t].at[me, rows, :], send.at[t, k], recv.at[t, k], to))
                incoming.append(_remote(ins[t].at[rows, :], outs[t].at[ids[k], rows, :], send.at[t, k], recv.at[t, k], to))
        return local, outgoing, incoming


class _Forward(_Exchange):
    def __init__(self, arrays):
        self.inputs = list(arrays)
        self.out_shape = [jax.ShapeDtypeStruct(a.shape, a.dtype) for a in arrays]
        self.aliases = {t: t for t in range(len(arrays))}
        self.sems = [DMA_SEMS((len(arrays), 3)), DMA_SEMS((len(arrays), 3))]

    def _copies(self, ins, outs, sems):
        send, recv = sems
        (x, y, c), me, sibling, chips, ids = _place()
        outgoing, incoming = [], []
        for t, a in enumerate(self.inputs):
            for k in range(3):
                mine = outs[t].at[ids[k], _half(a.shape[1], c), :]
                theirs = outs[t].at[ids[k], _half(a.shape[1], 1 - c), :]
                outgoing.append(_remote(mine, mine, send.at[t, k], recv.at[t, k], sibling))
                incoming.append(_remote(theirs, theirs, send.at[t, k], recv.at[t, k], sibling))
        return [], outgoing, incoming


class _GatherForward(_Exchange):
    def __init__(self, shards):
        self.gather = _Gather(shards, [True] * len(shards))
        self.forward = _Forward(self.gather.out_shape)
        self.inputs, self.out_shape = self.gather.inputs, self.gather.out_shape
        self.sems = self.gather.sems + self.forward.sems

    def start(self, ins, outs, sems):
        self.gather.start(ins, outs, sems[:3])

    def finish(self, ins, outs, sems):
        local, outgoing, incoming = self.gather._copies(ins, outs, sems[:3])
        _, passed, from_sibling = self.forward._copies(outs, outs, sems[3:])
        for landed, onward in zip(incoming, passed):
            landed.wait_recv()
            onward.start()
        for cp in from_sibling:
            cp.wait_recv()
        for cp in outgoing + passed:
            cp.wait_send()
        for cp in local:
            cp.wait()


class _Pair(_Exchange):
    def __init__(self, grads):
        self.inputs = list(grads)
        self.out_shape = [jax.ShapeDtypeStruct((g.shape[0], g.shape[1] // 2, g.shape[2]), g.dtype) for g in grads]
        self.sems = [DMA_SEMS((len(grads),)), DMA_SEMS((len(grads),))]

    def _copies(self, ins, outs, sems):
        send, recv = sems
        (x, y, c), me, sibling, chips, ids = _place()
        cps = [_remote(ins[t].at[:, _half(g.shape[1], 1 - c), :], outs[t], send.at[t], recv.at[t], sibling)
               for t, g in enumerate(self.inputs)]
        return [], cps, cps


class _Chip(_Exchange):
    def __init__(self, parts):
        self.inputs = list(parts)
        self.out_shape = [jax.ShapeDtypeStruct(p.shape, p.dtype) for p in parts]
        self.sems = [DMA_SEMS((len(parts), 3)), DMA_SEMS((len(parts), 3)), DMA_SEMS((len(parts),))]

    def _copies(self, ins, outs, sems):
        send, recv, lsem = sems
        (x, y, c), me, sibling, chips, ids = _place()
        local = [pltpu.make_async_copy(ins[t].at[me], outs[t].at[me], lsem.at[t]) for t in range(len(ins))]
        outgoing, incoming = [], []
        for t in range(len(ins)):
            for k in range(3):
                to = (*chips[k], c)
                outgoing.append(_remote(ins[t].at[ids[k]], outs[t].at[me], send.at[t, k], recv.at[t, k], to))
                incoming.append(_remote(ins[t].at[ids[k]], outs[t].at[ids[k]], send.at[t, k], recv.at[t, k], to))
        return local, outgoing, incoming


class _Share(_Exchange):
    def __init__(self, grads):
        self.inputs = list(grads)
        self.out_shape = [jax.ShapeDtypeStruct(g.shape, g.dtype) for g in grads]
        self.aliases = {t: t for t in range(len(grads))}
        self.sems = [DMA_SEMS((len(grads),)), DMA_SEMS((len(grads),))]

    def _copies(self, ins, outs, sems):
        send, recv = sems
        (x, y, c), me, sibling, chips, ids = _place()
        outgoing, incoming = [], []
        for t, g in enumerate(self.inputs):
            mine = outs[t].at[_half(g.shape[0], c), :]
            theirs = outs[t].at[_half(g.shape[0], 1 - c), :]
            outgoing.append(_remote(mine, mine, send.at[t], recv.at[t], sibling))
            incoming.append(_remote(theirs, theirs, send.at[t], recv.at[t], sibling))
        return [], outgoing, incoming


class _GatherSmall(_Exchange):
    def __init__(self, block):
        self.inputs = [block]
        self.out_shape = [jax.ShapeDtypeStruct((N_DEV, *block.shape), block.dtype)]
        self.sems = [DMA_SEMS((7,)), DMA_SEMS((7,)), DMA_SEMS(())]

    def _copies(self, ins, outs, sems):
        send, recv, lsem = sems
        (x, y, c), me, sibling, chips, ids = _place()
        slot = lambda px, py, pc: outs[0].at[4 * px + 2 * py + pc]

        def copy(k, block, to, src=None):
            return _remote(slot(*block) if src is None else src, slot(*block), send.at[k], recv.at[k], to)

        local = [pltpu.make_async_copy(ins[0], slot(x, y, c), lsem)]
        first = [copy(0, (x, y, c), sibling, src=ins[0])] + [copy(1 + j, (x, y, c), (*chip, c), src=ins[0]) for j, chip in enumerate(chips)]
        passed = [copy(4 + j, (*chip, c), sibling) for j, chip in enumerate(chips)]
        landed = [copy(1 + j, (*chip, c), (x, y, c)) for j, chip in enumerate(chips)]
        from_sibling = [copy(0, (x, y, 1 - c), (x, y, c))] + [copy(4 + j, (*chip, 1 - c), (x, y, c)) for j, chip in enumerate(chips)]
        return local, first, (passed, landed, from_sibling)

    def finish(self, ins, outs, sems):
        local, first, (passed, landed, from_sibling) = self._copies(ins, outs, sems)
        for j in range(3):
            landed[j].wait_recv()
            passed[j].start()
        for cp in from_sibling:
            cp.wait_recv()
        for cp in first + passed:
            cp.wait_send()
        for cp in local:
            cp.wait()


def _split_refs(refs, counts):
    out = []
    for n in counts:
        out.append(refs[:n])
        refs = refs[n:]
    return out


def _each_exchange(exchanges, method, x_in, x_out, x_sem):
    for ex in exchanges:
        ni, no, ns = len(ex.inputs), len(ex.out_shape), len(ex.sems)
        getattr(ex, method)(x_in[:ni], x_out[:no], x_sem[:ns])
        x_in, x_out, x_sem = x_in[ni:], x_out[no:], x_sem[ns:]


def _call(body, *, name, grid, in_specs, out_specs, out_shape, operands, semantics, scratch_shapes=(), exchanges=()):
    x_in = [a for ex in exchanges for a in ex.inputs]
    x_out = [s for ex in exchanges for s in ex.out_shape]
    x_sem = [s for ex in exchanges for s in ex.sems]
    counts = (len(in_specs), len(x_in), len(out_specs), len(x_out), len(scratch_shapes), len(x_sem))
    aliases, i0, o0 = {}, len(in_specs), len(out_specs)
    for ex in exchanges:
        aliases.update({i0 + i: o0 + o for i, o in ex.aliases.items()})
        i0, o0 = i0 + len(ex.inputs), o0 + len(ex.out_shape)

    def full_body(*refs):
        ins, xi, outs, xo, scr, xs = _split_refs(list(refs), counts)
        if exchanges:
            @pl.when(functools.reduce(jnp.logical_and, [pl.program_id(a) == 0 for a in range(len(grid))]))
            def _():
                _each_exchange(exchanges, "start", xi, xo, xs)

        body(*ins, *outs, *scr)
        if exchanges:
            @pl.when(functools.reduce(jnp.logical_and, [pl.program_id(a) == grid[a] - 1 for a in range(len(grid))]))
            def _():
                _each_exchange(exchanges, "finish", xi, xo, xs)

    return pl.pallas_call(
        full_body, name=name, grid=grid,
        in_specs=list(in_specs) + [HBM] * len(x_in), out_specs=list(out_specs) + [HBM] * len(x_out),
        out_shape=list(out_shape) + x_out, scratch_shapes=list(scratch_shapes) + x_sem,
        input_output_aliases=aliases, compiler_params=_params(semantics),
    )(*operands, *x_in)


def _run_exchanges(exchanges, name):
    x_in = [a for ex in exchanges for a in ex.inputs]
    x_out = [s for ex in exchanges for s in ex.out_shape]
    x_sem = [s for ex in exchanges for s in ex.sems]
    aliases, i0, o0 = {}, 0, 0
    for ex in exchanges:
        aliases.update({i0 + i: o0 + o for i, o in ex.aliases.items()})
        i0, o0 = i0 + len(ex.inputs), o0 + len(ex.out_shape)

    def body(*refs):
        xi, xo, xs = _split_refs(list(refs), (len(x_in), len(x_out), len(x_sem)))
        _each_exchange(exchanges, "start", xi, xo, xs)
        _each_exchange(exchanges, "finish", xi, xo, xs)

    return pl.pallas_call(
        body, name=name, in_specs=[HBM] * len(x_in), out_specs=[HBM] * len(x_out), out_shape=x_out,
        scratch_shapes=x_sem, input_output_aliases=aliases,
    )(*x_in)


def _inproj_fwd(x, g1, w_in_all, exchanges=()):
    L, D = x.shape
    ns, _, nc = w_in_all.shape
    tm = _tile(L, TM_PROJ)

    def body(x_ref, g_ref, w_ref, hn_ref, proj_ref, u_ref):
        hn = _rms_fwd(x_ref[...], g_ref[...]).astype(MXU_DTYPE)
        hn_ref[...] = hn
        for j in range(ns):
            proj_ref[:, j * nc:(j + 1) * nc] = _dot(hn, w_ref[j])
        _store_slabs(u_ref, proj_ref[:, 0:nc])

    return _call(
        body, name="inproj_fwd", grid=(L // tm,), exchanges=exchanges, semantics=("arbitrary",), operands=(x, g1, w_in_all),
        in_specs=[pl.BlockSpec((tm, D), lambda i: (i, 0)), _resident((1, D)), _resident(w_in_all.shape)],
        out_specs=[pl.BlockSpec((tm, D), lambda i: (i, 0)), pl.BlockSpec((tm, ns * nc), lambda i: (i, 0)), _slab_spec(nc, tm)],
        out_shape=[jax.ShapeDtypeStruct((L, D), MXU_DTYPE), jax.ShapeDtypeStruct((L, ns * nc), F32), _slab_shape(L, nc)],
    )


def _slab_shape(L, n):
    return jax.ShapeDtypeStruct((n // LANES, L, LANES), F32)


def _slab_spec(n, tm, index=lambda i: (0, i, 0)):
    return pl.BlockSpec((n // LANES, tm, LANES), index)


def _store_slabs(ref, value):
    for k in range(ref.shape[0]):
        ref[k] = value[:, k * LANES:(k + 1) * LANES]


def _load_slabs(ref):
    return jnp.concatenate([ref[k] for k in range(ref.shape[0])], axis=1)


SEG_ROWS = SUBLANES * SUBLANES


def _load_permuted(ref):
    tm = ref.shape[1]
    slabs = []
    for k in range(ref.shape[0]):
        tiles = [ref.at[k][pl.ds(b * SEG_ROWS + j, SUBLANES, stride=SUBLANES), :] for b in range(tm // SEG_ROWS) for j in range(SUBLANES)]
        slabs.append(jnp.concatenate(tiles, axis=0))
    return jnp.concatenate(slabs, axis=1)


def _store_permuted(ref, value):
    tm = ref.shape[1]
    for k in range(ref.shape[0]):
        for b in range(tm // SEG_ROWS):
            for j in range(SUBLANES):
                r = b * SEG_ROWS + j * SUBLANES
                ref.at[k][pl.ds(b * SEG_ROWS + j, SUBLANES, stride=SUBLANES), :] = value[r:r + SUBLANES, k * LANES:(k + 1) * LANES]


def _scan_tile(xr, xi, hr, hi, coef_ref, lanes, reverse):
    for k, j in ((1, 0), (2, 2), (4, 4)):
        ar = coef_ref[0, j, :, lanes]
        ai = coef_ref[0, j + 1, :, lanes]
        shift = SUBLANES - k if reverse else k
        sr = pltpu.roll(xr, shift, 0)
        si = pltpu.roll(xi, shift, 0)
        xr, xi = xr + (ar * sr - ai * si), xi + (ar * si + ai * sr)
    pr = coef_ref[0, 6, :, lanes]
    pi = coef_ref[0, 7, :, lanes]
    return xr + (pr * hr - pi * hi), xi + (pr * hi + pi * hr)


def _scan_block(read, write, hr, hi, coef_ref, lanes, reverse):
    order = list(range(SUBLANES - 1, -1, -1) if reverse else range(SUBLANES))
    near = 8 + 2 * order[0]
    ar = coef_ref[0, near, :, lanes]
    ai = coef_ref[0, near + 1, :, lanes]
    xr, xi = read(order[0])
    local = {order[0]: (xr, xi)}
    for j in order[1:]:
        br, bi = read(j)
        xr, xi = br + (ar * xr - ai * xi), bi + (ar * xi + ai * xr)
        local[j] = (xr, xi)
    er, ei = _scan_tile(xr, xi, hr, hi, coef_ref, lanes, reverse)
    edge = lax.broadcasted_iota(jnp.int32, er.shape, 0) == (SUBLANES - 1 if reverse else 0)
    shift = SUBLANES - 1 if reverse else 1
    pr = jnp.where(edge, hr, pltpu.roll(er, shift, 0))
    pi = jnp.where(edge, hi, pltpu.roll(ei, shift, 0))
    for j in range(SUBLANES):
        cr = coef_ref[0, 8 + 2 * j, :, lanes]
        ci = coef_ref[0, 9 + 2 * j, :, lanes]
        xr, xi = local[j]
        write(j, xr + (cr * pr - ci * pi), xi + (cr * pi + ci * pr))
    end = 0 if reverse else SUBLANES - 1
    return jnp.broadcast_to(er[end:end + 1, :], er.shape), jnp.broadcast_to(ei[end:end + 1, :], ei.shape)


def _s5_fwd(u4, bmat, cmat, coef, dskip, exchanges=()):
    L = u4.shape[1]
    tm = _tile(L, TM_S5)
    lc = min(LANE_CHUNK, WB)

    def body(u_ref, bm_ref, cm_ref, coef_ref, d_ref, sre_ref, sim_ref, ys_ref, hr_ref, hi_ref):
        @pl.when(pl.program_id(1) == 0)
        def _():
            hr_ref[...] = jnp.zeros_like(hr_ref)
            hi_ref[...] = jnp.zeros_like(hi_ref)

        u = _load_permuted(u_ref)
        bu = _dot(u.astype(MXU_DTYPE), bm_ref[0])
        sre_ref[...] = bu[:, :WB]
        sim_ref[...] = bu[:, WB:]
        for c in range(WB // lc):
            lanes = slice(c * lc, (c + 1) * lc)
            hr, hi = hr_ref[:, lanes], hi_ref[:, lanes]
            for b in range(tm // SEG_ROWS):
                rows = lambda j, b=b: slice(b * SEG_ROWS + j * SUBLANES, b * SEG_ROWS + (j + 1) * SUBLANES)

                def read(j, rows=rows, lanes=lanes):
                    return sre_ref[rows(j), lanes], sim_ref[rows(j), lanes]

                def write(j, xr, xi, rows=rows, lanes=lanes):
                    sre_ref[rows(j), lanes] = xr
                    sim_ref[rows(j), lanes] = xi

                hr, hi = _scan_block(read, write, hr, hi, coef_ref, lanes, False)
            hr_ref[:, lanes] = hr
            hi_ref[:, lanes] = hi
        ys = _dot(sre_ref[...].astype(MXU_DTYPE), cm_ref[0, :WB, :]) + _dot(sim_ref[...].astype(MXU_DTYPE), cm_ref[0, WB:, :])
        _store_permuted(ys_ref, ys + d_ref[0] * u)

    return _call(
        body, name="s5_fwd", grid=(N_GBLK, L // tm), exchanges=exchanges, semantics=("arbitrary", "arbitrary"),
        operands=(u4, bmat, cmat, coef, dskip),
        in_specs=[
            _slab_spec(UB, tm, lambda b, i: (b, i, 0)),
            pl.BlockSpec((1, UB, 2 * WB), lambda b, i: (b, 0, 0)),
            pl.BlockSpec((1, 2 * WB, UB), lambda b, i: (b, 0, 0)),
            pl.BlockSpec((1, N_TABLES, SUBLANES, WB), lambda b, i: (b, 0, 0, 0)),
            pl.BlockSpec((1, 1, UB), lambda b, i: (b, 0, 0)),
        ],
        out_specs=[
            pl.BlockSpec((tm, WB), lambda b, i: (i, b)),
            pl.BlockSpec((tm, WB), lambda b, i: (i, b)),
            _slab_spec(UB, tm, lambda b, i: (b, i, 0)),
        ],
        out_shape=[
            jax.ShapeDtypeStruct((L, N_GBLK * WB), F32),
            jax.ShapeDtypeStruct((L, N_GBLK * WB), F32),
            _slab_shape(L, D_SSM),
        ],
        scratch_shapes=[pltpu.VMEM((SUBLANES, WB), F32), pltpu.VMEM((SUBLANES, WB), F32)],
    )


def _tail_fwd(x, ys, proj, w_glu, conv_w, g_ssm, g_conv, w_out, g_post, exchanges=()):
    L, D = x.shape
    tm = _tile(L, TM_TAIL)

    def body(x_ref, ys_ref, h_ref, bg_ref, cg_ref, wglu_ref, cw_ref, gs_ref, gc_ref, wout_ref, gp_ref,
             ycat_ref, o_ref, x1_ref, zbuf):
        @pl.when(pl.program_id(0) == 0)
        def _():
            zbuf[0:SUBLANES, :] = jnp.zeros((SUBLANES, D_CONV), F32)

        y1, _ = _gelu(_load_slabs(ys_ref))
        y2 = y1 * jax.nn.sigmoid(_dot(y1.astype(MXU_DTYPE), wglu_ref[...]))
        ycat_ref[:, :D_SSM] = _rms_fwd(y2, gs_ref[...]).astype(MXU_DTYPE)
        z = cg_ref[...] * h_ref[...]
        zbuf[SUBLANES:, :] = z
        conv = cw_ref[0:1, :] * zbuf[SUBLANES - 2:SUBLANES - 2 + tm, :] + cw_ref[1:2, :] * zbuf[SUBLANES - 1:SUBLANES - 1 + tm, :] + cw_ref[2:3, :] * z
        zbuf[0:SUBLANES, :] = zbuf[tm:tm + SUBLANES, :]
        ycat_ref[:, D_SSM:] = _rms_fwd(bg_ref[...] * conv, gc_ref[...]).astype(MXU_DTYPE)
        o = _dot(ycat_ref[...], wout_ref[...])
        o_ref[...] = o
        x1_ref[...] = x_ref[...] + _rms_fwd(o, gp_ref[...])

    row = lambda i: (i, 0)
    return _call(
        body, name="tail_fwd", grid=(L // tm,), exchanges=exchanges, semantics=("arbitrary",),
        operands=(x, ys, proj, proj, proj, w_glu, conv_w, g_ssm, g_conv, w_out, g_post),
        in_specs=[
            pl.BlockSpec((tm, D), row), _slab_spec(D_SSM, tm),
            pl.BlockSpec((tm, D_CONV), lambda i: (i, 1)), pl.BlockSpec((tm, D_CONV), lambda i: (i, 2)),
            pl.BlockSpec((tm, D_CONV), lambda i: (i, 3)),
            _resident(w_glu.shape), _resident(conv_w.shape), _resident(g_ssm.shape), _resident(g_conv.shape),
            _resident(w_out.shape), _resident(g_post.shape),
        ],
        out_specs=[pl.BlockSpec((tm, D), row), pl.BlockSpec((tm, D), row), pl.BlockSpec((tm, D), row)],
        out_shape=[jax.ShapeDtypeStruct((L, D), MXU_DTYPE), jax.ShapeDtypeStruct((L, D), F32), jax.ShapeDtypeStruct((L, D), F32)],
        scratch_shapes=[pltpu.VMEM((tm + SUBLANES, D_CONV), F32)],
    )


def _mlp_fwd(x1, target, w_up_all, w_down, g_pre, g_post):
    L, D = x1.shape
    ns, _, fc = w_up_all.shape
    tm = _tile(L, TM_MLP)

    def body(x1_ref, t_ref, wup_ref, wdn_ref, gpre_ref, gpost_ref, hn2_ref, up_ref, m_ref, dx2_ref, loss_ref):
        @pl.when(pl.program_id(0) == 0)
        def _():
            loss_ref[...] = jnp.zeros_like(loss_ref)

        x1v = x1_ref[...]
        hn2 = _rms_fwd(x1v, gpre_ref[...]).astype(MXU_DTYPE)
        hn2_ref[...] = hn2
        m = jnp.zeros((tm, D), F32)
        for j in range(ns):
            up = _dot(hn2, wup_ref[j])
            up_ref[:, j * fc:(j + 1) * fc] = up
            act = jnp.square(jnp.maximum(up, 0.0)).astype(MXU_DTYPE)
            m = m + _dot(act, wdn_ref[j * fc:(j + 1) * fc, :])
        m_ref[...] = m
        err = x1v + _rms_fwd(m, gpost_ref[...]) - t_ref[...]
        loss_ref[...] += 0.5 * jnp.sum(jnp.mean(err * err, axis=-1, keepdims=True))
        dx2_ref[...] = err * (1.0 / D)

    row = lambda i: (i, 0)
    return pl.pallas_call(
        body, name="mlp_fwd", grid=(L // tm,),
        in_specs=[pl.BlockSpec((tm, D), row), pl.BlockSpec((tm, D), row), _resident(w_up_all.shape), _resident(w_down.shape),
                  _resident(g_pre.shape), _resident(g_post.shape)],
        out_specs=[pl.BlockSpec((tm, D), row), pl.BlockSpec((tm, ns * fc), row), pl.BlockSpec((tm, D), row),
                   pl.BlockSpec((tm, D), row), pl.BlockSpec((SUBLANES, 128), lambda i: (0, 0))],
        out_shape=[jax.ShapeDtypeStruct((L, D), MXU_DTYPE), jax.ShapeDtypeStruct((L, ns * fc), F32), jax.ShapeDtypeStruct((L, D), F32),
                   jax.ShapeDtypeStruct((L, D), F32), jax.ShapeDtypeStruct((SUBLANES, 128), F32)],
        compiler_params=_params(("arbitrary",)),
    )(x1, target, w_up_all, w_down, g_pre, g_post)


def _mlp_bwd(dx2, m, up, x1, w_up_all, w_down, g_pre, g_post):
    L, D = x1.shape
    ns, _, fc = w_up_all.shape
    tm = _tile(L, TM_MLP)

    def body(dx2_ref, m_ref, up_ref, x1_ref, wup_ref, wdn_ref, gpre_ref, gpost_ref,
             dm_ref, dup_ref, act_ref, dx1_ref, dgpost_ref, dgpre_ref):
        @pl.when(pl.program_id(0) == 0)
        def _():
            dgpost_ref[...] = jnp.zeros_like(dgpost_ref)
            dgpre_ref[...] = jnp.zeros_like(dgpre_ref)

        dx2v = dx2_ref[...]
        dm, dg = _rms_bwd(m_ref[...], gpost_ref[...], dx2v)
        dgpost_ref[...] += dg
        dm_b = dm.astype(MXU_DTYPE)
        dm_ref[...] = dm_b
        dhn2 = jnp.zeros((tm, D), F32)
        for j in range(ns):
            cols = slice(j * fc, (j + 1) * fc)
            relu = jnp.maximum(up_ref[:, cols], 0.0)
            act_ref[:, cols] = jnp.square(relu).astype(MXU_DTYPE)
            dup = (_dot_nt(dm_b, wdn_ref[cols, :]) * (2.0 * relu)).astype(MXU_DTYPE)
            dup_ref[:, cols] = dup
            dhn2 = dhn2 + _dot_nt(dup, wup_ref[j])
        dx, dg = _rms_bwd(x1_ref[...], gpre_ref[...], dhn2)
        dgpre_ref[...] += dg
        dx1_ref[...] = dx2v + dx

    row = lambda i: (i, 0)
    vec = pl.BlockSpec((1, D), lambda i: (0, 0))
    return pl.pallas_call(
        body, name="mlp_bwd", grid=(L // tm,),
        in_specs=[pl.BlockSpec((tm, D), row), pl.BlockSpec((tm, D), row), pl.BlockSpec((tm, ns * fc), row), pl.BlockSpec((tm, D), row),
                  _resident(w_up_all.shape), _resident(w_down.shape), _resident(g_pre.shape), _resident(g_post.shape)],
        out_specs=[pl.BlockSpec((tm, D), row), pl.BlockSpec((tm, ns * fc), row), pl.BlockSpec((tm, ns * fc), row),
                   pl.BlockSpec((tm, D), row), vec, vec],
        out_shape=[jax.ShapeDtypeStruct((L, D), MXU_DTYPE), jax.ShapeDtypeStruct((L, ns * fc), MXU_DTYPE),
                   jax.ShapeDtypeStruct((L, ns * fc), MXU_DTYPE), jax.ShapeDtypeStruct((L, D), F32),
                   jax.ShapeDtypeStruct((1, D), F32), jax.ShapeDtypeStruct((1, D), F32)],
        compiler_params=_params(("arbitrary",)),
    )(dx2, m, up, x1, w_up_all, w_down, g_pre, g_post)


def _tail_bwd(dx1, o, ys, proj, w_glu, conv_w, g_ssm, g_conv, w_out, g_post, exchanges=()):
    L, D = dx1.shape
    tm = _tile(L, TM_TAIL)
    nt = L // tm
    hb = tm // SUBLANES

    def body(dx1_ref, o_ref, ys_ref, h_ref, bg_ref, cg_ref, hh_ref, hcg_ref, wglu_ref, cw_ref, gs_ref, gc_ref, wout_ref, gp_ref,
             do_ref, da_ref, y1_ref, dys_ref, dhbc_ref, dgp_ref, dgs_ref, dgc_ref, dcw_ref, zbuf, dcbuf):
        step = pl.program_id(0)

        @pl.when(step == 0)
        def _():
            dcbuf[tm:, :] = jnp.zeros((SUBLANES, D_CONV), F32)
            dgp_ref[...] = jnp.zeros_like(dgp_ref)
            dgs_ref[...] = jnp.zeros_like(dgs_ref)
            dgc_ref[...] = jnp.zeros_like(dgc_ref)
            dcw_ref[...] = jnp.zeros_like(dcw_ref)

        do, dg = _rms_bwd(o_ref[...], gp_ref[...], dx1_ref[...])
        dgp_ref[...] += dg
        do_b = do.astype(MXU_DTYPE)
        do_ref[...] = do_b
        dycat = _dot_nt(do_b, wout_ref[...])
        y1, dgelu = _gelu(_load_slabs(ys_ref))
        y1_b = y1.astype(MXU_DTYPE)
        y1_ref[...] = y1_b
        s = jax.nn.sigmoid(_dot(y1_b, wglu_ref[...]))
        dy2, dg = _rms_bwd(y1 * s, gs_ref[...], dycat[:, :D_SSM])
        dgs_ref[...] += dg
        da_b = (dy2 * y1 * s * (1.0 - s)).astype(MXU_DTYPE)
        da_ref[...] = da_b
        _store_slabs(dys_ref, (dy2 * s + _dot_nt(da_b, wglu_ref[...])) * dgelu)
        h = h_ref[...]
        cg = cg_ref[...]
        bg = bg_ref[...]
        z = cg * h
        first = step == nt - 1
        zbuf[0:SUBLANES, :] = jnp.where(first, 0.0, hcg_ref[...] * hh_ref[...])
        zbuf[SUBLANES:, :] = z
        z1 = zbuf[SUBLANES - 1:SUBLANES - 1 + tm, :]
        z2 = zbuf[SUBLANES - 2:SUBLANES - 2 + tm, :]
        conv = cw_ref[0:1, :] * z2 + cw_ref[1:2, :] * z1 + cw_ref[2:3, :] * z
        dyc, dg = _rms_bwd(bg * conv, gc_ref[...], dycat[:, D_SSM:])
        dgc_ref[...] += dg
        dconv = dyc * bg
        dcw_ref[0:1, :] += jnp.sum(dconv * z2, axis=0, keepdims=True)
        dcw_ref[1:2, :] += jnp.sum(dconv * z1, axis=0, keepdims=True)
        dcw_ref[2:3, :] += jnp.sum(dconv * z, axis=0, keepdims=True)
        dcbuf[0:tm, :] = dconv
        dz = cw_ref[2:3, :] * dconv + cw_ref[1:2, :] * dcbuf[1:1 + tm, :] + cw_ref[0:1, :] * dcbuf[2:2 + tm, :]
        dcbuf[tm:, :] = dcbuf[0:SUBLANES, :]
        dhbc_ref[:, 0:D_CONV] = (dz * cg).astype(MXU_DTYPE)
        dhbc_ref[:, D_CONV:2 * D_CONV] = (dyc * conv).astype(MXU_DTYPE)
        dhbc_ref[:, 2 * D_CONV:] = (dz * h).astype(MXU_DTYPE)

    rev = lambda i: (nt - 1 - i, 0)
    slab = lambda i: (0, nt - 1 - i, 0)
    col = lambda c: (lambda i: (nt - 1 - i, c))
    halo = lambda c: (lambda i: (jnp.maximum((nt - 1 - i) * hb - 1, 0), c))
    vec = lambda n: pl.BlockSpec((1, n), lambda i: (0, 0))
    return _call(
        body, name="tail_bwd", grid=(nt,), exchanges=exchanges, semantics=("arbitrary",),
        operands=(dx1, o, ys, proj, proj, proj, proj, proj, w_glu, conv_w, g_ssm, g_conv, w_out, g_post),
        in_specs=[
            pl.BlockSpec((tm, D), rev), pl.BlockSpec((tm, D), rev), _slab_spec(D_SSM, tm, slab),
            pl.BlockSpec((tm, D_CONV), col(1)), pl.BlockSpec((tm, D_CONV), col(2)), pl.BlockSpec((tm, D_CONV), col(3)),
            pl.BlockSpec((SUBLANES, D_CONV), halo(1)), pl.BlockSpec((SUBLANES, D_CONV), halo(3)),
            _resident(w_glu.shape), _resident(conv_w.shape), _resident(g_ssm.shape), _resident(g_conv.shape),
            _resident(w_out.shape), _resident(g_post.shape),
        ],
        out_specs=[
            pl.BlockSpec((tm, D), rev), pl.BlockSpec((tm, D_SSM), rev), pl.BlockSpec((tm, D_SSM), rev), _slab_spec(D_SSM, tm, slab),
            pl.BlockSpec((tm, 3 * D_CONV), rev), vec(D), vec(D_SSM), vec(D_CONV),
            pl.BlockSpec((SUBLANES, D_CONV), lambda i: (0, 0)),
        ],
        out_shape=[
            jax.ShapeDtypeStruct((L, D), MXU_DTYPE), jax.ShapeDtypeStruct((L, D_SSM), MXU_DTYPE), jax.ShapeDtypeStruct((L, D_SSM), MXU_DTYPE),
            _slab_shape(L, D_SSM), jax.ShapeDtypeStruct((L, 3 * D_CONV), MXU_DTYPE),
            jax.ShapeDtypeStruct((1, D), F32), jax.ShapeDtypeStruct((1, D_SSM), F32), jax.ShapeDtypeStruct((1, D_CONV), F32),
            jax.ShapeDtypeStruct((SUBLANES, D_CONV), F32),
        ],
        scratch_shapes=[pltpu.VMEM((tm + SUBLANES, D_CONV), F32), pltpu.VMEM((tm + SUBLANES, D_CONV), F32)],
    )


def _s5_bwd(dys, u4, s_re, s_im, bmat, cmat, coef_rev, dskip, exchanges=()):
    L = dys.shape[1]
    tm = _tile(L, TM_S5)
    nt = L // tm
    lc = min(LANE_CHUNK, WB)

    def body(dys_ref, u_ref, sre_ref, sim_ref, bm_ref, cm_ref, coef_ref, d_ref,
             du_ref, gb_ref, gc_ref, q_ref, gd_ref, dr_ref, di_ref, lr_ref, li_ref, hr_ref, hi_ref, qr_acc, qi_acc):
        step = pl.program_id(1)

        @pl.when(step == 0)
        def _():
            for ref in (hr_ref, hi_ref, qr_acc, qi_acc, gb_ref, gc_ref, gd_ref):
                ref[...] = jnp.zeros_like(ref)

        dys_v = _load_permuted(dys_ref)
        u = _load_permuted(u_ref)
        dys_b = dys_v.astype(MXU_DTYPE)
        u_b = u.astype(MXU_DTYPE)
        d = _dot_nt(dys_b, cm_ref[0])
        dr_ref[...] = d[:, :WB]
        di_ref[...] = d[:, WB:]
        for c in range(WB // lc):
            lanes = slice(c * lc, (c + 1) * lc)
            hr, hi = hr_ref[:, lanes], hi_ref[:, lanes]
            q = [qr_acc[:, lanes], qi_acc[:, lanes]]
            for b in range(tm // SEG_ROWS - 1, -1, -1):
                rows = lambda j, b=b: slice(b * SEG_ROWS + j * SUBLANES, b * SEG_ROWS + (j + 1) * SUBLANES)

                def read(j, rows=rows, lanes=lanes):
                    return dr_ref[rows(j), lanes], di_ref[rows(j), lanes]

                def write(j, xr, xi, rows=rows, lanes=lanes, q=q):
                    lr_ref[rows(j), lanes] = xr
                    li_ref[rows(j), lanes] = xi
                    er = xr - dr_ref[rows(j), lanes]
                    ei = xi - di_ref[rows(j), lanes]
                    sr = sre_ref[rows(j), lanes]
                    si = sim_ref[rows(j), lanes]
                    q[0] = q[0] + (er * sr + ei * si)
                    q[1] = q[1] + (ei * sr - er * si)

                hr, hi = _scan_block(read, write, hr, hi, coef_ref, lanes, True)
            hr_ref[:, lanes] = hr
            hi_ref[:, lanes] = hi
            qr_acc[:, lanes] = q[0]
            qi_acc[:, lanes] = q[1]
        lr_b = lr_ref[...].astype(MXU_DTYPE)
        li_b = li_ref[...].astype(MXU_DTYPE)
        _store_permuted(du_ref, _dot_nt(lr_b, bm_ref[0, :, :WB]) + _dot_nt(li_b, bm_ref[0, :, WB:]) + d_ref[0] * dys_v)
        gb_ref[0, :WB, :] += _dot_tn(lr_b, u_b)
        gb_ref[0, WB:, :] += _dot_tn(li_b, u_b)
        gc_ref[0, :, :WB] += _dot_tn(dys_b, sre_ref[...].astype(MXU_DTYPE))
        gc_ref[0, :, WB:] += _dot_tn(dys_b, sim_ref[...].astype(MXU_DTYPE))
        gd_ref[0] += jnp.sum(dys_v * u, axis=0, keepdims=True)

        @pl.when(step == nt - 1)
        def _():
            q_ref[0, 0:1, :] = jnp.sum(qr_acc[...], axis=0, keepdims=True)
            q_ref[0, 1:2, :] = jnp.sum(qi_acc[...], axis=0, keepdims=True)

    rev = lambda b, i: (nt - 1 - i, b)
    slab = lambda b, i: (b, nt - 1 - i, 0)
    blk = lambda b, i: (b, 0, 0)
    return _call(
        body, name="s5_bwd", grid=(N_GBLK, nt), exchanges=exchanges, semantics=("arbitrary", "arbitrary"),
        operands=(dys, u4, s_re, s_im, bmat, cmat, coef_rev, dskip),
        in_specs=[
            _slab_spec(UB, tm, slab), _slab_spec(UB, tm, slab), pl.BlockSpec((tm, WB), rev), pl.BlockSpec((tm, WB), rev),
            pl.BlockSpec((1, UB, 2 * WB), blk), pl.BlockSpec((1, 2 * WB, UB), blk),
            pl.BlockSpec((1, N_TABLES, SUBLANES, WB), lambda b, i: (b, 0, 0, 0)), pl.BlockSpec((1, 1, UB), blk),
        ],
        out_specs=[
            _slab_spec(UB, tm, slab), pl.BlockSpec((1, 2 * WB, UB), blk), pl.BlockSpec((1, UB, 2 * WB), blk),
            pl.BlockSpec((1, 2, WB), blk), pl.BlockSpec((1, 1, UB), blk),
        ],
        out_shape=[
            _slab_shape(L, D_SSM), jax.ShapeDtypeStruct((N_GBLK, 2 * WB, UB), F32),
            jax.ShapeDtypeStruct((N_GBLK, UB, 2 * WB), F32), jax.ShapeDtypeStruct((N_GBLK, 2, WB), F32),
            jax.ShapeDtypeStruct((N_GBLK, 1, UB), F32),
        ],
        scratch_shapes=[pltpu.VMEM((tm, WB), F32)] * 4 + [pltpu.VMEM((SUBLANES, WB), F32)] * 4,
    )


def _inproj_bwd(du, dhbc, x, dx1, w_in_all, g1):
    L, D = x.shape
    ns, _, nc = w_in_all.shape
    tm = _tile(L, TM_PROJ)

    def body(du_ref, dhbc_ref, x_ref, dx1_ref, w_ref, g_ref, gx_ref, dproj_ref, dg_ref):
        @pl.when(pl.program_id(0) == 0)
        def _():
            dg_ref[...] = jnp.zeros_like(dg_ref)

        du_b = _load_slabs(du_ref).astype(MXU_DTYPE)
        dproj_ref[:, :nc] = du_b
        dproj_ref[:, nc:] = dhbc_ref[...]
        dhn = _dot_nt(du_b, w_ref[0])
        for j in range(1, ns):
            dhn = dhn + _dot_nt(dhbc_ref[:, (j - 1) * nc:j * nc], w_ref[j])
        dx, dg = _rms_bwd(x_ref[...], g_ref[...], dhn)
        dg_ref[...] += dg
        gx_ref[...] = dx1_ref[...] + dx

    row = lambda i: (i, 0)
    return pl.pallas_call(
        body, name="inproj_bwd", grid=(L // tm,),
        in_specs=[_slab_spec(nc, tm), pl.BlockSpec((tm, (ns - 1) * nc), row), pl.BlockSpec((tm, D), row), pl.BlockSpec((tm, D), row),
                  _resident(w_in_all.shape), _resident(g1.shape)],
        out_specs=[pl.BlockSpec((tm, D), row), pl.BlockSpec((tm, ns * nc), row), pl.BlockSpec((1, D), lambda i: (0, 0))],
        out_shape=[jax.ShapeDtypeStruct((L, D), F32), jax.ShapeDtypeStruct((L, ns * nc), MXU_DTYPE), jax.ShapeDtypeStruct((1, D), F32)],
        compiler_params=_params(("arbitrary",)),
    )(du, dhbc, x, dx1, w_in_all, g1)


def _matmul_tn(a, b, name, col_shards=1, exchanges=()):
    L, K = a.shape
    N = b.shape[1]
    tl = _tile(L, TL_TN)
    tk = _tile(K, 1024)
    nw = N // col_shards
    tn = _tile(nw, 1024)
    npb = nw // tn

    def body(a_ref, b_ref, o_ref):
        @pl.when(pl.program_id(2) == 0)
        def _():
            o_ref[...] = jnp.zeros_like(o_ref)

        o_ref[0] += _dot_tn(a_ref[...], b_ref[...])

    return _call(
        body, name=name, grid=(K // tk, N // tn, L // tl), exchanges=exchanges, semantics=("arbitrary", "arbitrary", "arbitrary"),
        operands=(a, b),
        in_specs=[pl.BlockSpec((tl, tk), lambda k, n, l: (l, k)), pl.BlockSpec((tl, tn), lambda k, n, l: (l, n))],
        out_specs=[pl.BlockSpec((1, tk, tn), lambda k, n, l: (n // npb, k, n % npb))],
        out_shape=[jax.ShapeDtypeStruct((col_shards, K, nw), F32)],
    )


def _ssm_discretize(lam_re, lam_im, log_dt, b_re, b_im):
    dt = jnp.exp(log_dt)[:, None]
    zr = lam_re * dt
    zi = lam_im * dt
    mag = jnp.exp(zr)
    abr = mag * jnp.cos(zi)
    abi = mag * jnp.sin(zi)
    nr, ni = abr - 1.0, abi
    den = lam_re * lam_re + lam_im * lam_im
    coef_r = (nr * lam_re + ni * lam_im) / den
    coef_i = (ni * lam_re - nr * lam_im) / den
    bbar_r = coef_r[..., None] * b_re - coef_i[..., None] * b_im
    bbar_i = coef_r[..., None] * b_im + coef_i[..., None] * b_re
    return zr, zi, bbar_r, bbar_i


def _block_diag(t):
    nb, g, r, c = t.shape
    eye = jnp.eye(g, dtype=t.dtype)
    return (t[:, :, :, None, :] * eye[None, :, None, :, None]).reshape(nb, g * r, g * c)


def _diag_blocks(m, r, c):
    nb = m.shape[0]
    g = m.shape[1] // r
    eye = jnp.eye(g, dtype=m.dtype)
    t = (m.reshape(nb, g, r, g, c) * eye[None, :, None, :, None]).sum(axis=3)
    return t.reshape(nb * g, r, c)


def _powers(a):
    def mul(p, q):
        return (p[0] * q[0] - p[1] * q[1], p[0] * q[1] + p[1] * q[0])

    a2 = mul(a, a)
    a3 = mul(a2, a)
    a4 = mul(a2, a2)
    return [a, a2, a3, a4, mul(a4, a), mul(a4, a2), mul(a4, a3), mul(a4, a4)]


def _scan_tables(ar, ai, reverse):
    pw = _powers((ar, -ai if reverse else ai))
    seg = _powers(pw[SUBLANES - 1])
    t = jnp.arange(SUBLANES)[:, None]
    tabs = []
    for k in (1, 2, 4):
        mask = (t + k <= SUBLANES - 1) if reverse else (t >= k)
        tabs += [jnp.where(mask, seg[k - 1][0][None, :], 0.0), jnp.where(mask, seg[k - 1][1][None, :], 0.0)]
    order = range(SUBLANES - 1, -1, -1) if reverse else range(SUBLANES)
    tabs += [jnp.stack([seg[j][0] for j in order]), jnp.stack([seg[j][1] for j in order])]
    for j in range(SUBLANES):
        p = pw[SUBLANES - 1 - j] if reverse else pw[j]
        tabs += [jnp.broadcast_to(p[0][None, :], (SUBLANES, p[0].shape[0])), jnp.broadcast_to(p[1][None, :], (SUBLANES, p[1].shape[0]))]
    coef = jnp.stack(tabs)
    return coef.reshape(N_TABLES, SUBLANES, N_GBLK, WB).transpose(2, 0, 1, 3)


def _ssm_matrices(lam_re, lam_im, log_dt, b_re, b_im, c_re, c_im):
    zr, zi, bbar_r, bbar_i = _ssm_discretize(lam_re, lam_im, log_dt, b_re, b_im)
    mag = jnp.exp(zr)
    ar = (mag * jnp.cos(zi)).reshape(-1)
    ai = (mag * jnp.sin(zi)).reshape(-1)
    blk = lambda t: t.reshape(N_GBLK, G_PER_BLK, *t.shape[1:])
    bmat = jnp.concatenate([_block_diag(blk(bbar_r).transpose(0, 1, 3, 2)), _block_diag(blk(bbar_i).transpose(0, 1, 3, 2))], axis=2)
    cmat = jnp.concatenate([_block_diag(blk(c_re).transpose(0, 1, 3, 2)), _block_diag(blk(-c_im).transpose(0, 1, 3, 2))], axis=1)
    return bmat.astype(MXU_DTYPE), cmat.astype(MXU_DTYPE), _scan_tables(ar, ai, False), _scan_tables(ar, ai, True)


def _ssm_param_grads(lam_re, lam_im, log_dt, b_re, b_im, gb, gc, q, gd):
    gbr = _diag_blocks(gb[:, :WB, :], STATE, GROUP)
    gbi = _diag_blocks(gb[:, WB:, :], STATE, GROUP)
    d_c_re = _diag_blocks(gc[:, :, :WB], GROUP, STATE)
    d_c_im = -_diag_blocks(gc[:, :, WB:], GROUP, STATE)
    qr = q[:, 0, :].reshape(N_GROUPS, STATE)
    qi = q[:, 1, :].reshape(N_GROUPS, STATE)
    _, vjp = jax.vjp(_ssm_discretize, lam_re, lam_im, log_dt, b_re, b_im)
    d_lam_re, d_lam_im, d_log_dt, d_b_re, d_b_im = vjp((qr, qi, gbr, gbi))
    return d_lam_re, d_lam_im, d_log_dt, d_b_re, d_b_im, d_c_re, d_c_im, gd.reshape(N_GROUPS, GROUP)


def _row_tile(rows, n):
    return _tile(rows, max(SUBLANES, (2 * 1024 * 1024) // (4 * n)))


def _pair_add(grad, other, core, name):
    ns, h, n = other.shape
    tr = _row_tile(h, n)
    nb = h // tr

    def body(c_ref, g_ref, o_ref, out_ref):
        out_ref[...] = (g_ref[...] + o_ref[...]).astype(WIRE_DTYPE)

    return pl.pallas_call(
        body, name=name,
        grid_spec=pltpu.PrefetchScalarGridSpec(
            num_scalar_prefetch=1, grid=(ns, nb),
            in_specs=[pl.BlockSpec((1, tr, n), lambda s, i, c: (s, c[0] * nb + i, 0)), pl.BlockSpec((1, tr, n), lambda s, i, c: (s, i, 0))],
            out_specs=pl.BlockSpec((1, tr, n), lambda s, i, c: (s, i, 0))),
        out_shape=jax.ShapeDtypeStruct(other.shape, WIRE_DTYPE),
        compiler_params=_params(("arbitrary", "arbitrary")),
    )(core, grad, other)


def _quad_sum(parts, core, name):
    ns, h, n = parts.shape
    tr = _row_tile(h, n)
    nb = h // tr

    def body(c_ref, p_ref, out_ref):
        p = [p_ref[k].astype(F32) for k in range(ns)]
        out_ref[...] = ((p[0] + p[1]) + p[2]) + p[3]

    return pl.pallas_call(
        body, name=name,
        grid_spec=pltpu.PrefetchScalarGridSpec(
            num_scalar_prefetch=1, grid=(nb,),
            in_specs=[pl.BlockSpec((ns, tr, n), lambda i, c: (0, i, 0))],
            out_specs=pl.BlockSpec((tr, n), lambda i, c: (c[0] * nb + i, 0))),
        out_shape=jax.ShapeDtypeStruct((2 * h, n), F32),
        compiler_params=_params(("arbitrary",)),
    )(core, parts)


def _sum_devices(blocks):
    nd, m, n = blocks.shape

    def body(b_ref, out_ref):
        total = b_ref[0]
        for d in range(1, nd):
            total = total + b_ref[d]
        out_ref[...] = total

    return pl.pallas_call(body, name="sum_devices", in_specs=[VMEM], out_specs=VMEM,
                          out_shape=jax.ShapeDtypeStruct((m, n), blocks.dtype))(blocks)


def _adamw_math(w, g, m, v):
    m = ADAM_B1 * m + (1.0 - ADAM_B1) * g
    v = ADAM_B2 * v + (1.0 - ADAM_B2) * jnp.square(g)
    m_hat = m / (1.0 - ADAM_B1 ** ADAM_STEP)
    v_hat = v / (1.0 - ADAM_B2 ** ADAM_STEP)
    delta = -ADAM_LR * (m_hat / (jnp.sqrt(v_hat) + ADAM_EPS) + ADAM_WD * w)
    return delta, m, v


def _adamw(w, g, m, v, name):
    r, n = w.shape
    tr = _row_tile(r, n)

    def body(w_ref, g_ref, m_ref, v_ref, d_ref, nm_ref, nv_ref):
        d_ref[...], nm_ref[...], nv_ref[...] = _adamw_math(w_ref[...], g_ref[...], m_ref[...], v_ref[...])

    spec = pl.BlockSpec((tr, n), lambda i: (i, 0))
    return pl.pallas_call(
        body, name=name, grid=(r // tr,), in_specs=[spec] * 4, out_specs=[spec] * 3,
        out_shape=[jax.ShapeDtypeStruct((r, n), F32)] * 3,
        compiler_params=_params(("arbitrary",)),
    )(w, g, m, v)


SMALL_SHAPES = {
    "g_pre_mix": (D_MODEL,), "lam_re": (N_GROUPS, STATE), "lam_im": (N_GROUPS, STATE), "log_dt": (N_GROUPS,),
    "b_re": (N_GROUPS, STATE, GROUP), "b_im": (N_GROUPS, STATE, GROUP), "c_re": (N_GROUPS, GROUP, STATE), "c_im": (N_GROUPS, GROUP, STATE),
    "d_skip": (N_GROUPS, GROUP), "conv_w": (3, D_CONV), "g_ssm_out": (D_SSM,), "g_conv_out": (D_CONV,),
    "g_post_mix": (D_MODEL,), "g_pre_mlp": (D_MODEL,), "g_post_mlp": (D_MODEL,),
}
LANES = 128
PACK_TILE = SUBLANES * LANES


def _pack_rows(name):
    n = math.prod(SMALL_SHAPES[name])
    return SUBLANES * (-(-n // PACK_TILE))


def _pack_offsets():
    offs, row = {}, 0
    for name in SMALL_SHAPES:
        offs[name] = row
        row += _pack_rows(name)
    return offs, row


def _pack_small(grads):
    parts = []
    for name in SMALL_SHAPES:
        flat = grads[name].reshape(-1)
        parts.append(jnp.pad(flat, (0, _pack_rows(name) * LANES - flat.shape[0])).reshape(-1, LANES))
    return jnp.concatenate(parts, axis=0)


def _lane_shape(name):
    n = math.prod(SMALL_SHAPES[name])
    return (n // LANES, LANES) if n % LANES == 0 else (1, n)


def _adamw_small(pack, conv_grad, w, m, v):
    names = list(SMALL_SHAPES)
    offs, _ = _pack_offsets()
    nn = len(names)

    def body(*refs):
        pack_ref, cg_ref = refs[0], refs[1]
        w_refs, m_refs, v_refs = refs[2:2 + nn], refs[2 + nn:2 + 2 * nn], refs[2 + 2 * nn:2 + 3 * nn]
        outs = refs[2 + 3 * nn:]
        for j, name in enumerate(names):
            r, n = w_refs[j].shape
            g = cg_ref[...] if name == "conv_w" else pack_ref[offs[name]:offs[name] + r, 0:n]
            delta, nm, nv = _adamw_math(w_refs[j][...], g, m_refs[j][...], v_refs[j][...])
            outs[j][...] = g
            outs[nn + j][...] = delta
            outs[2 * nn + j][...] = nm
            outs[3 * nn + j][...] = nv

    args = [pack, conv_grad] + [w[k] for k in names] + [m[k] for k in names] + [v[k] for k in names]
    res = pl.pallas_call(
        body, name="adamw_small", in_specs=[VMEM] * len(args), out_specs=[VMEM] * (4 * nn),
        out_shape=[jax.ShapeDtypeStruct(w[k].shape, F32) for k in names] * 4,
    )(*args)
    return [dict(zip(names, res[q * nn:(q + 1) * nn])) for q in range(4)]


WEIGHTS = ["g_pre_mix", "w_in", "lam_re", "lam_im", "log_dt", "b_re", "b_im", "c_re", "c_im", "d_skip", "w_glu", "conv_w",
           "g_ssm_out", "g_conv_out", "w_out", "g_post_mix", "g_pre_mlp", "w_up", "w_down", "g_post_mlp"]
BIG = ["w_in", "w_glu", "w_out", "w_up", "w_down"]


def kernel(x, g_pre_mix, w_in, lam_re, lam_im, log_dt, b_re, b_im, c_re, c_im, d_skip, w_glu, conv_w, g_ssm_out, g_conv_out, w_out, g_post_mix, g_pre_mlp, w_up, w_down, g_post_mlp, loss_target, m_g_pre_mix, m_w_in, m_lam_re, m_lam_im, m_log_dt, m_b_re, m_b_im, m_c_re, m_c_im, m_d_skip, m_w_glu, m_conv_w, m_g_ssm_out, m_g_conv_out, m_w_out, m_g_post_mix, m_g_pre_mlp, m_w_up, m_w_down, m_g_post_mlp, v_g_pre_mix, v_w_in, v_lam_re, v_lam_im, v_log_dt, v_b_re, v_b_im, v_c_re, v_c_im, v_d_skip, v_w_glu, v_conv_w, v_g_ssm_out, v_g_conv_out, v_w_out, v_g_post_mix, v_g_pre_mlp, v_w_up, v_w_down, v_g_post_mlp):
    w = dict(g_pre_mix=g_pre_mix, w_in=w_in, lam_re=lam_re, lam_im=lam_im, log_dt=log_dt, b_re=b_re, b_im=b_im, c_re=c_re, c_im=c_im,
             d_skip=d_skip, w_glu=w_glu, conv_w=conv_w, g_ssm_out=g_ssm_out, g_conv_out=g_conv_out, w_out=w_out, g_post_mix=g_post_mix,
             g_pre_mlp=g_pre_mlp, w_up=w_up, w_down=w_down, g_post_mlp=g_post_mlp)
    m = dict(g_pre_mix=m_g_pre_mix, w_in=m_w_in, lam_re=m_lam_re, lam_im=m_lam_im, log_dt=m_log_dt, b_re=m_b_re, b_im=m_b_im, c_re=m_c_re,
             c_im=m_c_im, d_skip=m_d_skip, w_glu=m_w_glu, conv_w=m_conv_w, g_ssm_out=m_g_ssm_out, g_conv_out=m_g_conv_out, w_out=m_w_out,
             g_post_mix=m_g_post_mix, g_pre_mlp=m_g_pre_mlp, w_up=m_w_up, w_down=m_w_down, g_post_mlp=m_g_post_mlp)
    v = dict(g_pre_mix=v_g_pre_mix, w_in=v_w_in, lam_re=v_lam_re, lam_im=v_lam_im, log_dt=v_log_dt, b_re=v_b_re, b_im=v_b_im, c_re=v_c_re,
             c_im=v_c_im, d_skip=v_d_skip, w_glu=v_w_glu, conv_w=v_conv_w, g_ssm_out=v_g_ssm_out, g_conv_out=v_g_conv_out, w_out=v_w_out,
             g_post_mix=v_g_post_mix, g_pre_mlp=v_g_pre_mlp, w_up=v_w_up, w_down=v_w_down, g_post_mlp=v_g_post_mlp)
    shapes = {k: a.shape for k, a in w.items()}
    w, m, v = ({k: a[0] for k, a in d.items()} for d in (w, m, v))
    chip = 2 * lax.axis_index("x") + lax.axis_index("y")
    core = lax.axis_index("c").astype(jnp.int32).reshape(1)

    xs, target = x[0], loss_target[0]
    g1 = w["g_pre_mix"][None]
    g_ssm, g_conv = w["g_ssm_out"][None], w["g_conv_out"][None]
    g_post_mix, g_pre_mlp, g_post_mlp = w["g_post_mix"][None], w["g_pre_mlp"][None], w["g_post_mlp"][None]
    bmat, cmat, coef_f, coef_r = _ssm_matrices(w["lam_re"], w["lam_im"], w["log_dt"], w["b_re"], w["b_im"], w["c_re"], w["c_im"])
    dskip = w["d_skip"].reshape(N_GBLK, 1, UB)
    shard = {k: w[k].astype(MXU_DTYPE) for k in BIG}
    conv_pad = jnp.pad(w["conv_w"], ((0, SUBLANES - 3), (0, 0)))

    (w_in_all,) = _run_exchanges([_GatherForward([shard["w_in"]])], "ag_w_in")
    hn, proj, u4, w_down_all = _inproj_fwd(xs, g1, w_in_all, exchanges=[_Gather([shard["w_down"]], [True])])
    rest = _Gather([shard["w_glu"], shard["w_out"], conv_pad, shard["w_up"]], [False, False, False, True])
    s_re, s_im, ys, w_glu_all, w_out_all, conv_all, w_up_all = _s5_fwd(u4, bmat, cmat, coef_f, dskip, exchanges=[rest])
    w_glu_f, w_out_f = w_glu_all.reshape(D_SSM, D_SSM), w_out_all.reshape(D_MODEL, D_MODEL)
    conv_f = jnp.transpose(conv_all, (1, 0, 2)).reshape(SUBLANES, D_CONV)
    ycat, o, x1, w_up_all, w_down_all = _tail_fwd(xs, ys, proj, w_glu_f, conv_f, g_ssm, g_conv, w_out_f, g_post_mix,
                                                  exchanges=[_Forward([w_up_all, w_down_all])])
    w_down_f = w_down_all.reshape(D_FF, D_MODEL)
    hn2, up, m_act, dx2, loss = _mlp_fwd(x1, target, w_up_all, w_down_f, g_pre_mlp, g_post_mlp)

    dm, dup, act, dx1, dg_post_mlp, dg_pre_mlp = _mlp_bwd(dx2, m_act, up, x1, w_up_all, w_down_f, g_pre_mlp, g_post_mlp)
    gw_down = _matmul_tn(act, dm, "dw_down")[0].reshape(N_CHIPS, D_FF // N_CHIPS, D_MODEL)
    gw_up = _matmul_tn(hn2, dup, "dw_up", col_shards=N_CHIPS)[0]
    do, da, y1, dys, dhbc, dg_post_mix, dg_ssm, dg_conv, dconv_w, o_down, o_up = _tail_bwd(
        dx1, o, ys, proj, w_glu_f, conv_f, g_ssm, g_conv, w_out_f, g_post_mix, exchanges=[_Pair([gw_down, gw_up])])
    p_down = _pair_add(gw_down, o_down, core, "pair_add_w_down")
    p_up = _pair_add(gw_up, o_up, core, "pair_add_w_up")
    gw_out = _matmul_tn(ycat, do, "dw_out")[0].reshape(N_CHIPS, D_MODEL // N_CHIPS, D_MODEL)
    gw_glu = _matmul_tn(y1, da, "dw_glu")[0].reshape(N_CHIPS, D_SSM // N_CHIPS, D_SSM)
    du, gb, gc, q, gd, q_down, q_up, o_out, o_glu = _s5_bwd(
        dys, u4, s_re, s_im, bmat, cmat, coef_r, dskip, exchanges=[_Chip([p_down, p_up]), _Pair([gw_out, gw_glu])])
    h_down = _quad_sum(q_down, core, "quad_sum_w_down")
    h_up = _quad_sum(q_up, core, "quad_sum_w_up")
    p_out = _pair_add(gw_out, o_out, core, "pair_add_w_out")
    p_glu = _pair_add(gw_glu, o_glu, core, "pair_add_w_glu")
    grad_x, dproj, dg_pre_mix = _inproj_bwd(du, dhbc, xs, dx1, w_in_all, g1)
    d_lam_re, d_lam_im, d_log_dt, d_b_re, d_b_im, d_c_re, d_c_im, d_d_skip = _ssm_param_grads(
        w["lam_re"], w["lam_im"], w["log_dt"], w["b_re"], w["b_im"], gb, gc, q, gd)
    small = {
        "g_pre_mix": dg_pre_mix[0], "lam_re": d_lam_re, "lam_im": d_lam_im, "log_dt": d_log_dt, "b_re": d_b_re, "b_im": d_b_im,
        "c_re": d_c_re, "c_im": d_c_im, "d_skip": d_d_skip, "conv_w": dconv_w[:3], "g_ssm_out": dg_ssm[0], "g_conv_out": dg_conv[0],
        "g_post_mix": dg_post_mix[0], "g_pre_mlp": dg_pre_mlp[0], "g_post_mlp": dg_post_mlp[0],
    }
    gw_in, g_down, g_up, q_out, q_glu, packs = _matmul_tn(
        hn, dproj, "dw_in", col_shards=N_CHIPS,
        exchanges=[_Share([h_down, h_up]), _Chip([p_out, p_glu]), _GatherSmall(jnp.concatenate([_pack_small(small), loss], axis=0))])
    h_out = _quad_sum(q_out, core, "quad_sum_w_out")
    h_glu = _quad_sum(q_glu, core, "quad_sum_w_glu")
    pack = _sum_devices(packs)
    loss = pack[pack.shape[0] - SUBLANES, 0]
    (o_in,) = _run_exchanges([_Pair([gw_in])], "rs_pair_w_in")
    p_in = _pair_add(gw_in, o_in, core, "pair_add_w_in")
    q_in, g_out, g_glu = _run_exchanges([_Chip([p_in]), _Share([h_out, h_glu])], "rs_chip_w_in")
    h_in = _quad_sum(q_in, core, "quad_sum_w_in")
    (g_in,) = _run_exchanges([_Share([h_in])], "rs_share_w_in")
    shard_grads = {"w_in": g_in, "w_glu": g_glu, "w_out": g_out, "w_up": g_up, "w_down": g_down}
    offs, _ = _pack_offsets()
    conv_grad = pack[offs["conv_w"]:offs["conv_w"] + 3 * D_CONV // LANES].reshape(3, D_CONV)
    conv_grad = lax.dynamic_slice(conv_grad, (0, chip * (D_CONV // N_CHIPS)), (3, D_CONV // N_CHIPS))

    out = {q: {} for q in ("grad", "delta", "new_m", "new_v")}
    for k in BIG:
        out["grad"][k] = shard_grads[k]
        out["delta"][k], out["new_m"][k], out["new_v"][k] = _adamw(w[k], shard_grads[k], m[k], v[k], "adamw_" + k)
    lane = lambda d: {k: (d[k] if k == "conv_w" else d[k].reshape(_lane_shape(k))) for k in SMALL_SHAPES}
    res = _adamw_small(pack, conv_grad, lane(w), lane(m), lane(v))
    for q, d in zip(("grad", "delta", "new_m", "new_v"), res):
        out[q].update(d)
    flat = [loss, grad_x[None]]
    for q in ("grad", "delta", "new_m", "new_v"):
        flat += [out[q][k].reshape(shapes[k]) for k in WEIGHTS]
    return tuple(flat)
```

```python
import functools
import math

import jax
import jax.numpy as jnp
from jax import lax
from jax.experimental import pallas as pl
from jax.experimental.pallas import tpu as pltpu

F32 = jnp.float32
MXU_DTYPE = jnp.bfloat16
WIRE_DTYPE = jnp.bfloat16

D_MODEL = 1024
D_SSM = 512
D_CONV = 512
N_GROUPS = 32
GROUP = 16
STATE = 64
D_FF = 4096
RMS_EPS = 1e-6
N_CHIPS = 4
N_DEV = 8

ADAM_LR = 0.001
ADAM_B1 = 0.9
ADAM_B2 = 0.999
ADAM_EPS = 1e-08
ADAM_WD = 0.01
ADAM_STEP = 10

N_GBLK = 2
G_PER_BLK = N_GROUPS // N_GBLK
UB = G_PER_BLK * GROUP
WB = G_PER_BLK * STATE
LANE_CHUNK = 256
SUBLANES = 8
N_TABLES = 24

TM_PROJ = 512
TM_S5 = 512
TM_TAIL = 256
TM_MLP = 256
TL_TN = 2048
VMEM_LIMIT = 56 * 1024 * 1024

MESH = pl.DeviceIdType.MESH


def _params(sem, vmem=VMEM_LIMIT):
    return pltpu.CompilerParams(dimension_semantics=sem, vmem_limit_bytes=vmem)


def _resident(shape):
    nd = len(shape)
    return pl.BlockSpec(shape, lambda *_: (0,) * nd, pipeline_mode=pl.Buffered(1))


def _dot(a, b):
    return jnp.dot(a, b, preferred_element_type=F32)


def _dot_nt(a, b):
    return lax.dot_general(a, b, (((1,), (1,)), ((), ())), preferred_element_type=F32)


def _dot_tn(a, b):
    return lax.dot_general(a, b, (((0,), (0,)), ((), ())), preferred_element_type=F32)


def _rms_fwd(x, g):
    r = lax.rsqrt(jnp.mean(x * x, axis=-1, keepdims=True) + RMS_EPS)
    return x * r * g


def _rms_bwd(x, g, dy):
    r = lax.rsqrt(jnp.mean(x * x, axis=-1, keepdims=True) + RMS_EPS)
    xn = x * r
    q = dy * g
    dx = r * (q - xn * jnp.mean(q * xn, axis=-1, keepdims=True))
    return dx, jnp.sum(dy * xn, axis=0, keepdims=True)


_GELU_C = math.sqrt(2.0 / math.pi)


def _gelu(x):
    t = jnp.tanh(_GELU_C * (x + 0.044715 * (x * x * x)))
    y = x * (0.5 * (1.0 + t))
    dy = 0.5 * (1.0 + t) + 0.5 * x * (1.0 - t * t) * (_GELU_C * (1.0 + 3 * 0.044715 * (x * x)))
    return y, dy


def _tile(n, pref):
    t = min(n, pref)
    assert n % t == 0, (n, t)
    return t


HBM = pl.BlockSpec(memory_space=pltpu.HBM)
VMEM = pl.BlockSpec(memory_space=pltpu.VMEM)
DMA_SEMS = pltpu.SemaphoreType.DMA


def _place():
    x, y, c = lax.axis_index("x"), lax.axis_index("y"), lax.axis_index("c")
    chips = [(1 - x, y), (x, 1 - y), (1 - x, 1 - y)]
    return (x, y, c), 2 * x + y, (x, y, 1 - c), chips, [2 * px + py for px, py in chips]


def _remote(src, dst, send_sem, recv_sem, device):
    return pltpu.make_async_remote_copy(src_ref=src, dst_ref=dst, send_sem=send_sem, recv_sem=recv_sem,
                                        device_id=device, device_id_type=MESH)


def _half(rows, c):
    return pl.ds(c * (rows // 2), rows // 2)


class _Exchange:
    aliases = {}

    def start(self, ins, outs, sems):
        local, outgoing, _ = self._copies(ins, outs, sems)
        for cp in local + outgoing:
            cp.start()

    def finish(self, ins, outs, sems):
        local, outgoing, incoming = self._copies(ins, outs, sems)
        for cp in incoming:
            cp.wait_recv()
        for cp in outgoing:
            cp.wait_send()
        for cp in local:
            cp.wait()


class _Gather(_Exchange):
    def __init__(self, shards, split):
        self.inputs, self.split = list(shards), split
        self.out_shape = [jax.ShapeDtypeStruct((N_CHIPS, *a.shape), a.dtype) for a in shards]
        self.sems = [DMA_SEMS((len(shards), 3)), DMA_SEMS((len(shards), 3)), DMA_SEMS((len(shards),))]

    def _copies(self, ins, outs, sems):
        send, recv, lsem = sems
        (x, y, c), me, sibling, chips, ids = _place()
        local = [pltpu.make_async_copy(ins[t], outs[t].at[me], lsem.at[t]) for t in range(len(ins))]
        outgoing, incoming = [], []
        for t, a in enumerate(self.inputs):
            rows = _half(a.shape[0], c) if self.split[t] else pl.ds(0, a.shape[0])
            for k in range(3):
                to = (*chips[k], c)
                outgoing.append(_remote(ins[t].at[rows, :], outs[t].at[me, rows, :], send.at[t, k], recv.at[t, k], to))
                incoming.append(_remote(ins[t].at[rows, :], outs[t].at[ids[k], rows, :], send.at[t, k], recv.at[t, k], to))
        return local, outgoing, incoming


class _Forward(_Exchange):
    def __init__(self, arrays):
        self.inputs = list(arrays)
        self.out_shape = [jax.ShapeDtypeStruct(a.shape, a.dtype) for a in arrays]
        self.aliases = {t: t for t in range(len(arrays))}
        self.sems = [DMA_SEMS((len(arrays), 3)), DMA_SEMS((len(arrays), 3))]

    def _copies(self, ins, outs, sems):
        send, recv = sems
        (x, y, c), me, sibling, chips, ids = _place()
        outgoing, incoming = [], []
        for t, a in enumerate(self.inputs):
            for k in range(3):
                mine = outs[t].at[ids[k], _half(a.shape[1], c), :]
                theirs = outs[t].at[ids[k], _half(a.shape[1], 1 - c), :]
                outgoing.append(_remote(mine, mine, send.at[t, k], recv.at[t, k], sibling))
                incoming.append(_remote(theirs, theirs, send.at[t, k], recv.at[t, k], sibling))
        return [], outgoing, incoming


class _GatherForward(_Exchange):
    def __init__(self, shards):
        self.gather = _Gather(shards, [True] * len(shards))
        self.forward = _Forward(self.gather.out_shape)
        self.inputs, self.out_shape = self.gather.inputs, self.gather.out_shape
        self.sems = self.gather.sems + self.forward.sems

    def start(self, ins, outs, sems):
        self.gather.start(ins, outs, sems[:3])

    def finish(self, ins, outs, sems):
        local, outgoing, incoming = self.gather._copies(ins, outs, sems[:3])
        _, passed, from_sibling = self.forward._copies(outs, outs, sems[3:])
        for landed, onward in zip(incoming, passed):
            landed.wait_recv()
            onward.start()
        for cp in from_sibling:
            cp.wait_recv()
        for cp in outgoing + passed:
            cp.wait_send()
        for cp in local:
            cp.wait()


class _Pair(_Exchange):
    def __init__(self, grads):
        self.inputs = list(grads)
        self.out_shape = [jax.ShapeDtypeStruct((g.shape[0], g.shape[1] // 2, g.shape[2]), g.dtype) for g in grads]
        self.sems = [DMA_SEMS((len(grads),)), DMA_SEMS((len(grads),))]

    def _copies(self, ins, outs, sems):
        send, recv = sems
        (x, y, c), me, sibling, chips, ids = _place()
        cps = [_remote(ins[t].at[:, _half(g.shape[1], 1 - c), :], outs[t], send.at[t], recv.at[t], sibling)
               for t, g in enumerate(self.inputs)]
        return [], cps, cps


class _Chip(_Exchange):
    def __init__(self, parts):
        self.inputs = list(parts)
        self.out_shape = [jax.ShapeDtypeStruct(p.shape, p.dtype) for p in parts]
        self.sems = [DMA_SEMS((len(parts), 3)), DMA_SEMS((len(parts), 3)), DMA_SEMS((len(parts),))]

    def _copies(self, ins, outs, sems):
        send, recv, lsem = sems
        (x, y, c), me, sibling, chips, ids = _place()
        local = [pltpu.make_async_copy(ins[t].at[me], outs[t].at[me], lsem.at[t]) for t in range(len(ins))]
        outgoing, incoming = [], []
        for t in range(len(ins)):
            for k in range(3):
                to = (*chips[k], c)
                outgoing.append(_remote(ins[t].at[ids[k]], outs[t].at[me], send.at[t, k], recv.at[t, k], to))
                incoming.append(_remote(ins[t].at[ids[k]], outs[t].at[ids[k]], send.at[t, k], recv.at[t, k], to))
        return local, outgoing, incoming


class _Share(_Exchange):
    def __init__(self, grads):
        self.inputs = list(grads)
        self.out_shape = [jax.ShapeDtypeStruct(g.shape, g.dtype) for g in grads]
        self.aliases = {t: t for t in range(len(grads))}
        self.sems = [DMA_SEMS((len(grads),)), DMA_SEMS((len(grads),))]

    def _copies(self, ins, outs, sems):
        send, recv = sems
        (x, y, c), me, sibling, chips, ids = _place()
        outgoing, incoming = [], []
        for t, g in enumerate(self.inputs):
            mine = outs[t].at[_half(g.shape[0], c), :]
            theirs = outs[t].at[_half(g.shape[0], 1 - c), :]
            outgoing.append(_remote(mine, mine, send.at[t], recv.at[t], sibling))
            incoming.append(_remote(theirs, theirs, send.at[t], recv.at[t], sibling))
        return [], outgoing, incoming


class _GatherSmall(_Exchange):
    def __init__(self, block):
        self.inputs = [block]
        self.out_shape = [jax.ShapeDtypeStruct((N_DEV, *block.shape), block.dtype)]
        self.sems = [DMA_SEMS((7,)), DMA_SEMS((7,)), DMA_SEMS(())]

    def _copies(self, ins, outs, sems):
        send, recv, lsem = sems
        (x, y, c), me, sibling, chips, ids = _place()
        slot = lambda px, py, pc: outs[0].at[4 * px + 2 * py + pc]

        def copy(k, block, to, src=None):
            return _remote(slot(*block) if src is None else src, slot(*block), send.at[k], recv.at[k], to)

        local = [pltpu.make_async_copy(ins[0], slot(x, y, c), lsem)]
        first = [copy(0, (x, y, c), sibling, src=ins[0])] + [copy(1 + j, (x, y, c), (*chip, c), src=ins[0]) for j, chip in enumerate(chips)]
        passed = [copy(4 + j, (*chip, c), sibling) for j, chip in enumerate(chips)]
        landed = [copy(1 + j, (*chip, c), (x, y, c)) for j, chip in enumerate(chips)]
        from_sibling = [copy(0, (x, y, 1 - c), (x, y, c))] + [copy(4 + j, (*chip, 1 - c), (x, y, c)) for j, chip in enumerate(chips)]
        return local, first, (passed, landed, from_sibling)

    def finish(self, ins, outs, sems):
        local, first, (passed, landed, from_sibling) = self._copies(ins, outs, sems)
        for j in range(3):
            landed[j].wait_recv()
            passed[j].start()
        for cp in from_sibling:
            cp.wait_recv()
        for cp in first + passed:
            cp.wait_send()
        for cp in local:
            cp.wait()


def _split_refs(refs, counts):
    out = []
    for n in counts:
        out.append(refs[:n])
        refs = refs[n:]
    return out


def _each_exchange(exchanges, method, x_in, x_out, x_sem):
    for ex in exchanges:
        ni, no, ns = len(ex.inputs), len(ex.out_shape), len(ex.sems)
        getattr(ex, method)(x_in[:ni], x_out[:no], x_sem[:ns])
        x_in, x_out, x_sem = x_in[ni:], x_out[no:], x_sem[ns:]


def _call(body, *, name, grid, in_specs, out_specs, out_shape, operands, semantics, scratch_shapes=(), exchanges=()):
    x_in = [a for ex in exchanges for a in ex.inputs]
    x_out = [s for ex in exchanges for s in ex.out_shape]
    x_sem = [s for ex in exchanges for s in ex.sems]
    counts = (len(in_specs), len(x_in), len(out_specs), len(x_out), len(scratch_shapes), len(x_sem))
    aliases, i0, o0 = {}, len(in_specs), len(out_specs)
    for ex in exchanges:
        aliases.update({i0 + i: o0 + o for i, o in ex.aliases.items()})
        i0, o0 = i0 + len(ex.inputs), o0 + len(ex.out_shape)

    def full_body(*refs):
        ins, xi, outs, xo, scr, xs = _split_refs(list(refs), counts)
        if exchanges:
            @pl.when(functools.reduce(jnp.logical_and, [pl.program_id(a) == 0 for a in range(len(grid))]))
            def _():
                _each_exchange(exchanges, "start", xi, xo, xs)

        body(*ins, *outs, *scr)
        if exchanges:
            @pl.when(functools.reduce(jnp.logical_and, [pl.program_id(a) == grid[a] - 1 for a in range(len(grid))]))
            def _():
                _each_exchange(exchanges, "finish", xi, xo, xs)

    return pl.pallas_call(
        full_body, name=name, grid=grid,
        in_specs=list(in_specs) + [HBM] * len(x_in), out_specs=list(out_specs) + [HBM] * len(x_out),
        out_shape=list(out_shape) + x_out, scratch_shapes=list(scratch_shapes) + x_sem,
        input_output_aliases=aliases, compiler_params=_params(semantics),
    )(*operands, *x_in)


def _run_exchanges(exchanges, name):
    x_in = [a for ex in exchanges for a in ex.inputs]
    x_out = [s for ex in exchanges for s in ex.out_shape]
    x_sem = [s for ex in exchanges for s in ex.sems]
    aliases, i0, o0 = {}, 0, 0
    for ex in exchanges:
        aliases.update({i0 + i: o0 + o for i, o in ex.aliases.items()})
        i0, o0 = i0 + len(ex.inputs), o0 + len(ex.out_shape)

    def body(*refs):
        xi, xo, xs = _split_refs(list(refs), (len(x_in), len(x_out), len(x_sem)))
        _each_exchange(exchanges, "start", xi, xo, xs)
        _each_exchange(exchanges, "finish", xi, xo, xs)

    return pl.pallas_call(
        body, name=name, in_specs=[HBM] * len(x_in), out_specs=[HBM] * len(x_out), out_shape=x_out,
        scratch_shapes=x_sem, input_output_aliases=aliases,
    )(*x_in)


def _inproj_fwd(x, g1, w_in_all, exchanges=()):
    L, D = x.shape
    ns, _, nc = w_in_all.shape
    tm = _tile(L, TM_PROJ)

    def body(x_ref, g_ref, w_ref, hn_ref, proj_ref, u_ref):
        hn = _rms_fwd(x_ref[...], g_ref[...]).astype(MXU_DTYPE)
        hn_ref[...] = hn
        for j in range(ns):
            proj_ref[:, j * nc:(j + 1) * nc] = _dot(hn, w_ref[j])
        _store_slabs(u_ref, proj_ref[:, 0:nc])

    return _call(
        body, name="inproj_fwd", grid=(L // tm,), exchanges=exchanges, semantics=("arbitrary",), operands=(x, g1, w_in_all),
        in_specs=[pl.BlockSpec((tm, D), lambda i: (i, 0)), _resident((1, D)), _resident(w_in_all.shape)],
        out_specs=[pl.BlockSpec((tm, D), lambda i: (i, 0)), pl.BlockSpec((tm, ns * nc), lambda i: (i, 0)), _slab_spec(nc, tm)],
        out_shape=[jax.ShapeDtypeStruct((L, D), MXU_DTYPE), jax.ShapeDtypeStruct((L, ns * nc), F32), _slab_shape(L, nc)],
    )


def _slab_shape(L, n):
    return jax.ShapeDtypeStruct((n // LANES, L, LANES), F32)


def _slab_spec(n, tm, index=lambda i: (0, i, 0)):
    return pl.BlockSpec((n // LANES, tm, LANES), index)


def _store_slabs(ref, value):
    for k in range(ref.shape[0]):
        ref[k] = value[:, k * LANES:(k + 1) * LANES]


def _load_slabs(ref):
    return jnp.concatenate([ref[k] for k in range(ref.shape[0])], axis=1)


SEG_ROWS = SUBLANES * SUBLANES


def _load_permuted(ref):
    tm = ref.shape[1]
    slabs = []
    for k in range(ref.shape[0]):
        tiles = [ref.at[k][pl.ds(b * SEG_ROWS + j, SUBLANES, stride=SUBLANES), :] for b in range(tm // SEG_ROWS) for j in range(SUBLANES)]
        slabs.append(jnp.concatenate(tiles, axis=0))
    return jnp.concatenate(slabs, axis=1)


def _store_permuted(ref, value):
    tm = ref.shape[1]
    for k in range(ref.shape[0]):
        for b in range(tm // SEG_ROWS):
            for j in range(SUBLANES):
                r = b * SEG_ROWS + j * SUBLANES
                ref.at[k][pl.ds(b * SEG_ROWS + j, SUBLANES, stride=SUBLANES), :] = value[r:r + SUBLANES, k * LANES:(k + 1) * LANES]


def _scan_tile(xr, xi, hr, hi, coef_ref, lanes, reverse):
    for k, j in ((1, 0), (2, 2), (4, 4)):
        ar = coef_ref[0, j, :, lanes]
        ai = coef_ref[0, j + 1, :, lanes]
        shift = SUBLANES - k if reverse else k
        sr = pltpu.roll(xr, shift, 0)
        si = pltpu.roll(xi, shift, 0)
        xr, xi = xr + (ar * sr - ai * si), xi + (ar * si + ai * sr)
    pr = coef_ref[0, 6, :, lanes]
    pi = coef_ref[0, 7, :, lanes]
    return xr + (pr * hr - pi * hi), xi + (pr * hi + pi * hr)


def _scan_block(read, write, hr, hi, coef_ref, lanes, reverse):
    order = list(range(SUBLANES - 1, -1, -1) if reverse else range(SUBLANES))
    near = 8 + 2 * order[0]
    ar = coef_ref[0, near, :, lanes]
    ai = coef_ref[0, near + 1, :, lanes]
    xr, xi = read(order[0])
    local = {order[0]: (xr, xi)}
    for j in order[1:]:
        br, bi = read(j)
        xr, xi = br + (ar * xr - ai * xi), bi + (ar * xi + ai * xr)
        local[j] = (xr, xi)
    er, ei = _scan_tile(xr, xi, hr, hi, coef_ref, lanes, reverse)
    edge = lax.broadcasted_iota(jnp.int32, er.shape, 0) == (SUBLANES - 1 if reverse else 0)
    shift = SUBLANES - 1 if reverse else 1
    pr = jnp.where(edge, hr, pltpu.roll(er, shift, 0))
    pi = jnp.where(edge, hi, pltpu.roll(ei, shift, 0))
    for j in range(SUBLANES):
        cr = coef_ref[0, 8 + 2 * j, :, lanes]
        ci = coef_ref[0, 9 + 2 * j, :, lanes]
        xr, xi = local[j]
        write(j, xr + (cr * pr - ci * pi), xi + (cr * pi + ci * pr))
    end = 0 if reverse else SUBLANES - 1
    return jnp.broadcast_to(er[end:end + 1, :], er.shape), jnp.broadcast_to(ei[end:end + 1, :], ei.shape)


def _s5_fwd(u4, bmat, cmat, coef, dskip, exchanges=()):
    L = u4.shape[1]
    tm = _tile(L, TM_S5)
    lc = min(LANE_CHUNK, WB)

    def body(u_ref, bm_ref, cm_ref, coef_ref, d_ref, sre_ref, sim_ref, ys_ref, hr_ref, hi_ref):
        @pl.when(pl.program_id(1) == 0)
        def _():
            hr_ref[...] = jnp.zeros_like(hr_ref)
            hi_ref[...] = jnp.zeros_like(hi_ref)

        u = _load_permuted(u_ref)
        bu = _dot(u.astype(MXU_DTYPE), bm_ref[0])
        sre_ref[...] = bu[:, :WB]
        sim_ref[...] = bu[:, WB:]
        for c in range(WB // lc):
            lanes = slice(c * lc, (c + 1) * lc)
            hr, hi = hr_ref[:, lanes], hi_ref[:, lanes]
            for b in range(tm // SEG_ROWS):
                rows = lambda j, b=b: slice(b * SEG_ROWS + j * SUBLANES, b * SEG_ROWS + (j + 1) * SUBLANES)

                def read(j, rows=rows, lanes=lanes):
                    return sre_ref[rows(j), lanes], sim_ref[rows(j), lanes]

                def write(j, xr, xi, rows=rows, lanes=lanes):
                    sre_ref[rows(j), lanes] = xr
                    sim_ref[rows(j), lanes] = xi

                hr, hi = _scan_block(read, write, hr, hi, coef_ref, lanes, False)
            hr_ref[:, lanes] = hr
            hi_ref[:, lanes] = hi
        ys = _dot(sre_ref[...].astype(MXU_DTYPE), cm_ref[0, :WB, :]) + _dot(sim_ref[...].astype(MXU_DTYPE), cm_ref[0, WB:, :])
        _store_permuted(ys_ref, ys + d_ref[0] * u)

    return _call(
        body, name="s5_fwd", grid=(N_GBLK, L // tm), exchanges=exchanges, semantics=("arbitrary", "arbitrary"),
        operands=(u4, bmat, cmat, coef, dskip),
        in_specs=[
            _slab_spec(UB, tm, lambda b, i: (b, i, 0)),
            pl.BlockSpec((1, UB, 2 * WB), lambda b, i: (b, 0, 0)),
            pl.BlockSpec((1, 2 * WB, UB), lambda b, i: (b, 0, 0)),
            pl.BlockSpec((1, N_TABLES, SUBLANES, WB), lambda b, i: (b, 0, 0, 0)),
            pl.BlockSpec((1, 1, UB), lambda b, i: (b, 0, 0)),
        ],
        out_specs=[
            pl.BlockSpec((tm, WB), lambda b, i: (i, b)),
            pl.BlockSpec((tm, WB), lambda b, i: (i, b)),
            _slab_spec(UB, tm, lambda b, i: (b, i, 0)),
        ],
        out_shape=[
            jax.ShapeDtypeStruct((L, N_GBLK * WB), F32),
            jax.ShapeDtypeStruct((L, N_GBLK * WB), F32),
            _slab_shape(L, D_SSM),
        ],
        scratch_shapes=[pltpu.VMEM((SUBLANES, WB), F32), pltpu.VMEM((SUBLANES, WB), F32)],
    )


def _tail_fwd(x, ys, proj, w_glu, conv_w, g_ssm, g_conv, w_out, g_post, exchanges=()):
    L, D = x.shape
    tm = _tile(L, TM_TAIL)

    def body(x_ref, ys_ref, h_ref, bg_ref, cg_ref, wglu_ref, cw_ref, gs_ref, gc_ref, wout_ref, gp_ref,
             ycat_ref, o_ref, x1_ref, zbuf):
        @pl.when(pl.program_id(0) == 0)
        def _():
            zbuf[0:SUBLANES, :] = jnp.zeros((SUBLANES, D_CONV), F32)

        y1, _ = _gelu(_load_slabs(ys_ref))
        y2 = y1 * jax.nn.sigmoid(_dot(y1.astype(MXU_DTYPE), wglu_ref[...]))
        ycat_ref[:, :D_SSM] = _rms_fwd(y2, gs_ref[...]).astype(MXU_DTYPE)
        z = cg_ref[...] * h_ref[...]
        zbuf[SUBLANES:, :] = z
        conv = cw_ref[0:1, :] * zbuf[SUBLANES - 2:SUBLANES - 2 + tm, :] + cw_ref[1:2, :] * zbuf[SUBLANES - 1:SUBLANES - 1 + tm, :] + cw_ref[2:3, :] * z
        zbuf[0:SUBLANES, :] = zbuf[tm:tm + SUBLANES, :]
        ycat_ref[:, D_SSM:] = _rms_fwd(bg_ref[...] * conv, gc_ref[...]).astype(MXU_DTYPE)
        o = _dot(ycat_ref[...], wout_ref[...])
        o_ref[...] = o
        x1_ref[...] = x_ref[...] + _rms_fwd(o, gp_ref[...])

    row = lambda i: (i, 0)
    return _call(
        body, name="tail_fwd", grid=(L // tm,), exchanges=exchanges, semantics=("arbitrary",),
        operands=(x, ys, proj, proj, proj, w_glu, conv_w, g_ssm, g_conv, w_out, g_post),
        in_specs=[
            pl.BlockSpec((tm, D), row), _slab_spec(D_SSM, tm),
            pl.BlockSpec((tm, D_CONV), lambda i: (i, 1)), pl.BlockSpec((tm, D_CONV), lambda i: (i, 2)),
            pl.BlockSpec((tm, D_CONV), lambda i: (i, 3)),
            _resident(w_glu.shape), _resident(conv_w.shape), _resident(g_ssm.shape), _resident(g_conv.shape),
            _resident(w_out.shape), _resident(g_post.shape),
        ],
        out_specs=[pl.BlockSpec((tm, D), row), pl.BlockSpec((tm, D), row), pl.BlockSpec((tm, D), row)],
        out_shape=[jax.ShapeDtypeStruct((L, D), MXU_DTYPE), jax.ShapeDtypeStruct((L, D), F32), jax.ShapeDtypeStruct((L, D), F32)],
        scratch_shapes=[pltpu.VMEM((tm + SUBLANES, D_CONV), F32)],
    )


def _mlp_fwd(x1, target, w_up_all, w_down, g_pre, g_post):
    L, D = x1.shape
    ns, _, fc = w_up_all.shape
    tm = _tile(L, TM_MLP)

    def body(x1_ref, t_ref, wup_ref, wdn_ref, gpre_ref, gpost_ref, hn2_ref, up_ref, m_ref, dx2_ref, loss_ref):
        @pl.when(pl.program_id(0) == 0)
        def _():
            loss_ref[...] = jnp.zeros_like(loss_ref)

        x1v = x1_ref[...]
        hn2 = _rms_fwd(x1v, gpre_ref[...]).astype(MXU_DTYPE)
        hn2_ref[...] = hn2
        m = jnp.zeros((tm, D), F32)
        for j in range(ns):
            up = _dot(hn2, wup_ref[j])
            up_ref[:, j * fc:(j + 1) * fc] = up
            act = jnp.square(jnp.maximum(up, 0.0)).astype(MXU_DTYPE)
            m = m + _dot(act, wdn_ref[j * fc:(j + 1) * fc, :])
        m_ref[...] = m
        err = x1v + _rms_fwd(m, gpost_ref[...]) - t_ref[...]
        loss_ref[...] += 0.5 * jnp.sum(jnp.mean(err * err, axis=-1, keepdims=True))
        dx2_ref[...] = err * (1.0 / D)

    row = lambda i: (i, 0)
    return pl.pallas_call(
        body, name="mlp_fwd", grid=(L // tm,),
        in_specs=[pl.BlockSpec((tm, D), row), pl.BlockSpec((tm, D), row), _resident(w_up_all.shape), _resident(w_down.shape),
                  _resident(g_pre.shape), _resident(g_post.shape)],
        out_specs=[pl.BlockSpec((tm, D), row), pl.BlockSpec((tm, ns * fc), row), pl.BlockSpec((tm, D), row),
                   pl.BlockSpec((tm, D), row), pl.BlockSpec((SUBLANES, 128), lambda i: (0, 0))],
        out_shape=[jax.ShapeDtypeStruct((L, D), MXU_DTYPE), jax.ShapeDtypeStruct((L, ns * fc), F32), jax.ShapeDtypeStruct((L, D), F32),
                   jax.ShapeDtypeStruct((L, D), F32), jax.ShapeDtypeStruct((SUBLANES, 128), F32)],
        compiler_params=_params(("arbitrary",)),
    )(x1, target, w_up_all, w_down, g_pre, g_post)


def _mlp_bwd(dx2, m, up, x1, w_up_all, w_down, g_pre, g_post):
    L, D = x1.shape
    ns, _, fc = w_up_all.shape
    tm = _tile(L, TM_MLP)

    def body(dx2_ref, m_ref, up_ref, x1_ref, wup_ref, wdn_ref, gpre_ref, gpost_ref,
             dm_ref, dup_ref, act_ref, dx1_ref, dgpost_ref, dgpre_ref):
        @pl.when(pl.program_id(0) == 0)
        def _():
            dgpost_ref[...] = jnp.zeros_like(dgpost_ref)
            dgpre_ref[...] = jnp.zeros_like(dgpre_ref)

        dx2v = dx2_ref[...]
        dm, dg = _rms_bwd(m_ref[...], gpost_ref[...], dx2v)
        dgpost_ref[...] += dg
        dm_b = dm.astype(MXU_DTYPE)
        dm_ref[...] = dm_b
        dhn2 = jnp.zeros((tm, D), F32)
        for j in range(ns):
            cols = slice(j * fc, (j + 1) * fc)
            relu = jnp.maximum(up_ref[:, cols], 0.0)
            act_ref[:, cols] = jnp.square(relu).astype(MXU_DTYPE)
            dup = (_dot_nt(dm_b, wdn_ref[cols, :]) * (2.0 * relu)).astype(MXU_DTYPE)
            dup_ref[:, cols] = dup
            dhn2 = dhn2 + _dot_nt(dup, wup_ref[j])
        dx, dg = _rms_bwd(x1_ref[...], gpre_ref[...], dhn2)
        dgpre_ref[...] += dg
        dx1_ref[...] = dx2v + dx

    row = lambda i: (i, 0)
    vec = pl.BlockSpec((1, D), lambda i: (0, 0))
    return pl.pallas_call(
        body, name="mlp_bwd", grid=(L // tm,),
        in_specs=[pl.BlockSpec((tm, D), row), pl.BlockSpec((tm, D), row), pl.BlockSpec((tm, ns * fc), row), pl.BlockSpec((tm, D), row),
                  _resident(w_up_all.shape), _resident(w_down.shape), _resident(g_pre.shape), _resident(g_post.shape)],
        out_specs=[pl.BlockSpec((tm, D), row), pl.BlockSpec((tm, ns * fc), row), pl.BlockSpec((tm, ns * fc), row),
                   pl.BlockSpec((tm, D), row), vec, vec],
        out_shape=[jax.ShapeDtypeStruct((L, D), MXU_DTYPE), jax.ShapeDtypeStruct((L, ns * fc), MXU_DTYPE),
                   jax.ShapeDtypeStruct((L, ns * fc), MXU_DTYPE), jax.ShapeDtypeStruct((L, D), F32),
                   jax.ShapeDtypeStruct((1, D), F32), jax.ShapeDtypeStruct((1, D), F32)],
        compiler_params=_params(("arbitrary",)),
    )(dx2, m, up, x1, w_up_all, w_down, g_pre, g_post)


def _tail_bwd(dx1, o, ys, proj, w_glu, conv_w, g_ssm, g_conv, w_out, g_post, exchanges=()):
    L, D = dx1.shape
    tm = _tile(L, TM_TAIL)
    nt = L // tm
    hb = tm // SUBLANES

    def body(dx1_ref, o_ref, ys_ref, h_ref, bg_ref, cg_ref, hh_ref, hcg_ref, wglu_ref, cw_ref, gs_ref, gc_ref, wout_ref, gp_ref,
             do_ref, da_ref, y1_ref, dys_ref, dhbc_ref, dgp_ref, dgs_ref, dgc_ref, dcw_ref, zbuf, dcbuf):
        step = pl.program_id(0)

        @pl.when(step == 0)
        def _():
            dcbuf[tm:, :] = jnp.zeros((SUBLANES, D_CONV), F32)
            dgp_ref[...] = jnp.zeros_like(dgp_ref)
            dgs_ref[...] = jnp.zeros_like(dgs_ref)
            dgc_ref[...] = jnp.zeros_like(dgc_ref)
            dcw_ref[...] = jnp.zeros_like(dcw_ref)

        do, dg = _rms_bwd(o_ref[...], gp_ref[...], dx1_ref[...])
        dgp_ref[...] += dg
        do_b = do.astype(MXU_DTYPE)
        do_ref[...] = do_b
        dycat = _dot_nt(do_b, wout_ref[...])
        y1, dgelu = _gelu(_load_slabs(ys_ref))
        y1_b = y1.astype(MXU_DTYPE)
        y1_ref[...] = y1_b
        s = jax.nn.sigmoid(_dot(y1_b, wglu_ref[...]))
        dy2, dg = _rms_bwd(y1 * s, gs_ref[...], dycat[:, :D_SSM])
        dgs_ref[...] += dg
        da_b = (dy2 * y1 * s * (1.0 - s)).astype(MXU_DTYPE)
        da_ref[...] = da_b
        _store_slabs(dys_ref, (dy2 * s + _dot_nt(da_b, wglu_ref[...])) * dgelu)
        h = h_ref[...]
        cg = cg_ref[...]
        bg = bg_ref[...]
        z = cg * h
        first = step == nt - 1
        zbuf[0:SUBLANES, :] = jnp.where(first, 0.0, hcg_ref[...] * hh_ref[...])
        zbuf[SUBLANES:, :] = z
        z1 = zbuf[SUBLANES - 1:SUBLANES - 1 + tm, :]
        z2 = zbuf[SUBLANES - 2:SUBLANES - 2 + tm, :]
        conv = cw_ref[0:1, :] * z2 + cw_ref[1:2, :] * z1 + cw_ref[2:3, :] * z
        dyc, dg = _rms_bwd(bg * conv, gc_ref[...], dycat[:, D_SSM:])
        dgc_ref[...] += dg
        dconv = dyc * bg
        dcw_ref[0:1, :] += jnp.sum(dconv * z2, axis=0, keepdims=True)
        dcw_ref[1:2, :] += jnp.sum(dconv * z1, axis=0, keepdims=True)
        dcw_ref[2:3, :] += jnp.sum(dconv * z, axis=0, keepdims=True)
        dcbuf[0:tm, :] = dconv
        dz = cw_ref[2:3, :] * dconv + cw_ref[1:2, :] * dcbuf[1:1 + tm, :] + cw_ref[0:1, :] * dcbuf[2:2 + tm, :]
        dcbuf[tm:, :] = dcbuf[0:SUBLANES, :]
        dhbc_ref[:, 0:D_CONV] = (dz * cg).astype(MXU_DTYPE)
        dhbc_ref[:, D_CONV:2 * D_CONV] = (dyc * conv).astype(MXU_DTYPE)
        dhbc_ref[:, 2 * D_CONV:] = (dz * h).astype(MXU_DTYPE)

    rev = lambda i: (nt - 1 - i, 0)
    slab = lambda i: (0, nt - 1 - i, 0)
    col = lambda c: (lambda i: (nt - 1 - i, c))
    halo = lambda c: (lambda i: (jnp.maximum((nt - 1 - i) * hb - 1, 0), c))
    vec = lambda n: pl.BlockSpec((1, n), lambda i: (0, 0))
    return _call(
        body, name="tail_bwd", grid=(nt,), exchanges=exchanges, semantics=("arbitrary",),
        operands=(dx1, o, ys, proj, proj, proj, proj, proj, w_glu, conv_w, g_ssm, g_conv, w_out, g_post),
        in_specs=[
            pl.BlockSpec((tm, D), rev), pl.BlockSpec((tm, D), rev), _slab_spec(D_SSM, tm, slab),
            pl.BlockSpec((tm, D_CONV), col(1)), pl.BlockSpec((tm, D_CONV), col(2)), pl.BlockSpec((tm, D_CONV), col(3)),
            pl.BlockSpec((SUBLANES, D_CONV), halo(1)), pl.BlockSpec((SUBLANES, D_CONV), halo(3)),
            _resident(w_glu.shape), _resident(conv_w.shape), _resident(g_ssm.shape), _resident(g_conv.shape),
            _resident(w_out.shape), _resident(g_post.shape),
        ],
        out_specs=[
            pl.BlockSpec((tm, D), rev), pl.BlockSpec((tm, D_SSM), rev), pl.BlockSpec((tm, D_SSM), rev), _slab_spec(D_SSM, tm, slab),
            pl.BlockSpec((tm, 3 * D_CONV), rev), vec(D), vec(D_SSM), vec(D_CONV),
            pl.BlockSpec((SUBLANES, D_CONV), lambda i: (0, 0)),
        ],
        out_shape=[
            jax.ShapeDtypeStruct((L, D), MXU_DTYPE), jax.ShapeDtypeStruct((L, D_SSM), MXU_DTYPE), jax.ShapeDtypeStruct((L, D_SSM), MXU_DTYPE),
            _slab_shape(L, D_SSM), jax.ShapeDtypeStruct((L, 3 * D_CONV), MXU_DTYPE),
            jax.ShapeDtypeStruct((1, D), F32), jax.ShapeDtypeStruct((1, D_SSM), F32), jax.ShapeDtypeStruct((1, D_CONV), F32),
            jax.ShapeDtypeStruct((SUBLANES, D_CONV), F32),
        ],
        scratch_shapes=[pltpu.VMEM((tm + SUBLANES, D_CONV), F32), pltpu.VMEM((tm + SUBLANES, D_CONV), F32)],
    )


def _s5_bwd(dys, u4, s_re, s_im, bmat, cmat, coef_rev, dskip, exchanges=()):
    L = dys.shape[1]
    tm = _tile(L, TM_S5)
    nt = L // tm
    lc = min(LANE_CHUNK, WB)

    def body(dys_ref, u_ref, sre_ref, sim_ref, bm_ref, cm_ref, coef_ref, d_ref,
             du_ref, gb_ref, gc_ref, q_ref, gd_ref, dr_ref, di_ref, lr_ref, li_ref, hr_ref, hi_ref, qr_acc, qi_acc):
        step = pl.program_id(1)

        @pl.when(step == 0)
        def _():
            for ref in (hr_ref, hi_ref, qr_acc, qi_acc, gb_ref, gc_ref, gd_ref):
                ref[...] = jnp.zeros_like(ref)

        dys_v = _load_permuted(dys_ref)
        u = _load_permuted(u_ref)
        dys_b = dys_v.astype(MXU_DTYPE)
        u_b = u.astype(MXU_DTYPE)
        d = _dot_nt(dys_b, cm_ref[0])
        dr_ref[...] = d[:, :WB]
        di_ref[...] = d[:, WB:]
        for c in range(WB // lc):
            lanes = slice(c * lc, (c + 1) * lc)
            hr, hi = hr_ref[:, lanes], hi_ref[:, lanes]
            q = [qr_acc[:, lanes], qi_acc[:, lanes]]
            for b in range(tm // SEG_ROWS - 1, -1, -1):
                rows = lambda j, b=b: slice(b * SEG_ROWS + j * SUBLANES, b * SEG_ROWS + (j + 1) * SUBLANES)

                def read(j, rows=rows, lanes=lanes):
                    return dr_ref[rows(j), lanes], di_ref[rows(j), lanes]

                def write(j, xr, xi, rows=rows, lanes=lanes, q=q):
                    lr_ref[rows(j), lanes] = xr
                    li_ref[rows(j), lanes] = xi
                    er = xr - dr_ref[rows(j), lanes]
                    ei = xi - di_ref[rows(j), lanes]
                    sr = sre_ref[rows(j), lanes]
                    si = sim_ref[rows(j), lanes]
                    q[0] = q[0] + (er * sr + ei * si)
                    q[1] = q[1] + (ei * sr - er * si)

                hr, hi = _scan_block(read, write, hr, hi, coef_ref, lanes, True)
            hr_ref[:, lanes] = hr
            hi_ref[:, lanes] = hi
            qr_acc[:, lanes] = q[0]
            qi_acc[:, lanes] = q[1]
        lr_b = lr_ref[...].astype(MXU_DTYPE)
        li_b = li_ref[...].astype(MXU_DTYPE)
        _store_permuted(du_ref, _dot_nt(lr_b, bm_ref[0, :, :WB]) + _dot_nt(li_b, bm_ref[0, :, WB:]) + d_ref[0] * dys_v)
        gb_ref[0, :WB, :] += _dot_tn(lr_b, u_b)
        gb_ref[0, WB:, :] += _dot_tn(li_b, u_b)
        gc_ref[0, :, :WB] += _dot_tn(dys_b, sre_ref[...].astype(MXU_DTYPE))
        gc_ref[0, :, WB:] += _dot_tn(dys_b, sim_ref[...].astype(MXU_DTYPE))
        gd_ref[0] += jnp.sum(dys_v * u, axis=0, keepdims=True)

        @pl.when(step == nt - 1)
        def _():
            q_ref[0, 0:1, :] = jnp.sum(qr_acc[...], axis=0, keepdims=True)
            q_ref[0, 1:2, :] = jnp.sum(qi_acc[...], axis=0, keepdims=True)

    rev = lambda b, i: (nt - 1 - i, b)
    slab = lambda b, i: (b, nt - 1 - i, 0)
    blk = lambda b, i: (b, 0, 0)
    return _call(
        body, name="s5_bwd", grid=(N_GBLK, nt), exchanges=exchanges, semantics=("arbitrary", "arbitrary"),
        operands=(dys, u4, s_re, s_im, bmat, cmat, coef_rev, dskip),
        in_specs=[
            _slab_spec(UB, tm, slab), _slab_spec(UB, tm, slab), pl.BlockSpec((tm, WB), rev), pl.BlockSpec((tm, WB), rev),
            pl.BlockSpec((1, UB, 2 * WB), blk), pl.BlockSpec((1, 2 * WB, UB), blk),
            pl.BlockSpec((1, N_TABLES, SUBLANES, WB), lambda b, i: (b, 0, 0, 0)), pl.BlockSpec((1, 1, UB), blk),
        ],
        out_specs=[
            _slab_spec(UB, tm, slab), pl.BlockSpec((1, 2 * WB, UB), blk), pl.BlockSpec((1, UB, 2 * WB), blk),
            pl.BlockSpec((1, 2, WB), blk), pl.BlockSpec((1, 1, UB), blk),
        ],
        out_shape=[
            _slab_shape(L, D_SSM), jax.ShapeDtypeStruct((N_GBLK, 2 * WB, UB), F32),
            jax.ShapeDtypeStruct((N_GBLK, UB, 2 * WB), F32), jax.ShapeDtypeStruct((N_GBLK, 2, WB), F32),
            jax.ShapeDtypeStruct((N_GBLK, 1, UB), F32),
        ],
        scratch_shapes=[pltpu.VMEM((tm, WB), F32)] * 4 + [pltpu.VMEM((SUBLANES, WB), F32)] * 4,
    )


def _inproj_bwd(du, dhbc, x, dx1, w_in_all, g1):
    L, D = x.shape
    ns, _, nc = w_in_all.shape
    tm = _tile(L, TM_PROJ)

    def body(du_ref, dhbc_ref, x_ref, dx1_ref, w_ref, g_ref, gx_ref, dproj_ref, dg_ref):
        @pl.when(pl.program_id(0) == 0)
        def _():
            dg_ref[...] = jnp.zeros_like(dg_ref)

        du_b = _load_slabs(du_ref).astype(MXU_DTYPE)
        dproj_ref[:, :nc] = du_b
        dproj_ref[:, nc:] = dhbc_ref[...]
        dhn = _dot_nt(du_b, w_ref[0])
        for j in range(1, ns):
            dhn = dhn + _dot_nt(dhbc_ref[:, (j - 1) * nc:j * nc], w_ref[j])
        dx, dg = _rms_bwd(x_ref[...], g_ref[...], dhn)
        dg_ref[...] += dg
        gx_ref[...] = dx1_ref[...] + dx

    row = lambda i: (i, 0)
    return pl.pallas_call(
        body, name="inproj_bwd", grid=(L // tm,),
        in_specs=[_slab_spec(nc, tm), pl.BlockSpec((tm, (ns - 1) * nc), row), pl.BlockSpec((tm, D), row), pl.BlockSpec((tm, D), row),
                  _resident(w_in_all.shape), _resident(g1.shape)],
        out_specs=[pl.BlockSpec((tm, D), row), pl.BlockSpec((tm, ns * nc), row), pl.BlockSpec((1, D), lambda i: (0, 0))],
        out_shape=[jax.ShapeDtypeStruct((L, D), F32), jax.ShapeDtypeStruct((L, ns * nc), MXU_DTYPE), jax.ShapeDtypeStruct((1, D), F32)],
        compiler_params=_params(("arbitrary",)),
    )(du, dhbc, x, dx1, w_in_all, g1)


def _matmul_tn(a, b, name, col_shards=1, exchanges=()):
    L, K = a.shape
    N = b.shape[1]
    tl = _tile(L, TL_TN)
    tk = _tile(K, 1024)
    nw = N // col_shards
    tn = _tile(nw, 1024)
    npb = nw // tn

    def body(a_ref, b_ref, o_ref):
        @pl.when(pl.program_id(2) == 0)
        def _():
            o_ref[...] = jnp.zeros_like(o_ref)

        o_ref[0] += _dot_tn(a_ref[...], b_ref[...])

    return _call(
        body, name=name, grid=(K // tk, N // tn, L // tl), exchanges=exchanges, semantics=("arbitrary", "arbitrary", "arbitrary"),
        operands=(a, b),
        in_specs=[pl.BlockSpec((tl, tk), lambda k, n, l: (l, k)), pl.BlockSpec((tl, tn), lambda k, n, l: (l, n))],
        out_specs=[pl.BlockSpec((1, tk, tn), lambda k, n, l: (n // npb, k, n % npb))],
        out_shape=[jax.ShapeDtypeStruct((col_shards, K, nw), F32)],
    )


def _ssm_discretize(lam_re, lam_im, log_dt, b_re, b_im):
    dt = jnp.exp(log_dt)[:, None]
    zr = lam_re * dt
    zi = lam_im * dt
    mag = jnp.exp(zr)
    abr = mag * jnp.cos(zi)
    abi = mag * jnp.sin(zi)
    nr, ni = abr - 1.0, abi
    den = lam_re * lam_re + lam_im * lam_im
    coef_r = (nr * lam_re + ni * lam_im) / den
    coef_i = (ni * lam_re - nr * lam_im) / den
    bbar_r = coef_r[..., None] * b_re - coef_i[..., None] * b_im
    bbar_i = coef_r[..., None] * b_im + coef_i[..., None] * b_re
    return zr, zi, bbar_r, bbar_i


def _block_diag(t):
    nb, g, r, c = t.shape
    eye = jnp.eye(g, dtype=t.dtype)
    return (t[:, :, :, None, :] * eye[None, :, None, :, None]).reshape(nb, g * r, g * c)


def _diag_blocks(m, r, c):
    nb = m.shape[0]
    g = m.shape[1] // r
    eye = jnp.eye(g, dtype=m.dtype)
    t = (m.reshape(nb, g, r, g, c) * eye[None, :, None, :, None]).sum(axis=3)
    return t.reshape(nb * g, r, c)


def _powers(a):
    def mul(p, q):
        return (p[0] * q[0] - p[1] * q[1], p[0] * q[1] + p[1] * q[0])

    a2 = mul(a, a)
    a3 = mul(a2, a)
    a4 = mul(a2, a2)
    return [a, a2, a3, a4, mul(a4, a), mul(a4, a2), mul(a4, a3), mul(a4, a4)]


def _scan_tables(ar, ai, reverse):
    pw = _powers((ar, -ai if reverse else ai))
    seg = _powers(pw[SUBLANES - 1])
    t = jnp.arange(SUBLANES)[:, None]
    tabs = []
    for k in (1, 2, 4):
        mask = (t + k <= SUBLANES - 1) if reverse else (t >= k)
        tabs += [jnp.where(mask, seg[k - 1][0][None, :], 0.0), jnp.where(mask, seg[k - 1][1][None, :], 0.0)]
    order = range(SUBLANES - 1, -1, -1) if reverse else range(SUBLANES)
    tabs += [jnp.stack([seg[j][0] for j in order]), jnp.stack([seg[j][1] for j in order])]
    for j in range(SUBLANES):
        p = pw[SUBLANES - 1 - j] if reverse else pw[j]
        tabs += [jnp.broadcast_to(p[0][None, :], (SUBLANES, p[0].shape[0])), jnp.broadcast_to(p[1][None, :], (SUBLANES, p[1].shape[0]))]
    coef = jnp.stack(tabs)
    return coef.reshape(N_TABLES, SUBLANES, N_GBLK, WB).transpose(2, 0, 1, 3)


def _ssm_matrices(lam_re, lam_im, log_dt, b_re, b_im, c_re, c_im):
    zr, zi, bbar_r, bbar_i = _ssm_discretize(lam_re, lam_im, log_dt, b_re, b_im)
    mag = jnp.exp(zr)
    ar = (mag * jnp.cos(zi)).reshape(-1)
    ai = (mag * jnp.sin(zi)).reshape(-1)
    blk = lambda t: t.reshape(N_GBLK, G_PER_BLK, *t.shape[1:])
    bmat = jnp.concatenate([_block_diag(blk(bbar_r).transpose(0, 1, 3, 2)), _block_diag(blk(bbar_i).transpose(0, 1, 3, 2))], axis=2)
    cmat = jnp.concatenate([_block_diag(blk(c_re).transpose(0, 1, 3, 2)), _block_diag(blk(-c_im).transpose(0, 1, 3, 2))], axis=1)
    return bmat.astype(MXU_DTYPE), cmat.astype(MXU_DTYPE), _scan_tables(ar, ai, False), _scan_tables(ar, ai, True)


def _ssm_param_grads(lam_re, lam_im, log_dt, b_re, b_im, gb, gc, q, gd):
    gbr = _diag_blocks(gb[:, :WB, :], STATE, GROUP)
    gbi = _diag_blocks(gb[:, WB:, :], STATE, GROUP)
    d_c_re = _diag_blocks(gc[:, :, :WB], GROUP, STATE)
    d_c_im = -_diag_blocks(gc[:, :, WB:], GROUP, STATE)
    qr = q[:, 0, :].reshape(N_GROUPS, STATE)
    qi = q[:, 1, :].reshape(N_GROUPS, STATE)
    _, vjp = jax.vjp(_ssm_discretize, lam_re, lam_im, log_dt, b_re, b_im)
    d_lam_re, d_lam_im, d_log_dt, d_b_re, d_b_im = vjp((qr, qi, gbr, gbi))
    return d_lam_re, d_lam_im, d_log_dt, d_b_re, d_b_im, d_c_re, d_c_im, gd.reshape(N_GROUPS, GROUP)


def _row_tile(rows, n):
    return _tile(rows, max(SUBLANES, (2 * 1024 * 1024) // (4 * n)))


def _pair_add(grad, other, core, name):
    ns, h, n = other.shape
    tr = _row_tile(h, n)
    nb = h // tr

    def body(c_ref, g_ref, o_ref, out_ref):
        out_ref[...] = (g_ref[...] + o_ref[...]).astype(WIRE_DTYPE)

    return pl.pallas_call(
        body, name=name,
        grid_spec=pltpu.PrefetchScalarGridSpec(
            num_scalar_prefetch=1, grid=(ns, nb),
            in_specs=[pl.BlockSpec((1, tr, n), lambda s, i, c: (s, c[0] * nb + i, 0)), pl.BlockSpec((1, tr, n), lambda s, i, c: (s, i, 0))],
            out_specs=pl.BlockSpec((1, tr, n), lambda s, i, c: (s, i, 0))),
        out_shape=jax.ShapeDtypeStruct(other.shape, WIRE_DTYPE),
        compiler_params=_params(("arbitrary", "arbitrary")),
    )(core, grad, other)


def _quad_sum(parts, core, name):
    ns, h, n = parts.shape
    tr = _row_tile(h, n)
    nb = h // tr

    def body(c_ref, p_ref, out_ref):
        p = [p_ref[k].astype(F32) for k in range(ns)]
        out_ref[...] = ((p[0] + p[1]) + p[2]) + p[3]

    return pl.pallas_call(
        body, name=name,
        grid_spec=pltpu.PrefetchScalarGridSpec(
            num_scalar_prefetch=1, grid=(nb,),
            in_specs=[pl.BlockSpec((ns, tr, n), lambda i, c: (0, i, 0))],
            out_specs=pl.BlockSpec((tr, n), lambda i, c: (c[0] * nb + i, 0))),
        out_shape=jax.ShapeDtypeStruct((2 * h, n), F32),
        compiler_params=_params(("arbitrary",)),
    )(core, parts)


def _sum_devices(blocks):
    nd, m, n = blocks.shape

    def body(b_ref, out_ref):
        total = b_ref[0]
        for d in range(1, nd):
            total = total + b_ref[d]
        out_ref[...] = total

    return pl.pallas_call(body, name="sum_devices", in_specs=[VMEM], out_specs=VMEM,
                          out_shape=jax.ShapeDtypeStruct((m, n), blocks.dtype))(blocks)


def _adamw_math(w, g, m, v):
    m = ADAM_B1 * m + (1.0 - ADAM_B1) * g
    v = ADAM_B2 * v + (1.0 - ADAM_B2) * jnp.square(g)
    m_hat = m / (1.0 - ADAM_B1 ** ADAM_STEP)
    v_hat = v / (1.0 - ADAM_B2 ** ADAM_STEP)
    delta = -ADAM_LR * (m_hat / (jnp.sqrt(v_hat) + ADAM_EPS) + ADAM_WD * w)
    return delta, m, v


def _adamw(w, g, m, v, name):
    r, n = w.shape
    tr = _row_tile(r, n)

    def body(w_ref, g_ref, m_ref, v_ref, d_ref, nm_ref, nv_ref):
        d_ref[...], nm_ref[...], nv_ref[...] = _adamw_math(w_ref[...], g_ref[...], m_ref[...], v_ref[...])

    spec = pl.BlockSpec((tr, n), lambda i: (i, 0))
    return pl.pallas_call(
        body, name=name, grid=(r // tr,), in_specs=[spec] * 4, out_specs=[spec] * 3,
        out_shape=[jax.ShapeDtypeStruct((r, n), F32)] * 3,
        compiler_params=_params(("arbitrary",)),
    )(w, g, m, v)


SMALL_SHAPES = {
    "g_pre_mix": (D_MODEL,), "lam_re": (N_GROUPS, STATE), "lam_im": (N_GROUPS, STATE), "log_dt": (N_GROUPS,),
    "b_re": (N_GROUPS, STATE, GROUP), "b_im": (N_GROUPS, STATE, GROUP), "c_re": (N_GROUPS, GROUP, STATE), "c_im": (N_GROUPS, GROUP, STATE),
    "d_skip": (N_GROUPS, GROUP), "conv_w": (3, D_CONV), "g_ssm_out": (D_SSM,), "g_conv_out": (D_CONV,),
    "g_post_mix": (D_MODEL,), "g_pre_mlp": (D_MODEL,), "g_post_mlp": (D_MODEL,),
}
LANES = 128
PACK_TILE = SUBLANES * LANES


def _pack_rows(name):
    n = math.prod(SMALL_SHAPES[name])
    return SUBLANES * (-(-n // PACK_TILE))


def _pack_offsets():
    offs, row = {}, 0
    for name in SMALL_SHAPES:
        offs[name] = row
        row += _pack_rows(name)
    return offs, row


def _pack_small(grads):
    parts = []
    for name in SMALL_SHAPES:
        flat = grads[name].reshape(-1)
        parts.append(flat)
        fill = _pack_rows(name) * LANES - flat.shape[0]
        if fill:
            parts.append(jnp.zeros((fill,), flat.dtype))
    return jnp.concatenate(parts).reshape(-1, LANES)


def _lane_shape(name):
    n = math.prod(SMALL_SHAPES[name])
    return (n // LANES, LANES) if n % LANES == 0 else (1, n)


def _adamw_small(pack, conv_grad, w, m, v):
    names = list(SMALL_SHAPES)
    offs, _ = _pack_offsets()
    nn = len(names)

    def body(*refs):
        pack_ref, cg_ref = refs[0], refs[1]
        w_refs, m_refs, v_refs = refs[2:2 + nn], refs[2 + nn:2 + 2 * nn], refs[2 + 2 * nn:2 + 3 * nn]
        outs = refs[2 + 3 * nn:]
        for j, name in enumerate(names):
            r, n = w_refs[j].shape
            g = cg_ref[...] if name == "conv_w" else pack_ref[offs[name]:offs[name] + r, 0:n]
            delta, nm, nv = _adamw_math(w_refs[j][...], g, m_refs[j][...], v_refs[j][...])
            outs[j][...] = g
            outs[nn + j][...] = delta
            outs[2 * nn + j][...] = nm
            outs[3 * nn + j][...] = nv

    args = [pack, conv_grad] + [w[k] for k in names] + [m[k] for k in names] + [v[k] for k in names]
    res = pl.pallas_call(
        body, name="adamw_small", in_specs=[VMEM] * len(args), out_specs=[VMEM] * (4 * nn),
        out_shape=[jax.ShapeDtypeStruct(w[k].shape, F32) for k in names] * 4,
    )(*args)
    return [dict(zip(names, res[q * nn:(q + 1) * nn])) for q in range(4)]


WEIGHTS = ["g_pre_mix", "w_in", "lam_re", "lam_im", "log_dt", "b_re", "b_im", "c_re", "c_im", "d_skip", "w_glu", "conv_w",
           "g_ssm_out", "g_conv_out", "w_out", "g_post_mix", "g_pre_mlp", "w_up", "w_down", "g_post_mlp"]
BIG = ["w_in", "w_glu", "w_out", "w_up", "w_down"]


def kernel(x, g_pre_mix, w_in, lam_re, lam_im, log_dt, b_re, b_im, c_re, c_im, d_skip, w_glu, conv_w, g_ssm_out, g_conv_out, w_out, g_post_mix, g_pre_mlp, w_up, w_down, g_post_mlp, loss_target, m_g_pre_mix, m_w_in, m_lam_re, m_lam_im, m_log_dt, m_b_re, m_b_im, m_c_re, m_c_im, m_d_skip, m_w_glu, m_conv_w, m_g_ssm_out, m_g_conv_out, m_w_out, m_g_post_mix, m_g_pre_mlp, m_w_up, m_w_down, m_g_post_mlp, v_g_pre_mix, v_w_in, v_lam_re, v_lam_im, v_log_dt, v_b_re, v_b_im, v_c_re, v_c_im, v_d_skip, v_w_glu, v_conv_w, v_g_ssm_out, v_g_conv_out, v_w_out, v_g_post_mix, v_g_pre_mlp, v_w_up, v_w_down, v_g_post_mlp):
    w = dict(g_pre_mix=g_pre_mix, w_in=w_in, lam_re=lam_re, lam_im=lam_im, log_dt=log_dt, b_re=b_re, b_im=b_im, c_re=c_re, c_im=c_im,
             d_skip=d_skip, w_glu=w_glu, conv_w=conv_w, g_ssm_out=g_ssm_out, g_conv_out=g_conv_out, w_out=w_out, g_post_mix=g_post_mix,
             g_pre_mlp=g_pre_mlp, w_up=w_up, w_down=w_down, g_post_mlp=g_post_mlp)
    m = dict(g_pre_mix=m_g_pre_mix, w_in=m_w_in, lam_re=m_lam_re, lam_im=m_lam_im, log_dt=m_log_dt, b_re=m_b_re, b_im=m_b_im, c_re=m_c_re,
             c_im=m_c_im, d_skip=m_d_skip, w_glu=m_w_glu, conv_w=m_conv_w, g_ssm_out=m_g_ssm_out, g_conv_out=m_g_conv_out, w_out=m_w_out,
             g_post_mix=m_g_post_mix, g_pre_mlp=m_g_pre_mlp, w_up=m_w_up, w_down=m_w_down, g_post_mlp=m_g_post_mlp)
    v = dict(g_pre_mix=v_g_pre_mix, w_in=v_w_in, lam_re=v_lam_re, lam_im=v_lam_im, log_dt=v_log_dt, b_re=v_b_re, b_im=v_b_im, c_re=v_c_re,
             c_im=v_c_im, d_skip=v_d_skip, w_glu=v_w_glu, conv_w=v_conv_w, g_ssm_out=v_g_ssm_out, g_conv_out=v_g_conv_out, w_out=v_w_out,
             g_post_mix=v_g_post_mix, g_pre_mlp=v_g_pre_mlp, w_up=v_w_up, w_down=v_w_down, g_post_mlp=v_g_post_mlp)
    shapes = {k: a.shape for k, a in w.items()}
    w, m, v = ({k: a[0] for k, a in d.items()} for d in (w, m, v))
    chip = 2 * lax.axis_index("x") + lax.axis_index("y")
    core = lax.axis_index("c").astype(jnp.int32).reshape(1)

    xs, target = x[0], loss_target[0]
    g1 = w["g_pre_mix"][None]
    g_ssm, g_conv = w["g_ssm_out"][None], w["g_conv_out"][None]
    g_post_mix, g_pre_mlp, g_post_mlp = w["g_post_mix"][None], w["g_pre_mlp"][None], w["g_post_mlp"][None]
    bmat, cmat, coef_f, coef_r = _ssm_matrices(w["lam_re"], w["lam_im"], w["log_dt"], w["b_re"], w["b_im"], w["c_re"], w["c_im"])
    dskip = w["d_skip"].reshape(N_GBLK, 1, UB)
    shard = {k: w[k].astype(MXU_DTYPE) for k in BIG}
    conv_pad = jnp.pad(w["conv_w"], ((0, SUBLANES - 3), (0, 0)))

    (w_in_all,) = _run_exchanges([_GatherForward([shard["w_in"]])], "ag_w_in")
    hn, proj, u4, w_glu_all, w_out_all, conv_all = _inproj_fwd(
        xs, g1, w_in_all, exchanges=[_Gather([shard["w_glu"], shard["w_out"], conv_pad], [False, False, False])])
    s_re, s_im, ys, w_up_all, w_down_all = _s5_fwd(
        u4, bmat, cmat, coef_f, dskip, exchanges=[_Gather([shard["w_up"], shard["w_down"]], [True, True])])
    w_glu_f, w_out_f = w_glu_all.reshape(D_SSM, D_SSM), w_out_all.reshape(D_MODEL, D_MODEL)
    conv_f = jnp.transpose(conv_all, (1, 0, 2)).reshape(SUBLANES, D_CONV)
    ycat, o, x1, w_up_all, w_down_all = _tail_fwd(xs, ys, proj, w_glu_f, conv_f, g_ssm, g_conv, w_out_f, g_post_mix,
                                                  exchanges=[_Forward([w_up_all, w_down_all])])
    w_down_f = w_down_all.reshape(D_FF, D_MODEL)
    hn2, up, m_act, dx2, loss = _mlp_fwd(x1, target, w_up_all, w_down_f, g_pre_mlp, g_post_mlp)

    dm, dup, act, dx1, dg_post_mlp, dg_pre_mlp = _mlp_bwd(dx2, m_act, up, x1, w_up_all, w_down_f, g_pre_mlp, g_post_mlp)
    gw_down = _matmul_tn(act, dm, "dw_down")[0].reshape(N_CHIPS, D_FF // N_CHIPS, D_MODEL)
    gw_up = _matmul_tn(hn2, dup, "dw_up", col_shards=N_CHIPS)[0]
    do, da, y1, dys, dhbc, dg_post_mix, dg_ssm, dg_conv, dconv_w, o_down, o_up = _tail_bwd(
        dx1, o, ys, proj, w_glu_f, conv_f, g_ssm, g_conv, w_out_f, g_post_mix, exchanges=[_Pair([gw_down, gw_up])])
    p_down = _pair_add(gw_down, o_down, core, "pair_add_w_down")
    p_up = _pair_add(gw_up, o_up, core, "pair_add_w_up")
    gw_out = _matmul_tn(ycat, do, "dw_out")[0].reshape(N_CHIPS, D_MODEL // N_CHIPS, D_MODEL)
    gw_glu = _matmul_tn(y1, da, "dw_glu")[0].reshape(N_CHIPS, D_SSM // N_CHIPS, D_SSM)
    du, gb, gc, q, gd, q_down, q_up, o_out, o_glu = _s5_bwd(
        dys, u4, s_re, s_im, bmat, cmat, coef_r, dskip, exchanges=[_Chip([p_down, p_up]), _Pair([gw_out, gw_glu])])
    h_down = _quad_sum(q_down, core, "quad_sum_w_down")
    h_up = _quad_sum(q_up, core, "quad_sum_w_up")
    p_out = _pair_add(gw_out, o_out, core, "pair_add_w_out")
    p_glu = _pair_add(gw_glu, o_glu, core, "pair_add_w_glu")
    grad_x, dproj, dg_pre_mix = _inproj_bwd(du, dhbc, xs, dx1, w_in_all, g1)
    d_lam_re, d_lam_im, d_log_dt, d_b_re, d_b_im, d_c_re, d_c_im, d_d_skip = _ssm_param_grads(
        w["lam_re"], w["lam_im"], w["log_dt"], w["b_re"], w["b_im"], gb, gc, q, gd)
    small = {
        "g_pre_mix": dg_pre_mix[0], "lam_re": d_lam_re, "lam_im": d_lam_im, "log_dt": d_log_dt, "b_re": d_b_re, "b_im": d_b_im,
        "c_re": d_c_re, "c_im": d_c_im, "d_skip": d_d_skip, "conv_w": dconv_w[:3], "g_ssm_out": dg_ssm[0], "g_conv_out": dg_conv[0],
        "g_post_mix": dg_post_mix[0], "g_pre_mlp": dg_pre_mlp[0], "g_post_mlp": dg_post_mlp[0],
    }
    gw_in, g_down, g_up, q_out, q_glu, packs = _matmul_tn(
        hn, dproj, "dw_in", col_shards=N_CHIPS,
        exchanges=[_Share([h_down, h_up]), _Chip([p_out, p_glu]), _GatherSmall(jnp.concatenate([_pack_small(small), loss], axis=0))])
    h_out = _quad_sum(q_out, core, "quad_sum_w_out")
    h_glu = _quad_sum(q_glu, core, "quad_sum_w_glu")
    pack = _sum_devices(packs)
    loss = pack[pack.shape[0] - SUBLANES, 0]
    (o_in,) = _run_exchanges([_Pair([gw_in])], "rs_pair_w_in")
    p_in = _pair_add(gw_in, o_in, core, "pair_add_w_in")
    q_in, g_out, g_glu = _run_exchanges([_Chip([p_in]), _Share([h_out, h_glu])], "rs_chip_w_in")
    h_in = _quad_sum(q_in, core, "quad_sum_w_in")
    (g_in,) = _run_exchanges([_Share([h_in])], "rs_share_w_in")
    shard_grads = {"w_in": g_in, "w_glu": g_glu, "w_out": g_out, "w_up": g_up, "w_down": g_down}
    offs, _ = _pack_offsets()
    conv_grad = pack[offs["conv_w"]:offs["conv_w"] + 3 * D_CONV // LANES].reshape(3, D_CONV)
    conv_grad = lax.dynamic_slice(conv_grad, (0, chip * (D_CONV // N_CHIPS)), (3, D_CONV // N_CHIPS))

    out = {q: {} for q in ("grad", "delta", "new_m", "new_v")}
    for k in BIG:
        out["grad"][k] = shard_grads[k]
        out["delta"][k], out["new_m"][k], out["new_v"][k] = _adamw(w[k], shard_grads[k], m[k], v[k], "adamw_" + k)
    lane = lambda d: {k: (d[k] if k == "conv_w" else d[k].reshape(_lane_shape(k))) for k in SMALL_SHAPES}
    res = _adamw_small(pack, conv_grad, lane(w), lane(m), lane(v))
    for q, d in zip(("grad", "delta", "new_m", "new_v"), res):
        out[q].update(d)
    flat = [loss, grad_x[None]]
    for q in ("grad", "delta", "new_m", "new_v"):
        flat += [out[q][k].reshape(shapes[k]) for k in WEIGHTS]
    return tuple(flat)
```

```python
import functools
import math

import jax
import jax.numpy as jnp
from jax import lax
from jax.experimental import pallas as pl
from jax.experimental.pallas import tpu as pltpu

F32 = jnp.float32
MXU_DTYPE = jnp.bfloat16
WIRE_DTYPE = jnp.bfloat16

D_MODEL = 1024
D_SSM = 512
D_CONV = 512
N_GROUPS = 32
GROUP = 16
STATE = 64
D_FF = 4096
RMS_EPS = 1e-6
N_CHIPS = 4
N_DEV = 8

ADAM_LR = 0.001
ADAM_B1 = 0.9
ADAM_B2 = 0.999
ADAM_EPS = 1e-08
ADAM_WD = 0.01
ADAM_STEP = 10

N_GBLK = 2
G_PER_BLK = N_GROUPS // N_GBLK
UB = G_PER_BLK * GROUP
WB = G_PER_BLK * STATE
LANE_CHUNK = 256
SUBLANES = 8
N_TABLES = 24

TM_PROJ = 512
TM_S5 = 512
TM_TAIL = 256
TM_MLP = 256
TL_TN = 2048
VMEM_LIMIT = 56 * 1024 * 1024

MESH = pl.DeviceIdType.MESH


def _params(sem, vmem=VMEM_LIMIT):
    return pltpu.CompilerParams(dimension_semantics=sem, vmem_limit_bytes=vmem)


def _resident(shape):
    nd = len(shape)
    return pl.BlockSpec(shape, lambda *_: (0,) * nd, pipeline_mode=pl.Buffered(1))


def _dot(a, b):
    return jnp.dot(a, b, preferred_element_type=F32)


def _dot_nt(a, b):
    return lax.dot_general(a, b, (((1,), (1,)), ((), ())), preferred_element_type=F32)


def _dot_tn(a, b):
    return lax.dot_general(a, b, (((0,), (0,)), ((), ())), preferred_element_type=F32)


def _rms_fwd(x, g):
    r = lax.rsqrt(jnp.mean(x * x, axis=-1, keepdims=True) + RMS_EPS)
    return x * r * g


def _rms_bwd(x, g, dy):
    r = lax.rsqrt(jnp.mean(x * x, axis=-1, keepdims=True) + RMS_EPS)
    xn = x * r
    q = dy * g
    dx = r * (q - xn * jnp.mean(q * xn, axis=-1, keepdims=True))
    return dx, jnp.sum(dy * xn, axis=0, keepdims=True)


_GELU_C = math.sqrt(2.0 / math.pi)


def _gelu(x):
    t = jnp.tanh(_GELU_C * (x + 0.044715 * (x * x * x)))
    y = x * (0.5 * (1.0 + t))
    dy = 0.5 * (1.0 + t) + 0.5 * x * (1.0 - t * t) * (_GELU_C * (1.0 + 3 * 0.044715 * (x * x)))
    return y, dy


def _tile(n, pref):
    t = min(n, pref)
    assert n % t == 0, (n, t)
    return t


HBM = pl.BlockSpec(memory_space=pltpu.HBM)
VMEM = pl.BlockSpec(memory_space=pltpu.VMEM)
DMA_SEMS = pltpu.SemaphoreType.DMA


def _place():
    x, y, c = lax.axis_index("x"), lax.axis_index("y"), lax.axis_index("c")
    chips = [(1 - x, y), (x, 1 - y), (1 - x, 1 - y)]
    return (x, y, c), 2 * x + y, (x, y, 1 - c), chips, [2 * px + py for px, py in chips]


def _remote(src, dst, send_sem, recv_sem, device):
    return pltpu.make_async_remote_copy(src_ref=src, dst_ref=dst, send_sem=send_sem, recv_sem=recv_sem,
                                        device_id=device, device_id_type=MESH)


def _half(rows, c):
    return pl.ds(c * (rows // 2), rows // 2)


class _Exchange:
    aliases = {}

    def start(self, ins, outs, sems):
        local, outgoing, _ = self._copies(ins, outs, sems)
        for cp in local + outgoing:
            cp.start()

    def finish(self, ins, outs, sems):
        local, outgoing, incoming = self._copies(ins, outs, sems)
        for cp in incoming:
            cp.wait_recv()
        for cp in outgoing:
            cp.wait_send()
        for cp in local:
            cp.wait()


class _Gather(_Exchange):
    def __init__(self, shards, split):
        self.inputs, self.split = list(shards), split
        self.out_shape = [jax.ShapeDtypeStruct((N_CHIPS, *a.shape), a.dtype) for a in shards]
        self.sems = [DMA_SEMS((len(shards), 3)), DMA_SEMS((len(shards), 3)), DMA_SEMS((len(shards),))]

    def _copies(self, ins, outs, sems):
        send, recv, lsem = sems
        (x, y, c), me, sibling, chips, ids = _place()
        local = [pltpu.make_async_copy(ins[t], outs[t].at[me], lsem.at[t]) for t in range(len(ins))]
        outgoing, incoming = [], []
        for t, a in enumerate(self.inputs):
            rows = _half(a.shape[0], c) if self.split[t] else pl.ds(0, a.shape[0])
            for k in range(3):
                to = (*chips[k], c)
                outgoing.append(_remote(ins[t].at[rows, :], outs[t].at[me, rows, :], send.at[t, k], recv.at[t, k], to))
                incoming.append(_remote(ins[t].at[rows, :], outs[t].at[ids[k], rows, :], send.at[t, k], recv.at[t, k], to))
        return local, outgoing, incoming


class _Forward(_Exchange):
    def __init__(self, arrays):
        self.inputs = list(arrays)
        self.out_shape = [jax.ShapeDtypeStruct(a.shape, a.dtype) for a in arrays]
        self.aliases = {t: t for t in range(len(arrays))}
        self.sems = [DMA_SEMS((len(arrays), 3)), DMA_SEMS((len(arrays), 3))]

    def _copies(self, ins, outs, sems):
        send, recv = sems
        (x, y, c), me, sibling, chips, ids = _place()
        outgoing, incoming = [], []
        for t, a in enumerate(self.inputs):
            for k in range(3):
                mine = outs[t].at[ids[k], _half(a.shape[1], c), :]
                theirs = outs[t].at[ids[k], _half(a.shape[1], 1 - c), :]
                outgoing.append(_remote(mine, mine, send.at[t, k], recv.at[t, k], sibling))
                incoming.append(_remote(theirs, theirs, send.at[t, k], recv.at[t, k], sibling))
        return [], outgoing, incoming


class _GatherForward(_Exchange):
    def __init__(self, shards):
        self.gather = _Gather(shards, [True] * len(shards))
        self.forward = _Forward(self.gather.out_shape)
        self.inputs, self.out_shape = self.gather.inputs, self.gather.out_shape
        self.sems = self.gather.sems + self.forward.sems

    def start(self, ins, outs, sems):
        self.gather.start(ins, outs, sems[:3])

    def finish(self, ins, outs, sems):
        local, outgoing, incoming = self.gather._copies(ins, outs, sems[:3])
        _, passed, from_sibling = self.forward._copies(outs, outs, sems[3:])
        for landed, onward in zip(incoming, passed):
            landed.wait_recv()
            onward.start()
        for cp in from_sibling:
            cp.wait_recv()
        for cp in outgoing + passed:
            cp.wait_send()
        for cp in local:
            cp.wait()


class _Pair(_Exchange):
    def __init__(self, grads):
        self.inputs = list(grads)
        self.out_shape = [jax.ShapeDtypeStruct((g.shape[0], g.shape[1] // 2, g.shape[2]), g.dtype) for g in grads]
        self.sems = [DMA_SEMS((len(grads),)), DMA_SEMS((len(grads),))]

    def _copies(self, ins, outs, sems):
        send, recv = sems
        (x, y, c), me, sibling, chips, ids = _place()
        cps = [_remote(ins[t].at[:, _half(g.shape[1], 1 - c), :], outs[t], send.at[t], recv.at[t], sibling)
               for t, g in enumerate(self.inputs)]
        return [], cps, cps


class _Chip(_Exchange):
    def __init__(self, parts):
        self.inputs = list(parts)
        self.out_shape = [jax.ShapeDtypeStruct(p.shape, p.dtype) for p in parts]
        self.sems = [DMA_SEMS((len(parts), 3)), DMA_SEMS((len(parts), 3)), DMA_SEMS((len(parts),))]

    def _copies(self, ins, outs, sems):
        send, recv, lsem = sems
        (x, y, c), me, sibling, chips, ids = _place()
        local = [pltpu.make_async_copy(ins[t].at[me], outs[t].at[me], lsem.at[t]) for t in range(len(ins))]
        outgoing, incoming = [], []
        for t in range(len(ins)):
            for k in range(3):
                to = (*chips[k], c)
                outgoing.append(_remote(ins[t].at[ids[k]], outs[t].at[me], send.at[t, k], recv.at[t, k], to))
                incoming.append(_remote(ins[t].at[ids[k]], outs[t].at[ids[k]], send.at[t, k], recv.at[t, k], to))
        return local, outgoing, incoming


class _Share(_Exchange):
    def __init__(self, grads):
        self.inputs = list(grads)
        self.out_shape = [jax.ShapeDtypeStruct(g.shape, g.dtype) for g in grads]
        self.aliases = {t: t for t in range(len(grads))}
        self.sems = [DMA_SEMS((len(grads),)), DMA_SEMS((len(grads),))]

    def _copies(self, ins, outs, sems):
        send, recv = sems
        (x, y, c), me, sibling, chips, ids = _place()
        outgoing, incoming = [], []
        for t, g in enumerate(self.inputs):
            mine = outs[t].at[_half(g.shape[0], c), :]
            theirs = outs[t].at[_half(g.shape[0], 1 - c), :]
            outgoing.append(_remote(mine, mine, send.at[t], recv.at[t], sibling))
            incoming.append(_remote(theirs, theirs, send.at[t], recv.at[t], sibling))
        return [], outgoing, incoming


class _GatherSmall(_Exchange):
    def __init__(self, block):
        self.inputs = [block]
        self.out_shape = [jax.ShapeDtypeStruct((N_DEV, *block.shape), block.dtype)]
        self.sems = [DMA_SEMS((7,)), DMA_SEMS((7,)), DMA_SEMS(())]

    def _copies(self, ins, outs, sems):
        send, recv, lsem = sems
        (x, y, c), me, sibling, chips, ids = _place()
        slot = lambda px, py, pc: outs[0].at[4 * px + 2 * py + pc]

        def copy(k, block, to, src=None):
            return _remote(slot(*block) if src is None else src, slot(*block), send.at[k], recv.at[k], to)

        local = [pltpu.make_async_copy(ins[0], slot(x, y, c), lsem)]
        first = [copy(0, (x, y, c), sibling, src=ins[0])] + [copy(1 + j, (x, y, c), (*chip, c), src=ins[0]) for j, chip in enumerate(chips)]
        passed = [copy(4 + j, (*chip, c), sibling) for j, chip in enumerate(chips)]
        landed = [copy(1 + j, (*chip, c), (x, y, c)) for j, chip in enumerate(chips)]
        from_sibling = [copy(0, (x, y, 1 - c), (x, y, c))] + [copy(4 + j, (*chip, 1 - c), (x, y, c)) for j, chip in enumerate(chips)]
        return local, first, (passed, landed, from_sibling)

    def finish(self, ins, outs, sems):
        local, first, (passed, landed, from_sibling) = self._copies(ins, outs, sems)
        for j in range(3):
            landed[j].wait_recv()
            passed[j].start()
        for cp in from_sibling:
            cp.wait_recv()
        for cp in first + passed:
            cp.wait_send()
        for cp in local:
            cp.wait()


def _split_refs(refs, counts):
    out = []
    for n in counts:
        out.append(refs[:n])
        refs = refs[n:]
    return out


def _each_exchange(exchanges, method, x_in, x_out, x_sem):
    for ex in exchanges:
        ni, no, ns = len(ex.inputs), len(ex.out_shape), len(ex.sems)
        getattr(ex, method)(x_in[:ni], x_out[:no], x_sem[:ns])
        x_in, x_out, x_sem = x_in[ni:], x_out[no:], x_sem[ns:]


def _call(body, *, name, grid, in_specs, out_specs, out_shape, operands, semantics, scratch_shapes=(), exchanges=()):
    x_in = [a for ex in exchanges for a in ex.inputs]
    x_out = [s for ex in exchanges for s in ex.out_shape]
    x_sem = [s for ex in exchanges for s in ex.sems]
    counts = (len(in_specs), len(x_in), len(out_specs), len(x_out), len(scratch_shapes), len(x_sem))
    aliases, i0, o0 = {}, len(in_specs), len(out_specs)
    for ex in exchanges:
        aliases.update({i0 + i: o0 + o for i, o in ex.aliases.items()})
        i0, o0 = i0 + len(ex.inputs), o0 + len(ex.out_shape)

    def full_body(*refs):
        ins, xi, outs, xo, scr, xs = _split_refs(list(refs), counts)
        if exchanges:
            @pl.when(functools.reduce(jnp.logical_and, [pl.program_id(a) == 0 for a in range(len(grid))]))
            def _():
                _each_exchange(exchanges, "start", xi, xo, xs)

        body(*ins, *outs, *scr)
        if exchanges:
            @pl.when(functools.reduce(jnp.logical_and, [pl.program_id(a) == grid[a] - 1 for a in range(len(grid))]))
            def _():
                _each_exchange(exchanges, "finish", xi, xo, xs)

    return pl.pallas_call(
        full_body, name=name, grid=grid,
        in_specs=list(in_specs) + [HBM] * len(x_in), out_specs=list(out_specs) + [HBM] * len(x_out),
        out_shape=list(out_shape) + x_out, scratch_shapes=list(scratch_shapes) + x_sem,
        input_output_aliases=aliases, compiler_params=_params(semantics),
    )(*operands, *x_in)


def _run_exchanges(exchanges, name):
    x_in = [a for ex in exchanges for a in ex.inputs]
    x_out = [s for ex in exchanges for s in ex.out_shape]
    x_sem = [s for ex in exchanges for s in ex.sems]
    aliases, i0, o0 = {}, 0, 0
    for ex in exchanges:
        aliases.update({i0 + i: o0 + o for i, o in ex.aliases.items()})
        i0, o0 = i0 + len(ex.inputs), o0 + len(ex.out_shape)

    def body(*refs):
        xi, xo, xs = _split_refs(list(refs), (len(x_in), len(x_out), len(x_sem)))
        _each_exchange(exchanges, "start", xi, xo, xs)
        _each_exchange(exchanges, "finish", xi, xo, xs)

    return pl.pallas_call(
        body, name=name, in_specs=[HBM] * len(x_in), out_specs=[HBM] * len(x_out), out_shape=x_out,
        scratch_shapes=x_sem, input_output_aliases=aliases,
    )(*x_in)


def _inproj_fwd(x, g1, w_in_all, exchanges=()):
    L, D = x.shape
    ns, _, nc = w_in_all.shape
    tm = _tile(L, TM_PROJ)

    def body(x_ref, g_ref, w_ref, hn_ref, proj_ref, u_ref):
        hn = _rms_fwd(x_ref[...], g_ref[...]).astype(MXU_DTYPE)
        hn_ref[...] = hn
        for j in range(ns):
            proj_ref[:, j * nc:(j + 1) * nc] = _dot(hn, w_ref[j])
        _store_slabs(u_ref, proj_ref[:, 0:nc])

    return _call(
        body, name="inproj_fwd", grid=(L // tm,), exchanges=exchanges, semantics=("arbitrary",), operands=(x, g1, w_in_all),
        in_specs=[pl.BlockSpec((tm, D), lambda i: (i, 0)), _resident((1, D)), _resident(w_in_all.shape)],
        out_specs=[pl.BlockSpec((tm, D), lambda i: (i, 0)), pl.BlockSpec((tm, ns * nc), lambda i: (i, 0)), _slab_spec(nc, tm)],
        out_shape=[jax.ShapeDtypeStruct((L, D), MXU_DTYPE), jax.ShapeDtypeStruct((L, ns * nc), F32), _slab_shape(L, nc)],
    )


def _slab_shape(L, n):
    return jax.ShapeDtypeStruct((n // LANES, L, LANES), F32)


def _slab_spec(n, tm, index=lambda i: (0, i, 0)):
    return pl.BlockSpec((n // LANES, tm, LANES), index)


def _store_slabs(ref, value):
    for k in range(ref.shape[0]):
        ref[k] = value[:, k * LANES:(k + 1) * LANES]


def _load_slabs(ref):
    return jnp.concatenate([ref[k] for k in range(ref.shape[0])], axis=1)


SEG_ROWS = SUBLANES * SUBLANES


def _load_permuted(ref):
    tm = ref.shape[1]
    slabs = []
    for k in range(ref.shape[0]):
        tiles = [ref.at[k][pl.ds(b * SEG_ROWS + j, SUBLANES, stride=SUBLANES), :] for b in range(tm // SEG_ROWS) for j in range(SUBLANES)]
        slabs.append(jnp.concatenate(tiles, axis=0))
    return jnp.concatenate(slabs, axis=1)


def _store_permuted(ref, value):
    tm = ref.shape[1]
    for k in range(ref.shape[0]):
        for b in range(tm // SEG_ROWS):
            for j in range(SUBLANES):
                r = b * SEG_ROWS + j * SUBLANES
                ref.at[k][pl.ds(b * SEG_ROWS + j, SUBLANES, stride=SUBLANES), :] = value[r:r + SUBLANES, k * LANES:(k + 1) * LANES]


def _scan_tile(xr, xi, hr, hi, coef_ref, lanes, reverse):
    for k, j in ((1, 0), (2, 2), (4, 4)):
        ar = coef_ref[0, j, :, lanes]
        ai = coef_ref[0, j + 1, :, lanes]
        shift = SUBLANES - k if reverse else k
        sr = pltpu.roll(xr, shift, 0)
        si = pltpu.roll(xi, shift, 0)
        xr, xi = xr + (ar * sr - ai * si), xi + (ar * si + ai * sr)
    pr = coef_ref[0, 6, :, lanes]
    pi = coef_ref[0, 7, :, lanes]
    return xr + (pr * hr - pi * hi), xi + (pr * hi + pi * hr)


def _scan_block(read, write, hr, hi, coef_ref, lanes, reverse):
    order = list(range(SUBLANES - 1, -1, -1) if reverse else range(SUBLANES))
    near = 8 + 2 * order[0]
    ar = coef_ref[0, near, :, lanes]
    ai = coef_ref[0, near + 1, :, lanes]
    xr, xi = read(order[0])
    local = {order[0]: (xr, xi)}
    for j in order[1:]:
        br, bi = read(j)
        xr, xi = br + (ar * xr - ai * xi), bi + (ar * xi + ai * xr)
        local[j] = (xr, xi)
    er, ei = _scan_tile(xr, xi, hr, hi, coef_ref, lanes, reverse)
    edge = lax.broadcasted_iota(jnp.int32, er.shape, 0) == (SUBLANES - 1 if reverse else 0)
    shift = SUBLANES - 1 if reverse else 1
    pr = jnp.where(edge, hr, pltpu.roll(er, shift, 0))
    pi = jnp.where(edge, hi, pltpu.roll(ei, shift, 0))
    for j in range(SUBLANES):
        cr = coef_ref[0, 8 + 2 * j, :, lanes]
        ci = coef_ref[0, 9 + 2 * j, :, lanes]
        xr, xi = local[j]
        write(j, xr + (cr * pr - ci * pi), xi + (cr * pi + ci * pr))
    end = 0 if reverse else SUBLANES - 1
    return jnp.broadcast_to(er[end:end + 1, :], er.shape), jnp.broadcast_to(ei[end:end + 1, :], ei.shape)


def _s5_fwd(u4, bmat, cmat, coef, dskip, exchanges=()):
    L = u4.shape[1]
    tm = _tile(L, TM_S5)
    lc = min(LANE_CHUNK, WB)

    def body(u_ref, bm_ref, cm_ref, coef_ref, d_ref, sre_ref, sim_ref, ys_ref, hr_ref, hi_ref):
        @pl.when(pl.program_id(1) == 0)
        def _():
            hr_ref[...] = jnp.zeros_like(hr_ref)
            hi_ref[...] = jnp.zeros_like(hi_ref)

        u = _load_permuted(u_ref)
        bu = _dot(u.astype(MXU_DTYPE), bm_ref[0])
        sre_ref[...] = bu[:, :WB]
        sim_ref[...] = bu[:, WB:]
        for c in range(WB // lc):
            lanes = slice(c * lc, (c + 1) * lc)
            hr, hi = hr_ref[:, lanes], hi_ref[:, lanes]
            for b in range(tm // SEG_ROWS):
                rows = lambda j, b=b: slice(b * SEG_ROWS + j * SUBLANES, b * SEG_ROWS + (j + 1) * SUBLANES)

                def read(j, rows=rows, lanes=lanes):
                    return sre_ref[rows(j), lanes], sim_ref[rows(j), lanes]

                def write(j, xr, xi, rows=rows, lanes=lanes):
                    sre_ref[rows(j), lanes] = xr
                    sim_ref[rows(j), lanes] = xi

                hr, hi = _scan_block(read, write, hr, hi, coef_ref, lanes, False)
            hr_ref[:, lanes] = hr
            hi_ref[:, lanes] = hi
        ys = _dot(sre_ref[...].astype(MXU_DTYPE), cm_ref[0, :WB, :]) + _dot(sim_ref[...].astype(MXU_DTYPE), cm_ref[0, WB:, :])
        _store_permuted(ys_ref, ys + d_ref[0] * u)

    return _call(
        body, name="s5_fwd", grid=(N_GBLK, L // tm), exchanges=exchanges, semantics=("arbitrary", "arbitrary"),
        operands=(u4, bmat, cmat, coef, dskip),
        in_specs=[
            _slab_spec(UB, tm, lambda b, i: (b, i, 0)),
            pl.BlockSpec((1, UB, 2 * WB), lambda b, i: (b, 0, 0)),
            pl.BlockSpec((1, 2 * WB, UB), lambda b, i: (b, 0, 0)),
            pl.BlockSpec((1, N_TABLES, SUBLANES, WB), lambda b, i: (b, 0, 0, 0)),
            pl.BlockSpec((1, 1, UB), lambda b, i: (b, 0, 0)),
        ],
        out_specs=[
            pl.BlockSpec((tm, WB), lambda b, i: (i, b)),
            pl.BlockSpec((tm, WB), lambda b, i: (i, b)),
            _slab_spec(UB, tm, lambda b, i: (b, i, 0)),
        ],
        out_shape=[
            jax.ShapeDtypeStruct((L, N_GBLK * WB), F32),
            jax.ShapeDtypeStruct((L, N_GBLK * WB), F32),
            _slab_shape(L, D_SSM),
        ],
        scratch_shapes=[pltpu.VMEM((SUBLANES, WB), F32), pltpu.VMEM((SUBLANES, WB), F32)],
    )


def _tail_fwd(x, ys, proj, w_glu, conv_w, g_ssm, g_conv, w_out, g_post, exchanges=()):
    L, D = x.shape
    tm = _tile(L, TM_TAIL)

    def body(x_ref, ys_ref, h_ref, bg_ref, cg_ref, wglu_ref, cw_ref, gs_ref, gc_ref, wout_ref, gp_ref,
             ycat_ref, o_ref, x1_ref, zbuf):
        @pl.when(pl.program_id(0) == 0)
        def _():
            zbuf[0:SUBLANES, :] = jnp.zeros((SUBLANES, D_CONV), F32)

        y1, _ = _gelu(_load_slabs(ys_ref))
        y2 = y1 * jax.nn.sigmoid(_dot(y1.astype(MXU_DTYPE), wglu_ref[...]))
        ycat_ref[:, :D_SSM] = _rms_fwd(y2, gs_ref[...]).astype(MXU_DTYPE)
        z = cg_ref[...] * h_ref[...]
        zbuf[SUBLANES:, :] = z
        conv = cw_ref[0:1, :] * zbuf[SUBLANES - 2:SUBLANES - 2 + tm, :] + cw_ref[1:2, :] * zbuf[SUBLANES - 1:SUBLANES - 1 + tm, :] + cw_ref[2:3, :] * z
        zbuf[0:SUBLANES, :] = zbuf[tm:tm + SUBLANES, :]
        ycat_ref[:, D_SSM:] = _rms_fwd(bg_ref[...] * conv, gc_ref[...]).astype(MXU_DTYPE)
        o = _dot(ycat_ref[...], wout_ref[...])
        o_ref[...] = o
        x1_ref[...] = x_ref[...] + _rms_fwd(o, gp_ref[...])

    row = lambda i: (i, 0)
    return _call(
        body, name="tail_fwd", grid=(L // tm,), exchanges=exchanges, semantics=("arbitrary",),
        operands=(x, ys, proj, proj, proj, w_glu, conv_w, g_ssm, g_conv, w_out, g_post),
        in_specs=[
            pl.BlockSpec((tm, D), row), _slab_spec(D_SSM, tm),
            pl.BlockSpec((tm, D_CONV), lambda i: (i, 1)), pl.BlockSpec((tm, D_CONV), lambda i: (i, 2)),
            pl.BlockSpec((tm, D_CONV), lambda i: (i, 3)),
            _resident(w_glu.shape), _resident(conv_w.shape), _resident(g_ssm.shape), _resident(g_conv.shape),
            _resident(w_out.shape), _resident(g_post.shape),
        ],
        out_specs=[pl.BlockSpec((tm, D), row), pl.BlockSpec((tm, D), row), pl.BlockSpec((tm, D), row)],
        out_shape=[jax.ShapeDtypeStruct((L, D), MXU_DTYPE), jax.ShapeDtypeStruct((L, D), F32), jax.ShapeDtypeStruct((L, D), F32)],
        scratch_shapes=[pltpu.VMEM((tm + SUBLANES, D_CONV), F32)],
    )


def _mlp_fwd(x1, target, w_up_all, w_down, g_pre, g_post):
    L, D = x1.shape
    ns, _, fc = w_up_all.shape
    tm = _tile(L, TM_MLP)

    def body(x1_ref, t_ref, wup_ref, wdn_ref, gpre_ref, gpost_ref, hn2_ref, up_ref, m_ref, dx2_ref, loss_ref):
        @pl.when(pl.program_id(0) == 0)
        def _():
            loss_ref[...] = jnp.zeros_like(loss_ref)

        x1v = x1_ref[...]
        hn2 = _rms_fwd(x1v, gpre_ref[...]).astype(MXU_DTYPE)
        hn2_ref[...] = hn2
        m = jnp.zeros((tm, D), F32)
        for j in range(ns):
            up = _dot(hn2, wup_ref[j])
            up_ref[:, j * fc:(j + 1) * fc] = up
            act = jnp.square(jnp.maximum(up, 0.0)).astype(MXU_DTYPE)
            m = m + _dot(act, wdn_ref[j * fc:(j + 1) * fc, :])
        m_ref[...] = m
        err = x1v + _rms_fwd(m, gpost_ref[...]) - t_ref[...]
        loss_ref[...] += 0.5 * jnp.sum(jnp.mean(err * err, axis=-1, keepdims=True))
        dx2_ref[...] = err * (1.0 / D)

    row = lambda i: (i, 0)
    return pl.pallas_call(
        body, name="mlp_fwd", grid=(L // tm,),
        in_specs=[pl.BlockSpec((tm, D), row), pl.BlockSpec((tm, D), row), _resident(w_up_all.shape), _resident(w_down.shape),
                  _resident(g_pre.shape), _resident(g_post.shape)],
        out_specs=[pl.BlockSpec((tm, D), row), pl.BlockSpec((tm, ns * fc), row), pl.BlockSpec((tm, D), row),
                   pl.BlockSpec((tm, D), row), pl.BlockSpec((SUBLANES, 128), lambda i: (0, 0))],
        out_shape=[jax.ShapeDtypeStruct((L, D), MXU_DTYPE), jax.ShapeDtypeStruct((L, ns * fc), F32), jax.ShapeDtypeStruct((L, D), F32),
                   jax.ShapeDtypeStruct((L, D), F32), jax.ShapeDtypeStruct((SUBLANES, 128), F32)],
        compiler_params=_params(("arbitrary",)),
    )(x1, target, w_up_all, w_down, g_pre, g_post)


def _mlp_bwd(dx2, m, up, x1, w_up_all, w_down, g_pre, g_post):
    L, D = x1.shape
    ns, _, fc = w_up_all.shape
    tm = _tile(L, TM_MLP)

    def body(dx2_ref, m_ref, up_ref, x1_ref, wup_ref, wdn_ref, gpre_ref, gpost_ref,
             dm_ref, dup_ref, act_ref, dx1_ref, dgpost_ref, dgpre_ref):
        @pl.when(pl.program_id(0) == 0)
        def _():
            dgpost_ref[...] = jnp.zeros_like(dgpost_ref)
            dgpre_ref[...] = jnp.zeros_like(dgpre_ref)

        dx2v = dx2_ref[...]
        dm, dg = _rms_bwd(m_ref[...], gpost_ref[...], dx2v)
        dgpost_ref[...] += dg
        dm_b = dm.astype(MXU_DTYPE)
        dm_ref[...] = dm_b
        dhn2 = jnp.zeros((tm, D), F32)
        for j in range(ns):
            cols = slice(j * fc, (j + 1) * fc)
            relu = jnp.maximum(up_ref[:, cols], 0.0)
            act_ref[:, cols] = jnp.square(relu).astype(MXU_DTYPE)
            dup = (_dot_nt(dm_b, wdn_ref[cols, :]) * (2.0 * relu)).astype(MXU_DTYPE)
            dup_ref[:, cols] = dup
            dhn2 = dhn2 + _dot_nt(dup, wup_ref[j])
        dx, dg = _rms_bwd(x1_ref[...], gpre_ref[...], dhn2)
        dgpre_ref[...] += dg
        dx1_ref[...] = dx2v + dx

    row = lambda i: (i, 0)
    vec = pl.BlockSpec((1, D), lambda i: (0, 0))
    return pl.pallas_call(
        body, name="mlp_bwd", grid=(L // tm,),
        in_specs=[pl.BlockSpec((tm, D), row), pl.BlockSpec((tm, D), row), pl.BlockSpec((tm, ns * fc), row), pl.BlockSpec((tm, D), row),
                  _resident(w_up_all.shape), _resident(w_down.shape), _resident(g_pre.shape), _resident(g_post.shape)],
        out_specs=[pl.BlockSpec((tm, D), row), pl.BlockSpec((tm, ns * fc), row), pl.BlockSpec((tm, ns * fc), row),
                   pl.BlockSpec((tm, D), row), vec, vec],
        out_shape=[jax.ShapeDtypeStruct((L, D), MXU_DTYPE), jax.ShapeDtypeStruct((L, ns * fc), MXU_DTYPE),
                   jax.ShapeDtypeStruct((L, ns * fc), MXU_DTYPE), jax.ShapeDtypeStruct((L, D), F32),
                   jax.ShapeDtypeStruct((1, D), F32), jax.ShapeDtypeStruct((1, D), F32)],
        compiler_params=_params(("arbitrary",)),
    )(dx2, m, up, x1, w_up_all, w_down, g_pre, g_post)


def _tail_bwd(dx1, o, ys, proj, w_glu, conv_w, g_ssm, g_conv, w_out, g_post, exchanges=()):
    L, D = dx1.shape
    tm = _tile(L, TM_TAIL)
    nt = L // tm
    hb = tm // SUBLANES

    def body(dx1_ref, o_ref, ys_ref, h_ref, bg_ref, cg_ref, hh_ref, hcg_ref, wglu_ref, cw_ref, gs_ref, gc_ref, wout_ref, gp_ref,
             do_ref, da_ref, y1_ref, dys_ref, dhbc_ref, dgp_ref, dgs_ref, dgc_ref, dcw_ref, zbuf, dcbuf):
        step = pl.program_id(0)

        @pl.when(step == 0)
        def _():
            dcbuf[tm:, :] = jnp.zeros((SUBLANES, D_CONV), F32)
            dgp_ref[...] = jnp.zeros_like(dgp_ref)
            dgs_ref[...] = jnp.zeros_like(dgs_ref)
            dgc_ref[...] = jnp.zeros_like(dgc_ref)
            dcw_ref[...] = jnp.zeros_like(dcw_ref)

        do, dg = _rms_bwd(o_ref[...], gp_ref[...], dx1_ref[...])
        dgp_ref[...] += dg
        do_b = do.astype(MXU_DTYPE)
        do_ref[...] = do_b
        dycat = _dot_nt(do_b, wout_ref[...])
        y1, dgelu = _gelu(_load_slabs(ys_ref))
        y1_b = y1.astype(MXU_DTYPE)
        y1_ref[...] = y1_b
        s = jax.nn.sigmoid(_dot(y1_b, wglu_ref[...]))
        dy2, dg = _rms_bwd(y1 * s, gs_ref[...], dycat[:, :D_SSM])
        dgs_ref[...] += dg
        da_b = (dy2 * y1 * s * (1.0 - s)).astype(MXU_DTYPE)
        da_ref[...] = da_b
        _store_slabs(dys_ref, (dy2 * s + _dot_nt(da_b, wglu_ref[...])) * dgelu)
        h = h_ref[...]
        cg = cg_ref[...]
        bg = bg_ref[...]
        z = cg * h
        first = step == nt - 1
        zbuf[0:SUBLANES, :] = jnp.where(first, 0.0, hcg_ref[...] * hh_ref[...])
        zbuf[SUBLANES:, :] = z
        z1 = zbuf[SUBLANES - 1:SUBLANES - 1 + tm, :]
        z2 = zbuf[SUBLANES - 2:SUBLANES - 2 + tm, :]
        conv = cw_ref[0:1, :] * z2 + cw_ref[1:2, :] * z1 + cw_ref[2:3, :] * z
        dyc, dg = _rms_bwd(bg * conv, gc_ref[...], dycat[:, D_SSM:])
        dgc_ref[...] += dg
        dconv = dyc * bg
        dcw_ref[0:1, :] += jnp.sum(dconv * z2, axis=0, keepdims=True)
        dcw_ref[1:2, :] += jnp.sum(dconv * z1, axis=0, keepdims=True)
        dcw_ref[2:3, :] += jnp.sum(dconv * z, axis=0, keepdims=True)
        dcbuf[0:tm, :] = dconv
        dz = cw_ref[2:3, :] * dconv + cw_ref[1:2, :] * dcbuf[1:1 + tm, :] + cw_ref[0:1, :] * dcbuf[2:2 + tm, :]
        dcbuf[tm:, :] = dcbuf[0:SUBLANES, :]
        dhbc_ref[:, 0:D_CONV] = (dz * cg).astype(MXU_DTYPE)
        dhbc_ref[:, D_CONV:2 * D_CONV] = (dyc * conv).astype(MXU_DTYPE)
        dhbc_ref[:, 2 * D_CONV:] = (dz * h).astype(MXU_DTYPE)

    rev = lambda i: (nt - 1 - i, 0)
    slab = lambda i: (0, nt - 1 - i, 0)
    col = lambda c: (lambda i: (nt - 1 - i, c))
    halo = lambda c: (lambda i: (jnp.maximum((nt - 1 - i) * hb - 1, 0), c))
    vec = lambda n: pl.BlockSpec((1, n), lambda i: (0, 0))
    return _call(
        body, name="tail_bwd", grid=(nt,), exchanges=exchanges, semantics=("arbitrary",),
        operands=(dx1, o, ys, proj, proj, proj, proj, proj, w_glu, conv_w, g_ssm, g_conv, w_out, g_post),
        in_specs=[
            pl.BlockSpec((tm, D), rev), pl.BlockSpec((tm, D), rev), _slab_spec(D_SSM, tm, slab),
            pl.BlockSpec((tm, D_CONV), col(1)), pl.BlockSpec((tm, D_CONV), col(2)), pl.BlockSpec((tm, D_CONV), col(3)),
            pl.BlockSpec((SUBLANES, D_CONV), halo(1)), pl.BlockSpec((SUBLANES, D_CONV), halo(3)),
            _resident(w_glu.shape), _resident(conv_w.shape), _resident(g_ssm.shape), _resident(g_conv.shape),
            _resident(w_out.shape), _resident(g_post.shape),
        ],
        out_specs=[
            pl.BlockSpec((tm, D), rev), pl.BlockSpec((tm, D_SSM), rev), pl.BlockSpec((tm, D_SSM), rev), _slab_spec(D_SSM, tm, slab),
            pl.BlockSpec((tm, 3 * D_CONV), rev), vec(D), vec(D_SSM), vec(D_CONV),
            pl.BlockSpec((SUBLANES, D_CONV), lambda i: (0, 0)),
        ],
        out_shape=[
            jax.ShapeDtypeStruct((L, D), MXU_DTYPE), jax.ShapeDtypeStruct((L, D_SSM), MXU_DTYPE), jax.ShapeDtypeStruct((L, D_SSM), MXU_DTYPE),
            _slab_shape(L, D_SSM), jax.ShapeDtypeStruct((L, 3 * D_CONV), MXU_DTYPE),
            jax.ShapeDtypeStruct((1, D), F32), jax.ShapeDtypeStruct((1, D_SSM), F32), jax.ShapeDtypeStruct((1, D_CONV), F32),
            jax.ShapeDtypeStruct((SUBLANES, D_CONV), F32),
        ],
        scratch_shapes=[pltpu.VMEM((tm + SUBLANES, D_CONV), F32), pltpu.VMEM((tm + SUBLANES, D_CONV), F32)],
    )


def _s5_bwd(dys, u4, s_re, s_im, bmat, cmat, coef_rev, dskip, exchanges=()):
    L = dys.shape[1]
    tm = _tile(L, TM_S5)
    nt = L // tm
    lc = min(LANE_CHUNK, WB)

    def body(dys_ref, u_ref, sre_ref, sim_ref, bm_ref, cm_ref, coef_ref, d_ref,
             du_ref, gb_ref, gc_ref, q_ref, gd_ref, dr_ref, di_ref, lr_ref, li_ref, hr_ref, hi_ref, qr_acc, qi_acc):
        step = pl.program_id(1)

        @pl.when(step == 0)
        def _():
            for ref in (hr_ref, hi_ref, qr_acc, qi_acc, gb_ref, gc_ref, gd_ref):
                ref[...] = jnp.zeros_like(ref)

        dys_v = _load_permuted(dys_ref)
        u = _load_permuted(u_ref)
        dys_b = dys_v.astype(MXU_DTYPE)
        u_b = u.astype(MXU_DTYPE)
        d = _dot_nt(dys_b, cm_ref[0])
        dr_ref[...] = d[:, :WB]
        di_ref[...] = d[:, WB:]
        for c in range(WB // lc):
            lanes = slice(c * lc, (c + 1) * lc)
            hr, hi = hr_ref[:, lanes], hi_ref[:, lanes]
            q = [qr_acc[:, lanes], qi_acc[:, lanes]]
            for b in range(tm // SEG_ROWS - 1, -1, -1):
                rows = lambda j, b=b: slice(b * SEG_ROWS + j * SUBLANES, b * SEG_ROWS + (j + 1) * SUBLANES)

                def read(j, rows=rows, lanes=lanes):
                    return dr_ref[rows(j), lanes], di_ref[rows(j), lanes]

                def write(j, xr, xi, rows=rows, lanes=lanes, q=q):
                    lr_ref[rows(j), lanes] = xr
                    li_ref[rows(j), lanes] = xi
                    er = xr - dr_ref[rows(j), lanes]
                    ei = xi - di_ref[rows(j), lanes]
                    sr = sre_ref[rows(j), lanes]
                    si = sim_ref[rows(j), lanes]
                    q[0] = q[0] + (er * sr + ei * si)
                    q[1] = q[1] + (ei * sr - er * si)

                hr, hi = _scan_block(read, write, hr, hi, coef_ref, lanes, True)
            hr_ref[:, lanes] = hr
            hi_ref[:, lanes] = hi
            qr_acc[:, lanes] = q[0]
            qi_acc[:, lanes] = q[1]
        lr_b = lr_ref[...].astype(MXU_DTYPE)
        li_b = li_ref[...].astype(MXU_DTYPE)
        _store_permuted(du_ref, _dot_nt(lr_b, bm_ref[0, :, :WB]) + _dot_nt(li_b, bm_ref[0, :, WB:]) + d_ref[0] * dys_v)
        gb_ref[0, :WB, :] += _dot_tn(lr_b, u_b)
        gb_ref[0, WB:, :] += _dot_tn(li_b, u_b)
        gc_ref[0, :, :WB] += _dot_tn(dys_b, sre_ref[...].astype(MXU_DTYPE))
        gc_ref[0, :, WB:] += _dot_tn(dys_b, sim_ref[...].astype(MXU_DTYPE))
        gd_ref[0] += jnp.sum(dys_v * u, axis=0, keepdims=True)

        @pl.when(step == nt - 1)
        def _():
            q_ref[0, 0:1, :] = jnp.sum(qr_acc[...], axis=0, keepdims=True)
            q_ref[0, 1:2, :] = jnp.sum(qi_acc[...], axis=0, keepdims=True)

    rev = lambda b, i: (nt - 1 - i, b)
    slab = lambda b, i: (b, nt - 1 - i, 0)
    blk = lambda b, i: (b, 0, 0)
    return _call(
        body, name="s5_bwd", grid=(N_GBLK, nt), exchanges=exchanges, semantics=("arbitrary", "arbitrary"),
        operands=(dys, u4, s_re, s_im, bmat, cmat, coef_rev, dskip),
        in_specs=[
            _slab_spec(UB, tm, slab), _slab_spec(UB, tm, slab), pl.BlockSpec((tm, WB), rev), pl.BlockSpec((tm, WB), rev),
            pl.BlockSpec((1, UB, 2 * WB), blk), pl.BlockSpec((1, 2 * WB, UB), blk),
            pl.BlockSpec((1, N_TABLES, SUBLANES, WB), lambda b, i: (b, 0, 0, 0)), pl.BlockSpec((1, 1, UB), blk),
        ],
        out_specs=[
            _slab_spec(UB, tm, slab), pl.BlockSpec((1, 2 * WB, UB), blk), pl.BlockSpec((1, UB, 2 * WB), blk),
            pl.BlockSpec((1, 2, WB), blk), pl.BlockSpec((1, 1, UB), blk),
        ],
        out_shape=[
            _slab_shape(L, D_SSM), jax.ShapeDtypeStruct((N_GBLK, 2 * WB, UB), F32),
            jax.ShapeDtypeStruct((N_GBLK, UB, 2 * WB), F32), jax.ShapeDtypeStruct((N_GBLK, 2, WB), F32),
            jax.ShapeDtypeStruct((N_GBLK, 1, UB), F32),
        ],
        scratch_shapes=[pltpu.VMEM((tm, WB), F32)] * 4 + [pltpu.VMEM((SUBLANES, WB), F32)] * 4,
    )


def _inproj_bwd(du, dhbc, x, dx1, w_in_all, g1):
    L, D = x.shape
    ns, _, nc = w_in_all.shape
    tm = _tile(L, TM_PROJ)

    def body(du_ref, dhbc_ref, x_ref, dx1_ref, w_ref, g_ref, gx_ref, dproj_ref, dg_ref):
        @pl.when(pl.program_id(0) == 0)
        def _():
            dg_ref[...] = jnp.zeros_like(dg_ref)

        du_b = _load_slabs(du_ref).astype(MXU_DTYPE)
        dproj_ref[:, :nc] = du_b
        dproj_ref[:, nc:] = dhbc_ref[...]
        dhn = _dot_nt(du_b, w_ref[0])
        for j in range(1, ns):
            dhn = dhn + _dot_nt(dhbc_ref[:, (j - 1) * nc:j * nc], w_ref[j])
        dx, dg = _rms_bwd(x_ref[...], g_ref[...], dhn)
        dg_ref[...] += dg
        gx_ref[...] = dx1_ref[...] + dx

    row = lambda i: (i, 0)
    return pl.pallas_call(
        body, name="inproj_bwd", grid=(L // tm,),
        in_specs=[_slab_spec(nc, tm), pl.BlockSpec((tm, (ns - 1) * nc), row), pl.BlockSpec((tm, D), row), pl.BlockSpec((tm, D), row),
                  _resident(w_in_all.shape), _resident(g1.shape)],
        out_specs=[pl.BlockSpec((tm, D), row), pl.BlockSpec((tm, ns * nc), row), pl.BlockSpec((1, D), lambda i: (0, 0))],
        out_shape=[jax.ShapeDtypeStruct((L, D), F32), jax.ShapeDtypeStruct((L, ns * nc), MXU_DTYPE), jax.ShapeDtypeStruct((1, D), F32)],
        compiler_params=_params(("arbitrary",)),
    )(du, dhbc, x, dx1, w_in_all, g1)


def _matmul_tn(a, b, name, col_shards=1, exchanges=()):
    L, K = a.shape
    N = b.shape[1]
    tl = _tile(L, TL_TN)
    tk = _tile(K, 1024)
    nw = N // col_shards
    tn = _tile(nw, 1024)
    npb = nw // tn

    def body(a_ref, b_ref, o_ref):
        @pl.when(pl.program_id(2) == 0)
        def _():
            o_ref[...] = jnp.zeros_like(o_ref)

        o_ref[0] += _dot_tn(a_ref[...], b_ref[...])

    return _call(
        body, name=name, grid=(K // tk, N // tn, L // tl), exchanges=exchanges, semantics=("arbitrary", "arbitrary", "arbitrary"),
        operands=(a, b),
        in_specs=[pl.BlockSpec((tl, tk), lambda k, n, l: (l, k)), pl.BlockSpec((tl, tn), lambda k, n, l: (l, n))],
        out_specs=[pl.BlockSpec((1, tk, tn), lambda k, n, l: (n // npb, k, n % npb))],
        out_shape=[jax.ShapeDtypeStruct((col_shards, K, nw), F32)],
    )


def _ssm_discretize(lam_re, lam_im, log_dt, b_re, b_im):
    dt = jnp.exp(log_dt)[:, None]
    zr = lam_re * dt
    zi = lam_im * dt
    mag = jnp.exp(zr)
    abr = mag * jnp.cos(zi)
    abi = mag * jnp.sin(zi)
    nr, ni = abr - 1.0, abi
    den = lam_re * lam_re + lam_im * lam_im
    coef_r = (nr * lam_re + ni * lam_im) / den
    coef_i = (ni * lam_re - nr * lam_im) / den
    bbar_r = coef_r[..., None] * b_re - coef_i[..., None] * b_im
    bbar_i = coef_r[..., None] * b_im + coef_i[..., None] * b_re
    return zr, zi, bbar_r, bbar_i


def _block_diag(t):
    nb, g, r, c = t.shape
    eye = jnp.eye(g, dtype=t.dtype)
    return (t[:, :, :, None, :] * eye[None, :, None, :, None]).reshape(nb, g * r, g * c)


def _diag_blocks(m, r, c):
    nb = m.shape[0]
    g = m.shape[1] // r
    eye = jnp.eye(g, dtype=m.dtype)
    t = (m.reshape(nb, g, r, g, c) * eye[None, :, None, :, None]).sum(axis=3)
    return t.reshape(nb * g, r, c)


def _powers(a):
    def mul(p, q):
        return (p[0] * q[0] - p[1] * q[1], p[0] * q[1] + p[1] * q[0])

    a2 = mul(a, a)
    a3 = mul(a2, a)
    a4 = mul(a2, a2)
    return [a, a2, a3, a4, mul(a4, a), mul(a4, a2), mul(a4, a3), mul(a4, a4)]


def _scan_tables(ar, ai, reverse):
    pw = _powers((ar, -ai if reverse else ai))
    seg = _powers(pw[SUBLANES - 1])
    t = jnp.arange(SUBLANES)[:, None]
    tabs = []
    for k in (1, 2, 4):
        mask = (t + k <= SUBLANES - 1) if reverse else (t >= k)
        tabs += [jnp.where(mask, seg[k - 1][0][None, :], 0.0), jnp.where(mask, seg[k - 1][1][None, :], 0.0)]
    order = range(SUBLANES - 1, -1, -1) if reverse else range(SUBLANES)
    tabs += [jnp.stack([seg[j][0] for j in order]), jnp.stack([seg[j][1] for j in order])]
    for j in range(SUBLANES):
        p = pw[SUBLANES - 1 - j] if reverse else pw[j]
        tabs += [jnp.broadcast_to(p[0][None, :], (SUBLANES, p[0].shape[0])), jnp.broadcast_to(p[1][None, :], (SUBLANES, p[1].shape[0]))]
    coef = jnp.stack(tabs)
    return coef.reshape(N_TABLES, SUBLANES, N_GBLK, WB).transpose(2, 0, 1, 3)


def _ssm_matrices(lam_re, lam_im, log_dt, b_re, b_im, c_re, c_im):
    zr, zi, bbar_r, bbar_i = _ssm_discretize(lam_re, lam_im, log_dt, b_re, b_im)
    mag = jnp.exp(zr)
    ar = (mag * jnp.cos(zi)).reshape(-1)
    ai = (mag * jnp.sin(zi)).reshape(-1)
    blk = lambda t: t.reshape(N_GBLK, G_PER_BLK, *t.shape[1:])
    bmat = jnp.concatenate([_block_diag(blk(bbar_r).transpose(0, 1, 3, 2)), _block_diag(blk(bbar_i).transpose(0, 1, 3, 2))], axis=2)
    cmat = jnp.concatenate([_block_diag(blk(c_re).transpose(0, 1, 3, 2)), _block_diag(blk(-c_im).transpose(0, 1, 3, 2))], axis=1)
    return bmat.astype(MXU_DTYPE), cmat.astype(MXU_DTYPE), _scan_tables(ar, ai, False), _scan_tables(ar, ai, True)


def _ssm_param_grads(lam_re, lam_im, log_dt, b_re, b_im, gb, gc, q, gd):
    gbr = _diag_blocks(gb[:, :WB, :], STATE, GROUP)
    gbi = _diag_blocks(gb[:, WB:, :], STATE, GROUP)
    d_c_re = _diag_blocks(gc[:, :, :WB], GROUP, STATE)
    d_c_im = -_diag_blocks(gc[:, :, WB:], GROUP, STATE)
    qr = q[:, 0, :].reshape(N_GROUPS, STATE)
    qi = q[:, 1, :].reshape(N_GROUPS, STATE)
    _, vjp = jax.vjp(_ssm_discretize, lam_re, lam_im, log_dt, b_re, b_im)
    d_lam_re, d_lam_im, d_log_dt, d_b_re, d_b_im = vjp((qr, qi, gbr, gbi))
    return d_lam_re, d_lam_im, d_log_dt, d_b_re, d_b_im, d_c_re, d_c_im, gd.reshape(N_GROUPS, GROUP)


def _row_tile(rows, n):
    return _tile(rows, max(SUBLANES, (2 * 1024 * 1024) // (4 * n)))


def _pair_add(grad, other, core, name):
    ns, h, n = other.shape
    tr = _row_tile(h, n)
    nb = h // tr

    def body(c_ref, g_ref, o_ref, out_ref):
        out_ref[...] = (g_ref[...] + o_ref[...]).astype(WIRE_DTYPE)

    return pl.pallas_call(
        body, name=name,
        grid_spec=pltpu.PrefetchScalarGridSpec(
            num_scalar_prefetch=1, grid=(ns, nb),
            in_specs=[pl.BlockSpec((1, tr, n), lambda s, i, c: (s, c[0] * nb + i, 0)), pl.BlockSpec((1, tr, n), lambda s, i, c: (s, i, 0))],
            out_specs=pl.BlockSpec((1, tr, n), lambda s, i, c: (s, i, 0))),
        out_shape=jax.ShapeDtypeStruct(other.shape, WIRE_DTYPE),
        compiler_params=_params(("arbitrary", "arbitrary")),
    )(core, grad, other)


def _quad_sum(parts, core, name):
    ns, h, n = parts.shape
    tr = _row_tile(h, n)
    nb = h // tr

    def body(c_ref, p_ref, out_ref):
        p = [p_ref[k].astype(F32) for k in range(ns)]
        out_ref[...] = ((p[0] + p[1]) + p[2]) + p[3]

    return pl.pallas_call(
        body, name=name,
        grid_spec=pltpu.PrefetchScalarGridSpec(
            num_scalar_prefetch=1, grid=(nb,),
            in_specs=[pl.BlockSpec((ns, tr, n), lambda i, c: (0, i, 0))],
            out_specs=pl.BlockSpec((tr, n), lambda i, c: (c[0] * nb + i, 0))),
        out_shape=jax.ShapeDtypeStruct((2 * h, n), F32),
        compiler_params=_params(("arbitrary",)),
    )(core, parts)


def _adamw_math(w, g, m, v):
    m = ADAM_B1 * m + (1.0 - ADAM_B1) * g
    v = ADAM_B2 * v + (1.0 - ADAM_B2) * jnp.square(g)
    m_hat = m / (1.0 - ADAM_B1 ** ADAM_STEP)
    v_hat = v / (1.0 - ADAM_B2 ** ADAM_STEP)
    delta = -ADAM_LR * (m_hat / (jnp.sqrt(v_hat) + ADAM_EPS) + ADAM_WD * w)
    return delta, m, v


def _adamw(w, g, m, v, name):
    r, n = w.shape
    tr = _row_tile(r, n)

    def body(w_ref, g_ref, m_ref, v_ref, d_ref, nm_ref, nv_ref):
        d_ref[...], nm_ref[...], nv_ref[...] = _adamw_math(w_ref[...], g_ref[...], m_ref[...], v_ref[...])

    spec = pl.BlockSpec((tr, n), lambda i: (i, 0))
    return pl.pallas_call(
        body, name=name, grid=(r // tr,), in_specs=[spec] * 4, out_specs=[spec] * 3,
        out_shape=[jax.ShapeDtypeStruct((r, n), F32)] * 3,
        compiler_params=_params(("arbitrary",)),
    )(w, g, m, v)


LANES = 128
SMALL = ["g_pre_mix", "lam_re", "lam_im", "log_dt", "b_re", "b_im", "c_re", "c_im", "d_skip", "conv_w", "g_ssm_out", "g_conv_out",
         "g_post_mix", "g_pre_mlp", "g_post_mlp"]
TILE_SLOTS = {"b_re": (0, N_GROUPS), "b_im": (N_GROUPS, N_GROUPS), "c_re": (2 * N_GROUPS, N_GROUPS), "c_im": (3 * N_GROUPS, N_GROUPS),
              "lam_re": (4 * N_GROUPS, 2), "lam_im": (4 * N_GROUPS + 2, 2)}
N_TILE_SLOTS = 4 * N_GROUPS + 4
VEC_ROWS = {"g_pre_mix": 0, "g_post_mix": 1, "g_pre_mlp": 2, "g_post_mlp": 3, "g_ssm_out": 4, "g_conv_out": 5, "log_dt": 6}
ROW_LOSS, ROW_DSKIP, ROW_CONV, N_PACK_ROWS = 7, 8, 24, 32


def _kernel_form(name, a):
    if name in ("b_re", "b_im"):
        return jnp.transpose(a, (0, 1, 3, 2)).reshape(N_GROUPS, GROUP, STATE)
    if name in ("c_re", "c_im"):
        return a.reshape(N_GROUPS, GROUP, STATE)
    if name in ("lam_re", "lam_im"):
        return a.reshape(2, GROUP, STATE)
    if name == "d_skip":
        return jnp.transpose(a, (0, 2, 1)).reshape(GROUP, N_GROUPS)
    if name == "conv_w":
        return jnp.transpose(a, (1, 0, 2))
    return a


def _param_form(name, k):
    if name in ("b_re", "b_im"):
        return jnp.transpose(k.reshape(1, N_GROUPS, GROUP, STATE), (0, 1, 3, 2))
    if name in ("c_re", "c_im"):
        return k.reshape(1, N_GROUPS, GROUP, STATE)
    if name in ("lam_re", "lam_im"):
        return k.reshape(1, N_GROUPS, STATE)
    if name == "d_skip":
        return jnp.transpose(k.reshape(1, GROUP, N_GROUPS), (0, 2, 1))
    if name == "conv_w":
        return jnp.transpose(k, (1, 0, 2))
    return k


def _pack_tiles(g):
    t = lambda a: jnp.transpose(a, (0, 2, 1))
    lam = lambda a: a.reshape(2, GROUP, STATE)
    return jnp.concatenate([t(g["b_re"]), t(g["b_im"]), g["c_re"], g["c_im"], lam(g["lam_re"]), lam(g["lam_im"])], axis=0)


def _pack_rows(g, loss):
    row = lambda a: jnp.pad(a, ((0, 0), (0, D_MODEL - a.shape[1])))
    rows = [row(g[k][None]) for k in VEC_ROWS] + [row(loss[0:1]), row(g["d_skip"].T), row(g["conv_w"])]
    rows.append(jnp.zeros((N_PACK_ROWS - ROW_CONV - 3, D_MODEL), F32))
    return jnp.concatenate(rows, axis=0)


def _adamw_small(tiles, rows, w, m, v):
    nn = len(SMALL)

    def body(*refs):
        t_ref, r_ref = refs[0], refs[1]
        w_refs, m_refs, v_refs = refs[2:2 + nn], refs[2 + nn:2 + 2 * nn], refs[2 + 2 * nn:2 + 3 * nn]
        loss_ref, outs = refs[2 + 3 * nn], refs[3 + 3 * nn:]

        def tile_sum(first, count):
            total = t_ref[0, first:first + count]
            for d in range(1, N_DEV):
                total = total + t_ref[d, first:first + count]
            return total

        def row_sum(first, count, lanes):
            total = r_ref[0, first:first + count, 0:lanes]
            for d in range(1, N_DEV):
                total = total + r_ref[d, first:first + count, 0:lanes]
            return total

        def step(j, g, at=lambda ref: ref):
            delta, nm, nv = _adamw_math(at(w_refs[j])[...], g, at(m_refs[j])[...], at(v_refs[j])[...])
            at(outs[j])[...] = g
            at(outs[nn + j])[...] = delta
            at(outs[2 * nn + j])[...] = nm
            at(outs[3 * nn + j])[...] = nv

        loss_ref[...] = row_sum(ROW_LOSS, 1, LANES)
        chip = 2 * lax.axis_index("x") + lax.axis_index("y")
        for j, name in enumerate(SMALL):
            if name in TILE_SLOTS:
                step(j, tile_sum(*TILE_SLOTS[name]))
            elif name == "d_skip":
                step(j, row_sum(ROW_DSKIP, GROUP, N_GROUPS))
            elif name == "conv_w":
                full = row_sum(ROW_CONV, 3, D_CONV)
                mine = full[:, 0:LANES]
                for s in range(1, N_CHIPS):
                    mine = jnp.where(chip == s, full[:, s * LANES:(s + 1) * LANES], mine)
                for k in range(3):
                    step(j, mine[k:k + 1, :], at=lambda ref, k=k: ref.at[k])
            else:
                step(j, row_sum(VEC_ROWS[name], 1, w_refs[j].shape[1]))

    args = [tiles, rows] + [w[k] for k in SMALL] + [m[k] for k in SMALL] + [v[k] for k in SMALL]
    res = pl.pallas_call(
        body, name="adamw_small", in_specs=[VMEM] * len(args), out_specs=[VMEM] * (1 + 4 * nn),
        out_shape=[jax.ShapeDtypeStruct((1, LANES), F32)] + [jax.ShapeDtypeStruct(w[k].shape, F32) for k in SMALL] * 4,
        compiler_params=pltpu.CompilerParams(vmem_limit_bytes=VMEM_LIMIT),
    )(*args)
    return res[0], [dict(zip(SMALL, res[1 + q * nn:1 + (q + 1) * nn])) for q in range(4)]


WEIGHTS = ["g_pre_mix", "w_in", "lam_re", "lam_im", "log_dt", "b_re", "b_im", "c_re", "c_im", "d_skip", "w_glu", "conv_w",
           "g_ssm_out", "g_conv_out", "w_out", "g_post_mix", "g_pre_mlp", "w_up", "w_down", "g_post_mlp"]
BIG = ["w_in", "w_glu", "w_out", "w_up", "w_down"]


def kernel(x, g_pre_mix, w_in, lam_re, lam_im, log_dt, b_re, b_im, c_re, c_im, d_skip, w_glu, conv_w, g_ssm_out, g_conv_out, w_out, g_post_mix, g_pre_mlp, w_up, w_down, g_post_mlp, loss_target, m_g_pre_mix, m_w_in, m_lam_re, m_lam_im, m_log_dt, m_b_re, m_b_im, m_c_re, m_c_im, m_d_skip, m_w_glu, m_conv_w, m_g_ssm_out, m_g_conv_out, m_w_out, m_g_post_mix, m_g_pre_mlp, m_w_up, m_w_down, m_g_post_mlp, v_g_pre_mix, v_w_in, v_lam_re, v_lam_im, v_log_dt, v_b_re, v_b_im, v_c_re, v_c_im, v_d_skip, v_w_glu, v_conv_w, v_g_ssm_out, v_g_conv_out, v_w_out, v_g_post_mix, v_g_pre_mlp, v_w_up, v_w_down, v_g_post_mlp):
    w = dict(g_pre_mix=g_pre_mix, w_in=w_in, lam_re=lam_re, lam_im=lam_im, log_dt=log_dt, b_re=b_re, b_im=b_im, c_re=c_re, c_im=c_im,
             d_skip=d_skip, w_glu=w_glu, conv_w=conv_w, g_ssm_out=g_ssm_out, g_conv_out=g_conv_out, w_out=w_out, g_post_mix=g_post_mix,
             g_pre_mlp=g_pre_mlp, w_up=w_up, w_down=w_down, g_post_mlp=g_post_mlp)
    m = dict(g_pre_mix=m_g_pre_mix, w_in=m_w_in, lam_re=m_lam_re, lam_im=m_lam_im, log_dt=m_log_dt, b_re=m_b_re, b_im=m_b_im, c_re=m_c_re,
             c_im=m_c_im, d_skip=m_d_skip, w_glu=m_w_glu, conv_w=m_conv_w, g_ssm_out=m_g_ssm_out, g_conv_out=m_g_conv_out, w_out=m_w_out,
             g_post_mix=m_g_post_mix, g_pre_mlp=m_g_pre_mlp, w_up=m_w_up, w_down=m_w_down, g_post_mlp=m_g_post_mlp)
    v = dict(g_pre_mix=v_g_pre_mix, w_in=v_w_in, lam_re=v_lam_re, lam_im=v_lam_im, log_dt=v_log_dt, b_re=v_b_re, b_im=v_b_im, c_re=v_c_re,
             c_im=v_c_im, d_skip=v_d_skip, w_glu=v_w_glu, conv_w=v_conv_w, g_ssm_out=v_g_ssm_out, g_conv_out=v_g_conv_out, w_out=v_w_out,
             g_post_mix=v_g_post_mix, g_pre_mlp=v_g_pre_mlp, w_up=v_w_up, w_down=v_w_down, g_post_mlp=v_g_post_mlp)
    w_dev, m_dev, v_dev = w, m, v
    w, m, v = ({k: a[0] for k, a in d.items()} for d in (w, m, v))
    core = lax.axis_index("c").astype(jnp.int32).reshape(1)

    xs, target = x[0], loss_target[0]
    g1 = w["g_pre_mix"][None]
    g_ssm, g_conv = w["g_ssm_out"][None], w["g_conv_out"][None]
    g_post_mix, g_pre_mlp, g_post_mlp = w["g_post_mix"][None], w["g_pre_mlp"][None], w["g_post_mlp"][None]
    bmat, cmat, coef_f, coef_r = _ssm_matrices(w["lam_re"], w["lam_im"], w["log_dt"], w["b_re"], w["b_im"], w["c_re"], w["c_im"])
    dskip = w["d_skip"].reshape(N_GBLK, 1, UB)
    shard = {k: w[k].astype(MXU_DTYPE) for k in BIG}
    conv_pad = jnp.pad(w["conv_w"], ((0, SUBLANES - 3), (0, 0)))

    (w_in_all,) = _run_exchanges([_GatherForward([shard["w_in"]])], "ag_w_in")
    hn, proj, u4, w_glu_all, w_out_all, conv_all = _inproj_fwd(
        xs, g1, w_in_all, exchanges=[_Gather([shard["w_glu"], shard["w_out"], conv_pad], [False, False, False])])
    s_re, s_im, ys, w_up_all, w_down_all = _s5_fwd(
        u4, bmat, cmat, coef_f, dskip, exchanges=[_Gather([shard["w_up"], shard["w_down"]], [True, True])])
    w_glu_f, w_out_f = w_glu_all.reshape(D_SSM, D_SSM), w_out_all.reshape(D_MODEL, D_MODEL)
    conv_f = jnp.transpose(conv_all, (1, 0, 2)).reshape(SUBLANES, D_CONV)
    ycat, o, x1, w_up_all, w_down_all = _tail_fwd(xs, ys, proj, w_glu_f, conv_f, g_ssm, g_conv, w_out_f, g_post_mix,
                                                  exchanges=[_Forward([w_up_all, w_down_all])])
    w_down_f = w_down_all.reshape(D_FF, D_MODEL)
    hn2, up, m_act, dx2, loss = _mlp_fwd(x1, target, w_up_all, w_down_f, g_pre_mlp, g_post_mlp)

    dm, dup, act, dx1, dg_post_mlp, dg_pre_mlp = _mlp_bwd(dx2, m_act, up, x1, w_up_all, w_down_f, g_pre_mlp, g_post_mlp)
    gw_down = _matmul_tn(act, dm, "dw_down")[0].reshape(N_CHIPS, D_FF // N_CHIPS, D_MODEL)
    gw_up = _matmul_tn(hn2, dup, "dw_up", col_shards=N_CHIPS)[0]
    do, da, y1, dys, dhbc, dg_post_mix, dg_ssm, dg_conv, dconv_w, o_down, o_up = _tail_bwd(
        dx1, o, ys, proj, w_glu_f, conv_f, g_ssm, g_conv, w_out_f, g_post_mix, exchanges=[_Pair([gw_down, gw_up])])
    p_down = _pair_add(gw_down, o_down, core, "pair_add_w_down")
    p_up = _pair_add(gw_up, o_up, core, "pair_add_w_up")
    gw_out = _matmul_tn(ycat, do, "dw_out")[0].reshape(N_CHIPS, D_MODEL // N_CHIPS, D_MODEL)
    gw_glu = _matmul_tn(y1, da, "dw_glu")[0].reshape(N_CHIPS, D_SSM // N_CHIPS, D_SSM)
    du, gb, gc, q, gd, q_down, q_up, o_out, o_glu = _s5_bwd(
        dys, u4, s_re, s_im, bmat, cmat, coef_r, dskip, exchanges=[_Chip([p_down, p_up]), _Pair([gw_out, gw_glu])])
    h_down = _quad_sum(q_down, core, "quad_sum_w_down")
    h_up = _quad_sum(q_up, core, "quad_sum_w_up")
    p_out = _pair_add(gw_out, o_out, core, "pair_add_w_out")
    p_glu = _pair_add(gw_glu, o_glu, core, "pair_add_w_glu")
    grad_x, dproj, dg_pre_mix = _inproj_bwd(du, dhbc, xs, dx1, w_in_all, g1)
    d_lam_re, d_lam_im, d_log_dt, d_b_re, d_b_im, d_c_re, d_c_im, d_d_skip = _ssm_param_grads(
        w["lam_re"], w["lam_im"], w["log_dt"], w["b_re"], w["b_im"], gb, gc, q, gd)
    small = {
        "g_pre_mix": dg_pre_mix[0], "lam_re": d_lam_re, "lam_im": d_lam_im, "log_dt": d_log_dt, "b_re": d_b_re, "b_im": d_b_im,
        "c_re": d_c_re, "c_im": d_c_im, "d_skip": d_d_skip, "conv_w": dconv_w[:3], "g_ssm_out": dg_ssm[0], "g_conv_out": dg_conv[0],
        "g_post_mix": dg_post_mix[0], "g_pre_mlp": dg_pre_mlp[0], "g_post_mlp": dg_post_mlp[0],
    }
    gw_in, g_down, g_up, q_out, q_glu, tiles, rows = _matmul_tn(
        hn, dproj, "dw_in", col_shards=N_CHIPS,
        exchanges=[_Share([h_down, h_up]), _Chip([p_out, p_glu]), _GatherSmall(_pack_tiles(small)), _GatherSmall(_pack_rows(small, loss))])
    h_out = _quad_sum(q_out, core, "quad_sum_w_out")
    h_glu = _quad_sum(q_glu, core, "quad_sum_w_glu")
    (o_in,) = _run_exchanges([_Pair([gw_in])], "rs_pair_w_in")
    p_in = _pair_add(gw_in, o_in, core, "pair_add_w_in")
    q_in, g_out, g_glu = _run_exchanges([_Chip([p_in]), _Share([h_out, h_glu])], "rs_chip_w_in")
    h_in = _quad_sum(q_in, core, "quad_sum_w_in")
    (g_in,) = _run_exchanges([_Share([h_in])], "rs_share_w_in")
    shard_grads = {"w_in": g_in, "w_glu": g_glu, "w_out": g_out, "w_up": g_up, "w_down": g_down}

    out = {q: {} for q in ("grad", "delta", "new_m", "new_v")}
    for k in BIG:
        out["grad"][k] = shard_grads[k][None]
        delta, new_m, new_v = _adamw(w[k], shard_grads[k], m[k], v[k], "adamw_" + k)
        out["delta"][k], out["new_m"][k], out["new_v"][k] = delta[None], new_m[None], new_v[None]
    form = lambda d: {k: _kernel_form(k, d[k]) for k in SMALL}
    loss, res = _adamw_small(tiles, rows, form(w_dev), form(m_dev), form(v_dev))
    for q, d in zip(("grad", "delta", "new_m", "new_v"), res):
        out[q].update({k: _param_form(k, d[k]) for k in SMALL})
    flat = [loss[0, 0], grad_x[None]]
    for q in ("grad", "delta", "new_m", "new_v"):
        flat += [out[q][k] for k in WEIGHTS]
    return tuple(flat)
```

```python
import functools
import math

import jax
import jax.numpy as jnp
import numpy as np
from jax import lax
from jax.experimental import pallas as pl
from jax.experimental.pallas import tpu as pltpu

F32 = jnp.float32
MXU_DTYPE = jnp.bfloat16
WIRE_DTYPE = jnp.bfloat16

D_MODEL = 1024
D_SSM = 512
D_CONV = 512
N_GROUPS = 32
GROUP = 16
STATE = 64
D_FF = 4096
RMS_EPS = 1e-6
N_CHIPS = 4
N_DEV = 8

ADAM_LR = 0.001
ADAM_B1 = 0.9
ADAM_B2 = 0.999
ADAM_EPS = 1e-08
ADAM_WD = 0.01
ADAM_STEP = 10

N_GBLK = 2
G_PER_BLK = N_GROUPS // N_GBLK
UB = G_PER_BLK * GROUP
WB = G_PER_BLK * STATE
LANE_CHUNK = 256
SUBLANES = 8
N_TABLES = 24

TM_PROJ = 512
TM_S5 = 512
TM_TAIL = 256
TM_MLP = 256
TL_TN = 2048
VMEM_LIMIT = 56 * 1024 * 1024

MESH = pl.DeviceIdType.MESH


def _params(sem, vmem=VMEM_LIMIT):
    return pltpu.CompilerParams(dimension_semantics=sem, vmem_limit_bytes=vmem)


def _resident(shape):
    nd = len(shape)
    return pl.BlockSpec(shape, lambda *_: (0,) * nd, pipeline_mode=pl.Buffered(1))


def _dot(a, b):
    return jnp.dot(a, b, preferred_element_type=F32)


def _dot_nt(a, b):
    return lax.dot_general(a, b, (((1,), (1,)), ((), ())), preferred_element_type=F32)


def _dot_tn(a, b):
    return lax.dot_general(a, b, (((0,), (0,)), ((), ())), preferred_element_type=F32)


def _rms_fwd(x, g):
    r = lax.rsqrt(jnp.mean(x * x, axis=-1, keepdims=True) + RMS_EPS)
    return x * r * g


def _rms_bwd(x, g, dy):
    r = lax.rsqrt(jnp.mean(x * x, axis=-1, keepdims=True) + RMS_EPS)
    xn = x * r
    q = dy * g
    dx = r * (q - xn * jnp.mean(q * xn, axis=-1, keepdims=True))
    return dx, jnp.sum(dy * xn, axis=0, keepdims=True)


_GELU_C = math.sqrt(2.0 / math.pi)


def _gelu(x):
    t = jnp.tanh(_GELU_C * (x + 0.044715 * (x * x * x)))
    y = x * (0.5 * (1.0 + t))
    dy = 0.5 * (1.0 + t) + 0.5 * x * (1.0 - t * t) * (_GELU_C * (1.0 + 3 * 0.044715 * (x * x)))
    return y, dy


def _tile(n, pref):
    t = min(n, pref)
    assert n % t == 0, (n, t)
    return t


HBM = pl.BlockSpec(memory_space=pltpu.HBM)
VMEM = pl.BlockSpec(memory_space=pltpu.VMEM)
DMA_SEMS = pltpu.SemaphoreType.DMA


def _place():
    x, y, c = lax.axis_index("x"), lax.axis_index("y"), lax.axis_index("c")
    chips = [(1 - x, y), (x, 1 - y), (1 - x, 1 - y)]
    return (x, y, c), 2 * x + y, (x, y, 1 - c), chips, [2 * px + py for px, py in chips]


def _remote(src, dst, send_sem, recv_sem, device):
    return pltpu.make_async_remote_copy(src_ref=src, dst_ref=dst, send_sem=send_sem, recv_sem=recv_sem,
                                        device_id=device, device_id_type=MESH)


def _half(rows, c):
    return pl.ds(c * (rows // 2), rows // 2)


class _Exchange:
    aliases = {}

    def start(self, ins, outs, sems):
        local, outgoing, _ = self._copies(ins, outs, sems)
        for cp in local + outgoing:
            cp.start()

    def finish(self, ins, outs, sems):
        local, outgoing, incoming = self._copies(ins, outs, sems)
        for cp in incoming:
            cp.wait_recv()
        for cp in outgoing:
            cp.wait_send()
        for cp in local:
            cp.wait()


class _Gather(_Exchange):
    def __init__(self, shards, split):
        self.inputs, self.split = list(shards), split
        self.out_shape = [jax.ShapeDtypeStruct((N_CHIPS, *a.shape), a.dtype) for a in shards]
        self.sems = [DMA_SEMS((len(shards), 3)), DMA_SEMS((len(shards), 3)), DMA_SEMS((len(shards),))]

    def _copies(self, ins, outs, sems):
        send, recv, lsem = sems
        (x, y, c), me, sibling, chips, ids = _place()
        local = [pltpu.make_async_copy(ins[t], outs[t].at[me], lsem.at[t]) for t in range(len(ins))]
        outgoing, incoming = [], []
        for t, a in enumerate(self.inputs):
            rows = _half(a.shape[0], c) if self.split[t] else pl.ds(0, a.shape[0])
            for k in range(3):
                to = (*chips[k], c)
                outgoing.append(_remote(ins[t].at[rows, :], outs[t].at[me, rows, :], send.at[t, k], recv.at[t, k], to))
                incoming.append(_remote(ins[t].at[rows, :], outs[t].at[ids[k], rows, :], send.at[t, k], recv.at[t, k], to))
        return local, outgoing, incoming


class _Forward(_Exchange):
    def __init__(self, arrays):
        self.inputs = list(arrays)
        self.out_shape = [jax.ShapeDtypeStruct(a.shape, a.dtype) for a in arrays]
        self.aliases = {t: t for t in range(len(arrays))}
        self.sems = [DMA_SEMS((len(arrays), 3)), DMA_SEMS((len(arrays), 3))]

    def _copies(self, ins, outs, sems):
        send, recv = sems
        (x, y, c), me, sibling, chips, ids = _place()
        outgoing, incoming = [], []
        for t, a in enumerate(self.inputs):
            for k in range(3):
                mine = outs[t].at[ids[k], _half(a.shape[1], c), :]
                theirs = outs[t].at[ids[k], _half(a.shape[1], 1 - c), :]
                outgoing.append(_remote(mine, mine, send.at[t, k], recv.at[t, k], sibling))
                incoming.append(_remote(theirs, theirs, send.at[t, k], recv.at[t, k], sibling))
        return [], outgoing, incoming


class _GatherForward(_Exchange):
    def __init__(self, shards):
        self.gather = _Gather(shards, [True] * len(shards))
        self.forward = _Forward(self.gather.out_shape)
        self.inputs, self.out_shape = self.gather.inputs, self.gather.out_shape
        self.sems = self.gather.sems + self.forward.sems

    def start(self, ins, outs, sems):
        self.gather.start(ins, outs, sems[:3])

    def finish(self, ins, outs, sems):
        local, outgoing, incoming = self.gather._copies(ins, outs, sems[:3])
        _, passed, from_sibling = self.forward._copies(outs, outs, sems[3:])
        for landed, onward in zip(incoming, passed):
            landed.wait_recv()
            onward.start()
        for cp in from_sibling:
            cp.wait_recv()
        for cp in outgoing + passed:
            cp.wait_send()
        for cp in local:
            cp.wait()


class _Pair(_Exchange):
    def __init__(self, grads):
        self.inputs = list(grads)
        self.out_shape = [jax.ShapeDtypeStruct((g.shape[0], g.shape[1] // 2, g.shape[2]), g.dtype) for g in grads]
        self.sems = [DMA_SEMS((len(grads),)), DMA_SEMS((len(grads),))]

    def _copies(self, ins, outs, sems):
        send, recv = sems
        (x, y, c), me, sibling, chips, ids = _place()
        cps = [_remote(ins[t].at[:, _half(g.shape[1], 1 - c), :], outs[t], send.at[t], recv.at[t], sibling)
               for t, g in enumerate(self.inputs)]
        return [], cps, cps


class _Chip(_Exchange):
    def __init__(self, parts):
        self.inputs = list(parts)
        self.out_shape = [jax.ShapeDtypeStruct(p.shape, p.dtype) for p in parts]
        self.sems = [DMA_SEMS((len(parts), 3)), DMA_SEMS((len(parts), 3)), DMA_SEMS((len(parts),))]

    def _copies(self, ins, outs, sems):
        send, recv, lsem = sems
        (x, y, c), me, sibling, chips, ids = _place()
        local = [pltpu.make_async_copy(ins[t].at[me], outs[t].at[me], lsem.at[t]) for t in range(len(ins))]
        outgoing, incoming = [], []
        for t in range(len(ins)):
            for k in range(3):
                to = (*chips[k], c)
                outgoing.append(_remote(ins[t].at[ids[k]], outs[t].at[me], send.at[t, k], recv.at[t, k], to))
                incoming.append(_remote(ins[t].at[ids[k]], outs[t].at[ids[k]], send.at[t, k], recv.at[t, k], to))
        return local, outgoing, incoming


class _Share(_Exchange):
    def __init__(self, grads):
        self.inputs = list(grads)
        self.out_shape = [jax.ShapeDtypeStruct(g.shape, g.dtype) for g in grads]
        self.aliases = {t: t for t in range(len(grads))}
        self.sems = [DMA_SEMS((len(grads),)), DMA_SEMS((len(grads),))]

    def _copies(self, ins, outs, sems):
        send, recv = sems
        (x, y, c), me, sibling, chips, ids = _place()
        outgoing, incoming = [], []
        for t, g in enumerate(self.inputs):
            mine = outs[t].at[_half(g.shape[0], c), :]
            theirs = outs[t].at[_half(g.shape[0], 1 - c), :]
            outgoing.append(_remote(mine, mine, send.at[t], recv.at[t], sibling))
            incoming.append(_remote(theirs, theirs, send.at[t], recv.at[t], sibling))
        return [], outgoing, incoming


class _GatherSmall(_Exchange):
    def __init__(self, block):
        self.inputs = [block]
        self.out_shape = [jax.ShapeDtypeStruct((N_DEV, *block.shape), block.dtype)]
        self.sems = [DMA_SEMS((7,)), DMA_SEMS((7,)), DMA_SEMS(())]

    def _copies(self, ins, outs, sems):
        send, recv, lsem = sems
        (x, y, c), me, sibling, chips, ids = _place()
        slot = lambda px, py, pc: outs[0].at[4 * px + 2 * py + pc]

        def copy(k, block, to, src=None):
            return _remote(slot(*block) if src is None else src, slot(*block), send.at[k], recv.at[k], to)

        local = [pltpu.make_async_copy(ins[0], slot(x, y, c), lsem)]
        first = [copy(0, (x, y, c), sibling, src=ins[0])] + [copy(1 + j, (x, y, c), (*chip, c), src=ins[0]) for j, chip in enumerate(chips)]
        passed = [copy(4 + j, (*chip, c), sibling) for j, chip in enumerate(chips)]
        landed = [copy(1 + j, (*chip, c), (x, y, c)) for j, chip in enumerate(chips)]
        from_sibling = [copy(0, (x, y, 1 - c), (x, y, c))] + [copy(4 + j, (*chip, 1 - c), (x, y, c)) for j, chip in enumerate(chips)]
        return local, first, (passed, landed, from_sibling)

    def finish(self, ins, outs, sems):
        local, first, (passed, landed, from_sibling) = self._copies(ins, outs, sems)
        for j in range(3):
            landed[j].wait_recv()
            passed[j].start()
        for cp in from_sibling:
            cp.wait_recv()
        for cp in first + passed:
            cp.wait_send()
        for cp in local:
            cp.wait()


def _split_refs(refs, counts):
    out = []
    for n in counts:
        out.append(refs[:n])
        refs = refs[n:]
    return out


def _each_exchange(exchanges, method, x_in, x_out, x_sem):
    for ex in exchanges:
        ni, no, ns = len(ex.inputs), len(ex.out_shape), len(ex.sems)
        getattr(ex, method)(x_in[:ni], x_out[:no], x_sem[:ns])
        x_in, x_out, x_sem = x_in[ni:], x_out[no:], x_sem[ns:]


def _call(body, *, name, grid, in_specs, out_specs, out_shape, operands, semantics, scratch_shapes=(), exchanges=()):
    x_in = [a for ex in exchanges for a in ex.inputs]
    x_out = [s for ex in exchanges for s in ex.out_shape]
    x_sem = [s for ex in exchanges for s in ex.sems]
    counts = (len(in_specs), len(x_in), len(out_specs), len(x_out), len(scratch_shapes), len(x_sem))
    aliases, i0, o0 = {}, len(in_specs), len(out_specs)
    for ex in exchanges:
        aliases.update({i0 + i: o0 + o for i, o in ex.aliases.items()})
        i0, o0 = i0 + len(ex.inputs), o0 + len(ex.out_shape)

    def full_body(*refs):
        ins, xi, outs, xo, scr, xs = _split_refs(list(refs), counts)
        if exchanges:
            @pl.when(functools.reduce(jnp.logical_and, [pl.program_id(a) == 0 for a in range(len(grid))]))
            def _():
                _each_exchange(exchanges, "start", xi, xo, xs)

        body(*ins, *outs, *scr)
        if exchanges:
            @pl.when(functools.reduce(jnp.logical_and, [pl.program_id(a) == grid[a] - 1 for a in range(len(grid))]))
            def _():
                _each_exchange(exchanges, "finish", xi, xo, xs)

    return pl.pallas_call(
        full_body, name=name, grid=grid,
        in_specs=list(in_specs) + [HBM] * len(x_in), out_specs=list(out_specs) + [HBM] * len(x_out),
        out_shape=list(out_shape) + x_out, scratch_shapes=list(scratch_shapes) + x_sem,
        input_output_aliases=aliases, compiler_params=_params(semantics),
    )(*operands, *x_in)


def _run_exchanges(exchanges, name):
    x_in = [a for ex in exchanges for a in ex.inputs]
    x_out = [s for ex in exchanges for s in ex.out_shape]
    x_sem = [s for ex in exchanges for s in ex.sems]
    aliases, i0, o0 = {}, 0, 0
    for ex in exchanges:
        aliases.update({i0 + i: o0 + o for i, o in ex.aliases.items()})
        i0, o0 = i0 + len(ex.inputs), o0 + len(ex.out_shape)

    def body(*refs):
        xi, xo, xs = _split_refs(list(refs), (len(x_in), len(x_out), len(x_sem)))
        _each_exchange(exchanges, "start", xi, xo, xs)
        _each_exchange(exchanges, "finish", xi, xo, xs)

    return pl.pallas_call(
        body, name=name, in_specs=[HBM] * len(x_in), out_specs=[HBM] * len(x_out), out_shape=x_out,
        scratch_shapes=x_sem, input_output_aliases=aliases,
    )(*x_in)


def _inproj_fwd(x, g1, w_in_all, exchanges=()):
    L, D = x.shape
    ns, _, nc = w_in_all.shape
    tm = _tile(L, TM_PROJ)

    def body(x_ref, g_ref, w_ref, hn_ref, proj_ref, u_ref):
        hn = _rms_fwd(x_ref[...], g_ref[...]).astype(MXU_DTYPE)
        hn_ref[...] = hn
        for j in range(ns):
            proj_ref[:, j * nc:(j + 1) * nc] = _dot(hn, w_ref[j])
        _store_slabs(u_ref, proj_ref[:, 0:nc])

    return _call(
        body, name="inproj_fwd", grid=(L // tm,), exchanges=exchanges, semantics=("arbitrary",), operands=(x, g1, w_in_all),
        in_specs=[pl.BlockSpec((tm, D), lambda i: (i, 0)), _resident((1, D)), _resident(w_in_all.shape)],
        out_specs=[pl.BlockSpec((tm, D), lambda i: (i, 0)), pl.BlockSpec((tm, ns * nc), lambda i: (i, 0)), _slab_spec(nc, tm)],
        out_shape=[jax.ShapeDtypeStruct((L, D), MXU_DTYPE), jax.ShapeDtypeStruct((L, ns * nc), F32), _slab_shape(L, nc)],
    )


def _slab_shape(L, n):
    return jax.ShapeDtypeStruct((n // LANES, L, LANES), F32)


def _slab_spec(n, tm, index=lambda i: (0, i, 0)):
    return pl.BlockSpec((n // LANES, tm, LANES), index)


def _store_slabs(ref, value):
    for k in range(ref.shape[0]):
        ref[k] = value[:, k * LANES:(k + 1) * LANES]


def _load_slabs(ref):
    return jnp.concatenate([ref[k] for k in range(ref.shape[0])], axis=1)


SEG_ROWS = SUBLANES * SUBLANES


def _load_permuted(ref):
    tm = ref.shape[1]
    slabs = []
    for k in range(ref.shape[0]):
        tiles = [ref.at[k][pl.ds(b * SEG_ROWS + j, SUBLANES, stride=SUBLANES), :] for b in range(tm // SEG_ROWS) for j in range(SUBLANES)]
        slabs.append(jnp.concatenate(tiles, axis=0))
    return jnp.concatenate(slabs, axis=1)


def _store_permuted(ref, value):
    tm = ref.shape[1]
    for k in range(ref.shape[0]):
        for b in range(tm // SEG_ROWS):
            for j in range(SUBLANES):
                r = b * SEG_ROWS + j * SUBLANES
                ref.at[k][pl.ds(b * SEG_ROWS + j, SUBLANES, stride=SUBLANES), :] = value[r:r + SUBLANES, k * LANES:(k + 1) * LANES]


def _scan_tile(xr, xi, hr, hi, coef_ref, lanes, reverse):
    for k, j in ((1, 0), (2, 2), (4, 4)):
        ar = coef_ref[0, j, :, lanes]
        ai = coef_ref[0, j + 1, :, lanes]
        shift = SUBLANES - k if reverse else k
        sr = pltpu.roll(xr, shift, 0)
        si = pltpu.roll(xi, shift, 0)
        xr, xi = xr + (ar * sr - ai * si), xi + (ar * si + ai * sr)
    pr = coef_ref[0, 6, :, lanes]
    pi = coef_ref[0, 7, :, lanes]
    return xr + (pr * hr - pi * hi), xi + (pr * hi + pi * hr)


def _scan_block(read, write, hr, hi, coef_ref, lanes, reverse):
    order = list(range(SUBLANES - 1, -1, -1) if reverse else range(SUBLANES))
    near = 8 + 2 * order[0]
    ar = coef_ref[0, near, :, lanes]
    ai = coef_ref[0, near + 1, :, lanes]
    xr, xi = read(order[0])
    local = {order[0]: (xr, xi)}
    for j in order[1:]:
        br, bi = read(j)
        xr, xi = br + (ar * xr - ai * xi), bi + (ar * xi + ai * xr)
        local[j] = (xr, xi)
    er, ei = _scan_tile(xr, xi, hr, hi, coef_ref, lanes, reverse)
    edge = lax.broadcasted_iota(jnp.int32, er.shape, 0) == (SUBLANES - 1 if reverse else 0)
    shift = SUBLANES - 1 if reverse else 1
    pr = jnp.where(edge, hr, pltpu.roll(er, shift, 0))
    pi = jnp.where(edge, hi, pltpu.roll(ei, shift, 0))
    for j in range(SUBLANES):
        cr = coef_ref[0, 8 + 2 * j, :, lanes]
        ci = coef_ref[0, 9 + 2 * j, :, lanes]
        xr, xi = local[j]
        write(j, xr + (cr * pr - ci * pi), xi + (cr * pi + ci * pr))
    end = 0 if reverse else SUBLANES - 1
    return jnp.broadcast_to(er[end:end + 1, :], er.shape), jnp.broadcast_to(ei[end:end + 1, :], ei.shape)


def _s5_fwd(u4, bmat, cmat, coef, dskip, exchanges=()):
    L = u4.shape[1]
    tm = _tile(L, TM_S5)
    lc = min(LANE_CHUNK, WB)

    def body(u_ref, bm_ref, cm_ref, coef_ref, d_ref, sre_ref, sim_ref, ys_ref, hr_ref, hi_ref):
        @pl.when(pl.program_id(1) == 0)
        def _():
            hr_ref[...] = jnp.zeros_like(hr_ref)
            hi_ref[...] = jnp.zeros_like(hi_ref)

        u = _load_permuted(u_ref)
        bu = _dot(u.astype(MXU_DTYPE), bm_ref[0])
        sre_ref[...] = bu[:, :WB]
        sim_ref[...] = bu[:, WB:]
        for c in range(WB // lc):
            lanes = slice(c * lc, (c + 1) * lc)
            hr, hi = hr_ref[:, lanes], hi_ref[:, lanes]
            for b in range(tm // SEG_ROWS):
                rows = lambda j, b=b: slice(b * SEG_ROWS + j * SUBLANES, b * SEG_ROWS + (j + 1) * SUBLANES)

                def read(j, rows=rows, lanes=lanes):
                    return sre_ref[rows(j), lanes], sim_ref[rows(j), lanes]

                def write(j, xr, xi, rows=rows, lanes=lanes):
                    sre_ref[rows(j), lanes] = xr
                    sim_ref[rows(j), lanes] = xi

                hr, hi = _scan_block(read, write, hr, hi, coef_ref, lanes, False)
            hr_ref[:, lanes] = hr
            hi_ref[:, lanes] = hi
        ys = _dot_nt(sre_ref[...].astype(MXU_DTYPE), cm_ref[0, :, :WB]) + _dot_nt(sim_ref[...].astype(MXU_DTYPE), cm_ref[0, :, WB:])
        _store_permuted(ys_ref, ys + d_ref[0] * u)

    return _call(
        body, name="s5_fwd", grid=(N_GBLK, L // tm), exchanges=exchanges, semantics=("arbitrary", "arbitrary"),
        operands=(u4, bmat, cmat, coef, dskip),
        in_specs=[
            _slab_spec(UB, tm, lambda b, i: (b, i, 0)),
            pl.BlockSpec((1, UB, 2 * WB), lambda b, i: (b, 0, 0)),
            pl.BlockSpec((1, UB, 2 * WB), lambda b, i: (b, 0, 0)),
            pl.BlockSpec((1, N_TABLES, SUBLANES, WB), lambda b, i: (b, 0, 0, 0)),
            pl.BlockSpec((1, 1, UB), lambda b, i: (b, 0, 0)),
        ],
        out_specs=[
            pl.BlockSpec((tm, WB), lambda b, i: (i, b)),
            pl.BlockSpec((tm, WB), lambda b, i: (i, b)),
            _slab_spec(UB, tm, lambda b, i: (b, i, 0)),
        ],
        out_shape=[
            jax.ShapeDtypeStruct((L, N_GBLK * WB), F32),
            jax.ShapeDtypeStruct((L, N_GBLK * WB), F32),
            _slab_shape(L, D_SSM),
        ],
        scratch_shapes=[pltpu.VMEM((SUBLANES, WB), F32), pltpu.VMEM((SUBLANES, WB), F32)],
    )


def _tail_fwd(x, ys, proj, w_glu, conv_w, g_ssm, g_conv, w_out, g_post, exchanges=()):
    L, D = x.shape
    tm = _tile(L, TM_TAIL)

    def body(x_ref, ys_ref, h_ref, bg_ref, cg_ref, wglu_ref, cw_ref, gs_ref, gc_ref, wout_ref, gp_ref,
             ycat_ref, o_ref, x1_ref, zbuf):
        @pl.when(pl.program_id(0) == 0)
        def _():
            zbuf[0:SUBLANES, :] = jnp.zeros((SUBLANES, D_CONV), F32)

        y1, _ = _gelu(_load_slabs(ys_ref))
        y2 = y1 * jax.nn.sigmoid(_dot(y1.astype(MXU_DTYPE), wglu_ref[...]))
        ycat_ref[:, :D_SSM] = _rms_fwd(y2, gs_ref[...]).astype(MXU_DTYPE)
        z = cg_ref[...] * h_ref[...]
        zbuf[SUBLANES:, :] = z
        conv = cw_ref[0:1, :] * zbuf[SUBLANES - 2:SUBLANES - 2 + tm, :] + cw_ref[1:2, :] * zbuf[SUBLANES - 1:SUBLANES - 1 + tm, :] + cw_ref[2:3, :] * z
        zbuf[0:SUBLANES, :] = zbuf[tm:tm + SUBLANES, :]
        ycat_ref[:, D_SSM:] = _rms_fwd(bg_ref[...] * conv, gc_ref[...]).astype(MXU_DTYPE)
        o = _dot(ycat_ref[...], wout_ref[...])
        o_ref[...] = o
        x1_ref[...] = x_ref[...] + _rms_fwd(o, gp_ref[...])

    row = lambda i: (i, 0)
    return _call(
        body, name="tail_fwd", grid=(L // tm,), exchanges=exchanges, semantics=("arbitrary",),
        operands=(x, ys, proj, proj, proj, w_glu, conv_w, g_ssm, g_conv, w_out, g_post),
        in_specs=[
            pl.BlockSpec((tm, D), row), _slab_spec(D_SSM, tm),
            pl.BlockSpec((tm, D_CONV), lambda i: (i, 1)), pl.BlockSpec((tm, D_CONV), lambda i: (i, 2)),
            pl.BlockSpec((tm, D_CONV), lambda i: (i, 3)),
            _resident(w_glu.shape), _resident(conv_w.shape), _resident(g_ssm.shape), _resident(g_conv.shape),
            _resident(w_out.shape), _resident(g_post.shape),
        ],
        out_specs=[pl.BlockSpec((tm, D), row), pl.BlockSpec((tm, D), row), pl.BlockSpec((tm, D), row)],
        out_shape=[jax.ShapeDtypeStruct((L, D), MXU_DTYPE), jax.ShapeDtypeStruct((L, D), F32), jax.ShapeDtypeStruct((L, D), F32)],
        scratch_shapes=[pltpu.VMEM((tm + SUBLANES, D_CONV), F32)],
    )


def _mlp_fwd(x1, target, w_up_all, w_down, g_pre, g_post):
    L, D = x1.shape
    ns, _, fc = w_up_all.shape
    tm = _tile(L, TM_MLP)

    def body(x1_ref, t_ref, wup_ref, wdn_ref, gpre_ref, gpost_ref, hn2_ref, up_ref, m_ref, dx2_ref, loss_ref):
        @pl.when(pl.program_id(0) == 0)
        def _():
            loss_ref[...] = jnp.zeros_like(loss_ref)

        x1v = x1_ref[...]
        hn2 = _rms_fwd(x1v, gpre_ref[...]).astype(MXU_DTYPE)
        hn2_ref[...] = hn2
        m = jnp.zeros((tm, D), F32)
        for j in range(ns):
            up = _dot(hn2, wup_ref[j])
            up_ref[:, j * fc:(j + 1) * fc] = up
            act = jnp.square(jnp.maximum(up, 0.0)).astype(MXU_DTYPE)
            m = m + _dot(act, wdn_ref[j * fc:(j + 1) * fc, :])
        m_ref[...] = m
        err = x1v + _rms_fwd(m, gpost_ref[...]) - t_ref[...]
        loss_ref[...] += 0.5 * jnp.sum(jnp.mean(err * err, axis=-1, keepdims=True))
        dx2_ref[...] = err * (1.0 / D)

    row = lambda i: (i, 0)
    return pl.pallas_call(
        body, name="mlp_fwd", grid=(L // tm,),
        in_specs=[pl.BlockSpec((tm, D), row), pl.BlockSpec((tm, D), row), _resident(w_up_all.shape), _resident(w_down.shape),
                  _resident(g_pre.shape), _resident(g_post.shape)],
        out_specs=[pl.BlockSpec((tm, D), row), pl.BlockSpec((tm, ns * fc), row), pl.BlockSpec((tm, D), row),
                   pl.BlockSpec((tm, D), row), pl.BlockSpec((SUBLANES, 128), lambda i: (0, 0))],
        out_shape=[jax.ShapeDtypeStruct((L, D), MXU_DTYPE), jax.ShapeDtypeStruct((L, ns * fc), F32), jax.ShapeDtypeStruct((L, D), F32),
                   jax.ShapeDtypeStruct((L, D), F32), jax.ShapeDtypeStruct((SUBLANES, 128), F32)],
        compiler_params=_params(("arbitrary",)),
    )(x1, target, w_up_all, w_down, g_pre, g_post)


def _mlp_bwd(dx2, m, up, x1, w_up_all, w_down, g_pre, g_post):
    L, D = x1.shape
    ns, _, fc = w_up_all.shape
    tm = _tile(L, TM_MLP)

    def body(dx2_ref, m_ref, up_ref, x1_ref, wup_ref, wdn_ref, gpre_ref, gpost_ref,
             dm_ref, dup_ref, act_ref, dx1_ref, dgpost_ref, dgpre_ref):
        @pl.when(pl.program_id(0) == 0)
        def _():
            dgpost_ref[...] = jnp.zeros_like(dgpost_ref)
            dgpre_ref[...] = jnp.zeros_like(dgpre_ref)

        dx2v = dx2_ref[...]
        dm, dg = _rms_bwd(m_ref[...], gpost_ref[...], dx2v)
        dgpost_ref[...] += dg
        dm_b = dm.astype(MXU_DTYPE)
        dm_ref[...] = dm_b
        dhn2 = jnp.zeros((tm, D), F32)
        for j in range(ns):
            cols = slice(j * fc, (j + 1) * fc)
            relu = jnp.maximum(up_ref[:, cols], 0.0)
            act_ref[:, cols] = jnp.square(relu).astype(MXU_DTYPE)
            dup = (_dot_nt(dm_b, wdn_ref[cols, :]) * (2.0 * relu)).astype(MXU_DTYPE)
            dup_ref[:, cols] = dup
            dhn2 = dhn2 + _dot_nt(dup, wup_ref[j])
        dx, dg = _rms_bwd(x1_ref[...], gpre_ref[...], dhn2)
        dgpre_ref[...] += dg
        dx1_ref[...] = dx2v + dx

    row = lambda i: (i, 0)
    vec = pl.BlockSpec((1, D), lambda i: (0, 0))
    return pl.pallas_call(
        body, name="mlp_bwd", grid=(L // tm,),
        in_specs=[pl.BlockSpec((tm, D), row), pl.BlockSpec((tm, D), row), pl.BlockSpec((tm, ns * fc), row), pl.BlockSpec((tm, D), row),
                  _resident(w_up_all.shape), _resident(w_down.shape), _resident(g_pre.shape), _resident(g_post.shape)],
        out_specs=[pl.BlockSpec((tm, D), row), pl.BlockSpec((tm, ns * fc), row), pl.BlockSpec((tm, ns * fc), row),
                   pl.BlockSpec((tm, D), row), vec, vec],
        out_shape=[jax.ShapeDtypeStruct((L, D), MXU_DTYPE), jax.ShapeDtypeStruct((L, ns * fc), MXU_DTYPE),
                   jax.ShapeDtypeStruct((L, ns * fc), MXU_DTYPE), jax.ShapeDtypeStruct((L, D), F32),
                   jax.ShapeDtypeStruct((1, D), F32), jax.ShapeDtypeStruct((1, D), F32)],
        compiler_params=_params(("arbitrary",)),
    )(dx2, m, up, x1, w_up_all, w_down, g_pre, g_post)


def _tail_bwd(dx1, o, ys, proj, w_glu, conv_w, g_ssm, g_conv, w_out, g_post, exchanges=()):
    L, D = dx1.shape
    tm = _tile(L, TM_TAIL)
    nt = L // tm
    hb = tm // SUBLANES

    def body(dx1_ref, o_ref, ys_ref, h_ref, bg_ref, cg_ref, hh_ref, hcg_ref, wglu_ref, cw_ref, gs_ref, gc_ref, wout_ref, gp_ref,
             do_ref, da_ref, y1_ref, dys_ref, dhbc_ref, dgp_ref, dgs_ref, dgc_ref, dcw_ref, zbuf, dcbuf):
        step = pl.program_id(0)

        @pl.when(step == 0)
        def _():
            dcbuf[tm:, :] = jnp.zeros((SUBLANES, D_CONV), F32)
            dgp_ref[...] = jnp.zeros_like(dgp_ref)
            dgs_ref[...] = jnp.zeros_like(dgs_ref)
            dgc_ref[...] = jnp.zeros_like(dgc_ref)
            dcw_ref[...] = jnp.zeros_like(dcw_ref)

        do, dg = _rms_bwd(o_ref[...], gp_ref[...], dx1_ref[...])
        dgp_ref[...] += dg
        do_b = do.astype(MXU_DTYPE)
        do_ref[...] = do_b
        dycat = _dot_nt(do_b, wout_ref[...])
        y1, dgelu = _gelu(_load_slabs(ys_ref))
        y1_b = y1.astype(MXU_DTYPE)
        y1_ref[...] = y1_b
        s = jax.nn.sigmoid(_dot(y1_b, wglu_ref[...]))
        dy2, dg = _rms_bwd(y1 * s, gs_ref[...], dycat[:, :D_SSM])
        dgs_ref[...] += dg
        da_b = (dy2 * y1 * s * (1.0 - s)).astype(MXU_DTYPE)
        da_ref[...] = da_b
        _store_slabs(dys_ref, (dy2 * s + _dot_nt(da_b, wglu_ref[...])) * dgelu)
        h = h_ref[...]
        cg = cg_ref[...]
        bg = bg_ref[...]
        z = cg * h
        first = step == nt - 1
        zbuf[0:SUBLANES, :] = jnp.where(first, 0.0, hcg_ref[...] * hh_ref[...])
        zbuf[SUBLANES:, :] = z
        z1 = zbuf[SUBLANES - 1:SUBLANES - 1 + tm, :]
        z2 = zbuf[SUBLANES - 2:SUBLANES - 2 + tm, :]
        conv = cw_ref[0:1, :] * z2 + cw_ref[1:2, :] * z1 + cw_ref[2:3, :] * z
        dyc, dg = _rms_bwd(bg * conv, gc_ref[...], dycat[:, D_SSM:])
        dgc_ref[...] += dg
        dconv = dyc * bg
        dcw_ref[0:1, :] += jnp.sum(dconv * z2, axis=0, keepdims=True)
        dcw_ref[1:2, :] += jnp.sum(dconv * z1, axis=0, keepdims=True)
        dcw_ref[2:3, :] += jnp.sum(dconv * z, axis=0, keepdims=True)
        dcbuf[0:tm, :] = dconv
        dz = cw_ref[2:3, :] * dconv + cw_ref[1:2, :] * dcbuf[1:1 + tm, :] + cw_ref[0:1, :] * dcbuf[2:2 + tm, :]
        dcbuf[tm:, :] = dcbuf[0:SUBLANES, :]
        dhbc_ref[:, 0:D_CONV] = (dz * cg).astype(MXU_DTYPE)
        dhbc_ref[:, D_CONV:2 * D_CONV] = (dyc * conv).astype(MXU_DTYPE)
        dhbc_ref[:, 2 * D_CONV:] = (dz * h).astype(MXU_DTYPE)

    rev = lambda i: (nt - 1 - i, 0)
    slab = lambda i: (0, nt - 1 - i, 0)
    col = lambda c: (lambda i: (nt - 1 - i, c))
    halo = lambda c: (lambda i: (jnp.maximum((nt - 1 - i) * hb - 1, 0), c))
    vec = lambda n: pl.BlockSpec((1, n), lambda i: (0, 0))
    return _call(
        body, name="tail_bwd", grid=(nt,), exchanges=exchanges, semantics=("arbitrary",),
        operands=(dx1, o, ys, proj, proj, proj, proj, proj, w_glu, conv_w, g_ssm, g_conv, w_out, g_post),
        in_specs=[
            pl.BlockSpec((tm, D), rev), pl.BlockSpec((tm, D), rev), _slab_spec(D_SSM, tm, slab),
            pl.BlockSpec((tm, D_CONV), col(1)), pl.BlockSpec((tm, D_CONV), col(2)), pl.BlockSpec((tm, D_CONV), col(3)),
            pl.BlockSpec((SUBLANES, D_CONV), halo(1)), pl.BlockSpec((SUBLANES, D_CONV), halo(3)),
            _resident(w_glu.shape), _resident(conv_w.shape), _resident(g_ssm.shape), _resident(g_conv.shape),
            _resident(w_out.shape), _resident(g_post.shape),
        ],
        out_specs=[
            pl.BlockSpec((tm, D), rev), pl.BlockSpec((tm, D_SSM), rev), pl.BlockSpec((tm, D_SSM), rev), _slab_spec(D_SSM, tm, slab),
            pl.BlockSpec((tm, 3 * D_CONV), rev), vec(D), vec(D_SSM), vec(D_CONV),
            pl.BlockSpec((SUBLANES, D_CONV), lambda i: (0, 0)),
        ],
        out_shape=[
            jax.ShapeDtypeStruct((L, D), MXU_DTYPE), jax.ShapeDtypeStruct((L, D_SSM), MXU_DTYPE), jax.ShapeDtypeStruct((L, D_SSM), MXU_DTYPE),
            _slab_shape(L, D_SSM), jax.ShapeDtypeStruct((L, 3 * D_CONV), MXU_DTYPE),
            jax.ShapeDtypeStruct((1, D), F32), jax.ShapeDtypeStruct((1, D_SSM), F32), jax.ShapeDtypeStruct((1, D_CONV), F32),
            jax.ShapeDtypeStruct((SUBLANES, D_CONV), F32),
        ],
        scratch_shapes=[pltpu.VMEM((tm + SUBLANES, D_CONV), F32), pltpu.VMEM((tm + SUBLANES, D_CONV), F32)],
    )


def _s5_bwd(dys, u4, s_re, s_im, bmat, cmat, coef_rev, dskip, exchanges=()):
    L = dys.shape[1]
    tm = _tile(L, TM_S5)
    nt = L // tm
    lc = min(LANE_CHUNK, WB)

    def body(dys_ref, u_ref, sre_ref, sim_ref, bm_ref, cm_ref, coef_ref, d_ref,
             du_ref, gb_ref, gc_ref, q_ref, gd_ref, dr_ref, di_ref, lr_ref, li_ref, hr_ref, hi_ref, qr_acc, qi_acc, gb_acc, gc_acc):
        step = pl.program_id(1)

        @pl.when(step == 0)
        def _():
            for ref in (hr_ref, hi_ref, qr_acc, qi_acc, gb_acc, gc_acc, gd_ref):
                ref[...] = jnp.zeros_like(ref)

        dys_v = _load_permuted(dys_ref)
        u = _load_permuted(u_ref)
        dys_b = dys_v.astype(MXU_DTYPE)
        u_b = u.astype(MXU_DTYPE)
        d = _dot(dys_b, cm_ref[0])
        dr_ref[...] = d[:, :WB]
        di_ref[...] = d[:, WB:]
        for c in range(WB // lc):
            lanes = slice(c * lc, (c + 1) * lc)
            hr, hi = hr_ref[:, lanes], hi_ref[:, lanes]
            q = [qr_acc[:, lanes], qi_acc[:, lanes]]
            for b in range(tm // SEG_ROWS - 1, -1, -1):
                rows = lambda j, b=b: slice(b * SEG_ROWS + j * SUBLANES, b * SEG_ROWS + (j + 1) * SUBLANES)

                def read(j, rows=rows, lanes=lanes):
                    return dr_ref[rows(j), lanes], di_ref[rows(j), lanes]

                def write(j, xr, xi, rows=rows, lanes=lanes, q=q):
                    lr_ref[rows(j), lanes] = xr
                    li_ref[rows(j), lanes] = xi
                    er = xr - dr_ref[rows(j), lanes]
                    ei = xi - di_ref[rows(j), lanes]
                    sr = sre_ref[rows(j), lanes]
                    si = sim_ref[rows(j), lanes]
                    q[0] = q[0] + (er * sr + ei * si)
                    q[1] = q[1] + (ei * sr - er * si)

                hr, hi = _scan_block(read, write, hr, hi, coef_ref, lanes, True)
            hr_ref[:, lanes] = hr
            hi_ref[:, lanes] = hi
            qr_acc[:, lanes] = q[0]
            qi_acc[:, lanes] = q[1]
        lr_b = lr_ref[...].astype(MXU_DTYPE)
        li_b = li_ref[...].astype(MXU_DTYPE)
        _store_permuted(du_ref, _dot_nt(lr_b, bm_ref[0, :, :WB]) + _dot_nt(li_b, bm_ref[0, :, WB:]) + d_ref[0] * dys_v)
        gb_acc[:, :WB] += _dot_tn(u_b, lr_b)
        gb_acc[:, WB:] += _dot_tn(u_b, li_b)
        gc_acc[:, :WB] += _dot_tn(dys_b, sre_ref[...].astype(MXU_DTYPE))
        gc_acc[:, WB:] += _dot_tn(dys_b, sim_ref[...].astype(MXU_DTYPE))
        gd_ref[0] += jnp.sum(dys_v * u, axis=0, keepdims=True)

        @pl.when(step == nt - 1)
        def _():
            q_ref[0, 0:1, :] = jnp.sum(qr_acc[...], axis=0, keepdims=True)
            q_ref[0, 1:2, :] = jnp.sum(qi_acc[...], axis=0, keepdims=True)
            mask = _group_mask(UB, WB)
            fold = (lax.broadcasted_iota(jnp.int32, (WB, STATE), 0) % STATE == lax.broadcasted_iota(jnp.int32, (WB, STATE), 1)).astype(F32)
            for acc, out in ((gb_acc, gb_ref), (gc_acc, gc_ref)):
                for k in range(2):
                    own = jnp.where(mask, acc[:, k * WB:(k + 1) * WB], 0.0)
                    out[0, k] = jnp.dot(own, fold, precision=lax.Precision.HIGHEST, preferred_element_type=F32)

    rev = lambda b, i: (nt - 1 - i, b)
    slab = lambda b, i: (b, nt - 1 - i, 0)
    blk = lambda b, i: (b, 0, 0)
    blk4 = lambda b, i: (b, 0, 0, 0)
    return _call(
        body, name="s5_bwd", grid=(N_GBLK, nt), exchanges=exchanges, semantics=("arbitrary", "arbitrary"),
        operands=(dys, u4, s_re, s_im, bmat, cmat, coef_rev, dskip),
        in_specs=[
            _slab_spec(UB, tm, slab), _slab_spec(UB, tm, slab), pl.BlockSpec((tm, WB), rev), pl.BlockSpec((tm, WB), rev),
            pl.BlockSpec((1, UB, 2 * WB), blk), pl.BlockSpec((1, UB, 2 * WB), blk),
            pl.BlockSpec((1, N_TABLES, SUBLANES, WB), lambda b, i: (b, 0, 0, 0)), pl.BlockSpec((1, 1, UB), blk),
        ],
        out_specs=[
            _slab_spec(UB, tm, slab), pl.BlockSpec((1, 2, UB, STATE), blk4), pl.BlockSpec((1, 2, UB, STATE), blk4),
            pl.BlockSpec((1, 2, WB), blk), pl.BlockSpec((1, 1, UB), blk),
        ],
        out_shape=[
            _slab_shape(L, D_SSM), jax.ShapeDtypeStruct((N_GBLK, 2, UB, STATE), F32),
            jax.ShapeDtypeStruct((N_GBLK, 2, UB, STATE), F32), jax.ShapeDtypeStruct((N_GBLK, 2, WB), F32),
            jax.ShapeDtypeStruct((N_GBLK, 1, UB), F32),
        ],
        scratch_shapes=[pltpu.VMEM((tm, WB), F32)] * 4 + [pltpu.VMEM((SUBLANES, WB), F32)] * 4 + [pltpu.VMEM((UB, 2 * WB), F32)] * 2,
    )


def _inproj_bwd(du, dhbc, x, dx1, w_in_all, g1):
    L, D = x.shape
    ns, _, nc = w_in_all.shape
    tm = _tile(L, TM_PROJ)

    def body(du_ref, dhbc_ref, x_ref, dx1_ref, w_ref, g_ref, gx_ref, dproj_ref, dg_ref):
        @pl.when(pl.program_id(0) == 0)
        def _():
            dg_ref[...] = jnp.zeros_like(dg_ref)

        du_b = _load_slabs(du_ref).astype(MXU_DTYPE)
        dproj_ref[:, :nc] = du_b
        dproj_ref[:, nc:] = dhbc_ref[...]
        dhn = _dot_nt(du_b, w_ref[0])
        for j in range(1, ns):
            dhn = dhn + _dot_nt(dhbc_ref[:, (j - 1) * nc:j * nc], w_ref[j])
        dx, dg = _rms_bwd(x_ref[...], g_ref[...], dhn)
        dg_ref[...] += dg
        gx_ref[...] = dx1_ref[...] + dx

    row = lambda i: (i, 0)
    return pl.pallas_call(
        body, name="inproj_bwd", grid=(L // tm,),
        in_specs=[_slab_spec(nc, tm), pl.BlockSpec((tm, (ns - 1) * nc), row), pl.BlockSpec((tm, D), row), pl.BlockSpec((tm, D), row),
                  _resident(w_in_all.shape), _resident(g1.shape)],
        out_specs=[pl.BlockSpec((tm, D), row), pl.BlockSpec((tm, ns * nc), row), pl.BlockSpec((1, D), lambda i: (0, 0))],
        out_shape=[jax.ShapeDtypeStruct((L, D), F32), jax.ShapeDtypeStruct((L, ns * nc), MXU_DTYPE), jax.ShapeDtypeStruct((1, D), F32)],
        compiler_params=_params(("arbitrary",)),
    )(du, dhbc, x, dx1, w_in_all, g1)


def _matmul_tn(a, b, name, col_shards=1, exchanges=()):
    L, K = a.shape
    N = b.shape[1]
    tl = _tile(L, TL_TN)
    tk = _tile(K, 1024)
    nw = N // col_shards
    tn = _tile(nw, 1024)
    npb = nw // tn

    def body(a_ref, b_ref, o_ref):
        @pl.when(pl.program_id(2) == 0)
        def _():
            o_ref[...] = jnp.zeros_like(o_ref)

        o_ref[0] += _dot_tn(a_ref[...], b_ref[...])

    return _call(
        body, name=name, grid=(K // tk, N // tn, L // tl), exchanges=exchanges, semantics=("arbitrary", "arbitrary", "arbitrary"),
        operands=(a, b),
        in_specs=[pl.BlockSpec((tl, tk), lambda k, n, l: (l, k)), pl.BlockSpec((tl, tn), lambda k, n, l: (l, n))],
        out_specs=[pl.BlockSpec((1, tk, tn), lambda k, n, l: (n // npb, k, n % npb))],
        out_shape=[jax.ShapeDtypeStruct((col_shards, K, nw), F32)],
    )


def _ssm_discretize(lam_re, lam_im, log_dt, bt_re, bt_im):
    dt = jnp.exp(log_dt)[:, None]
    zr = lam_re * dt
    zi = lam_im * dt
    mag = jnp.exp(zr)
    abr = mag * jnp.cos(zi)
    abi = mag * jnp.sin(zi)
    nr, ni = abr - 1.0, abi
    den = lam_re * lam_re + lam_im * lam_im
    coef_r = ((nr * lam_re + ni * lam_im) / den)[:, None, :]
    coef_i = ((ni * lam_re - nr * lam_im) / den)[:, None, :]
    return zr, zi, coef_r * bt_re - coef_i * bt_im, coef_r * bt_im + coef_i * bt_re


def _scan_tables(ar, ai, reverse):
    rows = np.arange(SUBLANES)
    exps = np.zeros((N_TABLES // 2, SUBLANES), np.int32)
    keep = np.ones((N_TABLES // 2, SUBLANES), bool)
    for t, k in enumerate((1, 2, 4)):
        exps[t] = SUBLANES * k
        keep[t] = (rows + k <= SUBLANES - 1) if reverse else (rows >= k)
    exps[3] = SUBLANES * (SUBLANES - rows) if reverse else SUBLANES * (rows + 1)
    for j in range(SUBLANES):
        exps[4 + j] = SUBLANES - j if reverse else j + 1
    pr, pi = ar, (-ai if reverse else ai)
    shape = (N_TABLES // 2, SUBLANES, ar.shape[0])
    xr, xi = jnp.ones(shape, F32), jnp.zeros(shape, F32)
    for bit in range(int(exps.max()).bit_length()):
        on = ((exps >> bit) & 1).astype(bool)[:, :, None]
        xr, xi = jnp.where(on, xr * pr - xi * pi, xr), jnp.where(on, xr * pi + xi * pr, xi)
        pr, pi = pr * pr - pi * pi, 2.0 * pr * pi
    xr = jnp.where(keep[:, :, None], xr, 0.0)
    xi = jnp.where(keep[:, :, None], xi, 0.0)
    coef = jnp.stack([xr, xi], axis=1).reshape(N_TABLES, SUBLANES, ar.shape[0])
    return coef.reshape(N_TABLES, SUBLANES, N_GBLK, WB).transpose(2, 0, 1, 3)


def _group_mask(rows, cols):
    r = lax.broadcasted_iota(jnp.int32, (rows, cols), 0) // GROUP
    c = lax.broadcasted_iota(jnp.int32, (rows, cols), 1) // STATE
    return r == c


def _ssm_expand(bt_re, bt_im, c_re, c_im):
    flat = lambda a: a.reshape(N_GROUPS * GROUP, STATE)

    def body(br_ref, bi_ref, cr_ref, ci_ref, bm_ref, cm_ref):
        spread = (lax.broadcasted_iota(jnp.int32, (STATE, WB), 1) % STATE == lax.broadcasted_iota(jnp.int32, (STATE, WB), 0)).astype(F32)
        mask = _group_mask(UB, WB)

        def expand(x):
            wide = jnp.dot(x, spread, precision=lax.Precision.HIGHEST, preferred_element_type=F32)
            return jnp.where(mask, wide, 0.0).astype(MXU_DTYPE)

        bm_ref[0, :, :WB] = expand(br_ref[...])
        bm_ref[0, :, WB:] = expand(bi_ref[...])
        cm_ref[0, :, :WB] = expand(cr_ref[...])
        cm_ref[0, :, WB:] = expand(-ci_ref[...])

    spec = pl.BlockSpec((UB, STATE), lambda b: (b, 0))
    out = pl.BlockSpec((1, UB, 2 * WB), lambda b: (b, 0, 0))
    return pl.pallas_call(
        body, name="ssm_expand", grid=(N_GBLK,), in_specs=[spec] * 4, out_specs=[out, out],
        out_shape=[jax.ShapeDtypeStruct((N_GBLK, UB, 2 * WB), MXU_DTYPE)] * 2,
        compiler_params=_params(("arbitrary",)),
    )(flat(bt_re), flat(bt_im), flat(c_re), flat(c_im))


def _ssm_matrices(lam_re, lam_im, log_dt, bt_re, bt_im, c_re, c_im):
    zr, zi, bbar_r, bbar_i = _ssm_discretize(lam_re, lam_im, log_dt, bt_re, bt_im)
    mag = jnp.exp(zr)
    ar = (mag * jnp.cos(zi)).reshape(-1)
    ai = (mag * jnp.sin(zi)).reshape(-1)
    bmat, cmat = _ssm_expand(bbar_r, bbar_i, c_re, c_im)
    return bmat, cmat, _scan_tables(ar, ai, False), _scan_tables(ar, ai, True)


def _ssm_param_grads(lam_re, lam_im, log_dt, bt_re, bt_im, gb, gc, q, gd):
    part = lambda g, k: g[:, k].reshape(N_GROUPS, GROUP, STATE)
    qr = q[:, 0, :].reshape(N_GROUPS, STATE)
    qi = q[:, 1, :].reshape(N_GROUPS, STATE)
    _, vjp = jax.vjp(_ssm_discretize, lam_re, lam_im, log_dt, bt_re, bt_im)
    d_lam_re, d_lam_im, d_log_dt, d_bt_re, d_bt_im = vjp((qr, qi, part(gb, 0), part(gb, 1)))
    return d_lam_re, d_lam_im, d_log_dt, d_bt_re, d_bt_im, part(gc, 0), -part(gc, 1), gd.reshape(N_GROUPS, GROUP)


def _row_tile(rows, n):
    return _tile(rows, max(SUBLANES, (2 * 1024 * 1024) // (4 * n)))


def _pair_add(grad, other, core, name):
    ns, h, n = other.shape
    tr = _row_tile(h, n)
    nb = h // tr

    def body(c_ref, g_ref, o_ref, out_ref):
        out_ref[...] = (g_ref[...] + o_ref[...]).astype(WIRE_DTYPE)

    return pl.pallas_call(
        body, name=name,
        grid_spec=pltpu.PrefetchScalarGridSpec(
            num_scalar_prefetch=1, grid=(ns, nb),
            in_specs=[pl.BlockSpec((1, tr, n), lambda s, i, c: (s, c[0] * nb + i, 0)), pl.BlockSpec((1, tr, n), lambda s, i, c: (s, i, 0))],
            out_specs=pl.BlockSpec((1, tr, n), lambda s, i, c: (s, i, 0))),
        out_shape=jax.ShapeDtypeStruct(other.shape, WIRE_DTYPE),
        compiler_params=_params(("arbitrary", "arbitrary")),
    )(core, grad, other)


def _quad_sum(parts, core, name):
    ns, h, n = parts.shape
    tr = _row_tile(h, n)
    nb = h // tr

    def body(c_ref, p_ref, out_ref):
        p = [p_ref[k].astype(F32) for k in range(ns)]
        out_ref[...] = ((p[0] + p[1]) + p[2]) + p[3]

    return pl.pallas_call(
        body, name=name,
        grid_spec=pltpu.PrefetchScalarGridSpec(
            num_scalar_prefetch=1, grid=(nb,),
            in_specs=[pl.BlockSpec((ns, tr, n), lambda i, c: (0, i, 0))],
            out_specs=pl.BlockSpec((tr, n), lambda i, c: (c[0] * nb + i, 0))),
        out_shape=jax.ShapeDtypeStruct((2 * h, n), F32),
        compiler_params=_params(("arbitrary",)),
    )(core, parts)


def _adamw_math(w, g, m, v):
    m = ADAM_B1 * m + (1.0 - ADAM_B1) * g
    v = ADAM_B2 * v + (1.0 - ADAM_B2) * jnp.square(g)
    m_hat = m / (1.0 - ADAM_B1 ** ADAM_STEP)
    v_hat = v / (1.0 - ADAM_B2 ** ADAM_STEP)
    delta = -ADAM_LR * (m_hat / (jnp.sqrt(v_hat) + ADAM_EPS) + ADAM_WD * w)
    return delta, m, v


def _adamw(w, g, m, v, name):
    r, n = w.shape
    tr = _row_tile(r, n)

    def body(w_ref, g_ref, m_ref, v_ref, d_ref, nm_ref, nv_ref):
        d_ref[...], nm_ref[...], nv_ref[...] = _adamw_math(w_ref[...], g_ref[...], m_ref[...], v_ref[...])

    spec = pl.BlockSpec((tr, n), lambda i: (i, 0))
    return pl.pallas_call(
        body, name=name, grid=(r // tr,), in_specs=[spec] * 4, out_specs=[spec] * 3,
        out_shape=[jax.ShapeDtypeStruct((r, n), F32)] * 3,
        compiler_params=_params(("arbitrary",)),
    )(w, g, m, v)


LANES = 128
SMALL = ["g_pre_mix", "lam_re", "lam_im", "log_dt", "b_re", "b_im", "c_re", "c_im", "d_skip", "conv_w", "g_ssm_out", "g_conv_out",
         "g_post_mix", "g_pre_mlp", "g_post_mlp"]
TILE_SLOTS = {"b_re": (0, N_GROUPS), "b_im": (N_GROUPS, N_GROUPS), "c_re": (2 * N_GROUPS, N_GROUPS), "c_im": (3 * N_GROUPS, N_GROUPS),
              "lam_re": (4 * N_GROUPS, 2), "lam_im": (4 * N_GROUPS + 2, 2)}
N_TILE_SLOTS = 4 * N_GROUPS + 4
VEC_ROWS = {"g_pre_mix": 0, "g_post_mix": 1, "g_pre_mlp": 2, "g_post_mlp": 3, "g_ssm_out": 4, "g_conv_out": 5, "log_dt": 6}
ROW_LOSS, ROW_DSKIP, ROW_CONV, N_PACK_ROWS = 7, 8, 24, 32


def _kernel_form(name, a):
    if name in ("b_re", "b_im"):
        return jnp.transpose(a, (0, 1, 3, 2)).reshape(N_GROUPS, GROUP, STATE)
    if name in ("c_re", "c_im"):
        return a.reshape(N_GROUPS, GROUP, STATE)
    if name in ("lam_re", "lam_im"):
        return a.reshape(2, GROUP, STATE)
    if name == "d_skip":
        return jnp.transpose(a, (0, 2, 1)).reshape(GROUP, N_GROUPS)
    if name == "conv_w":
        return jnp.transpose(a, (1, 0, 2))
    return a


def _param_form(name, k):
    if name in ("b_re", "b_im"):
        return jnp.transpose(k.reshape(1, N_GROUPS, GROUP, STATE), (0, 1, 3, 2))
    if name in ("c_re", "c_im"):
        return k.reshape(1, N_GROUPS, GROUP, STATE)
    if name in ("lam_re", "lam_im"):
        return k.reshape(1, N_GROUPS, STATE)
    if name == "d_skip":
        return jnp.transpose(k.reshape(1, GROUP, N_GROUPS), (0, 2, 1))
    if name == "conv_w":
        return jnp.transpose(k, (1, 0, 2))
    return k


def _pack_tiles(g):
    lam = lambda a: a.reshape(2, GROUP, STATE)
    return jnp.concatenate([g["b_re"], g["b_im"], g["c_re"], g["c_im"], lam(g["lam_re"]), lam(g["lam_im"])], axis=0)


def _pack_rows(g, loss):
    row = lambda a: jnp.pad(a, ((0, 0), (0, D_MODEL - a.shape[1])))
    rows = [row(g[k][None]) for k in VEC_ROWS] + [row(loss[0:1]), row(g["d_skip"].T), row(g["conv_w"])]
    rows.append(jnp.zeros((N_PACK_ROWS - ROW_CONV - 3, D_MODEL), F32))
    return jnp.concatenate(rows, axis=0)


def _adamw_small(tiles, rows, w, m, v):
    nn = len(SMALL)

    def body(*refs):
        t_ref, r_ref = refs[0], refs[1]
        w_refs, m_refs, v_refs = refs[2:2 + nn], refs[2 + nn:2 + 2 * nn], refs[2 + 2 * nn:2 + 3 * nn]
        loss_ref, outs = refs[2 + 3 * nn], refs[3 + 3 * nn:]

        def tile_sum(first, count):
            total = t_ref[0, first:first + count]
            for d in range(1, N_DEV):
                total = total + t_ref[d, first:first + count]
            return total

        def row_sum(first, count, lanes):
            total = r_ref[0, first:first + count, 0:lanes]
            for d in range(1, N_DEV):
                total = total + r_ref[d, first:first + count, 0:lanes]
            return total

        def step(j, g, at=lambda ref: ref):
            delta, nm, nv = _adamw_math(at(w_refs[j])[...], g, at(m_refs[j])[...], at(v_refs[j])[...])
            at(outs[j])[...] = g
            at(outs[nn + j])[...] = delta
            at(outs[2 * nn + j])[...] = nm
            at(outs[3 * nn + j])[...] = nv

        loss_ref[...] = row_sum(ROW_LOSS, 1, LANES)
        chip = 2 * lax.axis_index("x") + lax.axis_index("y")
        for j, name in enumerate(SMALL):
            if name in TILE_SLOTS:
                step(j, tile_sum(*TILE_SLOTS[name]))
            elif name == "d_skip":
                step(j, row_sum(ROW_DSKIP, GROUP, N_GROUPS))
            elif name == "conv_w":
                full = row_sum(ROW_CONV, 3, D_CONV)
                mine = full[:, 0:LANES]
                for s in range(1, N_CHIPS):
                    mine = jnp.where(chip == s, full[:, s * LANES:(s + 1) * LANES], mine)
                for k in range(3):
                    step(j, mine[k:k + 1, :], at=lambda ref, k=k: ref.at[k])
            else:
                step(j, row_sum(VEC_ROWS[name], 1, w_refs[j].shape[1]))

    args = [tiles, rows] + [w[k] for k in SMALL] + [m[k] for k in SMALL] + [v[k] for k in SMALL]
    res = pl.pallas_call(
        body, name="adamw_small", in_specs=[VMEM] * len(args), out_specs=[VMEM] * (1 + 4 * nn),
        out_shape=[jax.ShapeDtypeStruct((1, LANES), F32)] + [jax.ShapeDtypeStruct(w[k].shape, F32) for k in SMALL] * 4,
        compiler_params=pltpu.CompilerParams(vmem_limit_bytes=VMEM_LIMIT),
    )(*args)
    return res[0], [dict(zip(SMALL, res[1 + q * nn:1 + (q + 1) * nn])) for q in range(4)]


WEIGHTS = ["g_pre_mix", "w_in", "lam_re", "lam_im", "log_dt", "b_re", "b_im", "c_re", "c_im", "d_skip", "w_glu", "conv_w",
           "g_ssm_out", "g_conv_out", "w_out", "g_post_mix", "g_pre_mlp", "w_up", "w_down", "g_post_mlp"]
BIG = ["w_in", "w_glu", "w_out", "w_up", "w_down"]


def kernel(x, g_pre_mix, w_in, lam_re, lam_im, log_dt, b_re, b_im, c_re, c_im, d_skip, w_glu, conv_w, g_ssm_out, g_conv_out, w_out, g_post_mix, g_pre_mlp, w_up, w_down, g_post_mlp, loss_target, m_g_pre_mix, m_w_in, m_lam_re, m_lam_im, m_log_dt, m_b_re, m_b_im, m_c_re, m_c_im, m_d_skip, m_w_glu, m_conv_w, m_g_ssm_out, m_g_conv_out, m_w_out, m_g_post_mix, m_g_pre_mlp, m_w_up, m_w_down, m_g_post_mlp, v_g_pre_mix, v_w_in, v_lam_re, v_lam_im, v_log_dt, v_b_re, v_b_im, v_c_re, v_c_im, v_d_skip, v_w_glu, v_conv_w, v_g_ssm_out, v_g_conv_out, v_w_out, v_g_post_mix, v_g_pre_mlp, v_w_up, v_w_down, v_g_post_mlp):
    w = dict(g_pre_mix=g_pre_mix, w_in=w_in, lam_re=lam_re, lam_im=lam_im, log_dt=log_dt, b_re=b_re, b_im=b_im, c_re=c_re, c_im=c_im,
             d_skip=d_skip, w_glu=w_glu, conv_w=conv_w, g_ssm_out=g_ssm_out, g_conv_out=g_conv_out, w_out=w_out, g_post_mix=g_post_mix,
             g_pre_mlp=g_pre_mlp, w_up=w_up, w_down=w_down, g_post_mlp=g_post_mlp)
    m = dict(g_pre_mix=m_g_pre_mix, w_in=m_w_in, lam_re=m_lam_re, lam_im=m_lam_im, log_dt=m_log_dt, b_re=m_b_re, b_im=m_b_im, c_re=m_c_re,
             c_im=m_c_im, d_skip=m_d_skip, w_glu=m_w_glu, conv_w=m_conv_w, g_ssm_out=m_g_ssm_out, g_conv_out=m_g_conv_out, w_out=m_w_out,
             g_post_mix=m_g_post_mix, g_pre_mlp=m_g_pre_mlp, w_up=m_w_up, w_down=m_w_down, g_post_mlp=m_g_post_mlp)
    v = dict(g_pre_mix=v_g_pre_mix, w_in=v_w_in, lam_re=v_lam_re, lam_im=v_lam_im, log_dt=v_log_dt, b_re=v_b_re, b_im=v_b_im, c_re=v_c_re,
             c_im=v_c_im, d_skip=v_d_skip, w_glu=v_w_glu, conv_w=v_conv_w, g_ssm_out=v_g_ssm_out, g_conv_out=v_g_conv_out, w_out=v_w_out,
             g_post_mix=v_g_post_mix, g_pre_mlp=v_g_pre_mlp, w_up=v_w_up, w_down=v_w_down, g_post_mlp=v_g_post_mlp)
    w_dev, m_dev, v_dev = w, m, v
    w, m, v = ({k: a[0] for k, a in d.items()} for d in (w, m, v))
    core = lax.axis_index("c").astype(jnp.int32).reshape(1)

    xs, target = x[0], loss_target[0]
    g1 = w["g_pre_mix"][None]
    g_ssm, g_conv = w["g_ssm_out"][None], w["g_conv_out"][None]
    g_post_mix, g_pre_mlp, g_post_mlp = w["g_post_mix"][None], w["g_pre_mlp"][None], w["g_post_mlp"][None]
    bt_re, bt_im = (jnp.transpose(w[k], (0, 2, 1)) for k in ("b_re", "b_im"))
    bmat, cmat, coef_f, coef_r = _ssm_matrices(w["lam_re"], w["lam_im"], w["log_dt"], bt_re, bt_im, w["c_re"], w["c_im"])
    dskip = w["d_skip"].reshape(N_GBLK, 1, UB)
    shard = {k: w[k].astype(MXU_DTYPE) for k in BIG}
    conv_pad = jnp.pad(w["conv_w"], ((0, SUBLANES - 3), (0, 0)))

    (w_in_all,) = _run_exchanges([_GatherForward([shard["w_in"]])], "ag_w_in")
    hn, proj, u4, w_glu_all, w_out_all, conv_all = _inproj_fwd(
        xs, g1, w_in_all, exchanges=[_Gather([shard["w_glu"], shard["w_out"], conv_pad], [False, False, False])])
    s_re, s_im, ys, w_up_all, w_down_all = _s5_fwd(
        u4, bmat, cmat, coef_f, dskip, exchanges=[_Gather([shard["w_up"], shard["w_down"]], [True, True])])
    w_glu_f, w_out_f = w_glu_all.reshape(D_SSM, D_SSM), w_out_all.reshape(D_MODEL, D_MODEL)
    conv_f = jnp.transpose(conv_all, (1, 0, 2)).reshape(SUBLANES, D_CONV)
    ycat, o, x1, w_up_all, w_down_all = _tail_fwd(xs, ys, proj, w_glu_f, conv_f, g_ssm, g_conv, w_out_f, g_post_mix,
                                                  exchanges=[_Forward([w_up_all, w_down_all])])
    w_down_f = w_down_all.reshape(D_FF, D_MODEL)
    hn2, up, m_act, dx2, loss = _mlp_fwd(x1, target, w_up_all, w_down_f, g_pre_mlp, g_post_mlp)

    dm, dup, act, dx1, dg_post_mlp, dg_pre_mlp = _mlp_bwd(dx2, m_act, up, x1, w_up_all, w_down_f, g_pre_mlp, g_post_mlp)
    gw_down = _matmul_tn(act, dm, "dw_down")[0].reshape(N_CHIPS, D_FF // N_CHIPS, D_MODEL)
    gw_up = _matmul_tn(hn2, dup, "dw_up", col_shards=N_CHIPS)[0]
    do, da, y1, dys, dhbc, dg_post_mix, dg_ssm, dg_conv, dconv_w, o_down, o_up = _tail_bwd(
        dx1, o, ys, proj, w_glu_f, conv_f, g_ssm, g_conv, w_out_f, g_post_mix, exchanges=[_Pair([gw_down, gw_up])])
    p_down = _pair_add(gw_down, o_down, core, "pair_add_w_down")
    p_up = _pair_add(gw_up, o_up, core, "pair_add_w_up")
    gw_out = _matmul_tn(ycat, do, "dw_out")[0].reshape(N_CHIPS, D_MODEL // N_CHIPS, D_MODEL)
    gw_glu = _matmul_tn(y1, da, "dw_glu")[0].reshape(N_CHIPS, D_SSM // N_CHIPS, D_SSM)
    du, gb, gc, q, gd, q_down, q_up, o_out, o_glu = _s5_bwd(
        dys, u4, s_re, s_im, bmat, cmat, coef_r, dskip, exchanges=[_Chip([p_down, p_up]), _Pair([gw_out, gw_glu])])
    h_down = _quad_sum(q_down, core, "quad_sum_w_down")
    h_up = _quad_sum(q_up, core, "quad_sum_w_up")
    p_out = _pair_add(gw_out, o_out, core, "pair_add_w_out")
    p_glu = _pair_add(gw_glu, o_glu, core, "pair_add_w_glu")
    grad_x, dproj, dg_pre_mix = _inproj_bwd(du, dhbc, xs, dx1, w_in_all, g1)
    d_lam_re, d_lam_im, d_log_dt, d_b_re, d_b_im, d_c_re, d_c_im, d_d_skip = _ssm_param_grads(
        w["lam_re"], w["lam_im"], w["log_dt"], bt_re, bt_im, gb, gc, q, gd)
    small = {
        "g_pre_mix": dg_pre_mix[0], "lam_re": d_lam_re, "lam_im": d_lam_im, "log_dt": d_log_dt, "b_re": d_b_re, "b_im": d_b_im,
        "c_re": d_c_re, "c_im": d_c_im, "d_skip": d_d_skip, "conv_w": dconv_w[:3], "g_ssm_out": dg_ssm[0], "g_conv_out": dg_conv[0],
        "g_post_mix": dg_post_mix[0], "g_pre_mlp": dg_pre_mlp[0], "g_post_mlp": dg_post_mlp[0],
    }
    gw_in, g_down, g_up, q_out, q_glu, tiles, rows = _matmul_tn(
        hn, dproj, "dw_in", col_shards=N_CHIPS,
        exchanges=[_Share([h_down, h_up]), _Chip([p_out, p_glu]), _GatherSmall(_pack_tiles(small)), _GatherSmall(_pack_rows(small, loss))])
    h_out = _quad_sum(q_out, core, "quad_sum_w_out")
    h_glu = _quad_sum(q_glu, core, "quad_sum_w_glu")
    (o_in,) = _run_exchanges([_Pair([gw_in])], "rs_pair_w_in")
    p_in = _pair_add(gw_in, o_in, core, "pair_add_w_in")
    q_in, g_out, g_glu = _run_exchanges([_Chip([p_in]), _Share([h_out, h_glu])], "rs_chip_w_in")
    h_in = _quad_sum(q_in, core, "quad_sum_w_in")
    (g_in,) = _run_exchanges([_Share([h_in])], "rs_share_w_in")
    shard_grads = {"w_in": g_in, "w_glu": g_glu, "w_out": g_out, "w_up": g_up, "w_down": g_down}

    out = {q: {} for q in ("grad", "delta", "new_m", "new_v")}
    for k in BIG:
        out["grad"][k] = shard_grads[k][None]
        delta, new_m, new_v = _adamw(w[k], shard_grads[k], m[k], v[k], "adamw_" + k)
        out["delta"][k], out["new_m"][k], out["new_v"][k] = delta[None], new_m[None], new_v[None]
    form = lambda d: {k: _kernel_form(k, d[k]) for k in SMALL}
    loss, res = _adamw_small(tiles, rows, form(w_dev), form(m_dev), form(v_dev))
    for q, d in zip(("grad", "delta", "new_m", "new_v"), res):
        out[q].update({k: _param_form(k, d[k]) for k in SMALL})
    flat = [loss[0, 0], grad_x[None]]
    for q in ("grad", "delta", "new_m", "new_v"):
        flat += [out[q][k] for k in WEIGHTS]
    return tuple(flat)
```

```python
import functools
import math

import jax
import jax.numpy as jnp
import numpy as np
from jax import lax
from jax.experimental import pallas as pl
from jax.experimental.pallas import tpu as pltpu

F32 = jnp.float32
MXU_DTYPE = jnp.bfloat16
WIRE_DTYPE = jnp.bfloat16

D_MODEL = 1024
D_SSM = 512
D_CONV = 512
N_GROUPS = 32
GROUP = 16
STATE = 64
D_FF = 4096
RMS_EPS = 1e-6
N_CHIPS = 4
N_DEV = 8

ADAM_LR = 0.001
ADAM_B1 = 0.9
ADAM_B2 = 0.999
ADAM_EPS = 1e-08
ADAM_WD = 0.01
ADAM_STEP = 10

N_GBLK = 2
G_PER_BLK = N_GROUPS // N_GBLK
UB = G_PER_BLK * GROUP
WB = G_PER_BLK * STATE
LANE_CHUNK = 256
SUBLANES = 8
N_TABLES = 24

TM_PROJ = 512
TM_S5 = 512
TM_TAIL = 256
TM_MLP = 256
TL_TN = 2048
VMEM_LIMIT = 56 * 1024 * 1024

MESH = pl.DeviceIdType.MESH


def _params(sem, vmem=VMEM_LIMIT):
    return pltpu.CompilerParams(dimension_semantics=sem, vmem_limit_bytes=vmem)


def _resident(shape):
    nd = len(shape)
    return pl.BlockSpec(shape, lambda *_: (0,) * nd, pipeline_mode=pl.Buffered(1))


def _dot(a, b):
    return jnp.dot(a, b, preferred_element_type=F32)


def _dot_nt(a, b):
    return lax.dot_general(a, b, (((1,), (1,)), ((), ())), preferred_element_type=F32)


def _dot_tn(a, b):
    return lax.dot_general(a, b, (((0,), (0,)), ((), ())), preferred_element_type=F32)


def _rms_fwd(x, g):
    r = lax.rsqrt(jnp.mean(x * x, axis=-1, keepdims=True) + RMS_EPS)
    return x * r * g


def _rms_bwd(x, g, dy):
    r = lax.rsqrt(jnp.mean(x * x, axis=-1, keepdims=True) + RMS_EPS)
    xn = x * r
    q = dy * g
    dx = r * (q - xn * jnp.mean(q * xn, axis=-1, keepdims=True))
    return dx, jnp.sum(dy * xn, axis=0, keepdims=True)


_GELU_C = math.sqrt(2.0 / math.pi)


def _gelu(x):
    t = jnp.tanh(_GELU_C * (x + 0.044715 * (x * x * x)))
    y = x * (0.5 * (1.0 + t))
    dy = 0.5 * (1.0 + t) + 0.5 * x * (1.0 - t * t) * (_GELU_C * (1.0 + 3 * 0.044715 * (x * x)))
    return y, dy


def _tile(n, pref):
    t = min(n, pref)
    assert n % t == 0, (n, t)
    return t


HBM = pl.BlockSpec(memory_space=pltpu.HBM)
VMEM = pl.BlockSpec(memory_space=pltpu.VMEM)
DMA_SEMS = pltpu.SemaphoreType.DMA


def _place():
    x, y, c = lax.axis_index("x"), lax.axis_index("y"), lax.axis_index("c")
    chips = [(1 - x, y), (x, 1 - y), (1 - x, 1 - y)]
    return (x, y, c), 2 * x + y, (x, y, 1 - c), chips, [2 * px + py for px, py in chips]


def _remote(src, dst, send_sem, recv_sem, device):
    return pltpu.make_async_remote_copy(src_ref=src, dst_ref=dst, send_sem=send_sem, recv_sem=recv_sem,
                                        device_id=device, device_id_type=MESH)


def _half(rows, c):
    return pl.ds(c * (rows // 2), rows // 2)


class _Exchange:
    aliases = {}

    def start(self, ins, outs, sems):
        local, outgoing, _ = self._copies(ins, outs, sems)
        for cp in local + outgoing:
            cp.start()

    def finish(self, ins, outs, sems):
        local, outgoing, incoming = self._copies(ins, outs, sems)
        for cp in incoming:
            cp.wait_recv()
        for cp in outgoing:
            cp.wait_send()
        for cp in local:
            cp.wait()


class _Gather(_Exchange):
    def __init__(self, shards, split):
        self.inputs, self.split = list(shards), split
        self.out_shape = [jax.ShapeDtypeStruct((N_CHIPS, *a.shape), a.dtype) for a in shards]
        self.sems = [DMA_SEMS((len(shards), 3)), DMA_SEMS((len(shards), 3)), DMA_SEMS((len(shards),))]

    def _copies(self, ins, outs, sems):
        send, recv, lsem = sems
        (x, y, c), me, sibling, chips, ids = _place()
        local = [pltpu.make_async_copy(ins[t], outs[t].at[me], lsem.at[t]) for t in range(len(ins))]
        outgoing, incoming = [], []
        for t, a in enumerate(self.inputs):
            rows = _half(a.shape[0], c) if self.split[t] else pl.ds(0, a.shape[0])
            for k in range(3):
                to = (*chips[k], c)
                outgoing.append(_remote(ins[t].at[rows, :], outs[t].at[me, rows, :], send.at[t, k], recv.at[t, k], to))
                incoming.append(_remote(ins[t].at[rows, :], outs[t].at[ids[k], rows, :], send.at[t, k], recv.at[t, k], to))
        return local, outgoing, incoming


class _Forward(_Exchange):
    def __init__(self, arrays):
        self.inputs = list(arrays)
        self.out_shape = [jax.ShapeDtypeStruct(a.shape, a.dtype) for a in arrays]
        self.aliases = {t: t for t in range(len(arrays))}
        self.sems = [DMA_SEMS((len(arrays), 3)), DMA_SEMS((len(arrays), 3))]

    def _copies(self, ins, outs, sems):
        send, recv = sems
        (x, y, c), me, sibling, chips, ids = _place()
        outgoing, incoming = [], []
        for t, a in enumerate(self.inputs):
            for k in range(3):
                mine = outs[t].at[ids[k], _half(a.shape[1], c), :]
                theirs = outs[t].at[ids[k], _half(a.shape[1], 1 - c), :]
                outgoing.append(_remote(mine, mine, send.at[t, k], recv.at[t, k], sibling))
                incoming.append(_remote(theirs, theirs, send.at[t, k], recv.at[t, k], sibling))
        return [], outgoing, incoming


class _GatherForward(_Exchange):
    def __init__(self, shards):
        self.gather = _Gather(shards, [True] * len(shards))
        self.forward = _Forward(self.gather.out_shape)
        self.inputs, self.out_shape = self.gather.inputs, self.gather.out_shape
        self.sems = self.gather.sems + self.forward.sems

    def start(self, ins, outs, sems):
        self.gather.start(ins, outs, sems[:3])

    def finish(self, ins, outs, sems):
        local, outgoing, incoming = self.gather._copies(ins, outs, sems[:3])
        _, passed, from_sibling = self.forward._copies(outs, outs, sems[3:])
        for landed, onward in zip(incoming, passed):
            landed.wait_recv()
            onward.start()
        for cp in from_sibling:
            cp.wait_recv()
        for cp in outgoing + passed:
            cp.wait_send()
        for cp in local:
            cp.wait()


class _Pair(_Exchange):
    def __init__(self, grads):
        self.inputs = list(grads)
        self.out_shape = [jax.ShapeDtypeStruct((g.shape[0], g.shape[1] // 2, g.shape[2]), g.dtype) for g in grads]
        self.sems = [DMA_SEMS((len(grads),)), DMA_SEMS((len(grads),))]

    def _copies(self, ins, outs, sems):
        send, recv = sems
        (x, y, c), me, sibling, chips, ids = _place()
        cps = [_remote(ins[t].at[:, _half(g.shape[1], 1 - c), :], outs[t], send.at[t], recv.at[t], sibling)
               for t, g in enumerate(self.inputs)]
        return [], cps, cps


class _Chip(_Exchange):
    def __init__(self, parts):
        self.inputs = list(parts)
        self.out_shape = [jax.ShapeDtypeStruct(p.shape, p.dtype) for p in parts]
        self.sems = [DMA_SEMS((len(parts), 3)), DMA_SEMS((len(parts), 3)), DMA_SEMS((len(parts),))]

    def _copies(self, ins, outs, sems):
        send, recv, lsem = sems
        (x, y, c), me, sibling, chips, ids = _place()
        local = [pltpu.make_async_copy(ins[t].at[me], outs[t].at[me], lsem.at[t]) for t in range(len(ins))]
        outgoing, incoming = [], []
        for t in range(len(ins)):
            for k in range(3):
                to = (*chips[k], c)
                outgoing.append(_remote(ins[t].at[ids[k]], outs[t].at[me], send.at[t, k], recv.at[t, k], to))
                incoming.append(_remote(ins[t].at[ids[k]], outs[t].at[ids[k]], send.at[t, k], recv.at[t, k], to))
        return local, outgoing, incoming


class _Share(_Exchange):
    def __init__(self, grads):
        self.inputs = list(grads)
        self.out_shape = [jax.ShapeDtypeStruct(g.shape, g.dtype) for g in grads]
        self.aliases = {t: t for t in range(len(grads))}
        self.sems = [DMA_SEMS((len(grads),)), DMA_SEMS((len(grads),))]

    def _copies(self, ins, outs, sems):
        send, recv = sems
        (x, y, c), me, sibling, chips, ids = _place()
        outgoing, incoming = [], []
        for t, g in enumerate(self.inputs):
            mine = outs[t].at[_half(g.shape[0], c), :]
            theirs = outs[t].at[_half(g.shape[0], 1 - c), :]
            outgoing.append(_remote(mine, mine, send.at[t], recv.at[t], sibling))
            incoming.append(_remote(theirs, theirs, send.at[t], recv.at[t], sibling))
        return [], outgoing, incoming


class _GatherSmall(_Exchange):
    def __init__(self, block):
        self.inputs = [block]
        self.out_shape = [jax.ShapeDtypeStruct((N_DEV, *block.shape), block.dtype)]
        self.sems = [DMA_SEMS((7,)), DMA_SEMS((7,)), DMA_SEMS(())]

    def _copies(self, ins, outs, sems):
        send, recv, lsem = sems
        (x, y, c), me, sibling, chips, ids = _place()
        slot = lambda px, py, pc: outs[0].at[4 * px + 2 * py + pc]

        def copy(k, block, to, src=None):
            return _remote(slot(*block) if src is None else src, slot(*block), send.at[k], recv.at[k], to)

        local = [pltpu.make_async_copy(ins[0], slot(x, y, c), lsem)]
        first = [copy(0, (x, y, c), sibling, src=ins[0])] + [copy(1 + j, (x, y, c), (*chip, c), src=ins[0]) for j, chip in enumerate(chips)]
        passed = [copy(4 + j, (*chip, c), sibling) for j, chip in enumerate(chips)]
        landed = [copy(1 + j, (*chip, c), (x, y, c)) for j, chip in enumerate(chips)]
        from_sibling = [copy(0, (x, y, 1 - c), (x, y, c))] + [copy(4 + j, (*chip, 1 - c), (x, y, c)) for j, chip in enumerate(chips)]
        return local, first, (passed, landed, from_sibling)

    def finish(self, ins, outs, sems):
        local, first, (passed, landed, from_sibling) = self._copies(ins, outs, sems)
        for j in range(3):
            landed[j].wait_recv()
            passed[j].start()
        for cp in from_sibling:
            cp.wait_recv()
        for cp in first + passed:
            cp.wait_send()
        for cp in local:
            cp.wait()


def _split_refs(refs, counts):
    out = []
    for n in counts:
        out.append(refs[:n])
        refs = refs[n:]
    return out


def _each_exchange(exchanges, method, x_in, x_out, x_sem):
    for ex in exchanges:
        ni, no, ns = len(ex.inputs), len(ex.out_shape), len(ex.sems)
        getattr(ex, method)(x_in[:ni], x_out[:no], x_sem[:ns])
        x_in, x_out, x_sem = x_in[ni:], x_out[no:], x_sem[ns:]


def _call(body, *, name, grid, in_specs, out_specs, out_shape, operands, semantics, scratch_shapes=(), exchanges=()):
    x_in = [a for ex in exchanges for a in ex.inputs]
    x_out = [s for ex in exchanges for s in ex.out_shape]
    x_sem = [s for ex in exchanges for s in ex.sems]
    counts = (len(in_specs), len(x_in), len(out_specs), len(x_out), len(scratch_shapes), len(x_sem))
    aliases, i0, o0 = {}, len(in_specs), len(out_specs)
    for ex in exchanges:
        aliases.update({i0 + i: o0 + o for i, o in ex.aliases.items()})
        i0, o0 = i0 + len(ex.inputs), o0 + len(ex.out_shape)

    def full_body(*refs):
        ins, xi, outs, xo, scr, xs = _split_refs(list(refs), counts)
        if exchanges:
            @pl.when(functools.reduce(jnp.logical_and, [pl.program_id(a) == 0 for a in range(len(grid))]))
            def _():
                _each_exchange(exchanges, "start", xi, xo, xs)

        body(*ins, *outs, *scr)
        if exchanges:
            @pl.when(functools.reduce(jnp.logical_and, [pl.program_id(a) == grid[a] - 1 for a in range(len(grid))]))
            def _():
                _each_exchange(exchanges, "finish", xi, xo, xs)

    return pl.pallas_call(
        full_body, name=name, grid=grid,
        in_specs=list(in_specs) + [HBM] * len(x_in), out_specs=list(out_specs) + [HBM] * len(x_out),
        out_shape=list(out_shape) + x_out, scratch_shapes=list(scratch_shapes) + x_sem,
        input_output_aliases=aliases, compiler_params=_params(semantics),
    )(*operands, *x_in)


def _run_exchanges(exchanges, name):
    x_in = [a for ex in exchanges for a in ex.inputs]
    x_out = [s for ex in exchanges for s in ex.out_shape]
    x_sem = [s for ex in exchanges for s in ex.sems]
    aliases, i0, o0 = {}, 0, 0
    for ex in exchanges:
        aliases.update({i0 + i: o0 + o for i, o in ex.aliases.items()})
        i0, o0 = i0 + len(ex.inputs), o0 + len(ex.out_shape)

    def body(*refs):
        xi, xo, xs = _split_refs(list(refs), (len(x_in), len(x_out), len(x_sem)))
        _each_exchange(exchanges, "start", xi, xo, xs)
        _each_exchange(exchanges, "finish", xi, xo, xs)

    return pl.pallas_call(
        body, name=name, in_specs=[HBM] * len(x_in), out_specs=[HBM] * len(x_out), out_shape=x_out,
        scratch_shapes=x_sem, input_output_aliases=aliases,
    )(*x_in)


def _inproj_fwd(x, g1, w_in_all, exchanges=()):
    L, D = x.shape
    ns, _, nc = w_in_all.shape
    tm = _tile(L, TM_PROJ)

    def body(x_ref, g_ref, w_ref, hn_ref, proj_ref, u_ref):
        hn = _rms_fwd(x_ref[...], g_ref[...]).astype(MXU_DTYPE)
        hn_ref[...] = hn
        for j in range(ns):
            proj_ref[:, j * nc:(j + 1) * nc] = _dot(hn, w_ref[j])
        _store_slabs(u_ref, proj_ref[:, 0:nc])

    return _call(
        body, name="inproj_fwd", grid=(L // tm,), exchanges=exchanges, semantics=("arbitrary",), operands=(x, g1, w_in_all),
        in_specs=[pl.BlockSpec((tm, D), lambda i: (i, 0)), _resident((1, D)), _resident(w_in_all.shape)],
        out_specs=[pl.BlockSpec((tm, D), lambda i: (i, 0)), pl.BlockSpec((tm, ns * nc), lambda i: (i, 0)), _slab_spec(nc, tm)],
        out_shape=[jax.ShapeDtypeStruct((L, D), MXU_DTYPE), jax.ShapeDtypeStruct((L, ns * nc), F32), _slab_shape(L, nc)],
    )


def _slab_shape(L, n):
    return jax.ShapeDtypeStruct((n // LANES, L, LANES), F32)


def _slab_spec(n, tm, index=lambda i: (0, i, 0)):
    return pl.BlockSpec((n // LANES, tm, LANES), index)


def _store_slabs(ref, value):
    for k in range(ref.shape[0]):
        ref[k] = value[:, k * LANES:(k + 1) * LANES]


def _load_slabs(ref):
    return jnp.concatenate([ref[k] for k in range(ref.shape[0])], axis=1)


SEG_ROWS = SUBLANES * SUBLANES


def _load_permuted(ref):
    tm = ref.shape[1]
    slabs = []
    for k in range(ref.shape[0]):
        tiles = [ref.at[k][pl.ds(b * SEG_ROWS + j, SUBLANES, stride=SUBLANES), :] for b in range(tm // SEG_ROWS) for j in range(SUBLANES)]
        slabs.append(jnp.concatenate(tiles, axis=0))
    return jnp.concatenate(slabs, axis=1)


def _store_permuted(ref, value):
    tm = ref.shape[1]
    for k in range(ref.shape[0]):
        for b in range(tm // SEG_ROWS):
            for j in range(SUBLANES):
                r = b * SEG_ROWS + j * SUBLANES
                ref.at[k][pl.ds(b * SEG_ROWS + j, SUBLANES, stride=SUBLANES), :] = value[r:r + SUBLANES, k * LANES:(k + 1) * LANES]


def _scan_tile(xr, xi, hr, hi, coef_ref, lanes, reverse):
    for k, j in ((1, 0), (2, 2), (4, 4)):
        ar = coef_ref[j, :, lanes]
        ai = coef_ref[j + 1, :, lanes]
        shift = SUBLANES - k if reverse else k
        sr = pltpu.roll(xr, shift, 0)
        si = pltpu.roll(xi, shift, 0)
        xr, xi = xr + (ar * sr - ai * si), xi + (ar * si + ai * sr)
    pr = coef_ref[6, :, lanes]
    pi = coef_ref[7, :, lanes]
    return xr + (pr * hr - pi * hi), xi + (pr * hi + pi * hr)


def _scan_block(read, write, hr, hi, coef_ref, lanes, reverse):
    order = list(range(SUBLANES - 1, -1, -1) if reverse else range(SUBLANES))
    near = 8 + 2 * order[0]
    ar = coef_ref[near, :, lanes]
    ai = coef_ref[near + 1, :, lanes]
    xr, xi = read(order[0])
    local = {order[0]: (xr, xi)}
    for j in order[1:]:
        br, bi = read(j)
        xr, xi = br + (ar * xr - ai * xi), bi + (ar * xi + ai * xr)
        local[j] = (xr, xi)
    er, ei = _scan_tile(xr, xi, hr, hi, coef_ref, lanes, reverse)
    edge = lax.broadcasted_iota(jnp.int32, er.shape, 0) == (SUBLANES - 1 if reverse else 0)
    shift = SUBLANES - 1 if reverse else 1
    pr = jnp.where(edge, hr, pltpu.roll(er, shift, 0))
    pi = jnp.where(edge, hi, pltpu.roll(ei, shift, 0))
    for j in range(SUBLANES):
        cr = coef_ref[8 + 2 * j, :, lanes]
        ci = coef_ref[9 + 2 * j, :, lanes]
        xr, xi = local[j]
        write(j, xr + (cr * pr - ci * pi), xi + (cr * pi + ci * pr))
    end = 0 if reverse else SUBLANES - 1
    return jnp.broadcast_to(er[end:end + 1, :], er.shape), jnp.broadcast_to(ei[end:end + 1, :], ei.shape)


def _s5_fwd(u4, bmat, cmat, coef, dskip, exchanges=()):
    L = u4.shape[1]
    tm = _tile(L, TM_S5)
    lc = min(LANE_CHUNK, WB)

    def body(u_ref, bm_ref, cm_ref, coef_ref, d_ref, sre_ref, sim_ref, ys_ref, hr_ref, hi_ref):
        @pl.when(pl.program_id(1) == 0)
        def _():
            hr_ref[...] = jnp.zeros_like(hr_ref)
            hi_ref[...] = jnp.zeros_like(hi_ref)

        u = _load_permuted(u_ref)
        u_b = u.astype(MXU_DTYPE)
        ys = d_ref[0] * u
        for c in range(WB // lc):
            lanes = slice(c * lc, (c + 1) * lc)
            im_lanes = slice(WB + c * lc, WB + (c + 1) * lc)
            sre_ref[:, lanes] = _dot(u_b, bm_ref[0, :, lanes])
            sim_ref[:, lanes] = _dot(u_b, bm_ref[0, :, im_lanes])
            hr, hi = hr_ref[:, lanes], hi_ref[:, lanes]
            for b in range(tm // SEG_ROWS):
                rows = lambda j, b=b: slice(b * SEG_ROWS + j * SUBLANES, b * SEG_ROWS + (j + 1) * SUBLANES)

                def read(j, rows=rows, lanes=lanes):
                    return sre_ref[rows(j), lanes], sim_ref[rows(j), lanes]

                def write(j, xr, xi, rows=rows, lanes=lanes):
                    sre_ref[rows(j), lanes] = xr
                    sim_ref[rows(j), lanes] = xi

                hr, hi = _scan_block(read, write, hr, hi, coef_ref, lanes, False)
            hr_ref[:, lanes] = hr
            hi_ref[:, lanes] = hi
            ys = ys + _dot_nt(sre_ref[:, lanes].astype(MXU_DTYPE), cm_ref[0, :, lanes])
            ys = ys + _dot_nt(sim_ref[:, lanes].astype(MXU_DTYPE), cm_ref[0, :, im_lanes])
        _store_permuted(ys_ref, ys)

    return _call(
        body, name="s5_fwd", grid=(N_GBLK, L // tm), exchanges=exchanges, semantics=("arbitrary", "arbitrary"),
        operands=(u4, bmat, cmat, coef, dskip),
        in_specs=[
            _slab_spec(UB, tm, lambda b, i: (b, i, 0)),
            pl.BlockSpec((1, UB, 2 * WB), lambda b, i: (b, 0, 0)),
            pl.BlockSpec((1, UB, 2 * WB), lambda b, i: (b, 0, 0)),
            pl.BlockSpec((N_TABLES, SUBLANES, WB), lambda b, i: (0, 0, b)),
            pl.BlockSpec((1, 1, UB), lambda b, i: (b, 0, 0)),
        ],
        out_specs=[
            pl.BlockSpec((tm, WB), lambda b, i: (i, b)),
            pl.BlockSpec((tm, WB), lambda b, i: (i, b)),
            _slab_spec(UB, tm, lambda b, i: (b, i, 0)),
        ],
        out_shape=[
            jax.ShapeDtypeStruct((L, N_GBLK * WB), F32),
            jax.ShapeDtypeStruct((L, N_GBLK * WB), F32),
            _slab_shape(L, D_SSM),
        ],
        scratch_shapes=[pltpu.VMEM((SUBLANES, WB), F32), pltpu.VMEM((SUBLANES, WB), F32)],
    )


def _tail_fwd(x, ys, proj, w_glu, conv_w, g_ssm, g_conv, w_out, g_post, exchanges=()):
    L, D = x.shape
    tm = _tile(L, TM_TAIL)

    def body(x_ref, ys_ref, h_ref, bg_ref, cg_ref, wglu_ref, cw_ref, gs_ref, gc_ref, wout_ref, gp_ref,
             ycat_ref, o_ref, x1_ref, zbuf):
        @pl.when(pl.program_id(0) == 0)
        def _():
            zbuf[0:SUBLANES, :] = jnp.zeros((SUBLANES, D_CONV), F32)

        y1, _ = _gelu(_load_slabs(ys_ref))
        y2 = y1 * jax.nn.sigmoid(_dot(y1.astype(MXU_DTYPE), wglu_ref[...]))
        ycat_ref[:, :D_SSM] = _rms_fwd(y2, gs_ref[...]).astype(MXU_DTYPE)
        z = cg_ref[...] * h_ref[...]
        zbuf[SUBLANES:, :] = z
        conv = cw_ref[0:1, :] * zbuf[SUBLANES - 2:SUBLANES - 2 + tm, :] + cw_ref[1:2, :] * zbuf[SUBLANES - 1:SUBLANES - 1 + tm, :] + cw_ref[2:3, :] * z
        zbuf[0:SUBLANES, :] = zbuf[tm:tm + SUBLANES, :]
        ycat_ref[:, D_SSM:] = _rms_fwd(bg_ref[...] * conv, gc_ref[...]).astype(MXU_DTYPE)
        o = _dot(ycat_ref[...], wout_ref[...])
        o_ref[...] = o
        x1_ref[...] = x_ref[...] + _rms_fwd(o, gp_ref[...])

    row = lambda i: (i, 0)
    return _call(
        body, name="tail_fwd", grid=(L // tm,), exchanges=exchanges, semantics=("arbitrary",),
        operands=(x, ys, proj, proj, proj, w_glu, conv_w, g_ssm, g_conv, w_out, g_post),
        in_specs=[
            pl.BlockSpec((tm, D), row), _slab_spec(D_SSM, tm),
            pl.BlockSpec((tm, D_CONV), lambda i: (i, 1)), pl.BlockSpec((tm, D_CONV), lambda i: (i, 2)),
            pl.BlockSpec((tm, D_CONV), lambda i: (i, 3)),
            _resident(w_glu.shape), _resident(conv_w.shape), _resident(g_ssm.shape), _resident(g_conv.shape),
            _resident(w_out.shape), _resident(g_post.shape),
        ],
        out_specs=[pl.BlockSpec((tm, D), row), pl.BlockSpec((tm, D), row), pl.BlockSpec((tm, D), row)],
        out_shape=[jax.ShapeDtypeStruct((L, D), MXU_DTYPE), jax.ShapeDtypeStruct((L, D), F32), jax.ShapeDtypeStruct((L, D), F32)],
        scratch_shapes=[pltpu.VMEM((tm + SUBLANES, D_CONV), F32)],
    )


def _mlp_fwd(x1, target, w_up_all, w_down, g_pre, g_post):
    L, D = x1.shape
    ns, _, fc = w_up_all.shape
    tm = _tile(L, TM_MLP)

    def body(x1_ref, t_ref, wup_ref, wdn_ref, gpre_ref, gpost_ref, hn2_ref, up_ref, m_ref, dx2_ref, loss_ref):
        @pl.when(pl.program_id(0) == 0)
        def _():
            loss_ref[...] = jnp.zeros_like(loss_ref)

        x1v = x1_ref[...]
        hn2 = _rms_fwd(x1v, gpre_ref[...]).astype(MXU_DTYPE)
        hn2_ref[...] = hn2
        m = jnp.zeros((tm, D), F32)
        for j in range(ns):
            up = _dot(hn2, wup_ref[j])
            up_ref[:, j * fc:(j + 1) * fc] = up
            act = jnp.square(jnp.maximum(up, 0.0)).astype(MXU_DTYPE)
            m = m + _dot(act, wdn_ref[j * fc:(j + 1) * fc, :])
        m_ref[...] = m
        err = x1v + _rms_fwd(m, gpost_ref[...]) - t_ref[...]
        loss_ref[...] += 0.5 * jnp.sum(jnp.mean(err * err, axis=-1, keepdims=True))
        dx2_ref[...] = err * (1.0 / D)

    row = lambda i: (i, 0)
    return pl.pallas_call(
        body, name="mlp_fwd", grid=(L // tm,),
        in_specs=[pl.BlockSpec((tm, D), row), pl.BlockSpec((tm, D), row), _resident(w_up_all.shape), _resident(w_down.shape),
                  _resident(g_pre.shape), _resident(g_post.shape)],
        out_specs=[pl.BlockSpec((tm, D), row), pl.BlockSpec((tm, ns * fc), row), pl.BlockSpec((tm, D), row),
                   pl.BlockSpec((tm, D), row), pl.BlockSpec((SUBLANES, 128), lambda i: (0, 0))],
        out_shape=[jax.ShapeDtypeStruct((L, D), MXU_DTYPE), jax.ShapeDtypeStruct((L, ns * fc), F32), jax.ShapeDtypeStruct((L, D), F32),
                   jax.ShapeDtypeStruct((L, D), F32), jax.ShapeDtypeStruct((SUBLANES, 128), F32)],
        compiler_params=_params(("arbitrary",)),
    )(x1, target, w_up_all, w_down, g_pre, g_post)


def _mlp_bwd(dx2, m, up, x1, w_up_all, w_down, g_pre, g_post):
    L, D = x1.shape
    ns, _, fc = w_up_all.shape
    tm = _tile(L, TM_MLP)

    def body(dx2_ref, m_ref, up_ref, x1_ref, wup_ref, wdn_ref, gpre_ref, gpost_ref,
             dm_ref, dup_ref, act_ref, dx1_ref, dgpost_ref, dgpre_ref):
        @pl.when(pl.program_id(0) == 0)
        def _():
            dgpost_ref[...] = jnp.zeros_like(dgpost_ref)
            dgpre_ref[...] = jnp.zeros_like(dgpre_ref)

        dx2v = dx2_ref[...]
        dm, dg = _rms_bwd(m_ref[...], gpost_ref[...], dx2v)
        dgpost_ref[...] += dg
        dm_b = dm.astype(MXU_DTYPE)
        dm_ref[...] = dm_b
        dhn2 = jnp.zeros((tm, D), F32)
        for j in range(ns):
            cols = slice(j * fc, (j + 1) * fc)
            relu = jnp.maximum(up_ref[:, cols], 0.0)
            act_ref[:, cols] = jnp.square(relu).astype(MXU_DTYPE)
            dup = (_dot_nt(dm_b, wdn_ref[cols, :]) * (2.0 * relu)).astype(MXU_DTYPE)
            dup_ref[:, cols] = dup
            dhn2 = dhn2 + _dot_nt(dup, wup_ref[j])
        dx, dg = _rms_bwd(x1_ref[...], gpre_ref[...], dhn2)
        dgpre_ref[...] += dg
        dx1_ref[...] = dx2v + dx

    row = lambda i: (i, 0)
    vec = pl.BlockSpec((1, D), lambda i: (0, 0))
    return pl.pallas_call(
        body, name="mlp_bwd", grid=(L // tm,),
        in_specs=[pl.BlockSpec((tm, D), row), pl.BlockSpec((tm, D), row), pl.BlockSpec((tm, ns * fc), row), pl.BlockSpec((tm, D), row),
                  _resident(w_up_all.shape), _resident(w_down.shape), _resident(g_pre.shape), _resident(g_post.shape)],
        out_specs=[pl.BlockSpec((tm, D), row), pl.BlockSpec((tm, ns * fc), row), pl.BlockSpec((tm, ns * fc), row),
                   pl.BlockSpec((tm, D), row), vec, vec],
        out_shape=[jax.ShapeDtypeStruct((L, D), MXU_DTYPE), jax.ShapeDtypeStruct((L, ns * fc), MXU_DTYPE),
                   jax.ShapeDtypeStruct((L, ns * fc), MXU_DTYPE), jax.ShapeDtypeStruct((L, D), F32),
                   jax.ShapeDtypeStruct((1, D), F32), jax.ShapeDtypeStruct((1, D), F32)],
        compiler_params=_params(("arbitrary",)),
    )(dx2, m, up, x1, w_up_all, w_down, g_pre, g_post)


def _tail_bwd(dx1, o, ys, proj, w_glu, conv_w, g_ssm, g_conv, w_out, g_post, exchanges=()):
    L, D = dx1.shape
    tm = _tile(L, TM_TAIL)
    nt = L // tm
    hb = tm // SUBLANES

    def body(dx1_ref, o_ref, ys_ref, h_ref, bg_ref, cg_ref, hh_ref, hcg_ref, wglu_ref, cw_ref, gs_ref, gc_ref, wout_ref, gp_ref,
             do_ref, da_ref, y1_ref, dys_ref, dhbc_ref, dgp_ref, dgs_ref, dgc_ref, dcw_ref, zbuf, dcbuf):
        step = pl.program_id(0)

        @pl.when(step == 0)
        def _():
            dcbuf[tm:, :] = jnp.zeros((SUBLANES, D_CONV), F32)
            dgp_ref[...] = jnp.zeros_like(dgp_ref)
            dgs_ref[...] = jnp.zeros_like(dgs_ref)
            dgc_ref[...] = jnp.zeros_like(dgc_ref)
            dcw_ref[...] = jnp.zeros_like(dcw_ref)

        do, dg = _rms_bwd(o_ref[...], gp_ref[...], dx1_ref[...])
        dgp_ref[...] += dg
        do_b = do.astype(MXU_DTYPE)
        do_ref[...] = do_b
        dycat = _dot_nt(do_b, wout_ref[...])
        y1, dgelu = _gelu(_load_slabs(ys_ref))
        y1_b = y1.astype(MXU_DTYPE)
        y1_ref[...] = y1_b
        s = jax.nn.sigmoid(_dot(y1_b, wglu_ref[...]))
        dy2, dg = _rms_bwd(y1 * s, gs_ref[...], dycat[:, :D_SSM])
        dgs_ref[...] += dg
        da_b = (dy2 * y1 * s * (1.0 - s)).astype(MXU_DTYPE)
        da_ref[...] = da_b
        _store_slabs(dys_ref, (dy2 * s + _dot_nt(da_b, wglu_ref[...])) * dgelu)
        h = h_ref[...]
        cg = cg_ref[...]
        bg = bg_ref[...]
        z = cg * h
        first = step == nt - 1
        zbuf[0:SUBLANES, :] = jnp.where(first, 0.0, hcg_ref[...] * hh_ref[...])
        zbuf[SUBLANES:, :] = z
        z1 = zbuf[SUBLANES - 1:SUBLANES - 1 + tm, :]
        z2 = zbuf[SUBLANES - 2:SUBLANES - 2 + tm, :]
        conv = cw_ref[0:1, :] * z2 + cw_ref[1:2, :] * z1 + cw_ref[2:3, :] * z
        dyc, dg = _rms_bwd(bg * conv, gc_ref[...], dycat[:, D_SSM:])
        dgc_ref[...] += dg
        dconv = dyc * bg
        dcw_ref[0:1, :] += jnp.sum(dconv * z2, axis=0, keepdims=True)
        dcw_ref[1:2, :] += jnp.sum(dconv * z1, axis=0, keepdims=True)
        dcw_ref[2:3, :] += jnp.sum(dconv * z, axis=0, keepdims=True)
        dcbuf[0:tm, :] = dconv
        dz = cw_ref[2:3, :] * dconv + cw_ref[1:2, :] * dcbuf[1:1 + tm, :] + cw_ref[0:1, :] * dcbuf[2:2 + tm, :]
        dcbuf[tm:, :] = dcbuf[0:SUBLANES, :]
        dhbc_ref[:, 0:D_CONV] = (dz * cg).astype(MXU_DTYPE)
        dhbc_ref[:, D_CONV:2 * D_CONV] = (dyc * conv).astype(MXU_DTYPE)
        dhbc_ref[:, 2 * D_CONV:] = (dz * h).astype(MXU_DTYPE)

    rev = lambda i: (nt - 1 - i, 0)
    slab = lambda i: (0, nt - 1 - i, 0)
    col = lambda c: (lambda i: (nt - 1 - i, c))
    halo = lambda c: (lambda i: (jnp.maximum((nt - 1 - i) * hb - 1, 0), c))
    vec = lambda n: pl.BlockSpec((1, n), lambda i: (0, 0))
    return _call(
        body, name="tail_bwd", grid=(nt,), exchanges=exchanges, semantics=("arbitrary",),
        operands=(dx1, o, ys, proj, proj, proj, proj, proj, w_glu, conv_w, g_ssm, g_conv, w_out, g_post),
        in_specs=[
            pl.BlockSpec((tm, D), rev), pl.BlockSpec((tm, D), rev), _slab_spec(D_SSM, tm, slab),
            pl.BlockSpec((tm, D_CONV), col(1)), pl.BlockSpec((tm, D_CONV), col(2)), pl.BlockSpec((tm, D_CONV), col(3)),
            pl.BlockSpec((SUBLANES, D_CONV), halo(1)), pl.BlockSpec((SUBLANES, D_CONV), halo(3)),
            _resident(w_glu.shape), _resident(conv_w.shape), _resident(g_ssm.shape), _resident(g_conv.shape),
            _resident(w_out.shape), _resident(g_post.shape),
        ],
        out_specs=[
            pl.BlockSpec((tm, D), rev), pl.BlockSpec((tm, D_SSM), rev), pl.BlockSpec((tm, D_SSM), rev), _slab_spec(D_SSM, tm, slab),
            pl.BlockSpec((tm, 3 * D_CONV), rev), vec(D), vec(D_SSM), vec(D_CONV),
            pl.BlockSpec((SUBLANES, D_CONV), lambda i: (0, 0)),
        ],
        out_shape=[
            jax.ShapeDtypeStruct((L, D), MXU_DTYPE), jax.ShapeDtypeStruct((L, D_SSM), MXU_DTYPE), jax.ShapeDtypeStruct((L, D_SSM), MXU_DTYPE),
            _slab_shape(L, D_SSM), jax.ShapeDtypeStruct((L, 3 * D_CONV), MXU_DTYPE),
            jax.ShapeDtypeStruct((1, D), F32), jax.ShapeDtypeStruct((1, D_SSM), F32), jax.ShapeDtypeStruct((1, D_CONV), F32),
            jax.ShapeDtypeStruct((SUBLANES, D_CONV), F32),
        ],
        scratch_shapes=[pltpu.VMEM((tm + SUBLANES, D_CONV), F32), pltpu.VMEM((tm + SUBLANES, D_CONV), F32)],
    )


def _s5_bwd(dys, u4, s_re, s_im, bmat, cmat, coef_rev, dskip, exchanges=()):
    L = dys.shape[1]
    tm = _tile(L, TM_S5)
    nt = L // tm
    lc = min(LANE_CHUNK, WB)

    def body(dys_ref, u_ref, sre_ref, sim_ref, bm_ref, cm_ref, coef_ref, d_ref,
             du_ref, gb_ref, gc_ref, q_ref, gd_ref, dr_ref, di_ref, lr_ref, li_ref, hr_ref, hi_ref, qr_acc, qi_acc, gb_acc, gc_acc):
        step = pl.program_id(1)

        @pl.when(step == 0)
        def _():
            for ref in (hr_ref, hi_ref, qr_acc, qi_acc, gb_acc, gc_acc, gd_ref):
                ref[...] = jnp.zeros_like(ref)

        dys_v = _load_permuted(dys_ref)
        u = _load_permuted(u_ref)
        dys_b = dys_v.astype(MXU_DTYPE)
        u_b = u.astype(MXU_DTYPE)
        du = d_ref[0] * dys_v
        for c in range(WB // lc):
            lanes = slice(c * lc, (c + 1) * lc)
            im_lanes = slice(WB + c * lc, WB + (c + 1) * lc)
            dr_ref[:, lanes] = _dot(dys_b, cm_ref[0, :, lanes])
            di_ref[:, lanes] = _dot(dys_b, cm_ref[0, :, im_lanes])
            hr, hi = hr_ref[:, lanes], hi_ref[:, lanes]
            q = [qr_acc[:, lanes], qi_acc[:, lanes]]
            for b in range(tm // SEG_ROWS - 1, -1, -1):
                rows = lambda j, b=b: slice(b * SEG_ROWS + j * SUBLANES, b * SEG_ROWS + (j + 1) * SUBLANES)

                def read(j, rows=rows, lanes=lanes):
                    return dr_ref[rows(j), lanes], di_ref[rows(j), lanes]

                def write(j, xr, xi, rows=rows, lanes=lanes, q=q):
                    lr_ref[rows(j), lanes] = xr
                    li_ref[rows(j), lanes] = xi
                    er = xr - dr_ref[rows(j), lanes]
                    ei = xi - di_ref[rows(j), lanes]
                    sr = sre_ref[rows(j), lanes]
                    si = sim_ref[rows(j), lanes]
                    q[0] = q[0] + (er * sr + ei * si)
                    q[1] = q[1] + (ei * sr - er * si)

                hr, hi = _scan_block(read, write, hr, hi, coef_ref, lanes, True)
            hr_ref[:, lanes] = hr
            hi_ref[:, lanes] = hi
            qr_acc[:, lanes] = q[0]
            qi_acc[:, lanes] = q[1]
            lr_b = lr_ref[:, lanes].astype(MXU_DTYPE)
            li_b = li_ref[:, lanes].astype(MXU_DTYPE)
            du = du + _dot_nt(lr_b, bm_ref[0, :, lanes]) + _dot_nt(li_b, bm_ref[0, :, im_lanes])
            gb_acc[:, lanes] += _dot_tn(u_b, lr_b)
            gb_acc[:, im_lanes] += _dot_tn(u_b, li_b)
            gc_acc[:, lanes] += _dot_tn(dys_b, sre_ref[:, lanes].astype(MXU_DTYPE))
            gc_acc[:, im_lanes] += _dot_tn(dys_b, sim_ref[:, lanes].astype(MXU_DTYPE))
        _store_permuted(du_ref, du)
        gd_ref[0] += jnp.sum(dys_v * u, axis=0, keepdims=True)

        @pl.when(step == nt - 1)
        def _():
            q_ref[0, 0:1, :] = jnp.sum(qr_acc[...], axis=0, keepdims=True)
            q_ref[0, 1:2, :] = jnp.sum(qi_acc[...], axis=0, keepdims=True)
            mask = _group_mask(UB, WB)
            fold = (lax.broadcasted_iota(jnp.int32, (WB, STATE), 0) % STATE == lax.broadcasted_iota(jnp.int32, (WB, STATE), 1)).astype(F32)
            for acc, out in ((gb_acc, gb_ref), (gc_acc, gc_ref)):
                for k in range(2):
                    own = jnp.where(mask, acc[:, k * WB:(k + 1) * WB], 0.0)
                    out[0, k] = jnp.dot(own, fold, precision=lax.Precision.HIGHEST, preferred_element_type=F32)

    rev = lambda b, i: (nt - 1 - i, b)
    slab = lambda b, i: (b, nt - 1 - i, 0)
    blk = lambda b, i: (b, 0, 0)
    blk4 = lambda b, i: (b, 0, 0, 0)
    return _call(
        body, name="s5_bwd", grid=(N_GBLK, nt), exchanges=exchanges, semantics=("arbitrary", "arbitrary"),
        operands=(dys, u4, s_re, s_im, bmat, cmat, coef_rev, dskip),
        in_specs=[
            _slab_spec(UB, tm, slab), _slab_spec(UB, tm, slab), pl.BlockSpec((tm, WB), rev), pl.BlockSpec((tm, WB), rev),
            pl.BlockSpec((1, UB, 2 * WB), blk), pl.BlockSpec((1, UB, 2 * WB), blk),
            pl.BlockSpec((N_TABLES, SUBLANES, WB), lambda b, i: (0, 0, b)), pl.BlockSpec((1, 1, UB), blk),
        ],
        out_specs=[
            _slab_spec(UB, tm, slab), pl.BlockSpec((1, 2, UB, STATE), blk4), pl.BlockSpec((1, 2, UB, STATE), blk4),
            pl.BlockSpec((1, 2, WB), blk), pl.BlockSpec((1, 1, UB), blk),
        ],
        out_shape=[
            _slab_shape(L, D_SSM), jax.ShapeDtypeStruct((N_GBLK, 2, UB, STATE), F32),
            jax.ShapeDtypeStruct((N_GBLK, 2, UB, STATE), F32), jax.ShapeDtypeStruct((N_GBLK, 2, WB), F32),
            jax.ShapeDtypeStruct((N_GBLK, 1, UB), F32),
        ],
        scratch_shapes=[pltpu.VMEM((tm, WB), F32)] * 4 + [pltpu.VMEM((SUBLANES, WB), F32)] * 4 + [pltpu.VMEM((UB, 2 * WB), F32)] * 2,
    )


def _inproj_bwd(du, dhbc, x, dx1, w_in_all, g1):
    L, D = x.shape
    ns, _, nc = w_in_all.shape
    tm = _tile(L, TM_PROJ)

    def body(du_ref, dhbc_ref, x_ref, dx1_ref, w_ref, g_ref, gx_ref, dproj_ref, dg_ref):
        @pl.when(pl.program_id(0) == 0)
        def _():
            dg_ref[...] = jnp.zeros_like(dg_ref)

        du_b = _load_slabs(du_ref).astype(MXU_DTYPE)
        dproj_ref[:, :nc] = du_b
        dproj_ref[:, nc:] = dhbc_ref[...]
        dhn = _dot_nt(du_b, w_ref[0])
        for j in range(1, ns):
            dhn = dhn + _dot_nt(dhbc_ref[:, (j - 1) * nc:j * nc], w_ref[j])
        dx, dg = _rms_bwd(x_ref[...], g_ref[...], dhn)
        dg_ref[...] += dg
        gx_ref[...] = dx1_ref[...] + dx

    row = lambda i: (i, 0)
    return pl.pallas_call(
        body, name="inproj_bwd", grid=(L // tm,),
        in_specs=[_slab_spec(nc, tm), pl.BlockSpec((tm, (ns - 1) * nc), row), pl.BlockSpec((tm, D), row), pl.BlockSpec((tm, D), row),
                  _resident(w_in_all.shape), _resident(g1.shape)],
        out_specs=[pl.BlockSpec((tm, D), row), pl.BlockSpec((tm, ns * nc), row), pl.BlockSpec((1, D), lambda i: (0, 0))],
        out_shape=[jax.ShapeDtypeStruct((L, D), F32), jax.ShapeDtypeStruct((L, ns * nc), MXU_DTYPE), jax.ShapeDtypeStruct((1, D), F32)],
        compiler_params=_params(("arbitrary",)),
    )(du, dhbc, x, dx1, w_in_all, g1)


def _matmul_tn(a, b, name, col_shards=1, exchanges=()):
    L, K = a.shape
    N = b.shape[1]
    tl = _tile(L, TL_TN)
    tk = _tile(K, 1024)
    nw = N // col_shards
    tn = _tile(nw, 1024)
    npb = nw // tn

    def body(a_ref, b_ref, o_ref):
        @pl.when(pl.program_id(2) == 0)
        def _():
            o_ref[...] = jnp.zeros_like(o_ref)

        o_ref[0] += _dot_tn(a_ref[...], b_ref[...])

    return _call(
        body, name=name, grid=(K // tk, N // tn, L // tl), exchanges=exchanges, semantics=("arbitrary", "arbitrary", "arbitrary"),
        operands=(a, b),
        in_specs=[pl.BlockSpec((tl, tk), lambda k, n, l: (l, k)), pl.BlockSpec((tl, tn), lambda k, n, l: (l, n))],
        out_specs=[pl.BlockSpec((1, tk, tn), lambda k, n, l: (n // npb, k, n % npb))],
        out_shape=[jax.ShapeDtypeStruct((col_shards, K, nw), F32)],
    )


def _ssm_discretize(lam_re, lam_im, log_dt, bt_re, bt_im):
    dt = jnp.exp(log_dt)[:, None]
    zr = lam_re * dt
    zi = lam_im * dt
    mag = jnp.exp(zr)
    abr = mag * jnp.cos(zi)
    abi = mag * jnp.sin(zi)
    nr, ni = abr - 1.0, abi
    den = lam_re * lam_re + lam_im * lam_im
    coef_r = ((nr * lam_re + ni * lam_im) / den)[:, None, :]
    coef_i = ((ni * lam_re - nr * lam_im) / den)[:, None, :]
    return zr, zi, coef_r * bt_re - coef_i * bt_im, coef_r * bt_im + coef_i * bt_re


def _scan_tables(ar, ai, reverse):
    rows = np.arange(SUBLANES)
    exps = np.zeros((N_TABLES // 2, SUBLANES), np.int32)
    keep = np.ones((N_TABLES // 2, SUBLANES), bool)
    for t, k in enumerate((1, 2, 4)):
        exps[t] = SUBLANES * k
        keep[t] = (rows + k <= SUBLANES - 1) if reverse else (rows >= k)
    exps[3] = SUBLANES * (SUBLANES - rows) if reverse else SUBLANES * (rows + 1)
    for j in range(SUBLANES):
        exps[4 + j] = SUBLANES - j if reverse else j + 1
    pr, pi = ar, (-ai if reverse else ai)
    shape = (N_TABLES // 2, SUBLANES, ar.shape[0])
    xr, xi = jnp.ones(shape, F32), jnp.zeros(shape, F32)
    for bit in range(int(exps.max()).bit_length()):
        on = ((exps >> bit) & 1).astype(bool)[:, :, None]
        xr, xi = jnp.where(on, xr * pr - xi * pi, xr), jnp.where(on, xr * pi + xi * pr, xi)
        pr, pi = pr * pr - pi * pi, 2.0 * pr * pi
    xr = jnp.where(keep[:, :, None], xr, 0.0)
    xi = jnp.where(keep[:, :, None], xi, 0.0)
    return jnp.stack([xr, xi], axis=1).reshape(N_TABLES, SUBLANES, ar.shape[0])


def _group_mask(rows, cols):
    r = lax.broadcasted_iota(jnp.int32, (rows, cols), 0) // GROUP
    c = lax.broadcasted_iota(jnp.int32, (rows, cols), 1) // STATE
    return r == c


def _ssm_expand(bt_re, bt_im, c_re, c_im):
    flat = lambda a: a.reshape(N_GROUPS * GROUP, STATE)

    def body(br_ref, bi_ref, cr_ref, ci_ref, bm_ref, cm_ref):
        spread = (lax.broadcasted_iota(jnp.int32, (STATE, WB), 1) % STATE == lax.broadcasted_iota(jnp.int32, (STATE, WB), 0)).astype(F32)
        mask = _group_mask(UB, WB)

        def expand(x):
            wide = jnp.dot(x, spread, precision=lax.Precision.HIGHEST, preferred_element_type=F32)
            return jnp.where(mask, wide, 0.0).astype(MXU_DTYPE)

        bm_ref[0, :, :WB] = expand(br_ref[...])
        bm_ref[0, :, WB:] = expand(bi_ref[...])
        cm_ref[0, :, :WB] = expand(cr_ref[...])
        cm_ref[0, :, WB:] = expand(-ci_ref[...])

    spec = pl.BlockSpec((UB, STATE), lambda b: (b, 0))
    out = pl.BlockSpec((1, UB, 2 * WB), lambda b: (b, 0, 0))
    return pl.pallas_call(
        body, name="ssm_expand", grid=(N_GBLK,), in_specs=[spec] * 4, out_specs=[out, out],
        out_shape=[jax.ShapeDtypeStruct((N_GBLK, UB, 2 * WB), MXU_DTYPE)] * 2,
        compiler_params=_params(("arbitrary",)),
    )(flat(bt_re), flat(bt_im), flat(c_re), flat(c_im))


def _ssm_matrices(lam_re, lam_im, log_dt, bt_re, bt_im, c_re, c_im):
    zr, zi, bbar_r, bbar_i = _ssm_discretize(lam_re, lam_im, log_dt, bt_re, bt_im)
    mag = jnp.exp(zr)
    ar = (mag * jnp.cos(zi)).reshape(-1)
    ai = (mag * jnp.sin(zi)).reshape(-1)
    bmat, cmat = _ssm_expand(bbar_r, bbar_i, c_re, c_im)
    return bmat, cmat, _scan_tables(ar, ai, False), _scan_tables(ar, ai, True)


def _ssm_param_grads(lam_re, lam_im, log_dt, bt_re, bt_im, gb, gc, q, gd):
    part = lambda g, k: g[:, k].reshape(N_GROUPS, GROUP, STATE)
    qr = q[:, 0, :].reshape(N_GROUPS, STATE)
    qi = q[:, 1, :].reshape(N_GROUPS, STATE)
    _, vjp = jax.vjp(_ssm_discretize, lam_re, lam_im, log_dt, bt_re, bt_im)
    d_lam_re, d_lam_im, d_log_dt, d_bt_re, d_bt_im = vjp((qr, qi, part(gb, 0), part(gb, 1)))
    return d_lam_re, d_lam_im, d_log_dt, d_bt_re, d_bt_im, part(gc, 0), -part(gc, 1), gd.reshape(N_GROUPS, GROUP)


def _row_tile(rows, n):
    return _tile(rows, max(SUBLANES, (2 * 1024 * 1024) // (4 * n)))


def _pair_add(grad, other, core, name):
    ns, h, n = other.shape
    tr = _row_tile(h, n)
    nb = h // tr

    def body(c_ref, g_ref, o_ref, out_ref):
        out_ref[...] = (g_ref[...] + o_ref[...]).astype(WIRE_DTYPE)

    return pl.pallas_call(
        body, name=name,
        grid_spec=pltpu.PrefetchScalarGridSpec(
            num_scalar_prefetch=1, grid=(ns, nb),
            in_specs=[pl.BlockSpec((1, tr, n), lambda s, i, c: (s, c[0] * nb + i, 0)), pl.BlockSpec((1, tr, n), lambda s, i, c: (s, i, 0))],
            out_specs=pl.BlockSpec((1, tr, n), lambda s, i, c: (s, i, 0))),
        out_shape=jax.ShapeDtypeStruct(other.shape, WIRE_DTYPE),
        compiler_params=_params(("arbitrary", "arbitrary")),
    )(core, grad, other)


def _quad_sum(parts, core, name):
    ns, h, n = parts.shape
    tr = _row_tile(h, n)
    nb = h // tr

    def body(c_ref, p_ref, out_ref):
        p = [p_ref[k].astype(F32) for k in range(ns)]
        out_ref[...] = ((p[0] + p[1]) + p[2]) + p[3]

    return pl.pallas_call(
        body, name=name,
        grid_spec=pltpu.PrefetchScalarGridSpec(
            num_scalar_prefetch=1, grid=(nb,),
            in_specs=[pl.BlockSpec((ns, tr, n), lambda i, c: (0, i, 0))],
            out_specs=pl.BlockSpec((tr, n), lambda i, c: (c[0] * nb + i, 0))),
        out_shape=jax.ShapeDtypeStruct((2 * h, n), F32),
        compiler_params=_params(("arbitrary",)),
    )(core, parts)


def _adamw_math(w, g, m, v):
    m = ADAM_B1 * m + (1.0 - ADAM_B1) * g
    v = ADAM_B2 * v + (1.0 - ADAM_B2) * jnp.square(g)
    m_hat = m / (1.0 - ADAM_B1 ** ADAM_STEP)
    v_hat = v / (1.0 - ADAM_B2 ** ADAM_STEP)
    delta = -ADAM_LR * (m_hat / (jnp.sqrt(v_hat) + ADAM_EPS) + ADAM_WD * w)
    return delta, m, v


def _adamw(w, g, m, v, name):
    r, n = w.shape
    tr = _row_tile(r, n)

    def body(w_ref, g_ref, m_ref, v_ref, d_ref, nm_ref, nv_ref):
        d_ref[...], nm_ref[...], nv_ref[...] = _adamw_math(w_ref[...], g_ref[...], m_ref[...], v_ref[...])

    spec = pl.BlockSpec((tr, n), lambda i: (i, 0))
    return pl.pallas_call(
        body, name=name, grid=(r // tr,), in_specs=[spec] * 4, out_specs=[spec] * 3,
        out_shape=[jax.ShapeDtypeStruct((r, n), F32)] * 3,
        compiler_params=_params(("arbitrary",)),
    )(w, g, m, v)


LANES = 128
SMALL = ["g_pre_mix", "lam_re", "lam_im", "log_dt", "b_re", "b_im", "c_re", "c_im", "d_skip", "conv_w", "g_ssm_out", "g_conv_out",
         "g_post_mix", "g_pre_mlp", "g_post_mlp"]
TILE_SLOTS = {"b_re": (0, N_GROUPS), "b_im": (N_GROUPS, N_GROUPS), "c_re": (2 * N_GROUPS, N_GROUPS), "c_im": (3 * N_GROUPS, N_GROUPS),
              "lam_re": (4 * N_GROUPS, 2), "lam_im": (4 * N_GROUPS + 2, 2)}
N_TILE_SLOTS = 4 * N_GROUPS + 4
VEC_ROWS = {"g_pre_mix": 0, "g_post_mix": 1, "g_pre_mlp": 2, "g_post_mlp": 3, "g_ssm_out": 4, "g_conv_out": 5, "log_dt": 6}
ROW_LOSS, ROW_DSKIP, ROW_CONV, N_PACK_ROWS = 7, 8, 24, 32


def _kernel_form(name, a):
    if name in ("b_re", "b_im"):
        return jnp.transpose(a, (0, 1, 3, 2)).reshape(N_GROUPS, GROUP, STATE)
    if name in ("c_re", "c_im"):
        return a.reshape(N_GROUPS, GROUP, STATE)
    if name in ("lam_re", "lam_im"):
        return a.reshape(2, GROUP, STATE)
    if name == "d_skip":
        return jnp.transpose(a, (0, 2, 1)).reshape(GROUP, N_GROUPS)
    if name == "conv_w":
        return jnp.transpose(a, (1, 0, 2))
    return a


def _param_form(name, k):
    if name in ("b_re", "b_im"):
        return jnp.transpose(k.reshape(1, N_GROUPS, GROUP, STATE), (0, 1, 3, 2))
    if name in ("c_re", "c_im"):
        return k.reshape(1, N_GROUPS, GROUP, STATE)
    if name in ("lam_re", "lam_im"):
        return k.reshape(1, N_GROUPS, STATE)
    if name == "d_skip":
        return jnp.transpose(k.reshape(1, GROUP, N_GROUPS), (0, 2, 1))
    if name == "conv_w":
        return jnp.transpose(k, (1, 0, 2))
    return k


def _pack_tiles(g):
    lam = lambda a: a.reshape(2, GROUP, STATE)
    return jnp.concatenate([g["b_re"], g["b_im"], g["c_re"], g["c_im"], lam(g["lam_re"]), lam(g["lam_im"])], axis=0)


def _pack_rows(g, loss):
    row = lambda a: jnp.pad(a, ((0, 0), (0, D_MODEL - a.shape[1])))
    rows = [row(g[k][None]) for k in VEC_ROWS] + [row(loss[0:1]), row(g["d_skip"].T), row(g["conv_w"])]
    rows.append(jnp.zeros((N_PACK_ROWS - ROW_CONV - 3, D_MODEL), F32))
    return jnp.concatenate(rows, axis=0)


def _adamw_small(tiles, rows, w, m, v):
    nn = len(SMALL)

    def body(*refs):
        t_ref, r_ref = refs[0], refs[1]
        w_refs, m_refs, v_refs = refs[2:2 + nn], refs[2 + nn:2 + 2 * nn], refs[2 + 2 * nn:2 + 3 * nn]
        loss_ref, outs = refs[2 + 3 * nn], refs[3 + 3 * nn:]

        def tile_sum(first, count):
            total = t_ref[0, first:first + count]
            for d in range(1, N_DEV):
                total = total + t_ref[d, first:first + count]
            return total

        def row_sum(first, count, lanes):
            total = r_ref[0, first:first + count, 0:lanes]
            for d in range(1, N_DEV):
                total = total + r_ref[d, first:first + count, 0:lanes]
            return total

        def step(j, g, at=lambda ref: ref):
            delta, nm, nv = _adamw_math(at(w_refs[j])[...], g, at(m_refs[j])[...], at(v_refs[j])[...])
            at(outs[j])[...] = g
            at(outs[nn + j])[...] = delta
            at(outs[2 * nn + j])[...] = nm
            at(outs[3 * nn + j])[...] = nv

        loss_ref[...] = row_sum(ROW_LOSS, 1, LANES)
        chip = 2 * lax.axis_index("x") + lax.axis_index("y")
        for j, name in enumerate(SMALL):
            if name in TILE_SLOTS:
                step(j, tile_sum(*TILE_SLOTS[name]))
            elif name == "d_skip":
                step(j, row_sum(ROW_DSKIP, GROUP, N_GROUPS))
            elif name == "conv_w":
                full = row_sum(ROW_CONV, 3, D_CONV)
                mine = full[:, 0:LANES]
                for s in range(1, N_CHIPS):
                    mine = jnp.where(chip == s, full[:, s * LANES:(s + 1) * LANES], mine)
                for k in range(3):
                    step(j, mine[k:k + 1, :], at=lambda ref, k=k: ref.at[k])
            else:
                step(j, row_sum(VEC_ROWS[name], 1, w_refs[j].shape[1]))

    args = [tiles, rows] + [w[k] for k in SMALL] + [m[k] for k in SMALL] + [v[k] for k in SMALL]
    res = pl.pallas_call(
        body, name="adamw_small", in_specs=[VMEM] * len(args), out_specs=[VMEM] * (1 + 4 * nn),
        out_shape=[jax.ShapeDtypeStruct((1, LANES), F32)] + [jax.ShapeDtypeStruct(w[k].shape, F32) for k in SMALL] * 4,
        compiler_params=pltpu.CompilerParams(vmem_limit_bytes=VMEM_LIMIT),
    )(*args)
    return res[0], [dict(zip(SMALL, res[1 + q * nn:1 + (q + 1) * nn])) for q in range(4)]


WEIGHTS = ["g_pre_mix", "w_in", "lam_re", "lam_im", "log_dt", "b_re", "b_im", "c_re", "c_im", "d_skip", "w_glu", "conv_w",
           "g_ssm_out", "g_conv_out", "w_out", "g_post_mix", "g_pre_mlp", "w_up", "w_down", "g_post_mlp"]
BIG = ["w_in", "w_glu", "w_out", "w_up", "w_down"]


def kernel(x, g_pre_mix, w_in, lam_re, lam_im, log_dt, b_re, b_im, c_re, c_im, d_skip, w_glu, conv_w, g_ssm_out, g_conv_out, w_out, g_post_mix, g_pre_mlp, w_up, w_down, g_post_mlp, loss_target, m_g_pre_mix, m_w_in, m_lam_re, m_lam_im, m_log_dt, m_b_re, m_b_im, m_c_re, m_c_im, m_d_skip, m_w_glu, m_conv_w, m_g_ssm_out, m_g_conv_out, m_w_out, m_g_post_mix, m_g_pre_mlp, m_w_up, m_w_down, m_g_post_mlp, v_g_pre_mix, v_w_in, v_lam_re, v_lam_im, v_log_dt, v_b_re, v_b_im, v_c_re, v_c_im, v_d_skip, v_w_glu, v_conv_w, v_g_ssm_out, v_g_conv_out, v_w_out, v_g_post_mix, v_g_pre_mlp, v_w_up, v_w_down, v_g_post_mlp):
    w = dict(g_pre_mix=g_pre_mix, w_in=w_in, lam_re=lam_re, lam_im=lam_im, log_dt=log_dt, b_re=b_re, b_im=b_im, c_re=c_re, c_im=c_im,
             d_skip=d_skip, w_glu=w_glu, conv_w=conv_w, g_ssm_out=g_ssm_out, g_conv_out=g_conv_out, w_out=w_out, g_post_mix=g_post_mix,
             g_pre_mlp=g_pre_mlp, w_up=w_up, w_down=w_down, g_post_mlp=g_post_mlp)
    m = dict(g_pre_mix=m_g_pre_mix, w_in=m_w_in, lam_re=m_lam_re, lam_im=m_lam_im, log_dt=m_log_dt, b_re=m_b_re, b_im=m_b_im, c_re=m_c_re,
             c_im=m_c_im, d_skip=m_d_skip, w_glu=m_w_glu, conv_w=m_conv_w, g_ssm_out=m_g_ssm_out, g_conv_out=m_g_conv_out, w_out=m_w_out,
             g_post_mix=m_g_post_mix, g_pre_mlp=m_g_pre_mlp, w_up=m_w_up, w_down=m_w_down, g_post_mlp=m_g_post_mlp)
    v = dict(g_pre_mix=v_g_pre_mix, w_in=v_w_in, lam_re=v_lam_re, lam_im=v_lam_im, log_dt=v_log_dt, b_re=v_b_re, b_im=v_b_im, c_re=v_c_re,
             c_im=v_c_im, d_skip=v_d_skip, w_glu=v_w_glu, conv_w=v_conv_w, g_ssm_out=v_g_ssm_out, g_conv_out=v_g_conv_out, w_out=v_w_out,
             g_post_mix=v_g_post_mix, g_pre_mlp=v_g_pre_mlp, w_up=v_w_up, w_down=v_w_down, g_post_mlp=v_g_post_mlp)
    w_dev, m_dev, v_dev = w, m, v
    w, m, v = ({k: a[0] for k, a in d.items()} for d in (w, m, v))
    core = lax.axis_index("c").astype(jnp.int32).reshape(1)

    xs, target = x[0], loss_target[0]
    g1 = w["g_pre_mix"][None]
    g_ssm, g_conv = w["g_ssm_out"][None], w["g_conv_out"][None]
    g_post_mix, g_pre_mlp, g_post_mlp = w["g_post_mix"][None], w["g_pre_mlp"][None], w["g_post_mlp"][None]
    bt_re, bt_im = (jnp.transpose(w[k], (0, 2, 1)) for k in ("b_re", "b_im"))
    bmat, cmat, coef_f, coef_r = _ssm_matrices(w["lam_re"], w["lam_im"], w["log_dt"], bt_re, bt_im, w["c_re"], w["c_im"])
    dskip = w["d_skip"].reshape(N_GBLK, 1, UB)
    shard = {k: w[k].astype(MXU_DTYPE) for k in BIG}
    conv_pad = jnp.pad(w["conv_w"], ((0, SUBLANES - 3), (0, 0)))

    (w_in_all,) = _run_exchanges([_GatherForward([shard["w_in"]])], "ag_w_in")
    hn, proj, u4, w_glu_all, w_out_all, conv_all = _inproj_fwd(
        xs, g1, w_in_all, exchanges=[_Gather([shard["w_glu"], shard["w_out"], conv_pad], [False, False, False])])
    s_re, s_im, ys, w_up_all, w_down_all = _s5_fwd(
        u4, bmat, cmat, coef_f, dskip, exchanges=[_Gather([shard["w_up"], shard["w_down"]], [True, True])])
    w_glu_f, w_out_f = w_glu_all.reshape(D_SSM, D_SSM), w_out_all.reshape(D_MODEL, D_MODEL)
    conv_f = jnp.transpose(conv_all, (1, 0, 2)).reshape(SUBLANES, D_CONV)
    ycat, o, x1, w_up_all, w_down_all = _tail_fwd(xs, ys, proj, w_glu_f, conv_f, g_ssm, g_conv, w_out_f, g_post_mix,
                                                  exchanges=[_Forward([w_up_all, w_down_all])])
    w_down_f = w_down_all.reshape(D_FF, D_MODEL)
    hn2, up, m_act, dx2, loss = _mlp_fwd(x1, target, w_up_all, w_down_f, g_pre_mlp, g_post_mlp)

    dm, dup, act, dx1, dg_post_mlp, dg_pre_mlp = _mlp_bwd(dx2, m_act, up, x1, w_up_all, w_down_f, g_pre_mlp, g_post_mlp)
    gw_down = _matmul_tn(act, dm, "dw_down")[0].reshape(N_CHIPS, D_FF // N_CHIPS, D_MODEL)
    gw_up = _matmul_tn(hn2, dup, "dw_up", col_shards=N_CHIPS)[0]
    do, da, y1, dys, dhbc, dg_post_mix, dg_ssm, dg_conv, dconv_w, o_down, o_up = _tail_bwd(
        dx1, o, ys, proj, w_glu_f, conv_f, g_ssm, g_conv, w_out_f, g_post_mix, exchanges=[_Pair([gw_down, gw_up])])
    p_down = _pair_add(gw_down, o_down, core, "pair_add_w_down")
    p_up = _pair_add(gw_up, o_up, core, "pair_add_w_up")
    gw_out = _matmul_tn(ycat, do, "dw_out")[0].reshape(N_CHIPS, D_MODEL // N_CHIPS, D_MODEL)
    gw_glu = _matmul_tn(y1, da, "dw_glu")[0].reshape(N_CHIPS, D_SSM // N_CHIPS, D_SSM)
    du, gb, gc, q, gd, q_down, q_up, o_out, o_glu = _s5_bwd(
        dys, u4, s_re, s_im, bmat, cmat, coef_r, dskip, exchanges=[_Chip([p_down, p_up]), _Pair([gw_out, gw_glu])])
    h_down = _quad_sum(q_down, core, "quad_sum_w_down")
    h_up = _quad_sum(q_up, core, "quad_sum_w_up")
    p_out = _pair_add(gw_out, o_out, core, "pair_add_w_out")
    p_glu = _pair_add(gw_glu, o_glu, core, "pair_add_w_glu")
    grad_x, dproj, dg_pre_mix = _inproj_bwd(du, dhbc, xs, dx1, w_in_all, g1)
    d_lam_re, d_lam_im, d_log_dt, d_b_re, d_b_im, d_c_re, d_c_im, d_d_skip = _ssm_param_grads(
        w["lam_re"], w["lam_im"], w["log_dt"], bt_re, bt_im, gb, gc, q, gd)
    small = {
        "g_pre_mix": dg_pre_mix[0], "lam_re": d_lam_re, "lam_im": d_lam_im, "log_dt": d_log_dt, "b_re": d_b_re, "b_im": d_b_im,
        "c_re": d_c_re, "c_im": d_c_im, "d_skip": d_d_skip, "conv_w": dconv_w[:3], "g_ssm_out": dg_ssm[0], "g_conv_out": dg_conv[0],
        "g_post_mix": dg_post_mix[0], "g_pre_mlp": dg_pre_mlp[0], "g_post_mlp": dg_post_mlp[0],
    }
    gw_in, g_down, g_up, q_out, q_glu, tiles, rows = _matmul_tn(
        hn, dproj, "dw_in", col_shards=N_CHIPS,
        exchanges=[_Share([h_down, h_up]), _Chip([p_out, p_glu]), _GatherSmall(_pack_tiles(small)), _GatherSmall(_pack_rows(small, loss))])
    h_out = _quad_sum(q_out, core, "quad_sum_w_out")
    h_glu = _quad_sum(q_glu, core, "quad_sum_w_glu")
    (o_in,) = _run_exchanges([_Pair([gw_in])], "rs_pair_w_in")
    p_in = _pair_add(gw_in, o_in, core, "pair_add_w_in")
    q_in, g_out, g_glu = _run_exchanges([_Chip([p_in]), _Share([h_out, h_glu])], "rs_chip_w_in")
    h_in = _quad_sum(q_in, core, "quad_sum_w_in")
    (g_in,) = _run_exchanges([_Share([h_in])], "rs_share_w_in")
    shard_grads = {"w_in": g_in, "w_glu": g_glu, "w_out": g_out, "w_up": g_up, "w_down": g_down}

    out = {q: {} for q in ("grad", "delta", "new_m", "new_v")}
    for k in BIG:
        out["grad"][k] = shard_grads[k][None]
        delta, new_m, new_v = _adamw(w[k], shard_grads[k], m[k], v[k], "adamw_" + k)
        out["delta"][k], out["new_m"][k], out["new_v"][k] = delta[None], new_m[None], new_v[None]
    form = lambda d: {k: _kernel_form(k, d[k]) for k in SMALL}
    loss, res = _adamw_small(tiles, rows, form(w_dev), form(m_dev), form(v_dev))
    for q, d in zip(("grad", "delta", "new_m", "new_v"), res):
        out[q].update({k: _param_form(k, d[k]) for k in SMALL})
    flat = [loss[0, 0], grad_x[None]]
    for q in ("grad", "delta", "new_m", "new_v"):
        flat += [out[q][k] for k in WEIGHTS]
    return tuple(flat)
```

```python
import functools
import math

import jax
import jax.numpy as jnp
import numpy as np
from jax import lax
from jax.experimental import pallas as pl
from jax.experimental.pallas import tpu as pltpu

F32 = jnp.float32
MXU_DTYPE = jnp.bfloat16
WIRE_DTYPE = jnp.bfloat16

D_MODEL = 1024
D_SSM = 512
D_CONV = 512
N_GROUPS = 32
GROUP = 16
STATE = 64
D_FF = 4096
RMS_EPS = 1e-6
N_CHIPS = 4
N_DEV = 8

ADAM_LR = 0.001
ADAM_B1 = 0.9
ADAM_B2 = 0.999
ADAM_EPS = 1e-08
ADAM_WD = 0.01
ADAM_STEP = 10

N_GBLK = 2
G_PER_BLK = N_GROUPS // N_GBLK
UB = G_PER_BLK * GROUP
WB = G_PER_BLK * STATE
LANE_CHUNK = 256
SUBLANES = 8
N_TABLES = 24

TM_PROJ = 512
TM_S5 = 512
TM_TAIL = 256
TM_MLP = 256
TL_TN = 2048
VMEM_LIMIT = 56 * 1024 * 1024

MESH = pl.DeviceIdType.MESH


def _params(sem, vmem=VMEM_LIMIT):
    return pltpu.CompilerParams(dimension_semantics=sem, vmem_limit_bytes=vmem)


def _resident(shape):
    nd = len(shape)
    return pl.BlockSpec(shape, lambda *_: (0,) * nd, pipeline_mode=pl.Buffered(1))


def _dot(a, b):
    return jnp.dot(a, b, preferred_element_type=F32)


def _dot_nt(a, b):
    return lax.dot_general(a, b, (((1,), (1,)), ((), ())), preferred_element_type=F32)


def _dot_tn(a, b):
    return lax.dot_general(a, b, (((0,), (0,)), ((), ())), preferred_element_type=F32)


def _rms_fwd(x, g):
    r = lax.rsqrt(jnp.mean(x * x, axis=-1, keepdims=True) + RMS_EPS)
    return x * r * g


def _rms_bwd(x, g, dy):
    r = lax.rsqrt(jnp.mean(x * x, axis=-1, keepdims=True) + RMS_EPS)
    xn = x * r
    q = dy * g
    dx = r * (q - xn * jnp.mean(q * xn, axis=-1, keepdims=True))
    return dx, jnp.sum(dy * xn, axis=0, keepdims=True)


_GELU_C = math.sqrt(2.0 / math.pi)


def _gelu(x):
    t = jnp.tanh(_GELU_C * (x + 0.044715 * (x * x * x)))
    y = x * (0.5 * (1.0 + t))
    dy = 0.5 * (1.0 + t) + 0.5 * x * (1.0 - t * t) * (_GELU_C * (1.0 + 3 * 0.044715 * (x * x)))
    return y, dy


def _tile(n, pref):
    t = min(n, pref)
    assert n % t == 0, (n, t)
    return t


HBM = pl.BlockSpec(memory_space=pltpu.HBM)
VMEM = pl.BlockSpec(memory_space=pltpu.VMEM)
DMA_SEMS = pltpu.SemaphoreType.DMA


def _place():
    x, y, c = lax.axis_index("x"), lax.axis_index("y"), lax.axis_index("c")
    chips = [(1 - x, y), (x, 1 - y), (1 - x, 1 - y)]
    return (x, y, c), 2 * x + y, (x, y, 1 - c), chips, [2 * px + py for px, py in chips]


def _remote(src, dst, send_sem, recv_sem, device):
    return pltpu.make_async_remote_copy(src_ref=src, dst_ref=dst, send_sem=send_sem, recv_sem=recv_sem,
                                        device_id=device, device_id_type=MESH)


def _half(rows, c):
    return pl.ds(c * (rows // 2), rows // 2)


class _Exchange:
    aliases = {}

    def start(self, ins, outs, sems):
        local, outgoing, _ = self._copies(ins, outs, sems)
        for cp in local + outgoing:
            cp.start()

    def finish(self, ins, outs, sems):
        local, outgoing, incoming = self._copies(ins, outs, sems)
        for cp in incoming:
            cp.wait_recv()
        for cp in outgoing:
            cp.wait_send()
        for cp in local:
            cp.wait()


class _Gather(_Exchange):
    def __init__(self, shards, split):
        self.inputs, self.split = list(shards), split
        self.out_shape = [jax.ShapeDtypeStruct((N_CHIPS, *a.shape), a.dtype) for a in shards]
        self.sems = [DMA_SEMS((len(shards), 3)), DMA_SEMS((len(shards), 3)), DMA_SEMS((len(shards),))]

    def _copies(self, ins, outs, sems):
        send, recv, lsem = sems
        (x, y, c), me, sibling, chips, ids = _place()
        local = [pltpu.make_async_copy(ins[t], outs[t].at[me], lsem.at[t]) for t in range(len(ins))]
        outgoing, incoming = [], []
        for t, a in enumerate(self.inputs):
            rows = _half(a.shape[0], c) if self.split[t] else pl.ds(0, a.shape[0])
            for k in range(3):
                to = (*chips[k], c)
                outgoing.append(_remote(ins[t].at[rows, :], outs[t].at[me, rows, :], send.at[t, k], recv.at[t, k], to))
                incoming.append(_remote(ins[t].at[rows, :], outs[t].at[ids[k], rows, :], send.at[t, k], recv.at[t, k], to))
        return local, outgoing, incoming


class _Forward(_Exchange):
    def __init__(self, arrays):
        self.inputs = list(arrays)
        self.out_shape = [jax.ShapeDtypeStruct(a.shape, a.dtype) for a in arrays]
        self.aliases = {t: t for t in range(len(arrays))}
        self.sems = [DMA_SEMS((len(arrays), 3)), DMA_SEMS((len(arrays), 3))]

    def _copies(self, ins, outs, sems):
        send, recv = sems
        (x, y, c), me, sibling, chips, ids = _place()
        outgoing, incoming = [], []
        for t, a in enumerate(self.inputs):
            for k in range(3):
                mine = outs[t].at[ids[k], _half(a.shape[1], c), :]
                theirs = outs[t].at[ids[k], _half(a.shape[1], 1 - c), :]
                outgoing.append(_remote(mine, mine, send.at[t, k], recv.at[t, k], sibling))
                incoming.append(_remote(theirs, theirs, send.at[t, k], recv.at[t, k], sibling))
        return [], outgoing, incoming


class _GatherForward(_Exchange):
    def __init__(self, shards):
        self.gather = _Gather(shards, [True] * len(shards))
        self.forward = _Forward(self.gather.out_shape)
        self.inputs, self.out_shape = self.gather.inputs, self.gather.out_shape
        self.sems = self.gather.sems + self.forward.sems

    def start(self, ins, outs, sems):
        self.gather.start(ins, outs, sems[:3])

    def finish(self, ins, outs, sems):
        local, outgoing, incoming = self.gather._copies(ins, outs, sems[:3])
        _, passed, from_sibling = self.forward._copies(outs, outs, sems[3:])
        for landed, onward in zip(incoming, passed):
            landed.wait_recv()
            onward.start()
        for cp in from_sibling:
            cp.wait_recv()
        for cp in outgoing + passed:
            cp.wait_send()
        for cp in local:
            cp.wait()


class _Pair(_Exchange):
    def __init__(self, grads):
        self.inputs = list(grads)
        self.out_shape = [jax.ShapeDtypeStruct((g.shape[0], g.shape[1] // 2, g.shape[2]), g.dtype) for g in grads]
        self.sems = [DMA_SEMS((len(grads),)), DMA_SEMS((len(grads),))]

    def _copies(self, ins, outs, sems):
        send, recv = sems
        (x, y, c), me, sibling, chips, ids = _place()
        cps = [_remote(ins[t].at[:, _half(g.shape[1], 1 - c), :], outs[t], send.at[t], recv.at[t], sibling)
               for t, g in enumerate(self.inputs)]
        return [], cps, cps


class _Chip(_Exchange):
    def __init__(self, parts):
        self.inputs = list(parts)
        self.out_shape = [jax.ShapeDtypeStruct(p.shape, p.dtype) for p in parts]
        self.sems = [DMA_SEMS((len(parts), 3)), DMA_SEMS((len(parts), 3)), DMA_SEMS((len(parts),))]

    def _copies(self, ins, outs, sems):
        send, recv, lsem = sems
        (x, y, c), me, sibling, chips, ids = _place()
        local = [pltpu.make_async_copy(ins[t].at[me], outs[t].at[me], lsem.at[t]) for t in range(len(ins))]
        outgoing, incoming = [], []
        for t in range(len(ins)):
            for k in range(3):
                to = (*chips[k], c)
                outgoing.append(_remote(ins[t].at[ids[k]], outs[t].at[me], send.at[t, k], recv.at[t, k], to))
                incoming.append(_remote(ins[t].at[ids[k]], outs[t].at[ids[k]], send.at[t, k], recv.at[t, k], to))
        return local, outgoing, incoming


class _Share(_Exchange):
    def __init__(self, grads):
        self.inputs = list(grads)
        self.out_shape = [jax.ShapeDtypeStruct(g.shape, g.dtype) for g in grads]
        self.aliases = {t: t for t in range(len(grads))}
        self.sems = [DMA_SEMS((len(grads),)), DMA_SEMS((len(grads),))]

    def _copies(self, ins, outs, sems):
        send, recv = sems
        (x, y, c), me, sibling, chips, ids = _place()
        outgoing, incoming = [], []
        for t, g in enumerate(self.inputs):
            mine = outs[t].at[_half(g.shape[0], c), :]
            theirs = outs[t].at[_half(g.shape[0], 1 - c), :]
            outgoing.append(_remote(mine, mine, send.at[t], recv.at[t], sibling))
            incoming.append(_remote(theirs, theirs, send.at[t], recv.at[t], sibling))
        return [], outgoing, incoming


class _GatherSmall(_Exchange):
    def __init__(self, block):
        self.inputs = [block]
        self.out_shape = [jax.ShapeDtypeStruct((N_DEV, *block.shape), block.dtype)]
        self.sems = [DMA_SEMS((7,)), DMA_SEMS((7,)), DMA_SEMS(())]

    def _copies(self, ins, outs, sems):
        send, recv, lsem = sems
        (x, y, c), me, sibling, chips, ids = _place()
        slot = lambda px, py, pc: outs[0].at[4 * px + 2 * py + pc]

        def copy(k, block, to, src=None):
            return _remote(slot(*block) if src is None else src, slot(*block), send.at[k], recv.at[k], to)

        local = [pltpu.make_async_copy(ins[0], slot(x, y, c), lsem)]
        first = [copy(0, (x, y, c), sibling, src=ins[0])] + [copy(1 + j, (x, y, c), (*chip, c), src=ins[0]) for j, chip in enumerate(chips)]
        passed = [copy(4 + j, (*chip, c), sibling) for j, chip in enumerate(chips)]
        landed = [copy(1 + j, (*chip, c), (x, y, c)) for j, chip in enumerate(chips)]
        from_sibling = [copy(0, (x, y, 1 - c), (x, y, c))] + [copy(4 + j, (*chip, 1 - c), (x, y, c)) for j, chip in enumerate(chips)]
        return local, first, (passed, landed, from_sibling)

    def finish(self, ins, outs, sems):
        local, first, (passed, landed, from_sibling) = self._copies(ins, outs, sems)
        for j in range(3):
            landed[j].wait_recv()
            passed[j].start()
        for cp in from_sibling:
            cp.wait_recv()
        for cp in first + passed:
            cp.wait_send()
        for cp in local:
            cp.wait()


def _split_refs(refs, counts):
    out = []
    for n in counts:
        out.append(refs[:n])
        refs = refs[n:]
    return out


def _each_exchange(exchanges, method, x_in, x_out, x_sem):
    for ex in exchanges:
        ni, no, ns = len(ex.inputs), len(ex.out_shape), len(ex.sems)
        getattr(ex, method)(x_in[:ni], x_out[:no], x_sem[:ns])
        x_in, x_out, x_sem = x_in[ni:], x_out[no:], x_sem[ns:]


def _call(body, *, name, grid, in_specs, out_specs, out_shape, operands, semantics, scratch_shapes=(), exchanges=()):
    x_in = [a for ex in exchanges for a in ex.inputs]
    x_out = [s for ex in exchanges for s in ex.out_shape]
    x_sem = [s for ex in exchanges for s in ex.sems]
    counts = (len(in_specs), len(x_in), len(out_specs), len(x_out), len(scratch_shapes), len(x_sem))
    aliases, i0, o0 = {}, len(in_specs), len(out_specs)
    for ex in exchanges:
        aliases.update({i0 + i: o0 + o for i, o in ex.aliases.items()})
        i0, o0 = i0 + len(ex.inputs), o0 + len(ex.out_shape)

    def full_body(*refs):
        ins, xi, outs, xo, scr, xs = _split_refs(list(refs), counts)
        if exchanges:
            @pl.when(functools.reduce(jnp.logical_and, [pl.program_id(a) == 0 for a in range(len(grid))]))
            def _():
                _each_exchange(exchanges, "start", xi, xo, xs)

        body(*ins, *outs, *scr)
        if exchanges:
            @pl.when(functools.reduce(jnp.logical_and, [pl.program_id(a) == grid[a] - 1 for a in range(len(grid))]))
            def _():
                _each_exchange(exchanges, "finish", xi, xo, xs)

    return pl.pallas_call(
        full_body, name=name, grid=grid,
        in_specs=list(in_specs) + [HBM] * len(x_in), out_specs=list(out_specs) + [HBM] * len(x_out),
        out_shape=list(out_shape) + x_out, scratch_shapes=list(scratch_shapes) + x_sem,
        input_output_aliases=aliases, compiler_params=_params(semantics),
    )(*operands, *x_in)


def _run_exchanges(exchanges, name):
    x_in = [a for ex in exchanges for a in ex.inputs]
    x_out = [s for ex in exchanges for s in ex.out_shape]
    x_sem = [s for ex in exchanges for s in ex.sems]
    aliases, i0, o0 = {}, 0, 0
    for ex in exchanges:
        aliases.update({i0 + i: o0 + o for i, o in ex.aliases.items()})
        i0, o0 = i0 + len(ex.inputs), o0 + len(ex.out_shape)

    def body(*refs):
        xi, xo, xs = _split_refs(list(refs), (len(x_in), len(x_out), len(x_sem)))
        _each_exchange(exchanges, "start", xi, xo, xs)
        _each_exchange(exchanges, "finish", xi, xo, xs)

    return pl.pallas_call(
        body, name=name, in_specs=[HBM] * len(x_in), out_specs=[HBM] * len(x_out), out_shape=x_out,
        scratch_shapes=x_sem, input_output_aliases=aliases,
    )(*x_in)


def _inproj_fwd(x, g1, w_in_all, exchanges=()):
    L, D = x.shape
    ns, _, nc = w_in_all.shape
    tm = _tile(L, TM_PROJ)

    def body(x_ref, g_ref, w_ref, hn_ref, proj_ref, u_ref):
        hn = _rms_fwd(x_ref[...], g_ref[...]).astype(MXU_DTYPE)
        hn_ref[...] = hn
        for j in range(ns):
            proj_ref[:, j * nc:(j + 1) * nc] = _dot(hn, w_ref[j])
        _store_slabs(u_ref, proj_ref[:, 0:nc])

    return _call(
        body, name="inproj_fwd", grid=(L // tm,), exchanges=exchanges, semantics=("arbitrary",), operands=(x, g1, w_in_all),
        in_specs=[pl.BlockSpec((tm, D), lambda i: (i, 0)), _resident((1, D)), _resident(w_in_all.shape)],
        out_specs=[pl.BlockSpec((tm, D), lambda i: (i, 0)), pl.BlockSpec((tm, ns * nc), lambda i: (i, 0)), _slab_spec(nc, tm)],
        out_shape=[jax.ShapeDtypeStruct((L, D), MXU_DTYPE), jax.ShapeDtypeStruct((L, ns * nc), F32), _slab_shape(L, nc)],
    )


def _slab_shape(L, n):
    return jax.ShapeDtypeStruct((n // LANES, L, LANES), F32)


def _slab_spec(n, tm, index=lambda i: (0, i, 0)):
    return pl.BlockSpec((n // LANES, tm, LANES), index)


def _store_slabs(ref, value):
    for k in range(ref.shape[0]):
        ref[k] = value[:, k * LANES:(k + 1) * LANES]


def _load_slabs(ref):
    return jnp.concatenate([ref[k] for k in range(ref.shape[0])], axis=1)


SEG_ROWS = SUBLANES * SUBLANES


def _load_permuted(ref):
    tm = ref.shape[1]
    slabs = []
    for k in range(ref.shape[0]):
        tiles = [ref.at[k][pl.ds(b * SEG_ROWS + j, SUBLANES, stride=SUBLANES), :] for b in range(tm // SEG_ROWS) for j in range(SUBLANES)]
        slabs.append(jnp.concatenate(tiles, axis=0))
    return jnp.concatenate(slabs, axis=1)


def _store_permuted(ref, value):
    tm = ref.shape[1]
    for k in range(ref.shape[0]):
        for b in range(tm // SEG_ROWS):
            for j in range(SUBLANES):
                r = b * SEG_ROWS + j * SUBLANES
                ref.at[k][pl.ds(b * SEG_ROWS + j, SUBLANES, stride=SUBLANES), :] = value[r:r + SUBLANES, k * LANES:(k + 1) * LANES]


def _scan_tile(xr, xi, hr, hi, coef_ref, lanes, reverse):
    for k, j in ((1, 0), (2, 2), (4, 4)):
        ar = coef_ref[j, :, lanes]
        ai = coef_ref[j + 1, :, lanes]
        shift = SUBLANES - k if reverse else k
        sr = pltpu.roll(xr, shift, 0)
        si = pltpu.roll(xi, shift, 0)
        xr, xi = xr + (ar * sr - ai * si), xi + (ar * si + ai * sr)
    pr = coef_ref[6, :, lanes]
    pi = coef_ref[7, :, lanes]
    return xr + (pr * hr - pi * hi), xi + (pr * hi + pi * hr)


def _scan_block(read, write, hr, hi, coef_ref, lanes, reverse):
    order = list(range(SUBLANES - 1, -1, -1) if reverse else range(SUBLANES))
    near = 8 + 2 * order[0]
    ar = coef_ref[near, :, lanes]
    ai = coef_ref[near + 1, :, lanes]
    xr, xi = read(order[0])
    local = {order[0]: (xr, xi)}
    for j in order[1:]:
        br, bi = read(j)
        xr, xi = br + (ar * xr - ai * xi), bi + (ar * xi + ai * xr)
        local[j] = (xr, xi)
    er, ei = _scan_tile(xr, xi, hr, hi, coef_ref, lanes, reverse)
    edge = lax.broadcasted_iota(jnp.int32, er.shape, 0) == (SUBLANES - 1 if reverse else 0)
    shift = SUBLANES - 1 if reverse else 1
    pr = jnp.where(edge, hr, pltpu.roll(er, shift, 0))
    pi = jnp.where(edge, hi, pltpu.roll(ei, shift, 0))
    for j in range(SUBLANES):
        cr = coef_ref[8 + 2 * j, :, lanes]
        ci = coef_ref[9 + 2 * j, :, lanes]
        xr, xi = local[j]
        write(j, xr + (cr * pr - ci * pi), xi + (cr * pi + ci * pr))
    end = 0 if reverse else SUBLANES - 1
    return jnp.broadcast_to(er[end:end + 1, :], er.shape), jnp.broadcast_to(ei[end:end + 1, :], ei.shape)


def _s5_fwd(u4, bmat, cmat, coef, dskip, exchanges=()):
    L = u4.shape[1]
    tm = _tile(L, TM_S5)
    lc = min(LANE_CHUNK, WB)

    def body(u_ref, bm_ref, cm_ref, coef_ref, d_ref, sre_ref, sim_ref, ys_ref, hr_ref, hi_ref):
        @pl.when(pl.program_id(1) == 0)
        def _():
            hr_ref[...] = jnp.zeros_like(hr_ref)
            hi_ref[...] = jnp.zeros_like(hi_ref)

        u = _load_permuted(u_ref)
        bu = _dot(u.astype(MXU_DTYPE), bm_ref[0])
        sre_ref[...] = bu[:, :WB]
        sim_ref[...] = bu[:, WB:]
        for c in range(WB // lc):
            lanes = slice(c * lc, (c + 1) * lc)
            hr, hi = hr_ref[:, lanes], hi_ref[:, lanes]
            for b in range(tm // SEG_ROWS):
                rows = lambda j, b=b: slice(b * SEG_ROWS + j * SUBLANES, b * SEG_ROWS + (j + 1) * SUBLANES)

                def read(j, rows=rows, lanes=lanes):
                    return sre_ref[rows(j), lanes], sim_ref[rows(j), lanes]

                def write(j, xr, xi, rows=rows, lanes=lanes):
                    sre_ref[rows(j), lanes] = xr
                    sim_ref[rows(j), lanes] = xi

                hr, hi = _scan_block(read, write, hr, hi, coef_ref, lanes, False)
            hr_ref[:, lanes] = hr
            hi_ref[:, lanes] = hi
        ys = _dot_nt(sre_ref[...].astype(MXU_DTYPE), cm_ref[0, :, :WB]) + _dot_nt(sim_ref[...].astype(MXU_DTYPE), cm_ref[0, :, WB:])
        _store_permuted(ys_ref, ys + d_ref[0] * u)

    return _call(
        body, name="s5_fwd", grid=(N_GBLK, L // tm), exchanges=exchanges, semantics=("arbitrary", "arbitrary"),
        operands=(u4, bmat, cmat, coef, dskip),
        in_specs=[
            _slab_spec(UB, tm, lambda b, i: (b, i, 0)),
            pl.BlockSpec((1, UB, 2 * WB), lambda b, i: (b, 0, 0)),
            pl.BlockSpec((1, UB, 2 * WB), lambda b, i: (b, 0, 0)),
            pl.BlockSpec((N_TABLES, SUBLANES, WB), lambda b, i: (0, 0, b)),
            pl.BlockSpec((1, 1, UB), lambda b, i: (b, 0, 0)),
        ],
        out_specs=[
            pl.BlockSpec((tm, WB), lambda b, i: (i, b)),
            pl.BlockSpec((tm, WB), lambda b, i: (i, b)),
            _slab_spec(UB, tm, lambda b, i: (b, i, 0)),
        ],
        out_shape=[
            jax.ShapeDtypeStruct((L, N_GBLK * WB), F32),
            jax.ShapeDtypeStruct((L, N_GBLK * WB), F32),
            _slab_shape(L, D_SSM),
        ],
        scratch_shapes=[pltpu.VMEM((SUBLANES, WB), F32), pltpu.VMEM((SUBLANES, WB), F32)],
    )


def _tail_fwd(x, ys, proj, w_glu, conv_w, g_ssm, g_conv, w_out, g_post, exchanges=()):
    L, D = x.shape
    tm = _tile(L, TM_TAIL)

    def body(x_ref, ys_ref, h_ref, bg_ref, cg_ref, wglu_ref, cw_ref, gs_ref, gc_ref, wout_ref, gp_ref,
             ycat_ref, o_ref, x1_ref, zbuf):
        @pl.when(pl.program_id(0) == 0)
        def _():
            zbuf[0:SUBLANES, :] = jnp.zeros((SUBLANES, D_CONV), F32)

        y1, _ = _gelu(_load_slabs(ys_ref))
        y2 = y1 * jax.nn.sigmoid(_dot(y1.astype(MXU_DTYPE), wglu_ref[...]))
        ycat_ref[:, :D_SSM] = _rms_fwd(y2, gs_ref[...]).astype(MXU_DTYPE)
        z = cg_ref[...] * h_ref[...]
        zbuf[SUBLANES:, :] = z
        conv = cw_ref[0:1, :] * zbuf[SUBLANES - 2:SUBLANES - 2 + tm, :] + cw_ref[1:2, :] * zbuf[SUBLANES - 1:SUBLANES - 1 + tm, :] + cw_ref[2:3, :] * z
        zbuf[0:SUBLANES, :] = zbuf[tm:tm + SUBLANES, :]
        ycat_ref[:, D_SSM:] = _rms_fwd(bg_ref[...] * conv, gc_ref[...]).astype(MXU_DTYPE)
        o = _dot(ycat_ref[...], wout_ref[...])
        o_ref[...] = o
        x1_ref[...] = x_ref[...] + _rms_fwd(o, gp_ref[...])

    row = lambda i: (i, 0)
    return _call(
        body, name="tail_fwd", grid=(L // tm,), exchanges=exchanges, semantics=("arbitrary",),
        operands=(x, ys, proj, proj, proj, w_glu, conv_w, g_ssm, g_conv, w_out, g_post),
        in_specs=[
            pl.BlockSpec((tm, D), row), _slab_spec(D_SSM, tm),
            pl.BlockSpec((tm, D_CONV), lambda i: (i, 1)), pl.BlockSpec((tm, D_CONV), lambda i: (i, 2)),
            pl.BlockSpec((tm, D_CONV), lambda i: (i, 3)),
            _resident(w_glu.shape), _resident(conv_w.shape), _resident(g_ssm.shape), _resident(g_conv.shape),
            _resident(w_out.shape), _resident(g_post.shape),
        ],
        out_specs=[pl.BlockSpec((tm, D), row), pl.BlockSpec((tm, D), row), pl.BlockSpec((tm, D), row)],
        out_shape=[jax.ShapeDtypeStruct((L, D), MXU_DTYPE), jax.ShapeDtypeStruct((L, D), F32), jax.ShapeDtypeStruct((L, D), F32)],
        scratch_shapes=[pltpu.VMEM((tm + SUBLANES, D_CONV), F32)],
    )


def _mlp_fwd(x1, target, w_up_all, w_down_a, w_down_b, g_pre, g_post):
    L, D = x1.shape
    ns, _, fc = w_up_all.shape
    half = w_down_a.shape[1]
    tm = _tile(L, TM_MLP)

    def body(x1_ref, t_ref, wup_ref, wda_ref, wdb_ref, gpre_ref, gpost_ref, hn2_ref, up_ref, m_ref, dx2_ref, loss_ref):
        @pl.when(pl.program_id(0) == 0)
        def _():
            loss_ref[...] = jnp.zeros_like(loss_ref)

        x1v = x1_ref[...]
        hn2 = _rms_fwd(x1v, gpre_ref[...]).astype(MXU_DTYPE)
        hn2_ref[...] = hn2
        m = jnp.zeros((tm, D), F32)
        for j in range(ns):
            up = _dot(hn2, wup_ref[j])
            up_ref[:, j * fc:(j + 1) * fc] = up
            act = jnp.square(jnp.maximum(up, 0.0)).astype(MXU_DTYPE)
            m = m + _dot(act[:, :half], wda_ref[j]) + _dot(act[:, half:], wdb_ref[j])
        m_ref[...] = m
        err = x1v + _rms_fwd(m, gpost_ref[...]) - t_ref[...]
        loss_ref[...] += 0.5 * jnp.sum(jnp.mean(err * err, axis=-1, keepdims=True))
        dx2_ref[...] = err * (1.0 / D)

    row = lambda i: (i, 0)
    return pl.pallas_call(
        body, name="mlp_fwd", grid=(L // tm,),
        in_specs=[pl.BlockSpec((tm, D), row), pl.BlockSpec((tm, D), row), _resident(w_up_all.shape), _resident(w_down_a.shape),
                  _resident(w_down_b.shape), _resident(g_pre.shape), _resident(g_post.shape)],
        out_specs=[pl.BlockSpec((tm, D), row), pl.BlockSpec((tm, ns * fc), row), pl.BlockSpec((tm, D), row),
                   pl.BlockSpec((tm, D), row), pl.BlockSpec((SUBLANES, 128), lambda i: (0, 0))],
        out_shape=[jax.ShapeDtypeStruct((L, D), MXU_DTYPE), jax.ShapeDtypeStruct((L, ns * fc), F32), jax.ShapeDtypeStruct((L, D), F32),
                   jax.ShapeDtypeStruct((L, D), F32), jax.ShapeDtypeStruct((SUBLANES, 128), F32)],
        compiler_params=_params(("arbitrary",)),
    )(x1, target, w_up_all, w_down_a, w_down_b, g_pre, g_post)


def _mlp_bwd(dx2, m, up, x1, w_up_all, w_down_a, w_down_b, g_pre, g_post):
    L, D = x1.shape
    ns, _, fc = w_up_all.shape
    tm = _tile(L, TM_MLP)

    def body(dx2_ref, m_ref, up_ref, x1_ref, wup_ref, wda_ref, wdb_ref, gpre_ref, gpost_ref,
             dm_ref, dup_ref, act_ref, dx1_ref, dgpost_ref, dgpre_ref):
        @pl.when(pl.program_id(0) == 0)
        def _():
            dgpost_ref[...] = jnp.zeros_like(dgpost_ref)
            dgpre_ref[...] = jnp.zeros_like(dgpre_ref)

        dx2v = dx2_ref[...]
        dm, dg = _rms_bwd(m_ref[...], gpost_ref[...], dx2v)
        dgpost_ref[...] += dg
        dm_b = dm.astype(MXU_DTYPE)
        dm_ref[...] = dm_b
        dhn2 = jnp.zeros((tm, D), F32)
        for j in range(ns):
            cols = slice(j * fc, (j + 1) * fc)
            relu = jnp.maximum(up_ref[:, cols], 0.0)
            act_ref[:, cols] = jnp.square(relu).astype(MXU_DTYPE)
            dact = jnp.concatenate([_dot_nt(dm_b, wda_ref[j]), _dot_nt(dm_b, wdb_ref[j])], axis=1)
            dup = (dact * (2.0 * relu)).astype(MXU_DTYPE)
            dup_ref[:, cols] = dup
            dhn2 = dhn2 + _dot_nt(dup, wup_ref[j])
        dx, dg = _rms_bwd(x1_ref[...], gpre_ref[...], dhn2)
        dgpre_ref[...] += dg
        dx1_ref[...] = dx2v + dx

    row = lambda i: (i, 0)
    vec = pl.BlockSpec((1, D), lambda i: (0, 0))
    return pl.pallas_call(
        body, name="mlp_bwd", grid=(L // tm,),
        in_specs=[pl.BlockSpec((tm, D), row), pl.BlockSpec((tm, D), row), pl.BlockSpec((tm, ns * fc), row), pl.BlockSpec((tm, D), row),
                  _resident(w_up_all.shape), _resident(w_down_a.shape), _resident(w_down_b.shape), _resident(g_pre.shape), _resident(g_post.shape)],
        out_specs=[pl.BlockSpec((tm, D), row), pl.BlockSpec((tm, ns * fc), row), pl.BlockSpec((tm, ns * fc), row),
                   pl.BlockSpec((tm, D), row), vec, vec],
        out_shape=[jax.ShapeDtypeStruct((L, D), MXU_DTYPE), jax.ShapeDtypeStruct((L, ns * fc), MXU_DTYPE),
                   jax.ShapeDtypeStruct((L, ns * fc), MXU_DTYPE), jax.ShapeDtypeStruct((L, D), F32),
                   jax.ShapeDtypeStruct((1, D), F32), jax.ShapeDtypeStruct((1, D), F32)],
        compiler_params=_params(("arbitrary",)),
    )(dx2, m, up, x1, w_up_all, w_down_a, w_down_b, g_pre, g_post)


def _tail_bwd(dx1, o, ys, proj, w_glu, conv_w, g_ssm, g_conv, w_out, g_post, exchanges=()):
    L, D = dx1.shape
    tm = _tile(L, TM_TAIL)
    nt = L // tm
    hb = tm // SUBLANES

    def body(dx1_ref, o_ref, ys_ref, h_ref, bg_ref, cg_ref, hh_ref, hcg_ref, wglu_ref, cw_ref, gs_ref, gc_ref, wout_ref, gp_ref,
             do_ref, da_ref, y1_ref, dys_ref, dhbc_ref, dgp_ref, dgs_ref, dgc_ref, dcw_ref, zbuf, dcbuf):
        step = pl.program_id(0)

        @pl.when(step == 0)
        def _():
            dcbuf[tm:, :] = jnp.zeros((SUBLANES, D_CONV), F32)
            dgp_ref[...] = jnp.zeros_like(dgp_ref)
            dgs_ref[...] = jnp.zeros_like(dgs_ref)
            dgc_ref[...] = jnp.zeros_like(dgc_ref)
            dcw_ref[...] = jnp.zeros_like(dcw_ref)

        do, dg = _rms_bwd(o_ref[...], gp_ref[...], dx1_ref[...])
        dgp_ref[...] += dg
        do_b = do.astype(MXU_DTYPE)
        do_ref[...] = do_b
        dycat = _dot_nt(do_b, wout_ref[...])
        y1, dgelu = _gelu(_load_slabs(ys_ref))
        y1_b = y1.astype(MXU_DTYPE)
        y1_ref[...] = y1_b
        s = jax.nn.sigmoid(_dot(y1_b, wglu_ref[...]))
        dy2, dg = _rms_bwd(y1 * s, gs_ref[...], dycat[:, :D_SSM])
        dgs_ref[...] += dg
        da_b = (dy2 * y1 * s * (1.0 - s)).astype(MXU_DTYPE)
        da_ref[...] = da_b
        _store_slabs(dys_ref, (dy2 * s + _dot_nt(da_b, wglu_ref[...])) * dgelu)
        h = h_ref[...]
        cg = cg_ref[...]
        bg = bg_ref[...]
        z = cg * h
        first = step == nt - 1
        zbuf[0:SUBLANES, :] = jnp.where(first, 0.0, hcg_ref[...] * hh_ref[...])
        zbuf[SUBLANES:, :] = z
        z1 = zbuf[SUBLANES - 1:SUBLANES - 1 + tm, :]
        z2 = zbuf[SUBLANES - 2:SUBLANES - 2 + tm, :]
        conv = cw_ref[0:1, :] * z2 + cw_ref[1:2, :] * z1 + cw_ref[2:3, :] * z
        dyc, dg = _rms_bwd(bg * conv, gc_ref[...], dycat[:, D_SSM:])
        dgc_ref[...] += dg
        dconv = dyc * bg
        dcw_ref[0:1, :] += jnp.sum(dconv * z2, axis=0, keepdims=True)
        dcw_ref[1:2, :] += jnp.sum(dconv * z1, axis=0, keepdims=True)
        dcw_ref[2:3, :] += jnp.sum(dconv * z, axis=0, keepdims=True)
        dcbuf[0:tm, :] = dconv
        dz = cw_ref[2:3, :] * dconv + cw_ref[1:2, :] * dcbuf[1:1 + tm, :] + cw_ref[0:1, :] * dcbuf[2:2 + tm, :]
        dcbuf[tm:, :] = dcbuf[0:SUBLANES, :]
        dhbc_ref[:, 0:D_CONV] = (dz * cg).astype(MXU_DTYPE)
        dhbc_ref[:, D_CONV:2 * D_CONV] = (dyc * conv).astype(MXU_DTYPE)
        dhbc_ref[:, 2 * D_CONV:] = (dz * h).astype(MXU_DTYPE)

    rev = lambda i: (nt - 1 - i, 0)
    slab = lambda i: (0, nt - 1 - i, 0)
    col = lambda c: (lambda i: (nt - 1 - i, c))
    halo = lambda c: (lambda i: (jnp.maximum((nt - 1 - i) * hb - 1, 0), c))
    vec = lambda n: pl.BlockSpec((1, n), lambda i: (0, 0))
    return _call(
        body, name="tail_bwd", grid=(nt,), exchanges=exchanges, semantics=("arbitrary",),
        operands=(dx1, o, ys, proj, proj, proj, proj, proj, w_glu, conv_w, g_ssm, g_conv, w_out, g_post),
        in_specs=[
            pl.BlockSpec((tm, D), rev), pl.BlockSpec((tm, D), rev), _slab_spec(D_SSM, tm, slab),
            pl.BlockSpec((tm, D_CONV), col(1)), pl.BlockSpec((tm, D_CONV), col(2)), pl.BlockSpec((tm, D_CONV), col(3)),
            pl.BlockSpec((SUBLANES, D_CONV), halo(1)), pl.BlockSpec((SUBLANES, D_CONV), halo(3)),
            _resident(w_glu.shape), _resident(conv_w.shape), _resident(g_ssm.shape), _resident(g_conv.shape),
            _resident(w_out.shape), _resident(g_post.shape),
        ],
        out_specs=[
            pl.BlockSpec((tm, D), rev), pl.BlockSpec((tm, D_SSM), rev), pl.BlockSpec((tm, D_SSM), rev), _slab_spec(D_SSM, tm, slab),
            pl.BlockSpec((tm, 3 * D_CONV), rev), vec(D), vec(D_SSM), vec(D_CONV),
            pl.BlockSpec((SUBLANES, D_CONV), lambda i: (0, 0)),
        ],
        out_shape=[
            jax.ShapeDtypeStruct((L, D), MXU_DTYPE), jax.ShapeDtypeStruct((L, D_SSM), MXU_DTYPE), jax.ShapeDtypeStruct((L, D_SSM), MXU_DTYPE),
            _slab_shape(L, D_SSM), jax.ShapeDtypeStruct((L, 3 * D_CONV), MXU_DTYPE),
            jax.ShapeDtypeStruct((1, D), F32), jax.ShapeDtypeStruct((1, D_SSM), F32), jax.ShapeDtypeStruct((1, D_CONV), F32),
            jax.ShapeDtypeStruct((SUBLANES, D_CONV), F32),
        ],
        scratch_shapes=[pltpu.VMEM((tm + SUBLANES, D_CONV), F32), pltpu.VMEM((tm + SUBLANES, D_CONV), F32)],
    )


def _s5_bwd(dys, u4, s_re, s_im, bmat, cmat, coef_rev, dskip, exchanges=()):
    L = dys.shape[1]
    tm = _tile(L, TM_S5)
    nt = L // tm
    lc = min(LANE_CHUNK, WB)

    def body(dys_ref, u_ref, sre_ref, sim_ref, bm_ref, cm_ref, coef_ref, d_ref,
             du_ref, gb_ref, gc_ref, q_ref, gd_ref, dr_ref, di_ref, lr_ref, li_ref, hr_ref, hi_ref, qr_acc, qi_acc, gb_acc, gc_acc):
        step = pl.program_id(1)

        @pl.when(step == 0)
        def _():
            for ref in (hr_ref, hi_ref, qr_acc, qi_acc, gb_acc, gc_acc, gd_ref):
                ref[...] = jnp.zeros_like(ref)

        dys_v = _load_permuted(dys_ref)
        u = _load_permuted(u_ref)
        dys_b = dys_v.astype(MXU_DTYPE)
        u_b = u.astype(MXU_DTYPE)
        d = _dot(dys_b, cm_ref[0])
        dr_ref[...] = d[:, :WB]
        di_ref[...] = d[:, WB:]
        for c in range(WB // lc):
            lanes = slice(c * lc, (c + 1) * lc)
            hr, hi = hr_ref[:, lanes], hi_ref[:, lanes]
            q = [qr_acc[:, lanes], qi_acc[:, lanes]]
            for b in range(tm // SEG_ROWS - 1, -1, -1):
                rows = lambda j, b=b: slice(b * SEG_ROWS + j * SUBLANES, b * SEG_ROWS + (j + 1) * SUBLANES)

                def read(j, rows=rows, lanes=lanes):
                    return dr_ref[rows(j), lanes], di_ref[rows(j), lanes]

                def write(j, xr, xi, rows=rows, lanes=lanes, q=q):
                    lr_ref[rows(j), lanes] = xr
                    li_ref[rows(j), lanes] = xi
                    er = xr - dr_ref[rows(j), lanes]
                    ei = xi - di_ref[rows(j), lanes]
                    sr = sre_ref[rows(j), lanes]
                    si = sim_ref[rows(j), lanes]
                    q[0] = q[0] + (er * sr + ei * si)
                    q[1] = q[1] + (ei * sr - er * si)

                hr, hi = _scan_block(read, write, hr, hi, coef_ref, lanes, True)
            hr_ref[:, lanes] = hr
            hi_ref[:, lanes] = hi
            qr_acc[:, lanes] = q[0]
            qi_acc[:, lanes] = q[1]
        lr_b = lr_ref[...].astype(MXU_DTYPE)
        li_b = li_ref[...].astype(MXU_DTYPE)
        _store_permuted(du_ref, _dot_nt(lr_b, bm_ref[0, :, :WB]) + _dot_nt(li_b, bm_ref[0, :, WB:]) + d_ref[0] * dys_v)
        gb_acc[:, :WB] += _dot_tn(u_b, lr_b)
        gb_acc[:, WB:] += _dot_tn(u_b, li_b)
        gc_acc[:, :WB] += _dot_tn(dys_b, sre_ref[...].astype(MXU_DTYPE))
        gc_acc[:, WB:] += _dot_tn(dys_b, sim_ref[...].astype(MXU_DTYPE))
        gd_ref[0] += jnp.sum(dys_v * u, axis=0, keepdims=True)

        @pl.when(step == nt - 1)
        def _():
            q_ref[0, 0:1, :] = jnp.sum(qr_acc[...], axis=0, keepdims=True)
            q_ref[0, 1:2, :] = jnp.sum(qi_acc[...], axis=0, keepdims=True)
            mask = _group_mask(UB, WB)
            fold = (lax.broadcasted_iota(jnp.int32, (WB, STATE), 0) % STATE == lax.broadcasted_iota(jnp.int32, (WB, STATE), 1)).astype(F32)
            for acc, out in ((gb_acc, gb_ref), (gc_acc, gc_ref)):
                for k in range(2):
                    own = jnp.where(mask, acc[:, k * WB:(k + 1) * WB], 0.0)
                    out[0, k] = jnp.dot(own, fold, precision=lax.Precision.HIGHEST, preferred_element_type=F32)

    rev = lambda b, i: (nt - 1 - i, b)
    slab = lambda b, i: (b, nt - 1 - i, 0)
    blk = lambda b, i: (b, 0, 0)
    blk4 = lambda b, i: (b, 0, 0, 0)
    return _call(
        body, name="s5_bwd", grid=(N_GBLK, nt), exchanges=exchanges, semantics=("arbitrary", "arbitrary"),
        operands=(dys, u4, s_re, s_im, bmat, cmat, coef_rev, dskip),
        in_specs=[
            _slab_spec(UB, tm, slab), _slab_spec(UB, tm, slab), pl.BlockSpec((tm, WB), rev), pl.BlockSpec((tm, WB), rev),
            pl.BlockSpec((1, UB, 2 * WB), blk), pl.BlockSpec((1, UB, 2 * WB), blk),
            pl.BlockSpec((N_TABLES, SUBLANES, WB), lambda b, i: (0, 0, b)), pl.BlockSpec((1, 1, UB), blk),
        ],
        out_specs=[
            _slab_spec(UB, tm, slab), pl.BlockSpec((1, 2, UB, STATE), blk4), pl.BlockSpec((1, 2, UB, STATE), blk4),
            pl.BlockSpec((1, 2, WB), blk), pl.BlockSpec((1, 1, UB), blk),
        ],
        out_shape=[
            _slab_shape(L, D_SSM), jax.ShapeDtypeStruct((N_GBLK, 2, UB, STATE), F32),
            jax.ShapeDtypeStruct((N_GBLK, 2, UB, STATE), F32), jax.ShapeDtypeStruct((N_GBLK, 2, WB), F32),
            jax.ShapeDtypeStruct((N_GBLK, 1, UB), F32),
        ],
        scratch_shapes=[pltpu.VMEM((tm, WB), F32)] * 4 + [pltpu.VMEM((SUBLANES, WB), F32)] * 4 + [pltpu.VMEM((UB, 2 * WB), F32)] * 2,
    )


def _inproj_bwd(du, dhbc, x, dx1, w_in_all, g1):
    L, D = x.shape
    ns, _, nc = w_in_all.shape
    tm = _tile(L, TM_PROJ)

    def body(du_ref, dhbc_ref, x_ref, dx1_ref, w_ref, g_ref, gx_ref, dproj_ref, dg_ref):
        @pl.when(pl.program_id(0) == 0)
        def _():
            dg_ref[...] = jnp.zeros_like(dg_ref)

        du_b = _load_slabs(du_ref).astype(MXU_DTYPE)
        dproj_ref[:, :nc] = du_b
        dproj_ref[:, nc:] = dhbc_ref[...]
        dhn = _dot_nt(du_b, w_ref[0])
        for j in range(1, ns):
            dhn = dhn + _dot_nt(dhbc_ref[:, (j - 1) * nc:j * nc], w_ref[j])
        dx, dg = _rms_bwd(x_ref[...], g_ref[...], dhn)
        dg_ref[...] += dg
        gx_ref[...] = dx1_ref[...] + dx

    row = lambda i: (i, 0)
    return pl.pallas_call(
        body, name="inproj_bwd", grid=(L // tm,),
        in_specs=[_slab_spec(nc, tm), pl.BlockSpec((tm, (ns - 1) * nc), row), pl.BlockSpec((tm, D), row), pl.BlockSpec((tm, D), row),
                  _resident(w_in_all.shape), _resident(g1.shape)],
        out_specs=[pl.BlockSpec((tm, D), row), pl.BlockSpec((tm, ns * nc), row), pl.BlockSpec((1, D), lambda i: (0, 0))],
        out_shape=[jax.ShapeDtypeStruct((L, D), F32), jax.ShapeDtypeStruct((L, ns * nc), MXU_DTYPE), jax.ShapeDtypeStruct((1, D), F32)],
        compiler_params=_params(("arbitrary",)),
    )(du, dhbc, x, dx1, w_in_all, g1)


def _matmul_tn(a, b, name, col_shards=1, exchanges=()):
    L, K = a.shape
    N = b.shape[1]
    tl = _tile(L, TL_TN)
    tk = _tile(K, 1024)
    nw = N // col_shards
    tn = _tile(nw, 1024)
    npb = nw // tn

    def body(a_ref, b_ref, o_ref):
        @pl.when(pl.program_id(2) == 0)
        def _():
            o_ref[...] = jnp.zeros_like(o_ref)

        o_ref[0] += _dot_tn(a_ref[...], b_ref[...])

    return _call(
        body, name=name, grid=(K // tk, N // tn, L // tl), exchanges=exchanges, semantics=("arbitrary", "arbitrary", "arbitrary"),
        operands=(a, b),
        in_specs=[pl.BlockSpec((tl, tk), lambda k, n, l: (l, k)), pl.BlockSpec((tl, tn), lambda k, n, l: (l, n))],
        out_specs=[pl.BlockSpec((1, tk, tn), lambda k, n, l: (n // npb, k, n % npb))],
        out_shape=[jax.ShapeDtypeStruct((col_shards, K, nw), F32)],
    )


def _ssm_discretize(lam_re, lam_im, log_dt, bt_re, bt_im):
    dt = jnp.exp(log_dt)[:, None]
    zr = lam_re * dt
    zi = lam_im * dt
    mag = jnp.exp(zr)
    abr = mag * jnp.cos(zi)
    abi = mag * jnp.sin(zi)
    nr, ni = abr - 1.0, abi
    den = lam_re * lam_re + lam_im * lam_im
    coef_r = ((nr * lam_re + ni * lam_im) / den)[:, None, :]
    coef_i = ((ni * lam_re - nr * lam_im) / den)[:, None, :]
    return zr, zi, coef_r * bt_re - coef_i * bt_im, coef_r * bt_im + coef_i * bt_re


def _scan_tables(ar, ai, reverse):
    rows = np.arange(SUBLANES)
    exps = np.zeros((N_TABLES // 2, SUBLANES), np.int32)
    keep = np.ones((N_TABLES // 2, SUBLANES), bool)
    for t, k in enumerate((1, 2, 4)):
        exps[t] = SUBLANES * k
        keep[t] = (rows + k <= SUBLANES - 1) if reverse else (rows >= k)
    exps[3] = SUBLANES * (SUBLANES - rows) if reverse else SUBLANES * (rows + 1)
    for j in range(SUBLANES):
        exps[4 + j] = SUBLANES - j if reverse else j + 1
    pr, pi = ar, (-ai if reverse else ai)
    shape = (N_TABLES // 2, SUBLANES, ar.shape[0])
    xr, xi = jnp.ones(shape, F32), jnp.zeros(shape, F32)
    for bit in range(int(exps.max()).bit_length()):
        on = ((exps >> bit) & 1).astype(bool)[:, :, None]
        xr, xi = jnp.where(on, xr * pr - xi * pi, xr), jnp.where(on, xr * pi + xi * pr, xi)
        pr, pi = pr * pr - pi * pi, 2.0 * pr * pi
    xr = jnp.where(keep[:, :, None], xr, 0.0)
    xi = jnp.where(keep[:, :, None], xi, 0.0)
    return jnp.stack([xr, xi], axis=1).reshape(N_TABLES, SUBLANES, ar.shape[0])


def _group_mask(rows, cols):
    r = lax.broadcasted_iota(jnp.int32, (rows, cols), 0) // GROUP
    c = lax.broadcasted_iota(jnp.int32, (rows, cols), 1) // STATE
    return r == c


def _ssm_expand(bt_re, bt_im, c_re, c_im):
    flat = lambda a: a.reshape(N_GROUPS * GROUP, STATE)

    def body(br_ref, bi_ref, cr_ref, ci_ref, bm_ref, cm_ref):
        spread = (lax.broadcasted_iota(jnp.int32, (STATE, WB), 1) % STATE == lax.broadcasted_iota(jnp.int32, (STATE, WB), 0)).astype(F32)
        mask = _group_mask(UB, WB)

        def expand(x):
            wide = jnp.dot(x, spread, precision=lax.Precision.HIGHEST, preferred_element_type=F32)
            return jnp.where(mask, wide, 0.0).astype(MXU_DTYPE)

        bm_ref[0, :, :WB] = expand(br_ref[...])
        bm_ref[0, :, WB:] = expand(bi_ref[...])
        cm_ref[0, :, :WB] = expand(cr_ref[...])
        cm_ref[0, :, WB:] = expand(-ci_ref[...])

    spec = pl.BlockSpec((UB, STATE), lambda b: (b, 0))
    out = pl.BlockSpec((1, UB, 2 * WB), lambda b: (b, 0, 0))
    return pl.pallas_call(
        body, name="ssm_expand", grid=(N_GBLK,), in_specs=[spec] * 4, out_specs=[out, out],
        out_shape=[jax.ShapeDtypeStruct((N_GBLK, UB, 2 * WB), MXU_DTYPE)] * 2,
        compiler_params=_params(("arbitrary",)),
    )(flat(bt_re), flat(bt_im), flat(c_re), flat(c_im))


def _ssm_matrices(lam_re, lam_im, log_dt, bt_re, bt_im, c_re, c_im):
    zr, zi, bbar_r, bbar_i = _ssm_discretize(lam_re, lam_im, log_dt, bt_re, bt_im)
    mag = jnp.exp(zr)
    ar = (mag * jnp.cos(zi)).reshape(-1)
    ai = (mag * jnp.sin(zi)).reshape(-1)
    bmat, cmat = _ssm_expand(bbar_r, bbar_i, c_re, c_im)
    return bmat, cmat, _scan_tables(ar, ai, False), _scan_tables(ar, ai, True)


def _ssm_param_grads(lam_re, lam_im, log_dt, bt_re, bt_im, gb, gc, q, gd):
    part = lambda g, k: g[:, k].reshape(N_GROUPS, GROUP, STATE)
    qr = q[:, 0, :].reshape(N_GROUPS, STATE)
    qi = q[:, 1, :].reshape(N_GROUPS, STATE)
    _, vjp = jax.vjp(_ssm_discretize, lam_re, lam_im, log_dt, bt_re, bt_im)
    d_lam_re, d_lam_im, d_log_dt, d_bt_re, d_bt_im = vjp((qr, qi, part(gb, 0), part(gb, 1)))
    return d_lam_re, d_lam_im, d_log_dt, d_bt_re, d_bt_im, part(gc, 0), -part(gc, 1), gd.reshape(N_GROUPS, GROUP)


def _row_tile(rows, n):
    return _tile(rows, max(SUBLANES, (2 * 1024 * 1024) // (4 * n)))


def _pair_add(grad, other, core, name):
    ns, h, n = other.shape
    tr = _row_tile(h, n)
    nb = h // tr

    def body(c_ref, g_ref, o_ref, out_ref):
        out_ref[...] = (g_ref[...] + o_ref[...]).astype(WIRE_DTYPE)

    return pl.pallas_call(
        body, name=name,
        grid_spec=pltpu.PrefetchScalarGridSpec(
            num_scalar_prefetch=1, grid=(ns, nb),
            in_specs=[pl.BlockSpec((1, tr, n), lambda s, i, c: (s, c[0] * nb + i, 0)), pl.BlockSpec((1, tr, n), lambda s, i, c: (s, i, 0))],
            out_specs=pl.BlockSpec((1, tr, n), lambda s, i, c: (s, i, 0))),
        out_shape=jax.ShapeDtypeStruct(other.shape, WIRE_DTYPE),
        compiler_params=_params(("arbitrary", "arbitrary")),
    )(core, grad, other)


def _quad_sum(parts, core, name):
    ns, h, n = parts.shape
    tr = _row_tile(h, n)
    nb = h // tr

    def body(c_ref, p_ref, out_ref):
        p = [p_ref[k].astype(F32) for k in range(ns)]
        out_ref[...] = ((p[0] + p[1]) + p[2]) + p[3]

    return pl.pallas_call(
        body, name=name,
        grid_spec=pltpu.PrefetchScalarGridSpec(
            num_scalar_prefetch=1, grid=(nb,),
            in_specs=[pl.BlockSpec((ns, tr, n), lambda i, c: (0, i, 0))],
            out_specs=pl.BlockSpec((tr, n), lambda i, c: (c[0] * nb + i, 0))),
        out_shape=jax.ShapeDtypeStruct((2 * h, n), F32),
        compiler_params=_params(("arbitrary",)),
    )(core, parts)


def _adamw_math(w, g, m, v):
    m = ADAM_B1 * m + (1.0 - ADAM_B1) * g
    v = ADAM_B2 * v + (1.0 - ADAM_B2) * jnp.square(g)
    m_hat = m / (1.0 - ADAM_B1 ** ADAM_STEP)
    v_hat = v / (1.0 - ADAM_B2 ** ADAM_STEP)
    delta = -ADAM_LR * (m_hat / (jnp.sqrt(v_hat) + ADAM_EPS) + ADAM_WD * w)
    return delta, m, v


def _adamw(w, g, m, v, name):
    r, n = w.shape
    tr = _row_tile(r, n)

    def body(w_ref, g_ref, m_ref, v_ref, d_ref, nm_ref, nv_ref):
        d_ref[...], nm_ref[...], nv_ref[...] = _adamw_math(w_ref[...], g_ref[...], m_ref[...], v_ref[...])

    spec = pl.BlockSpec((tr, n), lambda i: (i, 0))
    return pl.pallas_call(
        body, name=name, grid=(r // tr,), in_specs=[spec] * 4, out_specs=[spec] * 3,
        out_shape=[jax.ShapeDtypeStruct((r, n), F32)] * 3,
        compiler_params=_params(("arbitrary",)),
    )(w, g, m, v)


LANES = 128
SMALL = ["g_pre_mix", "lam_re", "lam_im", "log_dt", "b_re", "b_im", "c_re", "c_im", "d_skip", "conv_w", "g_ssm_out", "g_conv_out",
         "g_post_mix", "g_pre_mlp", "g_post_mlp"]
TILE_SLOTS = {"b_re": (0, N_GROUPS), "b_im": (N_GROUPS, N_GROUPS), "c_re": (2 * N_GROUPS, N_GROUPS), "c_im": (3 * N_GROUPS, N_GROUPS),
              "lam_re": (4 * N_GROUPS, 2), "lam_im": (4 * N_GROUPS + 2, 2)}
N_TILE_SLOTS = 4 * N_GROUPS + 4
VEC_ROWS = {"g_pre_mix": 0, "g_post_mix": 1, "g_pre_mlp": 2, "g_post_mlp": 3, "g_ssm_out": 4, "g_conv_out": 5, "log_dt": 6}
ROW_LOSS, ROW_DSKIP, ROW_CONV, N_PACK_ROWS = 7, 8, 24, 32


def _kernel_form(name, a):
    if name in ("b_re", "b_im"):
        return jnp.transpose(a, (0, 1, 3, 2)).reshape(N_GROUPS, GROUP, STATE)
    if name in ("c_re", "c_im"):
        return a.reshape(N_GROUPS, GROUP, STATE)
    if name in ("lam_re", "lam_im"):
        return a.reshape(2, GROUP, STATE)
    if name == "d_skip":
        return jnp.transpose(a, (0, 2, 1)).reshape(GROUP, N_GROUPS)
    if name == "conv_w":
        return jnp.transpose(a, (1, 0, 2))
    return a


def _param_form(name, k):
    if name in ("b_re", "b_im"):
        return jnp.transpose(k.reshape(1, N_GROUPS, GROUP, STATE), (0, 1, 3, 2))
    if name in ("c_re", "c_im"):
        return k.reshape(1, N_GROUPS, GROUP, STATE)
    if name in ("lam_re", "lam_im"):
        return k.reshape(1, N_GROUPS, STATE)
    if name == "d_skip":
        return jnp.transpose(k.reshape(1, GROUP, N_GROUPS), (0, 2, 1))
    if name == "conv_w":
        return jnp.transpose(k, (1, 0, 2))
    return k


def _pack_tiles(g):
    lam = lambda a: a.reshape(2, GROUP, STATE)
    return jnp.concatenate([g["b_re"], g["b_im"], g["c_re"], g["c_im"], lam(g["lam_re"]), lam(g["lam_im"])], axis=0)


def _pack_rows(g, loss):
    row = lambda a: jnp.pad(a, ((0, 0), (0, D_MODEL - a.shape[1])))
    rows = [row(g[k][None]) for k in VEC_ROWS] + [row(loss[0:1]), row(g["d_skip"].T), row(g["conv_w"])]
    rows.append(jnp.zeros((N_PACK_ROWS - ROW_CONV - 3, D_MODEL), F32))
    return jnp.concatenate(rows, axis=0)


def _adamw_small(tiles, rows, w, m, v):
    nn = len(SMALL)

    def body(*refs):
        t_ref, r_ref = refs[0], refs[1]
        w_refs, m_refs, v_refs = refs[2:2 + nn], refs[2 + nn:2 + 2 * nn], refs[2 + 2 * nn:2 + 3 * nn]
        loss_ref, outs = refs[2 + 3 * nn], refs[3 + 3 * nn:]

        def tile_sum(first, count):
            total = t_ref[0, first:first + count]
            for d in range(1, N_DEV):
                total = total + t_ref[d, first:first + count]
            return total

        def row_sum(first, count, lanes):
            total = r_ref[0, first:first + count, 0:lanes]
            for d in range(1, N_DEV):
                total = total + r_ref[d, first:first + count, 0:lanes]
            return total

        def step(j, g, at=lambda ref: ref):
            delta, nm, nv = _adamw_math(at(w_refs[j])[...], g, at(m_refs[j])[...], at(v_refs[j])[...])
            at(outs[j])[...] = g
            at(outs[nn + j])[...] = delta
            at(outs[2 * nn + j])[...] = nm
            at(outs[3 * nn + j])[...] = nv

        loss_ref[...] = row_sum(ROW_LOSS, 1, LANES)
        chip = 2 * lax.axis_index("x") + lax.axis_index("y")
        for j, name in enumerate(SMALL):
            if name in TILE_SLOTS:
                step(j, tile_sum(*TILE_SLOTS[name]))
            elif name == "d_skip":
                step(j, row_sum(ROW_DSKIP, GROUP, N_GROUPS))
            elif name == "conv_w":
                full = row_sum(ROW_CONV, 3, D_CONV)
                mine = full[:, 0:LANES]
                for s in range(1, N_CHIPS):
                    mine = jnp.where(chip == s, full[:, s * LANES:(s + 1) * LANES], mine)
                for k in range(3):
                    step(j, mine[k:k + 1, :], at=lambda ref, k=k: ref.at[k])
            else:
                step(j, row_sum(VEC_ROWS[name], 1, w_refs[j].shape[1]))

    args = [tiles, rows] + [w[k] for k in SMALL] + [m[k] for k in SMALL] + [v[k] for k in SMALL]
    res = pl.pallas_call(
        body, name="adamw_small", in_specs=[VMEM] * len(args), out_specs=[VMEM] * (1 + 4 * nn),
        out_shape=[jax.ShapeDtypeStruct((1, LANES), F32)] + [jax.ShapeDtypeStruct(w[k].shape, F32) for k in SMALL] * 4,
        compiler_params=pltpu.CompilerParams(vmem_limit_bytes=VMEM_LIMIT),
    )(*args)
    return res[0], [dict(zip(SMALL, res[1 + q * nn:1 + (q + 1) * nn])) for q in range(4)]


WEIGHTS = ["g_pre_mix", "w_in", "lam_re", "lam_im", "log_dt", "b_re", "b_im", "c_re", "c_im", "d_skip", "w_glu", "conv_w",
           "g_ssm_out", "g_conv_out", "w_out", "g_post_mix", "g_pre_mlp", "w_up", "w_down", "g_post_mlp"]
BIG = ["w_in", "w_glu", "w_out", "w_up", "w_down"]


def kernel(x, g_pre_mix, w_in, lam_re, lam_im, log_dt, b_re, b_im, c_re, c_im, d_skip, w_glu, conv_w, g_ssm_out, g_conv_out, w_out, g_post_mix, g_pre_mlp, w_up, w_down, g_post_mlp, loss_target, m_g_pre_mix, m_w_in, m_lam_re, m_lam_im, m_log_dt, m_b_re, m_b_im, m_c_re, m_c_im, m_d_skip, m_w_glu, m_conv_w, m_g_ssm_out, m_g_conv_out, m_w_out, m_g_post_mix, m_g_pre_mlp, m_w_up, m_w_down, m_g_post_mlp, v_g_pre_mix, v_w_in, v_lam_re, v_lam_im, v_log_dt, v_b_re, v_b_im, v_c_re, v_c_im, v_d_skip, v_w_glu, v_conv_w, v_g_ssm_out, v_g_conv_out, v_w_out, v_g_post_mix, v_g_pre_mlp, v_w_up, v_w_down, v_g_post_mlp):
    w = dict(g_pre_mix=g_pre_mix, w_in=w_in, lam_re=lam_re, lam_im=lam_im, log_dt=log_dt, b_re=b_re, b_im=b_im, c_re=c_re, c_im=c_im,
             d_skip=d_skip, w_glu=w_glu, conv_w=conv_w, g_ssm_out=g_ssm_out, g_conv_out=g_conv_out, w_out=w_out, g_post_mix=g_post_mix,
             g_pre_mlp=g_pre_mlp, w_up=w_up, w_down=w_down, g_post_mlp=g_post_mlp)
    m = dict(g_pre_mix=m_g_pre_mix, w_in=m_w_in, lam_re=m_lam_re, lam_im=m_lam_im, log_dt=m_log_dt, b_re=m_b_re, b_im=m_b_im, c_re=m_c_re,
             c_im=m_c_im, d_skip=m_d_skip, w_glu=m_w_glu, conv_w=m_conv_w, g_ssm_out=m_g_ssm_out, g_conv_out=m_g_conv_out, w_out=m_w_out,
             g_post_mix=m_g_post_mix, g_pre_mlp=m_g_pre_mlp, w_up=m_w_up, w_down=m_w_down, g_post_mlp=m_g_post_mlp)
    v = dict(g_pre_mix=v_g_pre_mix, w_in=v_w_in, lam_re=v_lam_re, lam_im=v_lam_im, log_dt=v_log_dt, b_re=v_b_re, b_im=v_b_im, c_re=v_c_re,
             c_im=v_c_im, d_skip=v_d_skip, w_glu=v_w_glu, conv_w=v_conv_w, g_ssm_out=v_g_ssm_out, g_conv_out=v_g_conv_out, w_out=v_w_out,
             g_post_mix=v_g_post_mix, g_pre_mlp=v_g_pre_mlp, w_up=v_w_up, w_down=v_w_down, g_post_mlp=v_g_post_mlp)
    w_dev, m_dev, v_dev = w, m, v
    w, m, v = ({k: a[0] for k, a in d.items()} for d in (w, m, v))
    core = lax.axis_index("c").astype(jnp.int32).reshape(1)

    xs, target = x[0], loss_target[0]
    g1 = w["g_pre_mix"][None]
    g_ssm, g_conv = w["g_ssm_out"][None], w["g_conv_out"][None]
    g_post_mix, g_pre_mlp, g_post_mlp = w["g_post_mix"][None], w["g_pre_mlp"][None], w["g_post_mlp"][None]
    bt_re, bt_im = (jnp.transpose(w[k], (0, 2, 1)) for k in ("b_re", "b_im"))
    bmat, cmat, coef_f, coef_r = _ssm_matrices(w["lam_re"], w["lam_im"], w["log_dt"], bt_re, bt_im, w["c_re"], w["c_im"])
    dskip = w["d_skip"].reshape(N_GBLK, 1, UB)
    shard = {k: w[k].astype(MXU_DTYPE) for k in BIG}
    conv_pad = jnp.pad(w["conv_w"], ((0, SUBLANES - 3), (0, 0)))

    (w_in_all,) = _run_exchanges([_GatherForward([shard["w_in"]])], "ag_w_in")
    hn, proj, u4, w_glu_all, w_out_all, conv_all = _inproj_fwd(
        xs, g1, w_in_all, exchanges=[_Gather([shard["w_glu"], shard["w_out"], conv_pad], [False, False, False])])
    wd_half = shard["w_down"].shape[0] // 2
    wda, wdb = shard["w_down"][:wd_half], shard["w_down"][wd_half:]
    s_re, s_im, ys, w_up_all, wda_all = _s5_fwd(u4, bmat, cmat, coef_f, dskip, exchanges=[_Gather([shard["w_up"], wda], [True, True])])
    w_glu_f, w_out_f = w_glu_all.reshape(D_SSM, D_SSM), w_out_all.reshape(D_MODEL, D_MODEL)
    conv_f = jnp.transpose(conv_all, (1, 0, 2)).reshape(SUBLANES, D_CONV)
    ycat, o, x1, w_up_all, wda_all, wdb_all = _tail_fwd(xs, ys, proj, w_glu_f, conv_f, g_ssm, g_conv, w_out_f, g_post_mix,
                                                        exchanges=[_Forward([w_up_all, wda_all]), _GatherForward([wdb])])
    hn2, up, m_act, dx2, loss = _mlp_fwd(x1, target, w_up_all, wda_all, wdb_all, g_pre_mlp, g_post_mlp)

    dm, dup, act, dx1, dg_post_mlp, dg_pre_mlp = _mlp_bwd(dx2, m_act, up, x1, w_up_all, wda_all, wdb_all, g_pre_mlp, g_post_mlp)
    gw_down = _matmul_tn(act, dm, "dw_down")[0].reshape(N_CHIPS, D_FF // N_CHIPS, D_MODEL)
    gw_up = _matmul_tn(hn2, dup, "dw_up", col_shards=N_CHIPS)[0]
    do, da, y1, dys, dhbc, dg_post_mix, dg_ssm, dg_conv, dconv_w, o_down, o_up = _tail_bwd(
        dx1, o, ys, proj, w_glu_f, conv_f, g_ssm, g_conv, w_out_f, g_post_mix, exchanges=[_Pair([gw_down, gw_up])])
    p_down = _pair_add(gw_down, o_down, core, "pair_add_w_down")
    p_up = _pair_add(gw_up, o_up, core, "pair_add_w_up")
    gw_out = _matmul_tn(ycat, do, "dw_out")[0].reshape(N_CHIPS, D_MODEL // N_CHIPS, D_MODEL)
    gw_glu = _matmul_tn(y1, da, "dw_glu")[0].reshape(N_CHIPS, D_SSM // N_CHIPS, D_SSM)
    du, gb, gc, q, gd, q_down, q_up, o_out, o_glu = _s5_bwd(
        dys, u4, s_re, s_im, bmat, cmat, coef_r, dskip, exchanges=[_Chip([p_down, p_up]), _Pair([gw_out, gw_glu])])
    h_down = _quad_sum(q_down, core, "quad_sum_w_down")
    h_up = _quad_sum(q_up, core, "quad_sum_w_up")
    p_out = _pair_add(gw_out, o_out, core, "pair_add_w_out")
    p_glu = _pair_add(gw_glu, o_glu, core, "pair_add_w_glu")
    grad_x, dproj, dg_pre_mix = _inproj_bwd(du, dhbc, xs, dx1, w_in_all, g1)
    d_lam_re, d_lam_im, d_log_dt, d_b_re, d_b_im, d_c_re, d_c_im, d_d_skip = _ssm_param_grads(
        w["lam_re"], w["lam_im"], w["log_dt"], bt_re, bt_im, gb, gc, q, gd)
    small = {
        "g_pre_mix": dg_pre_mix[0], "lam_re": d_lam_re, "lam_im": d_lam_im, "log_dt": d_log_dt, "b_re": d_b_re, "b_im": d_b_im,
        "c_re": d_c_re, "c_im": d_c_im, "d_skip": d_d_skip, "conv_w": dconv_w[:3], "g_ssm_out": dg_ssm[0], "g_conv_out": dg_conv[0],
        "g_post_mix": dg_post_mix[0], "g_pre_mlp": dg_pre_mlp[0], "g_post_mlp": dg_post_mlp[0],
    }
    gw_in, g_down, g_up, q_out, q_glu, tiles, rows = _matmul_tn(
        hn, dproj, "dw_in", col_shards=N_CHIPS,
        exchanges=[_Share([h_down, h_up]), _Chip([p_out, p_glu]), _GatherSmall(_pack_tiles(small)), _GatherSmall(_pack_rows(small, loss))])
    h_out = _quad_sum(q_out, core, "quad_sum_w_out")
    h_glu = _quad_sum(q_glu, core, "quad_sum_w_glu")
    (o_in,) = _run_exchanges([_Pair([gw_in])], "rs_pair_w_in")
    p_in = _pair_add(gw_in, o_in, core, "pair_add_w_in")
    q_in, g_out, g_glu = _run_exchanges([_Chip([p_in]), _Share([h_out, h_glu])], "rs_chip_w_in")
    h_in = _quad_sum(q_in, core, "quad_sum_w_in")
    (g_in,) = _run_exchanges([_Share([h_in])], "rs_share_w_in")
    shard_grads = {"w_in": g_in, "w_glu": g_glu, "w_out": g_out, "w_up": g_up, "w_down": g_down}

    out = {q: {} for q in ("grad", "delta", "new_m", "new_v")}
    for k in BIG:
        out["grad"][k] = shard_grads[k][None]
        delta, new_m, new_v = _adamw(w[k], shard_grads[k], m[k], v[k], "adamw_" + k)
        out["delta"][k], out["new_m"][k], out["new_v"][k] = delta[None], new_m[None], new_v[None]
    form = lambda d: {k: _kernel_form(k, d[k]) for k in SMALL}
    loss, res = _adamw_small(tiles, rows, form(w_dev), form(m_dev), form(v_dev))
    for q, d in zip(("grad", "delta", "new_m", "new_v"), res):
        out[q].update({k: _param_form(k, d[k]) for k in SMALL})
    flat = [loss[0, 0], grad_x[None]]
    for q in ("grad", "delta", "new_m", "new_v"):
        flat += [out[q][k] for k in WEIGHTS]
    return tuple(flat)
```

```python
import functools
import math

import jax
import jax.numpy as jnp
import numpy as np
from jax import lax
from jax.experimental import pallas as pl
from jax.experimental.pallas import tpu as pltpu

F32 = jnp.float32
MXU_DTYPE = jnp.bfloat16
WIRE_DTYPE = jnp.bfloat16

D_MODEL = 1024
D_SSM = 512
D_CONV = 512
N_GROUPS = 32
GROUP = 16
STATE = 64
D_FF = 4096
RMS_EPS = 1e-6
N_CHIPS = 4
N_DEV = 8

ADAM_LR = 0.001
ADAM_B1 = 0.9
ADAM_B2 = 0.999
ADAM_EPS = 1e-08
ADAM_WD = 0.01
ADAM_STEP = 10

N_GBLK = 2
G_PER_BLK = N_GROUPS // N_GBLK
UB = G_PER_BLK * GROUP
WB = G_PER_BLK * STATE
LANE_CHUNK = 256
SUBLANES = 8
N_TABLES = 24

TM_PROJ = 512
TM_S5 = 512
TM_TAIL = 256
TM_MLP = 256
TL_TN = 2048
VMEM_LIMIT = 56 * 1024 * 1024

MESH = pl.DeviceIdType.MESH


def _params(sem, vmem=VMEM_LIMIT):
    return pltpu.CompilerParams(dimension_semantics=sem, vmem_limit_bytes=vmem)


def _resident(shape):
    nd = len(shape)
    return pl.BlockSpec(shape, lambda *_: (0,) * nd, pipeline_mode=pl.Buffered(1))


def _dot(a, b):
    return jnp.dot(a, b, preferred_element_type=F32)


def _dot_nt(a, b):
    return lax.dot_general(a, b, (((1,), (1,)), ((), ())), preferred_element_type=F32)


def _dot_tn(a, b):
    return lax.dot_general(a, b, (((0,), (0,)), ((), ())), preferred_element_type=F32)


def _rms_fwd(x, g):
    r = lax.rsqrt(jnp.mean(x * x, axis=-1, keepdims=True) + RMS_EPS)
    return x * r * g


def _rms_bwd(x, g, dy):
    r = lax.rsqrt(jnp.mean(x * x, axis=-1, keepdims=True) + RMS_EPS)
    xn = x * r
    q = dy * g
    dx = r * (q - xn * jnp.mean(q * xn, axis=-1, keepdims=True))
    return dx, jnp.sum(dy * xn, axis=0, keepdims=True)


_GELU_C = math.sqrt(2.0 / math.pi)


def _gelu(x):
    t = jnp.tanh(_GELU_C * (x + 0.044715 * (x * x * x)))
    y = x * (0.5 * (1.0 + t))
    dy = 0.5 * (1.0 + t) + 0.5 * x * (1.0 - t * t) * (_GELU_C * (1.0 + 3 * 0.044715 * (x * x)))
    return y, dy


def _tile(n, pref):
    t = min(n, pref)
    assert n % t == 0, (n, t)
    return t


HBM = pl.BlockSpec(memory_space=pltpu.HBM)
VMEM = pl.BlockSpec(memory_space=pltpu.VMEM)
DMA_SEMS = pltpu.SemaphoreType.DMA


def _place():
    x, y, c = lax.axis_index("x"), lax.axis_index("y"), lax.axis_index("c")
    chips = [(1 - x, y), (x, 1 - y), (1 - x, 1 - y)]
    return (x, y, c), 2 * x + y, (x, y, 1 - c), chips, [2 * px + py for px, py in chips]


def _remote(src, dst, send_sem, recv_sem, device):
    return pltpu.make_async_remote_copy(src_ref=src, dst_ref=dst, send_sem=send_sem, recv_sem=recv_sem,
                                        device_id=device, device_id_type=MESH)


def _half(rows, c):
    return pl.ds(c * (rows // 2), rows // 2)


class _Exchange:
    aliases = {}

    def start(self, ins, outs, sems):
        local, outgoing, _ = self._copies(ins, outs, sems)
        for cp in local + outgoing:
            cp.start()

    def finish(self, ins, outs, sems):
        local, outgoing, incoming = self._copies(ins, outs, sems)
        for cp in incoming:
            cp.wait_recv()
        for cp in outgoing:
            cp.wait_send()
        for cp in local:
            cp.wait()


class _Gather(_Exchange):
    def __init__(self, shards, split):
        self.inputs, self.split = list(shards), split
        self.out_shape = [jax.ShapeDtypeStruct((N_CHIPS, *a.shape), a.dtype) for a in shards]
        self.sems = [DMA_SEMS((len(shards), 3)), DMA_SEMS((len(shards), 3)), DMA_SEMS((len(shards),))]

    def _copies(self, ins, outs, sems):
        send, recv, lsem = sems
        (x, y, c), me, sibling, chips, ids = _place()
        local = [pltpu.make_async_copy(ins[t], outs[t].at[me], lsem.at[t]) for t in range(len(ins))]
        outgoing, incoming = [], []
        for t, a in enumerate(self.inputs):
            rows = _half(a.shape[0], c) if self.split[t] else pl.ds(0, a.shape[0])
            for k in range(3):
                to = (*chips[k], c)
                outgoing.append(_remote(ins[t].at[rows, :], outs[t].at[me, rows, :], send.at[t, k], recv.at[t, k], to))
                incoming.append(_remote(ins[t].at[rows, :], outs[t].at[ids[k], rows, :], send.at[t, k], recv.at[t, k], to))
        return local, outgoing, incoming


class _Forward(_Exchange):
    def __init__(self, arrays):
        self.inputs = list(arrays)
        self.out_shape = [jax.ShapeDtypeStruct(a.shape, a.dtype) for a in arrays]
        self.aliases = {t: t for t in range(len(arrays))}
        self.sems = [DMA_SEMS((len(arrays), 3)), DMA_SEMS((len(arrays), 3))]

    def _copies(self, ins, outs, sems):
        send, recv = sems
        (x, y, c), me, sibling, chips, ids = _place()
        outgoing, incoming = [], []
        for t, a in enumerate(self.inputs):
            for k in range(3):
                mine = outs[t].at[ids[k], _half(a.shape[1], c), :]
                theirs = outs[t].at[ids[k], _half(a.shape[1], 1 - c), :]
                outgoing.append(_remote(mine, mine, send.at[t, k], recv.at[t, k], sibling))
                incoming.append(_remote(theirs, theirs, send.at[t, k], recv.at[t, k], sibling))
        return [], outgoing, incoming


class _GatherForward(_Exchange):
    def __init__(self, shards):
        self.gather = _Gather(shards, [True] * len(shards))
        self.forward = _Forward(self.gather.out_shape)
        self.inputs, self.out_shape = self.gather.inputs, self.gather.out_shape
        self.sems = self.gather.sems + self.forward.sems

    def start(self, ins, outs, sems):
        self.gather.start(ins, outs, sems[:3])

    def finish(self, ins, outs, sems):
        local, outgoing, incoming = self.gather._copies(ins, outs, sems[:3])
        _, passed, from_sibling = self.forward._copies(outs, outs, sems[3:])
        for landed, onward in zip(incoming, passed):
            landed.wait_recv()
            onward.start()
        for cp in from_sibling:
            cp.wait_recv()
        for cp in outgoing + passed:
            cp.wait_send()
        for cp in local:
            cp.wait()


class _Pair(_Exchange):
    def __init__(self, grads):
        self.inputs = list(grads)
        self.out_shape = [jax.ShapeDtypeStruct((g.shape[0], g.shape[1] // 2, g.shape[2]), g.dtype) for g in grads]
        self.sems = [DMA_SEMS((len(grads),)), DMA_SEMS((len(grads),))]

    def _copies(self, ins, outs, sems):
        send, recv = sems
        (x, y, c), me, sibling, chips, ids = _place()
        cps = [_remote(ins[t].at[:, _half(g.shape[1], 1 - c), :], outs[t], send.at[t], recv.at[t], sibling)
               for t, g in enumerate(self.inputs)]
        return [], cps, cps


class _Chip(_Exchange):
    def __init__(self, parts):
        self.inputs = list(parts)
        self.out_shape = [jax.ShapeDtypeStruct(p.shape, p.dtype) for p in parts]
        self.sems = [DMA_SEMS((len(parts), 3)), DMA_SEMS((len(parts), 3)), DMA_SEMS((len(parts),))]

    def _copies(self, ins, outs, sems):
        send, recv, lsem = sems
        (x, y, c), me, sibling, chips, ids = _place()
        local = [pltpu.make_async_copy(ins[t].at[me], outs[t].at[me], lsem.at[t]) for t in range(len(ins))]
        outgoing, incoming = [], []
        for t in range(len(ins)):
            for k in range(3):
                to = (*chips[k], c)
                outgoing.append(_remote(ins[t].at[ids[k]], outs[t].at[me], send.at[t, k], recv.at[t, k], to))
                incoming.append(_remote(ins[t].at[ids[k]], outs[t].at[ids[k]], send.at[t, k], recv.at[t, k], to))
        return local, outgoing, incoming


class _Share(_Exchange):
    def __init__(self, grads):
        self.inputs = list(grads)
        self.out_shape = [jax.ShapeDtypeStruct(g.shape, g.dtype) for g in grads]
        self.aliases = {t: t for t in range(len(grads))}
        self.sems = [DMA_SEMS((len(grads),)), DMA_SEMS((len(grads),))]

    def _copies(self, ins, outs, sems):
        send, recv = sems
        (x, y, c), me, sibling, chips, ids = _place()
        outgoing, incoming = [], []
        for t, g in enumerate(self.inputs):
            mine = outs[t].at[_half(g.shape[0], c), :]
            theirs = outs[t].at[_half(g.shape[0], 1 - c), :]
            outgoing.append(_remote(mine, mine, send.at[t], recv.at[t], sibling))
            incoming.append(_remote(theirs, theirs, send.at[t], recv.at[t], sibling))
        return [], outgoing, incoming


class _GatherSmall(_Exchange):
    def __init__(self, block):
        self.inputs = [block]
        self.out_shape = [jax.ShapeDtypeStruct((N_DEV, *block.shape), block.dtype)]
        self.sems = [DMA_SEMS((7,)), DMA_SEMS((7,)), DMA_SEMS(())]

    def _copies(self, ins, outs, sems):
        send, recv, lsem = sems
        (x, y, c), me, sibling, chips, ids = _place()
        slot = lambda px, py, pc: outs[0].at[4 * px + 2 * py + pc]

        def copy(k, block, to, src=None):
            return _remote(slot(*block) if src is None else src, slot(*block), send.at[k], recv.at[k], to)

        local = [pltpu.make_async_copy(ins[0], slot(x, y, c), lsem)]
        first = [copy(0, (x, y, c), sibling, src=ins[0])] + [copy(1 + j, (x, y, c), (*chip, c), src=ins[0]) for j, chip in enumerate(chips)]
        passed = [copy(4 + j, (*chip, c), sibling) for j, chip in enumerate(chips)]
        landed = [copy(1 + j, (*chip, c), (x, y, c)) for j, chip in enumerate(chips)]
        from_sibling = [copy(0, (x, y, 1 - c), (x, y, c))] + [copy(4 + j, (*chip, 1 - c), (x, y, c)) for j, chip in enumerate(chips)]
        return local, first, (passed, landed, from_sibling)

    def finish(self, ins, outs, sems):
        local, first, (passed, landed, from_sibling) = self._copies(ins, outs, sems)
        for j in range(3):
            landed[j].wait_recv()
            passed[j].start()
        for cp in from_sibling:
            cp.wait_recv()
        for cp in first + passed:
            cp.wait_send()
        for cp in local:
            cp.wait()


def _split_refs(refs, counts):
    out = []
    for n in counts:
        out.append(refs[:n])
        refs = refs[n:]
    return out


def _each_exchange(exchanges, method, x_in, x_out, x_sem):
    for ex in exchanges:
        ni, no, ns = len(ex.inputs), len(ex.out_shape), len(ex.sems)
        getattr(ex, method)(x_in[:ni], x_out[:no], x_sem[:ns])
        x_in, x_out, x_sem = x_in[ni:], x_out[no:], x_sem[ns:]


def _call(body, *, name, grid, in_specs, out_specs, out_shape, operands, semantics, scratch_shapes=(), exchanges=()):
    x_in = [a for ex in exchanges for a in ex.inputs]
    x_out = [s for ex in exchanges for s in ex.out_shape]
    x_sem = [s for ex in exchanges for s in ex.sems]
    counts = (len(in_specs), len(x_in), len(out_specs), len(x_out), len(scratch_shapes), len(x_sem))
    aliases, i0, o0 = {}, len(in_specs), len(out_specs)
    for ex in exchanges:
        aliases.update({i0 + i: o0 + o for i, o in ex.aliases.items()})
        i0, o0 = i0 + len(ex.inputs), o0 + len(ex.out_shape)

    def full_body(*refs):
        ins, xi, outs, xo, scr, xs = _split_refs(list(refs), counts)
        if exchanges:
            @pl.when(functools.reduce(jnp.logical_and, [pl.program_id(a) == 0 for a in range(len(grid))]))
            def _():
                _each_exchange(exchanges, "start", xi, xo, xs)

        body(*ins, *outs, *scr)
        if exchanges:
            @pl.when(functools.reduce(jnp.logical_and, [pl.program_id(a) == grid[a] - 1 for a in range(len(grid))]))
            def _():
                _each_exchange(exchanges, "finish", xi, xo, xs)

    return pl.pallas_call(
        full_body, name=name, grid=grid,
        in_specs=list(in_specs) + [HBM] * len(x_in), out_specs=list(out_specs) + [HBM] * len(x_out),
        out_shape=list(out_shape) + x_out, scratch_shapes=list(scratch_shapes) + x_sem,
        input_output_aliases=aliases, compiler_params=_params(semantics),
    )(*operands, *x_in)


def _run_exchanges(exchanges, name):
    x_in = [a for ex in exchanges for a in ex.inputs]
    x_out = [s for ex in exchanges for s in ex.out_shape]
    x_sem = [s for ex in exchanges for s in ex.sems]
    aliases, i0, o0 = {}, 0, 0
    for ex in exchanges:
        aliases.update({i0 + i: o0 + o for i, o in ex.aliases.items()})
        i0, o0 = i0 + len(ex.inputs), o0 + len(ex.out_shape)

    def body(*refs):
        xi, xo, xs = _split_refs(list(refs), (len(x_in), len(x_out), len(x_sem)))
        _each_exchange(exchanges, "start", xi, xo, xs)
        _each_exchange(exchanges, "finish", xi, xo, xs)

    return pl.pallas_call(
        body, name=name, in_specs=[HBM] * len(x_in), out_specs=[HBM] * len(x_out), out_shape=x_out,
        scratch_shapes=x_sem, input_output_aliases=aliases,
    )(*x_in)


def _inproj_fwd(x, g1, w_in_all, exchanges=()):
    L, D = x.shape
    ns, _, nc = w_in_all.shape
    tm = _tile(L, TM_PROJ)

    def body(x_ref, g_ref, w_ref, hn_ref, proj_ref, u_ref):
        hn = _rms_fwd(x_ref[...], g_ref[...]).astype(MXU_DTYPE)
        hn_ref[...] = hn
        for j in range(ns):
            proj_ref[:, j * nc:(j + 1) * nc] = _dot(hn, w_ref[j])
        _store_slabs(u_ref, proj_ref[:, 0:nc])

    return _call(
        body, name="inproj_fwd", grid=(L // tm,), exchanges=exchanges, semantics=("arbitrary",), operands=(x, g1, w_in_all),
        in_specs=[pl.BlockSpec((tm, D), lambda i: (i, 0)), _resident((1, D)), _resident(w_in_all.shape)],
        out_specs=[pl.BlockSpec((tm, D), lambda i: (i, 0)), pl.BlockSpec((tm, ns * nc), lambda i: (i, 0)), _slab_spec(nc, tm)],
        out_shape=[jax.ShapeDtypeStruct((L, D), MXU_DTYPE), jax.ShapeDtypeStruct((L, ns * nc), F32), _slab_shape(L, nc)],
    )


def _slab_shape(L, n):
    return jax.ShapeDtypeStruct((n // LANES, L, LANES), F32)


def _slab_spec(n, tm, index=lambda i: (0, i, 0)):
    return pl.BlockSpec((n // LANES, tm, LANES), index)


def _store_slabs(ref, value):
    for k in range(ref.shape[0]):
        ref[k] = value[:, k * LANES:(k + 1) * LANES]


def _load_slabs(ref):
    return jnp.concatenate([ref[k] for k in range(ref.shape[0])], axis=1)


SEG_ROWS = SUBLANES * SUBLANES


def _load_permuted(ref):
    tm = ref.shape[1]
    slabs = []
    for k in range(ref.shape[0]):
        tiles = [ref.at[k][pl.ds(b * SEG_ROWS + j, SUBLANES, stride=SUBLANES), :] for b in range(tm // SEG_ROWS) for j in range(SUBLANES)]
        slabs.append(jnp.concatenate(tiles, axis=0))
    return jnp.concatenate(slabs, axis=1)


def _store_permuted(ref, value):
    tm = ref.shape[1]
    for k in range(ref.shape[0]):
        for b in range(tm // SEG_ROWS):
            for j in range(SUBLANES):
                r = b * SEG_ROWS + j * SUBLANES
                ref.at[k][pl.ds(b * SEG_ROWS + j, SUBLANES, stride=SUBLANES), :] = value[r:r + SUBLANES, k * LANES:(k + 1) * LANES]


def _scan_tile(xr, xi, hr, hi, coef_ref, lanes, reverse):
    for k, j in ((1, 0), (2, 2), (4, 4)):
        ar = coef_ref[j, :, lanes]
        ai = coef_ref[j + 1, :, lanes]
        shift = SUBLANES - k if reverse else k
        sr = pltpu.roll(xr, shift, 0)
        si = pltpu.roll(xi, shift, 0)
        xr, xi = xr + (ar * sr - ai * si), xi + (ar * si + ai * sr)
    pr = coef_ref[6, :, lanes]
    pi = coef_ref[7, :, lanes]
    return xr + (pr * hr - pi * hi), xi + (pr * hi + pi * hr)


def _scan_block(read, write, hr, hi, coef_ref, lanes, reverse):
    order = list(range(SUBLANES - 1, -1, -1) if reverse else range(SUBLANES))
    near = 8 + 2 * order[0]
    ar = coef_ref[near, :, lanes]
    ai = coef_ref[near + 1, :, lanes]
    xr, xi = read(order[0])
    local = {order[0]: (xr, xi)}
    for j in order[1:]:
        br, bi = read(j)
        xr, xi = br + (ar * xr - ai * xi), bi + (ar * xi + ai * xr)
        local[j] = (xr, xi)
    er, ei = _scan_tile(xr, xi, hr, hi, coef_ref, lanes, reverse)
    edge = lax.broadcasted_iota(jnp.int32, er.shape, 0) == (SUBLANES - 1 if reverse else 0)
    shift = SUBLANES - 1 if reverse else 1
    pr = jnp.where(edge, hr, pltpu.roll(er, shift, 0))
    pi = jnp.where(edge, hi, pltpu.roll(ei, shift, 0))
    for j in range(SUBLANES):
        cr = coef_ref[8 + 2 * j, :, lanes]
        ci = coef_ref[9 + 2 * j, :, lanes]
        xr, xi = local[j]
        write(j, xr + (cr * pr - ci * pi), xi + (cr * pi + ci * pr))
    end = 0 if reverse else SUBLANES - 1
    return jnp.broadcast_to(er[end:end + 1, :], er.shape), jnp.broadcast_to(ei[end:end + 1, :], ei.shape)


def _s5_fwd(u4, bmat, cmat, coef, dskip, exchanges=()):
    L = u4.shape[1]
    tm = _tile(L, TM_S5)
    lc = min(LANE_CHUNK, WB)

    def body(u_ref, bm_ref, cm_ref, coef_ref, d_ref, sre_ref, sim_ref, ys_ref, hr_ref, hi_ref):
        @pl.when(pl.program_id(1) == 0)
        def _():
            hr_ref[...] = jnp.zeros_like(hr_ref)
            hi_ref[...] = jnp.zeros_like(hi_ref)

        u = _load_permuted(u_ref)
        bu = _dot(u.astype(MXU_DTYPE), bm_ref[0])
        sre_ref[...] = bu[:, :WB]
        sim_ref[...] = bu[:, WB:]
        for c in range(WB // lc):
            lanes = slice(c * lc, (c + 1) * lc)
            hr, hi = hr_ref[:, lanes], hi_ref[:, lanes]
            for b in range(tm // SEG_ROWS):
                rows = lambda j, b=b: slice(b * SEG_ROWS + j * SUBLANES, b * SEG_ROWS + (j + 1) * SUBLANES)

                def read(j, rows=rows, lanes=lanes):
                    return sre_ref[rows(j), lanes], sim_ref[rows(j), lanes]

                def write(j, xr, xi, rows=rows, lanes=lanes):
                    sre_ref[rows(j), lanes] = xr
                    sim_ref[rows(j), lanes] = xi

                hr, hi = _scan_block(read, write, hr, hi, coef_ref, lanes, False)
            hr_ref[:, lanes] = hr
            hi_ref[:, lanes] = hi
        ys = _dot_nt(sre_ref[...].astype(MXU_DTYPE), cm_ref[0, :, :WB]) + _dot_nt(sim_ref[...].astype(MXU_DTYPE), cm_ref[0, :, WB:])
        _store_permuted(ys_ref, ys + d_ref[0] * u)

    return _call(
        body, name="s5_fwd", grid=(N_GBLK, L // tm), exchanges=exchanges, semantics=("arbitrary", "arbitrary"),
        operands=(u4, bmat, cmat, coef, dskip),
        in_specs=[
            _slab_spec(UB, tm, lambda b, i: (b, i, 0)),
            pl.BlockSpec((1, UB, 2 * WB), lambda b, i: (b, 0, 0)),
            pl.BlockSpec((1, UB, 2 * WB), lambda b, i: (b, 0, 0)),
            pl.BlockSpec((N_TABLES, SUBLANES, WB), lambda b, i: (0, 0, b)),
            pl.BlockSpec((1, 1, UB), lambda b, i: (b, 0, 0)),
        ],
        out_specs=[
            pl.BlockSpec((tm, WB), lambda b, i: (i, b)),
            pl.BlockSpec((tm, WB), lambda b, i: (i, b)),
            _slab_spec(UB, tm, lambda b, i: (b, i, 0)),
        ],
        out_shape=[
            jax.ShapeDtypeStruct((L, N_GBLK * WB), F32),
            jax.ShapeDtypeStruct((L, N_GBLK * WB), F32),
            _slab_shape(L, D_SSM),
        ],
        scratch_shapes=[pltpu.VMEM((SUBLANES, WB), F32), pltpu.VMEM((SUBLANES, WB), F32)],
    )


def _tail_fwd(x, ys, proj, w_glu, conv_w, g_ssm, g_conv, w_out, g_post, exchanges=()):
    L, D = x.shape
    tm = _tile(L, TM_TAIL)

    def body(x_ref, ys_ref, h_ref, bg_ref, cg_ref, wglu_ref, cw_ref, gs_ref, gc_ref, wout_ref, gp_ref,
             ycat_ref, o_ref, x1_ref, zbuf):
        @pl.when(pl.program_id(0) == 0)
        def _():
            zbuf[0:SUBLANES, :] = jnp.zeros((SUBLANES, D_CONV), F32)

        y1, _ = _gelu(_load_slabs(ys_ref))
        y2 = y1 * jax.nn.sigmoid(_dot(y1.astype(MXU_DTYPE), wglu_ref[...]))
        ycat_ref[:, :D_SSM] = _rms_fwd(y2, gs_ref[...]).astype(MXU_DTYPE)
        z = cg_ref[...] * h_ref[...]
        zbuf[SUBLANES:, :] = z
        conv = cw_ref[0:1, :] * zbuf[SUBLANES - 2:SUBLANES - 2 + tm, :] + cw_ref[1:2, :] * zbuf[SUBLANES - 1:SUBLANES - 1 + tm, :] + cw_ref[2:3, :] * z
        zbuf[0:SUBLANES, :] = zbuf[tm:tm + SUBLANES, :]
        ycat_ref[:, D_SSM:] = _rms_fwd(bg_ref[...] * conv, gc_ref[...]).astype(MXU_DTYPE)
        o = _dot(ycat_ref[...], wout_ref[...])
        o_ref[...] = o
        x1_ref[...] = x_ref[...] + _rms_fwd(o, gp_ref[...])

    row = lambda i: (i, 0)
    return _call(
        body, name="tail_fwd", grid=(L // tm,), exchanges=exchanges, semantics=("arbitrary",),
        operands=(x, ys, proj, proj, proj, w_glu, conv_w, g_ssm, g_conv, w_out, g_post),
        in_specs=[
            pl.BlockSpec((tm, D), row), _slab_spec(D_SSM, tm),
            pl.BlockSpec((tm, D_CONV), lambda i: (i, 1)), pl.BlockSpec((tm, D_CONV), lambda i: (i, 2)),
            pl.BlockSpec((tm, D_CONV), lambda i: (i, 3)),
            _resident(w_glu.shape), _resident(conv_w.shape), _resident(g_ssm.shape), _resident(g_conv.shape),
            _resident(w_out.shape), _resident(g_post.shape),
        ],
        out_specs=[pl.BlockSpec((tm, D), row), pl.BlockSpec((tm, D), row), pl.BlockSpec((tm, D), row)],
        out_shape=[jax.ShapeDtypeStruct((L, D), MXU_DTYPE), jax.ShapeDtypeStruct((L, D), F32), jax.ShapeDtypeStruct((L, D), F32)],
        scratch_shapes=[pltpu.VMEM((tm + SUBLANES, D_CONV), F32)],
    )


def _mlp_fwd(x1, target, w_up_all, w_down_a, w_down_b, g_pre, g_post):
    L, D = x1.shape
    ns, _, fc = w_up_all.shape
    half = w_down_a.shape[1]
    tm = _tile(L, TM_MLP)

    def body(x1_ref, t_ref, wup_ref, wda_ref, wdb_ref, gpre_ref, gpost_ref, hn2_ref, up_ref, m_ref, dx2_ref, loss_ref):
        @pl.when(pl.program_id(0) == 0)
        def _():
            loss_ref[...] = jnp.zeros_like(loss_ref)

        x1v = x1_ref[...]
        hn2 = _rms_fwd(x1v, gpre_ref[...]).astype(MXU_DTYPE)
        hn2_ref[...] = hn2
        m = jnp.zeros((tm, D), F32)
        for j in range(ns):
            up = _dot(hn2, wup_ref[j])
            up_ref[:, j * fc:(j + 1) * fc] = up
            act = jnp.square(jnp.maximum(up, 0.0)).astype(MXU_DTYPE)
            m = m + _dot(act[:, :half], wda_ref[j]) + _dot(act[:, half:], wdb_ref[j])
        m_ref[...] = m
        err = x1v + _rms_fwd(m, gpost_ref[...]) - t_ref[...]
        loss_ref[...] += 0.5 * jnp.sum(jnp.mean(err * err, axis=-1, keepdims=True))
        dx2_ref[...] = err * (1.0 / D)

    row = lambda i: (i, 0)
    return pl.pallas_call(
        body, name="mlp_fwd", grid=(L // tm,),
        in_specs=[pl.BlockSpec((tm, D), row), pl.BlockSpec((tm, D), row), _resident(w_up_all.shape), _resident(w_down_a.shape),
                  _resident(w_down_b.shape), _resident(g_pre.shape), _resident(g_post.shape)],
        out_specs=[pl.BlockSpec((tm, D), row), pl.BlockSpec((tm, ns * fc), row), pl.BlockSpec((tm, D), row),
                   pl.BlockSpec((tm, D), row), pl.BlockSpec((SUBLANES, 128), lambda i: (0, 0))],
        out_shape=[jax.ShapeDtypeStruct((L, D), MXU_DTYPE), jax.ShapeDtypeStruct((L, ns * fc), F32), jax.ShapeDtypeStruct((L, D), F32),
                   jax.ShapeDtypeStruct((L, D), F32), jax.ShapeDtypeStruct((SUBLANES, 128), F32)],
        compiler_params=_params(("arbitrary",)),
    )(x1, target, w_up_all, w_down_a, w_down_b, g_pre, g_post)


def _mlp_bwd(dx2, m, up, x1, w_up_all, w_down_a, w_down_b, g_pre, g_post):
    L, D = x1.shape
    ns, _, fc = w_up_all.shape
    tm = _tile(L, TM_MLP)

    def body(dx2_ref, m_ref, up_ref, x1_ref, wup_ref, wda_ref, wdb_ref, gpre_ref, gpost_ref,
             dm_ref, dup_ref, act_ref, dx1_ref, dgpost_ref, dgpre_ref):
        @pl.when(pl.program_id(0) == 0)
        def _():
            dgpost_ref[...] = jnp.zeros_like(dgpost_ref)
            dgpre_ref[...] = jnp.zeros_like(dgpre_ref)

        dx2v = dx2_ref[...]
        dm, dg = _rms_bwd(m_ref[...], gpost_ref[...], dx2v)
        dgpost_ref[...] += dg
        dm_b = dm.astype(MXU_DTYPE)
        dm_ref[...] = dm_b
        dhn2 = jnp.zeros((tm, D), F32)
        for j in range(ns):
            cols = slice(j * fc, (j + 1) * fc)
            relu = jnp.maximum(up_ref[:, cols], 0.0)
            act_ref[:, cols] = jnp.square(relu).astype(MXU_DTYPE)
            dact = jnp.concatenate([_dot_nt(dm_b, wda_ref[j]), _dot_nt(dm_b, wdb_ref[j])], axis=1)
            dup = (dact * (2.0 * relu)).astype(MXU_DTYPE)
            dup_ref[:, cols] = dup
            dhn2 = dhn2 + _dot_nt(dup, wup_ref[j])
        dx, dg = _rms_bwd(x1_ref[...], gpre_ref[...], dhn2)
        dgpre_ref[...] += dg
        dx1_ref[...] = dx2v + dx

    row = lambda i: (i, 0)
    vec = pl.BlockSpec((1, D), lambda i: (0, 0))
    return pl.pallas_call(
        body, name="mlp_bwd", grid=(L // tm,),
        in_specs=[pl.BlockSpec((tm, D), row), pl.BlockSpec((tm, D), row), pl.BlockSpec((tm, ns * fc), row), pl.BlockSpec((tm, D), row),
                  _resident(w_up_all.shape), _resident(w_down_a.shape), _resident(w_down_b.shape), _resident(g_pre.shape), _resident(g_post.shape)],
        out_specs=[pl.BlockSpec((tm, D), row), pl.BlockSpec((tm, ns * fc), row), pl.BlockSpec((tm, ns * fc), row),
                   pl.BlockSpec((tm, D), row), vec, vec],
        out_shape=[jax.ShapeDtypeStruct((L, D), MXU_DTYPE), jax.ShapeDtypeStruct((L, ns * fc), MXU_DTYPE),
                   jax.ShapeDtypeStruct((L, ns * fc), MXU_DTYPE), jax.ShapeDtypeStruct((L, D), F32),
                   jax.ShapeDtypeStruct((1, D), F32), jax.ShapeDtypeStruct((1, D), F32)],
        compiler_params=_params(("arbitrary",)),
    )(dx2, m, up, x1, w_up_all, w_down_a, w_down_b, g_pre, g_post)


def _tail_bwd(dx1, o, ys, proj, w_glu, conv_w, g_ssm, g_conv, w_out, g_post, exchanges=()):
    L, D = dx1.shape
    tm = _tile(L, TM_TAIL)
    nt = L // tm
    hb = tm // SUBLANES

    def body(dx1_ref, o_ref, ys_ref, h_ref, bg_ref, cg_ref, hh_ref, hcg_ref, wglu_ref, cw_ref, gs_ref, gc_ref, wout_ref, gp_ref,
             do_ref, da_ref, y1_ref, dys_ref, dhbc_ref, dgp_ref, dgs_ref, dgc_ref, dcw_ref, zbuf, dcbuf):
        step = pl.program_id(0)

        @pl.when(step == 0)
        def _():
            dcbuf[tm:, :] = jnp.zeros((SUBLANES, D_CONV), F32)
            dgp_ref[...] = jnp.zeros_like(dgp_ref)
            dgs_ref[...] = jnp.zeros_like(dgs_ref)
            dgc_ref[...] = jnp.zeros_like(dgc_ref)
            dcw_ref[...] = jnp.zeros_like(dcw_ref)

        do, dg = _rms_bwd(o_ref[...], gp_ref[...], dx1_ref[...])
        dgp_ref[...] += dg
        do_b = do.astype(MXU_DTYPE)
        do_ref[...] = do_b
        dycat = _dot_nt(do_b, wout_ref[...])
        y1, dgelu = _gelu(_load_slabs(ys_ref))
        y1_b = y1.astype(MXU_DTYPE)
        y1_ref[...] = y1_b
        s = jax.nn.sigmoid(_dot(y1_b, wglu_ref[...]))
        dy2, dg = _rms_bwd(y1 * s, gs_ref[...], dycat[:, :D_SSM])
        dgs_ref[...] += dg
        da_b = (dy2 * y1 * s * (1.0 - s)).astype(MXU_DTYPE)
        da_ref[...] = da_b
        _store_slabs(dys_ref, (dy2 * s + _dot_nt(da_b, wglu_ref[...])) * dgelu)
        h = h_ref[...]
        cg = cg_ref[...]
        bg = bg_ref[...]
        z = cg * h
        first = step == nt - 1
        zbuf[0:SUBLANES, :] = jnp.where(first, 0.0, hcg_ref[...] * hh_ref[...])
        zbuf[SUBLANES:, :] = z
        z1 = zbuf[SUBLANES - 1:SUBLANES - 1 + tm, :]
        z2 = zbuf[SUBLANES - 2:SUBLANES - 2 + tm, :]
        conv = cw_ref[0:1, :] * z2 + cw_ref[1:2, :] * z1 + cw_ref[2:3, :] * z
        dyc, dg = _rms_bwd(bg * conv, gc_ref[...], dycat[:, D_SSM:])
        dgc_ref[...] += dg
        dconv = dyc * bg
        dcw_ref[0:1, :] += jnp.sum(dconv * z2, axis=0, keepdims=True)
        dcw_ref[1:2, :] += jnp.sum(dconv * z1, axis=0, keepdims=True)
        dcw_ref[2:3, :] += jnp.sum(dconv * z, axis=0, keepdims=True)
        dcbuf[0:tm, :] = dconv
        dz = cw_ref[2:3, :] * dconv + cw_ref[1:2, :] * dcbuf[1:1 + tm, :] + cw_ref[0:1, :] * dcbuf[2:2 + tm, :]
        dcbuf[tm:, :] = dcbuf[0:SUBLANES, :]
        dhbc_ref[:, 0:D_CONV] = (dz * cg).astype(MXU_DTYPE)
        dhbc_ref[:, D_CONV:2 * D_CONV] = (dyc * conv).astype(MXU_DTYPE)
        dhbc_ref[:, 2 * D_CONV:] = (dz * h).astype(MXU_DTYPE)

    rev = lambda i: (nt - 1 - i, 0)
    slab = lambda i: (0, nt - 1 - i, 0)
    col = lambda c: (lambda i: (nt - 1 - i, c))
    halo = lambda c: (lambda i: (jnp.maximum((nt - 1 - i) * hb - 1, 0), c))
    vec = lambda n: pl.BlockSpec((1, n), lambda i: (0, 0))
    return _call(
        body, name="tail_bwd", grid=(nt,), exchanges=exchanges, semantics=("arbitrary",),
        operands=(dx1, o, ys, proj, proj, proj, proj, proj, w_glu, conv_w, g_ssm, g_conv, w_out, g_post),
        in_specs=[
            pl.BlockSpec((tm, D), rev), pl.BlockSpec((tm, D), rev), _slab_spec(D_SSM, tm, slab),
            pl.BlockSpec((tm, D_CONV), col(1)), pl.BlockSpec((tm, D_CONV), col(2)), pl.BlockSpec((tm, D_CONV), col(3)),
            pl.BlockSpec((SUBLANES, D_CONV), halo(1)), pl.BlockSpec((SUBLANES, D_CONV), halo(3)),
            _resident(w_glu.shape), _resident(conv_w.shape), _resident(g_ssm.shape), _resident(g_conv.shape),
            _resident(w_out.shape), _resident(g_post.shape),
        ],
        out_specs=[
            pl.BlockSpec((tm, D), rev), pl.BlockSpec((tm, D_SSM), rev), pl.BlockSpec((tm, D_SSM), rev), _slab_spec(D_SSM, tm, slab),
            pl.BlockSpec((tm, 3 * D_CONV), rev), vec(D), vec(D_SSM), vec(D_CONV),
            pl.BlockSpec((SUBLANES, D_CONV), lambda i: (0, 0)),
        ],
        out_shape=[
            jax.ShapeDtypeStruct((L, D), MXU_DTYPE), jax.ShapeDtypeStruct((L, D_SSM), MXU_DTYPE), jax.ShapeDtypeStruct((L, D_SSM), MXU_DTYPE),
            _slab_shape(L, D_SSM), jax.ShapeDtypeStruct((L, 3 * D_CONV), MXU_DTYPE),
            jax.ShapeDtypeStruct((1, D), F32), jax.ShapeDtypeStruct((1, D_SSM), F32), jax.ShapeDtypeStruct((1, D_CONV), F32),
            jax.ShapeDtypeStruct((SUBLANES, D_CONV), F32),
        ],
        scratch_shapes=[pltpu.VMEM((tm + SUBLANES, D_CONV), F32), pltpu.VMEM((tm + SUBLANES, D_CONV), F32)],
    )


def _s5_bwd(dys, u4, s_re, s_im, bmat, cmat, coef_rev, dskip, exchanges=()):
    L = dys.shape[1]
    tm = _tile(L, TM_S5)
    nt = L // tm
    lc = min(LANE_CHUNK, WB)

    def body(dys_ref, u_ref, sre_ref, sim_ref, bm_ref, cm_ref, coef_ref, d_ref,
             du_ref, gb_ref, gc_ref, q_ref, gd_ref, dr_ref, di_ref, lr_ref, li_ref, hr_ref, hi_ref, qr_acc, qi_acc, gb_acc, gc_acc):
        step = pl.program_id(1)

        @pl.when(step == 0)
        def _():
            for ref in (hr_ref, hi_ref, qr_acc, qi_acc, gb_acc, gc_acc, gd_ref):
                ref[...] = jnp.zeros_like(ref)

        dys_v = _load_permuted(dys_ref)
        u = _load_permuted(u_ref)
        dys_b = dys_v.astype(MXU_DTYPE)
        u_b = u.astype(MXU_DTYPE)
        d = _dot(dys_b, cm_ref[0])
        dr_ref[...] = d[:, :WB]
        di_ref[...] = d[:, WB:]
        for c in range(WB // lc):
            lanes = slice(c * lc, (c + 1) * lc)
            hr, hi = hr_ref[:, lanes], hi_ref[:, lanes]
            q = [qr_acc[:, lanes], qi_acc[:, lanes]]
            for b in range(tm // SEG_ROWS - 1, -1, -1):
                rows = lambda j, b=b: slice(b * SEG_ROWS + j * SUBLANES, b * SEG_ROWS + (j + 1) * SUBLANES)

                def read(j, rows=rows, lanes=lanes):
                    return dr_ref[rows(j), lanes], di_ref[rows(j), lanes]

                def write(j, xr, xi, rows=rows, lanes=lanes, q=q):
                    lr_ref[rows(j), lanes] = xr
                    li_ref[rows(j), lanes] = xi
                    er = xr - dr_ref[rows(j), lanes]
                    ei = xi - di_ref[rows(j), lanes]
                    sr = sre_ref[rows(j), lanes]
                    si = sim_ref[rows(j), lanes]
                    q[0] = q[0] + (er * sr + ei * si)
                    q[1] = q[1] + (ei * sr - er * si)

                hr, hi = _scan_block(read, write, hr, hi, coef_ref, lanes, True)
            hr_ref[:, lanes] = hr
            hi_ref[:, lanes] = hi
            qr_acc[:, lanes] = q[0]
            qi_acc[:, lanes] = q[1]
        lr_b = lr_ref[...].astype(MXU_DTYPE)
        li_b = li_ref[...].astype(MXU_DTYPE)
        _store_permuted(du_ref, _dot_nt(lr_b, bm_ref[0, :, :WB]) + _dot_nt(li_b, bm_ref[0, :, WB:]) + d_ref[0] * dys_v)
        gb_acc[:, :WB] += _dot_tn(u_b, lr_b)
        gb_acc[:, WB:] += _dot_tn(u_b, li_b)
        gc_acc[:, :WB] += _dot_tn(dys_b, sre_ref[...].astype(MXU_DTYPE))
        gc_acc[:, WB:] += _dot_tn(dys_b, sim_ref[...].astype(MXU_DTYPE))
        gd_ref[0] += jnp.sum(dys_v * u, axis=0, keepdims=True)

        @pl.when(step == nt - 1)
        def _():
            q_ref[0, 0:1, :] = jnp.sum(qr_acc[...], axis=0, keepdims=True)
            q_ref[0, 1:2, :] = jnp.sum(qi_acc[...], axis=0, keepdims=True)
            mask = _group_mask(UB, WB)
            fold = (lax.broadcasted_iota(jnp.int32, (WB, STATE), 0) % STATE == lax.broadcasted_iota(jnp.int32, (WB, STATE), 1)).astype(F32)
            for acc, out in ((gb_acc, gb_ref), (gc_acc, gc_ref)):
                for k in range(2):
                    own = jnp.where(mask, acc[:, k * WB:(k + 1) * WB], 0.0)
                    out[0, k] = jnp.dot(own, fold, precision=lax.Precision.HIGHEST, preferred_element_type=F32)

    rev = lambda b, i: (nt - 1 - i, b)
    slab = lambda b, i: (b, nt - 1 - i, 0)
    blk = lambda b, i: (b, 0, 0)
    blk4 = lambda b, i: (b, 0, 0, 0)
    return _call(
        body, name="s5_bwd", grid=(N_GBLK, nt), exchanges=exchanges, semantics=("arbitrary", "arbitrary"),
        operands=(dys, u4, s_re, s_im, bmat, cmat, coef_rev, dskip),
        in_specs=[
            _slab_spec(UB, tm, slab), _slab_spec(UB, tm, slab), pl.BlockSpec((tm, WB), rev), pl.BlockSpec((tm, WB), rev),
            pl.BlockSpec((1, UB, 2 * WB), blk), pl.BlockSpec((1, UB, 2 * WB), blk),
            pl.BlockSpec((N_TABLES, SUBLANES, WB), lambda b, i: (0, 0, b)), pl.BlockSpec((1, 1, UB), blk),
        ],
        out_specs=[
            _slab_spec(UB, tm, slab), pl.BlockSpec((1, 2, UB, STATE), blk4), pl.BlockSpec((1, 2, UB, STATE), blk4),
            pl.BlockSpec((1, 2, WB), blk), pl.BlockSpec((1, 1, UB), blk),
        ],
        out_shape=[
            _slab_shape(L, D_SSM), jax.ShapeDtypeStruct((N_GBLK, 2, UB, STATE), F32),
            jax.ShapeDtypeStruct((N_GBLK, 2, UB, STATE), F32), jax.ShapeDtypeStruct((N_GBLK, 2, WB), F32),
            jax.ShapeDtypeStruct((N_GBLK, 1, UB), F32),
        ],
        scratch_shapes=[pltpu.VMEM((tm, WB), F32)] * 4 + [pltpu.VMEM((SUBLANES, WB), F32)] * 4 + [pltpu.VMEM((UB, 2 * WB), F32)] * 2,
    )


def _inproj_bwd(du, dhbc, x, dx1, w_in_all, g1):
    L, D = x.shape
    ns, _, nc = w_in_all.shape
    tm = _tile(L, TM_PROJ)

    def body(du_ref, dhbc_ref, x_ref, dx1_ref, w_ref, g_ref, gx_ref, dproj_ref, dg_ref):
        @pl.when(pl.program_id(0) == 0)
        def _():
            dg_ref[...] = jnp.zeros_like(dg_ref)

        du_b = _load_slabs(du_ref).astype(MXU_DTYPE)
        dproj_ref[:, :nc] = du_b
        dproj_ref[:, nc:] = dhbc_ref[...]
        dhn = _dot_nt(du_b, w_ref[0])
        for j in range(1, ns):
            dhn = dhn + _dot_nt(dhbc_ref[:, (j - 1) * nc:j * nc], w_ref[j])
        dx, dg = _rms_bwd(x_ref[...], g_ref[...], dhn)
        dg_ref[...] += dg
        gx_ref[...] = dx1_ref[...] + dx

    row = lambda i: (i, 0)
    return pl.pallas_call(
        body, name="inproj_bwd", grid=(L // tm,),
        in_specs=[_slab_spec(nc, tm), pl.BlockSpec((tm, (ns - 1) * nc), row), pl.BlockSpec((tm, D), row), pl.BlockSpec((tm, D), row),
                  _resident(w_in_all.shape), _resident(g1.shape)],
        out_specs=[pl.BlockSpec((tm, D), row), pl.BlockSpec((tm, ns * nc), row), pl.BlockSpec((1, D), lambda i: (0, 0))],
        out_shape=[jax.ShapeDtypeStruct((L, D), F32), jax.ShapeDtypeStruct((L, ns * nc), MXU_DTYPE), jax.ShapeDtypeStruct((1, D), F32)],
        compiler_params=_params(("arbitrary",)),
    )(du, dhbc, x, dx1, w_in_all, g1)


def _matmul_tn(a, b, name, col_shards=1, exchanges=()):
    L, K = a.shape
    N = b.shape[1]
    tl = _tile(L, TL_TN)
    tk = _tile(K, 1024)
    nw = N // col_shards
    tn = _tile(nw, 1024)
    npb = nw // tn

    def body(a_ref, b_ref, o_ref):
        @pl.when(pl.program_id(2) == 0)
        def _():
            o_ref[...] = jnp.zeros_like(o_ref)

        o_ref[0] += _dot_tn(a_ref[...], b_ref[...])

    return _call(
        body, name=name, grid=(K // tk, N // tn, L // tl), exchanges=exchanges, semantics=("arbitrary", "arbitrary", "arbitrary"),
        operands=(a, b),
        in_specs=[pl.BlockSpec((tl, tk), lambda k, n, l: (l, k)), pl.BlockSpec((tl, tn), lambda k, n, l: (l, n))],
        out_specs=[pl.BlockSpec((1, tk, tn), lambda k, n, l: (n // npb, k, n % npb))],
        out_shape=[jax.ShapeDtypeStruct((col_shards, K, nw), F32)],
    )


def _ssm_discretize(lam_re, lam_im, log_dt, bt_re, bt_im):
    dt = jnp.exp(log_dt)[:, None]
    zr = lam_re * dt
    zi = lam_im * dt
    mag = jnp.exp(zr)
    abr = mag * jnp.cos(zi)
    abi = mag * jnp.sin(zi)
    nr, ni = abr - 1.0, abi
    den = lam_re * lam_re + lam_im * lam_im
    coef_r = ((nr * lam_re + ni * lam_im) / den)[:, None, :]
    coef_i = ((ni * lam_re - nr * lam_im) / den)[:, None, :]
    return zr, zi, coef_r * bt_re - coef_i * bt_im, coef_r * bt_im + coef_i * bt_re


def _scan_tables(ar, ai, reverse):
    rows = np.arange(SUBLANES)
    exps = np.zeros((N_TABLES // 2, SUBLANES), np.int32)
    keep = np.ones((N_TABLES // 2, SUBLANES), bool)
    for t, k in enumerate((1, 2, 4)):
        exps[t] = SUBLANES * k
        keep[t] = (rows + k <= SUBLANES - 1) if reverse else (rows >= k)
    exps[3] = SUBLANES * (SUBLANES - rows) if reverse else SUBLANES * (rows + 1)
    for j in range(SUBLANES):
        exps[4 + j] = SUBLANES - j if reverse else j + 1
    pr, pi = ar, (-ai if reverse else ai)
    shape = (N_TABLES // 2, SUBLANES, ar.shape[0])
    xr, xi = jnp.ones(shape, F32), jnp.zeros(shape, F32)
    for bit in range(int(exps.max()).bit_length()):
        on = ((exps >> bit) & 1).astype(bool)[:, :, None]
        xr, xi = jnp.where(on, xr * pr - xi * pi, xr), jnp.where(on, xr * pi + xi * pr, xi)
        pr, pi = pr * pr - pi * pi, 2.0 * pr * pi
    xr = jnp.where(keep[:, :, None], xr, 0.0)
    xi = jnp.where(keep[:, :, None], xi, 0.0)
    return jnp.stack([xr, xi], axis=1).reshape(N_TABLES, SUBLANES, ar.shape[0])


def _group_mask(rows, cols):
    r = lax.broadcasted_iota(jnp.int32, (rows, cols), 0) // GROUP
    c = lax.broadcasted_iota(jnp.int32, (rows, cols), 1) // STATE
    return r == c


def _ssm_expand(bt_re, bt_im, c_re, c_im, exchanges=()):
    flat = lambda a: a.reshape(N_GROUPS * GROUP, STATE)

    def body(br_ref, bi_ref, cr_ref, ci_ref, bm_ref, cm_ref):
        spread = (lax.broadcasted_iota(jnp.int32, (STATE, WB), 1) % STATE == lax.broadcasted_iota(jnp.int32, (STATE, WB), 0)).astype(F32)
        mask = _group_mask(UB, WB)

        def expand(x):
            wide = jnp.dot(x, spread, precision=lax.Precision.HIGHEST, preferred_element_type=F32)
            return jnp.where(mask, wide, 0.0).astype(MXU_DTYPE)

        bm_ref[0, :, :WB] = expand(br_ref[...])
        bm_ref[0, :, WB:] = expand(bi_ref[...])
        cm_ref[0, :, :WB] = expand(cr_ref[...])
        cm_ref[0, :, WB:] = expand(-ci_ref[...])

    spec = pl.BlockSpec((UB, STATE), lambda b: (b, 0))
    out = pl.BlockSpec((1, UB, 2 * WB), lambda b: (b, 0, 0))
    return _call(
        body, name="ssm_expand", grid=(N_GBLK,), exchanges=exchanges, semantics=("arbitrary",),
        operands=(flat(bt_re), flat(bt_im), flat(c_re), flat(c_im)), in_specs=[spec] * 4, out_specs=[out, out],
        out_shape=[jax.ShapeDtypeStruct((N_GBLK, UB, 2 * WB), MXU_DTYPE)] * 2,
    )


def _ssm_matrices(lam_re, lam_im, log_dt, bt_re, bt_im, c_re, c_im, exchanges=()):
    zr, zi, bbar_r, bbar_i = _ssm_discretize(lam_re, lam_im, log_dt, bt_re, bt_im)
    mag = jnp.exp(zr)
    ar = (mag * jnp.cos(zi)).reshape(-1)
    ai = (mag * jnp.sin(zi)).reshape(-1)
    bmat, cmat, *rest = _ssm_expand(bbar_r, bbar_i, c_re, c_im, exchanges)
    return bmat, cmat, _scan_tables(ar, ai, False), _scan_tables(ar, ai, True), rest


def _ssm_param_grads(lam_re, lam_im, log_dt, bt_re, bt_im, gb, gc, q, gd):
    part = lambda g, k: g[:, k].reshape(N_GROUPS, GROUP, STATE)
    qr = q[:, 0, :].reshape(N_GROUPS, STATE)
    qi = q[:, 1, :].reshape(N_GROUPS, STATE)
    _, vjp = jax.vjp(_ssm_discretize, lam_re, lam_im, log_dt, bt_re, bt_im)
    d_lam_re, d_lam_im, d_log_dt, d_bt_re, d_bt_im = vjp((qr, qi, part(gb, 0), part(gb, 1)))
    return d_lam_re, d_lam_im, d_log_dt, d_bt_re, d_bt_im, part(gc, 0), -part(gc, 1), gd.reshape(N_GROUPS, GROUP)


def _row_tile(rows, n):
    return _tile(rows, max(SUBLANES, (2 * 1024 * 1024) // (4 * n)))


def _pair_add(grad, other, core, name):
    ns, h, n = other.shape
    tr = _row_tile(h, n)
    nb = h // tr

    def body(c_ref, g_ref, o_ref, out_ref):
        out_ref[...] = (g_ref[...] + o_ref[...]).astype(WIRE_DTYPE)

    return pl.pallas_call(
        body, name=name,
        grid_spec=pltpu.PrefetchScalarGridSpec(
            num_scalar_prefetch=1, grid=(ns, nb),
            in_specs=[pl.BlockSpec((1, tr, n), lambda s, i, c: (s, c[0] * nb + i, 0)), pl.BlockSpec((1, tr, n), lambda s, i, c: (s, i, 0))],
            out_specs=pl.BlockSpec((1, tr, n), lambda s, i, c: (s, i, 0))),
        out_shape=jax.ShapeDtypeStruct(other.shape, WIRE_DTYPE),
        compiler_params=_params(("arbitrary", "arbitrary")),
    )(core, grad, other)


def _quad_sum(parts, core, name):
    ns, h, n = parts.shape
    tr = _row_tile(h, n)
    nb = h // tr

    def body(c_ref, p_ref, out_ref):
        p = [p_ref[k].astype(F32) for k in range(ns)]
        out_ref[...] = ((p[0] + p[1]) + p[2]) + p[3]

    return pl.pallas_call(
        body, name=name,
        grid_spec=pltpu.PrefetchScalarGridSpec(
            num_scalar_prefetch=1, grid=(nb,),
            in_specs=[pl.BlockSpec((ns, tr, n), lambda i, c: (0, i, 0))],
            out_specs=pl.BlockSpec((tr, n), lambda i, c: (c[0] * nb + i, 0))),
        out_shape=jax.ShapeDtypeStruct((2 * h, n), F32),
        compiler_params=_params(("arbitrary",)),
    )(core, parts)


def _adamw_math(w, g, m, v):
    m = ADAM_B1 * m + (1.0 - ADAM_B1) * g
    v = ADAM_B2 * v + (1.0 - ADAM_B2) * jnp.square(g)
    m_hat = m / (1.0 - ADAM_B1 ** ADAM_STEP)
    v_hat = v / (1.0 - ADAM_B2 ** ADAM_STEP)
    delta = -ADAM_LR * (m_hat / (jnp.sqrt(v_hat) + ADAM_EPS) + ADAM_WD * w)
    return delta, m, v


def _adamw(w, g, m, v, name):
    r, n = w.shape
    tr = _row_tile(r, n)

    def body(w_ref, g_ref, m_ref, v_ref, d_ref, nm_ref, nv_ref):
        d_ref[...], nm_ref[...], nv_ref[...] = _adamw_math(w_ref[...], g_ref[...], m_ref[...], v_ref[...])

    spec = pl.BlockSpec((tr, n), lambda i: (i, 0))
    return pl.pallas_call(
        body, name=name, grid=(r // tr,), in_specs=[spec] * 4, out_specs=[spec] * 3,
        out_shape=[jax.ShapeDtypeStruct((r, n), F32)] * 3,
        compiler_params=_params(("arbitrary",)),
    )(w, g, m, v)


LANES = 128
SMALL = ["g_pre_mix", "lam_re", "lam_im", "log_dt", "b_re", "b_im", "c_re", "c_im", "d_skip", "conv_w", "g_ssm_out", "g_conv_out",
         "g_post_mix", "g_pre_mlp", "g_post_mlp"]
TILE_SLOTS = {"b_re": (0, N_GROUPS), "b_im": (N_GROUPS, N_GROUPS), "c_re": (2 * N_GROUPS, N_GROUPS), "c_im": (3 * N_GROUPS, N_GROUPS),
              "lam_re": (4 * N_GROUPS, 2), "lam_im": (4 * N_GROUPS + 2, 2)}
N_TILE_SLOTS = 4 * N_GROUPS + 4
VEC_ROWS = {"g_pre_mix": 0, "g_post_mix": 1, "g_pre_mlp": 2, "g_post_mlp": 3, "g_ssm_out": 4, "g_conv_out": 5, "log_dt": 6}
ROW_LOSS, ROW_DSKIP, ROW_CONV, N_PACK_ROWS = 7, 8, 24, 32


def _kernel_form(name, a):
    if name in ("b_re", "b_im"):
        return jnp.transpose(a, (0, 1, 3, 2)).reshape(N_GROUPS, GROUP, STATE)
    if name in ("c_re", "c_im"):
        return a.reshape(N_GROUPS, GROUP, STATE)
    if name in ("lam_re", "lam_im"):
        return a.reshape(2, GROUP, STATE)
    if name == "d_skip":
        return jnp.transpose(a, (0, 2, 1)).reshape(GROUP, N_GROUPS)
    if name == "conv_w":
        return jnp.transpose(a, (1, 0, 2))
    return a


def _param_form(name, k):
    if name in ("b_re", "b_im"):
        return jnp.transpose(k.reshape(1, N_GROUPS, GROUP, STATE), (0, 1, 3, 2))
    if name in ("c_re", "c_im"):
        return k.reshape(1, N_GROUPS, GROUP, STATE)
    if name in ("lam_re", "lam_im"):
        return k.reshape(1, N_GROUPS, STATE)
    if name == "d_skip":
        return jnp.transpose(k.reshape(1, GROUP, N_GROUPS), (0, 2, 1))
    if name == "conv_w":
        return jnp.transpose(k, (1, 0, 2))
    return k


def _pack_tiles(g):
    lam = lambda a: a.reshape(2, GROUP, STATE)
    tiles = jnp.concatenate([g["b_re"], g["b_im"], g["c_re"], g["c_im"], lam(g["lam_re"]), lam(g["lam_im"])], axis=0)
    return tiles.astype(WIRE_DTYPE)


def _pack_rows(g, loss):
    row = lambda a: jnp.pad(a, ((0, 0), (0, D_MODEL - a.shape[1])))
    rows = [row(g[k][None]) for k in VEC_ROWS] + [row(loss[0:1]), row(g["d_skip"].T), row(g["conv_w"])]
    rows.append(jnp.zeros((N_PACK_ROWS - ROW_CONV - 3, D_MODEL), F32))
    return jnp.concatenate(rows, axis=0)


def _adamw_small(tiles, rows, w, m, v):
    nn = len(SMALL)

    def body(*refs):
        t_ref, r_ref = refs[0], refs[1]
        w_refs, m_refs, v_refs = refs[2:2 + nn], refs[2 + nn:2 + 2 * nn], refs[2 + 2 * nn:2 + 3 * nn]
        loss_ref, outs = refs[2 + 3 * nn], refs[3 + 3 * nn:]

        def tile_sum(first, count):
            total = t_ref[0, first:first + count].astype(F32)
            for d in range(1, N_DEV):
                total = total + t_ref[d, first:first + count].astype(F32)
            return total

        def row_sum(first, count, lanes):
            total = r_ref[0, first:first + count, 0:lanes]
            for d in range(1, N_DEV):
                total = total + r_ref[d, first:first + count, 0:lanes]
            return total

        def step(j, g, at=lambda ref: ref):
            delta, nm, nv = _adamw_math(at(w_refs[j])[...], g, at(m_refs[j])[...], at(v_refs[j])[...])
            at(outs[j])[...] = g
            at(outs[nn + j])[...] = delta
            at(outs[2 * nn + j])[...] = nm
            at(outs[3 * nn + j])[...] = nv

        loss_ref[...] = row_sum(ROW_LOSS, 1, LANES)
        chip = 2 * lax.axis_index("x") + lax.axis_index("y")
        for j, name in enumerate(SMALL):
            if name in TILE_SLOTS:
                step(j, tile_sum(*TILE_SLOTS[name]))
            elif name == "d_skip":
                step(j, row_sum(ROW_DSKIP, GROUP, N_GROUPS))
            elif name == "conv_w":
                full = row_sum(ROW_CONV, 3, D_CONV)
                mine = full[:, 0:LANES]
                for s in range(1, N_CHIPS):
                    mine = jnp.where(chip == s, full[:, s * LANES:(s + 1) * LANES], mine)
                for k in range(3):
                    step(j, mine[k:k + 1, :], at=lambda ref, k=k: ref.at[k])
            else:
                step(j, row_sum(VEC_ROWS[name], 1, w_refs[j].shape[1]))

    args = [tiles, rows] + [w[k] for k in SMALL] + [m[k] for k in SMALL] + [v[k] for k in SMALL]
    res = pl.pallas_call(
        body, name="adamw_small", in_specs=[VMEM] * len(args), out_specs=[VMEM] * (1 + 4 * nn),
        out_shape=[jax.ShapeDtypeStruct((1, LANES), F32)] + [jax.ShapeDtypeStruct(w[k].shape, F32) for k in SMALL] * 4,
        compiler_params=pltpu.CompilerParams(vmem_limit_bytes=VMEM_LIMIT),
    )(*args)
    return res[0], [dict(zip(SMALL, res[1 + q * nn:1 + (q + 1) * nn])) for q in range(4)]


WEIGHTS = ["g_pre_mix", "w_in", "lam_re", "lam_im", "log_dt", "b_re", "b_im", "c_re", "c_im", "d_skip", "w_glu", "conv_w",
           "g_ssm_out", "g_conv_out", "w_out", "g_post_mix", "g_pre_mlp", "w_up", "w_down", "g_post_mlp"]
BIG = ["w_in", "w_glu", "w_out", "w_up", "w_down"]


def kernel(x, g_pre_mix, w_in, lam_re, lam_im, log_dt, b_re, b_im, c_re, c_im, d_skip, w_glu, conv_w, g_ssm_out, g_conv_out, w_out, g_post_mix, g_pre_mlp, w_up, w_down, g_post_mlp, loss_target, m_g_pre_mix, m_w_in, m_lam_re, m_lam_im, m_log_dt, m_b_re, m_b_im, m_c_re, m_c_im, m_d_skip, m_w_glu, m_conv_w, m_g_ssm_out, m_g_conv_out, m_w_out, m_g_post_mix, m_g_pre_mlp, m_w_up, m_w_down, m_g_post_mlp, v_g_pre_mix, v_w_in, v_lam_re, v_lam_im, v_log_dt, v_b_re, v_b_im, v_c_re, v_c_im, v_d_skip, v_w_glu, v_conv_w, v_g_ssm_out, v_g_conv_out, v_w_out, v_g_post_mix, v_g_pre_mlp, v_w_up, v_w_down, v_g_post_mlp):
    w = dict(g_pre_mix=g_pre_mix, w_in=w_in, lam_re=lam_re, lam_im=lam_im, log_dt=log_dt, b_re=b_re, b_im=b_im, c_re=c_re, c_im=c_im,
             d_skip=d_skip, w_glu=w_glu, conv_w=conv_w, g_ssm_out=g_ssm_out, g_conv_out=g_conv_out, w_out=w_out, g_post_mix=g_post_mix,
             g_pre_mlp=g_pre_mlp, w_up=w_up, w_down=w_down, g_post_mlp=g_post_mlp)
    m = dict(g_pre_mix=m_g_pre_mix, w_in=m_w_in, lam_re=m_lam_re, lam_im=m_lam_im, log_dt=m_log_dt, b_re=m_b_re, b_im=m_b_im, c_re=m_c_re,
             c_im=m_c_im, d_skip=m_d_skip, w_glu=m_w_glu, conv_w=m_conv_w, g_ssm_out=m_g_ssm_out, g_conv_out=m_g_conv_out, w_out=m_w_out,
             g_post_mix=m_g_post_mix, g_pre_mlp=m_g_pre_mlp, w_up=m_w_up, w_down=m_w_down, g_post_mlp=m_g_post_mlp)
    v = dict(g_pre_mix=v_g_pre_mix, w_in=v_w_in, lam_re=v_lam_re, lam_im=v_lam_im, log_dt=v_log_dt, b_re=v_b_re, b_im=v_b_im, c_re=v_c_re,
             c_im=v_c_im, d_skip=v_d_skip, w_glu=v_w_glu, conv_w=v_conv_w, g_ssm_out=v_g_ssm_out, g_conv_out=v_g_conv_out, w_out=v_w_out,
             g_post_mix=v_g_post_mix, g_pre_mlp=v_g_pre_mlp, w_up=v_w_up, w_down=v_w_down, g_post_mlp=v_g_post_mlp)
    w_dev, m_dev, v_dev = w, m, v
    w, m, v = ({k: a[0] for k, a in d.items()} for d in (w, m, v))
    core = lax.axis_index("c").astype(jnp.int32).reshape(1)

    xs, target = x[0], loss_target[0]
    g1 = w["g_pre_mix"][None]
    g_ssm, g_conv = w["g_ssm_out"][None], w["g_conv_out"][None]
    g_post_mix, g_pre_mlp, g_post_mlp = w["g_post_mix"][None], w["g_pre_mlp"][None], w["g_post_mlp"][None]
    bt_re, bt_im = (jnp.transpose(w[k], (0, 2, 1)) for k in ("b_re", "b_im"))
    dskip = w["d_skip"].reshape(N_GBLK, 1, UB)
    shard = {k: w[k].astype(MXU_DTYPE) for k in BIG}
    conv_pad = jnp.pad(w["conv_w"], ((0, SUBLANES - 3), (0, 0)))

    bmat, cmat, coef_f, coef_r, (w_in_all,) = _ssm_matrices(
        w["lam_re"], w["lam_im"], w["log_dt"], bt_re, bt_im, w["c_re"], w["c_im"], exchanges=[_GatherForward([shard["w_in"]])])
    hn, proj, u4, w_glu_all, w_out_all, conv_all = _inproj_fwd(
        xs, g1, w_in_all, exchanges=[_Gather([shard["w_glu"], shard["w_out"], conv_pad], [False, False, False])])
    wd_half = shard["w_down"].shape[0] // 2
    wda, wdb = shard["w_down"][:wd_half], shard["w_down"][wd_half:]
    s_re, s_im, ys, w_up_all, wda_all = _s5_fwd(u4, bmat, cmat, coef_f, dskip, exchanges=[_Gather([shard["w_up"], wda], [True, True])])
    w_glu_f, w_out_f = w_glu_all.reshape(D_SSM, D_SSM), w_out_all.reshape(D_MODEL, D_MODEL)
    conv_f = jnp.transpose(conv_all, (1, 0, 2)).reshape(SUBLANES, D_CONV)
    ycat, o, x1, w_up_all, wda_all, wdb_all = _tail_fwd(xs, ys, proj, w_glu_f, conv_f, g_ssm, g_conv, w_out_f, g_post_mix,
                                                        exchanges=[_Forward([w_up_all, wda_all]), _GatherForward([wdb])])
    hn2, up, m_act, dx2, loss = _mlp_fwd(x1, target, w_up_all, wda_all, wdb_all, g_pre_mlp, g_post_mlp)

    dm, dup, act, dx1, dg_post_mlp, dg_pre_mlp = _mlp_bwd(dx2, m_act, up, x1, w_up_all, wda_all, wdb_all, g_pre_mlp, g_post_mlp)
    gw_down = _matmul_tn(act, dm, "dw_down")[0].reshape(N_CHIPS, D_FF // N_CHIPS, D_MODEL)
    gw_up = _matmul_tn(hn2, dup, "dw_up", col_shards=N_CHIPS)[0]
    do, da, y1, dys, dhbc, dg_post_mix, dg_ssm, dg_conv, dconv_w, o_down, o_up = _tail_bwd(
        dx1, o, ys, proj, w_glu_f, conv_f, g_ssm, g_conv, w_out_f, g_post_mix, exchanges=[_Pair([gw_down, gw_up])])
    p_down = _pair_add(gw_down, o_down, core, "pair_add_w_down")
    p_up = _pair_add(gw_up, o_up, core, "pair_add_w_up")
    gw_out = _matmul_tn(ycat, do, "dw_out")[0].reshape(N_CHIPS, D_MODEL // N_CHIPS, D_MODEL)
    gw_glu = _matmul_tn(y1, da, "dw_glu")[0].reshape(N_CHIPS, D_SSM // N_CHIPS, D_SSM)
    du, gb, gc, q, gd, q_down, q_up, o_out, o_glu = _s5_bwd(
        dys, u4, s_re, s_im, bmat, cmat, coef_r, dskip, exchanges=[_Chip([p_down, p_up]), _Pair([gw_out, gw_glu])])
    h_down = _quad_sum(q_down, core, "quad_sum_w_down")
    h_up = _quad_sum(q_up, core, "quad_sum_w_up")
    p_out = _pair_add(gw_out, o_out, core, "pair_add_w_out")
    p_glu = _pair_add(gw_glu, o_glu, core, "pair_add_w_glu")
    grad_x, dproj, dg_pre_mix = _inproj_bwd(du, dhbc, xs, dx1, w_in_all, g1)
    d_lam_re, d_lam_im, d_log_dt, d_b_re, d_b_im, d_c_re, d_c_im, d_d_skip = _ssm_param_grads(
        w["lam_re"], w["lam_im"], w["log_dt"], bt_re, bt_im, gb, gc, q, gd)
    small = {
        "g_pre_mix": dg_pre_mix[0], "lam_re": d_lam_re, "lam_im": d_lam_im, "log_dt": d_log_dt, "b_re": d_b_re, "b_im": d_b_im,
        "c_re": d_c_re, "c_im": d_c_im, "d_skip": d_d_skip, "conv_w": dconv_w[:3], "g_ssm_out": dg_ssm[0], "g_conv_out": dg_conv[0],
        "g_post_mix": dg_post_mix[0], "g_pre_mlp": dg_pre_mlp[0], "g_post_mlp": dg_post_mlp[0],
    }
    gw_in, g_down, g_up, q_out, q_glu, tiles, rows = _matmul_tn(
        hn, dproj, "dw_in", col_shards=N_CHIPS,
        exchanges=[_Share([h_down, h_up]), _Chip([p_out, p_glu]), _GatherSmall(_pack_tiles(small)), _GatherSmall(_pack_rows(small, loss))])
    h_out = _quad_sum(q_out, core, "quad_sum_w_out")
    h_glu = _quad_sum(q_glu, core, "quad_sum_w_glu")
    (o_in,) = _run_exchanges([_Pair([gw_in])], "rs_pair_w_in")
    p_in = _pair_add(gw_in, o_in, core, "pair_add_w_in")
    q_in, g_out, g_glu = _run_exchanges([_Chip([p_in]), _Share([h_out, h_glu])], "rs_chip_w_in")
    h_in = _quad_sum(q_in, core, "quad_sum_w_in")
    (g_in,) = _run_exchanges([_Share([h_in])], "rs_share_w_in")
    shard_grads = {"w_in": g_in, "w_glu": g_glu, "w_out": g_out, "w_up": g_up, "w_down": g_down}

    out = {q: {} for q in ("grad", "delta", "new_m", "new_v")}
    for k in BIG:
        out["grad"][k] = shard_grads[k][None]
        delta, new_m, new_v = _adamw(w[k], shard_grads[k], m[k], v[k], "adamw_" + k)
        out["delta"][k], out["new_m"][k], out["new_v"][k] = delta[None], new_m[None], new_v[None]
    form = lambda d: {k: _kernel_form(k, d[k]) for k in SMALL}
    loss, res = _adamw_small(tiles, rows, form(w_dev), form(m_dev), form(v_dev))
    for q, d in zip(("grad", "delta", "new_m", "new_v"), res):
        out[q].update({k: _param_form(k, d[k]) for k in SMALL})
    flat = [loss[0, 0], grad_x[None]]
    for q in ("grad", "delta", "new_m", "new_v"):
        flat += [out[q][k] for k in WEIGHTS]
    return tuple(flat)
```

```python
import functools
import math

import jax
import jax.numpy as jnp
import numpy as np
from jax import lax
from jax.experimental import pallas as pl
from jax.experimental.pallas import tpu as pltpu

F32 = jnp.float32
MXU_DTYPE = jnp.bfloat16
WIRE_DTYPE = jnp.bfloat16

D_MODEL = 1024
D_SSM = 512
D_CONV = 512
N_GROUPS = 32
GROUP = 16
STATE = 64
D_FF = 4096
RMS_EPS = 1e-6
N_CHIPS = 4
N_DEV = 8

ADAM_LR = 0.001
ADAM_B1 = 0.9
ADAM_B2 = 0.999
ADAM_EPS = 1e-08
ADAM_WD = 0.01
ADAM_STEP = 10

N_GBLK = 2
G_PER_BLK = N_GROUPS // N_GBLK
UB = G_PER_BLK * GROUP
WB = G_PER_BLK * STATE
LANE_CHUNK = 256
SUBLANES = 8
N_TABLES = 24

TM_PROJ = 512
TM_S5 = 512
TM_TAIL = 512
TM_MLP = 256
TL_TN = 2048
VMEM_LIMIT = 56 * 1024 * 1024

MESH = pl.DeviceIdType.MESH


def _params(sem, vmem=VMEM_LIMIT):
    return pltpu.CompilerParams(dimension_semantics=sem, vmem_limit_bytes=vmem)


def _resident(shape):
    nd = len(shape)
    return pl.BlockSpec(shape, lambda *_: (0,) * nd, pipeline_mode=pl.Buffered(1))


def _dot(a, b):
    return jnp.dot(a, b, preferred_element_type=F32)


def _dot_nt(a, b):
    return lax.dot_general(a, b, (((1,), (1,)), ((), ())), preferred_element_type=F32)


def _dot_tn(a, b):
    return lax.dot_general(a, b, (((0,), (0,)), ((), ())), preferred_element_type=F32)


def _rms_fwd(x, g):
    r = lax.rsqrt(jnp.mean(x * x, axis=-1, keepdims=True) + RMS_EPS)
    return x * r * g


def _rms_bwd(x, g, dy):
    r = lax.rsqrt(jnp.mean(x * x, axis=-1, keepdims=True) + RMS_EPS)
    xn = x * r
    q = dy * g
    dx = r * (q - xn * jnp.mean(q * xn, axis=-1, keepdims=True))
    return dx, jnp.sum(dy * xn, axis=0, keepdims=True)


_GELU_C = math.sqrt(2.0 / math.pi)


def _gelu(x):
    t = jnp.tanh(_GELU_C * (x + 0.044715 * (x * x * x)))
    y = x * (0.5 * (1.0 + t))
    dy = 0.5 * (1.0 + t) + 0.5 * x * (1.0 - t * t) * (_GELU_C * (1.0 + 3 * 0.044715 * (x * x)))
    return y, dy


def _tile(n, pref):
    t = min(n, pref)
    assert n % t == 0, (n, t)
    return t


HBM = pl.BlockSpec(memory_space=pltpu.HBM)
VMEM = pl.BlockSpec(memory_space=pltpu.VMEM)
DMA_SEMS = pltpu.SemaphoreType.DMA


def _place():
    x, y, c = lax.axis_index("x"), lax.axis_index("y"), lax.axis_index("c")
    chips = [(1 - x, y), (x, 1 - y), (1 - x, 1 - y)]
    return (x, y, c), 2 * x + y, (x, y, 1 - c), chips, [2 * px + py for px, py in chips]


def _remote(src, dst, send_sem, recv_sem, device):
    return pltpu.make_async_remote_copy(src_ref=src, dst_ref=dst, send_sem=send_sem, recv_sem=recv_sem,
                                        device_id=device, device_id_type=MESH)


def _half(rows, c):
    return pl.ds(c * (rows // 2), rows // 2)


class _Exchange:
    aliases = {}

    def start(self, ins, outs, sems):
        local, outgoing, _ = self._copies(ins, outs, sems)
        for cp in local + outgoing:
            cp.start()

    def finish(self, ins, outs, sems):
        local, outgoing, incoming = self._copies(ins, outs, sems)
        for cp in incoming:
            cp.wait_recv()
        for cp in outgoing:
            cp.wait_send()
        for cp in local:
            cp.wait()


class _Gather(_Exchange):
    def __init__(self, shards, split):
        self.inputs, self.split = list(shards), split
        self.out_shape = [jax.ShapeDtypeStruct((N_CHIPS, *a.shape), a.dtype) for a in shards]
        self.sems = [DMA_SEMS((len(shards), 3)), DMA_SEMS((len(shards), 3)), DMA_SEMS((len(shards),))]

    def _copies(self, ins, outs, sems):
        send, recv, lsem = sems
        (x, y, c), me, sibling, chips, ids = _place()
        local = [pltpu.make_async_copy(ins[t], outs[t].at[me], lsem.at[t]) for t in range(len(ins))]
        outgoing, incoming = [], []
        for t, a in enumerate(self.inputs):
            rows = _half(a.shape[0], c) if self.split[t] else pl.ds(0, a.shape[0])
            for k in range(3):
                to = (*chips[k], c)
                outgoing.append(_remote(ins[t].at[rows, :], outs[t].at[me, rows, :], send.at[t, k], recv.at[t, k], to))
                incoming.append(_remote(ins[t].at[rows, :], outs[t].at[ids[k], rows, :], send.at[t, k], recv.at[t, k], to))
        return local, outgoing, incoming


class _Forward(_Exchange):
    def __init__(self, arrays):
        self.inputs = list(arrays)
        self.out_shape = [jax.ShapeDtypeStruct(a.shape, a.dtype) for a in arrays]
        self.aliases = {t: t for t in range(len(arrays))}
        self.sems = [DMA_SEMS((len(arrays), 3)), DMA_SEMS((len(arrays), 3))]

    def _copies(self, ins, outs, sems):
        send, recv = sems
        (x, y, c), me, sibling, chips, ids = _place()
        outgoing, incoming = [], []
        for t, a in enumerate(self.inputs):
            for k in range(3):
                mine = outs[t].at[ids[k], _half(a.shape[1], c), :]
                theirs = outs[t].at[ids[k], _half(a.shape[1], 1 - c), :]
                outgoing.append(_remote(mine, mine, send.at[t, k], recv.at[t, k], sibling))
                incoming.append(_remote(theirs, theirs, send.at[t, k], recv.at[t, k], sibling))
        return [], outgoing, incoming


class _GatherForward(_Exchange):
    def __init__(self, shards):
        self.gather = _Gather(shards, [True] * len(shards))
        self.forward = _Forward(self.gather.out_shape)
        self.inputs, self.out_shape = self.gather.inputs, self.gather.out_shape
        self.sems = self.gather.sems + self.forward.sems

    def start(self, ins, outs, sems):
        self.gather.start(ins, outs, sems[:3])

    def finish(self, ins, outs, sems):
        local, outgoing, incoming = self.gather._copies(ins, outs, sems[:3])
        _, passed, from_sibling = self.forward._copies(outs, outs, sems[3:])
        for landed, onward in zip(incoming, passed):
            landed.wait_recv()
            onward.start()
        for cp in from_sibling:
            cp.wait_recv()
        for cp in outgoing + passed:
            cp.wait_send()
        for cp in local:
            cp.wait()


class _Pair(_Exchange):
    def __init__(self, grads):
        self.inputs = list(grads)
        self.out_shape = [jax.ShapeDtypeStruct((g.shape[0], g.shape[1] // 2, g.shape[2]), g.dtype) for g in grads]
        self.sems = [DMA_SEMS((len(grads),)), DMA_SEMS((len(grads),))]

    def _copies(self, ins, outs, sems):
        send, recv = sems
        (x, y, c), me, sibling, chips, ids = _place()
        cps = [_remote(ins[t].at[:, _half(g.shape[1], 1 - c), :], outs[t], send.at[t], recv.at[t], sibling)
               for t, g in enumerate(self.inputs)]
        return [], cps, cps


class _Chip(_Exchange):
    def __init__(self, parts):
        self.inputs = list(parts)
        self.out_shape = [jax.ShapeDtypeStruct(p.shape, p.dtype) for p in parts]
        self.sems = [DMA_SEMS((len(parts), 3)), DMA_SEMS((len(parts), 3)), DMA_SEMS((len(parts),))]

    def _copies(self, ins, outs, sems):
        send, recv, lsem = sems
        (x, y, c), me, sibling, chips, ids = _place()
        local = [pltpu.make_async_copy(ins[t].at[me], outs[t].at[me], lsem.at[t]) for t in range(len(ins))]
        outgoing, incoming = [], []
        for t in range(len(ins)):
            for k in range(3):
                to = (*chips[k], c)
                outgoing.append(_remote(ins[t].at[ids[k]], outs[t].at[me], send.at[t, k], recv.at[t, k], to))
                incoming.append(_remote(ins[t].at[ids[k]], outs[t].at[ids[k]], send.at[t, k], recv.at[t, k], to))
        return local, outgoing, incoming


class _Share(_Exchange):
    def __init__(self, grads):
        self.inputs = list(grads)
        self.out_shape = [jax.ShapeDtypeStruct(g.shape, g.dtype) for g in grads]
        self.aliases = {t: t for t in range(len(grads))}
        self.sems = [DMA_SEMS((len(grads),)), DMA_SEMS((len(grads),))]

    def _copies(self, ins, outs, sems):
        send, recv = sems
        (x, y, c), me, sibling, chips, ids = _place()
        outgoing, incoming = [], []
        for t, g in enumerate(self.inputs):
            mine = outs[t].at[_half(g.shape[0], c), :]
            theirs = outs[t].at[_half(g.shape[0], 1 - c), :]
            outgoing.append(_remote(mine, mine, send.at[t], recv.at[t], sibling))
            incoming.append(_remote(theirs, theirs, send.at[t], recv.at[t], sibling))
        return [], outgoing, incoming


class _GatherSmall(_Exchange):
    def __init__(self, block):
        self.inputs = [block]
        self.out_shape = [jax.ShapeDtypeStruct((N_DEV, *block.shape), block.dtype)]
        self.sems = [DMA_SEMS((7,)), DMA_SEMS((7,)), DMA_SEMS(())]

    def _copies(self, ins, outs, sems):
        send, recv, lsem = sems
        (x, y, c), me, sibling, chips, ids = _place()
        slot = lambda px, py, pc: outs[0].at[4 * px + 2 * py + pc]

        def copy(k, block, to, src=None):
            return _remote(slot(*block) if src is None else src, slot(*block), send.at[k], recv.at[k], to)

        local = [pltpu.make_async_copy(ins[0], slot(x, y, c), lsem)]
        first = [copy(0, (x, y, c), sibling, src=ins[0])] + [copy(1 + j, (x, y, c), (*chip, c), src=ins[0]) for j, chip in enumerate(chips)]
        passed = [copy(4 + j, (*chip, c), sibling) for j, chip in enumerate(chips)]
        landed = [copy(1 + j, (*chip, c), (x, y, c)) for j, chip in enumerate(chips)]
        from_sibling = [copy(0, (x, y, 1 - c), (x, y, c))] + [copy(4 + j, (*chip, 1 - c), (x, y, c)) for j, chip in enumerate(chips)]
        return local, first, (passed, landed, from_sibling)

    def finish(self, ins, outs, sems):
        local, first, (passed, landed, from_sibling) = self._copies(ins, outs, sems)
        for j in range(3):
            landed[j].wait_recv()
            passed[j].start()
        for cp in from_sibling:
            cp.wait_recv()
        for cp in first + passed:
            cp.wait_send()
        for cp in local:
            cp.wait()


def _split_refs(refs, counts):
    out = []
    for n in counts:
        out.append(refs[:n])
        refs = refs[n:]
    return out


def _each_exchange(exchanges, method, x_in, x_out, x_sem):
    for ex in exchanges:
        ni, no, ns = len(ex.inputs), len(ex.out_shape), len(ex.sems)
        getattr(ex, method)(x_in[:ni], x_out[:no], x_sem[:ns])
        x_in, x_out, x_sem = x_in[ni:], x_out[no:], x_sem[ns:]


def _call(body, *, name, grid, in_specs, out_specs, out_shape, operands, semantics, scratch_shapes=(), exchanges=()):
    x_in = [a for ex in exchanges for a in ex.inputs]
    x_out = [s for ex in exchanges for s in ex.out_shape]
    x_sem = [s for ex in exchanges for s in ex.sems]
    counts = (len(in_specs), len(x_in), len(out_specs), len(x_out), len(scratch_shapes), len(x_sem))
    aliases, i0, o0 = {}, len(in_specs), len(out_specs)
    for ex in exchanges:
        aliases.update({i0 + i: o0 + o for i, o in ex.aliases.items()})
        i0, o0 = i0 + len(ex.inputs), o0 + len(ex.out_shape)

    def full_body(*refs):
        ins, xi, outs, xo, scr, xs = _split_refs(list(refs), counts)
        if exchanges:
            @pl.when(functools.reduce(jnp.logical_and, [pl.program_id(a) == 0 for a in range(len(grid))]))
            def _():
                _each_exchange(exchanges, "start", xi, xo, xs)

        body(*ins, *outs, *scr)
        if exchanges:
            @pl.when(functools.reduce(jnp.logical_and, [pl.program_id(a) == grid[a] - 1 for a in range(len(grid))]))
            def _():
                _each_exchange(exchanges, "finish", xi, xo, xs)

    return pl.pallas_call(
        full_body, name=name, grid=grid,
        in_specs=list(in_specs) + [HBM] * len(x_in), out_specs=list(out_specs) + [HBM] * len(x_out),
        out_shape=list(out_shape) + x_out, scratch_shapes=list(scratch_shapes) + x_sem,
        input_output_aliases=aliases, compiler_params=_params(semantics),
    )(*operands, *x_in)


def _run_exchanges(exchanges, name):
    x_in = [a for ex in exchanges for a in ex.inputs]
    x_out = [s for ex in exchanges for s in ex.out_shape]
    x_sem = [s for ex in exchanges for s in ex.sems]
    aliases, i0, o0 = {}, 0, 0
    for ex in exchanges:
        aliases.update({i0 + i: o0 + o for i, o in ex.aliases.items()})
        i0, o0 = i0 + len(ex.inputs), o0 + len(ex.out_shape)

    def body(*refs):
        xi, xo, xs = _split_refs(list(refs), (len(x_in), len(x_out), len(x_sem)))
        _each_exchange(exchanges, "start", xi, xo, xs)
        _each_exchange(exchanges, "finish", xi, xo, xs)

    return pl.pallas_call(
        body, name=name, in_specs=[HBM] * len(x_in), out_specs=[HBM] * len(x_out), out_shape=x_out,
        scratch_shapes=x_sem, input_output_aliases=aliases,
    )(*x_in)


def _inproj_fwd(x, g1, w_in_all, exchanges=()):
    L, D = x.shape
    ns, _, nc = w_in_all.shape
    tm = _tile(L, TM_PROJ)

    def body(x_ref, g_ref, w_ref, hn_ref, proj_ref, u_ref):
        hn = _rms_fwd(x_ref[...], g_ref[...]).astype(MXU_DTYPE)
        hn_ref[...] = hn
        for j in range(ns):
            proj_ref[:, j * nc:(j + 1) * nc] = _dot(hn, w_ref[j])
        _store_slabs(u_ref, proj_ref[:, 0:nc])

    return _call(
        body, name="inproj_fwd", grid=(L // tm,), exchanges=exchanges, semantics=("arbitrary",), operands=(x, g1, w_in_all),
        in_specs=[pl.BlockSpec((tm, D), lambda i: (i, 0)), _resident((1, D)), _resident(w_in_all.shape)],
        out_specs=[pl.BlockSpec((tm, D), lambda i: (i, 0)), pl.BlockSpec((tm, ns * nc), lambda i: (i, 0)), _slab_spec(nc, tm)],
        out_shape=[jax.ShapeDtypeStruct((L, D), MXU_DTYPE), jax.ShapeDtypeStruct((L, ns * nc), F32), _slab_shape(L, nc)],
    )


def _slab_shape(L, n):
    return jax.ShapeDtypeStruct((n // LANES, L, LANES), F32)


def _slab_spec(n, tm, index=lambda i: (0, i, 0)):
    return pl.BlockSpec((n // LANES, tm, LANES), index)


def _store_slabs(ref, value):
    for k in range(ref.shape[0]):
        ref[k] = value[:, k * LANES:(k + 1) * LANES]


def _load_slabs(ref):
    return jnp.concatenate([ref[k] for k in range(ref.shape[0])], axis=1)


SEG_ROWS = SUBLANES * SUBLANES


def _load_permuted(ref):
    tm = ref.shape[1]
    slabs = []
    for k in range(ref.shape[0]):
        tiles = [ref.at[k][pl.ds(b * SEG_ROWS + j, SUBLANES, stride=SUBLANES), :] for b in range(tm // SEG_ROWS) for j in range(SUBLANES)]
        slabs.append(jnp.concatenate(tiles, axis=0))
    return jnp.concatenate(slabs, axis=1)


def _store_permuted(ref, value):
    tm = ref.shape[1]
    for k in range(ref.shape[0]):
        for b in range(tm // SEG_ROWS):
            for j in range(SUBLANES):
                r = b * SEG_ROWS + j * SUBLANES
                ref.at[k][pl.ds(b * SEG_ROWS + j, SUBLANES, stride=SUBLANES), :] = value[r:r + SUBLANES, k * LANES:(k + 1) * LANES]


def _scan_tile(xr, xi, hr, hi, coef_ref, lanes, reverse):
    for k, j in ((1, 0), (2, 2), (4, 4)):
        ar = coef_ref[j, :, lanes]
        ai = coef_ref[j + 1, :, lanes]
        shift = SUBLANES - k if reverse else k
        sr = pltpu.roll(xr, shift, 0)
        si = pltpu.roll(xi, shift, 0)
        xr, xi = xr + (ar * sr - ai * si), xi + (ar * si + ai * sr)
    pr = coef_ref[6, :, lanes]
    pi = coef_ref[7, :, lanes]
    return xr + (pr * hr - pi * hi), xi + (pr * hi + pi * hr)


def _scan_block(read, write, hr, hi, coef_ref, lanes, reverse):
    order = list(range(SUBLANES - 1, -1, -1) if reverse else range(SUBLANES))
    near = 8 + 2 * order[0]
    ar = coef_ref[near, :, lanes]
    ai = coef_ref[near + 1, :, lanes]
    xr, xi = read(order[0])
    local = {order[0]: (xr, xi)}
    for j in order[1:]:
        br, bi = read(j)
        xr, xi = br + (ar * xr - ai * xi), bi + (ar * xi + ai * xr)
        local[j] = (xr, xi)
    er, ei = _scan_tile(xr, xi, hr, hi, coef_ref, lanes, reverse)
    edge = lax.broadcasted_iota(jnp.int32, er.shape, 0) == (SUBLANES - 1 if reverse else 0)
    shift = SUBLANES - 1 if reverse else 1
    pr = jnp.where(edge, hr, pltpu.roll(er, shift, 0))
    pi = jnp.where(edge, hi, pltpu.roll(ei, shift, 0))
    for j in range(SUBLANES):
        cr = coef_ref[8 + 2 * j, :, lanes]
        ci = coef_ref[9 + 2 * j, :, lanes]
        xr, xi = local[j]
        write(j, xr + (cr * pr - ci * pi), xi + (cr * pi + ci * pr))
    end = 0 if reverse else SUBLANES - 1
    return jnp.broadcast_to(er[end:end + 1, :], er.shape), jnp.broadcast_to(ei[end:end + 1, :], ei.shape)


def _s5_fwd(u4, bmat, cmat, coef, dskip, exchanges=()):
    L = u4.shape[1]
    tm = _tile(L, TM_S5)
    lc = min(LANE_CHUNK, WB)

    def body(u_ref, bm_ref, cm_ref, coef_ref, d_ref, sre_ref, sim_ref, ys_ref, hr_ref, hi_ref):
        @pl.when(pl.program_id(1) == 0)
        def _():
            hr_ref[...] = jnp.zeros_like(hr_ref)
            hi_ref[...] = jnp.zeros_like(hi_ref)

        u = _load_permuted(u_ref)
        bu = _dot(u.astype(MXU_DTYPE), bm_ref[0])
        sre_ref[...] = bu[:, :WB]
        sim_ref[...] = bu[:, WB:]
        for c in range(WB // lc):
            lanes = slice(c * lc, (c + 1) * lc)
            hr, hi = hr_ref[:, lanes], hi_ref[:, lanes]
            for b in range(tm // SEG_ROWS):
                rows = lambda j, b=b: slice(b * SEG_ROWS + j * SUBLANES, b * SEG_ROWS + (j + 1) * SUBLANES)

                def read(j, rows=rows, lanes=lanes):
                    return sre_ref[rows(j), lanes], sim_ref[rows(j), lanes]

                def write(j, xr, xi, rows=rows, lanes=lanes):
                    sre_ref[rows(j), lanes] = xr
                    sim_ref[rows(j), lanes] = xi

                hr, hi = _scan_block(read, write, hr, hi, coef_ref, lanes, False)
            hr_ref[:, lanes] = hr
            hi_ref[:, lanes] = hi
        ys = _dot_nt(sre_ref[...].astype(MXU_DTYPE), cm_ref[0, :, :WB]) + _dot_nt(sim_ref[...].astype(MXU_DTYPE), cm_ref[0, :, WB:])
        _store_permuted(ys_ref, ys + d_ref[0] * u)

    return _call(
        body, name="s5_fwd", grid=(N_GBLK, L // tm), exchanges=exchanges, semantics=("arbitrary", "arbitrary"),
        operands=(u4, bmat, cmat, coef, dskip),
        in_specs=[
            _slab_spec(UB, tm, lambda b, i: (b, i, 0)),
            pl.BlockSpec((1, UB, 2 * WB), lambda b, i: (b, 0, 0)),
            pl.BlockSpec((1, UB, 2 * WB), lambda b, i: (b, 0, 0)),
            pl.BlockSpec((N_TABLES, SUBLANES, WB), lambda b, i: (0, 0, b)),
            pl.BlockSpec((1, 1, UB), lambda b, i: (b, 0, 0)),
        ],
        out_specs=[
            pl.BlockSpec((tm, WB), lambda b, i: (i, b)),
            pl.BlockSpec((tm, WB), lambda b, i: (i, b)),
            _slab_spec(UB, tm, lambda b, i: (b, i, 0)),
        ],
        out_shape=[
            jax.ShapeDtypeStruct((L, N_GBLK * WB), F32),
            jax.ShapeDtypeStruct((L, N_GBLK * WB), F32),
            _slab_shape(L, D_SSM),
        ],
        scratch_shapes=[pltpu.VMEM((SUBLANES, WB), F32), pltpu.VMEM((SUBLANES, WB), F32)],
    )


def _tail_fwd(x, ys, proj, w_glu, conv_w, g_ssm, g_conv, w_out, g_post, exchanges=()):
    L, D = x.shape
    tm = _tile(L, TM_TAIL)

    def body(x_ref, ys_ref, h_ref, bg_ref, cg_ref, wglu_ref, cw_ref, gs_ref, gc_ref, wout_ref, gp_ref,
             ycat_ref, o_ref, x1_ref, zbuf):
        @pl.when(pl.program_id(0) == 0)
        def _():
            zbuf[0:SUBLANES, :] = jnp.zeros((SUBLANES, D_CONV), F32)

        y1, _ = _gelu(_load_slabs(ys_ref))
        y2 = y1 * jax.nn.sigmoid(_dot(y1.astype(MXU_DTYPE), wglu_ref[...]))
        ycat_ref[:, :D_SSM] = _rms_fwd(y2, gs_ref[...]).astype(MXU_DTYPE)
        z = cg_ref[...] * h_ref[...]
        zbuf[SUBLANES:, :] = z
        conv = cw_ref[0:1, :] * zbuf[SUBLANES - 2:SUBLANES - 2 + tm, :] + cw_ref[1:2, :] * zbuf[SUBLANES - 1:SUBLANES - 1 + tm, :] + cw_ref[2:3, :] * z
        zbuf[0:SUBLANES, :] = zbuf[tm:tm + SUBLANES, :]
        ycat_ref[:, D_SSM:] = _rms_fwd(bg_ref[...] * conv, gc_ref[...]).astype(MXU_DTYPE)
        o = _dot(ycat_ref[...], wout_ref[...])
        o_ref[...] = o
        x1_ref[...] = x_ref[...] + _rms_fwd(o, gp_ref[...])

    row = lambda i: (i, 0)
    return _call(
        body, name="tail_fwd", grid=(L // tm,), exchanges=exchanges, semantics=("arbitrary",),
        operands=(x, ys, proj, proj, proj, w_glu, conv_w, g_ssm, g_conv, w_out, g_post),
        in_specs=[
            pl.BlockSpec((tm, D), row), _slab_spec(D_SSM, tm),
            pl.BlockSpec((tm, D_CONV), lambda i: (i, 1)), pl.BlockSpec((tm, D_CONV), lambda i: (i, 2)),
            pl.BlockSpec((tm, D_CONV), lambda i: (i, 3)),
            _resident(w_glu.shape), _resident(conv_w.shape), _resident(g_ssm.shape), _resident(g_conv.shape),
            _resident(w_out.shape), _resident(g_post.shape),
        ],
        out_specs=[pl.BlockSpec((tm, D), row), pl.BlockSpec((tm, D), row), pl.BlockSpec((tm, D), row)],
        out_shape=[jax.ShapeDtypeStruct((L, D), MXU_DTYPE), jax.ShapeDtypeStruct((L, D), F32), jax.ShapeDtypeStruct((L, D), F32)],
        scratch_shapes=[pltpu.VMEM((tm + SUBLANES, D_CONV), F32)],
    )


def _mlp_fwd(x1, target, w_up_all, w_down_a, w_down_b, g_pre, g_post):
    L, D = x1.shape
    ns, _, fc = w_up_all.shape
    half = w_down_a.shape[1]
    tm = _tile(L, TM_MLP)

    def body(x1_ref, t_ref, wup_ref, wda_ref, wdb_ref, gpre_ref, gpost_ref, hn2_ref, up_ref, m_ref, dx2_ref, loss_ref):
        @pl.when(pl.program_id(0) == 0)
        def _():
            loss_ref[...] = jnp.zeros_like(loss_ref)

        x1v = x1_ref[...]
        hn2 = _rms_fwd(x1v, gpre_ref[...]).astype(MXU_DTYPE)
        hn2_ref[...] = hn2
        m = jnp.zeros((tm, D), F32)
        for j in range(ns):
            up = _dot(hn2, wup_ref[j])
            up_ref[:, j * fc:(j + 1) * fc] = up
            act = jnp.square(jnp.maximum(up, 0.0)).astype(MXU_DTYPE)
            m = m + _dot(act[:, :half], wda_ref[j]) + _dot(act[:, half:], wdb_ref[j])
        m_ref[...] = m
        err = x1v + _rms_fwd(m, gpost_ref[...]) - t_ref[...]
        loss_ref[...] += 0.5 * jnp.sum(jnp.mean(err * err, axis=-1, keepdims=True))
        dx2_ref[...] = err * (1.0 / D)

    row = lambda i: (i, 0)
    return pl.pallas_call(
        body, name="mlp_fwd", grid=(L // tm,),
        in_specs=[pl.BlockSpec((tm, D), row), pl.BlockSpec((tm, D), row), _resident(w_up_all.shape), _resident(w_down_a.shape),
                  _resident(w_down_b.shape), _resident(g_pre.shape), _resident(g_post.shape)],
        out_specs=[pl.BlockSpec((tm, D), row), pl.BlockSpec((tm, ns * fc), row), pl.BlockSpec((tm, D), row),
                   pl.BlockSpec((tm, D), row), pl.BlockSpec((SUBLANES, 128), lambda i: (0, 0))],
        out_shape=[jax.ShapeDtypeStruct((L, D), MXU_DTYPE), jax.ShapeDtypeStruct((L, ns * fc), F32), jax.ShapeDtypeStruct((L, D), F32),
                   jax.ShapeDtypeStruct((L, D), F32), jax.ShapeDtypeStruct((SUBLANES, 128), F32)],
        compiler_params=_params(("arbitrary",)),
    )(x1, target, w_up_all, w_down_a, w_down_b, g_pre, g_post)


def _mlp_bwd(dx2, m, up, x1, w_up_all, w_down_a, w_down_b, g_pre, g_post):
    L, D = x1.shape
    ns, _, fc = w_up_all.shape
    tm = _tile(L, TM_MLP)

    def body(dx2_ref, m_ref, up_ref, x1_ref, wup_ref, wda_ref, wdb_ref, gpre_ref, gpost_ref,
             dm_ref, dup_ref, act_ref, dx1_ref, dgpost_ref, dgpre_ref):
        @pl.when(pl.program_id(0) == 0)
        def _():
            dgpost_ref[...] = jnp.zeros_like(dgpost_ref)
            dgpre_ref[...] = jnp.zeros_like(dgpre_ref)

        dx2v = dx2_ref[...]
        dm, dg = _rms_bwd(m_ref[...], gpost_ref[...], dx2v)
        dgpost_ref[...] += dg
        dm_b = dm.astype(MXU_DTYPE)
        dm_ref[...] = dm_b
        dhn2 = jnp.zeros((tm, D), F32)
        for j in range(ns):
            cols = slice(j * fc, (j + 1) * fc)
            relu = jnp.maximum(up_ref[:, cols], 0.0)
            act_ref[:, cols] = jnp.square(relu).astype(MXU_DTYPE)
            dact = jnp.concatenate([_dot_nt(dm_b, wda_ref[j]), _dot_nt(dm_b, wdb_ref[j])], axis=1)
            dup = (dact * (2.0 * relu)).astype(MXU_DTYPE)
            dup_ref[:, cols] = dup
            dhn2 = dhn2 + _dot_nt(dup, wup_ref[j])
        dx, dg = _rms_bwd(x1_ref[...], gpre_ref[...], dhn2)
        dgpre_ref[...] += dg
        dx1_ref[...] = dx2v + dx

    row = lambda i: (i, 0)
    vec = pl.BlockSpec((1, D), lambda i: (0, 0))
    return pl.pallas_call(
        body, name="mlp_bwd", grid=(L // tm,),
        in_specs=[pl.BlockSpec((tm, D), row), pl.BlockSpec((tm, D), row), pl.BlockSpec((tm, ns * fc), row), pl.BlockSpec((tm, D), row),
                  _resident(w_up_all.shape), _resident(w_down_a.shape), _resident(w_down_b.shape), _resident(g_pre.shape), _resident(g_post.shape)],
        out_specs=[pl.BlockSpec((tm, D), row), pl.BlockSpec((tm, ns * fc), row), pl.BlockSpec((tm, ns * fc), row),
                   pl.BlockSpec((tm, D), row), vec, vec],
        out_shape=[jax.ShapeDtypeStruct((L, D), MXU_DTYPE), jax.ShapeDtypeStruct((L, ns * fc), MXU_DTYPE),
                   jax.ShapeDtypeStruct((L, ns * fc), MXU_DTYPE), jax.ShapeDtypeStruct((L, D), F32),
                   jax.ShapeDtypeStruct((1, D), F32), jax.ShapeDtypeStruct((1, D), F32)],
        compiler_params=_params(("arbitrary",)),
    )(dx2, m, up, x1, w_up_all, w_down_a, w_down_b, g_pre, g_post)


def _tail_bwd(dx1, o, ys, proj, w_glu, conv_w, g_ssm, g_conv, w_out, g_post, exchanges=()):
    L, D = dx1.shape
    tm = _tile(L, TM_TAIL)
    nt = L // tm
    hb = tm // SUBLANES

    def body(dx1_ref, o_ref, ys_ref, h_ref, bg_ref, cg_ref, hh_ref, hcg_ref, wglu_ref, cw_ref, gs_ref, gc_ref, wout_ref, gp_ref,
             do_ref, da_ref, y1_ref, dys_ref, dhbc_ref, dgp_ref, dgs_ref, dgc_ref, dcw_ref, zbuf, dcbuf):
        step = pl.program_id(0)

        @pl.when(step == 0)
        def _():
            dcbuf[tm:, :] = jnp.zeros((SUBLANES, D_CONV), F32)
            dgp_ref[...] = jnp.zeros_like(dgp_ref)
            dgs_ref[...] = jnp.zeros_like(dgs_ref)
            dgc_ref[...] = jnp.zeros_like(dgc_ref)
            dcw_ref[...] = jnp.zeros_like(dcw_ref)

        do, dg = _rms_bwd(o_ref[...], gp_ref[...], dx1_ref[...])
        dgp_ref[...] += dg
        do_b = do.astype(MXU_DTYPE)
        do_ref[...] = do_b
        dycat = _dot_nt(do_b, wout_ref[...])
        y1, dgelu = _gelu(_load_slabs(ys_ref))
        y1_b = y1.astype(MXU_DTYPE)
        y1_ref[...] = y1_b
        s = jax.nn.sigmoid(_dot(y1_b, wglu_ref[...]))
        dy2, dg = _rms_bwd(y1 * s, gs_ref[...], dycat[:, :D_SSM])
        dgs_ref[...] += dg
        da_b = (dy2 * y1 * s * (1.0 - s)).astype(MXU_DTYPE)
        da_ref[...] = da_b
        _store_slabs(dys_ref, (dy2 * s + _dot_nt(da_b, wglu_ref[...])) * dgelu)
        h = h_ref[...]
        cg = cg_ref[...]
        bg = bg_ref[...]
        z = cg * h
        first = step == nt - 1
        zbuf[0:SUBLANES, :] = jnp.where(first, 0.0, hcg_ref[...] * hh_ref[...])
        zbuf[SUBLANES:, :] = z
        z1 = zbuf[SUBLANES - 1:SUBLANES - 1 + tm, :]
        z2 = zbuf[SUBLANES - 2:SUBLANES - 2 + tm, :]
        conv = cw_ref[0:1, :] * z2 + cw_ref[1:2, :] * z1 + cw_ref[2:3, :] * z
        dyc, dg = _rms_bwd(bg * conv, gc_ref[...], dycat[:, D_SSM:])
        dgc_ref[...] += dg
        dconv = dyc * bg
        dcw_ref[0:1, :] += jnp.sum(dconv * z2, axis=0, keepdims=True)
        dcw_ref[1:2, :] += jnp.sum(dconv * z1, axis=0, keepdims=True)
        dcw_ref[2:3, :] += jnp.sum(dconv * z, axis=0, keepdims=True)
        dcbuf[0:tm, :] = dconv
        dz = cw_ref[2:3, :] * dconv + cw_ref[1:2, :] * dcbuf[1:1 + tm, :] + cw_ref[0:1, :] * dcbuf[2:2 + tm, :]
        dcbuf[tm:, :] = dcbuf[0:SUBLANES, :]
        dhbc_ref[:, 0:D_CONV] = (dz * cg).astype(MXU_DTYPE)
        dhbc_ref[:, D_CONV:2 * D_CONV] = (dyc * conv).astype(MXU_DTYPE)
        dhbc_ref[:, 2 * D_CONV:] = (dz * h).astype(MXU_DTYPE)

    rev = lambda i: (nt - 1 - i, 0)
    slab = lambda i: (0, nt - 1 - i, 0)
    col = lambda c: (lambda i: (nt - 1 - i, c))
    halo = lambda c: (lambda i: (jnp.maximum((nt - 1 - i) * hb - 1, 0), c))
    vec = lambda n: pl.BlockSpec((1, n), lambda i: (0, 0))
    return _call(
        body, name="tail_bwd", grid=(nt,), exchanges=exchanges, semantics=("arbitrary",),
        operands=(dx1, o, ys, proj, proj, proj, proj, proj, w_glu, conv_w, g_ssm, g_conv, w_out, g_post),
        in_specs=[
            pl.BlockSpec((tm, D), rev), pl.BlockSpec((tm, D), rev), _slab_spec(D_SSM, tm, slab),
            pl.BlockSpec((tm, D_CONV), col(1)), pl.BlockSpec((tm, D_CONV), col(2)), pl.BlockSpec((tm, D_CONV), col(3)),
            pl.BlockSpec((SUBLANES, D_CONV), halo(1)), pl.BlockSpec((SUBLANES, D_CONV), halo(3)),
            _resident(w_glu.shape), _resident(conv_w.shape), _resident(g_ssm.shape), _resident(g_conv.shape),
            _resident(w_out.shape), _resident(g_post.shape),
        ],
        out_specs=[
            pl.BlockSpec((tm, D), rev), pl.BlockSpec((tm, D_SSM), rev), pl.BlockSpec((tm, D_SSM), rev), _slab_spec(D_SSM, tm, slab),
            pl.BlockSpec((tm, 3 * D_CONV), rev), vec(D), vec(D_SSM), vec(D_CONV),
            pl.BlockSpec((SUBLANES, D_CONV), lambda i: (0, 0)),
        ],
        out_shape=[
            jax.ShapeDtypeStruct((L, D), MXU_DTYPE), jax.ShapeDtypeStruct((L, D_SSM), MXU_DTYPE), jax.ShapeDtypeStruct((L, D_SSM), MXU_DTYPE),
            _slab_shape(L, D_SSM), jax.ShapeDtypeStruct((L, 3 * D_CONV), MXU_DTYPE),
            jax.ShapeDtypeStruct((1, D), F32), jax.ShapeDtypeStruct((1, D_SSM), F32), jax.ShapeDtypeStruct((1, D_CONV), F32),
            jax.ShapeDtypeStruct((SUBLANES, D_CONV), F32),
        ],
        scratch_shapes=[pltpu.VMEM((tm + SUBLANES, D_CONV), F32), pltpu.VMEM((tm + SUBLANES, D_CONV), F32)],
    )


def _s5_bwd(dys, u4, s_re, s_im, bmat, cmat, coef_rev, dskip, exchanges=()):
    L = dys.shape[1]
    tm = _tile(L, TM_S5)
    nt = L // tm
    lc = min(LANE_CHUNK, WB)

    def body(dys_ref, u_ref, sre_ref, sim_ref, bm_ref, cm_ref, coef_ref, d_ref,
             du_ref, gb_ref, gc_ref, q_ref, gd_ref, dr_ref, di_ref, lr_ref, li_ref, hr_ref, hi_ref, qr_acc, qi_acc, gb_acc, gc_acc):
        step = pl.program_id(1)

        @pl.when(step == 0)
        def _():
            for ref in (hr_ref, hi_ref, qr_acc, qi_acc, gb_acc, gc_acc, gd_ref):
                ref[...] = jnp.zeros_like(ref)

        dys_v = _load_permuted(dys_ref)
        u = _load_permuted(u_ref)
        dys_b = dys_v.astype(MXU_DTYPE)
        u_b = u.astype(MXU_DTYPE)
        d = _dot(dys_b, cm_ref[0])
        dr_ref[...] = d[:, :WB]
        di_ref[...] = d[:, WB:]
        for c in range(WB // lc):
            lanes = slice(c * lc, (c + 1) * lc)
            hr, hi = hr_ref[:, lanes], hi_ref[:, lanes]
            q = [qr_acc[:, lanes], qi_acc[:, lanes]]
            for b in range(tm // SEG_ROWS - 1, -1, -1):
                rows = lambda j, b=b: slice(b * SEG_ROWS + j * SUBLANES, b * SEG_ROWS + (j + 1) * SUBLANES)

                def read(j, rows=rows, lanes=lanes):
                    return dr_ref[rows(j), lanes], di_ref[rows(j), lanes]

                def write(j, xr, xi, rows=rows, lanes=lanes, q=q):
                    lr_ref[rows(j), lanes] = xr
                    li_ref[rows(j), lanes] = xi
                    er = xr - dr_ref[rows(j), lanes]
                    ei = xi - di_ref[rows(j), lanes]
                    sr = sre_ref[rows(j), lanes]
                    si = sim_ref[rows(j), lanes]
                    q[0] = q[0] + (er * sr + ei * si)
                    q[1] = q[1] + (ei * sr - er * si)

                hr, hi = _scan_block(read, write, hr, hi, coef_ref, lanes, True)
            hr_ref[:, lanes] = hr
            hi_ref[:, lanes] = hi
            qr_acc[:, lanes] = q[0]
            qi_acc[:, lanes] = q[1]
        lr_b = lr_ref[...].astype(MXU_DTYPE)
        li_b = li_ref[...].astype(MXU_DTYPE)
        _store_permuted(du_ref, _dot_nt(lr_b, bm_ref[0, :, :WB]) + _dot_nt(li_b, bm_ref[0, :, WB:]) + d_ref[0] * dys_v)
        gb_acc[:, :WB] += _dot_tn(u_b, lr_b)
        gb_acc[:, WB:] += _dot_tn(u_b, li_b)
        gc_acc[:, :WB] += _dot_tn(dys_b, sre_ref[...].astype(MXU_DTYPE))
        gc_acc[:, WB:] += _dot_tn(dys_b, sim_ref[...].astype(MXU_DTYPE))
        gd_ref[0] += jnp.sum(dys_v * u, axis=0, keepdims=True)

        @pl.when(step == nt - 1)
        def _():
            q_ref[0, 0:1, :] = jnp.sum(qr_acc[...], axis=0, keepdims=True)
            q_ref[0, 1:2, :] = jnp.sum(qi_acc[...], axis=0, keepdims=True)
            mask = _group_mask(UB, WB)
            fold = (lax.broadcasted_iota(jnp.int32, (WB, STATE), 0) % STATE == lax.broadcasted_iota(jnp.int32, (WB, STATE), 1)).astype(F32)
            for acc, out in ((gb_acc, gb_ref), (gc_acc, gc_ref)):
                for k in range(2):
                    own = jnp.where(mask, acc[:, k * WB:(k + 1) * WB], 0.0)
                    out[0, k] = jnp.dot(own, fold, precision=lax.Precision.HIGHEST, preferred_element_type=F32)

    rev = lambda b, i: (nt - 1 - i, b)
    slab = lambda b, i: (b, nt - 1 - i, 0)
    blk = lambda b, i: (b, 0, 0)
    blk4 = lambda b, i: (b, 0, 0, 0)
    return _call(
        body, name="s5_bwd", grid=(N_GBLK, nt), exchanges=exchanges, semantics=("arbitrary", "arbitrary"),
        operands=(dys, u4, s_re, s_im, bmat, cmat, coef_rev, dskip),
        in_specs=[
            _slab_spec(UB, tm, slab), _slab_spec(UB, tm, slab), pl.BlockSpec((tm, WB), rev), pl.BlockSpec((tm, WB), rev),
            pl.BlockSpec((1, UB, 2 * WB), blk), pl.BlockSpec((1, UB, 2 * WB), blk),
            pl.BlockSpec((N_TABLES, SUBLANES, WB), lambda b, i: (0, 0, b)), pl.BlockSpec((1, 1, UB), blk),
        ],
        out_specs=[
            _slab_spec(UB, tm, slab), pl.BlockSpec((1, 2, UB, STATE), blk4), pl.BlockSpec((1, 2, UB, STATE), blk4),
            pl.BlockSpec((1, 2, WB), blk), pl.BlockSpec((1, 1, UB), blk),
        ],
        out_shape=[
            _slab_shape(L, D_SSM), jax.ShapeDtypeStruct((N_GBLK, 2, UB, STATE), F32),
            jax.ShapeDtypeStruct((N_GBLK, 2, UB, STATE), F32), jax.ShapeDtypeStruct((N_GBLK, 2, WB), F32),
            jax.ShapeDtypeStruct((N_GBLK, 1, UB), F32),
        ],
        scratch_shapes=[pltpu.VMEM((tm, WB), F32)] * 4 + [pltpu.VMEM((SUBLANES, WB), F32)] * 4 + [pltpu.VMEM((UB, 2 * WB), F32)] * 2,
    )


def _inproj_bwd(du, dhbc, x, dx1, w_in_all, g1):
    L, D = x.shape
    ns, _, nc = w_in_all.shape
    tm = _tile(L, TM_PROJ)

    def body(du_ref, dhbc_ref, x_ref, dx1_ref, w_ref, g_ref, gx_ref, dproj_ref, dg_ref):
        @pl.when(pl.program_id(0) == 0)
        def _():
            dg_ref[...] = jnp.zeros_like(dg_ref)

        du_b = _load_slabs(du_ref).astype(MXU_DTYPE)
        dproj_ref[:, :nc] = du_b
        dproj_ref[:, nc:] = dhbc_ref[...]
        dhn = _dot_nt(du_b, w_ref[0])
        for j in range(1, ns):
            dhn = dhn + _dot_nt(dhbc_ref[:, (j - 1) * nc:j * nc], w_ref[j])
        dx, dg = _rms_bwd(x_ref[...], g_ref[...], dhn)
        dg_ref[...] += dg
        gx_ref[...] = dx1_ref[...] + dx

    row = lambda i: (i, 0)
    return pl.pallas_call(
        body, name="inproj_bwd", grid=(L // tm,),
        in_specs=[_slab_spec(nc, tm), pl.BlockSpec((tm, (ns - 1) * nc), row), pl.BlockSpec((tm, D), row), pl.BlockSpec((tm, D), row),
                  _resident(w_in_all.shape), _resident(g1.shape)],
        out_specs=[pl.BlockSpec((tm, D), row), pl.BlockSpec((tm, ns * nc), row), pl.BlockSpec((1, D), lambda i: (0, 0))],
        out_shape=[jax.ShapeDtypeStruct((L, D), F32), jax.ShapeDtypeStruct((L, ns * nc), MXU_DTYPE), jax.ShapeDtypeStruct((1, D), F32)],
        compiler_params=_params(("arbitrary",)),
    )(du, dhbc, x, dx1, w_in_all, g1)


def _matmul_tn(a, b, name, col_shards=1, exchanges=()):
    L, K = a.shape
    N = b.shape[1]
    tl = _tile(L, TL_TN)
    tk = _tile(K, 1024)
    nw = N // col_shards
    tn = _tile(nw, 1024)
    npb = nw // tn

    def body(a_ref, b_ref, o_ref):
        @pl.when(pl.program_id(2) == 0)
        def _():
            o_ref[...] = jnp.zeros_like(o_ref)

        o_ref[0] += _dot_tn(a_ref[...], b_ref[...])

    return _call(
        body, name=name, grid=(K // tk, N // tn, L // tl), exchanges=exchanges, semantics=("arbitrary", "arbitrary", "arbitrary"),
        operands=(a, b),
        in_specs=[pl.BlockSpec((tl, tk), lambda k, n, l: (l, k)), pl.BlockSpec((tl, tn), lambda k, n, l: (l, n))],
        out_specs=[pl.BlockSpec((1, tk, tn), lambda k, n, l: (n // npb, k, n % npb))],
        out_shape=[jax.ShapeDtypeStruct((col_shards, K, nw), F32)],
    )


def _ssm_discretize(lam_re, lam_im, log_dt, bt_re, bt_im):
    dt = jnp.exp(log_dt)[:, None]
    zr = lam_re * dt
    zi = lam_im * dt
    mag = jnp.exp(zr)
    abr = mag * jnp.cos(zi)
    abi = mag * jnp.sin(zi)
    nr, ni = abr - 1.0, abi
    den = lam_re * lam_re + lam_im * lam_im
    coef_r = ((nr * lam_re + ni * lam_im) / den)[:, None, :]
    coef_i = ((ni * lam_re - nr * lam_im) / den)[:, None, :]
    return zr, zi, coef_r * bt_re - coef_i * bt_im, coef_r * bt_im + coef_i * bt_re


def _scan_tables(ar, ai, reverse):
    rows = np.arange(SUBLANES)
    exps = np.zeros((N_TABLES // 2, SUBLANES), np.int32)
    keep = np.ones((N_TABLES // 2, SUBLANES), bool)
    for t, k in enumerate((1, 2, 4)):
        exps[t] = SUBLANES * k
        keep[t] = (rows + k <= SUBLANES - 1) if reverse else (rows >= k)
    exps[3] = SUBLANES * (SUBLANES - rows) if reverse else SUBLANES * (rows + 1)
    for j in range(SUBLANES):
        exps[4 + j] = SUBLANES - j if reverse else j + 1
    pr, pi = ar, (-ai if reverse else ai)
    shape = (N_TABLES // 2, SUBLANES, ar.shape[0])
    xr, xi = jnp.ones(shape, F32), jnp.zeros(shape, F32)
    for bit in range(int(exps.max()).bit_length()):
        on = ((exps >> bit) & 1).astype(bool)[:, :, None]
        xr, xi = jnp.where(on, xr * pr - xi * pi, xr), jnp.where(on, xr * pi + xi * pr, xi)
        pr, pi = pr * pr - pi * pi, 2.0 * pr * pi
    xr = jnp.where(keep[:, :, None], xr, 0.0)
    xi = jnp.where(keep[:, :, None], xi, 0.0)
    return jnp.stack([xr, xi], axis=1).reshape(N_TABLES, SUBLANES, ar.shape[0])


def _group_mask(rows, cols):
    r = lax.broadcasted_iota(jnp.int32, (rows, cols), 0) // GROUP
    c = lax.broadcasted_iota(jnp.int32, (rows, cols), 1) // STATE
    return r == c


def _ssm_expand(bt_re, bt_im, c_re, c_im, exchanges=()):
    flat = lambda a: a.reshape(N_GROUPS * GROUP, STATE)

    def body(br_ref, bi_ref, cr_ref, ci_ref, bm_ref, cm_ref):
        spread = (lax.broadcasted_iota(jnp.int32, (STATE, WB), 1) % STATE == lax.broadcasted_iota(jnp.int32, (STATE, WB), 0)).astype(F32)
        mask = _group_mask(UB, WB)

        def expand(x):
            wide = jnp.dot(x, spread, precision=lax.Precision.HIGHEST, preferred_element_type=F32)
            return jnp.where(mask, wide, 0.0).astype(MXU_DTYPE)

        bm_ref[0, :, :WB] = expand(br_ref[...])
        bm_ref[0, :, WB:] = expand(bi_ref[...])
        cm_ref[0, :, :WB] = expand(cr_ref[...])
        cm_ref[0, :, WB:] = expand(-ci_ref[...])

    spec = pl.BlockSpec((UB, STATE), lambda b: (b, 0))
    out = pl.BlockSpec((1, UB, 2 * WB), lambda b: (b, 0, 0))
    return _call(
        body, name="ssm_expand", grid=(N_GBLK,), exchanges=exchanges, semantics=("arbitrary",),
        operands=(flat(bt_re), flat(bt_im), flat(c_re), flat(c_im)), in_specs=[spec] * 4, out_specs=[out, out],
        out_shape=[jax.ShapeDtypeStruct((N_GBLK, UB, 2 * WB), MXU_DTYPE)] * 2,
    )


def _ssm_matrices(lam_re, lam_im, log_dt, bt_re, bt_im, c_re, c_im, exchanges=()):
    zr, zi, bbar_r, bbar_i = _ssm_discretize(lam_re, lam_im, log_dt, bt_re, bt_im)
    mag = jnp.exp(zr)
    ar = (mag * jnp.cos(zi)).reshape(-1)
    ai = (mag * jnp.sin(zi)).reshape(-1)
    bmat, cmat, *rest = _ssm_expand(bbar_r, bbar_i, c_re, c_im, exchanges)
    return bmat, cmat, _scan_tables(ar, ai, False), _scan_tables(ar, ai, True), rest


def _ssm_param_grads(lam_re, lam_im, log_dt, bt_re, bt_im, gb, gc, q, gd):
    part = lambda g, k: g[:, k].reshape(N_GROUPS, GROUP, STATE)
    qr = q[:, 0, :].reshape(N_GROUPS, STATE)
    qi = q[:, 1, :].reshape(N_GROUPS, STATE)
    _, vjp = jax.vjp(_ssm_discretize, lam_re, lam_im, log_dt, bt_re, bt_im)
    d_lam_re, d_lam_im, d_log_dt, d_bt_re, d_bt_im = vjp((qr, qi, part(gb, 0), part(gb, 1)))
    return d_lam_re, d_lam_im, d_log_dt, d_bt_re, d_bt_im, part(gc, 0), -part(gc, 1), gd.reshape(N_GROUPS, GROUP)


def _row_tile(rows, n):
    return _tile(rows, max(SUBLANES, (2 * 1024 * 1024) // (4 * n)))


def _pair_add(grad, other, core, name):
    ns, h, n = other.shape
    tr = _row_tile(h, n)
    nb = h // tr

    def body(c_ref, g_ref, o_ref, out_ref):
        out_ref[...] = (g_ref[...] + o_ref[...]).astype(WIRE_DTYPE)

    return pl.pallas_call(
        body, name=name,
        grid_spec=pltpu.PrefetchScalarGridSpec(
            num_scalar_prefetch=1, grid=(ns, nb),
            in_specs=[pl.BlockSpec((1, tr, n), lambda s, i, c: (s, c[0] * nb + i, 0)), pl.BlockSpec((1, tr, n), lambda s, i, c: (s, i, 0))],
            out_specs=pl.BlockSpec((1, tr, n), lambda s, i, c: (s, i, 0))),
        out_shape=jax.ShapeDtypeStruct(other.shape, WIRE_DTYPE),
        compiler_params=_params(("arbitrary", "arbitrary")),
    )(core, grad, other)


def _quad_sum(parts, core, name):
    ns, h, n = parts.shape
    tr = _row_tile(h, n)
    nb = h // tr

    def body(c_ref, p_ref, out_ref):
        p = [p_ref[k].astype(F32) for k in range(ns)]
        out_ref[...] = ((p[0] + p[1]) + p[2]) + p[3]

    return pl.pallas_call(
        body, name=name,
        grid_spec=pltpu.PrefetchScalarGridSpec(
            num_scalar_prefetch=1, grid=(nb,),
            in_specs=[pl.BlockSpec((ns, tr, n), lambda i, c: (0, i, 0))],
            out_specs=pl.BlockSpec((tr, n), lambda i, c: (c[0] * nb + i, 0))),
        out_shape=jax.ShapeDtypeStruct((2 * h, n), F32),
        compiler_params=_params(("arbitrary",)),
    )(core, parts)


def _adamw_math(w, g, m, v):
    m = ADAM_B1 * m + (1.0 - ADAM_B1) * g
    v = ADAM_B2 * v + (1.0 - ADAM_B2) * jnp.square(g)
    m_hat = m / (1.0 - ADAM_B1 ** ADAM_STEP)
    v_hat = v / (1.0 - ADAM_B2 ** ADAM_STEP)
    delta = -ADAM_LR * (m_hat / (jnp.sqrt(v_hat) + ADAM_EPS) + ADAM_WD * w)
    return delta, m, v


def _adamw(w, g, m, v, name):
    r, n = w.shape
    tr = _row_tile(r, n)

    def body(w_ref, g_ref, m_ref, v_ref, d_ref, nm_ref, nv_ref):
        d_ref[...], nm_ref[...], nv_ref[...] = _adamw_math(w_ref[...], g_ref[...], m_ref[...], v_ref[...])

    spec = pl.BlockSpec((tr, n), lambda i: (i, 0))
    return pl.pallas_call(
        body, name=name, grid=(r // tr,), in_specs=[spec] * 4, out_specs=[spec] * 3,
        out_shape=[jax.ShapeDtypeStruct((r, n), F32)] * 3,
        compiler_params=_params(("arbitrary",)),
    )(w, g, m, v)


LANES = 128
SMALL = ["g_pre_mix", "lam_re", "lam_im", "log_dt", "b_re", "b_im", "c_re", "c_im", "d_skip", "conv_w", "g_ssm_out", "g_conv_out",
         "g_post_mix", "g_pre_mlp", "g_post_mlp"]
TILE_SLOTS = {"b_re": (0, N_GROUPS), "b_im": (N_GROUPS, N_GROUPS), "c_re": (2 * N_GROUPS, N_GROUPS), "c_im": (3 * N_GROUPS, N_GROUPS),
              "lam_re": (4 * N_GROUPS, 2), "lam_im": (4 * N_GROUPS + 2, 2)}
N_TILE_SLOTS = 4 * N_GROUPS + 4
VEC_ROWS = {"g_pre_mix": 0, "g_post_mix": 1, "g_pre_mlp": 2, "g_post_mlp": 3, "g_ssm_out": 4, "g_conv_out": 5, "log_dt": 6}
ROW_LOSS, ROW_DSKIP, ROW_CONV, N_PACK_ROWS = 7, 8, 24, 32


def _kernel_form(name, a):
    if name in ("b_re", "b_im"):
        return jnp.transpose(a, (0, 1, 3, 2)).reshape(N_GROUPS, GROUP, STATE)
    if name in ("c_re", "c_im"):
        return a.reshape(N_GROUPS, GROUP, STATE)
    if name in ("lam_re", "lam_im"):
        return a.reshape(2, GROUP, STATE)
    if name == "d_skip":
        return jnp.transpose(a, (0, 2, 1)).reshape(GROUP, N_GROUPS)
    if name == "conv_w":
        return jnp.transpose(a, (1, 0, 2))
    return a


def _param_form(name, k):
    if name in ("b_re", "b_im"):
        return jnp.transpose(k.reshape(1, N_GROUPS, GROUP, STATE), (0, 1, 3, 2))
    if name in ("c_re", "c_im"):
        return k.reshape(1, N_GROUPS, GROUP, STATE)
    if name in ("lam_re", "lam_im"):
        return k.reshape(1, N_GROUPS, STATE)
    if name == "d_skip":
        return jnp.transpose(k.reshape(1, GROUP, N_GROUPS), (0, 2, 1))
    if name == "conv_w":
        return jnp.transpose(k, (1, 0, 2))
    return k


def _pack_tiles(g):
    lam = lambda a: a.reshape(2, GROUP, STATE)
    tiles = jnp.concatenate([g["b_re"], g["b_im"], g["c_re"], g["c_im"], lam(g["lam_re"]), lam(g["lam_im"])], axis=0)
    return tiles.astype(WIRE_DTYPE)


def _pack_rows(g, loss):
    row = lambda a: jnp.pad(a, ((0, 0), (0, D_MODEL - a.shape[1])))
    rows = [row(g[k][None]) for k in VEC_ROWS] + [row(loss[0:1]), row(g["d_skip"].T), row(g["conv_w"])]
    rows.append(jnp.zeros((N_PACK_ROWS - ROW_CONV - 3, D_MODEL), F32))
    return jnp.concatenate(rows, axis=0)


def _adamw_small(tiles, rows, w, m, v):
    nn = len(SMALL)

    def body(*refs):
        t_ref, r_ref = refs[0], refs[1]
        w_refs, m_refs, v_refs = refs[2:2 + nn], refs[2 + nn:2 + 2 * nn], refs[2 + 2 * nn:2 + 3 * nn]
        loss_ref, outs = refs[2 + 3 * nn], refs[3 + 3 * nn:]

        def tile_sum(first, count):
            total = t_ref[0, first:first + count].astype(F32)
            for d in range(1, N_DEV):
                total = total + t_ref[d, first:first + count].astype(F32)
            return total

        def row_sum(first, count, lanes):
            total = r_ref[0, first:first + count, 0:lanes]
            for d in range(1, N_DEV):
                total = total + r_ref[d, first:first + count, 0:lanes]
            return total

        def step(j, g, at=lambda ref: ref):
            delta, nm, nv = _adamw_math(at(w_refs[j])[...], g, at(m_refs[j])[...], at(v_refs[j])[...])
            at(outs[j])[...] = g
            at(outs[nn + j])[...] = delta
            at(outs[2 * nn + j])[...] = nm
            at(outs[3 * nn + j])[...] = nv

        loss_ref[...] = row_sum(ROW_LOSS, 1, LANES)
        chip = 2 * lax.axis_index("x") + lax.axis_index("y")
        for j, name in enumerate(SMALL):
            if name in TILE_SLOTS:
                step(j, tile_sum(*TILE_SLOTS[name]))
            elif name == "d_skip":
                step(j, row_sum(ROW_DSKIP, GROUP, N_GROUPS))
            elif name == "conv_w":
                full = row_sum(ROW_CONV, 3, D_CONV)
                mine = full[:, 0:LANES]
                for s in range(1, N_CHIPS):
                    mine = jnp.where(chip == s, full[:, s * LANES:(s + 1) * LANES], mine)
                for k in range(3):
                    step(j, mine[k:k + 1, :], at=lambda ref, k=k: ref.at[k])
            else:
                step(j, row_sum(VEC_ROWS[name], 1, w_refs[j].shape[1]))

    args = [tiles, rows] + [w[k] for k in SMALL] + [m[k] for k in SMALL] + [v[k] for k in SMALL]
    res = pl.pallas_call(
        body, name="adamw_small", in_specs=[VMEM] * len(args), out_specs=[VMEM] * (1 + 4 * nn),
        out_shape=[jax.ShapeDtypeStruct((1, LANES), F32)] + [jax.ShapeDtypeStruct(w[k].shape, F32) for k in SMALL] * 4,
        compiler_params=pltpu.CompilerParams(vmem_limit_bytes=VMEM_LIMIT),
    )(*args)
    return res[0], [dict(zip(SMALL, res[1 + q * nn:1 + (q + 1) * nn])) for q in range(4)]


WEIGHTS = ["g_pre_mix", "w_in", "lam_re", "lam_im", "log_dt", "b_re", "b_im", "c_re", "c_im", "d_skip", "w_glu", "conv_w",
           "g_ssm_out", "g_conv_out", "w_out", "g_post_mix", "g_pre_mlp", "w_up", "w_down", "g_post_mlp"]
BIG = ["w_in", "w_glu", "w_out", "w_up", "w_down"]


def kernel(x, g_pre_mix, w_in, lam_re, lam_im, log_dt, b_re, b_im, c_re, c_im, d_skip, w_glu, conv_w, g_ssm_out, g_conv_out, w_out, g_post_mix, g_pre_mlp, w_up, w_down, g_post_mlp, loss_target, m_g_pre_mix, m_w_in, m_lam_re, m_lam_im, m_log_dt, m_b_re, m_b_im, m_c_re, m_c_im, m_d_skip, m_w_glu, m_conv_w, m_g_ssm_out, m_g_conv_out, m_w_out, m_g_post_mix, m_g_pre_mlp, m_w_up, m_w_down, m_g_post_mlp, v_g_pre_mix, v_w_in, v_lam_re, v_lam_im, v_log_dt, v_b_re, v_b_im, v_c_re, v_c_im, v_d_skip, v_w_glu, v_conv_w, v_g_ssm_out, v_g_conv_out, v_w_out, v_g_post_mix, v_g_pre_mlp, v_w_up, v_w_down, v_g_post_mlp):
    w = dict(g_pre_mix=g_pre_mix, w_in=w_in, lam_re=lam_re, lam_im=lam_im, log_dt=log_dt, b_re=b_re, b_im=b_im, c_re=c_re, c_im=c_im,
             d_skip=d_skip, w_glu=w_glu, conv_w=conv_w, g_ssm_out=g_ssm_out, g_conv_out=g_conv_out, w_out=w_out, g_post_mix=g_post_mix,
             g_pre_mlp=g_pre_mlp, w_up=w_up, w_down=w_down, g_post_mlp=g_post_mlp)
    m = dict(g_pre_mix=m_g_pre_mix, w_in=m_w_in, lam_re=m_lam_re, lam_im=m_lam_im, log_dt=m_log_dt, b_re=m_b_re, b_im=m_b_im, c_re=m_c_re,
             c_im=m_c_im, d_skip=m_d_skip, w_glu=m_w_glu, conv_w=m_conv_w, g_ssm_out=m_g_ssm_out, g_conv_out=m_g_conv_out, w_out=m_w_out,
             g_post_mix=m_g_post_mix, g_pre_mlp=m_g_pre_mlp, w_up=m_w_up, w_down=m_w_down, g_post_mlp=m_g_post_mlp)
    v = dict(g_pre_mix=v_g_pre_mix, w_in=v_w_in, lam_re=v_lam_re, lam_im=v_lam_im, log_dt=v_log_dt, b_re=v_b_re, b_im=v_b_im, c_re=v_c_re,
             c_im=v_c_im, d_skip=v_d_skip, w_glu=v_w_glu, conv_w=v_conv_w, g_ssm_out=v_g_ssm_out, g_conv_out=v_g_conv_out, w_out=v_w_out,
             g_post_mix=v_g_post_mix, g_pre_mlp=v_g_pre_mlp, w_up=v_w_up, w_down=v_w_down, g_post_mlp=v_g_post_mlp)
    w_dev, m_dev, v_dev = w, m, v
    w, m, v = ({k: a[0] for k, a in d.items()} for d in (w, m, v))
    core = lax.axis_index("c").astype(jnp.int32).reshape(1)

    xs, target = x[0], loss_target[0]
    g1 = w["g_pre_mix"][None]
    g_ssm, g_conv = w["g_ssm_out"][None], w["g_conv_out"][None]
    g_post_mix, g_pre_mlp, g_post_mlp = w["g_post_mix"][None], w["g_pre_mlp"][None], w["g_post_mlp"][None]
    bt_re, bt_im = (jnp.transpose(w[k], (0, 2, 1)) for k in ("b_re", "b_im"))
    dskip = w["d_skip"].reshape(N_GBLK, 1, UB)
    shard = {k: w[k].astype(MXU_DTYPE) for k in BIG}
    conv_pad = jnp.pad(w["conv_w"], ((0, SUBLANES - 3), (0, 0)))

    bmat, cmat, coef_f, coef_r, (w_in_all,) = _ssm_matrices(
        w["lam_re"], w["lam_im"], w["log_dt"], bt_re, bt_im, w["c_re"], w["c_im"], exchanges=[_GatherForward([shard["w_in"]])])
    hn, proj, u4, w_glu_all, w_out_all, conv_all = _inproj_fwd(
        xs, g1, w_in_all, exchanges=[_Gather([shard["w_glu"], shard["w_out"], conv_pad], [False, False, False])])
    wd_half = shard["w_down"].shape[0] // 2
    wda, wdb = shard["w_down"][:wd_half], shard["w_down"][wd_half:]
    s_re, s_im, ys, w_up_all, wda_all = _s5_fwd(u4, bmat, cmat, coef_f, dskip, exchanges=[_Gather([shard["w_up"], wda], [True, True])])
    w_glu_f, w_out_f = w_glu_all.reshape(D_SSM, D_SSM), w_out_all.reshape(D_MODEL, D_MODEL)
    conv_f = jnp.transpose(conv_all, (1, 0, 2)).reshape(SUBLANES, D_CONV)
    ycat, o, x1, w_up_all, wda_all, wdb_all = _tail_fwd(xs, ys, proj, w_glu_f, conv_f, g_ssm, g_conv, w_out_f, g_post_mix,
                                                        exchanges=[_Forward([w_up_all, wda_all]), _GatherForward([wdb])])
    hn2, up, m_act, dx2, loss = _mlp_fwd(x1, target, w_up_all, wda_all, wdb_all, g_pre_mlp, g_post_mlp)

    dm, dup, act, dx1, dg_post_mlp, dg_pre_mlp = _mlp_bwd(dx2, m_act, up, x1, w_up_all, wda_all, wdb_all, g_pre_mlp, g_post_mlp)
    gw_down = _matmul_tn(act, dm, "dw_down")[0].reshape(N_CHIPS, D_FF // N_CHIPS, D_MODEL)
    gw_up = _matmul_tn(hn2, dup, "dw_up", col_shards=N_CHIPS)[0]
    do, da, y1, dys, dhbc, dg_post_mix, dg_ssm, dg_conv, dconv_w, o_down, o_up = _tail_bwd(
        dx1, o, ys, proj, w_glu_f, conv_f, g_ssm, g_conv, w_out_f, g_post_mix, exchanges=[_Pair([gw_down, gw_up])])
    p_down = _pair_add(gw_down, o_down, core, "pair_add_w_down")
    p_up = _pair_add(gw_up, o_up, core, "pair_add_w_up")
    gw_out = _matmul_tn(ycat, do, "dw_out")[0].reshape(N_CHIPS, D_MODEL // N_CHIPS, D_MODEL)
    gw_glu = _matmul_tn(y1, da, "dw_glu")[0].reshape(N_CHIPS, D_SSM // N_CHIPS, D_SSM)
    du, gb, gc, q, gd, q_down, q_up, o_out, o_glu = _s5_bwd(
        dys, u4, s_re, s_im, bmat, cmat, coef_r, dskip, exchanges=[_Chip([p_down, p_up]), _Pair([gw_out, gw_glu])])
    h_down = _quad_sum(q_down, core, "quad_sum_w_down")
    h_up = _quad_sum(q_up, core, "quad_sum_w_up")
    p_out = _pair_add(gw_out, o_out, core, "pair_add_w_out")
    p_glu = _pair_add(gw_glu, o_glu, core, "pair_add_w_glu")
    grad_x, dproj, dg_pre_mix = _inproj_bwd(du, dhbc, xs, dx1, w_in_all, g1)
    d_lam_re, d_lam_im, d_log_dt, d_b_re, d_b_im, d_c_re, d_c_im, d_d_skip = _ssm_param_grads(
        w["lam_re"], w["lam_im"], w["log_dt"], bt_re, bt_im, gb, gc, q, gd)
    small = {
        "g_pre_mix": dg_pre_mix[0], "lam_re": d_lam_re, "lam_im": d_lam_im, "log_dt": d_log_dt, "b_re": d_b_re, "b_im": d_b_im,
        "c_re": d_c_re, "c_im": d_c_im, "d_skip": d_d_skip, "conv_w": dconv_w[:3], "g_ssm_out": dg_ssm[0], "g_conv_out": dg_conv[0],
        "g_post_mix": dg_post_mix[0], "g_pre_mlp": dg_pre_mlp[0], "g_post_mlp": dg_post_mlp[0],
    }
    gw_in, g_down, g_up, q_out, q_glu, tiles, rows = _matmul_tn(
        hn, dproj, "dw_in", col_shards=N_CHIPS,
        exchanges=[_Share([h_down, h_up]), _Chip([p_out, p_glu]), _GatherSmall(_pack_tiles(small)), _GatherSmall(_pack_rows(small, loss))])
    h_out = _quad_sum(q_out, core, "quad_sum_w_out")
    h_glu = _quad_sum(q_glu, core, "quad_sum_w_glu")
    (o_in,) = _run_exchanges([_Pair([gw_in])], "rs_pair_w_in")
    p_in = _pair_add(gw_in, o_in, core, "pair_add_w_in")
    q_in, g_out, g_glu = _run_exchanges([_Chip([p_in]), _Share([h_out, h_glu])], "rs_chip_w_in")
    h_in = _quad_sum(q_in, core, "quad_sum_w_in")
    (g_in,) = _run_exchanges([_Share([h_in])], "rs_share_w_in")
    shard_grads = {"w_in": g_in, "w_glu": g_glu, "w_out": g_out, "w_up": g_up, "w_down": g_down}

    out = {q: {} for q in ("grad", "delta", "new_m", "new_v")}
    for k in BIG:
        out["grad"][k] = shard_grads[k][None]
        delta, new_m, new_v = _adamw(w[k], shard_grads[k], m[k], v[k], "adamw_" + k)
        out["delta"][k], out["new_m"][k], out["new_v"][k] = delta[None], new_m[None], new_v[None]
    form = lambda d: {k: _kernel_form(k, d[k]) for k in SMALL}
    loss, res = _adamw_small(tiles, rows, form(w_dev), form(m_dev), form(v_dev))
    for q, d in zip(("grad", "delta", "new_m", "new_v"), res):
        out[q].update({k: _param_form(k, d[k]) for k in SMALL})
    flat = [loss[0, 0], grad_x[None]]
    for q in ("grad", "delta", "new_m", "new_v"):
        flat += [out[q][k] for k in WEIGHTS]
    return tuple(flat)
```

```python
import functools
import math

import jax
import jax.numpy as jnp
import numpy as np
from jax import lax
from jax.experimental import pallas as pl
from jax.experimental.pallas import tpu as pltpu

F32 = jnp.float32
MXU_DTYPE = jnp.bfloat16
WIRE_DTYPE = jnp.bfloat16

D_MODEL = 1024
D_SSM = 512
D_CONV = 512
N_GROUPS = 32
GROUP = 16
STATE = 64
D_FF = 4096
RMS_EPS = 1e-6
N_CHIPS = 4
N_DEV = 8

ADAM_LR = 0.001
ADAM_B1 = 0.9
ADAM_B2 = 0.999
ADAM_EPS = 1e-08
ADAM_WD = 0.01
ADAM_STEP = 10

N_GBLK = 2
G_PER_BLK = N_GROUPS // N_GBLK
UB = G_PER_BLK * GROUP
WB = G_PER_BLK * STATE
LANE_CHUNK = 256
SUBLANES = 8
N_TABLES = 24

TM_PROJ = 512
TM_S5 = 512
TM_TAIL = 512
TM_MLP = 256
TL_TN = 2048
TN_TN = 2048
VMEM_LIMIT = 56 * 1024 * 1024

MESH = pl.DeviceIdType.MESH


def _params(sem, vmem=VMEM_LIMIT):
    return pltpu.CompilerParams(dimension_semantics=sem, vmem_limit_bytes=vmem)


def _resident(shape):
    nd = len(shape)
    return pl.BlockSpec(shape, lambda *_: (0,) * nd, pipeline_mode=pl.Buffered(1))


def _dot(a, b):
    return jnp.dot(a, b, preferred_element_type=F32)


def _dot_nt(a, b):
    return lax.dot_general(a, b, (((1,), (1,)), ((), ())), preferred_element_type=F32)


def _dot_tn(a, b):
    return lax.dot_general(a, b, (((0,), (0,)), ((), ())), preferred_element_type=F32)


def _rms_fwd(x, g):
    r = lax.rsqrt(jnp.mean(x * x, axis=-1, keepdims=True) + RMS_EPS)
    return x * r * g


def _rms_bwd(x, g, dy):
    r = lax.rsqrt(jnp.mean(x * x, axis=-1, keepdims=True) + RMS_EPS)
    xn = x * r
    q = dy * g
    dx = r * (q - xn * jnp.mean(q * xn, axis=-1, keepdims=True))
    return dx, jnp.sum(dy * xn, axis=0, keepdims=True)


_GELU_C = math.sqrt(2.0 / math.pi)


def _gelu(x):
    t = jnp.tanh(_GELU_C * (x + 0.044715 * (x * x * x)))
    y = x * (0.5 * (1.0 + t))
    dy = 0.5 * (1.0 + t) + 0.5 * x * (1.0 - t * t) * (_GELU_C * (1.0 + 3 * 0.044715 * (x * x)))
    return y, dy


def _tile(n, pref):
    t = min(n, pref)
    assert n % t == 0, (n, t)
    return t


HBM = pl.BlockSpec(memory_space=pltpu.HBM)
VMEM = pl.BlockSpec(memory_space=pltpu.VMEM)
DMA_SEMS = pltpu.SemaphoreType.DMA


def _place():
    x, y, c = lax.axis_index("x"), lax.axis_index("y"), lax.axis_index("c")
    chips = [(1 - x, y), (x, 1 - y), (1 - x, 1 - y)]
    return (x, y, c), 2 * x + y, (x, y, 1 - c), chips, [2 * px + py for px, py in chips]


def _remote(src, dst, send_sem, recv_sem, device):
    return pltpu.make_async_remote_copy(src_ref=src, dst_ref=dst, send_sem=send_sem, recv_sem=recv_sem,
                                        device_id=device, device_id_type=MESH)


def _half(rows, c):
    return pl.ds(c * (rows // 2), rows // 2)


class _Exchange:
    aliases = {}

    def start(self, ins, outs, sems):
        local, outgoing, _ = self._copies(ins, outs, sems)
        for cp in local + outgoing:
            cp.start()

    def finish(self, ins, outs, sems):
        local, outgoing, incoming = self._copies(ins, outs, sems)
        for cp in incoming:
            cp.wait_recv()
        for cp in outgoing:
            cp.wait_send()
        for cp in local:
            cp.wait()


class _Gather(_Exchange):
    def __init__(self, shards, split):
        self.inputs, self.split = list(shards), split
        self.out_shape = [jax.ShapeDtypeStruct((N_CHIPS, *a.shape), a.dtype) for a in shards]
        self.sems = [DMA_SEMS((len(shards), 3)), DMA_SEMS((len(shards), 3)), DMA_SEMS((len(shards),))]

    def _copies(self, ins, outs, sems):
        send, recv, lsem = sems
        (x, y, c), me, sibling, chips, ids = _place()
        local = [pltpu.make_async_copy(ins[t], outs[t].at[me], lsem.at[t]) for t in range(len(ins))]
        outgoing, incoming = [], []
        for t, a in enumerate(self.inputs):
            rows = _half(a.shape[0], c) if self.split[t] else pl.ds(0, a.shape[0])
            for k in range(3):
                to = (*chips[k], c)
                outgoing.append(_remote(ins[t].at[rows, :], outs[t].at[me, rows, :], send.at[t, k], recv.at[t, k], to))
                incoming.append(_remote(ins[t].at[rows, :], outs[t].at[ids[k], rows, :], send.at[t, k], recv.at[t, k], to))
        return local, outgoing, incoming


class _Forward(_Exchange):
    def __init__(self, arrays):
        self.inputs = list(arrays)
        self.out_shape = [jax.ShapeDtypeStruct(a.shape, a.dtype) for a in arrays]
        self.aliases = {t: t for t in range(len(arrays))}
        self.sems = [DMA_SEMS((len(arrays), 3)), DMA_SEMS((len(arrays), 3))]

    def _copies(self, ins, outs, sems):
        send, recv = sems
        (x, y, c), me, sibling, chips, ids = _place()
        outgoing, incoming = [], []
        for t, a in enumerate(self.inputs):
            for k in range(3):
                mine = outs[t].at[ids[k], _half(a.shape[1], c), :]
                theirs = outs[t].at[ids[k], _half(a.shape[1], 1 - c), :]
                outgoing.append(_remote(mine, mine, send.at[t, k], recv.at[t, k], sibling))
                incoming.append(_remote(theirs, theirs, send.at[t, k], recv.at[t, k], sibling))
        return [], outgoing, incoming


class _GatherForward(_Exchange):
    def __init__(self, shards):
        self.gather = _Gather(shards, [True] * len(shards))
        self.forward = _Forward(self.gather.out_shape)
        self.inputs, self.out_shape = self.gather.inputs, self.gather.out_shape
        self.sems = self.gather.sems + self.forward.sems

    def start(self, ins, outs, sems):
        self.gather.start(ins, outs, sems[:3])

    def finish(self, ins, outs, sems):
        local, outgoing, incoming = self.gather._copies(ins, outs, sems[:3])
        _, passed, from_sibling = self.forward._copies(outs, outs, sems[3:])
        for landed, onward in zip(incoming, passed):
            landed.wait_recv()
            onward.start()
        for cp in from_sibling:
            cp.wait_recv()
        for cp in outgoing + passed:
            cp.wait_send()
        for cp in local:
            cp.wait()


class _Pair(_Exchange):
    def __init__(self, grads):
        self.inputs = list(grads)
        self.out_shape = [jax.ShapeDtypeStruct((g.shape[0], g.shape[1] // 2, g.shape[2]), g.dtype) for g in grads]
        self.sems = [DMA_SEMS((len(grads),)), DMA_SEMS((len(grads),))]

    def _copies(self, ins, outs, sems):
        send, recv = sems
        (x, y, c), me, sibling, chips, ids = _place()
        cps = [_remote(ins[t].at[:, _half(g.shape[1], 1 - c), :], outs[t], send.at[t], recv.at[t], sibling)
               for t, g in enumerate(self.inputs)]
        return [], cps, cps


class _Chip(_Exchange):
    def __init__(self, parts):
        self.inputs = list(parts)
        self.out_shape = [jax.ShapeDtypeStruct(p.shape, p.dtype) for p in parts]
        self.sems = [DMA_SEMS((len(parts), 3)), DMA_SEMS((len(parts), 3)), DMA_SEMS((len(parts),))]

    def _copies(self, ins, outs, sems):
        send, recv, lsem = sems
        (x, y, c), me, sibling, chips, ids = _place()
        local = [pltpu.make_async_copy(ins[t].at[me], outs[t].at[me], lsem.at[t]) for t in range(len(ins))]
        outgoing, incoming = [], []
        for t in range(len(ins)):
            for k in range(3):
                to = (*chips[k], c)
                outgoing.append(_remote(ins[t].at[ids[k]], outs[t].at[me], send.at[t, k], recv.at[t, k], to))
                incoming.append(_remote(ins[t].at[ids[k]], outs[t].at[ids[k]], send.at[t, k], recv.at[t, k], to))
        return local, outgoing, incoming


class _Share(_Exchange):
    def __init__(self, grads):
        self.inputs = list(grads)
        self.out_shape = [jax.ShapeDtypeStruct(g.shape, g.dtype) for g in grads]
        self.aliases = {t: t for t in range(len(grads))}
        self.sems = [DMA_SEMS((len(grads),)), DMA_SEMS((len(grads),))]

    def _copies(self, ins, outs, sems):
        send, recv = sems
        (x, y, c), me, sibling, chips, ids = _place()
        outgoing, incoming = [], []
        for t, g in enumerate(self.inputs):
            mine = outs[t].at[_half(g.shape[0], c), :]
            theirs = outs[t].at[_half(g.shape[0], 1 - c), :]
            outgoing.append(_remote(mine, mine, send.at[t], recv.at[t], sibling))
            incoming.append(_remote(theirs, theirs, send.at[t], recv.at[t], sibling))
        return [], outgoing, incoming


class _GatherSmall(_Exchange):
    def __init__(self, block):
        self.inputs = [block]
        self.out_shape = [jax.ShapeDtypeStruct((N_DEV, *block.shape), block.dtype)]
        self.sems = [DMA_SEMS((7,)), DMA_SEMS((7,)), DMA_SEMS(())]

    def _copies(self, ins, outs, sems):
        send, recv, lsem = sems
        (x, y, c), me, sibling, chips, ids = _place()
        slot = lambda px, py, pc: outs[0].at[4 * px + 2 * py + pc]

        def copy(k, block, to, src=None):
            return _remote(slot(*block) if src is None else src, slot(*block), send.at[k], recv.at[k], to)

        local = [pltpu.make_async_copy(ins[0], slot(x, y, c), lsem)]
        first = [copy(0, (x, y, c), sibling, src=ins[0])] + [copy(1 + j, (x, y, c), (*chip, c), src=ins[0]) for j, chip in enumerate(chips)]
        passed = [copy(4 + j, (*chip, c), sibling) for j, chip in enumerate(chips)]
        landed = [copy(1 + j, (*chip, c), (x, y, c)) for j, chip in enumerate(chips)]
        from_sibling = [copy(0, (x, y, 1 - c), (x, y, c))] + [copy(4 + j, (*chip, 1 - c), (x, y, c)) for j, chip in enumerate(chips)]
        return local, first, (passed, landed, from_sibling)

    def finish(self, ins, outs, sems):
        local, first, (passed, landed, from_sibling) = self._copies(ins, outs, sems)
        for j in range(3):
            landed[j].wait_recv()
            passed[j].start()
        for cp in from_sibling:
            cp.wait_recv()
        for cp in first + passed:
            cp.wait_send()
        for cp in local:
            cp.wait()


def _split_refs(refs, counts):
    out = []
    for n in counts:
        out.append(refs[:n])
        refs = refs[n:]
    return out


def _each_exchange(exchanges, method, x_in, x_out, x_sem):
    for ex in exchanges:
        ni, no, ns = len(ex.inputs), len(ex.out_shape), len(ex.sems)
        getattr(ex, method)(x_in[:ni], x_out[:no], x_sem[:ns])
        x_in, x_out, x_sem = x_in[ni:], x_out[no:], x_sem[ns:]


def _call(body, *, name, grid, in_specs, out_specs, out_shape, operands, semantics, scratch_shapes=(), exchanges=()):
    x_in = [a for ex in exchanges for a in ex.inputs]
    x_out = [s for ex in exchanges for s in ex.out_shape]
    x_sem = [s for ex in exchanges for s in ex.sems]
    counts = (len(in_specs), len(x_in), len(out_specs), len(x_out), len(scratch_shapes), len(x_sem))
    aliases, i0, o0 = {}, len(in_specs), len(out_specs)
    for ex in exchanges:
        aliases.update({i0 + i: o0 + o for i, o in ex.aliases.items()})
        i0, o0 = i0 + len(ex.inputs), o0 + len(ex.out_shape)

    def full_body(*refs):
        ins, xi, outs, xo, scr, xs = _split_refs(list(refs), counts)
        if exchanges:
            @pl.when(functools.reduce(jnp.logical_and, [pl.program_id(a) == 0 for a in range(len(grid))]))
            def _():
                _each_exchange(exchanges, "start", xi, xo, xs)

        body(*ins, *outs, *scr)
        if exchanges:
            @pl.when(functools.reduce(jnp.logical_and, [pl.program_id(a) == grid[a] - 1 for a in range(len(grid))]))
            def _():
                _each_exchange(exchanges, "finish", xi, xo, xs)

    return pl.pallas_call(
        full_body, name=name, grid=grid,
        in_specs=list(in_specs) + [HBM] * len(x_in), out_specs=list(out_specs) + [HBM] * len(x_out),
        out_shape=list(out_shape) + x_out, scratch_shapes=list(scratch_shapes) + x_sem,
        input_output_aliases=aliases, compiler_params=_params(semantics),
    )(*operands, *x_in)


def _run_exchanges(exchanges, name):
    x_in = [a for ex in exchanges for a in ex.inputs]
    x_out = [s for ex in exchanges for s in ex.out_shape]
    x_sem = [s for ex in exchanges for s in ex.sems]
    aliases, i0, o0 = {}, 0, 0
    for ex in exchanges:
        aliases.update({i0 + i: o0 + o for i, o in ex.aliases.items()})
        i0, o0 = i0 + len(ex.inputs), o0 + len(ex.out_shape)

    def body(*refs):
        xi, xo, xs = _split_refs(list(refs), (len(x_in), len(x_out), len(x_sem)))
        _each_exchange(exchanges, "start", xi, xo, xs)
        _each_exchange(exchanges, "finish", xi, xo, xs)

    return pl.pallas_call(
        body, name=name, in_specs=[HBM] * len(x_in), out_specs=[HBM] * len(x_out), out_shape=x_out,
        scratch_shapes=x_sem, input_output_aliases=aliases,
    )(*x_in)


def _inproj_fwd(x, g1, w_in_all, exchanges=()):
    L, D = x.shape
    ns, _, nc = w_in_all.shape
    tm = _tile(L, TM_PROJ)

    def body(x_ref, g_ref, w_ref, hn_ref, proj_ref, u_ref):
        hn = _rms_fwd(x_ref[...], g_ref[...]).astype(MXU_DTYPE)
        hn_ref[...] = hn
        for j in range(ns):
            proj_ref[:, j * nc:(j + 1) * nc] = _dot(hn, w_ref[j])
        _store_slabs(u_ref, proj_ref[:, 0:nc])

    return _call(
        body, name="inproj_fwd", grid=(L // tm,), exchanges=exchanges, semantics=("arbitrary",), operands=(x, g1, w_in_all),
        in_specs=[pl.BlockSpec((tm, D), lambda i: (i, 0)), _resident((1, D)), _resident(w_in_all.shape)],
        out_specs=[pl.BlockSpec((tm, D), lambda i: (i, 0)), pl.BlockSpec((tm, ns * nc), lambda i: (i, 0)), _slab_spec(nc, tm)],
        out_shape=[jax.ShapeDtypeStruct((L, D), MXU_DTYPE), jax.ShapeDtypeStruct((L, ns * nc), F32), _slab_shape(L, nc)],
    )


def _slab_shape(L, n):
    return jax.ShapeDtypeStruct((n // LANES, L, LANES), F32)


def _slab_spec(n, tm, index=lambda i: (0, i, 0)):
    return pl.BlockSpec((n // LANES, tm, LANES), index)


def _store_slabs(ref, value):
    for k in range(ref.shape[0]):
        ref[k] = value[:, k * LANES:(k + 1) * LANES]


def _load_slabs(ref):
    return jnp.concatenate([ref[k] for k in range(ref.shape[0])], axis=1)


SEG_ROWS = SUBLANES * SUBLANES


def _load_permuted(ref):
    tm = ref.shape[1]
    slabs = []
    for k in range(ref.shape[0]):
        tiles = [ref.at[k][pl.ds(b * SEG_ROWS + j, SUBLANES, stride=SUBLANES), :] for b in range(tm // SEG_ROWS) for j in range(SUBLANES)]
        slabs.append(jnp.concatenate(tiles, axis=0))
    return jnp.concatenate(slabs, axis=1)


def _store_permuted(ref, value):
    tm = ref.shape[1]
    for k in range(ref.shape[0]):
        for b in range(tm // SEG_ROWS):
            for j in range(SUBLANES):
                r = b * SEG_ROWS + j * SUBLANES
                ref.at[k][pl.ds(b * SEG_ROWS + j, SUBLANES, stride=SUBLANES), :] = value[r:r + SUBLANES, k * LANES:(k + 1) * LANES]


def _scan_tile(xr, xi, hr, hi, coef_ref, lanes, reverse):
    for k, j in ((1, 0), (2, 2), (4, 4)):
        ar = coef_ref[j, :, lanes]
        ai = coef_ref[j + 1, :, lanes]
        shift = SUBLANES - k if reverse else k
        sr = pltpu.roll(xr, shift, 0)
        si = pltpu.roll(xi, shift, 0)
        xr, xi = xr + (ar * sr - ai * si), xi + (ar * si + ai * sr)
    pr = coef_ref[6, :, lanes]
    pi = coef_ref[7, :, lanes]
    return xr + (pr * hr - pi * hi), xi + (pr * hi + pi * hr)


def _scan_block(read, write, hr, hi, coef_ref, lanes, reverse):
    order = list(range(SUBLANES - 1, -1, -1) if reverse else range(SUBLANES))
    near = 8 + 2 * order[0]
    ar = coef_ref[near, :, lanes]
    ai = coef_ref[near + 1, :, lanes]
    xr, xi = read(order[0])
    local = {order[0]: (xr, xi)}
    for j in order[1:]:
        br, bi = read(j)
        xr, xi = br + (ar * xr - ai * xi), bi + (ar * xi + ai * xr)
        local[j] = (xr, xi)
    er, ei = _scan_tile(xr, xi, hr, hi, coef_ref, lanes, reverse)
    edge = lax.broadcasted_iota(jnp.int32, er.shape, 0) == (SUBLANES - 1 if reverse else 0)
    shift = SUBLANES - 1 if reverse else 1
    pr = jnp.where(edge, hr, pltpu.roll(er, shift, 0))
    pi = jnp.where(edge, hi, pltpu.roll(ei, shift, 0))
    for j in range(SUBLANES):
        cr = coef_ref[8 + 2 * j, :, lanes]
        ci = coef_ref[9 + 2 * j, :, lanes]
        xr, xi = local[j]
        write(j, xr + (cr * pr - ci * pi), xi + (cr * pi + ci * pr))
    end = 0 if reverse else SUBLANES - 1
    return jnp.broadcast_to(er[end:end + 1, :], er.shape), jnp.broadcast_to(ei[end:end + 1, :], ei.shape)


def _s5_fwd(u4, bmat, cmat, coef, dskip, exchanges=()):
    L = u4.shape[1]
    tm = _tile(L, TM_S5)
    lc = min(LANE_CHUNK, WB)

    def body(u_ref, bm_ref, cm_ref, coef_ref, d_ref, sre_ref, sim_ref, ys_ref, hr_ref, hi_ref):
        @pl.when(pl.program_id(1) == 0)
        def _():
            hr_ref[...] = jnp.zeros_like(hr_ref)
            hi_ref[...] = jnp.zeros_like(hi_ref)

        u = _load_permuted(u_ref)
        bu = _dot(u.astype(MXU_DTYPE), bm_ref[0])
        sre_ref[...] = bu[:, :WB]
        sim_ref[...] = bu[:, WB:]
        for c in range(WB // lc):
            lanes = slice(c * lc, (c + 1) * lc)
            hr, hi = hr_ref[:, lanes], hi_ref[:, lanes]
            for b in range(tm // SEG_ROWS):
                rows = lambda j, b=b: slice(b * SEG_ROWS + j * SUBLANES, b * SEG_ROWS + (j + 1) * SUBLANES)

                def read(j, rows=rows, lanes=lanes):
                    return sre_ref[rows(j), lanes], sim_ref[rows(j), lanes]

                def write(j, xr, xi, rows=rows, lanes=lanes):
                    sre_ref[rows(j), lanes] = xr
                    sim_ref[rows(j), lanes] = xi

                hr, hi = _scan_block(read, write, hr, hi, coef_ref, lanes, False)
            hr_ref[:, lanes] = hr
            hi_ref[:, lanes] = hi
        ys = _dot_nt(sre_ref[...].astype(MXU_DTYPE), cm_ref[0, :, :WB]) + _dot_nt(sim_ref[...].astype(MXU_DTYPE), cm_ref[0, :, WB:])
        _store_permuted(ys_ref, ys + d_ref[0] * u)

    return _call(
        body, name="s5_fwd", grid=(N_GBLK, L // tm), exchanges=exchanges, semantics=("arbitrary", "arbitrary"),
        operands=(u4, bmat, cmat, coef, dskip),
        in_specs=[
            _slab_spec(UB, tm, lambda b, i: (b, i, 0)),
            pl.BlockSpec((1, UB, 2 * WB), lambda b, i: (b, 0, 0)),
            pl.BlockSpec((1, UB, 2 * WB), lambda b, i: (b, 0, 0)),
            pl.BlockSpec((N_TABLES, SUBLANES, WB), lambda b, i: (0, 0, b)),
            pl.BlockSpec((1, 1, UB), lambda b, i: (b, 0, 0)),
        ],
        out_specs=[
            pl.BlockSpec((tm, WB), lambda b, i: (i, b)),
            pl.BlockSpec((tm, WB), lambda b, i: (i, b)),
            _slab_spec(UB, tm, lambda b, i: (b, i, 0)),
        ],
        out_shape=[
            jax.ShapeDtypeStruct((L, N_GBLK * WB), F32),
            jax.ShapeDtypeStruct((L, N_GBLK * WB), F32),
            _slab_shape(L, D_SSM),
        ],
        scratch_shapes=[pltpu.VMEM((SUBLANES, WB), F32), pltpu.VMEM((SUBLANES, WB), F32)],
    )


def _tail_fwd(x, ys, proj, w_glu, conv_w, g_ssm, g_conv, w_out, g_post, exchanges=()):
    L, D = x.shape
    tm = _tile(L, TM_TAIL)

    def body(x_ref, ys_ref, h_ref, bg_ref, cg_ref, wglu_ref, cw_ref, gs_ref, gc_ref, wout_ref, gp_ref,
             ycat_ref, o_ref, x1_ref, zbuf):
        @pl.when(pl.program_id(0) == 0)
        def _():
            zbuf[0:SUBLANES, :] = jnp.zeros((SUBLANES, D_CONV), F32)

        y1, _ = _gelu(_load_slabs(ys_ref))
        y2 = y1 * jax.nn.sigmoid(_dot(y1.astype(MXU_DTYPE), wglu_ref[...]))
        ycat_ref[:, :D_SSM] = _rms_fwd(y2, gs_ref[...]).astype(MXU_DTYPE)
        z = cg_ref[...] * h_ref[...]
        zbuf[SUBLANES:, :] = z
        conv = cw_ref[0:1, :] * zbuf[SUBLANES - 2:SUBLANES - 2 + tm, :] + cw_ref[1:2, :] * zbuf[SUBLANES - 1:SUBLANES - 1 + tm, :] + cw_ref[2:3, :] * z
        zbuf[0:SUBLANES, :] = zbuf[tm:tm + SUBLANES, :]
        ycat_ref[:, D_SSM:] = _rms_fwd(bg_ref[...] * conv, gc_ref[...]).astype(MXU_DTYPE)
        o = _dot(ycat_ref[...], wout_ref[...])
        o_ref[...] = o
        x1_ref[...] = x_ref[...] + _rms_fwd(o, gp_ref[...])

    row = lambda i: (i, 0)
    return _call(
        body, name="tail_fwd", grid=(L // tm,), exchanges=exchanges, semantics=("arbitrary",),
        operands=(x, ys, proj, proj, proj, w_glu, conv_w, g_ssm, g_conv, w_out, g_post),
        in_specs=[
            pl.BlockSpec((tm, D), row), _slab_spec(D_SSM, tm),
            pl.BlockSpec((tm, D_CONV), lambda i: (i, 1)), pl.BlockSpec((tm, D_CONV), lambda i: (i, 2)),
            pl.BlockSpec((tm, D_CONV), lambda i: (i, 3)),
            _resident(w_glu.shape), _resident(conv_w.shape), _resident(g_ssm.shape), _resident(g_conv.shape),
            _resident(w_out.shape), _resident(g_post.shape),
        ],
        out_specs=[pl.BlockSpec((tm, D), row), pl.BlockSpec((tm, D), row), pl.BlockSpec((tm, D), row)],
        out_shape=[jax.ShapeDtypeStruct((L, D), MXU_DTYPE), jax.ShapeDtypeStruct((L, D), F32), jax.ShapeDtypeStruct((L, D), F32)],
        scratch_shapes=[pltpu.VMEM((tm + SUBLANES, D_CONV), F32)],
    )


def _mlp_fwd(x1, target, w_up_all, w_down_a, w_down_b, g_pre, g_post):
    L, D = x1.shape
    ns, _, fc = w_up_all.shape
    half = w_down_a.shape[1]
    tm = _tile(L, TM_MLP)

    def body(x1_ref, t_ref, wup_ref, wda_ref, wdb_ref, gpre_ref, gpost_ref, hn2_ref, up_ref, m_ref, dx2_ref, loss_ref):
        @pl.when(pl.program_id(0) == 0)
        def _():
            loss_ref[...] = jnp.zeros_like(loss_ref)

        x1v = x1_ref[...]
        hn2 = _rms_fwd(x1v, gpre_ref[...]).astype(MXU_DTYPE)
        hn2_ref[...] = hn2
        m = jnp.zeros((tm, D), F32)
        for j in range(ns):
            up = _dot(hn2, wup_ref[j])
            up_ref[:, j * fc:(j + 1) * fc] = up
            act = jnp.square(jnp.maximum(up, 0.0)).astype(MXU_DTYPE)
            m = m + _dot(act[:, :half], wda_ref[j]) + _dot(act[:, half:], wdb_ref[j])
        m_ref[...] = m
        err = x1v + _rms_fwd(m, gpost_ref[...]) - t_ref[...]
        loss_ref[...] += 0.5 * jnp.sum(jnp.mean(err * err, axis=-1, keepdims=True))
        dx2_ref[...] = err * (1.0 / D)

    row = lambda i: (i, 0)
    return pl.pallas_call(
        body, name="mlp_fwd", grid=(L // tm,),
        in_specs=[pl.BlockSpec((tm, D), row), pl.BlockSpec((tm, D), row), _resident(w_up_all.shape), _resident(w_down_a.shape),
                  _resident(w_down_b.shape), _resident(g_pre.shape), _resident(g_post.shape)],
        out_specs=[pl.BlockSpec((tm, D), row), pl.BlockSpec((tm, ns * fc), row), pl.BlockSpec((tm, D), row),
                   pl.BlockSpec((tm, D), row), pl.BlockSpec((SUBLANES, 128), lambda i: (0, 0))],
        out_shape=[jax.ShapeDtypeStruct((L, D), MXU_DTYPE), jax.ShapeDtypeStruct((L, ns * fc), F32), jax.ShapeDtypeStruct((L, D), F32),
                   jax.ShapeDtypeStruct((L, D), F32), jax.ShapeDtypeStruct((SUBLANES, 128), F32)],
        compiler_params=_params(("arbitrary",)),
    )(x1, target, w_up_all, w_down_a, w_down_b, g_pre, g_post)


def _mlp_bwd(dx2, m, up, x1, w_up_all, w_down_a, w_down_b, g_pre, g_post):
    L, D = x1.shape
    ns, _, fc = w_up_all.shape
    tm = _tile(L, TM_MLP)

    def body(dx2_ref, m_ref, up_ref, x1_ref, wup_ref, wda_ref, wdb_ref, gpre_ref, gpost_ref,
             dm_ref, dup_ref, act_ref, dx1_ref, dgpost_ref, dgpre_ref):
        @pl.when(pl.program_id(0) == 0)
        def _():
            dgpost_ref[...] = jnp.zeros_like(dgpost_ref)
            dgpre_ref[...] = jnp.zeros_like(dgpre_ref)

        dx2v = dx2_ref[...]
        dm, dg = _rms_bwd(m_ref[...], gpost_ref[...], dx2v)
        dgpost_ref[...] += dg
        dm_b = dm.astype(MXU_DTYPE)
        dm_ref[...] = dm_b
        dhn2 = jnp.zeros((tm, D), F32)
        for j in range(ns):
            cols = slice(j * fc, (j + 1) * fc)
            relu = jnp.maximum(up_ref[:, cols], 0.0)
            act_ref[:, cols] = jnp.square(relu).astype(MXU_DTYPE)
            dact = jnp.concatenate([_dot_nt(dm_b, wda_ref[j]), _dot_nt(dm_b, wdb_ref[j])], axis=1)
            dup = (dact * (2.0 * relu)).astype(MXU_DTYPE)
            dup_ref[:, cols] = dup
            dhn2 = dhn2 + _dot_nt(dup, wup_ref[j])
        dx, dg = _rms_bwd(x1_ref[...], gpre_ref[...], dhn2)
        dgpre_ref[...] += dg
        dx1_ref[...] = dx2v + dx

    row = lambda i: (i, 0)
    vec = pl.BlockSpec((1, D), lambda i: (0, 0))
    return pl.pallas_call(
        body, name="mlp_bwd", grid=(L // tm,),
        in_specs=[pl.BlockSpec((tm, D), row), pl.BlockSpec((tm, D), row), pl.BlockSpec((tm, ns * fc), row), pl.BlockSpec((tm, D), row),
                  _resident(w_up_all.shape), _resident(w_down_a.shape), _resident(w_down_b.shape), _resident(g_pre.shape), _resident(g_post.shape)],
        out_specs=[pl.BlockSpec((tm, D), row), pl.BlockSpec((tm, ns * fc), row), pl.BlockSpec((tm, ns * fc), row),
                   pl.BlockSpec((tm, D), row), vec, vec],
        out_shape=[jax.ShapeDtypeStruct((L, D), MXU_DTYPE), jax.ShapeDtypeStruct((L, ns * fc), MXU_DTYPE),
                   jax.ShapeDtypeStruct((L, ns * fc), MXU_DTYPE), jax.ShapeDtypeStruct((L, D), F32),
                   jax.ShapeDtypeStruct((1, D), F32), jax.ShapeDtypeStruct((1, D), F32)],
        compiler_params=_params(("arbitrary",)),
    )(dx2, m, up, x1, w_up_all, w_down_a, w_down_b, g_pre, g_post)


def _tail_bwd(dx1, o, ys, proj, w_glu, conv_w, g_ssm, g_conv, w_out, g_post, exchanges=()):
    L, D = dx1.shape
    tm = _tile(L, TM_TAIL)
    nt = L // tm
    hb = tm // SUBLANES

    def body(dx1_ref, o_ref, ys_ref, h_ref, bg_ref, cg_ref, hh_ref, hcg_ref, wglu_ref, cw_ref, gs_ref, gc_ref, wout_ref, gp_ref,
             do_ref, da_ref, y1_ref, dys_ref, dhbc_ref, dgp_ref, dgs_ref, dgc_ref, dcw_ref, zbuf, dcbuf):
        step = pl.program_id(0)

        @pl.when(step == 0)
        def _():
            dcbuf[tm:, :] = jnp.zeros((SUBLANES, D_CONV), F32)
            dgp_ref[...] = jnp.zeros_like(dgp_ref)
            dgs_ref[...] = jnp.zeros_like(dgs_ref)
            dgc_ref[...] = jnp.zeros_like(dgc_ref)
            dcw_ref[...] = jnp.zeros_like(dcw_ref)

        do, dg = _rms_bwd(o_ref[...], gp_ref[...], dx1_ref[...])
        dgp_ref[...] += dg
        do_b = do.astype(MXU_DTYPE)
        do_ref[...] = do_b
        dycat = _dot_nt(do_b, wout_ref[...])
        y1, dgelu = _gelu(_load_slabs(ys_ref))
        y1_b = y1.astype(MXU_DTYPE)
        y1_ref[...] = y1_b
        s = jax.nn.sigmoid(_dot(y1_b, wglu_ref[...]))
        dy2, dg = _rms_bwd(y1 * s, gs_ref[...], dycat[:, :D_SSM])
        dgs_ref[...] += dg
        da_b = (dy2 * y1 * s * (1.0 - s)).astype(MXU_DTYPE)
        da_ref[...] = da_b
        _store_slabs(dys_ref, (dy2 * s + _dot_nt(da_b, wglu_ref[...])) * dgelu)
        h = h_ref[...]
        cg = cg_ref[...]
        bg = bg_ref[...]
        z = cg * h
        first = step == nt - 1
        zbuf[0:SUBLANES, :] = jnp.where(first, 0.0, hcg_ref[...] * hh_ref[...])
        zbuf[SUBLANES:, :] = z
        z1 = zbuf[SUBLANES - 1:SUBLANES - 1 + tm, :]
        z2 = zbuf[SUBLANES - 2:SUBLANES - 2 + tm, :]
        conv = cw_ref[0:1, :] * z2 + cw_ref[1:2, :] * z1 + cw_ref[2:3, :] * z
        dyc, dg = _rms_bwd(bg * conv, gc_ref[...], dycat[:, D_SSM:])
        dgc_ref[...] += dg
        dconv = dyc * bg
        dcw_ref[0:1, :] += jnp.sum(dconv * z2, axis=0, keepdims=True)
        dcw_ref[1:2, :] += jnp.sum(dconv * z1, axis=0, keepdims=True)
        dcw_ref[2:3, :] += jnp.sum(dconv * z, axis=0, keepdims=True)
        dcbuf[0:tm, :] = dconv
        dz = cw_ref[2:3, :] * dconv + cw_ref[1:2, :] * dcbuf[1:1 + tm, :] + cw_ref[0:1, :] * dcbuf[2:2 + tm, :]
        dcbuf[tm:, :] = dcbuf[0:SUBLANES, :]
        dhbc_ref[:, 0:D_CONV] = (dz * cg).astype(MXU_DTYPE)
        dhbc_ref[:, D_CONV:2 * D_CONV] = (dyc * conv).astype(MXU_DTYPE)
        dhbc_ref[:, 2 * D_CONV:] = (dz * h).astype(MXU_DTYPE)

    rev = lambda i: (nt - 1 - i, 0)
    slab = lambda i: (0, nt - 1 - i, 0)
    col = lambda c: (lambda i: (nt - 1 - i, c))
    halo = lambda c: (lambda i: (jnp.maximum((nt - 1 - i) * hb - 1, 0), c))
    vec = lambda n: pl.BlockSpec((1, n), lambda i: (0, 0))
    return _call(
        body, name="tail_bwd", grid=(nt,), exchanges=exchanges, semantics=("arbitrary",),
        operands=(dx1, o, ys, proj, proj, proj, proj, proj, w_glu, conv_w, g_ssm, g_conv, w_out, g_post),
        in_specs=[
            pl.BlockSpec((tm, D), rev), pl.BlockSpec((tm, D), rev), _slab_spec(D_SSM, tm, slab),
            pl.BlockSpec((tm, D_CONV), col(1)), pl.BlockSpec((tm, D_CONV), col(2)), pl.BlockSpec((tm, D_CONV), col(3)),
            pl.BlockSpec((SUBLANES, D_CONV), halo(1)), pl.BlockSpec((SUBLANES, D_CONV), halo(3)),
            _resident(w_glu.shape), _resident(conv_w.shape), _resident(g_ssm.shape), _resident(g_conv.shape),
            _resident(w_out.shape), _resident(g_post.shape),
        ],
        out_specs=[
            pl.BlockSpec((tm, D), rev), pl.BlockSpec((tm, D_SSM), rev), pl.BlockSpec((tm, D_SSM), rev), _slab_spec(D_SSM, tm, slab),
            pl.BlockSpec((tm, 3 * D_CONV), rev), vec(D), vec(D_SSM), vec(D_CONV),
            pl.BlockSpec((SUBLANES, D_CONV), lambda i: (0, 0)),
        ],
        out_shape=[
            jax.ShapeDtypeStruct((L, D), MXU_DTYPE), jax.ShapeDtypeStruct((L, D_SSM), MXU_DTYPE), jax.ShapeDtypeStruct((L, D_SSM), MXU_DTYPE),
            _slab_shape(L, D_SSM), jax.ShapeDtypeStruct((L, 3 * D_CONV), MXU_DTYPE),
            jax.ShapeDtypeStruct((1, D), F32), jax.ShapeDtypeStruct((1, D_SSM), F32), jax.ShapeDtypeStruct((1, D_CONV), F32),
            jax.ShapeDtypeStruct((SUBLANES, D_CONV), F32),
        ],
        scratch_shapes=[pltpu.VMEM((tm + SUBLANES, D_CONV), F32), pltpu.VMEM((tm + SUBLANES, D_CONV), F32)],
    )


def _s5_bwd(dys, u4, s_re, s_im, bmat, cmat, coef_rev, dskip, exchanges=()):
    L = dys.shape[1]
    tm = _tile(L, TM_S5)
    nt = L // tm
    lc = min(LANE_CHUNK, WB)

    def body(dys_ref, u_ref, sre_ref, sim_ref, bm_ref, cm_ref, coef_ref, d_ref,
             du_ref, gb_ref, gc_ref, q_ref, gd_ref, dr_ref, di_ref, lr_ref, li_ref, hr_ref, hi_ref, qr_acc, qi_acc, gb_acc, gc_acc):
        step = pl.program_id(1)

        @pl.when(step == 0)
        def _():
            for ref in (hr_ref, hi_ref, qr_acc, qi_acc, gb_acc, gc_acc, gd_ref):
                ref[...] = jnp.zeros_like(ref)

        dys_v = _load_permuted(dys_ref)
        u = _load_permuted(u_ref)
        dys_b = dys_v.astype(MXU_DTYPE)
        u_b = u.astype(MXU_DTYPE)
        d = _dot(dys_b, cm_ref[0])
        dr_ref[...] = d[:, :WB]
        di_ref[...] = d[:, WB:]
        for c in range(WB // lc):
            lanes = slice(c * lc, (c + 1) * lc)
            hr, hi = hr_ref[:, lanes], hi_ref[:, lanes]
            q = [qr_acc[:, lanes], qi_acc[:, lanes]]
            for b in range(tm // SEG_ROWS - 1, -1, -1):
                rows = lambda j, b=b: slice(b * SEG_ROWS + j * SUBLANES, b * SEG_ROWS + (j + 1) * SUBLANES)

                def read(j, rows=rows, lanes=lanes):
                    return dr_ref[rows(j), lanes], di_ref[rows(j), lanes]

                def write(j, xr, xi, rows=rows, lanes=lanes, q=q):
                    lr_ref[rows(j), lanes] = xr
                    li_ref[rows(j), lanes] = xi
                    er = xr - dr_ref[rows(j), lanes]
                    ei = xi - di_ref[rows(j), lanes]
                    sr = sre_ref[rows(j), lanes]
                    si = sim_ref[rows(j), lanes]
                    q[0] = q[0] + (er * sr + ei * si)
                    q[1] = q[1] + (ei * sr - er * si)

                hr, hi = _scan_block(read, write, hr, hi, coef_ref, lanes, True)
            hr_ref[:, lanes] = hr
            hi_ref[:, lanes] = hi
            qr_acc[:, lanes] = q[0]
            qi_acc[:, lanes] = q[1]
        lr_b = lr_ref[...].astype(MXU_DTYPE)
        li_b = li_ref[...].astype(MXU_DTYPE)
        _store_permuted(du_ref, _dot_nt(lr_b, bm_ref[0, :, :WB]) + _dot_nt(li_b, bm_ref[0, :, WB:]) + d_ref[0] * dys_v)
        gb_acc[:, :WB] += _dot_tn(u_b, lr_b)
        gb_acc[:, WB:] += _dot_tn(u_b, li_b)
        gc_acc[:, :WB] += _dot_tn(dys_b, sre_ref[...].astype(MXU_DTYPE))
        gc_acc[:, WB:] += _dot_tn(dys_b, sim_ref[...].astype(MXU_DTYPE))
        gd_ref[0] += jnp.sum(dys_v * u, axis=0, keepdims=True)

        @pl.when(step == nt - 1)
        def _():
            q_ref[0, 0:1, :] = jnp.sum(qr_acc[...], axis=0, keepdims=True)
            q_ref[0, 1:2, :] = jnp.sum(qi_acc[...], axis=0, keepdims=True)
            mask = _group_mask(UB, WB)
            fold = (lax.broadcasted_iota(jnp.int32, (WB, STATE), 0) % STATE == lax.broadcasted_iota(jnp.int32, (WB, STATE), 1)).astype(F32)
            for acc, out in ((gb_acc, gb_ref), (gc_acc, gc_ref)):
                for k in range(2):
                    own = jnp.where(mask, acc[:, k * WB:(k + 1) * WB], 0.0)
                    out[0, k] = jnp.dot(own, fold, precision=lax.Precision.HIGHEST, preferred_element_type=F32)

    rev = lambda b, i: (nt - 1 - i, b)
    slab = lambda b, i: (b, nt - 1 - i, 0)
    blk = lambda b, i: (b, 0, 0)
    blk4 = lambda b, i: (b, 0, 0, 0)
    return _call(
        body, name="s5_bwd", grid=(N_GBLK, nt), exchanges=exchanges, semantics=("arbitrary", "arbitrary"),
        operands=(dys, u4, s_re, s_im, bmat, cmat, coef_rev, dskip),
        in_specs=[
            _slab_spec(UB, tm, slab), _slab_spec(UB, tm, slab), pl.BlockSpec((tm, WB), rev), pl.BlockSpec((tm, WB), rev),
            pl.BlockSpec((1, UB, 2 * WB), blk), pl.BlockSpec((1, UB, 2 * WB), blk),
            pl.BlockSpec((N_TABLES, SUBLANES, WB), lambda b, i: (0, 0, b)), pl.BlockSpec((1, 1, UB), blk),
        ],
        out_specs=[
            _slab_spec(UB, tm, slab), pl.BlockSpec((1, 2, UB, STATE), blk4), pl.BlockSpec((1, 2, UB, STATE), blk4),
            pl.BlockSpec((1, 2, WB), blk), pl.BlockSpec((1, 1, UB), blk),
        ],
        out_shape=[
            _slab_shape(L, D_SSM), jax.ShapeDtypeStruct((N_GBLK, 2, UB, STATE), F32),
            jax.ShapeDtypeStruct((N_GBLK, 2, UB, STATE), F32), jax.ShapeDtypeStruct((N_GBLK, 2, WB), F32),
            jax.ShapeDtypeStruct((N_GBLK, 1, UB), F32),
        ],
        scratch_shapes=[pltpu.VMEM((tm, WB), F32)] * 4 + [pltpu.VMEM((SUBLANES, WB), F32)] * 4 + [pltpu.VMEM((UB, 2 * WB), F32)] * 2,
    )


def _inproj_bwd(du, dhbc, x, dx1, w_in_all, g1):
    L, D = x.shape
    ns, _, nc = w_in_all.shape
    tm = _tile(L, TM_PROJ)

    def body(du_ref, dhbc_ref, x_ref, dx1_ref, w_ref, g_ref, gx_ref, dproj_ref, dg_ref):
        @pl.when(pl.program_id(0) == 0)
        def _():
            dg_ref[...] = jnp.zeros_like(dg_ref)

        du_b = _load_slabs(du_ref).astype(MXU_DTYPE)
        dproj_ref[:, :nc] = du_b
        dproj_ref[:, nc:] = dhbc_ref[...]
        dhn = _dot_nt(du_b, w_ref[0])
        for j in range(1, ns):
            dhn = dhn + _dot_nt(dhbc_ref[:, (j - 1) * nc:j * nc], w_ref[j])
        dx, dg = _rms_bwd(x_ref[...], g_ref[...], dhn)
        dg_ref[...] += dg
        gx_ref[...] = dx1_ref[...] + dx

    row = lambda i: (i, 0)
    return pl.pallas_call(
        body, name="inproj_bwd", grid=(L // tm,),
        in_specs=[_slab_spec(nc, tm), pl.BlockSpec((tm, (ns - 1) * nc), row), pl.BlockSpec((tm, D), row), pl.BlockSpec((tm, D), row),
                  _resident(w_in_all.shape), _resident(g1.shape)],
        out_specs=[pl.BlockSpec((tm, D), row), pl.BlockSpec((tm, ns * nc), row), pl.BlockSpec((1, D), lambda i: (0, 0))],
        out_shape=[jax.ShapeDtypeStruct((L, D), F32), jax.ShapeDtypeStruct((L, ns * nc), MXU_DTYPE), jax.ShapeDtypeStruct((1, D), F32)],
        compiler_params=_params(("arbitrary",)),
    )(du, dhbc, x, dx1, w_in_all, g1)


def _matmul_tn(a, b, name, col_shards=1, exchanges=()):
    L, K = a.shape
    N = b.shape[1]
    tl = _tile(L, TL_TN)
    tk = _tile(K, 1024)
    nw = N // col_shards
    spb = max(1, min(col_shards, TN_TN // nw))
    tn = spb * nw if spb > 1 else _tile(nw, 1024)
    npb = nw // tn if spb == 1 else 1

    def body(a_ref, b_ref, o_ref):
        @pl.when(pl.program_id(2) == 0)
        def _():
            o_ref[...] = jnp.zeros_like(o_ref)

        res = _dot_tn(a_ref[...], b_ref[...])
        for s in range(spb):
            o_ref[s] += res[:, s * nw:(s + 1) * nw] if spb > 1 else res

    out_block = pl.BlockSpec((spb, tk, nw if spb > 1 else tn), (lambda k, n, l: (n, k, 0)) if spb > 1 else (lambda k, n, l: (n // npb, k, n % npb)))
    return _call(
        body, name=name, grid=(K // tk, N // tn, L // tl), exchanges=exchanges, semantics=("arbitrary", "arbitrary", "arbitrary"),
        operands=(a, b),
        in_specs=[pl.BlockSpec((tl, tk), lambda k, n, l: (l, k)), pl.BlockSpec((tl, tn), lambda k, n, l: (l, n))],
        out_specs=[out_block],
        out_shape=[jax.ShapeDtypeStruct((col_shards, K, nw), F32)],
    )


def _ssm_discretize(lam_re, lam_im, log_dt, bt_re, bt_im):
    dt = jnp.exp(log_dt)[:, None]
    zr = lam_re * dt
    zi = lam_im * dt
    mag = jnp.exp(zr)
    abr = mag * jnp.cos(zi)
    abi = mag * jnp.sin(zi)
    nr, ni = abr - 1.0, abi
    den = lam_re * lam_re + lam_im * lam_im
    coef_r = ((nr * lam_re + ni * lam_im) / den)[:, None, :]
    coef_i = ((ni * lam_re - nr * lam_im) / den)[:, None, :]
    return zr, zi, coef_r * bt_re - coef_i * bt_im, coef_r * bt_im + coef_i * bt_re


def _scan_tables(ar, ai, reverse):
    rows = np.arange(SUBLANES)
    exps = np.zeros((N_TABLES // 2, SUBLANES), np.int32)
    keep = np.ones((N_TABLES // 2, SUBLANES), bool)
    for t, k in enumerate((1, 2, 4)):
        exps[t] = SUBLANES * k
        keep[t] = (rows + k <= SUBLANES - 1) if reverse else (rows >= k)
    exps[3] = SUBLANES * (SUBLANES - rows) if reverse else SUBLANES * (rows + 1)
    for j in range(SUBLANES):
        exps[4 + j] = SUBLANES - j if reverse else j + 1
    pr, pi = ar, (-ai if reverse else ai)
    shape = (N_TABLES // 2, SUBLANES, ar.shape[0])
    xr, xi = jnp.ones(shape, F32), jnp.zeros(shape, F32)
    for bit in range(int(exps.max()).bit_length()):
        on = ((exps >> bit) & 1).astype(bool)[:, :, None]
        xr, xi = jnp.where(on, xr * pr - xi * pi, xr), jnp.where(on, xr * pi + xi * pr, xi)
        pr, pi = pr * pr - pi * pi, 2.0 * pr * pi
    xr = jnp.where(keep[:, :, None], xr, 0.0)
    xi = jnp.where(keep[:, :, None], xi, 0.0)
    return jnp.stack([xr, xi], axis=1).reshape(N_TABLES, SUBLANES, ar.shape[0])


def _group_mask(rows, cols):
    r = lax.broadcasted_iota(jnp.int32, (rows, cols), 0) // GROUP
    c = lax.broadcasted_iota(jnp.int32, (rows, cols), 1) // STATE
    return r == c


def _ssm_expand(bt_re, bt_im, c_re, c_im, exchanges=()):
    flat = lambda a: a.reshape(N_GROUPS * GROUP, STATE)

    def body(br_ref, bi_ref, cr_ref, ci_ref, bm_ref, cm_ref):
        spread = (lax.broadcasted_iota(jnp.int32, (STATE, WB), 1) % STATE == lax.broadcasted_iota(jnp.int32, (STATE, WB), 0)).astype(F32)
        mask = _group_mask(UB, WB)

        def expand(x):
            wide = jnp.dot(x, spread, precision=lax.Precision.HIGHEST, preferred_element_type=F32)
            return jnp.where(mask, wide, 0.0).astype(MXU_DTYPE)

        bm_ref[0, :, :WB] = expand(br_ref[...])
        bm_ref[0, :, WB:] = expand(bi_ref[...])
        cm_ref[0, :, :WB] = expand(cr_ref[...])
        cm_ref[0, :, WB:] = expand(-ci_ref[...])

    spec = pl.BlockSpec((UB, STATE), lambda b: (b, 0))
    out = pl.BlockSpec((1, UB, 2 * WB), lambda b: (b, 0, 0))
    return _call(
        body, name="ssm_expand", grid=(N_GBLK,), exchanges=exchanges, semantics=("arbitrary",),
        operands=(flat(bt_re), flat(bt_im), flat(c_re), flat(c_im)), in_specs=[spec] * 4, out_specs=[out, out],
        out_shape=[jax.ShapeDtypeStruct((N_GBLK, UB, 2 * WB), MXU_DTYPE)] * 2,
    )


def _ssm_matrices(lam_re, lam_im, log_dt, bt_re, bt_im, c_re, c_im, exchanges=()):
    zr, zi, bbar_r, bbar_i = _ssm_discretize(lam_re, lam_im, log_dt, bt_re, bt_im)
    mag = jnp.exp(zr)
    ar = (mag * jnp.cos(zi)).reshape(-1)
    ai = (mag * jnp.sin(zi)).reshape(-1)
    bmat, cmat, *rest = _ssm_expand(bbar_r, bbar_i, c_re, c_im, exchanges)
    return bmat, cmat, _scan_tables(ar, ai, False), _scan_tables(ar, ai, True), rest


def _ssm_param_grads(lam_re, lam_im, log_dt, bt_re, bt_im, gb, gc, q, gd):
    part = lambda g, k: g[:, k].reshape(N_GROUPS, GROUP, STATE)
    qr = q[:, 0, :].reshape(N_GROUPS, STATE)
    qi = q[:, 1, :].reshape(N_GROUPS, STATE)
    _, vjp = jax.vjp(_ssm_discretize, lam_re, lam_im, log_dt, bt_re, bt_im)
    d_lam_re, d_lam_im, d_log_dt, d_bt_re, d_bt_im = vjp((qr, qi, part(gb, 0), part(gb, 1)))
    return d_lam_re, d_lam_im, d_log_dt, d_bt_re, d_bt_im, part(gc, 0), -part(gc, 1), gd.reshape(N_GROUPS, GROUP)


def _row_tile(rows, n):
    return _tile(rows, max(SUBLANES, (2 * 1024 * 1024) // (4 * n)))


def _pair_add(grad, other, core, name):
    ns, h, n = other.shape
    tr = _row_tile(h, n)
    nb = h // tr

    def body(c_ref, g_ref, o_ref, out_ref):
        out_ref[...] = (g_ref[...] + o_ref[...]).astype(WIRE_DTYPE)

    return pl.pallas_call(
        body, name=name,
        grid_spec=pltpu.PrefetchScalarGridSpec(
            num_scalar_prefetch=1, grid=(ns, nb),
            in_specs=[pl.BlockSpec((1, tr, n), lambda s, i, c: (s, c[0] * nb + i, 0)), pl.BlockSpec((1, tr, n), lambda s, i, c: (s, i, 0))],
            out_specs=pl.BlockSpec((1, tr, n), lambda s, i, c: (s, i, 0))),
        out_shape=jax.ShapeDtypeStruct(other.shape, WIRE_DTYPE),
        compiler_params=_params(("arbitrary", "arbitrary")),
    )(core, grad, other)


def _quad_sum(parts, core, name):
    ns, h, n = parts.shape
    tr = _row_tile(h, n)
    nb = h // tr

    def body(c_ref, p_ref, out_ref):
        p = [p_ref[k].astype(F32) for k in range(ns)]
        out_ref[...] = ((p[0] + p[1]) + p[2]) + p[3]

    return pl.pallas_call(
        body, name=name,
        grid_spec=pltpu.PrefetchScalarGridSpec(
            num_scalar_prefetch=1, grid=(nb,),
            in_specs=[pl.BlockSpec((ns, tr, n), lambda i, c: (0, i, 0))],
            out_specs=pl.BlockSpec((tr, n), lambda i, c: (c[0] * nb + i, 0))),
        out_shape=jax.ShapeDtypeStruct((2 * h, n), F32),
        compiler_params=_params(("arbitrary",)),
    )(core, parts)


def _adamw_math(w, g, m, v):
    m = ADAM_B1 * m + (1.0 - ADAM_B1) * g
    v = ADAM_B2 * v + (1.0 - ADAM_B2) * jnp.square(g)
    m_hat = m / (1.0 - ADAM_B1 ** ADAM_STEP)
    v_hat = v / (1.0 - ADAM_B2 ** ADAM_STEP)
    delta = -ADAM_LR * (m_hat / (jnp.sqrt(v_hat) + ADAM_EPS) + ADAM_WD * w)
    return delta, m, v


def _adamw(w, g, m, v, name):
    r, n = w.shape
    tr = _row_tile(r, n)

    def body(w_ref, g_ref, m_ref, v_ref, d_ref, nm_ref, nv_ref):
        d_ref[...], nm_ref[...], nv_ref[...] = _adamw_math(w_ref[...], g_ref[...], m_ref[...], v_ref[...])

    spec = pl.BlockSpec((tr, n), lambda i: (i, 0))
    return pl.pallas_call(
        body, name=name, grid=(r // tr,), in_specs=[spec] * 4, out_specs=[spec] * 3,
        out_shape=[jax.ShapeDtypeStruct((r, n), F32)] * 3,
        compiler_params=_params(("arbitrary",)),
    )(w, g, m, v)


LANES = 128
SMALL = ["g_pre_mix", "lam_re", "lam_im", "log_dt", "b_re", "b_im", "c_re", "c_im", "d_skip", "conv_w", "g_ssm_out", "g_conv_out",
         "g_post_mix", "g_pre_mlp", "g_post_mlp"]
TILE_SLOTS = {"b_re": (0, N_GROUPS), "b_im": (N_GROUPS, N_GROUPS), "c_re": (2 * N_GROUPS, N_GROUPS), "c_im": (3 * N_GROUPS, N_GROUPS),
              "lam_re": (4 * N_GROUPS, 2), "lam_im": (4 * N_GROUPS + 2, 2)}
N_TILE_SLOTS = 4 * N_GROUPS + 4
VEC_ROWS = {"g_pre_mix": 0, "g_post_mix": 1, "g_pre_mlp": 2, "g_post_mlp": 3, "g_ssm_out": 4, "g_conv_out": 5, "log_dt": 6}
ROW_LOSS, ROW_DSKIP, ROW_CONV, N_PACK_ROWS = 7, 8, 24, 32


def _kernel_form(name, a):
    if name in ("b_re", "b_im"):
        return jnp.transpose(a, (0, 1, 3, 2)).reshape(N_GROUPS, GROUP, STATE)
    if name in ("c_re", "c_im"):
        return a.reshape(N_GROUPS, GROUP, STATE)
    if name in ("lam_re", "lam_im"):
        return a.reshape(2, GROUP, STATE)
    if name == "d_skip":
        return jnp.transpose(a, (0, 2, 1)).reshape(GROUP, N_GROUPS)
    if name == "conv_w":
        return jnp.transpose(a, (1, 0, 2))
    return a


def _param_form(name, k):
    if name in ("b_re", "b_im"):
        return jnp.transpose(k.reshape(1, N_GROUPS, GROUP, STATE), (0, 1, 3, 2))
    if name in ("c_re", "c_im"):
        return k.reshape(1, N_GROUPS, GROUP, STATE)
    if name in ("lam_re", "lam_im"):
        return k.reshape(1, N_GROUPS, STATE)
    if name == "d_skip":
        return jnp.transpose(k.reshape(1, GROUP, N_GROUPS), (0, 2, 1))
    if name == "conv_w":
        return jnp.transpose(k, (1, 0, 2))
    return k


def _pack_tiles(g):
    lam = lambda a: a.reshape(2, GROUP, STATE)
    tiles = jnp.concatenate([g["b_re"], g["b_im"], g["c_re"], g["c_im"], lam(g["lam_re"]), lam(g["lam_im"])], axis=0)
    return tiles.astype(WIRE_DTYPE)


def _pack_rows(g, loss):
    row = lambda a: jnp.pad(a, ((0, 0), (0, D_MODEL - a.shape[1])))
    rows = [row(g[k][None]) for k in VEC_ROWS] + [row(loss[0:1]), row(g["d_skip"].T), row(g["conv_w"])]
    rows.append(jnp.zeros((N_PACK_ROWS - ROW_CONV - 3, D_MODEL), F32))
    return jnp.concatenate(rows, axis=0)


def _adamw_small(tiles, rows, w, m, v):
    nn = len(SMALL)

    def body(*refs):
        t_ref, r_ref = refs[0], refs[1]
        w_refs, m_refs, v_refs = refs[2:2 + nn], refs[2 + nn:2 + 2 * nn], refs[2 + 2 * nn:2 + 3 * nn]
        loss_ref, outs = refs[2 + 3 * nn], refs[3 + 3 * nn:]

        def tile_sum(first, count):
            total = t_ref[0, first:first + count].astype(F32)
            for d in range(1, N_DEV):
                total = total + t_ref[d, first:first + count].astype(F32)
            return total

        def row_sum(first, count, lanes):
            total = r_ref[0, first:first + count, 0:lanes]
            for d in range(1, N_DEV):
                total = total + r_ref[d, first:first + count, 0:lanes]
            return total

        def step(j, g, at=lambda ref: ref):
            delta, nm, nv = _adamw_math(at(w_refs[j])[...], g, at(m_refs[j])[...], at(v_refs[j])[...])
            at(outs[j])[...] = g
            at(outs[nn + j])[...] = delta
            at(outs[2 * nn + j])[...] = nm
            at(outs[3 * nn + j])[...] = nv

        loss_ref[...] = row_sum(ROW_LOSS, 1, LANES)
        chip = 2 * lax.axis_index("x") + lax.axis_index("y")
        for j, name in enumerate(SMALL):
            if name in TILE_SLOTS:
                step(j, tile_sum(*TILE_SLOTS[name]))
            elif name == "d_skip":
                step(j, row_sum(ROW_DSKIP, GROUP, N_GROUPS))
            elif name == "conv_w":
                full = row_sum(ROW_CONV, 3, D_CONV)
                mine = full[:, 0:LANES]
                for s in range(1, N_CHIPS):
                    mine = jnp.where(chip == s, full[:, s * LANES:(s + 1) * LANES], mine)
                for k in range(3):
                    step(j, mine[k:k + 1, :], at=lambda ref, k=k: ref.at[k])
            else:
                step(j, row_sum(VEC_ROWS[name], 1, w_refs[j].shape[1]))

    args = [tiles, rows] + [w[k] for k in SMALL] + [m[k] for k in SMALL] + [v[k] for k in SMALL]
    res = pl.pallas_call(
        body, name="adamw_small", in_specs=[VMEM] * len(args), out_specs=[VMEM] * (1 + 4 * nn),
        out_shape=[jax.ShapeDtypeStruct((1, LANES), F32)] + [jax.ShapeDtypeStruct(w[k].shape, F32) for k in SMALL] * 4,
        compiler_params=pltpu.CompilerParams(vmem_limit_bytes=VMEM_LIMIT),
    )(*args)
    return res[0], [dict(zip(SMALL, res[1 + q * nn:1 + (q + 1) * nn])) for q in range(4)]


WEIGHTS = ["g_pre_mix", "w_in", "lam_re", "lam_im", "log_dt", "b_re", "b_im", "c_re", "c_im", "d_skip", "w_glu", "conv_w",
           "g_ssm_out", "g_conv_out", "w_out", "g_post_mix", "g_pre_mlp", "w_up", "w_down", "g_post_mlp"]
BIG = ["w_in", "w_glu", "w_out", "w_up", "w_down"]


def kernel(x, g_pre_mix, w_in, lam_re, lam_im, log_dt, b_re, b_im, c_re, c_im, d_skip, w_glu, conv_w, g_ssm_out, g_conv_out, w_out, g_post_mix, g_pre_mlp, w_up, w_down, g_post_mlp, loss_target, m_g_pre_mix, m_w_in, m_lam_re, m_lam_im, m_log_dt, m_b_re, m_b_im, m_c_re, m_c_im, m_d_skip, m_w_glu, m_conv_w, m_g_ssm_out, m_g_conv_out, m_w_out, m_g_post_mix, m_g_pre_mlp, m_w_up, m_w_down, m_g_post_mlp, v_g_pre_mix, v_w_in, v_lam_re, v_lam_im, v_log_dt, v_b_re, v_b_im, v_c_re, v_c_im, v_d_skip, v_w_glu, v_conv_w, v_g_ssm_out, v_g_conv_out, v_w_out, v_g_post_mix, v_g_pre_mlp, v_w_up, v_w_down, v_g_post_mlp):
    w = dict(g_pre_mix=g_pre_mix, w_in=w_in, lam_re=lam_re, lam_im=lam_im, log_dt=log_dt, b_re=b_re, b_im=b_im, c_re=c_re, c_im=c_im,
             d_skip=d_skip, w_glu=w_glu, conv_w=conv_w, g_ssm_out=g_ssm_out, g_conv_out=g_conv_out, w_out=w_out, g_post_mix=g_post_mix,
             g_pre_mlp=g_pre_mlp, w_up=w_up, w_down=w_down, g_post_mlp=g_post_mlp)
    m = dict(g_pre_mix=m_g_pre_mix, w_in=m_w_in, lam_re=m_lam_re, lam_im=m_lam_im, log_dt=m_log_dt, b_re=m_b_re, b_im=m_b_im, c_re=m_c_re,
             c_im=m_c_im, d_skip=m_d_skip, w_glu=m_w_glu, conv_w=m_conv_w, g_ssm_out=m_g_ssm_out, g_conv_out=m_g_conv_out, w_out=m_w_out,
             g_post_mix=m_g_post_mix, g_pre_mlp=m_g_pre_mlp, w_up=m_w_up, w_down=m_w_down, g_post_mlp=m_g_post_mlp)
    v = dict(g_pre_mix=v_g_pre_mix, w_in=v_w_in, lam_re=v_lam_re, lam_im=v_lam_im, log_dt=v_log_dt, b_re=v_b_re, b_im=v_b_im, c_re=v_c_re,
             c_im=v_c_im, d_skip=v_d_skip, w_glu=v_w_glu, conv_w=v_conv_w, g_ssm_out=v_g_ssm_out, g_conv_out=v_g_conv_out, w_out=v_w_out,
             g_post_mix=v_g_post_mix, g_pre_mlp=v_g_pre_mlp, w_up=v_w_up, w_down=v_w_down, g_post_mlp=v_g_post_mlp)
    w_dev, m_dev, v_dev = w, m, v
    w, m, v = ({k: a[0] for k, a in d.items()} for d in (w, m, v))
    core = lax.axis_index("c").astype(jnp.int32).reshape(1)

    xs, target = x[0], loss_target[0]
    g1 = w["g_pre_mix"][None]
    g_ssm, g_conv = w["g_ssm_out"][None], w["g_conv_out"][None]
    g_post_mix, g_pre_mlp, g_post_mlp = w["g_post_mix"][None], w["g_pre_mlp"][None], w["g_post_mlp"][None]
    bt_re, bt_im = (jnp.transpose(w[k], (0, 2, 1)) for k in ("b_re", "b_im"))
    dskip = w["d_skip"].reshape(N_GBLK, 1, UB)
    shard = {k: w[k].astype(MXU_DTYPE) for k in BIG}
    conv_pad = jnp.pad(w["conv_w"], ((0, SUBLANES - 3), (0, 0)))

    bmat, cmat, coef_f, coef_r, (w_in_all,) = _ssm_matrices(
        w["lam_re"], w["lam_im"], w["log_dt"], bt_re, bt_im, w["c_re"], w["c_im"], exchanges=[_GatherForward([shard["w_in"]])])
    hn, proj, u4, w_glu_all, w_out_all, conv_all = _inproj_fwd(
        xs, g1, w_in_all, exchanges=[_Gather([shard["w_glu"], shard["w_out"], conv_pad], [False, False, False])])
    wd_half = shard["w_down"].shape[0] // 2
    wda, wdb = shard["w_down"][:wd_half], shard["w_down"][wd_half:]
    s_re, s_im, ys, w_up_all, wda_all = _s5_fwd(u4, bmat, cmat, coef_f, dskip, exchanges=[_Gather([shard["w_up"], wda], [True, True])])
    w_glu_f, w_out_f = w_glu_all.reshape(D_SSM, D_SSM), w_out_all.reshape(D_MODEL, D_MODEL)
    conv_f = jnp.transpose(conv_all, (1, 0, 2)).reshape(SUBLANES, D_CONV)
    ycat, o, x1, w_up_all, wda_all, wdb_all = _tail_fwd(xs, ys, proj, w_glu_f, conv_f, g_ssm, g_conv, w_out_f, g_post_mix,
                                                        exchanges=[_Forward([w_up_all, wda_all]), _GatherForward([wdb])])
    hn2, up, m_act, dx2, loss = _mlp_fwd(x1, target, w_up_all, wda_all, wdb_all, g_pre_mlp, g_post_mlp)

    dm, dup, act, dx1, dg_post_mlp, dg_pre_mlp = _mlp_bwd(dx2, m_act, up, x1, w_up_all, wda_all, wdb_all, g_pre_mlp, g_post_mlp)
    gw_down = _matmul_tn(act, dm, "dw_down")[0].reshape(N_CHIPS, D_FF // N_CHIPS, D_MODEL)
    gw_up = _matmul_tn(hn2, dup, "dw_up", col_shards=N_CHIPS)[0]
    do, da, y1, dys, dhbc, dg_post_mix, dg_ssm, dg_conv, dconv_w, o_down, o_up = _tail_bwd(
        dx1, o, ys, proj, w_glu_f, conv_f, g_ssm, g_conv, w_out_f, g_post_mix, exchanges=[_Pair([gw_down, gw_up])])
    p_down = _pair_add(gw_down, o_down, core, "pair_add_w_down")
    p_up = _pair_add(gw_up, o_up, core, "pair_add_w_up")
    gw_out = _matmul_tn(ycat, do, "dw_out")[0].reshape(N_CHIPS, D_MODEL // N_CHIPS, D_MODEL)
    gw_glu = _matmul_tn(y1, da, "dw_glu")[0].reshape(N_CHIPS, D_SSM // N_CHIPS, D_SSM)
    du, gb, gc, q, gd, q_down, q_up, o_out, o_glu = _s5_bwd(
        dys, u4, s_re, s_im, bmat, cmat, coef_r, dskip, exchanges=[_Chip([p_down, p_up]), _Pair([gw_out, gw_glu])])
    h_down = _quad_sum(q_down, core, "quad_sum_w_down")
    h_up = _quad_sum(q_up, core, "quad_sum_w_up")
    p_out = _pair_add(gw_out, o_out, core, "pair_add_w_out")
    p_glu = _pair_add(gw_glu, o_glu, core, "pair_add_w_glu")
    grad_x, dproj, dg_pre_mix = _inproj_bwd(du, dhbc, xs, dx1, w_in_all, g1)
    d_lam_re, d_lam_im, d_log_dt, d_b_re, d_b_im, d_c_re, d_c_im, d_d_skip = _ssm_param_grads(
        w["lam_re"], w["lam_im"], w["log_dt"], bt_re, bt_im, gb, gc, q, gd)
    small = {
        "g_pre_mix": dg_pre_mix[0], "lam_re": d_lam_re, "lam_im": d_lam_im, "log_dt": d_log_dt, "b_re": d_b_re, "b_im": d_b_im,
        "c_re": d_c_re, "c_im": d_c_im, "d_skip": d_d_skip, "conv_w": dconv_w[:3], "g_ssm_out": dg_ssm[0], "g_conv_out": dg_conv[0],
        "g_post_mix": dg_post_mix[0], "g_pre_mlp": dg_pre_mlp[0], "g_post_mlp": dg_post_mlp[0],
    }
    gw_in, g_down, g_up, q_out, q_glu, tiles, rows = _matmul_tn(
        hn, dproj, "dw_in", col_shards=N_CHIPS,
        exchanges=[_Share([h_down, h_up]), _Chip([p_out, p_glu]), _GatherSmall(_pack_tiles(small)), _GatherSmall(_pack_rows(small, loss))])
    h_out = _quad_sum(q_out, core, "quad_sum_w_out")
    h_glu = _quad_sum(q_glu, core, "quad_sum_w_glu")
    (o_in,) = _run_exchanges([_Pair([gw_in])], "rs_pair_w_in")
    p_in = _pair_add(gw_in, o_in, core, "pair_add_w_in")
    q_in, g_out, g_glu = _run_exchanges([_Chip([p_in]), _Share([h_out, h_glu])], "rs_chip_w_in")
    h_in = _quad_sum(q_in, core, "quad_sum_w_in")
    (g_in,) = _run_exchanges([_Share([h_in])], "rs_share_w_in")
    shard_grads = {"w_in": g_in, "w_glu": g_glu, "w_out": g_out, "w_up": g_up, "w_down": g_down}

    out = {q: {} for q in ("grad", "delta", "new_m", "new_v")}
    for k in BIG:
        out["grad"][k] = shard_grads[k][None]
        delta, new_m, new_v = _adamw(w[k], shard_grads[k], m[k], v[k], "adamw_" + k)
        out["delta"][k], out["new_m"][k], out["new_v"][k] = delta[None], new_m[None], new_v[None]
    form = lambda d: {k: _kernel_form(k, d[k]) for k in SMALL}
    loss, res = _adamw_small(tiles, rows, form(w_dev), form(m_dev), form(v_dev))
    for q, d in zip(("grad", "delta", "new_m", "new_v"), res):
        out[q].update({k: _param_form(k, d[k]) for k in SMALL})
    flat = [loss[0, 0], grad_x[None]]
    for q in ("grad", "delta", "new_m", "new_v"):
        flat += [out[q][k] for k in WEIGHTS]
    return tuple(flat)
```

```python
import functools
import math

import jax
import jax.numpy as jnp
import numpy as np
from jax import lax
from jax.experimental import pallas as pl
from jax.experimental.pallas import tpu as pltpu

F32 = jnp.float32
MXU_DTYPE = jnp.bfloat16
WIRE_DTYPE = jnp.bfloat16

D_MODEL = 1024
D_SSM = 512
D_CONV = 512
N_GROUPS = 32
GROUP = 16
STATE = 64
D_FF = 4096
RMS_EPS = 1e-6
N_CHIPS = 4
N_DEV = 8

ADAM_LR = 0.001
ADAM_B1 = 0.9
ADAM_B2 = 0.999
ADAM_EPS = 1e-08
ADAM_WD = 0.01
ADAM_STEP = 10

N_GBLK = 2
G_PER_BLK = N_GROUPS // N_GBLK
UB = G_PER_BLK * GROUP
WB = G_PER_BLK * STATE
LANE_CHUNK = 256
SUBLANES = 8
N_TABLES = 24

TM_PROJ = 512
TM_S5 = 512
TM_TAIL = 512
TM_MLP = 256
TL_TN = 2048
TK_TN = 1024
TN_TN = 2048
VMEM_LIMIT = 56 * 1024 * 1024

MESH = pl.DeviceIdType.MESH


def _params(sem, vmem=VMEM_LIMIT):
    return pltpu.CompilerParams(dimension_semantics=sem, vmem_limit_bytes=vmem)


def _resident(shape):
    nd = len(shape)
    return pl.BlockSpec(shape, lambda *_: (0,) * nd, pipeline_mode=pl.Buffered(1))


def _dot(a, b):
    return jnp.dot(a, b, preferred_element_type=F32)


def _dot_nt(a, b):
    return lax.dot_general(a, b, (((1,), (1,)), ((), ())), preferred_element_type=F32)


def _dot_tn(a, b):
    return lax.dot_general(a, b, (((0,), (0,)), ((), ())), preferred_element_type=F32)


def _rms_fwd(x, g):
    r = lax.rsqrt(jnp.mean(x * x, axis=-1, keepdims=True) + RMS_EPS)
    return x * r * g


def _rms_bwd(x, g, dy):
    r = lax.rsqrt(jnp.mean(x * x, axis=-1, keepdims=True) + RMS_EPS)
    xn = x * r
    q = dy * g
    dx = r * (q - xn * jnp.mean(q * xn, axis=-1, keepdims=True))
    return dx, jnp.sum(dy * xn, axis=0, keepdims=True)


_GELU_C = math.sqrt(2.0 / math.pi)


def _gelu(x):
    t = jnp.tanh(_GELU_C * (x + 0.044715 * (x * x * x)))
    y = x * (0.5 * (1.0 + t))
    dy = 0.5 * (1.0 + t) + 0.5 * x * (1.0 - t * t) * (_GELU_C * (1.0 + 3 * 0.044715 * (x * x)))
    return y, dy


def _tile(n, pref):
    t = min(n, pref)
    assert n % t == 0, (n, t)
    return t


HBM = pl.BlockSpec(memory_space=pltpu.HBM)
VMEM = pl.BlockSpec(memory_space=pltpu.VMEM)
DMA_SEMS = pltpu.SemaphoreType.DMA


def _place():
    x, y, c = lax.axis_index("x"), lax.axis_index("y"), lax.axis_index("c")
    chips = [(1 - x, y), (x, 1 - y), (1 - x, 1 - y)]
    return (x, y, c), 2 * x + y, (x, y, 1 - c), chips, [2 * px + py for px, py in chips]


def _remote(src, dst, send_sem, recv_sem, device):
    return pltpu.make_async_remote_copy(src_ref=src, dst_ref=dst, send_sem=send_sem, recv_sem=recv_sem,
                                        device_id=device, device_id_type=MESH)


def _half(rows, c):
    return pl.ds(c * (rows // 2), rows // 2)


class _Exchange:
    aliases = {}

    def start(self, ins, outs, sems):
        local, outgoing, _ = self._copies(ins, outs, sems)
        for cp in local + outgoing:
            cp.start()

    def finish(self, ins, outs, sems):
        local, outgoing, incoming = self._copies(ins, outs, sems)
        for cp in incoming:
            cp.wait_recv()
        for cp in outgoing:
            cp.wait_send()
        for cp in local:
            cp.wait()


class _Gather(_Exchange):
    def __init__(self, shards, split):
        self.inputs, self.split = list(shards), split
        self.out_shape = [jax.ShapeDtypeStruct((N_CHIPS, *a.shape), a.dtype) for a in shards]
        self.sems = [DMA_SEMS((len(shards), 3)), DMA_SEMS((len(shards), 3)), DMA_SEMS((len(shards),))]

    def _copies(self, ins, outs, sems):
        send, recv, lsem = sems
        (x, y, c), me, sibling, chips, ids = _place()
        local = [pltpu.make_async_copy(ins[t], outs[t].at[me], lsem.at[t]) for t in range(len(ins))]
        outgoing, incoming = [], []
        for t, a in enumerate(self.inputs):
            rows = _half(a.shape[0], c) if self.split[t] else pl.ds(0, a.shape[0])
            for k in range(3):
                to = (*chips[k], c)
                outgoing.append(_remote(ins[t].at[rows, :], outs[t].at[me, rows, :], send.at[t, k], recv.at[t, k], to))
                incoming.append(_remote(ins[t].at[rows, :], outs[t].at[ids[k], rows, :], send.at[t, k], recv.at[t, k], to))
        return local, outgoing, incoming


class _Forward(_Exchange):
    def __init__(self, arrays):
        self.inputs = list(arrays)
        self.out_shape = [jax.ShapeDtypeStruct(a.shape, a.dtype) for a in arrays]
        self.aliases = {t: t for t in range(len(arrays))}
        self.sems = [DMA_SEMS((len(arrays), 3)), DMA_SEMS((len(arrays), 3))]

    def _copies(self, ins, outs, sems):
        send, recv = sems
        (x, y, c), me, sibling, chips, ids = _place()
        outgoing, incoming = [], []
        for t, a in enumerate(self.inputs):
            for k in range(3):
                mine = outs[t].at[ids[k], _half(a.shape[1], c), :]
                theirs = outs[t].at[ids[k], _half(a.shape[1], 1 - c), :]
                outgoing.append(_remote(mine, mine, send.at[t, k], recv.at[t, k], sibling))
                incoming.append(_remote(theirs, theirs, send.at[t, k], recv.at[t, k], sibling))
        return [], outgoing, incoming


class _GatherForward(_Exchange):
    def __init__(self, shards):
        self.gather = _Gather(shards, [True] * len(shards))
        self.forward = _Forward(self.gather.out_shape)
        self.inputs, self.out_shape = self.gather.inputs, self.gather.out_shape
        self.sems = self.gather.sems + self.forward.sems

    def start(self, ins, outs, sems):
        self.gather.start(ins, outs, sems[:3])

    def finish(self, ins, outs, sems):
        local, outgoing, incoming = self.gather._copies(ins, outs, sems[:3])
        _, passed, from_sibling = self.forward._copies(outs, outs, sems[3:])
        for landed, onward in zip(incoming, passed):
            landed.wait_recv()
            onward.start()
        for cp in from_sibling:
            cp.wait_recv()
        for cp in outgoing + passed:
            cp.wait_send()
        for cp in local:
            cp.wait()


class _Pair(_Exchange):
    def __init__(self, grads):
        self.inputs = list(grads)
        self.out_shape = [jax.ShapeDtypeStruct((g.shape[0], g.shape[1] // 2, g.shape[2]), g.dtype) for g in grads]
        self.sems = [DMA_SEMS((len(grads),)), DMA_SEMS((len(grads),))]

    def _copies(self, ins, outs, sems):
        send, recv = sems
        (x, y, c), me, sibling, chips, ids = _place()
        cps = [_remote(ins[t].at[:, _half(g.shape[1], 1 - c), :], outs[t], send.at[t], recv.at[t], sibling)
               for t, g in enumerate(self.inputs)]
        return [], cps, cps


class _Chip(_Exchange):
    def __init__(self, parts):
        self.inputs = list(parts)
        self.out_shape = [jax.ShapeDtypeStruct(p.shape, p.dtype) for p in parts]
        self.sems = [DMA_SEMS((len(parts), 3)), DMA_SEMS((len(parts), 3)), DMA_SEMS((len(parts),))]

    def _copies(self, ins, outs, sems):
        send, recv, lsem = sems
        (x, y, c), me, sibling, chips, ids = _place()
        local = [pltpu.make_async_copy(ins[t].at[me], outs[t].at[me], lsem.at[t]) for t in range(len(ins))]
        outgoing, incoming = [], []
        for t in range(len(ins)):
            for k in range(3):
                to = (*chips[k], c)
                outgoing.append(_remote(ins[t].at[ids[k]], outs[t].at[me], send.at[t, k], recv.at[t, k], to))
                incoming.append(_remote(ins[t].at[ids[k]], outs[t].at[ids[k]], send.at[t, k], recv.at[t, k], to))
        return local, outgoing, incoming


class _Share(_Exchange):
    def __init__(self, grads):
        self.inputs = list(grads)
        self.out_shape = [jax.ShapeDtypeStruct(g.shape, g.dtype) for g in grads]
        self.aliases = {t: t for t in range(len(grads))}
        self.sems = [DMA_SEMS((len(grads),)), DMA_SEMS((len(grads),))]

    def _copies(self, ins, outs, sems):
        send, recv = sems
        (x, y, c), me, sibling, chips, ids = _place()
        outgoing, incoming = [], []
        for t, g in enumerate(self.inputs):
            mine = outs[t].at[_half(g.shape[0], c), :]
            theirs = outs[t].at[_half(g.shape[0], 1 - c), :]
            outgoing.append(_remote(mine, mine, send.at[t], recv.at[t], sibling))
            incoming.append(_remote(theirs, theirs, send.at[t], recv.at[t], sibling))
        return [], outgoing, incoming


class _GatherSmall(_Exchange):
    def __init__(self, block):
        self.inputs = [block]
        self.out_shape = [jax.ShapeDtypeStruct((N_DEV, *block.shape), block.dtype)]
        self.sems = [DMA_SEMS((7,)), DMA_SEMS((7,)), DMA_SEMS(())]

    def _copies(self, ins, outs, sems):
        send, recv, lsem = sems
        (x, y, c), me, sibling, chips, ids = _place()
        slot = lambda px, py, pc: outs[0].at[4 * px + 2 * py + pc]

        def copy(k, block, to, src=None):
            return _remote(slot(*block) if src is None else src, slot(*block), send.at[k], recv.at[k], to)

        local = [pltpu.make_async_copy(ins[0], slot(x, y, c), lsem)]
        first = [copy(0, (x, y, c), sibling, src=ins[0])] + [copy(1 + j, (x, y, c), (*chip, c), src=ins[0]) for j, chip in enumerate(chips)]
        passed = [copy(4 + j, (*chip, c), sibling) for j, chip in enumerate(chips)]
        landed = [copy(1 + j, (*chip, c), (x, y, c)) for j, chip in enumerate(chips)]
        from_sibling = [copy(0, (x, y, 1 - c), (x, y, c))] + [copy(4 + j, (*chip, 1 - c), (x, y, c)) for j, chip in enumerate(chips)]
        return local, first, (passed, landed, from_sibling)

    def finish(self, ins, outs, sems):
        local, first, (passed, landed, from_sibling) = self._copies(ins, outs, sems)
        for j in range(3):
            landed[j].wait_recv()
            passed[j].start()
        for cp in from_sibling:
            cp.wait_recv()
        for cp in first + passed:
            cp.wait_send()
        for cp in local:
            cp.wait()


def _split_refs(refs, counts):
    out = []
    for n in counts:
        out.append(refs[:n])
        refs = refs[n:]
    return out


def _each_exchange(exchanges, method, x_in, x_out, x_sem):
    for ex in exchanges:
        ni, no, ns = len(ex.inputs), len(ex.out_shape), len(ex.sems)
        getattr(ex, method)(x_in[:ni], x_out[:no], x_sem[:ns])
        x_in, x_out, x_sem = x_in[ni:], x_out[no:], x_sem[ns:]


def _call(body, *, name, grid, in_specs, out_specs, out_shape, operands, semantics, scratch_shapes=(), exchanges=()):
    x_in = [a for ex in exchanges for a in ex.inputs]
    x_out = [s for ex in exchanges for s in ex.out_shape]
    x_sem = [s for ex in exchanges for s in ex.sems]
    counts = (len(in_specs), len(x_in), len(out_specs), len(x_out), len(scratch_shapes), len(x_sem))
    aliases, i0, o0 = {}, len(in_specs), len(out_specs)
    for ex in exchanges:
        aliases.update({i0 + i: o0 + o for i, o in ex.aliases.items()})
        i0, o0 = i0 + len(ex.inputs), o0 + len(ex.out_shape)

    def full_body(*refs):
        ins, xi, outs, xo, scr, xs = _split_refs(list(refs), counts)
        if exchanges:
            @pl.when(functools.reduce(jnp.logical_and, [pl.program_id(a) == 0 for a in range(len(grid))]))
            def _():
                _each_exchange(exchanges, "start", xi, xo, xs)

        body(*ins, *outs, *scr)
        if exchanges:
            @pl.when(functools.reduce(jnp.logical_and, [pl.program_id(a) == grid[a] - 1 for a in range(len(grid))]))
            def _():
                _each_exchange(exchanges, "finish", xi, xo, xs)

    return pl.pallas_call(
        full_body, name=name, grid=grid,
        in_specs=list(in_specs) + [HBM] * len(x_in), out_specs=list(out_specs) + [HBM] * len(x_out),
        out_shape=list(out_shape) + x_out, scratch_shapes=list(scratch_shapes) + x_sem,
        input_output_aliases=aliases, compiler_params=_params(semantics),
    )(*operands, *x_in)


def _run_exchanges(exchanges, name):
    x_in = [a for ex in exchanges for a in ex.inputs]
    x_out = [s for ex in exchanges for s in ex.out_shape]
    x_sem = [s for ex in exchanges for s in ex.sems]
    aliases, i0, o0 = {}, 0, 0
    for ex in exchanges:
        aliases.update({i0 + i: o0 + o for i, o in ex.aliases.items()})
        i0, o0 = i0 + len(ex.inputs), o0 + len(ex.out_shape)

    def body(*refs):
        xi, xo, xs = _split_refs(list(refs), (len(x_in), len(x_out), len(x_sem)))
        _each_exchange(exchanges, "start", xi, xo, xs)
        _each_exchange(exchanges, "finish", xi, xo, xs)

    return pl.pallas_call(
        body, name=name, in_specs=[HBM] * len(x_in), out_specs=[HBM] * len(x_out), out_shape=x_out,
        scratch_shapes=x_sem, input_output_aliases=aliases,
    )(*x_in)


def _inproj_fwd(x, g1, w_in_all, exchanges=()):
    L, D = x.shape
    ns, _, nc = w_in_all.shape
    tm = _tile(L, TM_PROJ)

    def body(x_ref, g_ref, w_ref, hn_ref, proj_ref, u_ref):
        hn = _rms_fwd(x_ref[...], g_ref[...]).astype(MXU_DTYPE)
        hn_ref[...] = hn
        for j in range(ns):
            proj_ref[:, j * nc:(j + 1) * nc] = _dot(hn, w_ref[j])
        _store_slabs(u_ref, proj_ref[:, 0:nc])

    return _call(
        body, name="inproj_fwd", grid=(L // tm,), exchanges=exchanges, semantics=("arbitrary",), operands=(x, g1, w_in_all),
        in_specs=[pl.BlockSpec((tm, D), lambda i: (i, 0)), _resident((1, D)), _resident(w_in_all.shape)],
        out_specs=[pl.BlockSpec((tm, D), lambda i: (i, 0)), pl.BlockSpec((tm, ns * nc), lambda i: (i, 0)), _slab_spec(nc, tm)],
        out_shape=[jax.ShapeDtypeStruct((L, D), MXU_DTYPE), jax.ShapeDtypeStruct((L, ns * nc), F32), _slab_shape(L, nc)],
    )


def _slab_shape(L, n):
    return jax.ShapeDtypeStruct((n // LANES, L, LANES), F32)


def _slab_spec(n, tm, index=lambda i: (0, i, 0)):
    return pl.BlockSpec((n // LANES, tm, LANES), index)


def _store_slabs(ref, value):
    for k in range(ref.shape[0]):
        ref[k] = value[:, k * LANES:(k + 1) * LANES]


def _load_slabs(ref):
    return jnp.concatenate([ref[k] for k in range(ref.shape[0])], axis=1)


SEG_ROWS = SUBLANES * SUBLANES


def _load_permuted(ref):
    tm = ref.shape[1]
    slabs = []
    for k in range(ref.shape[0]):
        tiles = [ref.at[k][pl.ds(b * SEG_ROWS + j, SUBLANES, stride=SUBLANES), :] for b in range(tm // SEG_ROWS) for j in range(SUBLANES)]
        slabs.append(jnp.concatenate(tiles, axis=0))
    return jnp.concatenate(slabs, axis=1)


def _store_permuted(ref, value):
    tm = ref.shape[1]
    for k in range(ref.shape[0]):
        for b in range(tm // SEG_ROWS):
            for j in range(SUBLANES):
                r = b * SEG_ROWS + j * SUBLANES
                ref.at[k][pl.ds(b * SEG_ROWS + j, SUBLANES, stride=SUBLANES), :] = value[r:r + SUBLANES, k * LANES:(k + 1) * LANES]


def _scan_tile(xr, xi, hr, hi, coef_ref, lanes, reverse):
    for k, j in ((1, 0), (2, 2), (4, 4)):
        ar = coef_ref[j, :, lanes]
        ai = coef_ref[j + 1, :, lanes]
        shift = SUBLANES - k if reverse else k
        sr = pltpu.roll(xr, shift, 0)
        si = pltpu.roll(xi, shift, 0)
        xr, xi = xr + (ar * sr - ai * si), xi + (ar * si + ai * sr)
    pr = coef_ref[6, :, lanes]
    pi = coef_ref[7, :, lanes]
    return xr + (pr * hr - pi * hi), xi + (pr * hi + pi * hr)


def _scan_block(read, write, hr, hi, coef_ref, lanes, reverse):
    order = list(range(SUBLANES - 1, -1, -1) if reverse else range(SUBLANES))
    near = 8 + 2 * order[0]
    ar = coef_ref[near, :, lanes]
    ai = coef_ref[near + 1, :, lanes]
    xr, xi = read(order[0])
    local = {order[0]: (xr, xi)}
    for j in order[1:]:
        br, bi = read(j)
        xr, xi = br + (ar * xr - ai * xi), bi + (ar * xi + ai * xr)
        local[j] = (xr, xi)
    er, ei = _scan_tile(xr, xi, hr, hi, coef_ref, lanes, reverse)
    edge = lax.broadcasted_iota(jnp.int32, er.shape, 0) == (SUBLANES - 1 if reverse else 0)
    shift = SUBLANES - 1 if reverse else 1
    pr = jnp.where(edge, hr, pltpu.roll(er, shift, 0))
    pi = jnp.where(edge, hi, pltpu.roll(ei, shift, 0))
    for j in range(SUBLANES):
        cr = coef_ref[8 + 2 * j, :, lanes]
        ci = coef_ref[9 + 2 * j, :, lanes]
        xr, xi = local[j]
        write(j, xr + (cr * pr - ci * pi), xi + (cr * pi + ci * pr))
    end = 0 if reverse else SUBLANES - 1
    return jnp.broadcast_to(er[end:end + 1, :], er.shape), jnp.broadcast_to(ei[end:end + 1, :], ei.shape)


def _s5_fwd(u4, bmat, cmat, coef, dskip, exchanges=()):
    L = u4.shape[1]
    tm = _tile(L, TM_S5)
    lc = min(LANE_CHUNK, WB)

    def body(u_ref, bm_ref, cm_ref, coef_ref, d_ref, sre_ref, sim_ref, ys_ref, hr_ref, hi_ref):
        @pl.when(pl.program_id(1) == 0)
        def _():
            hr_ref[...] = jnp.zeros_like(hr_ref)
            hi_ref[...] = jnp.zeros_like(hi_ref)

        u = _load_permuted(u_ref)
        bu = _dot(u.astype(MXU_DTYPE), bm_ref[0])
        sre_ref[...] = bu[:, :WB]
        sim_ref[...] = bu[:, WB:]
        for c in range(WB // lc):
            lanes = slice(c * lc, (c + 1) * lc)
            hr, hi = hr_ref[:, lanes], hi_ref[:, lanes]
            for b in range(tm // SEG_ROWS):
                rows = lambda j, b=b: slice(b * SEG_ROWS + j * SUBLANES, b * SEG_ROWS + (j + 1) * SUBLANES)

                def read(j, rows=rows, lanes=lanes):
                    return sre_ref[rows(j), lanes], sim_ref[rows(j), lanes]

                def write(j, xr, xi, rows=rows, lanes=lanes):
                    sre_ref[rows(j), lanes] = xr
                    sim_ref[rows(j), lanes] = xi

                hr, hi = _scan_block(read, write, hr, hi, coef_ref, lanes, False)
            hr_ref[:, lanes] = hr
            hi_ref[:, lanes] = hi
        ys = _dot_nt(sre_ref[...].astype(MXU_DTYPE), cm_ref[0, :, :WB]) + _dot_nt(sim_ref[...].astype(MXU_DTYPE), cm_ref[0, :, WB:])
        _store_permuted(ys_ref, ys + d_ref[0] * u)

    return _call(
        body, name="s5_fwd", grid=(N_GBLK, L // tm), exchanges=exchanges, semantics=("arbitrary", "arbitrary"),
        operands=(u4, bmat, cmat, coef, dskip),
        in_specs=[
            _slab_spec(UB, tm, lambda b, i: (b, i, 0)),
            pl.BlockSpec((1, UB, 2 * WB), lambda b, i: (b, 0, 0)),
            pl.BlockSpec((1, UB, 2 * WB), lambda b, i: (b, 0, 0)),
            pl.BlockSpec((N_TABLES, SUBLANES, WB), lambda b, i: (0, 0, b)),
            pl.BlockSpec((1, 1, UB), lambda b, i: (b, 0, 0)),
        ],
        out_specs=[
            pl.BlockSpec((tm, WB), lambda b, i: (i, b)),
            pl.BlockSpec((tm, WB), lambda b, i: (i, b)),
            _slab_spec(UB, tm, lambda b, i: (b, i, 0)),
        ],
        out_shape=[
            jax.ShapeDtypeStruct((L, N_GBLK * WB), F32),
            jax.ShapeDtypeStruct((L, N_GBLK * WB), F32),
            _slab_shape(L, D_SSM),
        ],
        scratch_shapes=[pltpu.VMEM((SUBLANES, WB), F32), pltpu.VMEM((SUBLANES, WB), F32)],
    )


def _tail_fwd(x, ys, proj, w_glu, conv_w, g_ssm, g_conv, w_out, g_post, exchanges=()):
    L, D = x.shape
    tm = _tile(L, TM_TAIL)

    def body(x_ref, ys_ref, h_ref, bg_ref, cg_ref, wglu_ref, cw_ref, gs_ref, gc_ref, wout_ref, gp_ref,
             ycat_ref, o_ref, x1_ref, zbuf):
        @pl.when(pl.program_id(0) == 0)
        def _():
            zbuf[0:SUBLANES, :] = jnp.zeros((SUBLANES, D_CONV), F32)

        y1, _ = _gelu(_load_slabs(ys_ref))
        y2 = y1 * jax.nn.sigmoid(_dot(y1.astype(MXU_DTYPE), wglu_ref[...]))
        ycat_ref[:, :D_SSM] = _rms_fwd(y2, gs_ref[...]).astype(MXU_DTYPE)
        z = cg_ref[...] * h_ref[...]
        zbuf[SUBLANES:, :] = z
        conv = cw_ref[0:1, :] * zbuf[SUBLANES - 2:SUBLANES - 2 + tm, :] + cw_ref[1:2, :] * zbuf[SUBLANES - 1:SUBLANES - 1 + tm, :] + cw_ref[2:3, :] * z
        zbuf[0:SUBLANES, :] = zbuf[tm:tm + SUBLANES, :]
        ycat_ref[:, D_SSM:] = _rms_fwd(bg_ref[...] * conv, gc_ref[...]).astype(MXU_DTYPE)
        o = _dot(ycat_ref[...], wout_ref[...])
        o_ref[...] = o
        x1_ref[...] = x_ref[...] + _rms_fwd(o, gp_ref[...])

    row = lambda i: (i, 0)
    return _call(
        body, name="tail_fwd", grid=(L // tm,), exchanges=exchanges, semantics=("arbitrary",),
        operands=(x, ys, proj, proj, proj, w_glu, conv_w, g_ssm, g_conv, w_out, g_post),
        in_specs=[
            pl.BlockSpec((tm, D), row), _slab_spec(D_SSM, tm),
            pl.BlockSpec((tm, D_CONV), lambda i: (i, 1)), pl.BlockSpec((tm, D_CONV), lambda i: (i, 2)),
            pl.BlockSpec((tm, D_CONV), lambda i: (i, 3)),
            _resident(w_glu.shape), _resident(conv_w.shape), _resident(g_ssm.shape), _resident(g_conv.shape),
            _resident(w_out.shape), _resident(g_post.shape),
        ],
        out_specs=[pl.BlockSpec((tm, D), row), pl.BlockSpec((tm, D), row), pl.BlockSpec((tm, D), row)],
        out_shape=[jax.ShapeDtypeStruct((L, D), MXU_DTYPE), jax.ShapeDtypeStruct((L, D), F32), jax.ShapeDtypeStruct((L, D), F32)],
        scratch_shapes=[pltpu.VMEM((tm + SUBLANES, D_CONV), F32)],
    )


def _mlp_fwd(x1, target, w_up_all, w_down_a, w_down_b, g_pre, g_post):
    L, D = x1.shape
    ns, _, fc = w_up_all.shape
    half = w_down_a.shape[1]
    tm = _tile(L, TM_MLP)

    def body(x1_ref, t_ref, wup_ref, wda_ref, wdb_ref, gpre_ref, gpost_ref, hn2_ref, up_ref, m_ref, dx2_ref, loss_ref):
        @pl.when(pl.program_id(0) == 0)
        def _():
            loss_ref[...] = jnp.zeros_like(loss_ref)

        x1v = x1_ref[...]
        hn2 = _rms_fwd(x1v, gpre_ref[...]).astype(MXU_DTYPE)
        hn2_ref[...] = hn2
        m = jnp.zeros((tm, D), F32)
        for j in range(ns):
            up = _dot(hn2, wup_ref[j])
            up_ref[:, j * fc:(j + 1) * fc] = up
            act = jnp.square(jnp.maximum(up, 0.0)).astype(MXU_DTYPE)
            m = m + _dot(act[:, :half], wda_ref[j]) + _dot(act[:, half:], wdb_ref[j])
        m_ref[...] = m
        err = x1v + _rms_fwd(m, gpost_ref[...]) - t_ref[...]
        loss_ref[...] += 0.5 * jnp.sum(jnp.mean(err * err, axis=-1, keepdims=True))
        dx2_ref[...] = err * (1.0 / D)

    row = lambda i: (i, 0)
    return pl.pallas_call(
        body, name="mlp_fwd", grid=(L // tm,),
        in_specs=[pl.BlockSpec((tm, D), row), pl.BlockSpec((tm, D), row), _resident(w_up_all.shape), _resident(w_down_a.shape),
                  _resident(w_down_b.shape), _resident(g_pre.shape), _resident(g_post.shape)],
        out_specs=[pl.BlockSpec((tm, D), row), pl.BlockSpec((tm, ns * fc), row), pl.BlockSpec((tm, D), row),
                   pl.BlockSpec((tm, D), row), pl.BlockSpec((SUBLANES, LANES), lambda i: (0, 0))],
        out_shape=[jax.ShapeDtypeStruct((L, D), MXU_DTYPE), jax.ShapeDtypeStruct((L, ns * fc), F32), jax.ShapeDtypeStruct((L, D), F32),
                   jax.ShapeDtypeStruct((L, D), F32), jax.ShapeDtypeStruct((SUBLANES, LANES), F32)],
        compiler_params=_params(("arbitrary",)),
    )(x1, target, w_up_all, w_down_a, w_down_b, g_pre, g_post)


def _mlp_bwd(dx2, m, up, x1, w_up_all, w_down_a, w_down_b, g_pre, g_post):
    L, D = x1.shape
    ns, _, fc = w_up_all.shape
    tm = _tile(L, TM_MLP)

    def body(dx2_ref, m_ref, up_ref, x1_ref, wup_ref, wda_ref, wdb_ref, gpre_ref, gpost_ref,
             dm_ref, dup_ref, act_ref, dx1_ref, dgpost_ref, dgpre_ref):
        @pl.when(pl.program_id(0) == 0)
        def _():
            dgpost_ref[...] = jnp.zeros_like(dgpost_ref)
            dgpre_ref[...] = jnp.zeros_like(dgpre_ref)

        dx2v = dx2_ref[...]
        dm, dg = _rms_bwd(m_ref[...], gpost_ref[...], dx2v)
        dgpost_ref[...] += dg
        dm_b = dm.astype(MXU_DTYPE)
        dm_ref[...] = dm_b
        dhn2 = jnp.zeros((tm, D), F32)
        for j in range(ns):
            cols = slice(j * fc, (j + 1) * fc)
            relu = jnp.maximum(up_ref[:, cols], 0.0)
            act_ref[:, cols] = jnp.square(relu).astype(MXU_DTYPE)
            dact = jnp.concatenate([_dot_nt(dm_b, wda_ref[j]), _dot_nt(dm_b, wdb_ref[j])], axis=1)
            dup = (dact * (2.0 * relu)).astype(MXU_DTYPE)
            dup_ref[:, cols] = dup
            dhn2 = dhn2 + _dot_nt(dup, wup_ref[j])
        dx, dg = _rms_bwd(x1_ref[...], gpre_ref[...], dhn2)
        dgpre_ref[...] += dg
        dx1_ref[...] = dx2v + dx

    row = lambda i: (i, 0)
    vec = pl.BlockSpec((1, D), lambda i: (0, 0))
    return pl.pallas_call(
        body, name="mlp_bwd", grid=(L // tm,),
        in_specs=[pl.BlockSpec((tm, D), row), pl.BlockSpec((tm, D), row), pl.BlockSpec((tm, ns * fc), row), pl.BlockSpec((tm, D), row),
                  _resident(w_up_all.shape), _resident(w_down_a.shape), _resident(w_down_b.shape), _resident(g_pre.shape), _resident(g_post.shape)],
        out_specs=[pl.BlockSpec((tm, D), row), pl.BlockSpec((tm, ns * fc), row), pl.BlockSpec((tm, ns * fc), row),
                   pl.BlockSpec((tm, D), row), vec, vec],
        out_shape=[jax.ShapeDtypeStruct((L, D), MXU_DTYPE), jax.ShapeDtypeStruct((L, ns * fc), MXU_DTYPE),
                   jax.ShapeDtypeStruct((L, ns * fc), MXU_DTYPE), jax.ShapeDtypeStruct((L, D), F32),
                   jax.ShapeDtypeStruct((1, D), F32), jax.ShapeDtypeStruct((1, D), F32)],
        compiler_params=_params(("arbitrary",)),
    )(dx2, m, up, x1, w_up_all, w_down_a, w_down_b, g_pre, g_post)


def _tail_bwd(dx1, o, ys, proj, w_glu, conv_w, g_ssm, g_conv, w_out, g_post, exchanges=()):
    L, D = dx1.shape
    tm = _tile(L, TM_TAIL)
    nt = L // tm
    hb = tm // SUBLANES

    def body(dx1_ref, o_ref, ys_ref, h_ref, bg_ref, cg_ref, hh_ref, hcg_ref, wglu_ref, cw_ref, gs_ref, gc_ref, wout_ref, gp_ref,
             do_ref, da_ref, y1_ref, dys_ref, dhbc_ref, dgp_ref, dgs_ref, dgc_ref, dcw_ref, zbuf, dcbuf):
        step = pl.program_id(0)

        @pl.when(step == 0)
        def _():
            dcbuf[tm:, :] = jnp.zeros((SUBLANES, D_CONV), F32)
            dgp_ref[...] = jnp.zeros_like(dgp_ref)
            dgs_ref[...] = jnp.zeros_like(dgs_ref)
            dgc_ref[...] = jnp.zeros_like(dgc_ref)
            dcw_ref[...] = jnp.zeros_like(dcw_ref)

        do, dg = _rms_bwd(o_ref[...], gp_ref[...], dx1_ref[...])
        dgp_ref[...] += dg
        do_b = do.astype(MXU_DTYPE)
        do_ref[...] = do_b
        dycat = _dot_nt(do_b, wout_ref[...])
        y1, dgelu = _gelu(_load_slabs(ys_ref))
        y1_b = y1.astype(MXU_DTYPE)
        y1_ref[...] = y1_b
        s = jax.nn.sigmoid(_dot(y1_b, wglu_ref[...]))
        dy2, dg = _rms_bwd(y1 * s, gs_ref[...], dycat[:, :D_SSM])
        dgs_ref[...] += dg
        da_b = (dy2 * y1 * s * (1.0 - s)).astype(MXU_DTYPE)
        da_ref[...] = da_b
        _store_slabs(dys_ref, (dy2 * s + _dot_nt(da_b, wglu_ref[...])) * dgelu)
        h = h_ref[...]
        cg = cg_ref[...]
        bg = bg_ref[...]
        z = cg * h
        first = step == nt - 1
        zbuf[0:SUBLANES, :] = jnp.where(first, 0.0, hcg_ref[...] * hh_ref[...])
        zbuf[SUBLANES:, :] = z
        z1 = zbuf[SUBLANES - 1:SUBLANES - 1 + tm, :]
        z2 = zbuf[SUBLANES - 2:SUBLANES - 2 + tm, :]
        conv = cw_ref[0:1, :] * z2 + cw_ref[1:2, :] * z1 + cw_ref[2:3, :] * z
        dyc, dg = _rms_bwd(bg * conv, gc_ref[...], dycat[:, D_SSM:])
        dgc_ref[...] += dg
        dconv = dyc * bg
        dcw_ref[0:1, :] += jnp.sum(dconv * z2, axis=0, keepdims=True)
        dcw_ref[1:2, :] += jnp.sum(dconv * z1, axis=0, keepdims=True)
        dcw_ref[2:3, :] += jnp.sum(dconv * z, axis=0, keepdims=True)
        dcbuf[0:tm, :] = dconv
        dz = cw_ref[2:3, :] * dconv + cw_ref[1:2, :] * dcbuf[1:1 + tm, :] + cw_ref[0:1, :] * dcbuf[2:2 + tm, :]
        dcbuf[tm:, :] = dcbuf[0:SUBLANES, :]
        dhbc_ref[:, 0:D_CONV] = (dz * cg).astype(MXU_DTYPE)
        dhbc_ref[:, D_CONV:2 * D_CONV] = (dyc * conv).astype(MXU_DTYPE)
        dhbc_ref[:, 2 * D_CONV:] = (dz * h).astype(MXU_DTYPE)

    rev = lambda i: (nt - 1 - i, 0)
    slab = lambda i: (0, nt - 1 - i, 0)
    col = lambda c: (lambda i: (nt - 1 - i, c))
    halo = lambda c: (lambda i: (jnp.maximum((nt - 1 - i) * hb - 1, 0), c))
    vec = lambda n: pl.BlockSpec((1, n), lambda i: (0, 0))
    return _call(
        body, name="tail_bwd", grid=(nt,), exchanges=exchanges, semantics=("arbitrary",),
        operands=(dx1, o, ys, proj, proj, proj, proj, proj, w_glu, conv_w, g_ssm, g_conv, w_out, g_post),
        in_specs=[
            pl.BlockSpec((tm, D), rev), pl.BlockSpec((tm, D), rev), _slab_spec(D_SSM, tm, slab),
            pl.BlockSpec((tm, D_CONV), col(1)), pl.BlockSpec((tm, D_CONV), col(2)), pl.BlockSpec((tm, D_CONV), col(3)),
            pl.BlockSpec((SUBLANES, D_CONV), halo(1)), pl.BlockSpec((SUBLANES, D_CONV), halo(3)),
            _resident(w_glu.shape), _resident(conv_w.shape), _resident(g_ssm.shape), _resident(g_conv.shape),
            _resident(w_out.shape), _resident(g_post.shape),
        ],
        out_specs=[
            pl.BlockSpec((tm, D), rev), pl.BlockSpec((tm, D_SSM), rev), pl.BlockSpec((tm, D_SSM), rev), _slab_spec(D_SSM, tm, slab),
            pl.BlockSpec((tm, 3 * D_CONV), rev), vec(D), vec(D_SSM), vec(D_CONV),
            pl.BlockSpec((SUBLANES, D_CONV), lambda i: (0, 0)),
        ],
        out_shape=[
            jax.ShapeDtypeStruct((L, D), MXU_DTYPE), jax.ShapeDtypeStruct((L, D_SSM), MXU_DTYPE), jax.ShapeDtypeStruct((L, D_SSM), MXU_DTYPE),
            _slab_shape(L, D_SSM), jax.ShapeDtypeStruct((L, 3 * D_CONV), MXU_DTYPE),
            jax.ShapeDtypeStruct((1, D), F32), jax.ShapeDtypeStruct((1, D_SSM), F32), jax.ShapeDtypeStruct((1, D_CONV), F32),
            jax.ShapeDtypeStruct((SUBLANES, D_CONV), F32),
        ],
        scratch_shapes=[pltpu.VMEM((tm + SUBLANES, D_CONV), F32), pltpu.VMEM((tm + SUBLANES, D_CONV), F32)],
    )


def _s5_bwd(dys, u4, s_re, s_im, bmat, cmat, coef_rev, dskip, exchanges=()):
    L = dys.shape[1]
    tm = _tile(L, TM_S5)
    nt = L // tm
    lc = min(LANE_CHUNK, WB)

    def body(dys_ref, u_ref, sre_ref, sim_ref, bm_ref, cm_ref, coef_ref, d_ref,
             du_ref, gb_ref, gc_ref, q_ref, gd_ref, dr_ref, di_ref, lr_ref, li_ref, hr_ref, hi_ref, qr_acc, qi_acc, gb_acc, gc_acc):
        step = pl.program_id(1)

        @pl.when(step == 0)
        def _():
            for ref in (hr_ref, hi_ref, qr_acc, qi_acc, gb_acc, gc_acc, gd_ref):
                ref[...] = jnp.zeros_like(ref)

        dys_v = _load_permuted(dys_ref)
        u = _load_permuted(u_ref)
        dys_b = dys_v.astype(MXU_DTYPE)
        u_b = u.astype(MXU_DTYPE)
        d = _dot(dys_b, cm_ref[0])
        dr_ref[...] = d[:, :WB]
        di_ref[...] = d[:, WB:]
        for c in range(WB // lc):
            lanes = slice(c * lc, (c + 1) * lc)
            hr, hi = hr_ref[:, lanes], hi_ref[:, lanes]
            q = [qr_acc[:, lanes], qi_acc[:, lanes]]
            for b in range(tm // SEG_ROWS - 1, -1, -1):
                rows = lambda j, b=b: slice(b * SEG_ROWS + j * SUBLANES, b * SEG_ROWS + (j + 1) * SUBLANES)

                def read(j, rows=rows, lanes=lanes):
                    return dr_ref[rows(j), lanes], di_ref[rows(j), lanes]

                def write(j, xr, xi, rows=rows, lanes=lanes, q=q):
                    lr_ref[rows(j), lanes] = xr
                    li_ref[rows(j), lanes] = xi
                    er = xr - dr_ref[rows(j), lanes]
                    ei = xi - di_ref[rows(j), lanes]
                    sr = sre_ref[rows(j), lanes]
                    si = sim_ref[rows(j), lanes]
                    q[0] = q[0] + (er * sr + ei * si)
                    q[1] = q[1] + (ei * sr - er * si)

                hr, hi = _scan_block(read, write, hr, hi, coef_ref, lanes, True)
            hr_ref[:, lanes] = hr
            hi_ref[:, lanes] = hi
            qr_acc[:, lanes] = q[0]
            qi_acc[:, lanes] = q[1]
        lr_b = lr_ref[...].astype(MXU_DTYPE)
        li_b = li_ref[...].astype(MXU_DTYPE)
        _store_permuted(du_ref, _dot_nt(lr_b, bm_ref[0, :, :WB]) + _dot_nt(li_b, bm_ref[0, :, WB:]) + d_ref[0] * dys_v)
        gb_acc[:, :WB] += _dot_tn(u_b, lr_b)
        gb_acc[:, WB:] += _dot_tn(u_b, li_b)
        gc_acc[:, :WB] += _dot_tn(dys_b, sre_ref[...].astype(MXU_DTYPE))
        gc_acc[:, WB:] += _dot_tn(dys_b, sim_ref[...].astype(MXU_DTYPE))
        gd_ref[0] += jnp.sum(dys_v * u, axis=0, keepdims=True)

        @pl.when(step == nt - 1)
        def _():
            q_ref[0, 0:1, :] = jnp.sum(qr_acc[...], axis=0, keepdims=True)
            q_ref[0, 1:2, :] = jnp.sum(qi_acc[...], axis=0, keepdims=True)
            mask = _group_mask(UB, WB)
            fold = (lax.broadcasted_iota(jnp.int32, (WB, STATE), 0) % STATE == lax.broadcasted_iota(jnp.int32, (WB, STATE), 1)).astype(F32)
            for acc, out in ((gb_acc, gb_ref), (gc_acc, gc_ref)):
                for k in range(2):
                    own = jnp.where(mask, acc[:, k * WB:(k + 1) * WB], 0.0)
                    out[0, k] = jnp.dot(own, fold, precision=lax.Precision.HIGHEST, preferred_element_type=F32)

    rev = lambda b, i: (nt - 1 - i, b)
    slab = lambda b, i: (b, nt - 1 - i, 0)
    blk = lambda b, i: (b, 0, 0)
    blk4 = lambda b, i: (b, 0, 0, 0)
    return _call(
        body, name="s5_bwd", grid=(N_GBLK, nt), exchanges=exchanges, semantics=("arbitrary", "arbitrary"),
        operands=(dys, u4, s_re, s_im, bmat, cmat, coef_rev, dskip),
        in_specs=[
            _slab_spec(UB, tm, slab), _slab_spec(UB, tm, slab), pl.BlockSpec((tm, WB), rev), pl.BlockSpec((tm, WB), rev),
            pl.BlockSpec((1, UB, 2 * WB), blk), pl.BlockSpec((1, UB, 2 * WB), blk),
            pl.BlockSpec((N_TABLES, SUBLANES, WB), lambda b, i: (0, 0, b)), pl.BlockSpec((1, 1, UB), blk),
        ],
        out_specs=[
            _slab_spec(UB, tm, slab), pl.BlockSpec((1, 2, UB, STATE), blk4), pl.BlockSpec((1, 2, UB, STATE), blk4),
            pl.BlockSpec((1, 2, WB), blk), pl.BlockSpec((1, 1, UB), blk),
        ],
        out_shape=[
            _slab_shape(L, D_SSM), jax.ShapeDtypeStruct((N_GBLK, 2, UB, STATE), F32),
            jax.ShapeDtypeStruct((N_GBLK, 2, UB, STATE), F32), jax.ShapeDtypeStruct((N_GBLK, 2, WB), F32),
            jax.ShapeDtypeStruct((N_GBLK, 1, UB), F32),
        ],
        scratch_shapes=[pltpu.VMEM((tm, WB), F32)] * 4 + [pltpu.VMEM((SUBLANES, WB), F32)] * 4 + [pltpu.VMEM((UB, 2 * WB), F32)] * 2,
    )


def _inproj_bwd(du, dhbc, x, dx1, w_in_all, g1):
    L, D = x.shape
    ns, _, nc = w_in_all.shape
    tm = _tile(L, TM_PROJ)

    def body(du_ref, dhbc_ref, x_ref, dx1_ref, w_ref, g_ref, gx_ref, dproj_ref, dg_ref):
        @pl.when(pl.program_id(0) == 0)
        def _():
            dg_ref[...] = jnp.zeros_like(dg_ref)

        du_b = _load_slabs(du_ref).astype(MXU_DTYPE)
        dproj_ref[:, :nc] = du_b
        dproj_ref[:, nc:] = dhbc_ref[...]
        dhn = _dot_nt(du_b, w_ref[0])
        for j in range(1, ns):
            dhn = dhn + _dot_nt(dhbc_ref[:, (j - 1) * nc:j * nc], w_ref[j])
        dx, dg = _rms_bwd(x_ref[...], g_ref[...], dhn)
        dg_ref[...] += dg
        gx_ref[...] = dx1_ref[...] + dx

    row = lambda i: (i, 0)
    return pl.pallas_call(
        body, name="inproj_bwd", grid=(L // tm,),
        in_specs=[_slab_spec(nc, tm), pl.BlockSpec((tm, (ns - 1) * nc), row), pl.BlockSpec((tm, D), row), pl.BlockSpec((tm, D), row),
                  _resident(w_in_all.shape), _resident(g1.shape)],
        out_specs=[pl.BlockSpec((tm, D), row), pl.BlockSpec((tm, ns * nc), row), pl.BlockSpec((1, D), lambda i: (0, 0))],
        out_shape=[jax.ShapeDtypeStruct((L, D), F32), jax.ShapeDtypeStruct((L, ns * nc), MXU_DTYPE), jax.ShapeDtypeStruct((1, D), F32)],
        compiler_params=_params(("arbitrary",)),
    )(du, dhbc, x, dx1, w_in_all, g1)


def _matmul_tn(a, b, name, col_shards=1, exchanges=()):
    L, K = a.shape
    N = b.shape[1]
    tl = _tile(L, TL_TN)
    tk = _tile(K, TK_TN)
    nw = N // col_shards
    spb = max(1, min(col_shards, TN_TN // nw))
    tn = spb * nw if spb > 1 else _tile(nw, TK_TN)
    npb = nw // tn if spb == 1 else 1

    def body(a_ref, b_ref, o_ref):
        @pl.when(pl.program_id(2) == 0)
        def _():
            o_ref[...] = jnp.zeros_like(o_ref)

        res = _dot_tn(a_ref[...], b_ref[...])
        for s in range(spb):
            o_ref[s] += res[:, s * nw:(s + 1) * nw] if spb > 1 else res

    out_block = pl.BlockSpec((spb, tk, nw if spb > 1 else tn), (lambda k, n, l: (n, k, 0)) if spb > 1 else (lambda k, n, l: (n // npb, k, n % npb)))
    return _call(
        body, name=name, grid=(K // tk, N // tn, L // tl), exchanges=exchanges, semantics=("arbitrary", "arbitrary", "arbitrary"),
        operands=(a, b),
        in_specs=[pl.BlockSpec((tl, tk), lambda k, n, l: (l, k)), pl.BlockSpec((tl, tn), lambda k, n, l: (l, n))],
        out_specs=[out_block],
        out_shape=[jax.ShapeDtypeStruct((col_shards, K, nw), F32)],
    )


def _ssm_discretize(lam_re, lam_im, log_dt, bt_re, bt_im):
    dt = jnp.exp(log_dt)[:, None]
    zr = lam_re * dt
    zi = lam_im * dt
    mag = jnp.exp(zr)
    abr = mag * jnp.cos(zi)
    abi = mag * jnp.sin(zi)
    nr, ni = abr - 1.0, abi
    den = lam_re * lam_re + lam_im * lam_im
    coef_r = ((nr * lam_re + ni * lam_im) / den)[:, None, :]
    coef_i = ((ni * lam_re - nr * lam_im) / den)[:, None, :]
    return zr, zi, coef_r * bt_re - coef_i * bt_im, coef_r * bt_im + coef_i * bt_re


def _scan_tables(ar, ai, reverse):
    rows = np.arange(SUBLANES)
    exps = np.zeros((N_TABLES // 2, SUBLANES), np.int32)
    keep = np.ones((N_TABLES // 2, SUBLANES), bool)
    for t, k in enumerate((1, 2, 4)):
        exps[t] = SUBLANES * k
        keep[t] = (rows + k <= SUBLANES - 1) if reverse else (rows >= k)
    exps[3] = SUBLANES * (SUBLANES - rows) if reverse else SUBLANES * (rows + 1)
    for j in range(SUBLANES):
        exps[4 + j] = SUBLANES - j if reverse else j + 1
    pr, pi = ar, (-ai if reverse else ai)
    shape = (N_TABLES // 2, SUBLANES, ar.shape[0])
    xr, xi = jnp.ones(shape, F32), jnp.zeros(shape, F32)
    for bit in range(int(exps.max()).bit_length()):
        on = ((exps >> bit) & 1).astype(bool)[:, :, None]
        xr, xi = jnp.where(on, xr * pr - xi * pi, xr), jnp.where(on, xr * pi + xi * pr, xi)
        pr, pi = pr * pr - pi * pi, 2.0 * pr * pi
    xr = jnp.where(keep[:, :, None], xr, 0.0)
    xi = jnp.where(keep[:, :, None], xi, 0.0)
    return jnp.stack([xr, xi], axis=1).reshape(N_TABLES, SUBLANES, ar.shape[0])


def _group_mask(rows, cols):
    r = lax.broadcasted_iota(jnp.int32, (rows, cols), 0) // GROUP
    c = lax.broadcasted_iota(jnp.int32, (rows, cols), 1) // STATE
    return r == c


def _ssm_expand(bt_re, bt_im, c_re, c_im, exchanges=()):
    flat = lambda a: a.reshape(N_GROUPS * GROUP, STATE)

    def body(br_ref, bi_ref, cr_ref, ci_ref, bm_ref, cm_ref):
        spread = (lax.broadcasted_iota(jnp.int32, (STATE, WB), 1) % STATE == lax.broadcasted_iota(jnp.int32, (STATE, WB), 0)).astype(F32)
        mask = _group_mask(UB, WB)

        def expand(x):
            wide = jnp.dot(x, spread, precision=lax.Precision.HIGHEST, preferred_element_type=F32)
            return jnp.where(mask, wide, 0.0).astype(MXU_DTYPE)

        bm_ref[0, :, :WB] = expand(br_ref[...])
        bm_ref[0, :, WB:] = expand(bi_ref[...])
        cm_ref[0, :, :WB] = expand(cr_ref[...])
        cm_ref[0, :, WB:] = expand(-ci_ref[...])

    spec = pl.BlockSpec((UB, STATE), lambda b: (b, 0))
    out = pl.BlockSpec((1, UB, 2 * WB), lambda b: (b, 0, 0))
    return _call(
        body, name="ssm_expand", grid=(N_GBLK,), exchanges=exchanges, semantics=("arbitrary",),
        operands=(flat(bt_re), flat(bt_im), flat(c_re), flat(c_im)), in_specs=[spec] * 4, out_specs=[out, out],
        out_shape=[jax.ShapeDtypeStruct((N_GBLK, UB, 2 * WB), MXU_DTYPE)] * 2,
    )


def _ssm_matrices(lam_re, lam_im, log_dt, bt_re, bt_im, c_re, c_im, exchanges=()):
    zr, zi, bbar_r, bbar_i = _ssm_discretize(lam_re, lam_im, log_dt, bt_re, bt_im)
    mag = jnp.exp(zr)
    ar = (mag * jnp.cos(zi)).reshape(-1)
    ai = (mag * jnp.sin(zi)).reshape(-1)
    bmat, cmat, *rest = _ssm_expand(bbar_r, bbar_i, c_re, c_im, exchanges)
    return bmat, cmat, _scan_tables(ar, ai, False), _scan_tables(ar, ai, True), rest


def _ssm_param_grads(lam_re, lam_im, log_dt, bt_re, bt_im, gb, gc, q, gd):
    part = lambda g, k: g[:, k].reshape(N_GROUPS, GROUP, STATE)
    qr = q[:, 0, :].reshape(N_GROUPS, STATE)
    qi = q[:, 1, :].reshape(N_GROUPS, STATE)
    _, vjp = jax.vjp(_ssm_discretize, lam_re, lam_im, log_dt, bt_re, bt_im)
    d_lam_re, d_lam_im, d_log_dt, d_bt_re, d_bt_im = vjp((qr, qi, part(gb, 0), part(gb, 1)))
    return d_lam_re, d_lam_im, d_log_dt, d_bt_re, d_bt_im, part(gc, 0), -part(gc, 1), gd.reshape(N_GROUPS, GROUP)


def _row_tile(rows, n):
    return _tile(rows, max(SUBLANES, (2 * 1024 * 1024) // (4 * n)))


def _pair_add(grad, other, core, name):
    ns, h, n = other.shape
    tr = _row_tile(h, n)
    nb = h // tr

    def body(c_ref, g_ref, o_ref, out_ref):
        out_ref[...] = (g_ref[...] + o_ref[...]).astype(WIRE_DTYPE)

    return pl.pallas_call(
        body, name=name,
        grid_spec=pltpu.PrefetchScalarGridSpec(
            num_scalar_prefetch=1, grid=(ns, nb),
            in_specs=[pl.BlockSpec((1, tr, n), lambda s, i, c: (s, c[0] * nb + i, 0)), pl.BlockSpec((1, tr, n), lambda s, i, c: (s, i, 0))],
            out_specs=pl.BlockSpec((1, tr, n), lambda s, i, c: (s, i, 0))),
        out_shape=jax.ShapeDtypeStruct(other.shape, WIRE_DTYPE),
        compiler_params=_params(("arbitrary", "arbitrary")),
    )(core, grad, other)


def _quad_sum(parts, core, name):
    ns, h, n = parts.shape
    tr = _row_tile(h, n)
    nb = h // tr

    def body(c_ref, p_ref, out_ref):
        p = [p_ref[k].astype(F32) for k in range(ns)]
        out_ref[...] = ((p[0] + p[1]) + p[2]) + p[3]

    return pl.pallas_call(
        body, name=name,
        grid_spec=pltpu.PrefetchScalarGridSpec(
            num_scalar_prefetch=1, grid=(nb,),
            in_specs=[pl.BlockSpec((ns, tr, n), lambda i, c: (0, i, 0))],
            out_specs=pl.BlockSpec((tr, n), lambda i, c: (c[0] * nb + i, 0))),
        out_shape=jax.ShapeDtypeStruct((2 * h, n), F32),
        compiler_params=_params(("arbitrary",)),
    )(core, parts)


def _adamw_math(w, g, m, v):
    m = ADAM_B1 * m + (1.0 - ADAM_B1) * g
    v = ADAM_B2 * v + (1.0 - ADAM_B2) * jnp.square(g)
    m_hat = m / (1.0 - ADAM_B1 ** ADAM_STEP)
    v_hat = v / (1.0 - ADAM_B2 ** ADAM_STEP)
    delta = -ADAM_LR * (m_hat / (jnp.sqrt(v_hat) + ADAM_EPS) + ADAM_WD * w)
    return delta, m, v


def _adamw(w, g, m, v, name):
    r, n = w.shape
    tr = _row_tile(r, n)

    def body(w_ref, g_ref, m_ref, v_ref, d_ref, nm_ref, nv_ref):
        d_ref[...], nm_ref[...], nv_ref[...] = _adamw_math(w_ref[...], g_ref[...], m_ref[...], v_ref[...])

    spec = pl.BlockSpec((tr, n), lambda i: (i, 0))
    return pl.pallas_call(
        body, name=name, grid=(r // tr,), in_specs=[spec] * 4, out_specs=[spec] * 3,
        out_shape=[jax.ShapeDtypeStruct((r, n), F32)] * 3,
        compiler_params=_params(("arbitrary",)),
    )(w, g, m, v)


LANES = 128
SMALL = ["g_pre_mix", "lam_re", "lam_im", "log_dt", "b_re", "b_im", "c_re", "c_im", "d_skip", "conv_w", "g_ssm_out", "g_conv_out",
         "g_post_mix", "g_pre_mlp", "g_post_mlp"]
TILE_SLOTS = {"b_re": (0, N_GROUPS), "b_im": (N_GROUPS, N_GROUPS), "c_re": (2 * N_GROUPS, N_GROUPS), "c_im": (3 * N_GROUPS, N_GROUPS),
              "lam_re": (4 * N_GROUPS, 2), "lam_im": (4 * N_GROUPS + 2, 2)}
N_TILE_SLOTS = 4 * N_GROUPS + 4
VEC_ROWS = {"g_pre_mix": 0, "g_post_mix": 1, "g_pre_mlp": 2, "g_post_mlp": 3, "g_ssm_out": 4, "g_conv_out": 5, "log_dt": 6}
ROW_LOSS, ROW_DSKIP, ROW_CONV, N_PACK_ROWS = 7, 8, 24, 32


def _kernel_form(name, a):
    if name in ("b_re", "b_im"):
        return jnp.transpose(a, (0, 1, 3, 2)).reshape(N_GROUPS, GROUP, STATE)
    if name in ("c_re", "c_im"):
        return a.reshape(N_GROUPS, GROUP, STATE)
    if name in ("lam_re", "lam_im"):
        return a.reshape(2, GROUP, STATE)
    if name == "d_skip":
        return jnp.transpose(a, (0, 2, 1)).reshape(GROUP, N_GROUPS)
    if name == "conv_w":
        return jnp.transpose(a, (1, 0, 2))
    return a


def _param_form(name, k):
    if name in ("b_re", "b_im"):
        return jnp.transpose(k.reshape(1, N_GROUPS, GROUP, STATE), (0, 1, 3, 2))
    if name in ("c_re", "c_im"):
        return k.reshape(1, N_GROUPS, GROUP, STATE)
    if name in ("lam_re", "lam_im"):
        return k.reshape(1, N_GROUPS, STATE)
    if name == "d_skip":
        return jnp.transpose(k.reshape(1, GROUP, N_GROUPS), (0, 2, 1))
    if name == "conv_w":
        return jnp.transpose(k, (1, 0, 2))
    return k


def _pack_tiles(g):
    lam = lambda a: a.reshape(2, GROUP, STATE)
    tiles = jnp.concatenate([g["b_re"], g["b_im"], g["c_re"], g["c_im"], lam(g["lam_re"]), lam(g["lam_im"])], axis=0)
    return tiles.astype(WIRE_DTYPE)


def _pack_rows(g, loss):
    row = lambda a: jnp.pad(a, ((0, 0), (0, D_MODEL - a.shape[1])))
    rows = [row(g[k][None]) for k in VEC_ROWS] + [row(loss[0:1]), row(g["d_skip"].T), row(g["conv_w"])]
    rows.append(jnp.zeros((N_PACK_ROWS - ROW_CONV - 3, D_MODEL), F32))
    return jnp.concatenate(rows, axis=0)


def _adamw_small(tiles, rows, w, m, v):
    nn = len(SMALL)

    def body(*refs):
        t_ref, r_ref = refs[0], refs[1]
        w_refs, m_refs, v_refs = refs[2:2 + nn], refs[2 + nn:2 + 2 * nn], refs[2 + 2 * nn:2 + 3 * nn]
        loss_ref, outs = refs[2 + 3 * nn], refs[3 + 3 * nn:]

        def tile_sum(first, count):
            total = t_ref[0, first:first + count].astype(F32)
            for d in range(1, N_DEV):
                total = total + t_ref[d, first:first + count].astype(F32)
            return total

        def row_sum(first, count, lanes):
            total = r_ref[0, first:first + count, 0:lanes]
            for d in range(1, N_DEV):
                total = total + r_ref[d, first:first + count, 0:lanes]
            return total

        def step(j, g, at=lambda ref: ref):
            delta, nm, nv = _adamw_math(at(w_refs[j])[...], g, at(m_refs[j])[...], at(v_refs[j])[...])
            at(outs[j])[...] = g
            at(outs[nn + j])[...] = delta
            at(outs[2 * nn + j])[...] = nm
            at(outs[3 * nn + j])[...] = nv

        loss_ref[...] = row_sum(ROW_LOSS, 1, LANES)
        chip = 2 * lax.axis_index("x") + lax.axis_index("y")
        for j, name in enumerate(SMALL):
            if name in TILE_SLOTS:
                step(j, tile_sum(*TILE_SLOTS[name]))
            elif name == "d_skip":
                step(j, row_sum(ROW_DSKIP, GROUP, N_GROUPS))
            elif name == "conv_w":
                full = row_sum(ROW_CONV, 3, D_CONV)
                mine = full[:, 0:LANES]
                for s in range(1, N_CHIPS):
                    mine = jnp.where(chip == s, full[:, s * LANES:(s + 1) * LANES], mine)
                for k in range(3):
                    step(j, mine[k:k + 1, :], at=lambda ref, k=k: ref.at[k])
            else:
                step(j, row_sum(VEC_ROWS[name], 1, w_refs[j].shape[1]))

    args = [tiles, rows] + [w[k] for k in SMALL] + [m[k] for k in SMALL] + [v[k] for k in SMALL]
    res = pl.pallas_call(
        body, name="adamw_small", in_specs=[VMEM] * len(args), out_specs=[VMEM] * (1 + 4 * nn),
        out_shape=[jax.ShapeDtypeStruct((1, LANES), F32)] + [jax.ShapeDtypeStruct(w[k].shape, F32) for k in SMALL] * 4,
        compiler_params=pltpu.CompilerParams(vmem_limit_bytes=VMEM_LIMIT),
    )(*args)
    return res[0], [dict(zip(SMALL, res[1 + q * nn:1 + (q + 1) * nn])) for q in range(4)]


WEIGHTS = ["g_pre_mix", "w_in", "lam_re", "lam_im", "log_dt", "b_re", "b_im", "c_re", "c_im", "d_skip", "w_glu", "conv_w",
           "g_ssm_out", "g_conv_out", "w_out", "g_post_mix", "g_pre_mlp", "w_up", "w_down", "g_post_mlp"]
BIG = ["w_in", "w_glu", "w_out", "w_up", "w_down"]


def kernel(x, g_pre_mix, w_in, lam_re, lam_im, log_dt, b_re, b_im, c_re, c_im, d_skip, w_glu, conv_w, g_ssm_out, g_conv_out, w_out, g_post_mix, g_pre_mlp, w_up, w_down, g_post_mlp, loss_target, m_g_pre_mix, m_w_in, m_lam_re, m_lam_im, m_log_dt, m_b_re, m_b_im, m_c_re, m_c_im, m_d_skip, m_w_glu, m_conv_w, m_g_ssm_out, m_g_conv_out, m_w_out, m_g_post_mix, m_g_pre_mlp, m_w_up, m_w_down, m_g_post_mlp, v_g_pre_mix, v_w_in, v_lam_re, v_lam_im, v_log_dt, v_b_re, v_b_im, v_c_re, v_c_im, v_d_skip, v_w_glu, v_conv_w, v_g_ssm_out, v_g_conv_out, v_w_out, v_g_post_mix, v_g_pre_mlp, v_w_up, v_w_down, v_g_post_mlp):
    w = dict(g_pre_mix=g_pre_mix, w_in=w_in, lam_re=lam_re, lam_im=lam_im, log_dt=log_dt, b_re=b_re, b_im=b_im, c_re=c_re, c_im=c_im,
             d_skip=d_skip, w_glu=w_glu, conv_w=conv_w, g_ssm_out=g_ssm_out, g_conv_out=g_conv_out, w_out=w_out, g_post_mix=g_post_mix,
             g_pre_mlp=g_pre_mlp, w_up=w_up, w_down=w_down, g_post_mlp=g_post_mlp)
    m = dict(g_pre_mix=m_g_pre_mix, w_in=m_w_in, lam_re=m_lam_re, lam_im=m_lam_im, log_dt=m_log_dt, b_re=m_b_re, b_im=m_b_im, c_re=m_c_re,
             c_im=m_c_im, d_skip=m_d_skip, w_glu=m_w_glu, conv_w=m_conv_w, g_ssm_out=m_g_ssm_out, g_conv_out=m_g_conv_out, w_out=m_w_out,
             g_post_mix=m_g_post_mix, g_pre_mlp=m_g_pre_mlp, w_up=m_w_up, w_down=m_w_down, g_post_mlp=m_g_post_mlp)
    v = dict(g_pre_mix=v_g_pre_mix, w_in=v_w_in, lam_re=v_lam_re, lam_im=v_lam_im, log_dt=v_log_dt, b_re=v_b_re, b_im=v_b_im, c_re=v_c_re,
             c_im=v_c_im, d_skip=v_d_skip, w_glu=v_w_glu, conv_w=v_conv_w, g_ssm_out=v_g_ssm_out, g_conv_out=v_g_conv_out, w_out=v_w_out,
             g_post_mix=v_g_post_mix, g_pre_mlp=v_g_pre_mlp, w_up=v_w_up, w_down=v_w_down, g_post_mlp=v_g_post_mlp)
    w_dev, m_dev, v_dev = w, m, v
    w, m, v = ({k: a[0] for k, a in d.items()} for d in (w, m, v))
    core = lax.axis_index("c").astype(jnp.int32).reshape(1)

    xs, target = x[0], loss_target[0]
    g1 = w["g_pre_mix"][None]
    g_ssm, g_conv = w["g_ssm_out"][None], w["g_conv_out"][None]
    g_post_mix, g_pre_mlp, g_post_mlp = w["g_post_mix"][None], w["g_pre_mlp"][None], w["g_post_mlp"][None]
    bt_re, bt_im = (jnp.transpose(w[k], (0, 2, 1)) for k in ("b_re", "b_im"))
    dskip = w["d_skip"].reshape(N_GBLK, 1, UB)
    shard = {k: w[k].astype(MXU_DTYPE) for k in BIG}
    conv_pad = jnp.pad(w["conv_w"], ((0, SUBLANES - 3), (0, 0)))

    bmat, cmat, coef_f, coef_r, (w_in_all,) = _ssm_matrices(
        w["lam_re"], w["lam_im"], w["log_dt"], bt_re, bt_im, w["c_re"], w["c_im"], exchanges=[_GatherForward([shard["w_in"]])])
    hn, proj, u4, w_glu_all, w_out_all, conv_all = _inproj_fwd(
        xs, g1, w_in_all, exchanges=[_Gather([shard["w_glu"], shard["w_out"], conv_pad], [False, False, False])])
    wd_half = shard["w_down"].shape[0] // 2
    wda, wdb = shard["w_down"][:wd_half], shard["w_down"][wd_half:]
    s_re, s_im, ys, w_up_all, wda_all = _s5_fwd(u4, bmat, cmat, coef_f, dskip, exchanges=[_Gather([shard["w_up"], wda], [True, True])])
    w_glu_f, w_out_f = w_glu_all.reshape(D_SSM, D_SSM), w_out_all.reshape(D_MODEL, D_MODEL)
    conv_f = jnp.transpose(conv_all, (1, 0, 2)).reshape(SUBLANES, D_CONV)
    ycat, o, x1, w_up_all, wda_all, wdb_all = _tail_fwd(xs, ys, proj, w_glu_f, conv_f, g_ssm, g_conv, w_out_f, g_post_mix,
                                                        exchanges=[_Forward([w_up_all, wda_all]), _GatherForward([wdb])])
    hn2, up, m_act, dx2, loss = _mlp_fwd(x1, target, w_up_all, wda_all, wdb_all, g_pre_mlp, g_post_mlp)

    dm, dup, act, dx1, dg_post_mlp, dg_pre_mlp = _mlp_bwd(dx2, m_act, up, x1, w_up_all, wda_all, wdb_all, g_pre_mlp, g_post_mlp)
    gw_down = _matmul_tn(act, dm, "dw_down")[0].reshape(N_CHIPS, D_FF // N_CHIPS, D_MODEL)
    gw_up = _matmul_tn(hn2, dup, "dw_up", col_shards=N_CHIPS)[0]
    do, da, y1, dys, dhbc, dg_post_mix, dg_ssm, dg_conv, dconv_w, o_down, o_up = _tail_bwd(
        dx1, o, ys, proj, w_glu_f, conv_f, g_ssm, g_conv, w_out_f, g_post_mix, exchanges=[_Pair([gw_down, gw_up])])
    p_down = _pair_add(gw_down, o_down, core, "pair_add_w_down")
    p_up = _pair_add(gw_up, o_up, core, "pair_add_w_up")
    gw_out = _matmul_tn(ycat, do, "dw_out")[0].reshape(N_CHIPS, D_MODEL // N_CHIPS, D_MODEL)
    gw_glu, o_out = _matmul_tn(y1, da, "dw_glu", exchanges=[_Pair([gw_out])])
    gw_glu = gw_glu.reshape(N_CHIPS, D_SSM // N_CHIPS, D_SSM)
    p_out = _pair_add(gw_out, o_out, core, "pair_add_w_out")
    du, gb, gc, q, gd, q_down, q_up, q_out, o_glu = _s5_bwd(
        dys, u4, s_re, s_im, bmat, cmat, coef_r, dskip, exchanges=[_Chip([p_down, p_up, p_out]), _Pair([gw_glu])])
    h_down = _quad_sum(q_down, core, "quad_sum_w_down")
    h_up = _quad_sum(q_up, core, "quad_sum_w_up")
    h_out = _quad_sum(q_out, core, "quad_sum_w_out")
    p_glu = _pair_add(gw_glu, o_glu, core, "pair_add_w_glu")
    grad_x, dproj, dg_pre_mix = _inproj_bwd(du, dhbc, xs, dx1, w_in_all, g1)
    d_lam_re, d_lam_im, d_log_dt, d_b_re, d_b_im, d_c_re, d_c_im, d_d_skip = _ssm_param_grads(
        w["lam_re"], w["lam_im"], w["log_dt"], bt_re, bt_im, gb, gc, q, gd)
    small = {
        "g_pre_mix": dg_pre_mix[0], "lam_re": d_lam_re, "lam_im": d_lam_im, "log_dt": d_log_dt, "b_re": d_b_re, "b_im": d_b_im,
        "c_re": d_c_re, "c_im": d_c_im, "d_skip": d_d_skip, "conv_w": dconv_w[:3], "g_ssm_out": dg_ssm[0], "g_conv_out": dg_conv[0],
        "g_post_mix": dg_post_mix[0], "g_pre_mlp": dg_pre_mlp[0], "g_post_mlp": dg_post_mlp[0],
    }
    gw_in, g_down, g_up, g_out, q_glu, tiles, rows = _matmul_tn(
        hn, dproj, "dw_in", col_shards=N_CHIPS,
        exchanges=[_Share([h_down, h_up, h_out]), _Chip([p_glu]), _GatherSmall(_pack_tiles(small)), _GatherSmall(_pack_rows(small, loss))])
    h_glu = _quad_sum(q_glu, core, "quad_sum_w_glu")
    (o_in,) = _run_exchanges([_Pair([gw_in])], "rs_pair_w_in")
    p_in = _pair_add(gw_in, o_in, core, "pair_add_w_in")
    q_in, g_glu = _run_exchanges([_Chip([p_in]), _Share([h_glu])], "rs_chip_w_in")
    h_in = _quad_sum(q_in, core, "quad_sum_w_in")
    (g_in,) = _run_exchanges([_Share([h_in])], "rs_share_w_in")
    shard_grads = {"w_in": g_in, "w_glu": g_glu, "w_out": g_out, "w_up": g_up, "w_down": g_down}

    out = {q: {} for q in ("grad", "delta", "new_m", "new_v")}
    for k in BIG:
        out["grad"][k] = shard_grads[k][None]
        delta, new_m, new_v = _adamw(w[k], shard_grads[k], m[k], v[k], "adamw_" + k)
        out["delta"][k], out["new_m"][k], out["new_v"][k] = delta[None], new_m[None], new_v[None]
    form = lambda d: {k: _kernel_form(k, d[k]) for k in SMALL}
    loss, res = _adamw_small(tiles, rows, form(w_dev), form(m_dev), form(v_dev))
    for q, d in zip(("grad", "delta", "new_m", "new_v"), res):
        out[q].update({k: _param_form(k, d[k]) for k in SMALL})
    flat = [loss[0, 0], grad_x[None]]
    for q in ("grad", "delta", "new_m", "new_v"):
        flat += [out[q][k] for k in WEIGHTS]
    return tuple(flat)
```

```python
import functools
import math

import jax
import jax.numpy as jnp
import numpy as np
from jax import lax
from jax.experimental import pallas as pl
from jax.experimental.pallas import tpu as pltpu

F32 = jnp.float32
MXU_DTYPE = jnp.bfloat16
WIRE_DTYPE = jnp.bfloat16

D_MODEL = 1024
D_SSM = 512
D_CONV = 512
N_GROUPS = 32
GROUP = 16
STATE = 64
D_FF = 4096
RMS_EPS = 1e-6
N_CHIPS = 4
N_DEV = 8

ADAM_LR = 0.001
ADAM_B1 = 0.9
ADAM_B2 = 0.999
ADAM_EPS = 1e-08
ADAM_WD = 0.01
ADAM_STEP = 10

N_GBLK = 2
G_PER_BLK = N_GROUPS // N_GBLK
UB = G_PER_BLK * GROUP
WB = G_PER_BLK * STATE
LANE_CHUNK = 256
SUBLANES = 8
N_TABLES = 24

TM_PROJ = 512
TM_S5 = 512
TM_TAIL = 512
TM_MLP = 512
TL_TN = 2048
TK_TN = 1024
TN_TN = 2048
VMEM_LIMIT = 56 * 1024 * 1024

MESH = pl.DeviceIdType.MESH


def _params(sem, vmem=VMEM_LIMIT):
    return pltpu.CompilerParams(dimension_semantics=sem, vmem_limit_bytes=vmem)


def _resident(shape):
    nd = len(shape)
    return pl.BlockSpec(shape, lambda *_: (0,) * nd, pipeline_mode=pl.Buffered(1))


def _dot(a, b):
    return jnp.dot(a, b, preferred_element_type=F32)


def _dot_nt(a, b):
    return lax.dot_general(a, b, (((1,), (1,)), ((), ())), preferred_element_type=F32)


def _dot_tn(a, b):
    return lax.dot_general(a, b, (((0,), (0,)), ((), ())), preferred_element_type=F32)


def _rms_fwd(x, g):
    r = lax.rsqrt(jnp.mean(x * x, axis=-1, keepdims=True) + RMS_EPS)
    return x * r * g


def _rms_bwd(x, g, dy):
    r = lax.rsqrt(jnp.mean(x * x, axis=-1, keepdims=True) + RMS_EPS)
    xn = x * r
    q = dy * g
    dx = r * (q - xn * jnp.mean(q * xn, axis=-1, keepdims=True))
    return dx, jnp.sum(dy * xn, axis=0, keepdims=True)


_GELU_C = math.sqrt(2.0 / math.pi)


def _gelu(x):
    t = jnp.tanh(_GELU_C * (x + 0.044715 * (x * x * x)))
    y = x * (0.5 * (1.0 + t))
    dy = 0.5 * (1.0 + t) + 0.5 * x * (1.0 - t * t) * (_GELU_C * (1.0 + 3 * 0.044715 * (x * x)))
    return y, dy


def _tile(n, pref):
    t = min(n, pref)
    assert n % t == 0, (n, t)
    return t


HBM = pl.BlockSpec(memory_space=pltpu.HBM)
VMEM = pl.BlockSpec(memory_space=pltpu.VMEM)
DMA_SEMS = pltpu.SemaphoreType.DMA


def _place():
    x, y, c = lax.axis_index("x"), lax.axis_index("y"), lax.axis_index("c")
    chips = [(1 - x, y), (x, 1 - y), (1 - x, 1 - y)]
    return (x, y, c), 2 * x + y, (x, y, 1 - c), chips, [2 * px + py for px, py in chips]


def _remote(src, dst, send_sem, recv_sem, device):
    return pltpu.make_async_remote_copy(src_ref=src, dst_ref=dst, send_sem=send_sem, recv_sem=recv_sem,
                                        device_id=device, device_id_type=MESH)


def _half(rows, c):
    return pl.ds(c * (rows // 2), rows // 2)


class _Exchange:
    aliases = {}

    def start(self, ins, outs, sems):
        local, outgoing, _ = self._copies(ins, outs, sems)
        for cp in local + outgoing:
            cp.start()

    def finish(self, ins, outs, sems):
        local, outgoing, incoming = self._copies(ins, outs, sems)
        for cp in incoming:
            cp.wait_recv()
        for cp in outgoing:
            cp.wait_send()
        for cp in local:
            cp.wait()


class _Gather(_Exchange):
    def __init__(self, shards, split):
        self.inputs, self.split = list(shards), split
        self.out_shape = [jax.ShapeDtypeStruct((N_CHIPS, *a.shape), a.dtype) for a in shards]
        self.sems = [DMA_SEMS((len(shards), 3)), DMA_SEMS((len(shards), 3)), DMA_SEMS((len(shards),))]

    def _copies(self, ins, outs, sems):
        send, recv, lsem = sems
        (x, y, c), me, sibling, chips, ids = _place()
        local = [pltpu.make_async_copy(ins[t], outs[t].at[me], lsem.at[t]) for t in range(len(ins))]
        outgoing, incoming = [], []
        for t, a in enumerate(self.inputs):
            rows = _half(a.shape[0], c) if self.split[t] else pl.ds(0, a.shape[0])
            for k in range(3):
                to = (*chips[k], c)
                outgoing.append(_remote(ins[t].at[rows, :], outs[t].at[me, rows, :], send.at[t, k], recv.at[t, k], to))
                incoming.append(_remote(ins[t].at[rows, :], outs[t].at[ids[k], rows, :], send.at[t, k], recv.at[t, k], to))
        return local, outgoing, incoming


class _Forward(_Exchange):
    def __init__(self, arrays):
        self.inputs = list(arrays)
        self.out_shape = [jax.ShapeDtypeStruct(a.shape, a.dtype) for a in arrays]
        self.aliases = {t: t for t in range(len(arrays))}
        self.sems = [DMA_SEMS((len(arrays), 3)), DMA_SEMS((len(arrays), 3))]

    def _copies(self, ins, outs, sems):
        send, recv = sems
        (x, y, c), me, sibling, chips, ids = _place()
        outgoing, incoming = [], []
        for t, a in enumerate(self.inputs):
            for k in range(3):
                mine = outs[t].at[ids[k], _half(a.shape[1], c), :]
                theirs = outs[t].at[ids[k], _half(a.shape[1], 1 - c), :]
                outgoing.append(_remote(mine, mine, send.at[t, k], recv.at[t, k], sibling))
                incoming.append(_remote(theirs, theirs, send.at[t, k], recv.at[t, k], sibling))
        return [], outgoing, incoming


class _GatherForward(_Exchange):
    def __init__(self, shards):
        self.gather = _Gather(shards, [True] * len(shards))
        self.forward = _Forward(self.gather.out_shape)
        self.inputs, self.out_shape = self.gather.inputs, self.gather.out_shape
        self.sems = self.gather.sems + self.forward.sems

    def start(self, ins, outs, sems):
        self.gather.start(ins, outs, sems[:3])

    def finish(self, ins, outs, sems):
        local, outgoing, incoming = self.gather._copies(ins, outs, sems[:3])
        _, passed, from_sibling = self.forward._copies(outs, outs, sems[3:])
        for landed, onward in zip(incoming, passed):
            landed.wait_recv()
            onward.start()
        for cp in from_sibling:
            cp.wait_recv()
        for cp in outgoing + passed:
            cp.wait_send()
        for cp in local:
            cp.wait()


class _Pair(_Exchange):
    def __init__(self, grads):
        self.inputs = list(grads)
        self.out_shape = [jax.ShapeDtypeStruct((g.shape[0], g.shape[1] // 2, g.shape[2]), g.dtype) for g in grads]
        self.sems = [DMA_SEMS((len(grads),)), DMA_SEMS((len(grads),))]

    def _copies(self, ins, outs, sems):
        send, recv = sems
        (x, y, c), me, sibling, chips, ids = _place()
        cps = [_remote(ins[t].at[:, _half(g.shape[1], 1 - c), :], outs[t], send.at[t], recv.at[t], sibling)
               for t, g in enumerate(self.inputs)]
        return [], cps, cps


class _Chip(_Exchange):
    def __init__(self, parts):
        self.inputs = list(parts)
        self.out_shape = [jax.ShapeDtypeStruct(p.shape, p.dtype) for p in parts]
        self.sems = [DMA_SEMS((len(parts), 3)), DMA_SEMS((len(parts), 3)), DMA_SEMS((len(parts),))]

    def _copies(self, ins, outs, sems):
        send, recv, lsem = sems
        (x, y, c), me, sibling, chips, ids = _place()
        local = [pltpu.make_async_copy(ins[t].at[me], outs[t].at[me], lsem.at[t]) for t in range(len(ins))]
        outgoing, incoming = [], []
        for t in range(len(ins)):
            for k in range(3):
                to = (*chips[k], c)
                outgoing.append(_remote(ins[t].at[ids[k]], outs[t].at[me], send.at[t, k], recv.at[t, k], to))
                incoming.append(_remote(ins[t].at[ids[k]], outs[t].at[ids[k]], send.at[t, k], recv.at[t, k], to))
        return local, outgoing, incoming


class _Share(_Exchange):
    def __init__(self, grads):
        self.inputs = list(grads)
        self.out_shape = [jax.ShapeDtypeStruct(g.shape, g.dtype) for g in grads]
        self.aliases = {t: t for t in range(len(grads))}
        self.sems = [DMA_SEMS((len(grads),)), DMA_SEMS((len(grads),))]

    def _copies(self, ins, outs, sems):
        send, recv = sems
        (x, y, c), me, sibling, chips, ids = _place()
        outgoing, incoming = [], []
        for t, g in enumerate(self.inputs):
            mine = outs[t].at[_half(g.shape[0], c), :]
            theirs = outs[t].at[_half(g.shape[0], 1 - c), :]
            outgoing.append(_remote(mine, mine, send.at[t], recv.at[t], sibling))
            incoming.append(_remote(theirs, theirs, send.at[t], recv.at[t], sibling))
        return [], outgoing, incoming


class _GatherSmall(_Exchange):
    def __init__(self, block):
        self.inputs = [block]
        self.out_shape = [jax.ShapeDtypeStruct((N_DEV, *block.shape), block.dtype)]
        self.sems = [DMA_SEMS((7,)), DMA_SEMS((7,)), DMA_SEMS(())]

    def _copies(self, ins, outs, sems):
        send, recv, lsem = sems
        (x, y, c), me, sibling, chips, ids = _place()
        slot = lambda px, py, pc: outs[0].at[4 * px + 2 * py + pc]

        def copy(k, block, to, src=None):
            return _remote(slot(*block) if src is None else src, slot(*block), send.at[k], recv.at[k], to)

        local = [pltpu.make_async_copy(ins[0], slot(x, y, c), lsem)]
        first = [copy(0, (x, y, c), sibling, src=ins[0])] + [copy(1 + j, (x, y, c), (*chip, c), src=ins[0]) for j, chip in enumerate(chips)]
        passed = [copy(4 + j, (*chip, c), sibling) for j, chip in enumerate(chips)]
        landed = [copy(1 + j, (*chip, c), (x, y, c)) for j, chip in enumerate(chips)]
        from_sibling = [copy(0, (x, y, 1 - c), (x, y, c))] + [copy(4 + j, (*chip, 1 - c), (x, y, c)) for j, chip in enumerate(chips)]
        return local, first, (passed, landed, from_sibling)

    def finish(self, ins, outs, sems):
        local, first, (passed, landed, from_sibling) = self._copies(ins, outs, sems)
        for j in range(3):
            landed[j].wait_recv()
            passed[j].start()
        for cp in from_sibling:
            cp.wait_recv()
        for cp in first + passed:
            cp.wait_send()
        for cp in local:
            cp.wait()


def _split_refs(refs, counts):
    out = []
    for n in counts:
        out.append(refs[:n])
        refs = refs[n:]
    return out


def _each_exchange(exchanges, method, x_in, x_out, x_sem):
    for ex in exchanges:
        ni, no, ns = len(ex.inputs), len(ex.out_shape), len(ex.sems)
        getattr(ex, method)(x_in[:ni], x_out[:no], x_sem[:ns])
        x_in, x_out, x_sem = x_in[ni:], x_out[no:], x_sem[ns:]


def _call(body, *, name, grid, in_specs, out_specs, out_shape, operands, semantics, scratch_shapes=(), exchanges=()):
    x_in = [a for ex in exchanges for a in ex.inputs]
    x_out = [s for ex in exchanges for s in ex.out_shape]
    x_sem = [s for ex in exchanges for s in ex.sems]
    counts = (len(in_specs), len(x_in), len(out_specs), len(x_out), len(scratch_shapes), len(x_sem))
    aliases, i0, o0 = {}, len(in_specs), len(out_specs)
    for ex in exchanges:
        aliases.update({i0 + i: o0 + o for i, o in ex.aliases.items()})
        i0, o0 = i0 + len(ex.inputs), o0 + len(ex.out_shape)

    def full_body(*refs):
        ins, xi, outs, xo, scr, xs = _split_refs(list(refs), counts)
        if exchanges:
            @pl.when(functools.reduce(jnp.logical_and, [pl.program_id(a) == 0 for a in range(len(grid))]))
            def _():
                _each_exchange(exchanges, "start", xi, xo, xs)

        body(*ins, *outs, *scr)
        if exchanges:
            @pl.when(functools.reduce(jnp.logical_and, [pl.program_id(a) == grid[a] - 1 for a in range(len(grid))]))
            def _():
                _each_exchange(exchanges, "finish", xi, xo, xs)

    return pl.pallas_call(
        full_body, name=name, grid=grid,
        in_specs=list(in_specs) + [HBM] * len(x_in), out_specs=list(out_specs) + [HBM] * len(x_out),
        out_shape=list(out_shape) + x_out, scratch_shapes=list(scratch_shapes) + x_sem,
        input_output_aliases=aliases, compiler_params=_params(semantics),
    )(*operands, *x_in)


def _run_exchanges(exchanges, name):
    x_in = [a for ex in exchanges for a in ex.inputs]
    x_out = [s for ex in exchanges for s in ex.out_shape]
    x_sem = [s for ex in exchanges for s in ex.sems]
    aliases, i0, o0 = {}, 0, 0
    for ex in exchanges:
        aliases.update({i0 + i: o0 + o for i, o in ex.aliases.items()})
        i0, o0 = i0 + len(ex.inputs), o0 + len(ex.out_shape)

    def body(*refs):
        xi, xo, xs = _split_refs(list(refs), (len(x_in), len(x_out), len(x_sem)))
        _each_exchange(exchanges, "start", xi, xo, xs)
        _each_exchange(exchanges, "finish", xi, xo, xs)

    return pl.pallas_call(
        body, name=name, in_specs=[HBM] * len(x_in), out_specs=[HBM] * len(x_out), out_shape=x_out,
        scratch_shapes=x_sem, input_output_aliases=aliases,
    )(*x_in)


def _inproj_fwd(x, g1, w_in_all, exchanges=()):
    L, D = x.shape
    ns, _, nc = w_in_all.shape
    tm = _tile(L, TM_PROJ)

    def body(x_ref, g_ref, w_ref, hn_ref, proj_ref, u_ref):
        hn = _rms_fwd(x_ref[...], g_ref[...]).astype(MXU_DTYPE)
        hn_ref[...] = hn
        for j in range(ns):
            proj_ref[:, j * nc:(j + 1) * nc] = _dot(hn, w_ref[j])
        _store_slabs(u_ref, proj_ref[:, 0:nc])

    return _call(
        body, name="inproj_fwd", grid=(L // tm,), exchanges=exchanges, semantics=("arbitrary",), operands=(x, g1, w_in_all),
        in_specs=[pl.BlockSpec((tm, D), lambda i: (i, 0)), _resident((1, D)), _resident(w_in_all.shape)],
        out_specs=[pl.BlockSpec((tm, D), lambda i: (i, 0)), pl.BlockSpec((tm, ns * nc), lambda i: (i, 0)), _slab_spec(nc, tm)],
        out_shape=[jax.ShapeDtypeStruct((L, D), MXU_DTYPE), jax.ShapeDtypeStruct((L, ns * nc), F32), _slab_shape(L, nc)],
    )


def _slab_shape(L, n):
    return jax.ShapeDtypeStruct((n // LANES, L, LANES), F32)


def _slab_spec(n, tm, index=lambda i: (0, i, 0)):
    return pl.BlockSpec((n // LANES, tm, LANES), index)


def _store_slabs(ref, value):
    for k in range(ref.shape[0]):
        ref[k] = value[:, k * LANES:(k + 1) * LANES]


def _load_slabs(ref):
    return jnp.concatenate([ref[k] for k in range(ref.shape[0])], axis=1)


SEG_ROWS = SUBLANES * SUBLANES


def _load_permuted(ref):
    tm = ref.shape[1]
    slabs = []
    for k in range(ref.shape[0]):
        tiles = [ref.at[k][pl.ds(b * SEG_ROWS + j, SUBLANES, stride=SUBLANES), :] for b in range(tm // SEG_ROWS) for j in range(SUBLANES)]
        slabs.append(jnp.concatenate(tiles, axis=0))
    return jnp.concatenate(slabs, axis=1)


def _store_permuted(ref, value):
    tm = ref.shape[1]
    for k in range(ref.shape[0]):
        for b in range(tm // SEG_ROWS):
            for j in range(SUBLANES):
                r = b * SEG_ROWS + j * SUBLANES
                ref.at[k][pl.ds(b * SEG_ROWS + j, SUBLANES, stride=SUBLANES), :] = value[r:r + SUBLANES, k * LANES:(k + 1) * LANES]


def _scan_tile(xr, xi, hr, hi, coef_ref, lanes, reverse):
    for k, j in ((1, 0), (2, 2), (4, 4)):
        ar = coef_ref[j, :, lanes]
        ai = coef_ref[j + 1, :, lanes]
        shift = SUBLANES - k if reverse else k
        sr = pltpu.roll(xr, shift, 0)
        si = pltpu.roll(xi, shift, 0)
        xr, xi = xr + (ar * sr - ai * si), xi + (ar * si + ai * sr)
    pr = coef_ref[6, :, lanes]
    pi = coef_ref[7, :, lanes]
    return xr + (pr * hr - pi * hi), xi + (pr * hi + pi * hr)


def _scan_block(read, write, hr, hi, coef_ref, lanes, reverse):
    order = list(range(SUBLANES - 1, -1, -1) if reverse else range(SUBLANES))
    near = 8 + 2 * order[0]
    ar = coef_ref[near, :, lanes]
    ai = coef_ref[near + 1, :, lanes]
    xr, xi = read(order[0])
    local = {order[0]: (xr, xi)}
    for j in order[1:]:
        br, bi = read(j)
        xr, xi = br + (ar * xr - ai * xi), bi + (ar * xi + ai * xr)
        local[j] = (xr, xi)
    er, ei = _scan_tile(xr, xi, hr, hi, coef_ref, lanes, reverse)
    edge = lax.broadcasted_iota(jnp.int32, er.shape, 0) == (SUBLANES - 1 if reverse else 0)
    shift = SUBLANES - 1 if reverse else 1
    pr = jnp.where(edge, hr, pltpu.roll(er, shift, 0))
    pi = jnp.where(edge, hi, pltpu.roll(ei, shift, 0))
    for j in range(SUBLANES):
        cr = coef_ref[8 + 2 * j, :, lanes]
        ci = coef_ref[9 + 2 * j, :, lanes]
        xr, xi = local[j]
        write(j, xr + (cr * pr - ci * pi), xi + (cr * pi + ci * pr))
    end = 0 if reverse else SUBLANES - 1
    return jnp.broadcast_to(er[end:end + 1, :], er.shape), jnp.broadcast_to(ei[end:end + 1, :], ei.shape)


def _s5_fwd(u4, bmat, cmat, coef, dskip, exchanges=()):
    L = u4.shape[1]
    tm = _tile(L, TM_S5)
    lc = min(LANE_CHUNK, WB)

    def body(u_ref, bm_ref, cm_ref, coef_ref, d_ref, sre_ref, sim_ref, ys_ref, hr_ref, hi_ref):
        @pl.when(pl.program_id(1) == 0)
        def _():
            hr_ref[...] = jnp.zeros_like(hr_ref)
            hi_ref[...] = jnp.zeros_like(hi_ref)

        u = _load_permuted(u_ref)
        bu = _dot(u.astype(MXU_DTYPE), bm_ref[0])
        sre_ref[...] = bu[:, :WB]
        sim_ref[...] = bu[:, WB:]
        for c in range(WB // lc):
            lanes = slice(c * lc, (c + 1) * lc)
            hr, hi = hr_ref[:, lanes], hi_ref[:, lanes]
            for b in range(tm // SEG_ROWS):
                rows = lambda j, b=b: slice(b * SEG_ROWS + j * SUBLANES, b * SEG_ROWS + (j + 1) * SUBLANES)

                def read(j, rows=rows, lanes=lanes):
                    return sre_ref[rows(j), lanes], sim_ref[rows(j), lanes]

                def write(j, xr, xi, rows=rows, lanes=lanes):
                    sre_ref[rows(j), lanes] = xr
                    sim_ref[rows(j), lanes] = xi

                hr, hi = _scan_block(read, write, hr, hi, coef_ref, lanes, False)
            hr_ref[:, lanes] = hr
            hi_ref[:, lanes] = hi
        ys = _dot_nt(sre_ref[...].astype(MXU_DTYPE), cm_ref[0, :, :WB]) + _dot_nt(sim_ref[...].astype(MXU_DTYPE), cm_ref[0, :, WB:])
        _store_permuted(ys_ref, ys + d_ref[0] * u)

    return _call(
        body, name="s5_fwd", grid=(N_GBLK, L // tm), exchanges=exchanges, semantics=("arbitrary", "arbitrary"),
        operands=(u4, bmat, cmat, coef, dskip),
        in_specs=[
            _slab_spec(UB, tm, lambda b, i: (b, i, 0)),
            pl.BlockSpec((1, UB, 2 * WB), lambda b, i: (b, 0, 0)),
            pl.BlockSpec((1, UB, 2 * WB), lambda b, i: (b, 0, 0)),
            pl.BlockSpec((N_TABLES, SUBLANES, WB), lambda b, i: (0, 0, b)),
            pl.BlockSpec((1, 1, UB), lambda b, i: (b, 0, 0)),
        ],
        out_specs=[
            pl.BlockSpec((tm, WB), lambda b, i: (i, b)),
            pl.BlockSpec((tm, WB), lambda b, i: (i, b)),
            _slab_spec(UB, tm, lambda b, i: (b, i, 0)),
        ],
        out_shape=[
            jax.ShapeDtypeStruct((L, N_GBLK * WB), F32),
            jax.ShapeDtypeStruct((L, N_GBLK * WB), F32),
            _slab_shape(L, D_SSM),
        ],
        scratch_shapes=[pltpu.VMEM((SUBLANES, WB), F32), pltpu.VMEM((SUBLANES, WB), F32)],
    )


def _tail_fwd(x, ys, proj, w_glu, conv_w, g_ssm, g_conv, w_out, g_post, exchanges=()):
    L, D = x.shape
    tm = _tile(L, TM_TAIL)

    def body(x_ref, ys_ref, h_ref, bg_ref, cg_ref, wglu_ref, cw_ref, gs_ref, gc_ref, wout_ref, gp_ref,
             ycat_ref, o_ref, x1_ref, zbuf):
        @pl.when(pl.program_id(0) == 0)
        def _():
            zbuf[0:SUBLANES, :] = jnp.zeros((SUBLANES, D_CONV), F32)

        y1, _ = _gelu(_load_slabs(ys_ref))
        y2 = y1 * jax.nn.sigmoid(_dot(y1.astype(MXU_DTYPE), wglu_ref[...]))
        ycat_ref[:, :D_SSM] = _rms_fwd(y2, gs_ref[...]).astype(MXU_DTYPE)
        z = cg_ref[...] * h_ref[...]
        zbuf[SUBLANES:, :] = z
        conv = cw_ref[0:1, :] * zbuf[SUBLANES - 2:SUBLANES - 2 + tm, :] + cw_ref[1:2, :] * zbuf[SUBLANES - 1:SUBLANES - 1 + tm, :] + cw_ref[2:3, :] * z
        zbuf[0:SUBLANES, :] = zbuf[tm:tm + SUBLANES, :]
        ycat_ref[:, D_SSM:] = _rms_fwd(bg_ref[...] * conv, gc_ref[...]).astype(MXU_DTYPE)
        o = _dot(ycat_ref[...], wout_ref[...])
        o_ref[...] = o
        x1_ref[...] = x_ref[...] + _rms_fwd(o, gp_ref[...])

    row = lambda i: (i, 0)
    return _call(
        body, name="tail_fwd", grid=(L // tm,), exchanges=exchanges, semantics=("arbitrary",),
        operands=(x, ys, proj, proj, proj, w_glu, conv_w, g_ssm, g_conv, w_out, g_post),
        in_specs=[
            pl.BlockSpec((tm, D), row), _slab_spec(D_SSM, tm),
            pl.BlockSpec((tm, D_CONV), lambda i: (i, 1)), pl.BlockSpec((tm, D_CONV), lambda i: (i, 2)),
            pl.BlockSpec((tm, D_CONV), lambda i: (i, 3)),
            _resident(w_glu.shape), _resident(conv_w.shape), _resident(g_ssm.shape), _resident(g_conv.shape),
            _resident(w_out.shape), _resident(g_post.shape),
        ],
        out_specs=[pl.BlockSpec((tm, D), row), pl.BlockSpec((tm, D), row), pl.BlockSpec((tm, D), row)],
        out_shape=[jax.ShapeDtypeStruct((L, D), MXU_DTYPE), jax.ShapeDtypeStruct((L, D), F32), jax.ShapeDtypeStruct((L, D), F32)],
        scratch_shapes=[pltpu.VMEM((tm + SUBLANES, D_CONV), F32)],
    )


def _mlp_fwd(x1, target, w_up_all, w_down_a, w_down_b, g_pre, g_post):
    L, D = x1.shape
    ns, _, fc = w_up_all.shape
    half = w_down_a.shape[1]
    tm = _tile(L, TM_MLP)

    def body(x1_ref, t_ref, wup_ref, wda_ref, wdb_ref, gpre_ref, gpost_ref, hn2_ref, up_ref, act_ref, m_ref, dx2_ref, loss_ref):
        @pl.when(pl.program_id(0) == 0)
        def _():
            loss_ref[...] = jnp.zeros_like(loss_ref)

        x1v = x1_ref[...]
        hn2 = _rms_fwd(x1v, gpre_ref[...]).astype(MXU_DTYPE)
        hn2_ref[...] = hn2
        m = jnp.zeros((tm, D), F32)
        for j in range(ns):
            up = _dot(hn2, wup_ref[j])
            up_ref[:, j * fc:(j + 1) * fc] = up.astype(MXU_DTYPE)
            act = jnp.square(jnp.maximum(up, 0.0)).astype(MXU_DTYPE)
            act_ref[:, j * fc:(j + 1) * fc] = act
            m = m + _dot(act[:, :half], wda_ref[j]) + _dot(act[:, half:], wdb_ref[j])
        m_ref[...] = m
        err = x1v + _rms_fwd(m, gpost_ref[...]) - t_ref[...]
        loss_ref[...] += 0.5 * jnp.sum(jnp.mean(err * err, axis=-1, keepdims=True))
        dx2_ref[...] = err * (1.0 / D)

    row = lambda i: (i, 0)
    return pl.pallas_call(
        body, name="mlp_fwd", grid=(L // tm,),
        in_specs=[pl.BlockSpec((tm, D), row), pl.BlockSpec((tm, D), row), _resident(w_up_all.shape), _resident(w_down_a.shape),
                  _resident(w_down_b.shape), _resident(g_pre.shape), _resident(g_post.shape)],
        out_specs=[pl.BlockSpec((tm, D), row), pl.BlockSpec((tm, ns * fc), row), pl.BlockSpec((tm, ns * fc), row), pl.BlockSpec((tm, D), row),
                   pl.BlockSpec((tm, D), row), pl.BlockSpec((SUBLANES, LANES), lambda i: (0, 0))],
        out_shape=[jax.ShapeDtypeStruct((L, D), MXU_DTYPE), jax.ShapeDtypeStruct((L, ns * fc), MXU_DTYPE), jax.ShapeDtypeStruct((L, ns * fc), MXU_DTYPE),
                   jax.ShapeDtypeStruct((L, D), F32), jax.ShapeDtypeStruct((L, D), F32), jax.ShapeDtypeStruct((SUBLANES, LANES), F32)],
        compiler_params=_params(("arbitrary",)),
    )(x1, target, w_up_all, w_down_a, w_down_b, g_pre, g_post)


def _mlp_bwd(dx2, m, up, x1, w_up_all, w_down_a, w_down_b, g_pre, g_post):
    L, D = x1.shape
    ns, _, fc = w_up_all.shape
    tm = _tile(L, TM_MLP)

    def body(dx2_ref, m_ref, up_ref, x1_ref, wup_ref, wda_ref, wdb_ref, gpre_ref, gpost_ref,
             dm_ref, dup_ref, dx1_ref, dgpost_ref, dgpre_ref):
        @pl.when(pl.program_id(0) == 0)
        def _():
            dgpost_ref[...] = jnp.zeros_like(dgpost_ref)
            dgpre_ref[...] = jnp.zeros_like(dgpre_ref)

        dx2v = dx2_ref[...]
        dm, dg = _rms_bwd(m_ref[...], gpost_ref[...], dx2v)
        dgpost_ref[...] += dg
        dm_b = dm.astype(MXU_DTYPE)
        dm_ref[...] = dm_b
        dhn2 = jnp.zeros((tm, D), F32)
        for j in range(ns):
            cols = slice(j * fc, (j + 1) * fc)
            relu = jnp.maximum(up_ref[:, cols].astype(F32), 0.0)
            dact = jnp.concatenate([_dot_nt(dm_b, wda_ref[j]), _dot_nt(dm_b, wdb_ref[j])], axis=1)
            dup = (dact * (2.0 * relu)).astype(MXU_DTYPE)
            dup_ref[:, cols] = dup
            dhn2 = dhn2 + _dot_nt(dup, wup_ref[j])
        dx, dg = _rms_bwd(x1_ref[...], gpre_ref[...], dhn2)
        dgpre_ref[...] += dg
        dx1_ref[...] = dx2v + dx

    row = lambda i: (i, 0)
    vec = pl.BlockSpec((1, D), lambda i: (0, 0))
    return pl.pallas_call(
        body, name="mlp_bwd", grid=(L // tm,),
        in_specs=[pl.BlockSpec((tm, D), row), pl.BlockSpec((tm, D), row), pl.BlockSpec((tm, ns * fc), row), pl.BlockSpec((tm, D), row),
                  _resident(w_up_all.shape), _resident(w_down_a.shape), _resident(w_down_b.shape), _resident(g_pre.shape), _resident(g_post.shape)],
        out_specs=[pl.BlockSpec((tm, D), row), pl.BlockSpec((tm, ns * fc), row), pl.BlockSpec((tm, D), row), vec, vec],
        out_shape=[jax.ShapeDtypeStruct((L, D), MXU_DTYPE), jax.ShapeDtypeStruct((L, ns * fc), MXU_DTYPE), jax.ShapeDtypeStruct((L, D), F32),
                   jax.ShapeDtypeStruct((1, D), F32), jax.ShapeDtypeStruct((1, D), F32)],
        compiler_params=_params(("arbitrary",)),
    )(dx2, m, up, x1, w_up_all, w_down_a, w_down_b, g_pre, g_post)


def _tail_bwd(dx1, o, ys, proj, w_glu, conv_w, g_ssm, g_conv, w_out, g_post, exchanges=()):
    L, D = dx1.shape
    tm = _tile(L, TM_TAIL)
    nt = L // tm
    hb = tm // SUBLANES

    def body(dx1_ref, o_ref, ys_ref, h_ref, bg_ref, cg_ref, hh_ref, hcg_ref, wglu_ref, cw_ref, gs_ref, gc_ref, wout_ref, gp_ref,
             do_ref, da_ref, y1_ref, dys_ref, dhbc_ref, dgp_ref, dgs_ref, dgc_ref, dcw_ref, zbuf, dcbuf):
        step = pl.program_id(0)

        @pl.when(step == 0)
        def _():
            dcbuf[tm:, :] = jnp.zeros((SUBLANES, D_CONV), F32)
            dgp_ref[...] = jnp.zeros_like(dgp_ref)
            dgs_ref[...] = jnp.zeros_like(dgs_ref)
            dgc_ref[...] = jnp.zeros_like(dgc_ref)
            dcw_ref[...] = jnp.zeros_like(dcw_ref)

        do, dg = _rms_bwd(o_ref[...], gp_ref[...], dx1_ref[...])
        dgp_ref[...] += dg
        do_b = do.astype(MXU_DTYPE)
        do_ref[...] = do_b
        dycat = _dot_nt(do_b, wout_ref[...])
        y1, dgelu = _gelu(_load_slabs(ys_ref))
        y1_b = y1.astype(MXU_DTYPE)
        y1_ref[...] = y1_b
        s = jax.nn.sigmoid(_dot(y1_b, wglu_ref[...]))
        dy2, dg = _rms_bwd(y1 * s, gs_ref[...], dycat[:, :D_SSM])
        dgs_ref[...] += dg
        da_b = (dy2 * y1 * s * (1.0 - s)).astype(MXU_DTYPE)
        da_ref[...] = da_b
        _store_slabs(dys_ref, (dy2 * s + _dot_nt(da_b, wglu_ref[...])) * dgelu)
        h = h_ref[...]
        cg = cg_ref[...]
        bg = bg_ref[...]
        z = cg * h
        first = step == nt - 1
        zbuf[0:SUBLANES, :] = jnp.where(first, 0.0, hcg_ref[...] * hh_ref[...])
        zbuf[SUBLANES:, :] = z
        z1 = zbuf[SUBLANES - 1:SUBLANES - 1 + tm, :]
        z2 = zbuf[SUBLANES - 2:SUBLANES - 2 + tm, :]
        conv = cw_ref[0:1, :] * z2 + cw_ref[1:2, :] * z1 + cw_ref[2:3, :] * z
        dyc, dg = _rms_bwd(bg * conv, gc_ref[...], dycat[:, D_SSM:])
        dgc_ref[...] += dg
        dconv = dyc * bg
        dcw_ref[0:1, :] += jnp.sum(dconv * z2, axis=0, keepdims=True)
        dcw_ref[1:2, :] += jnp.sum(dconv * z1, axis=0, keepdims=True)
        dcw_ref[2:3, :] += jnp.sum(dconv * z, axis=0, keepdims=True)
        dcbuf[0:tm, :] = dconv
        dz = cw_ref[2:3, :] * dconv + cw_ref[1:2, :] * dcbuf[1:1 + tm, :] + cw_ref[0:1, :] * dcbuf[2:2 + tm, :]
        dcbuf[tm:, :] = dcbuf[0:SUBLANES, :]
        dhbc_ref[:, 0:D_CONV] = (dz * cg).astype(MXU_DTYPE)
        dhbc_ref[:, D_CONV:2 * D_CONV] = (dyc * conv).astype(MXU_DTYPE)
        dhbc_ref[:, 2 * D_CONV:] = (dz * h).astype(MXU_DTYPE)

    rev = lambda i: (nt - 1 - i, 0)
    slab = lambda i: (0, nt - 1 - i, 0)
    col = lambda c: (lambda i: (nt - 1 - i, c))
    halo = lambda c: (lambda i: (jnp.maximum((nt - 1 - i) * hb - 1, 0), c))
    vec = lambda n: pl.BlockSpec((1, n), lambda i: (0, 0))
    return _call(
        body, name="tail_bwd", grid=(nt,), exchanges=exchanges, semantics=("arbitrary",),
        operands=(dx1, o, ys, proj, proj, proj, proj, proj, w_glu, conv_w, g_ssm, g_conv, w_out, g_post),
        in_specs=[
            pl.BlockSpec((tm, D), rev), pl.BlockSpec((tm, D), rev), _slab_spec(D_SSM, tm, slab),
            pl.BlockSpec((tm, D_CONV), col(1)), pl.BlockSpec((tm, D_CONV), col(2)), pl.BlockSpec((tm, D_CONV), col(3)),
            pl.BlockSpec((SUBLANES, D_CONV), halo(1)), pl.BlockSpec((SUBLANES, D_CONV), halo(3)),
            _resident(w_glu.shape), _resident(conv_w.shape), _resident(g_ssm.shape), _resident(g_conv.shape),
            _resident(w_out.shape), _resident(g_post.shape),
        ],
        out_specs=[
            pl.BlockSpec((tm, D), rev), pl.BlockSpec((tm, D_SSM), rev), pl.BlockSpec((tm, D_SSM), rev), _slab_spec(D_SSM, tm, slab),
            pl.BlockSpec((tm, 3 * D_CONV), rev), vec(D), vec(D_SSM), vec(D_CONV),
            pl.BlockSpec((SUBLANES, D_CONV), lambda i: (0, 0)),
        ],
        out_shape=[
            jax.ShapeDtypeStruct((L, D), MXU_DTYPE), jax.ShapeDtypeStruct((L, D_SSM), MXU_DTYPE), jax.ShapeDtypeStruct((L, D_SSM), MXU_DTYPE),
            _slab_shape(L, D_SSM), jax.ShapeDtypeStruct((L, 3 * D_CONV), MXU_DTYPE),
            jax.ShapeDtypeStruct((1, D), F32), jax.ShapeDtypeStruct((1, D_SSM), F32), jax.ShapeDtypeStruct((1, D_CONV), F32),
            jax.ShapeDtypeStruct((SUBLANES, D_CONV), F32),
        ],
        scratch_shapes=[pltpu.VMEM((tm + SUBLANES, D_CONV), F32), pltpu.VMEM((tm + SUBLANES, D_CONV), F32)],
    )


def _s5_bwd(dys, u4, s_re, s_im, bmat, cmat, coef_rev, dskip, exchanges=()):
    L = dys.shape[1]
    tm = _tile(L, TM_S5)
    nt = L // tm
    lc = min(LANE_CHUNK, WB)

    def body(dys_ref, u_ref, sre_ref, sim_ref, bm_ref, cm_ref, coef_ref, d_ref,
             du_ref, gb_ref, gc_ref, q_ref, gd_ref, dr_ref, di_ref, lr_ref, li_ref, hr_ref, hi_ref, qr_acc, qi_acc, gb_acc, gc_acc):
        step = pl.program_id(1)

        @pl.when(step == 0)
        def _():
            for ref in (hr_ref, hi_ref, qr_acc, qi_acc, gb_acc, gc_acc, gd_ref):
                ref[...] = jnp.zeros_like(ref)

        dys_v = _load_permuted(dys_ref)
        u = _load_permuted(u_ref)
        dys_b = dys_v.astype(MXU_DTYPE)
        u_b = u.astype(MXU_DTYPE)
        d = _dot(dys_b, cm_ref[0])
        dr_ref[...] = d[:, :WB]
        di_ref[...] = d[:, WB:]
        for c in range(WB // lc):
            lanes = slice(c * lc, (c + 1) * lc)
            hr, hi = hr_ref[:, lanes], hi_ref[:, lanes]
            q = [qr_acc[:, lanes], qi_acc[:, lanes]]
            for b in range(tm // SEG_ROWS - 1, -1, -1):
                rows = lambda j, b=b: slice(b * SEG_ROWS + j * SUBLANES, b * SEG_ROWS + (j + 1) * SUBLANES)

                def read(j, rows=rows, lanes=lanes):
                    return dr_ref[rows(j), lanes], di_ref[rows(j), lanes]

                def write(j, xr, xi, rows=rows, lanes=lanes, q=q):
                    lr_ref[rows(j), lanes] = xr
                    li_ref[rows(j), lanes] = xi
                    er = xr - dr_ref[rows(j), lanes]
                    ei = xi - di_ref[rows(j), lanes]
                    sr = sre_ref[rows(j), lanes]
                    si = sim_ref[rows(j), lanes]
                    q[0] = q[0] + (er * sr + ei * si)
                    q[1] = q[1] + (ei * sr - er * si)

                hr, hi = _scan_block(read, write, hr, hi, coef_ref, lanes, True)
            hr_ref[:, lanes] = hr
            hi_ref[:, lanes] = hi
            qr_acc[:, lanes] = q[0]
            qi_acc[:, lanes] = q[1]
        lr_b = lr_ref[...].astype(MXU_DTYPE)
        li_b = li_ref[...].astype(MXU_DTYPE)
        _store_permuted(du_ref, _dot_nt(lr_b, bm_ref[0, :, :WB]) + _dot_nt(li_b, bm_ref[0, :, WB:]) + d_ref[0] * dys_v)
        gb_acc[:, :WB] += _dot_tn(u_b, lr_b)
        gb_acc[:, WB:] += _dot_tn(u_b, li_b)
        gc_acc[:, :WB] += _dot_tn(dys_b, sre_ref[...].astype(MXU_DTYPE))
        gc_acc[:, WB:] += _dot_tn(dys_b, sim_ref[...].astype(MXU_DTYPE))
        gd_ref[0] += jnp.sum(dys_v * u, axis=0, keepdims=True)

        @pl.when(step == nt - 1)
        def _():
            q_ref[0, 0:1, :] = jnp.sum(qr_acc[...], axis=0, keepdims=True)
            q_ref[0, 1:2, :] = jnp.sum(qi_acc[...], axis=0, keepdims=True)
            mask = _group_mask(UB, WB)
            fold = (lax.broadcasted_iota(jnp.int32, (WB, STATE), 0) % STATE == lax.broadcasted_iota(jnp.int32, (WB, STATE), 1)).astype(F32)
            for acc, out in ((gb_acc, gb_ref), (gc_acc, gc_ref)):
                for k in range(2):
                    own = jnp.where(mask, acc[:, k * WB:(k + 1) * WB], 0.0)
                    out[0, k] = jnp.dot(own, fold, precision=lax.Precision.HIGHEST, preferred_element_type=F32)

    rev = lambda b, i: (nt - 1 - i, b)
    slab = lambda b, i: (b, nt - 1 - i, 0)
    blk = lambda b, i: (b, 0, 0)
    blk4 = lambda b, i: (b, 0, 0, 0)
    return _call(
        body, name="s5_bwd", grid=(N_GBLK, nt), exchanges=exchanges, semantics=("arbitrary", "arbitrary"),
        operands=(dys, u4, s_re, s_im, bmat, cmat, coef_rev, dskip),
        in_specs=[
            _slab_spec(UB, tm, slab), _slab_spec(UB, tm, slab), pl.BlockSpec((tm, WB), rev), pl.BlockSpec((tm, WB), rev),
            pl.BlockSpec((1, UB, 2 * WB), blk), pl.BlockSpec((1, UB, 2 * WB), blk),
            pl.BlockSpec((N_TABLES, SUBLANES, WB), lambda b, i: (0, 0, b)), pl.BlockSpec((1, 1, UB), blk),
        ],
        out_specs=[
            _slab_spec(UB, tm, slab), pl.BlockSpec((1, 2, UB, STATE), blk4), pl.BlockSpec((1, 2, UB, STATE), blk4),
            pl.BlockSpec((1, 2, WB), blk), pl.BlockSpec((1, 1, UB), blk),
        ],
        out_shape=[
            _slab_shape(L, D_SSM), jax.ShapeDtypeStruct((N_GBLK, 2, UB, STATE), F32),
            jax.ShapeDtypeStruct((N_GBLK, 2, UB, STATE), F32), jax.ShapeDtypeStruct((N_GBLK, 2, WB), F32),
            jax.ShapeDtypeStruct((N_GBLK, 1, UB), F32),
        ],
        scratch_shapes=[pltpu.VMEM((tm, WB), F32)] * 4 + [pltpu.VMEM((SUBLANES, WB), F32)] * 4 + [pltpu.VMEM((UB, 2 * WB), F32)] * 2,
    )


def _inproj_bwd(du, dhbc, x, dx1, w_in_all, g1):
    L, D = x.shape
    ns, _, nc = w_in_all.shape
    tm = _tile(L, TM_PROJ)

    def body(du_ref, dhbc_ref, x_ref, dx1_ref, w_ref, g_ref, gx_ref, dproj_ref, dg_ref):
        @pl.when(pl.program_id(0) == 0)
        def _():
            dg_ref[...] = jnp.zeros_like(dg_ref)

        du_b = _load_slabs(du_ref).astype(MXU_DTYPE)
        dproj_ref[:, :nc] = du_b
        dproj_ref[:, nc:] = dhbc_ref[...]
        dhn = _dot_nt(du_b, w_ref[0])
        for j in range(1, ns):
            dhn = dhn + _dot_nt(dhbc_ref[:, (j - 1) * nc:j * nc], w_ref[j])
        dx, dg = _rms_bwd(x_ref[...], g_ref[...], dhn)
        dg_ref[...] += dg
        gx_ref[...] = dx1_ref[...] + dx

    row = lambda i: (i, 0)
    return pl.pallas_call(
        body, name="inproj_bwd", grid=(L // tm,),
        in_specs=[_slab_spec(nc, tm), pl.BlockSpec((tm, (ns - 1) * nc), row), pl.BlockSpec((tm, D), row), pl.BlockSpec((tm, D), row),
                  _resident(w_in_all.shape), _resident(g1.shape)],
        out_specs=[pl.BlockSpec((tm, D), row), pl.BlockSpec((tm, ns * nc), row), pl.BlockSpec((1, D), lambda i: (0, 0))],
        out_shape=[jax.ShapeDtypeStruct((L, D), F32), jax.ShapeDtypeStruct((L, ns * nc), MXU_DTYPE), jax.ShapeDtypeStruct((1, D), F32)],
        compiler_params=_params(("arbitrary",)),
    )(du, dhbc, x, dx1, w_in_all, g1)


def _matmul_tn(a, b, name, col_shards=1, exchanges=()):
    L, K = a.shape
    N = b.shape[1]
    tl = _tile(L, TL_TN)
    tk = _tile(K, TK_TN)
    nw = N // col_shards
    spb = max(1, min(col_shards, TN_TN // nw))
    tn = spb * nw if spb > 1 else _tile(nw, TK_TN)
    npb = nw // tn if spb == 1 else 1

    def body(a_ref, b_ref, o_ref):
        @pl.when(pl.program_id(2) == 0)
        def _():
            o_ref[...] = jnp.zeros_like(o_ref)

        res = _dot_tn(a_ref[...], b_ref[...])
        for s in range(spb):
            o_ref[s] += res[:, s * nw:(s + 1) * nw] if spb > 1 else res

    out_block = pl.BlockSpec((spb, tk, nw if spb > 1 else tn), (lambda k, n, l: (n, k, 0)) if spb > 1 else (lambda k, n, l: (n // npb, k, n % npb)))
    return _call(
        body, name=name, grid=(K // tk, N // tn, L // tl), exchanges=exchanges, semantics=("arbitrary", "arbitrary", "arbitrary"),
        operands=(a, b),
        in_specs=[pl.BlockSpec((tl, tk), lambda k, n, l: (l, k)), pl.BlockSpec((tl, tn), lambda k, n, l: (l, n))],
        out_specs=[out_block],
        out_shape=[jax.ShapeDtypeStruct((col_shards, K, nw), F32)],
    )


def _ssm_discretize(lam_re, lam_im, log_dt, bt_re, bt_im):
    dt = jnp.exp(log_dt)[:, None]
    zr = lam_re * dt
    zi = lam_im * dt
    mag = jnp.exp(zr)
    abr = mag * jnp.cos(zi)
    abi = mag * jnp.sin(zi)
    nr, ni = abr - 1.0, abi
    den = lam_re * lam_re + lam_im * lam_im
    coef_r = ((nr * lam_re + ni * lam_im) / den)[:, None, :]
    coef_i = ((ni * lam_re - nr * lam_im) / den)[:, None, :]
    return zr, zi, coef_r * bt_re - coef_i * bt_im, coef_r * bt_im + coef_i * bt_re


def _scan_tables(ar, ai, reverse):
    rows = np.arange(SUBLANES)
    exps = np.zeros((N_TABLES // 2, SUBLANES), np.int32)
    keep = np.ones((N_TABLES // 2, SUBLANES), bool)
    for t, k in enumerate((1, 2, 4)):
        exps[t] = SUBLANES * k
        keep[t] = (rows + k <= SUBLANES - 1) if reverse else (rows >= k)
    exps[3] = SUBLANES * (SUBLANES - rows) if reverse else SUBLANES * (rows + 1)
    for j in range(SUBLANES):
        exps[4 + j] = SUBLANES - j if reverse else j + 1
    pr, pi = ar, (-ai if reverse else ai)
    shape = (N_TABLES // 2, SUBLANES, ar.shape[0])
    xr, xi = jnp.ones(shape, F32), jnp.zeros(shape, F32)
    for bit in range(int(exps.max()).bit_length()):
        on = ((exps >> bit) & 1).astype(bool)[:, :, None]
        xr, xi = jnp.where(on, xr * pr - xi * pi, xr), jnp.where(on, xr * pi + xi * pr, xi)
        pr, pi = pr * pr - pi * pi, 2.0 * pr * pi
    xr = jnp.where(keep[:, :, None], xr, 0.0)
    xi = jnp.where(keep[:, :, None], xi, 0.0)
    return jnp.stack([xr, xi], axis=1).reshape(N_TABLES, SUBLANES, ar.shape[0])


def _group_mask(rows, cols):
    r = lax.broadcasted_iota(jnp.int32, (rows, cols), 0) // GROUP
    c = lax.broadcasted_iota(jnp.int32, (rows, cols), 1) // STATE
    return r == c


def _ssm_expand(bt_re, bt_im, c_re, c_im, exchanges=()):
    flat = lambda a: a.reshape(N_GROUPS * GROUP, STATE)

    def body(br_ref, bi_ref, cr_ref, ci_ref, bm_ref, cm_ref):
        spread = (lax.broadcasted_iota(jnp.int32, (STATE, WB), 1) % STATE == lax.broadcasted_iota(jnp.int32, (STATE, WB), 0)).astype(F32)
        mask = _group_mask(UB, WB)

        def expand(x):
            wide = jnp.dot(x, spread, precision=lax.Precision.HIGHEST, preferred_element_type=F32)
            return jnp.where(mask, wide, 0.0).astype(MXU_DTYPE)

        bm_ref[0, :, :WB] = expand(br_ref[...])
        bm_ref[0, :, WB:] = expand(bi_ref[...])
        cm_ref[0, :, :WB] = expand(cr_ref[...])
        cm_ref[0, :, WB:] = expand(-ci_ref[...])

    spec = pl.BlockSpec((UB, STATE), lambda b: (b, 0))
    out = pl.BlockSpec((1, UB, 2 * WB), lambda b: (b, 0, 0))
    return _call(
        body, name="ssm_expand", grid=(N_GBLK,), exchanges=exchanges, semantics=("arbitrary",),
        operands=(flat(bt_re), flat(bt_im), flat(c_re), flat(c_im)), in_specs=[spec] * 4, out_specs=[out, out],
        out_shape=[jax.ShapeDtypeStruct((N_GBLK, UB, 2 * WB), MXU_DTYPE)] * 2,
    )


def _ssm_matrices(lam_re, lam_im, log_dt, bt_re, bt_im, c_re, c_im, exchanges=()):
    zr, zi, bbar_r, bbar_i = _ssm_discretize(lam_re, lam_im, log_dt, bt_re, bt_im)
    mag = jnp.exp(zr)
    ar = (mag * jnp.cos(zi)).reshape(-1)
    ai = (mag * jnp.sin(zi)).reshape(-1)
    bmat, cmat, *rest = _ssm_expand(bbar_r, bbar_i, c_re, c_im, exchanges)
    return bmat, cmat, _scan_tables(ar, ai, False), _scan_tables(ar, ai, True), rest


def _ssm_param_grads(lam_re, lam_im, log_dt, bt_re, bt_im, gb, gc, q, gd):
    part = lambda g, k: g[:, k].reshape(N_GROUPS, GROUP, STATE)
    qr = q[:, 0, :].reshape(N_GROUPS, STATE)
    qi = q[:, 1, :].reshape(N_GROUPS, STATE)
    _, vjp = jax.vjp(_ssm_discretize, lam_re, lam_im, log_dt, bt_re, bt_im)
    d_lam_re, d_lam_im, d_log_dt, d_bt_re, d_bt_im = vjp((qr, qi, part(gb, 0), part(gb, 1)))
    return d_lam_re, d_lam_im, d_log_dt, d_bt_re, d_bt_im, part(gc, 0), -part(gc, 1), gd.reshape(N_GROUPS, GROUP)


def _row_tile(rows, n):
    return _tile(rows, max(SUBLANES, (2 * 1024 * 1024) // (4 * n)))


def _pair_add(grad, other, core, name):
    ns, h, n = other.shape
    tr = _row_tile(h, n)
    nb = h // tr

    def body(c_ref, g_ref, o_ref, out_ref):
        out_ref[...] = (g_ref[...] + o_ref[...]).astype(WIRE_DTYPE)

    return pl.pallas_call(
        body, name=name,
        grid_spec=pltpu.PrefetchScalarGridSpec(
            num_scalar_prefetch=1, grid=(ns, nb),
            in_specs=[pl.BlockSpec((1, tr, n), lambda s, i, c: (s, c[0] * nb + i, 0)), pl.BlockSpec((1, tr, n), lambda s, i, c: (s, i, 0))],
            out_specs=pl.BlockSpec((1, tr, n), lambda s, i, c: (s, i, 0))),
        out_shape=jax.ShapeDtypeStruct(other.shape, WIRE_DTYPE),
        compiler_params=_params(("arbitrary", "arbitrary")),
    )(core, grad, other)


def _quad_sum(parts, core, name):
    ns, h, n = parts.shape
    tr = _row_tile(h, n)
    nb = h // tr

    def body(c_ref, p_ref, out_ref):
        p = [p_ref[k].astype(F32) for k in range(ns)]
        out_ref[...] = ((p[0] + p[1]) + p[2]) + p[3]

    return pl.pallas_call(
        body, name=name,
        grid_spec=pltpu.PrefetchScalarGridSpec(
            num_scalar_prefetch=1, grid=(nb,),
            in_specs=[pl.BlockSpec((ns, tr, n), lambda i, c: (0, i, 0))],
            out_specs=pl.BlockSpec((tr, n), lambda i, c: (c[0] * nb + i, 0))),
        out_shape=jax.ShapeDtypeStruct((2 * h, n), F32),
        compiler_params=_params(("arbitrary",)),
    )(core, parts)


def _adamw_math(w, g, m, v):
    m = ADAM_B1 * m + (1.0 - ADAM_B1) * g
    v = ADAM_B2 * v + (1.0 - ADAM_B2) * jnp.square(g)
    m_hat = m / (1.0 - ADAM_B1 ** ADAM_STEP)
    v_hat = v / (1.0 - ADAM_B2 ** ADAM_STEP)
    delta = -ADAM_LR * (m_hat / (jnp.sqrt(v_hat) + ADAM_EPS) + ADAM_WD * w)
    return delta, m, v


def _adamw(w, g, m, v, name):
    r, n = w.shape
    tr = _row_tile(r, n)

    def body(w_ref, g_ref, m_ref, v_ref, d_ref, nm_ref, nv_ref):
        d_ref[...], nm_ref[...], nv_ref[...] = _adamw_math(w_ref[...], g_ref[...], m_ref[...], v_ref[...])

    spec = pl.BlockSpec((tr, n), lambda i: (i, 0))
    return pl.pallas_call(
        body, name=name, grid=(r // tr,), in_specs=[spec] * 4, out_specs=[spec] * 3,
        out_shape=[jax.ShapeDtypeStruct((r, n), F32)] * 3,
        compiler_params=_params(("arbitrary",)),
    )(w, g, m, v)


LANES = 128
SMALL = ["g_pre_mix", "lam_re", "lam_im", "log_dt", "b_re", "b_im", "c_re", "c_im", "d_skip", "conv_w", "g_ssm_out", "g_conv_out",
         "g_post_mix", "g_pre_mlp", "g_post_mlp"]
TILE_SLOTS = {"b_re": (0, N_GROUPS), "b_im": (N_GROUPS, N_GROUPS), "c_re": (2 * N_GROUPS, N_GROUPS), "c_im": (3 * N_GROUPS, N_GROUPS),
              "lam_re": (4 * N_GROUPS, 2), "lam_im": (4 * N_GROUPS + 2, 2)}
N_TILE_SLOTS = 4 * N_GROUPS + 4
VEC_ROWS = {"g_pre_mix": 0, "g_post_mix": 1, "g_pre_mlp": 2, "g_post_mlp": 3, "g_ssm_out": 4, "g_conv_out": 5, "log_dt": 6}
ROW_LOSS, ROW_DSKIP, ROW_CONV, N_PACK_ROWS = 7, 8, 24, 32


def _kernel_form(name, a):
    if name in ("b_re", "b_im"):
        return jnp.transpose(a, (0, 1, 3, 2)).reshape(N_GROUPS, GROUP, STATE)
    if name in ("c_re", "c_im"):
        return a.reshape(N_GROUPS, GROUP, STATE)
    if name in ("lam_re", "lam_im"):
        return a.reshape(2, GROUP, STATE)
    if name == "d_skip":
        return jnp.transpose(a, (0, 2, 1)).reshape(GROUP, N_GROUPS)
    if name == "conv_w":
        return jnp.transpose(a, (1, 0, 2))
    return a


def _param_form(name, k):
    if name in ("b_re", "b_im"):
        return jnp.transpose(k.reshape(1, N_GROUPS, GROUP, STATE), (0, 1, 3, 2))
    if name in ("c_re", "c_im"):
        return k.reshape(1, N_GROUPS, GROUP, STATE)
    if name in ("lam_re", "lam_im"):
        return k.reshape(1, N_GROUPS, STATE)
    if name == "d_skip":
        return jnp.transpose(k.reshape(1, GROUP, N_GROUPS), (0, 2, 1))
    if name == "conv_w":
        return jnp.transpose(k, (1, 0, 2))
    return k


def _pack_tiles(g):
    lam = lambda a: a.reshape(2, GROUP, STATE)
    tiles = jnp.concatenate([g["b_re"], g["b_im"], g["c_re"], g["c_im"], lam(g["lam_re"]), lam(g["lam_im"])], axis=0)
    return tiles.astype(WIRE_DTYPE)


def _pack_rows(g, loss):
    row = lambda a: jnp.pad(a, ((0, 0), (0, D_MODEL - a.shape[1])))
    rows = [row(g[k][None]) for k in VEC_ROWS] + [row(loss[0:1]), row(g["d_skip"].T), row(g["conv_w"])]
    rows.append(jnp.zeros((N_PACK_ROWS - ROW_CONV - 3, D_MODEL), F32))
    return jnp.concatenate(rows, axis=0)


def _adamw_small(tiles, rows, w, m, v):
    nn = len(SMALL)

    def body(*refs):
        t_ref, r_ref = refs[0], refs[1]
        w_refs, m_refs, v_refs = refs[2:2 + nn], refs[2 + nn:2 + 2 * nn], refs[2 + 2 * nn:2 + 3 * nn]
        loss_ref, outs = refs[2 + 3 * nn], refs[3 + 3 * nn:]

        def tile_sum(first, count):
            total = t_ref[0, first:first + count].astype(F32)
            for d in range(1, N_DEV):
                total = total + t_ref[d, first:first + count].astype(F32)
            return total

        def row_sum(first, count, lanes):
            total = r_ref[0, first:first + count, 0:lanes]
            for d in range(1, N_DEV):
                total = total + r_ref[d, first:first + count, 0:lanes]
            return total

        def step(j, g, at=lambda ref: ref):
            delta, nm, nv = _adamw_math(at(w_refs[j])[...], g, at(m_refs[j])[...], at(v_refs[j])[...])
            at(outs[j])[...] = g
            at(outs[nn + j])[...] = delta
            at(outs[2 * nn + j])[...] = nm
            at(outs[3 * nn + j])[...] = nv

        loss_ref[...] = row_sum(ROW_LOSS, 1, LANES)
        chip = 2 * lax.axis_index("x") + lax.axis_index("y")
        for j, name in enumerate(SMALL):
            if name in TILE_SLOTS:
                step(j, tile_sum(*TILE_SLOTS[name]))
            elif name == "d_skip":
                step(j, row_sum(ROW_DSKIP, GROUP, N_GROUPS))
            elif name == "conv_w":
                full = row_sum(ROW_CONV, 3, D_CONV)
                mine = full[:, 0:LANES]
                for s in range(1, N_CHIPS):
                    mine = jnp.where(chip == s, full[:, s * LANES:(s + 1) * LANES], mine)
                for k in range(3):
                    step(j, mine[k:k + 1, :], at=lambda ref, k=k: ref.at[k])
            else:
                step(j, row_sum(VEC_ROWS[name], 1, w_refs[j].shape[1]))

    args = [tiles, rows] + [w[k] for k in SMALL] + [m[k] for k in SMALL] + [v[k] for k in SMALL]
    res = pl.pallas_call(
        body, name="adamw_small", in_specs=[VMEM] * len(args), out_specs=[VMEM] * (1 + 4 * nn),
        out_shape=[jax.ShapeDtypeStruct((1, LANES), F32)] + [jax.ShapeDtypeStruct(w[k].shape, F32) for k in SMALL] * 4,
        compiler_params=pltpu.CompilerParams(vmem_limit_bytes=VMEM_LIMIT),
    )(*args)
    return res[0], [dict(zip(SMALL, res[1 + q * nn:1 + (q + 1) * nn])) for q in range(4)]


WEIGHTS = ["g_pre_mix", "w_in", "lam_re", "lam_im", "log_dt", "b_re", "b_im", "c_re", "c_im", "d_skip", "w_glu", "conv_w",
           "g_ssm_out", "g_conv_out", "w_out", "g_post_mix", "g_pre_mlp", "w_up", "w_down", "g_post_mlp"]
BIG = ["w_in", "w_glu", "w_out", "w_up", "w_down"]


def kernel(x, g_pre_mix, w_in, lam_re, lam_im, log_dt, b_re, b_im, c_re, c_im, d_skip, w_glu, conv_w, g_ssm_out, g_conv_out, w_out, g_post_mix, g_pre_mlp, w_up, w_down, g_post_mlp, loss_target, m_g_pre_mix, m_w_in, m_lam_re, m_lam_im, m_log_dt, m_b_re, m_b_im, m_c_re, m_c_im, m_d_skip, m_w_glu, m_conv_w, m_g_ssm_out, m_g_conv_out, m_w_out, m_g_post_mix, m_g_pre_mlp, m_w_up, m_w_down, m_g_post_mlp, v_g_pre_mix, v_w_in, v_lam_re, v_lam_im, v_log_dt, v_b_re, v_b_im, v_c_re, v_c_im, v_d_skip, v_w_glu, v_conv_w, v_g_ssm_out, v_g_conv_out, v_w_out, v_g_post_mix, v_g_pre_mlp, v_w_up, v_w_down, v_g_post_mlp):
    w = dict(g_pre_mix=g_pre_mix, w_in=w_in, lam_re=lam_re, lam_im=lam_im, log_dt=log_dt, b_re=b_re, b_im=b_im, c_re=c_re, c_im=c_im,
             d_skip=d_skip, w_glu=w_glu, conv_w=conv_w, g_ssm_out=g_ssm_out, g_conv_out=g_conv_out, w_out=w_out, g_post_mix=g_post_mix,
             g_pre_mlp=g_pre_mlp, w_up=w_up, w_down=w_down, g_post_mlp=g_post_mlp)
    m = dict(g_pre_mix=m_g_pre_mix, w_in=m_w_in, lam_re=m_lam_re, lam_im=m_lam_im, log_dt=m_log_dt, b_re=m_b_re, b_im=m_b_im, c_re=m_c_re,
             c_im=m_c_im, d_skip=m_d_skip, w_glu=m_w_glu, conv_w=m_conv_w, g_ssm_out=m_g_ssm_out, g_conv_out=m_g_conv_out, w_out=m_w_out,
             g_post_mix=m_g_post_mix, g_pre_mlp=m_g_pre_mlp, w_up=m_w_up, w_down=m_w_down, g_post_mlp=m_g_post_mlp)
    v = dict(g_pre_mix=v_g_pre_mix, w_in=v_w_in, lam_re=v_lam_re, lam_im=v_lam_im, log_dt=v_log_dt, b_re=v_b_re, b_im=v_b_im, c_re=v_c_re,
             c_im=v_c_im, d_skip=v_d_skip, w_glu=v_w_glu, conv_w=v_conv_w, g_ssm_out=v_g_ssm_out, g_conv_out=v_g_conv_out, w_out=v_w_out,
             g_post_mix=v_g_post_mix, g_pre_mlp=v_g_pre_mlp, w_up=v_w_up, w_down=v_w_down, g_post_mlp=v_g_post_mlp)
    w_dev, m_dev, v_dev = w, m, v
    w, m, v = ({k: a[0] for k, a in d.items()} for d in (w, m, v))
    core = lax.axis_index("c").astype(jnp.int32).reshape(1)

    xs, target = x[0], loss_target[0]
    g1 = w["g_pre_mix"][None]
    g_ssm, g_conv = w["g_ssm_out"][None], w["g_conv_out"][None]
    g_post_mix, g_pre_mlp, g_post_mlp = w["g_post_mix"][None], w["g_pre_mlp"][None], w["g_post_mlp"][None]
    bt_re, bt_im = (jnp.transpose(w[k], (0, 2, 1)) for k in ("b_re", "b_im"))
    dskip = w["d_skip"].reshape(N_GBLK, 1, UB)
    shard = {k: w[k].astype(MXU_DTYPE) for k in BIG}
    conv_pad = jnp.pad(w["conv_w"], ((0, SUBLANES - 3), (0, 0)))

    bmat, cmat, coef_f, coef_r, (w_in_all,) = _ssm_matrices(
        w["lam_re"], w["lam_im"], w["log_dt"], bt_re, bt_im, w["c_re"], w["c_im"], exchanges=[_GatherForward([shard["w_in"]])])
    hn, proj, u4, w_glu_all, w_out_all, conv_all = _inproj_fwd(
        xs, g1, w_in_all, exchanges=[_Gather([shard["w_glu"], shard["w_out"], conv_pad], [False, False, False])])
    wd_half = shard["w_down"].shape[0] // 2
    wda, wdb = shard["w_down"][:wd_half], shard["w_down"][wd_half:]
    s_re, s_im, ys, w_up_all, wda_all = _s5_fwd(u4, bmat, cmat, coef_f, dskip, exchanges=[_Gather([shard["w_up"], wda], [True, True])])
    w_glu_f, w_out_f = w_glu_all.reshape(D_SSM, D_SSM), w_out_all.reshape(D_MODEL, D_MODEL)
    conv_f = jnp.transpose(conv_all, (1, 0, 2)).reshape(SUBLANES, D_CONV)
    ycat, o, x1, w_up_all, wda_all, wdb_all = _tail_fwd(xs, ys, proj, w_glu_f, conv_f, g_ssm, g_conv, w_out_f, g_post_mix,
                                                        exchanges=[_Forward([w_up_all, wda_all]), _GatherForward([wdb])])
    hn2, up, act, m_act, dx2, loss = _mlp_fwd(x1, target, w_up_all, wda_all, wdb_all, g_pre_mlp, g_post_mlp)

    dm, dup, dx1, dg_post_mlp, dg_pre_mlp = _mlp_bwd(dx2, m_act, up, x1, w_up_all, wda_all, wdb_all, g_pre_mlp, g_post_mlp)
    gw_down = _matmul_tn(act, dm, "dw_down")[0].reshape(N_CHIPS, D_FF // N_CHIPS, D_MODEL)
    gw_up = _matmul_tn(hn2, dup, "dw_up", col_shards=N_CHIPS)[0]
    do, da, y1, dys, dhbc, dg_post_mix, dg_ssm, dg_conv, dconv_w, o_down, o_up = _tail_bwd(
        dx1, o, ys, proj, w_glu_f, conv_f, g_ssm, g_conv, w_out_f, g_post_mix, exchanges=[_Pair([gw_down, gw_up])])
    p_down = _pair_add(gw_down, o_down, core, "pair_add_w_down")
    p_up = _pair_add(gw_up, o_up, core, "pair_add_w_up")
    gw_out = _matmul_tn(ycat, do, "dw_out")[0].reshape(N_CHIPS, D_MODEL // N_CHIPS, D_MODEL)
    gw_glu = _matmul_tn(y1, da, "dw_glu")[0].reshape(N_CHIPS, D_SSM // N_CHIPS, D_SSM)
    du, gb, gc, q, gd, q_down, q_up, o_out, o_glu = _s5_bwd(
        dys, u4, s_re, s_im, bmat, cmat, coef_r, dskip, exchanges=[_Chip([p_down, p_up]), _Pair([gw_out, gw_glu])])
    h_down = _quad_sum(q_down, core, "quad_sum_w_down")
    h_up = _quad_sum(q_up, core, "quad_sum_w_up")
    p_out = _pair_add(gw_out, o_out, core, "pair_add_w_out")
    p_glu = _pair_add(gw_glu, o_glu, core, "pair_add_w_glu")
    grad_x, dproj, dg_pre_mix = _inproj_bwd(du, dhbc, xs, dx1, w_in_all, g1)
    d_lam_re, d_lam_im, d_log_dt, d_b_re, d_b_im, d_c_re, d_c_im, d_d_skip = _ssm_param_grads(
        w["lam_re"], w["lam_im"], w["log_dt"], bt_re, bt_im, gb, gc, q, gd)
    small = {
        "g_pre_mix": dg_pre_mix[0], "lam_re": d_lam_re, "lam_im": d_lam_im, "log_dt": d_log_dt, "b_re": d_b_re, "b_im": d_b_im,
        "c_re": d_c_re, "c_im": d_c_im, "d_skip": d_d_skip, "conv_w": dconv_w[:3], "g_ssm_out": dg_ssm[0], "g_conv_out": dg_conv[0],
        "g_post_mix": dg_post_mix[0], "g_pre_mlp": dg_pre_mlp[0], "g_post_mlp": dg_post_mlp[0],
    }
    gw_in, g_down, g_up, q_out, q_glu, tiles, rows = _matmul_tn(
        hn, dproj, "dw_in", col_shards=N_CHIPS,
        exchanges=[_Share([h_down, h_up]), _Chip([p_out, p_glu]), _GatherSmall(_pack_tiles(small)), _GatherSmall(_pack_rows(small, loss))])
    h_out = _quad_sum(q_out, core, "quad_sum_w_out")
    h_glu = _quad_sum(q_glu, core, "quad_sum_w_glu")
    (o_in,) = _run_exchanges([_Pair([gw_in])], "rs_pair_w_in")
    p_in = _pair_add(gw_in, o_in, core, "pair_add_w_in")
    q_in, g_out, g_glu = _run_exchanges([_Chip([p_in]), _Share([h_out, h_glu])], "rs_chip_w_in")
    h_in = _quad_sum(q_in, core, "quad_sum_w_in")
    (g_in,) = _run_exchanges([_Share([h_in])], "rs_share_w_in")
    shard_grads = {"w_in": g_in, "w_glu": g_glu, "w_out": g_out, "w_up": g_up, "w_down": g_down}

    out = {q: {} for q in ("grad", "delta", "new_m", "new_v")}
    for k in BIG:
        out["grad"][k] = shard_grads[k][None]
        delta, new_m, new_v = _adamw(w[k], shard_grads[k], m[k], v[k], "adamw_" + k)
        out["delta"][k], out["new_m"][k], out["new_v"][k] = delta[None], new_m[None], new_v[None]
    form = lambda d: {k: _kernel_form(k, d[k]) for k in SMALL}
    loss, res = _adamw_small(tiles, rows, form(w_dev), form(m_dev), form(v_dev))
    for q, d in zip(("grad", "delta", "new_m", "new_v"), res):
        out[q].update({k: _param_form(k, d[k]) for k in SMALL})
    flat = [loss[0, 0], grad_x[None]]
    for q in ("grad", "delta", "new_m", "new_v"):
        flat += [out[q][k] for k in WEIGHTS]
    return tuple(flat)
```

```python
import functools
import math

import jax
import jax.numpy as jnp
import numpy as np
from jax import lax
from jax.experimental import pallas as pl
from jax.experimental.pallas import tpu as pltpu

F32 = jnp.float32
MXU_DTYPE = jnp.bfloat16
WIRE_DTYPE = jnp.bfloat16

D_MODEL = 1024
D_SSM = 512
D_CONV = 512
N_GROUPS = 32
GROUP = 16
STATE = 64
D_FF = 4096
RMS_EPS = 1e-6
N_CHIPS = 4
N_DEV = 8

ADAM_LR = 0.001
ADAM_B1 = 0.9
ADAM_B2 = 0.999
ADAM_EPS = 1e-08
ADAM_WD = 0.01
ADAM_STEP = 10

N_GBLK = 2
G_PER_BLK = N_GROUPS // N_GBLK
UB = G_PER_BLK * GROUP
WB = G_PER_BLK * STATE
LANE_CHUNK = 256
SUBLANES = 8
N_TABLES = 24
DIAG_COLS = 256
DIAG_ROWS = DIAG_COLS // STATE * GROUP

TM_PROJ = 512
TM_S5 = 512
TM_TAIL = 512
TM_MLP = 512
TL_TN = 2048
TK_TN = 1024
TN_TN = 2048
VMEM_LIMIT = 56 * 1024 * 1024

MESH = pl.DeviceIdType.MESH


def _params(sem, vmem=VMEM_LIMIT):
    return pltpu.CompilerParams(dimension_semantics=sem, vmem_limit_bytes=vmem)


def _resident(shape):
    nd = len(shape)
    return pl.BlockSpec(shape, lambda *_: (0,) * nd, pipeline_mode=pl.Buffered(1))


def _dot(a, b):
    return jnp.dot(a, b, preferred_element_type=F32)


def _dot_nt(a, b):
    return lax.dot_general(a, b, (((1,), (1,)), ((), ())), preferred_element_type=F32)


def _dot_tn(a, b):
    return lax.dot_general(a, b, (((0,), (0,)), ((), ())), preferred_element_type=F32)


def _rms_fwd(x, g):
    r = lax.rsqrt(jnp.mean(x * x, axis=-1, keepdims=True) + RMS_EPS)
    return x * r * g


def _rms_bwd(x, g, dy):
    r = lax.rsqrt(jnp.mean(x * x, axis=-1, keepdims=True) + RMS_EPS)
    xn = x * r
    q = dy * g
    dx = r * (q - xn * jnp.mean(q * xn, axis=-1, keepdims=True))
    return dx, jnp.sum(dy * xn, axis=0, keepdims=True)


_GELU_C = math.sqrt(2.0 / math.pi)


def _gelu(x):
    t = jnp.tanh(_GELU_C * (x + 0.044715 * (x * x * x)))
    y = x * (0.5 * (1.0 + t))
    dy = 0.5 * (1.0 + t) + 0.5 * x * (1.0 - t * t) * (_GELU_C * (1.0 + 3 * 0.044715 * (x * x)))
    return y, dy


def _tile(n, pref):
    t = min(n, pref)
    assert n % t == 0, (n, t)
    return t


HBM = pl.BlockSpec(memory_space=pltpu.HBM)
VMEM = pl.BlockSpec(memory_space=pltpu.VMEM)
DMA_SEMS = pltpu.SemaphoreType.DMA


def _place():
    x, y, c = lax.axis_index("x"), lax.axis_index("y"), lax.axis_index("c")
    chips = [(1 - x, y), (x, 1 - y), (1 - x, 1 - y)]
    return (x, y, c), 2 * x + y, (x, y, 1 - c), chips, [2 * px + py for px, py in chips]


def _remote(src, dst, send_sem, recv_sem, device):
    return pltpu.make_async_remote_copy(src_ref=src, dst_ref=dst, send_sem=send_sem, recv_sem=recv_sem,
                                        device_id=device, device_id_type=MESH)


def _half(rows, c):
    return pl.ds(c * (rows // 2), rows // 2)


class _Exchange:
    aliases = {}

    def start(self, ins, outs, sems):
        local, outgoing, _ = self._copies(ins, outs, sems)
        for cp in local + outgoing:
            cp.start()

    def finish(self, ins, outs, sems):
        local, outgoing, incoming = self._copies(ins, outs, sems)
        for cp in incoming:
            cp.wait_recv()
        for cp in outgoing:
            cp.wait_send()
        for cp in local:
            cp.wait()


class _Gather(_Exchange):
    def __init__(self, shards, split):
        self.inputs, self.split = list(shards), split
        self.out_shape = [jax.ShapeDtypeStruct((N_CHIPS, *a.shape), a.dtype) for a in shards]
        self.sems = [DMA_SEMS((len(shards), 3)), DMA_SEMS((len(shards), 3)), DMA_SEMS((len(shards),))]

    def _copies(self, ins, outs, sems):
        send, recv, lsem = sems
        (x, y, c), me, sibling, chips, ids = _place()
        local = [pltpu.make_async_copy(ins[t], outs[t].at[me], lsem.at[t]) for t in range(len(ins))]
        outgoing, incoming = [], []
        for t, a in enumerate(self.inputs):
            rows = _half(a.shape[0], c) if self.split[t] else pl.ds(0, a.shape[0])
            for k in range(3):
                to = (*chips[k], c)
                outgoing.append(_remote(ins[t].at[rows, :], outs[t].at[me, rows, :], send.at[t, k], recv.at[t, k], to))
                incoming.append(_remote(ins[t].at[rows, :], outs[t].at[ids[k], rows, :], send.at[t, k], recv.at[t, k], to))
        return local, outgoing, incoming


class _Forward(_Exchange):
    def __init__(self, arrays):
        self.inputs = list(arrays)
        self.out_shape = [jax.ShapeDtypeStruct(a.shape, a.dtype) for a in arrays]
        self.aliases = {t: t for t in range(len(arrays))}
        self.sems = [DMA_SEMS((len(arrays), 3)), DMA_SEMS((len(arrays), 3))]

    def _copies(self, ins, outs, sems):
        send, recv = sems
        (x, y, c), me, sibling, chips, ids = _place()
        outgoing, incoming = [], []
        for t, a in enumerate(self.inputs):
            for k in range(3):
                mine = outs[t].at[ids[k], _half(a.shape[1], c), :]
                theirs = outs[t].at[ids[k], _half(a.shape[1], 1 - c), :]
                outgoing.append(_remote(mine, mine, send.at[t, k], recv.at[t, k], sibling))
                incoming.append(_remote(theirs, theirs, send.at[t, k], recv.at[t, k], sibling))
        return [], outgoing, incoming


class _GatherForward(_Exchange):
    def __init__(self, shards):
        self.gather = _Gather(shards, [True] * len(shards))
        self.forward = _Forward(self.gather.out_shape)
        self.inputs, self.out_shape = self.gather.inputs, self.gather.out_shape
        self.sems = self.gather.sems + self.forward.sems

    def start(self, ins, outs, sems):
        self.gather.start(ins, outs, sems[:3])

    def finish(self, ins, outs, sems):
        local, outgoing, incoming = self.gather._copies(ins, outs, sems[:3])
        _, passed, from_sibling = self.forward._copies(outs, outs, sems[3:])
        for landed, onward in zip(incoming, passed):
            landed.wait_recv()
            onward.start()
        for cp in from_sibling:
            cp.wait_recv()
        for cp in outgoing + passed:
            cp.wait_send()
        for cp in local:
            cp.wait()


class _Pair(_Exchange):
    def __init__(self, grads):
        self.inputs = list(grads)
        self.out_shape = [jax.ShapeDtypeStruct((g.shape[0], g.shape[1] // 2, g.shape[2]), g.dtype) for g in grads]
        self.sems = [DMA_SEMS((len(grads),)), DMA_SEMS((len(grads),))]

    def _copies(self, ins, outs, sems):
        send, recv = sems
        (x, y, c), me, sibling, chips, ids = _place()
        cps = [_remote(ins[t].at[:, _half(g.shape[1], 1 - c), :], outs[t], send.at[t], recv.at[t], sibling)
               for t, g in enumerate(self.inputs)]
        return [], cps, cps


class _Chip(_Exchange):
    def __init__(self, parts):
        self.inputs = list(parts)
        self.out_shape = [jax.ShapeDtypeStruct(p.shape, p.dtype) for p in parts]
        self.sems = [DMA_SEMS((len(parts), 3)), DMA_SEMS((len(parts), 3)), DMA_SEMS((len(parts),))]

    def _copies(self, ins, outs, sems):
        send, recv, lsem = sems
        (x, y, c), me, sibling, chips, ids = _place()
        local = [pltpu.make_async_copy(ins[t].at[me], outs[t].at[me], lsem.at[t]) for t in range(len(ins))]
        outgoing, incoming = [], []
        for t in range(len(ins)):
            for k in range(3):
                to = (*chips[k], c)
                outgoing.append(_remote(ins[t].at[ids[k]], outs[t].at[me], send.at[t, k], recv.at[t, k], to))
                incoming.append(_remote(ins[t].at[ids[k]], outs[t].at[ids[k]], send.at[t, k], recv.at[t, k], to))
        return local, outgoing, incoming


class _Share(_Exchange):
    def __init__(self, grads):
        self.inputs = list(grads)
        self.out_shape = [jax.ShapeDtypeStruct(g.shape, g.dtype) for g in grads]
        self.aliases = {t: t for t in range(len(grads))}
        self.sems = [DMA_SEMS((len(grads),)), DMA_SEMS((len(grads),))]

    def _copies(self, ins, outs, sems):
        send, recv = sems
        (x, y, c), me, sibling, chips, ids = _place()
        outgoing, incoming = [], []
        for t, g in enumerate(self.inputs):
            mine = outs[t].at[_half(g.shape[0], c), :]
            theirs = outs[t].at[_half(g.shape[0], 1 - c), :]
            outgoing.append(_remote(mine, mine, send.at[t], recv.at[t], sibling))
            incoming.append(_remote(theirs, theirs, send.at[t], recv.at[t], sibling))
        return [], outgoing, incoming


class _GatherSmall(_Exchange):
    def __init__(self, block):
        self.inputs = [block]
        self.out_shape = [jax.ShapeDtypeStruct((N_DEV, *block.shape), block.dtype)]
        self.sems = [DMA_SEMS((7,)), DMA_SEMS((7,)), DMA_SEMS(())]

    def _copies(self, ins, outs, sems):
        send, recv, lsem = sems
        (x, y, c), me, sibling, chips, ids = _place()
        slot = lambda px, py, pc: outs[0].at[4 * px + 2 * py + pc]

        def copy(k, block, to, src=None):
            return _remote(slot(*block) if src is None else src, slot(*block), send.at[k], recv.at[k], to)

        local = [pltpu.make_async_copy(ins[0], slot(x, y, c), lsem)]
        first = [copy(0, (x, y, c), sibling, src=ins[0])] + [copy(1 + j, (x, y, c), (*chip, c), src=ins[0]) for j, chip in enumerate(chips)]
        passed = [copy(4 + j, (*chip, c), sibling) for j, chip in enumerate(chips)]
        landed = [copy(1 + j, (*chip, c), (x, y, c)) for j, chip in enumerate(chips)]
        from_sibling = [copy(0, (x, y, 1 - c), (x, y, c))] + [copy(4 + j, (*chip, 1 - c), (x, y, c)) for j, chip in enumerate(chips)]
        return local, first, (passed, landed, from_sibling)

    def finish(self, ins, outs, sems):
        local, first, (passed, landed, from_sibling) = self._copies(ins, outs, sems)
        for j in range(3):
            landed[j].wait_recv()
            passed[j].start()
        for cp in from_sibling:
            cp.wait_recv()
        for cp in first + passed:
            cp.wait_send()
        for cp in local:
            cp.wait()


def _split_refs(refs, counts):
    out = []
    for n in counts:
        out.append(refs[:n])
        refs = refs[n:]
    return out


def _each_exchange(exchanges, method, x_in, x_out, x_sem):
    for ex in exchanges:
        ni, no, ns = len(ex.inputs), len(ex.out_shape), len(ex.sems)
        getattr(ex, method)(x_in[:ni], x_out[:no], x_sem[:ns])
        x_in, x_out, x_sem = x_in[ni:], x_out[no:], x_sem[ns:]


def _call(body, *, name, grid, in_specs, out_specs, out_shape, operands, semantics, scratch_shapes=(), exchanges=()):
    x_in = [a for ex in exchanges for a in ex.inputs]
    x_out = [s for ex in exchanges for s in ex.out_shape]
    x_sem = [s for ex in exchanges for s in ex.sems]
    counts = (len(in_specs), len(x_in), len(out_specs), len(x_out), len(scratch_shapes), len(x_sem))
    aliases, i0, o0 = {}, len(in_specs), len(out_specs)
    for ex in exchanges:
        aliases.update({i0 + i: o0 + o for i, o in ex.aliases.items()})
        i0, o0 = i0 + len(ex.inputs), o0 + len(ex.out_shape)

    def full_body(*refs):
        ins, xi, outs, xo, scr, xs = _split_refs(list(refs), counts)
        if exchanges:
            @pl.when(functools.reduce(jnp.logical_and, [pl.program_id(a) == 0 for a in range(len(grid))]))
            def _():
                _each_exchange(exchanges, "start", xi, xo, xs)

        body(*ins, *outs, *scr)
        if exchanges:
            @pl.when(functools.reduce(jnp.logical_and, [pl.program_id(a) == grid[a] - 1 for a in range(len(grid))]))
            def _():
                _each_exchange(exchanges, "finish", xi, xo, xs)

    return pl.pallas_call(
        full_body, name=name, grid=grid,
        in_specs=list(in_specs) + [HBM] * len(x_in), out_specs=list(out_specs) + [HBM] * len(x_out),
        out_shape=list(out_shape) + x_out, scratch_shapes=list(scratch_shapes) + x_sem,
        input_output_aliases=aliases, compiler_params=_params(semantics),
    )(*operands, *x_in)


def _run_exchanges(exchanges, name):
    x_in = [a for ex in exchanges for a in ex.inputs]
    x_out = [s for ex in exchanges for s in ex.out_shape]
    x_sem = [s for ex in exchanges for s in ex.sems]
    aliases, i0, o0 = {}, 0, 0
    for ex in exchanges:
        aliases.update({i0 + i: o0 + o for i, o in ex.aliases.items()})
        i0, o0 = i0 + len(ex.inputs), o0 + len(ex.out_shape)

    def body(*refs):
        xi, xo, xs = _split_refs(list(refs), (len(x_in), len(x_out), len(x_sem)))
        _each_exchange(exchanges, "start", xi, xo, xs)
        _each_exchange(exchanges, "finish", xi, xo, xs)

    return pl.pallas_call(
        body, name=name, in_specs=[HBM] * len(x_in), out_specs=[HBM] * len(x_out), out_shape=x_out,
        scratch_shapes=x_sem, input_output_aliases=aliases,
    )(*x_in)


def _inproj_fwd(x, g1, w_in_all, exchanges=()):
    L, D = x.shape
    ns, _, nc = w_in_all.shape
    tm = _tile(L, TM_PROJ)

    def body(x_ref, g_ref, w_ref, hn_ref, proj_ref, u_ref):
        hn = _rms_fwd(x_ref[...], g_ref[...]).astype(MXU_DTYPE)
        hn_ref[...] = hn
        for j in range(ns):
            proj_ref[:, j * nc:(j + 1) * nc] = _dot(hn, w_ref[j])
        _store_slabs(u_ref, proj_ref[:, 0:nc])

    return _call(
        body, name="inproj_fwd", grid=(L // tm,), exchanges=exchanges, semantics=("arbitrary",), operands=(x, g1, w_in_all),
        in_specs=[pl.BlockSpec((tm, D), lambda i: (i, 0)), _resident((1, D)), _resident(w_in_all.shape)],
        out_specs=[pl.BlockSpec((tm, D), lambda i: (i, 0)), pl.BlockSpec((tm, ns * nc), lambda i: (i, 0)), _slab_spec(nc, tm)],
        out_shape=[jax.ShapeDtypeStruct((L, D), MXU_DTYPE), jax.ShapeDtypeStruct((L, ns * nc), F32), _slab_shape(L, nc)],
    )


def _slab_shape(L, n):
    return jax.ShapeDtypeStruct((n // LANES, L, LANES), F32)


def _slab_spec(n, tm, index=lambda i: (0, i, 0)):
    return pl.BlockSpec((n // LANES, tm, LANES), index)


def _store_slabs(ref, value):
    for k in range(ref.shape[0]):
        ref[k] = value[:, k * LANES:(k + 1) * LANES]


def _load_slabs(ref):
    return jnp.concatenate([ref[k] for k in range(ref.shape[0])], axis=1)


SEG_ROWS = SUBLANES * SUBLANES


def _load_permuted(ref):
    tm = ref.shape[1]
    slabs = []
    for k in range(ref.shape[0]):
        tiles = [ref.at[k][pl.ds(b * SEG_ROWS + j, SUBLANES, stride=SUBLANES), :] for b in range(tm // SEG_ROWS) for j in range(SUBLANES)]
        slabs.append(jnp.concatenate(tiles, axis=0))
    return jnp.concatenate(slabs, axis=1)


def _store_permuted(ref, value):
    tm = ref.shape[1]
    for k in range(ref.shape[0]):
        for b in range(tm // SEG_ROWS):
            for j in range(SUBLANES):
                r = b * SEG_ROWS + j * SUBLANES
                ref.at[k][pl.ds(b * SEG_ROWS + j, SUBLANES, stride=SUBLANES), :] = value[r:r + SUBLANES, k * LANES:(k + 1) * LANES]


def _scan_tile(xr, xi, hr, hi, coef_ref, lanes, reverse):
    for k, j in ((1, 0), (2, 2), (4, 4)):
        ar = coef_ref[j, :, lanes]
        ai = coef_ref[j + 1, :, lanes]
        shift = SUBLANES - k if reverse else k
        sr = pltpu.roll(xr, shift, 0)
        si = pltpu.roll(xi, shift, 0)
        xr, xi = xr + (ar * sr - ai * si), xi + (ar * si + ai * sr)
    pr = coef_ref[6, :, lanes]
    pi = coef_ref[7, :, lanes]
    return xr + (pr * hr - pi * hi), xi + (pr * hi + pi * hr)


def _scan_block(read, write, hr, hi, coef_ref, lanes, reverse):
    order = list(range(SUBLANES - 1, -1, -1) if reverse else range(SUBLANES))
    near = 8 + 2 * order[0]
    ar = coef_ref[near, :, lanes]
    ai = coef_ref[near + 1, :, lanes]
    xr, xi = read(order[0])
    local = {order[0]: (xr, xi)}
    for j in order[1:]:
        br, bi = read(j)
        xr, xi = br + (ar * xr - ai * xi), bi + (ar * xi + ai * xr)
        local[j] = (xr, xi)
    er, ei = _scan_tile(xr, xi, hr, hi, coef_ref, lanes, reverse)
    edge = lax.broadcasted_iota(jnp.int32, er.shape, 0) == (SUBLANES - 1 if reverse else 0)
    shift = SUBLANES - 1 if reverse else 1
    pr = jnp.where(edge, hr, pltpu.roll(er, shift, 0))
    pi = jnp.where(edge, hi, pltpu.roll(ei, shift, 0))
    for j in range(SUBLANES):
        cr = coef_ref[8 + 2 * j, :, lanes]
        ci = coef_ref[9 + 2 * j, :, lanes]
        xr, xi = local[j]
        write(j, xr + (cr * pr - ci * pi), xi + (cr * pi + ci * pr))
    end = 0 if reverse else SUBLANES - 1
    return jnp.broadcast_to(er[end:end + 1, :], er.shape), jnp.broadcast_to(ei[end:end + 1, :], ei.shape)


def _s5_fwd(u4, bmat, cmat, coef, dskip, exchanges=()):
    L = u4.shape[1]
    tm = _tile(L, TM_S5)
    lc = min(LANE_CHUNK, WB)

    def body(u_ref, bm_ref, cm_ref, coef_ref, d_ref, sre_ref, sim_ref, ys_ref, hr_ref, hi_ref):
        @pl.when(pl.program_id(1) == 0)
        def _():
            hr_ref[...] = jnp.zeros_like(hr_ref)
            hi_ref[...] = jnp.zeros_like(hi_ref)

        u = _load_permuted(u_ref)
        bu = _dot(u.astype(MXU_DTYPE), bm_ref[0])
        sre_ref[...] = bu[:, :WB]
        sim_ref[...] = bu[:, WB:]
        for c in range(WB // lc):
            lanes = slice(c * lc, (c + 1) * lc)
            hr, hi = hr_ref[:, lanes], hi_ref[:, lanes]
            for b in range(tm // SEG_ROWS):
                rows = lambda j, b=b: slice(b * SEG_ROWS + j * SUBLANES, b * SEG_ROWS + (j + 1) * SUBLANES)

                def read(j, rows=rows, lanes=lanes):
                    return sre_ref[rows(j), lanes], sim_ref[rows(j), lanes]

                def write(j, xr, xi, rows=rows, lanes=lanes):
                    sre_ref[rows(j), lanes] = xr
                    sim_ref[rows(j), lanes] = xi

                hr, hi = _scan_block(read, write, hr, hi, coef_ref, lanes, False)
            hr_ref[:, lanes] = hr
            hi_ref[:, lanes] = hi
        ys = _dot_nt(sre_ref[...].astype(MXU_DTYPE), cm_ref[0, :, :WB]) + _dot_nt(sim_ref[...].astype(MXU_DTYPE), cm_ref[0, :, WB:])
        _store_permuted(ys_ref, ys + d_ref[0] * u)

    return _call(
        body, name="s5_fwd", grid=(N_GBLK, L // tm), exchanges=exchanges, semantics=("arbitrary", "arbitrary"),
        operands=(u4, bmat, cmat, coef, dskip),
        in_specs=[
            _slab_spec(UB, tm, lambda b, i: (b, i, 0)),
            pl.BlockSpec((1, UB, 2 * WB), lambda b, i: (b, 0, 0)),
            pl.BlockSpec((1, UB, 2 * WB), lambda b, i: (b, 0, 0)),
            pl.BlockSpec((N_TABLES, SUBLANES, WB), lambda b, i: (0, 0, b)),
            pl.BlockSpec((1, 1, UB), lambda b, i: (b, 0, 0)),
        ],
        out_specs=[
            pl.BlockSpec((tm, WB), lambda b, i: (i, b)),
            pl.BlockSpec((tm, WB), lambda b, i: (i, b)),
            _slab_spec(UB, tm, lambda b, i: (b, i, 0)),
        ],
        out_shape=[
            jax.ShapeDtypeStruct((L, N_GBLK * WB), F32),
            jax.ShapeDtypeStruct((L, N_GBLK * WB), F32),
            _slab_shape(L, D_SSM),
        ],
        scratch_shapes=[pltpu.VMEM((SUBLANES, WB), F32), pltpu.VMEM((SUBLANES, WB), F32)],
    )


def _tail_fwd(x, ys, proj, w_glu, conv_w, g_ssm, g_conv, w_out, g_post, exchanges=()):
    L, D = x.shape
    tm = _tile(L, TM_TAIL)

    def body(x_ref, ys_ref, h_ref, bg_ref, cg_ref, wglu_ref, cw_ref, gs_ref, gc_ref, wout_ref, gp_ref,
             ycat_ref, o_ref, x1_ref, zbuf):
        @pl.when(pl.program_id(0) == 0)
        def _():
            zbuf[0:SUBLANES, :] = jnp.zeros((SUBLANES, D_CONV), F32)

        y1, _ = _gelu(_load_slabs(ys_ref))
        y2 = y1 * jax.nn.sigmoid(_dot(y1.astype(MXU_DTYPE), wglu_ref[...]))
        ycat_ref[:, :D_SSM] = _rms_fwd(y2, gs_ref[...]).astype(MXU_DTYPE)
        z = cg_ref[...] * h_ref[...]
        zbuf[SUBLANES:, :] = z
        conv = cw_ref[0:1, :] * zbuf[SUBLANES - 2:SUBLANES - 2 + tm, :] + cw_ref[1:2, :] * zbuf[SUBLANES - 1:SUBLANES - 1 + tm, :] + cw_ref[2:3, :] * z
        zbuf[0:SUBLANES, :] = zbuf[tm:tm + SUBLANES, :]
        ycat_ref[:, D_SSM:] = _rms_fwd(bg_ref[...] * conv, gc_ref[...]).astype(MXU_DTYPE)
        o = _dot(ycat_ref[...], wout_ref[...])
        o_ref[...] = o
        x1_ref[...] = x_ref[...] + _rms_fwd(o, gp_ref[...])

    row = lambda i: (i, 0)
    return _call(
        body, name="tail_fwd", grid=(L // tm,), exchanges=exchanges, semantics=("arbitrary",),
        operands=(x, ys, proj, proj, proj, w_glu, conv_w, g_ssm, g_conv, w_out, g_post),
        in_specs=[
            pl.BlockSpec((tm, D), row), _slab_spec(D_SSM, tm),
            pl.BlockSpec((tm, D_CONV), lambda i: (i, 1)), pl.BlockSpec((tm, D_CONV), lambda i: (i, 2)),
            pl.BlockSpec((tm, D_CONV), lambda i: (i, 3)),
            _resident(w_glu.shape), _resident(conv_w.shape), _resident(g_ssm.shape), _resident(g_conv.shape),
            _resident(w_out.shape), _resident(g_post.shape),
        ],
        out_specs=[pl.BlockSpec((tm, D), row), pl.BlockSpec((tm, D), row), pl.BlockSpec((tm, D), row)],
        out_shape=[jax.ShapeDtypeStruct((L, D), MXU_DTYPE), jax.ShapeDtypeStruct((L, D), F32), jax.ShapeDtypeStruct((L, D), F32)],
        scratch_shapes=[pltpu.VMEM((tm + SUBLANES, D_CONV), F32)],
    )


def _mlp_fwd(x1, target, w_up_all, w_down_a, w_down_b, g_pre, g_post):
    L, D = x1.shape
    ns, _, fc = w_up_all.shape
    half = w_down_a.shape[1]
    tm = _tile(L, TM_MLP)

    def body(x1_ref, t_ref, wup_ref, wda_ref, wdb_ref, gpre_ref, gpost_ref, hn2_ref, up_ref, act_ref, m_ref, dx2_ref, loss_ref):
        @pl.when(pl.program_id(0) == 0)
        def _():
            loss_ref[...] = jnp.zeros_like(loss_ref)

        x1v = x1_ref[...]
        hn2 = _rms_fwd(x1v, gpre_ref[...]).astype(MXU_DTYPE)
        hn2_ref[...] = hn2
        m = jnp.zeros((tm, D), F32)
        for j in range(ns):
            up = _dot(hn2, wup_ref[j])
            up_ref[:, j * fc:(j + 1) * fc] = up.astype(MXU_DTYPE)
            act = jnp.square(jnp.maximum(up, 0.0)).astype(MXU_DTYPE)
            act_ref[:, j * fc:(j + 1) * fc] = act
            m = m + _dot(act[:, :half], wda_ref[j]) + _dot(act[:, half:], wdb_ref[j])
        m_ref[...] = m
        err = x1v + _rms_fwd(m, gpost_ref[...]) - t_ref[...]
        loss_ref[...] += 0.5 * jnp.sum(jnp.mean(err * err, axis=-1, keepdims=True))
        dx2_ref[...] = err * (1.0 / D)

    row = lambda i: (i, 0)
    return pl.pallas_call(
        body, name="mlp_fwd", grid=(L // tm,),
        in_specs=[pl.BlockSpec((tm, D), row), pl.BlockSpec((tm, D), row), _resident(w_up_all.shape), _resident(w_down_a.shape),
                  _resident(w_down_b.shape), _resident(g_pre.shape), _resident(g_post.shape)],
        out_specs=[pl.BlockSpec((tm, D), row), pl.BlockSpec((tm, ns * fc), row), pl.BlockSpec((tm, ns * fc), row), pl.BlockSpec((tm, D), row),
                   pl.BlockSpec((tm, D), row), pl.BlockSpec((SUBLANES, LANES), lambda i: (0, 0))],
        out_shape=[jax.ShapeDtypeStruct((L, D), MXU_DTYPE), jax.ShapeDtypeStruct((L, ns * fc), MXU_DTYPE), jax.ShapeDtypeStruct((L, ns * fc), MXU_DTYPE),
                   jax.ShapeDtypeStruct((L, D), F32), jax.ShapeDtypeStruct((L, D), F32), jax.ShapeDtypeStruct((SUBLANES, LANES), F32)],
        compiler_params=_params(("arbitrary",)),
    )(x1, target, w_up_all, w_down_a, w_down_b, g_pre, g_post)


def _mlp_bwd(dx2, m, up, x1, w_up_all, w_down_a, w_down_b, g_pre, g_post):
    L, D = x1.shape
    ns, _, fc = w_up_all.shape
    tm = _tile(L, TM_MLP)

    def body(dx2_ref, m_ref, up_ref, x1_ref, wup_ref, wda_ref, wdb_ref, gpre_ref, gpost_ref,
             dm_ref, dup_ref, dx1_ref, dgpost_ref, dgpre_ref):
        @pl.when(pl.program_id(0) == 0)
        def _():
            dgpost_ref[...] = jnp.zeros_like(dgpost_ref)
            dgpre_ref[...] = jnp.zeros_like(dgpre_ref)

        dx2v = dx2_ref[...]
        dm, dg = _rms_bwd(m_ref[...], gpost_ref[...], dx2v)
        dgpost_ref[...] += dg
        dm_b = dm.astype(MXU_DTYPE)
        dm_ref[...] = dm_b
        dhn2 = jnp.zeros((tm, D), F32)
        for j in range(ns):
            cols = slice(j * fc, (j + 1) * fc)
            relu = jnp.maximum(up_ref[:, cols].astype(F32), 0.0)
            dact = jnp.concatenate([_dot_nt(dm_b, wda_ref[j]), _dot_nt(dm_b, wdb_ref[j])], axis=1)
            dup = (dact * (2.0 * relu)).astype(MXU_DTYPE)
            dup_ref[:, cols] = dup
            dhn2 = dhn2 + _dot_nt(dup, wup_ref[j])
        dx, dg = _rms_bwd(x1_ref[...], gpre_ref[...], dhn2)
        dgpre_ref[...] += dg
        dx1_ref[...] = dx2v + dx

    row = lambda i: (i, 0)
    vec = pl.BlockSpec((1, D), lambda i: (0, 0))
    return pl.pallas_call(
        body, name="mlp_bwd", grid=(L // tm,),
        in_specs=[pl.BlockSpec((tm, D), row), pl.BlockSpec((tm, D), row), pl.BlockSpec((tm, ns * fc), row), pl.BlockSpec((tm, D), row),
                  _resident(w_up_all.shape), _resident(w_down_a.shape), _resident(w_down_b.shape), _resident(g_pre.shape), _resident(g_post.shape)],
        out_specs=[pl.BlockSpec((tm, D), row), pl.BlockSpec((tm, ns * fc), row), pl.BlockSpec((tm, D), row), vec, vec],
        out_shape=[jax.ShapeDtypeStruct((L, D), MXU_DTYPE), jax.ShapeDtypeStruct((L, ns * fc), MXU_DTYPE), jax.ShapeDtypeStruct((L, D), F32),
                   jax.ShapeDtypeStruct((1, D), F32), jax.ShapeDtypeStruct((1, D), F32)],
        compiler_params=_params(("arbitrary",)),
    )(dx2, m, up, x1, w_up_all, w_down_a, w_down_b, g_pre, g_post)


def _tail_bwd(dx1, o, ys, proj, w_glu, conv_w, g_ssm, g_conv, w_out, g_post, exchanges=()):
    L, D = dx1.shape
    tm = _tile(L, TM_TAIL)
    nt = L // tm
    hb = tm // SUBLANES

    def body(dx1_ref, o_ref, ys_ref, h_ref, bg_ref, cg_ref, hh_ref, hcg_ref, wglu_ref, cw_ref, gs_ref, gc_ref, wout_ref, gp_ref,
             do_ref, da_ref, y1_ref, dys_ref, dhbc_ref, dgp_ref, dgs_ref, dgc_ref, dcw_ref, zbuf, dcbuf):
        step = pl.program_id(0)

        @pl.when(step == 0)
        def _():
            dcbuf[tm:, :] = jnp.zeros((SUBLANES, D_CONV), F32)
            dgp_ref[...] = jnp.zeros_like(dgp_ref)
            dgs_ref[...] = jnp.zeros_like(dgs_ref)
            dgc_ref[...] = jnp.zeros_like(dgc_ref)
            dcw_ref[...] = jnp.zeros_like(dcw_ref)

        do, dg = _rms_bwd(o_ref[...], gp_ref[...], dx1_ref[...])
        dgp_ref[...] += dg
        do_b = do.astype(MXU_DTYPE)
        do_ref[...] = do_b
        dycat = _dot_nt(do_b, wout_ref[...])
        y1, dgelu = _gelu(_load_slabs(ys_ref))
        y1_b = y1.astype(MXU_DTYPE)
        y1_ref[...] = y1_b
        s = jax.nn.sigmoid(_dot(y1_b, wglu_ref[...]))
        dy2, dg = _rms_bwd(y1 * s, gs_ref[...], dycat[:, :D_SSM])
        dgs_ref[...] += dg
        da_b = (dy2 * y1 * s * (1.0 - s)).astype(MXU_DTYPE)
        da_ref[...] = da_b
        _store_slabs(dys_ref, (dy2 * s + _dot_nt(da_b, wglu_ref[...])) * dgelu)
        h = h_ref[...]
        cg = cg_ref[...]
        bg = bg_ref[...]
        z = cg * h
        first = step == nt - 1
        zbuf[0:SUBLANES, :] = jnp.where(first, 0.0, hcg_ref[...] * hh_ref[...])
        zbuf[SUBLANES:, :] = z
        z1 = zbuf[SUBLANES - 1:SUBLANES - 1 + tm, :]
        z2 = zbuf[SUBLANES - 2:SUBLANES - 2 + tm, :]
        conv = cw_ref[0:1, :] * z2 + cw_ref[1:2, :] * z1 + cw_ref[2:3, :] * z
        dyc, dg = _rms_bwd(bg * conv, gc_ref[...], dycat[:, D_SSM:])
        dgc_ref[...] += dg
        dconv = dyc * bg
        dcw_ref[0:1, :] += jnp.sum(dconv * z2, axis=0, keepdims=True)
        dcw_ref[1:2, :] += jnp.sum(dconv * z1, axis=0, keepdims=True)
        dcw_ref[2:3, :] += jnp.sum(dconv * z, axis=0, keepdims=True)
        dcbuf[0:tm, :] = dconv
        dz = cw_ref[2:3, :] * dconv + cw_ref[1:2, :] * dcbuf[1:1 + tm, :] + cw_ref[0:1, :] * dcbuf[2:2 + tm, :]
        dcbuf[tm:, :] = dcbuf[0:SUBLANES, :]
        dhbc_ref[:, 0:D_CONV] = (dz * cg).astype(MXU_DTYPE)
        dhbc_ref[:, D_CONV:2 * D_CONV] = (dyc * conv).astype(MXU_DTYPE)
        dhbc_ref[:, 2 * D_CONV:] = (dz * h).astype(MXU_DTYPE)

    rev = lambda i: (nt - 1 - i, 0)
    slab = lambda i: (0, nt - 1 - i, 0)
    col = lambda c: (lambda i: (nt - 1 - i, c))
    halo = lambda c: (lambda i: (jnp.maximum((nt - 1 - i) * hb - 1, 0), c))
    vec = lambda n: pl.BlockSpec((1, n), lambda i: (0, 0))
    return _call(
        body, name="tail_bwd", grid=(nt,), exchanges=exchanges, semantics=("arbitrary",),
        operands=(dx1, o, ys, proj, proj, proj, proj, proj, w_glu, conv_w, g_ssm, g_conv, w_out, g_post),
        in_specs=[
            pl.BlockSpec((tm, D), rev), pl.BlockSpec((tm, D), rev), _slab_spec(D_SSM, tm, slab),
            pl.BlockSpec((tm, D_CONV), col(1)), pl.BlockSpec((tm, D_CONV), col(2)), pl.BlockSpec((tm, D_CONV), col(3)),
            pl.BlockSpec((SUBLANES, D_CONV), halo(1)), pl.BlockSpec((SUBLANES, D_CONV), halo(3)),
            _resident(w_glu.shape), _resident(conv_w.shape), _resident(g_ssm.shape), _resident(g_conv.shape),
            _resident(w_out.shape), _resident(g_post.shape),
        ],
        out_specs=[
            pl.BlockSpec((tm, D), rev), pl.BlockSpec((tm, D_SSM), rev), pl.BlockSpec((tm, D_SSM), rev), _slab_spec(D_SSM, tm, slab),
            pl.BlockSpec((tm, 3 * D_CONV), rev), vec(D), vec(D_SSM), vec(D_CONV),
            pl.BlockSpec((SUBLANES, D_CONV), lambda i: (0, 0)),
        ],
        out_shape=[
            jax.ShapeDtypeStruct((L, D), MXU_DTYPE), jax.ShapeDtypeStruct((L, D_SSM), MXU_DTYPE), jax.ShapeDtypeStruct((L, D_SSM), MXU_DTYPE),
            _slab_shape(L, D_SSM), jax.ShapeDtypeStruct((L, 3 * D_CONV), MXU_DTYPE),
            jax.ShapeDtypeStruct((1, D), F32), jax.ShapeDtypeStruct((1, D_SSM), F32), jax.ShapeDtypeStruct((1, D_CONV), F32),
            jax.ShapeDtypeStruct((SUBLANES, D_CONV), F32),
        ],
        scratch_shapes=[pltpu.VMEM((tm + SUBLANES, D_CONV), F32), pltpu.VMEM((tm + SUBLANES, D_CONV), F32)],
    )


def _s5_bwd(dys, u4, s_re, s_im, bmat, cmat, coef_rev, dskip, exchanges=()):
    L = dys.shape[1]
    tm = _tile(L, TM_S5)
    nt = L // tm
    lc = min(LANE_CHUNK, WB)

    def body(dys_ref, u_ref, sre_ref, sim_ref, bm_ref, cm_ref, coef_ref, d_ref,
             du_ref, gb_ref, gc_ref, q_ref, gd_ref, dr_ref, di_ref, lr_ref, li_ref, hr_ref, hi_ref, qr_acc, qi_acc, gb_acc, gc_acc):
        step = pl.program_id(1)

        @pl.when(step == 0)
        def _():
            for ref in (hr_ref, hi_ref, qr_acc, qi_acc, gb_acc, gc_acc, gd_ref):
                ref[...] = jnp.zeros_like(ref)

        dys_v = _load_permuted(dys_ref)
        u = _load_permuted(u_ref)
        dys_b = dys_v.astype(MXU_DTYPE)
        u_b = u.astype(MXU_DTYPE)
        d = _dot(dys_b, cm_ref[0])
        dr_ref[...] = d[:, :WB]
        di_ref[...] = d[:, WB:]
        for c in range(WB // lc):
            lanes = slice(c * lc, (c + 1) * lc)
            hr, hi = hr_ref[:, lanes], hi_ref[:, lanes]
            q = [qr_acc[:, lanes], qi_acc[:, lanes]]
            for b in range(tm // SEG_ROWS - 1, -1, -1):
                rows = lambda j, b=b: slice(b * SEG_ROWS + j * SUBLANES, b * SEG_ROWS + (j + 1) * SUBLANES)

                def read(j, rows=rows, lanes=lanes):
                    return dr_ref[rows(j), lanes], di_ref[rows(j), lanes]

                def write(j, xr, xi, rows=rows, lanes=lanes, q=q):
                    lr_ref[rows(j), lanes] = xr
                    li_ref[rows(j), lanes] = xi
                    er = xr - dr_ref[rows(j), lanes]
                    ei = xi - di_ref[rows(j), lanes]
                    sr = sre_ref[rows(j), lanes]
                    si = sim_ref[rows(j), lanes]
                    q[0] = q[0] + (er * sr + ei * si)
                    q[1] = q[1] + (ei * sr - er * si)

                hr, hi = _scan_block(read, write, hr, hi, coef_ref, lanes, True)
            hr_ref[:, lanes] = hr
            hi_ref[:, lanes] = hi
            qr_acc[:, lanes] = q[0]
            qi_acc[:, lanes] = q[1]
        lr_b = lr_ref[...].astype(MXU_DTYPE)
        li_b = li_ref[...].astype(MXU_DTYPE)
        _store_permuted(du_ref, _dot_nt(lr_b, bm_ref[0, :, :WB]) + _dot_nt(li_b, bm_ref[0, :, WB:]) + d_ref[0] * dys_v)
        sre_b = sre_ref[...].astype(MXU_DTYPE)
        sim_b = sim_ref[...].astype(MXU_DTYPE)
        for t in range(WB // DIAG_COLS):
            ch = slice(t * DIAG_ROWS, (t + 1) * DIAG_ROWS)
            cols = slice(t * DIAG_COLS, (t + 1) * DIAG_COLS)
            icols = slice(WB + t * DIAG_COLS, WB + (t + 1) * DIAG_COLS)
            gb_acc[ch, cols] += _dot_tn(u_b[:, ch], lr_b[:, cols])
            gb_acc[ch, icols] += _dot_tn(u_b[:, ch], li_b[:, cols])
            gc_acc[ch, cols] += _dot_tn(dys_b[:, ch], sre_b[:, cols])
            gc_acc[ch, icols] += _dot_tn(dys_b[:, ch], sim_b[:, cols])
        gd_ref[0] += jnp.sum(dys_v * u, axis=0, keepdims=True)

        @pl.when(step == nt - 1)
        def _():
            q_ref[0, 0:1, :] = jnp.sum(qr_acc[...], axis=0, keepdims=True)
            q_ref[0, 1:2, :] = jnp.sum(qi_acc[...], axis=0, keepdims=True)
            mask = _group_mask(UB, WB)
            fold = (lax.broadcasted_iota(jnp.int32, (WB, STATE), 0) % STATE == lax.broadcasted_iota(jnp.int32, (WB, STATE), 1)).astype(F32)
            for acc, out in ((gb_acc, gb_ref), (gc_acc, gc_ref)):
                for k in range(2):
                    own = jnp.where(mask, acc[:, k * WB:(k + 1) * WB], 0.0)
                    out[0, k] = jnp.dot(own, fold, precision=lax.Precision.HIGHEST, preferred_element_type=F32)

    rev = lambda b, i: (nt - 1 - i, b)
    slab = lambda b, i: (b, nt - 1 - i, 0)
    blk = lambda b, i: (b, 0, 0)
    blk4 = lambda b, i: (b, 0, 0, 0)
    return _call(
        body, name="s5_bwd", grid=(N_GBLK, nt), exchanges=exchanges, semantics=("arbitrary", "arbitrary"),
        operands=(dys, u4, s_re, s_im, bmat, cmat, coef_rev, dskip),
        in_specs=[
            _slab_spec(UB, tm, slab), _slab_spec(UB, tm, slab), pl.BlockSpec((tm, WB), rev), pl.BlockSpec((tm, WB), rev),
            pl.BlockSpec((1, UB, 2 * WB), blk), pl.BlockSpec((1, UB, 2 * WB), blk),
            pl.BlockSpec((N_TABLES, SUBLANES, WB), lambda b, i: (0, 0, b)), pl.BlockSpec((1, 1, UB), blk),
        ],
        out_specs=[
            _slab_spec(UB, tm, slab), pl.BlockSpec((1, 2, UB, STATE), blk4), pl.BlockSpec((1, 2, UB, STATE), blk4),
            pl.BlockSpec((1, 2, WB), blk), pl.BlockSpec((1, 1, UB), blk),
        ],
        out_shape=[
            _slab_shape(L, D_SSM), jax.ShapeDtypeStruct((N_GBLK, 2, UB, STATE), F32),
            jax.ShapeDtypeStruct((N_GBLK, 2, UB, STATE), F32), jax.ShapeDtypeStruct((N_GBLK, 2, WB), F32),
            jax.ShapeDtypeStruct((N_GBLK, 1, UB), F32),
        ],
        scratch_shapes=[pltpu.VMEM((tm, WB), F32)] * 4 + [pltpu.VMEM((SUBLANES, WB), F32)] * 4 + [pltpu.VMEM((UB, 2 * WB), F32)] * 2,
    )


def _inproj_bwd(du, dhbc, x, dx1, w_in_all, g1):
    L, D = x.shape
    ns, _, nc = w_in_all.shape
    tm = _tile(L, TM_PROJ)

    def body(du_ref, dhbc_ref, x_ref, dx1_ref, w_ref, g_ref, gx_ref, dproj_ref, dg_ref):
        @pl.when(pl.program_id(0) == 0)
        def _():
            dg_ref[...] = jnp.zeros_like(dg_ref)

        du_b = _load_slabs(du_ref).astype(MXU_DTYPE)
        dproj_ref[:, :nc] = du_b
        dproj_ref[:, nc:] = dhbc_ref[...]
        dhn = _dot_nt(du_b, w_ref[0])
        for j in range(1, ns):
            dhn = dhn + _dot_nt(dhbc_ref[:, (j - 1) * nc:j * nc], w_ref[j])
        dx, dg = _rms_bwd(x_ref[...], g_ref[...], dhn)
        dg_ref[...] += dg
        gx_ref[...] = dx1_ref[...] + dx

    row = lambda i: (i, 0)
    return pl.pallas_call(
        body, name="inproj_bwd", grid=(L // tm,),
        in_specs=[_slab_spec(nc, tm), pl.BlockSpec((tm, (ns - 1) * nc), row), pl.BlockSpec((tm, D), row), pl.BlockSpec((tm, D), row),
                  _resident(w_in_all.shape), _resident(g1.shape)],
        out_specs=[pl.BlockSpec((tm, D), row), pl.BlockSpec((tm, ns * nc), row), pl.BlockSpec((1, D), lambda i: (0, 0))],
        out_shape=[jax.ShapeDtypeStruct((L, D), F32), jax.ShapeDtypeStruct((L, ns * nc), MXU_DTYPE), jax.ShapeDtypeStruct((1, D), F32)],
        compiler_params=_params(("arbitrary",)),
    )(du, dhbc, x, dx1, w_in_all, g1)


def _matmul_tn(a, b, name, col_shards=1, exchanges=()):
    L, K = a.shape
    N = b.shape[1]
    tl = _tile(L, TL_TN)
    tk = _tile(K, TK_TN)
    nw = N // col_shards
    spb = max(1, min(col_shards, TN_TN // nw))
    tn = spb * nw if spb > 1 else _tile(nw, TK_TN)
    npb = nw // tn if spb == 1 else 1

    def body(a_ref, b_ref, o_ref):
        @pl.when(pl.program_id(2) == 0)
        def _():
            o_ref[...] = jnp.zeros_like(o_ref)

        res = _dot_tn(a_ref[...], b_ref[...])
        for s in range(spb):
            o_ref[s] += res[:, s * nw:(s + 1) * nw] if spb > 1 else res

    out_block = pl.BlockSpec((spb, tk, nw if spb > 1 else tn), (lambda k, n, l: (n, k, 0)) if spb > 1 else (lambda k, n, l: (n // npb, k, n % npb)))
    return _call(
        body, name=name, grid=(K // tk, N // tn, L // tl), exchanges=exchanges, semantics=("arbitrary", "arbitrary", "arbitrary"),
        operands=(a, b),
        in_specs=[pl.BlockSpec((tl, tk), lambda k, n, l: (l, k)), pl.BlockSpec((tl, tn), lambda k, n, l: (l, n))],
        out_specs=[out_block],
        out_shape=[jax.ShapeDtypeStruct((col_shards, K, nw), F32)],
    )


def _ssm_discretize(lam_re, lam_im, log_dt, bt_re, bt_im):
    dt = jnp.exp(log_dt)[:, None]
    zr = lam_re * dt
    zi = lam_im * dt
    mag = jnp.exp(zr)
    abr = mag * jnp.cos(zi)
    abi = mag * jnp.sin(zi)
    nr, ni = abr - 1.0, abi
    den = lam_re * lam_re + lam_im * lam_im
    coef_r = ((nr * lam_re + ni * lam_im) / den)[:, None, :]
    coef_i = ((ni * lam_re - nr * lam_im) / den)[:, None, :]
    return zr, zi, coef_r * bt_re - coef_i * bt_im, coef_r * bt_im + coef_i * bt_re


def _scan_tables(ar, ai, reverse):
    rows = np.arange(SUBLANES)
    exps = np.zeros((N_TABLES // 2, SUBLANES), np.int32)
    keep = np.ones((N_TABLES // 2, SUBLANES), bool)
    for t, k in enumerate((1, 2, 4)):
        exps[t] = SUBLANES * k
        keep[t] = (rows + k <= SUBLANES - 1) if reverse else (rows >= k)
    exps[3] = SUBLANES * (SUBLANES - rows) if reverse else SUBLANES * (rows + 1)
    for j in range(SUBLANES):
        exps[4 + j] = SUBLANES - j if reverse else j + 1
    pr, pi = ar, (-ai if reverse else ai)
    shape = (N_TABLES // 2, SUBLANES, ar.shape[0])
    xr, xi = jnp.ones(shape, F32), jnp.zeros(shape, F32)
    for bit in range(int(exps.max()).bit_length()):
        on = ((exps >> bit) & 1).astype(bool)[:, :, None]
        xr, xi = jnp.where(on, xr * pr - xi * pi, xr), jnp.where(on, xr * pi + xi * pr, xi)
        pr, pi = pr * pr - pi * pi, 2.0 * pr * pi
    xr = jnp.where(keep[:, :, None], xr, 0.0)
    xi = jnp.where(keep[:, :, None], xi, 0.0)
    return jnp.stack([xr, xi], axis=1).reshape(N_TABLES, SUBLANES, ar.shape[0])


def _group_mask(rows, cols):
    r = lax.broadcasted_iota(jnp.int32, (rows, cols), 0) // GROUP
    c = lax.broadcasted_iota(jnp.int32, (rows, cols), 1) // STATE
    return r == c


def _ssm_expand(bt_re, bt_im, c_re, c_im, exchanges=()):
    flat = lambda a: a.reshape(N_GROUPS * GROUP, STATE)

    def body(br_ref, bi_ref, cr_ref, ci_ref, bm_ref, cm_ref):
        spread = (lax.broadcasted_iota(jnp.int32, (STATE, WB), 1) % STATE == lax.broadcasted_iota(jnp.int32, (STATE, WB), 0)).astype(F32)
        mask = _group_mask(UB, WB)

        def expand(x):
            wide = jnp.dot(x, spread, precision=lax.Precision.HIGHEST, preferred_element_type=F32)
            return jnp.where(mask, wide, 0.0).astype(MXU_DTYPE)

        bm_ref[0, :, :WB] = expand(br_ref[...])
        bm_ref[0, :, WB:] = expand(bi_ref[...])
        cm_ref[0, :, :WB] = expand(cr_ref[...])
        cm_ref[0, :, WB:] = expand(-ci_ref[...])

    spec = pl.BlockSpec((UB, STATE), lambda b: (b, 0))
    out = pl.BlockSpec((1, UB, 2 * WB), lambda b: (b, 0, 0))
    return _call(
        body, name="ssm_expand", grid=(N_GBLK,), exchanges=exchanges, semantics=("arbitrary",),
        operands=(flat(bt_re), flat(bt_im), flat(c_re), flat(c_im)), in_specs=[spec] * 4, out_specs=[out, out],
        out_shape=[jax.ShapeDtypeStruct((N_GBLK, UB, 2 * WB), MXU_DTYPE)] * 2,
    )


def _ssm_matrices(lam_re, lam_im, log_dt, bt_re, bt_im, c_re, c_im, exchanges=()):
    zr, zi, bbar_r, bbar_i = _ssm_discretize(lam_re, lam_im, log_dt, bt_re, bt_im)
    mag = jnp.exp(zr)
    ar = (mag * jnp.cos(zi)).reshape(-1)
    ai = (mag * jnp.sin(zi)).reshape(-1)
    bmat, cmat, *rest = _ssm_expand(bbar_r, bbar_i, c_re, c_im, exchanges)
    return bmat, cmat, _scan_tables(ar, ai, False), _scan_tables(ar, ai, True), rest


def _ssm_param_grads(lam_re, lam_im, log_dt, bt_re, bt_im, gb, gc, q, gd):
    part = lambda g, k: g[:, k].reshape(N_GROUPS, GROUP, STATE)
    qr = q[:, 0, :].reshape(N_GROUPS, STATE)
    qi = q[:, 1, :].reshape(N_GROUPS, STATE)
    _, vjp = jax.vjp(_ssm_discretize, lam_re, lam_im, log_dt, bt_re, bt_im)
    d_lam_re, d_lam_im, d_log_dt, d_bt_re, d_bt_im = vjp((qr, qi, part(gb, 0), part(gb, 1)))
    return d_lam_re, d_lam_im, d_log_dt, d_bt_re, d_bt_im, part(gc, 0), -part(gc, 1), gd.reshape(N_GROUPS, GROUP)


def _row_tile(rows, n):
    return _tile(rows, max(SUBLANES, (2 * 1024 * 1024) // (4 * n)))


def _pair_add(grad, other, core, name):
    ns, h, n = other.shape
    tr = _row_tile(h, n)
    nb = h // tr

    def body(c_ref, g_ref, o_ref, out_ref):
        out_ref[...] = (g_ref[...] + o_ref[...]).astype(WIRE_DTYPE)

    return pl.pallas_call(
        body, name=name,
        grid_spec=pltpu.PrefetchScalarGridSpec(
            num_scalar_prefetch=1, grid=(ns, nb),
            in_specs=[pl.BlockSpec((1, tr, n), lambda s, i, c: (s, c[0] * nb + i, 0)), pl.BlockSpec((1, tr, n), lambda s, i, c: (s, i, 0))],
            out_specs=pl.BlockSpec((1, tr, n), lambda s, i, c: (s, i, 0))),
        out_shape=jax.ShapeDtypeStruct(other.shape, WIRE_DTYPE),
        compiler_params=_params(("arbitrary", "arbitrary")),
    )(core, grad, other)


def _quad_sum(parts, core, name):
    ns, h, n = parts.shape
    tr = _row_tile(h, n)
    nb = h // tr

    def body(c_ref, p_ref, out_ref):
        p = [p_ref[k].astype(F32) for k in range(ns)]
        out_ref[...] = ((p[0] + p[1]) + p[2]) + p[3]

    return pl.pallas_call(
        body, name=name,
        grid_spec=pltpu.PrefetchScalarGridSpec(
            num_scalar_prefetch=1, grid=(nb,),
            in_specs=[pl.BlockSpec((ns, tr, n), lambda i, c: (0, i, 0))],
            out_specs=pl.BlockSpec((tr, n), lambda i, c: (c[0] * nb + i, 0))),
        out_shape=jax.ShapeDtypeStruct((2 * h, n), F32),
        compiler_params=_params(("arbitrary",)),
    )(core, parts)


def _adamw_math(w, g, m, v):
    m = ADAM_B1 * m + (1.0 - ADAM_B1) * g
    v = ADAM_B2 * v + (1.0 - ADAM_B2) * jnp.square(g)
    m_hat = m / (1.0 - ADAM_B1 ** ADAM_STEP)
    v_hat = v / (1.0 - ADAM_B2 ** ADAM_STEP)
    delta = -ADAM_LR * (m_hat / (jnp.sqrt(v_hat) + ADAM_EPS) + ADAM_WD * w)
    return delta, m, v


def _adamw(w, g, m, v, name):
    r, n = w.shape
    tr = _row_tile(r, n)

    def body(w_ref, g_ref, m_ref, v_ref, d_ref, nm_ref, nv_ref):
        d_ref[...], nm_ref[...], nv_ref[...] = _adamw_math(w_ref[...], g_ref[...], m_ref[...], v_ref[...])

    spec = pl.BlockSpec((tr, n), lambda i: (i, 0))
    return pl.pallas_call(
        body, name=name, grid=(r // tr,), in_specs=[spec] * 4, out_specs=[spec] * 3,
        out_shape=[jax.ShapeDtypeStruct((r, n), F32)] * 3,
        compiler_params=_params(("arbitrary",)),
    )(w, g, m, v)


LANES = 128
SMALL = ["g_pre_mix", "lam_re", "lam_im", "log_dt", "b_re", "b_im", "c_re", "c_im", "d_skip", "conv_w", "g_ssm_out", "g_conv_out",
         "g_post_mix", "g_pre_mlp", "g_post_mlp"]
TILE_SLOTS = {"b_re": (0, N_GROUPS), "b_im": (N_GROUPS, N_GROUPS), "c_re": (2 * N_GROUPS, N_GROUPS), "c_im": (3 * N_GROUPS, N_GROUPS),
              "lam_re": (4 * N_GROUPS, 2), "lam_im": (4 * N_GROUPS + 2, 2)}
N_TILE_SLOTS = 4 * N_GROUPS + 4
VEC_ROWS = {"g_pre_mix": 0, "g_post_mix": 1, "g_pre_mlp": 2, "g_post_mlp": 3, "g_ssm_out": 4, "g_conv_out": 5, "log_dt": 6}
ROW_LOSS, ROW_DSKIP, ROW_CONV, N_PACK_ROWS = 7, 8, 24, 32


def _kernel_form(name, a):
    if name in ("b_re", "b_im"):
        return jnp.transpose(a, (0, 1, 3, 2)).reshape(N_GROUPS, GROUP, STATE)
    if name in ("c_re", "c_im"):
        return a.reshape(N_GROUPS, GROUP, STATE)
    if name in ("lam_re", "lam_im"):
        return a.reshape(2, GROUP, STATE)
    if name == "d_skip":
        return jnp.transpose(a, (0, 2, 1)).reshape(GROUP, N_GROUPS)
    if name == "conv_w":
        return jnp.transpose(a, (1, 0, 2))
    return a


def _param_form(name, k):
    if name in ("b_re", "b_im"):
        return jnp.transpose(k.reshape(1, N_GROUPS, GROUP, STATE), (0, 1, 3, 2))
    if name in ("c_re", "c_im"):
        return k.reshape(1, N_GROUPS, GROUP, STATE)
    if name in ("lam_re", "lam_im"):
        return k.reshape(1, N_GROUPS, STATE)
    if name == "d_skip":
        return jnp.transpose(k.reshape(1, GROUP, N_GROUPS), (0, 2, 1))
    if name == "conv_w":
        return jnp.transpose(k, (1, 0, 2))
    return k


def _pack_tiles(g):
    lam = lambda a: a.reshape(2, GROUP, STATE)
    tiles = jnp.concatenate([g["b_re"], g["b_im"], g["c_re"], g["c_im"], lam(g["lam_re"]), lam(g["lam_im"])], axis=0)
    return tiles.astype(WIRE_DTYPE)


def _pack_rows(g, loss):
    row = lambda a: jnp.pad(a, ((0, 0), (0, D_MODEL - a.shape[1])))
    rows = [row(g[k][None]) for k in VEC_ROWS] + [row(loss[0:1]), row(g["d_skip"].T), row(g["conv_w"])]
    rows.append(jnp.zeros((N_PACK_ROWS - ROW_CONV - 3, D_MODEL), F32))
    return jnp.concatenate(rows, axis=0)


def _adamw_small(tiles, rows, w, m, v):
    nn = len(SMALL)

    def body(*refs):
        t_ref, r_ref = refs[0], refs[1]
        w_refs, m_refs, v_refs = refs[2:2 + nn], refs[2 + nn:2 + 2 * nn], refs[2 + 2 * nn:2 + 3 * nn]
        loss_ref, outs = refs[2 + 3 * nn], refs[3 + 3 * nn:]

        def tile_sum(first, count):
            total = t_ref[0, first:first + count].astype(F32)
            for d in range(1, N_DEV):
                total = total + t_ref[d, first:first + count].astype(F32)
            return total

        def row_sum(first, count, lanes):
            total = r_ref[0, first:first + count, 0:lanes]
            for d in range(1, N_DEV):
                total = total + r_ref[d, first:first + count, 0:lanes]
            return total

        def step(j, g, at=lambda ref: ref):
            delta, nm, nv = _adamw_math(at(w_refs[j])[...], g, at(m_refs[j])[...], at(v_refs[j])[...])
            at(outs[j])[...] = g
            at(outs[nn + j])[...] = delta
            at(outs[2 * nn + j])[...] = nm
            at(outs[3 * nn + j])[...] = nv

        loss_ref[...] = row_sum(ROW_LOSS, 1, LANES)
        chip = 2 * lax.axis_index("x") + lax.axis_index("y")
        for j, name in enumerate(SMALL):
            if name in TILE_SLOTS:
                step(j, tile_sum(*TILE_SLOTS[name]))
            elif name == "d_skip":
                step(j, row_sum(ROW_DSKIP, GROUP, N_GROUPS))
            elif name == "conv_w":
                full = row_sum(ROW_CONV, 3, D_CONV)
                mine = full[:, 0:LANES]
                for s in range(1, N_CHIPS):
                    mine = jnp.where(chip == s, full[:, s * LANES:(s + 1) * LANES], mine)
                for k in range(3):
                    step(j, mine[k:k + 1, :], at=lambda ref, k=k: ref.at[k])
            else:
                step(j, row_sum(VEC_ROWS[name], 1, w_refs[j].shape[1]))

    args = [tiles, rows] + [w[k] for k in SMALL] + [m[k] for k in SMALL] + [v[k] for k in SMALL]
    res = pl.pallas_call(
        body, name="adamw_small", in_specs=[VMEM] * len(args), out_specs=[VMEM] * (1 + 4 * nn),
        out_shape=[jax.ShapeDtypeStruct((1, LANES), F32)] + [jax.ShapeDtypeStruct(w[k].shape, F32) for k in SMALL] * 4,
        compiler_params=pltpu.CompilerParams(vmem_limit_bytes=VMEM_LIMIT),
    )(*args)
    return res[0], [dict(zip(SMALL, res[1 + q * nn:1 + (q + 1) * nn])) for q in range(4)]


WEIGHTS = ["g_pre_mix", "w_in", "lam_re", "lam_im", "log_dt", "b_re", "b_im", "c_re", "c_im", "d_skip", "w_glu", "conv_w",
           "g_ssm_out", "g_conv_out", "w_out", "g_post_mix", "g_pre_mlp", "w_up", "w_down", "g_post_mlp"]
BIG = ["w_in", "w_glu", "w_out", "w_up", "w_down"]


def kernel(x, g_pre_mix, w_in, lam_re, lam_im, log_dt, b_re, b_im, c_re, c_im, d_skip, w_glu, conv_w, g_ssm_out, g_conv_out, w_out, g_post_mix, g_pre_mlp, w_up, w_down, g_post_mlp, loss_target, m_g_pre_mix, m_w_in, m_lam_re, m_lam_im, m_log_dt, m_b_re, m_b_im, m_c_re, m_c_im, m_d_skip, m_w_glu, m_conv_w, m_g_ssm_out, m_g_conv_out, m_w_out, m_g_post_mix, m_g_pre_mlp, m_w_up, m_w_down, m_g_post_mlp, v_g_pre_mix, v_w_in, v_lam_re, v_lam_im, v_log_dt, v_b_re, v_b_im, v_c_re, v_c_im, v_d_skip, v_w_glu, v_conv_w, v_g_ssm_out, v_g_conv_out, v_w_out, v_g_post_mix, v_g_pre_mlp, v_w_up, v_w_down, v_g_post_mlp):
    w = dict(g_pre_mix=g_pre_mix, w_in=w_in, lam_re=lam_re, lam_im=lam_im, log_dt=log_dt, b_re=b_re, b_im=b_im, c_re=c_re, c_im=c_im,
             d_skip=d_skip, w_glu=w_glu, conv_w=conv_w, g_ssm_out=g_ssm_out, g_conv_out=g_conv_out, w_out=w_out, g_post_mix=g_post_mix,
             g_pre_mlp=g_pre_mlp, w_up=w_up, w_down=w_down, g_post_mlp=g_post_mlp)
    m = dict(g_pre_mix=m_g_pre_mix, w_in=m_w_in, lam_re=m_lam_re, lam_im=m_lam_im, log_dt=m_log_dt, b_re=m_b_re, b_im=m_b_im, c_re=m_c_re,
             c_im=m_c_im, d_skip=m_d_skip, w_glu=m_w_glu, conv_w=m_conv_w, g_ssm_out=m_g_ssm_out, g_conv_out=m_g_conv_out, w_out=m_w_out,
             g_post_mix=m_g_post_mix, g_pre_mlp=m_g_pre_mlp, w_up=m_w_up, w_down=m_w_down, g_post_mlp=m_g_post_mlp)
    v = dict(g_pre_mix=v_g_pre_mix, w_in=v_w_in, lam_re=v_lam_re, lam_im=v_lam_im, log_dt=v_log_dt, b_re=v_b_re, b_im=v_b_im, c_re=v_c_re,
             c_im=v_c_im, d_skip=v_d_skip, w_glu=v_w_glu, conv_w=v_conv_w, g_ssm_out=v_g_ssm_out, g_conv_out=v_g_conv_out, w_out=v_w_out,
             g_post_mix=v_g_post_mix, g_pre_mlp=v_g_pre_mlp, w_up=v_w_up, w_down=v_w_down, g_post_mlp=v_g_post_mlp)
    w_dev, m_dev, v_dev = w, m, v
    w, m, v = ({k: a[0] for k, a in d.items()} for d in (w, m, v))
    core = lax.axis_index("c").astype(jnp.int32).reshape(1)

    xs, target = x[0], loss_target[0]
    g1 = w["g_pre_mix"][None]
    g_ssm, g_conv = w["g_ssm_out"][None], w["g_conv_out"][None]
    g_post_mix, g_pre_mlp, g_post_mlp = w["g_post_mix"][None], w["g_pre_mlp"][None], w["g_post_mlp"][None]
    bt_re, bt_im = (jnp.transpose(w[k], (0, 2, 1)) for k in ("b_re", "b_im"))
    dskip = w["d_skip"].reshape(N_GBLK, 1, UB)
    shard = {k: w[k].astype(MXU_DTYPE) for k in BIG}
    conv_pad = jnp.pad(w["conv_w"], ((0, SUBLANES - 3), (0, 0)))

    bmat, cmat, coef_f, coef_r, (w_in_all,) = _ssm_matrices(
        w["lam_re"], w["lam_im"], w["log_dt"], bt_re, bt_im, w["c_re"], w["c_im"], exchanges=[_GatherForward([shard["w_in"]])])
    hn, proj, u4, w_glu_all, w_out_all, conv_all = _inproj_fwd(
        xs, g1, w_in_all, exchanges=[_Gather([shard["w_glu"], shard["w_out"], conv_pad], [False, False, False])])
    wd_half = shard["w_down"].shape[0] // 2
    wda, wdb = shard["w_down"][:wd_half], shard["w_down"][wd_half:]
    s_re, s_im, ys, w_up_all, wda_all = _s5_fwd(u4, bmat, cmat, coef_f, dskip, exchanges=[_Gather([shard["w_up"], wda], [True, True])])
    w_glu_f, w_out_f = w_glu_all.reshape(D_SSM, D_SSM), w_out_all.reshape(D_MODEL, D_MODEL)
    conv_f = jnp.transpose(conv_all, (1, 0, 2)).reshape(SUBLANES, D_CONV)
    ycat, o, x1, w_up_all, wda_all, wdb_all = _tail_fwd(xs, ys, proj, w_glu_f, conv_f, g_ssm, g_conv, w_out_f, g_post_mix,
                                                        exchanges=[_Forward([w_up_all, wda_all]), _GatherForward([wdb])])
    hn2, up, act, m_act, dx2, loss = _mlp_fwd(x1, target, w_up_all, wda_all, wdb_all, g_pre_mlp, g_post_mlp)

    dm, dup, dx1, dg_post_mlp, dg_pre_mlp = _mlp_bwd(dx2, m_act, up, x1, w_up_all, wda_all, wdb_all, g_pre_mlp, g_post_mlp)
    gw_down = _matmul_tn(act, dm, "dw_down")[0].reshape(N_CHIPS, D_FF // N_CHIPS, D_MODEL)
    gw_up = _matmul_tn(hn2, dup, "dw_up", col_shards=N_CHIPS)[0]
    do, da, y1, dys, dhbc, dg_post_mix, dg_ssm, dg_conv, dconv_w, o_down, o_up = _tail_bwd(
        dx1, o, ys, proj, w_glu_f, conv_f, g_ssm, g_conv, w_out_f, g_post_mix, exchanges=[_Pair([gw_down, gw_up])])
    p_down = _pair_add(gw_down, o_down, core, "pair_add_w_down")
    p_up = _pair_add(gw_up, o_up, core, "pair_add_w_up")
    gw_out = _matmul_tn(ycat, do, "dw_out")[0].reshape(N_CHIPS, D_MODEL // N_CHIPS, D_MODEL)
    gw_glu = _matmul_tn(y1, da, "dw_glu")[0].reshape(N_CHIPS, D_SSM // N_CHIPS, D_SSM)
    du, gb, gc, q, gd, q_down, q_up, o_out, o_glu = _s5_bwd(
        dys, u4, s_re, s_im, bmat, cmat, coef_r, dskip, exchanges=[_Chip([p_down, p_up]), _Pair([gw_out, gw_glu])])
    h_down = _quad_sum(q_down, core, "quad_sum_w_down")
    h_up = _quad_sum(q_up, core, "quad_sum_w_up")
    p_out = _pair_add(gw_out, o_out, core, "pair_add_w_out")
    p_glu = _pair_add(gw_glu, o_glu, core, "pair_add_w_glu")
    grad_x, dproj, dg_pre_mix = _inproj_bwd(du, dhbc, xs, dx1, w_in_all, g1)
    d_lam_re, d_lam_im, d_log_dt, d_b_re, d_b_im, d_c_re, d_c_im, d_d_skip = _ssm_param_grads(
        w["lam_re"], w["lam_im"], w["log_dt"], bt_re, bt_im, gb, gc, q, gd)
    small = {
        "g_pre_mix": dg_pre_mix[0], "lam_re": d_lam_re, "lam_im": d_lam_im, "log_dt": d_log_dt, "b_re": d_b_re, "b_im": d_b_im,
        "c_re": d_c_re, "c_im": d_c_im, "d_skip": d_d_skip, "conv_w": dconv_w[:3], "g_ssm_out": dg_ssm[0], "g_conv_out": dg_conv[0],
        "g_post_mix": dg_post_mix[0], "g_pre_mlp": dg_pre_mlp[0], "g_post_mlp": dg_post_mlp[0],
    }
    gw_in, g_down, g_up, q_out, q_glu, tiles, rows = _matmul_tn(
        hn, dproj, "dw_in", col_shards=N_CHIPS,
        exchanges=[_Share([h_down, h_up]), _Chip([p_out, p_glu]), _GatherSmall(_pack_tiles(small)), _GatherSmall(_pack_rows(small, loss))])
    h_out = _quad_sum(q_out, core, "quad_sum_w_out")
    h_glu = _quad_sum(q_glu, core, "quad_sum_w_glu")
    (o_in,) = _run_exchanges([_Pair([gw_in])], "rs_pair_w_in")
    p_in = _pair_add(gw_in, o_in, core, "pair_add_w_in")
    q_in, g_out, g_glu = _run_exchanges([_Chip([p_in]), _Share([h_out, h_glu])], "rs_chip_w_in")
    h_in = _quad_sum(q_in, core, "quad_sum_w_in")
    (g_in,) = _run_exchanges([_Share([h_in])], "rs_share_w_in")
    shard_grads = {"w_in": g_in, "w_glu": g_glu, "w_out": g_out, "w_up": g_up, "w_down": g_down}

    out = {q: {} for q in ("grad", "delta", "new_m", "new_v")}
    for k in BIG:
        out["grad"][k] = shard_grads[k][None]
        delta, new_m, new_v = _adamw(w[k], shard_grads[k], m[k], v[k], "adamw_" + k)
        out["delta"][k], out["new_m"][k], out["new_v"][k] = delta[None], new_m[None], new_v[None]
    form = lambda d: {k: _kernel_form(k, d[k]) for k in SMALL}
    loss, res = _adamw_small(tiles, rows, form(w_dev), form(m_dev), form(v_dev))
    for q, d in zip(("grad", "delta", "new_m", "new_v"), res):
        out[q].update({k: _param_form(k, d[k]) for k in SMALL})
    flat = [loss[0, 0], grad_x[None]]
    for q in ("grad", "delta", "new_m", "new_v"):
        flat += [out[q][k] for k in WEIGHTS]
    return tuple(flat)
```

```python
import functools
import math

import jax
import jax.numpy as jnp
import numpy as np
from jax import lax
from jax.experimental import pallas as pl
from jax.experimental.pallas import tpu as pltpu

F32 = jnp.float32
MXU_DTYPE = jnp.bfloat16
WIRE_DTYPE = jnp.bfloat16

D_MODEL = 1024
D_SSM = 512
D_CONV = 512
N_GROUPS = 32
GROUP = 16
STATE = 64
D_FF = 4096
RMS_EPS = 1e-6
N_CHIPS = 4
N_DEV = 8

ADAM_LR = 0.001
ADAM_B1 = 0.9
ADAM_B2 = 0.999
ADAM_EPS = 1e-08
ADAM_WD = 0.01
ADAM_STEP = 10

N_GBLK = 2
G_PER_BLK = N_GROUPS // N_GBLK
UB = G_PER_BLK * GROUP
WB = G_PER_BLK * STATE
LANE_CHUNK = 256
SUBLANES = 8
N_TABLES = 24
DIAG_COLS = 256
DIAG_ROWS = DIAG_COLS // STATE * GROUP

TM_PROJ = 512
TM_S5 = 512
TM_TAIL = 512
TM_MLP = 512
TL_TN = 2048
TK_TN = 1024
TN_TN = 2048
VMEM_LIMIT = 56 * 1024 * 1024

MESH = pl.DeviceIdType.MESH


def _params(sem, vmem=VMEM_LIMIT):
    return pltpu.CompilerParams(dimension_semantics=sem, vmem_limit_bytes=vmem)


def _resident(shape):
    nd = len(shape)
    return pl.BlockSpec(shape, lambda *_: (0,) * nd, pipeline_mode=pl.Buffered(1))


def _dot(a, b):
    return jnp.dot(a, b, preferred_element_type=F32)


def _dot_nt(a, b):
    return lax.dot_general(a, b, (((1,), (1,)), ((), ())), preferred_element_type=F32)


def _dot_tn(a, b):
    return lax.dot_general(a, b, (((0,), (0,)), ((), ())), preferred_element_type=F32)


def _rms_fwd(x, g):
    r = lax.rsqrt(jnp.mean(x * x, axis=-1, keepdims=True) + RMS_EPS)
    return x * r * g


def _rms_bwd(x, g, dy):
    r = lax.rsqrt(jnp.mean(x * x, axis=-1, keepdims=True) + RMS_EPS)
    xn = x * r
    q = dy * g
    dx = r * (q - xn * jnp.mean(q * xn, axis=-1, keepdims=True))
    return dx, jnp.sum(dy * xn, axis=0, keepdims=True)


_GELU_C = math.sqrt(2.0 / math.pi)


def _gelu(x):
    t = jnp.tanh(_GELU_C * (x + 0.044715 * (x * x * x)))
    y = x * (0.5 * (1.0 + t))
    dy = 0.5 * (1.0 + t) + 0.5 * x * (1.0 - t * t) * (_GELU_C * (1.0 + 3 * 0.044715 * (x * x)))
    return y, dy


def _tile(n, pref):
    t = min(n, pref)
    assert n % t == 0, (n, t)
    return t


HBM = pl.BlockSpec(memory_space=pltpu.HBM)
VMEM = pl.BlockSpec(memory_space=pltpu.VMEM)
DMA_SEMS = pltpu.SemaphoreType.DMA


def _place():
    x, y, c = lax.axis_index("x"), lax.axis_index("y"), lax.axis_index("c")
    chips = [(1 - x, y), (x, 1 - y), (1 - x, 1 - y)]
    return (x, y, c), 2 * x + y, (x, y, 1 - c), chips, [2 * px + py for px, py in chips]


def _remote(src, dst, send_sem, recv_sem, device):
    return pltpu.make_async_remote_copy(src_ref=src, dst_ref=dst, send_sem=send_sem, recv_sem=recv_sem,
                                        device_id=device, device_id_type=MESH)


def _half(rows, c):
    return pl.ds(c * (rows // 2), rows // 2)


class _Exchange:
    aliases = {}

    def start(self, ins, outs, sems):
        local, outgoing, _ = self._copies(ins, outs, sems)
        for cp in local + outgoing:
            cp.start()

    def finish(self, ins, outs, sems):
        local, outgoing, incoming = self._copies(ins, outs, sems)
        for cp in incoming:
            cp.wait_recv()
        for cp in outgoing:
            cp.wait_send()
        for cp in local:
            cp.wait()


class _Gather(_Exchange):
    def __init__(self, shards, split):
        self.inputs, self.split = list(shards), split
        self.out_shape = [jax.ShapeDtypeStruct((N_CHIPS, *a.shape), a.dtype) for a in shards]
        self.sems = [DMA_SEMS((len(shards), 3)), DMA_SEMS((len(shards), 3)), DMA_SEMS((len(shards),))]

    def _copies(self, ins, outs, sems):
        send, recv, lsem = sems
        (x, y, c), me, sibling, chips, ids = _place()
        local = [pltpu.make_async_copy(ins[t], outs[t].at[me], lsem.at[t]) for t in range(len(ins))]
        outgoing, incoming = [], []
        for t, a in enumerate(self.inputs):
            rows = _half(a.shape[0], c) if self.split[t] else pl.ds(0, a.shape[0])
            for k in range(3):
                to = (*chips[k], c)
                outgoing.append(_remote(ins[t].at[rows, :], outs[t].at[me, rows, :], send.at[t, k], recv.at[t, k], to))
                incoming.append(_remote(ins[t].at[rows, :], outs[t].at[ids[k], rows, :], send.at[t, k], recv.at[t, k], to))
        return local, outgoing, incoming


class _Forward(_Exchange):
    def __init__(self, arrays):
        self.inputs = list(arrays)
        self.out_shape = [jax.ShapeDtypeStruct(a.shape, a.dtype) for a in arrays]
        self.aliases = {t: t for t in range(len(arrays))}
        self.sems = [DMA_SEMS((len(arrays), 3)), DMA_SEMS((len(arrays), 3))]

    def _copies(self, ins, outs, sems):
        send, recv = sems
        (x, y, c), me, sibling, chips, ids = _place()
        outgoing, incoming = [], []
        for t, a in enumerate(self.inputs):
            for k in range(3):
                mine = outs[t].at[ids[k], _half(a.shape[1], c), :]
                theirs = outs[t].at[ids[k], _half(a.shape[1], 1 - c), :]
                outgoing.append(_remote(mine, mine, send.at[t, k], recv.at[t, k], sibling))
                incoming.append(_remote(theirs, theirs, send.at[t, k], recv.at[t, k], sibling))
        return [], outgoing, incoming


class _GatherForward(_Exchange):
    def __init__(self, shards):
        self.gather = _Gather(shards, [True] * len(shards))
        self.forward = _Forward(self.gather.out_shape)
        self.inputs, self.out_shape = self.gather.inputs, self.gather.out_shape
        self.sems = self.gather.sems + self.forward.sems

    def start(self, ins, outs, sems):
        self.gather.start(ins, outs, sems[:3])

    def finish(self, ins, outs, sems):
        local, outgoing, incoming = self.gather._copies(ins, outs, sems[:3])
        _, passed, from_sibling = self.forward._copies(outs, outs, sems[3:])
        for landed, onward in zip(incoming, passed):
            landed.wait_recv()
            onward.start()
        for cp in from_sibling:
            cp.wait_recv()
        for cp in outgoing + passed:
            cp.wait_send()
        for cp in local:
            cp.wait()


class _Pair(_Exchange):
    def __init__(self, grads):
        self.inputs = list(grads)
        self.out_shape = [jax.ShapeDtypeStruct((g.shape[0], g.shape[1] // 2, g.shape[2]), g.dtype) for g in grads]
        self.sems = [DMA_SEMS((len(grads),)), DMA_SEMS((len(grads),))]

    def _copies(self, ins, outs, sems):
        send, recv = sems
        (x, y, c), me, sibling, chips, ids = _place()
        cps = [_remote(ins[t].at[:, _half(g.shape[1], 1 - c), :], outs[t], send.at[t], recv.at[t], sibling)
               for t, g in enumerate(self.inputs)]
        return [], cps, cps


class _Chip(_Exchange):
    def __init__(self, parts):
        self.inputs = list(parts)
        self.out_shape = [jax.ShapeDtypeStruct(p.shape, p.dtype) for p in parts]
        self.sems = [DMA_SEMS((len(parts), 3)), DMA_SEMS((len(parts), 3)), DMA_SEMS((len(parts),))]

    def _copies(self, ins, outs, sems):
        send, recv, lsem = sems
        (x, y, c), me, sibling, chips, ids = _place()
        local = [pltpu.make_async_copy(ins[t].at[me], outs[t].at[me], lsem.at[t]) for t in range(len(ins))]
        outgoing, incoming = [], []
        for t in range(len(ins)):
            for k in range(3):
                to = (*chips[k], c)
                outgoing.append(_remote(ins[t].at[ids[k]], outs[t].at[me], send.at[t, k], recv.at[t, k], to))
                incoming.append(_remote(ins[t].at[ids[k]], outs[t].at[ids[k]], send.at[t, k], recv.at[t, k], to))
        return local, outgoing, incoming


class _Share(_Exchange):
    def __init__(self, grads):
        self.inputs = list(grads)
        self.out_shape = [jax.ShapeDtypeStruct(g.shape, g.dtype) for g in grads]
        self.aliases = {t: t for t in range(len(grads))}
        self.sems = [DMA_SEMS((len(grads),)), DMA_SEMS((len(grads),))]

    def _copies(self, ins, outs, sems):
        send, recv = sems
        (x, y, c), me, sibling, chips, ids = _place()
        outgoing, incoming = [], []
        for t, g in enumerate(self.inputs):
            mine = outs[t].at[_half(g.shape[0], c), :]
            theirs = outs[t].at[_half(g.shape[0], 1 - c), :]
            outgoing.append(_remote(mine, mine, send.at[t], recv.at[t], sibling))
            incoming.append(_remote(theirs, theirs, send.at[t], recv.at[t], sibling))
        return [], outgoing, incoming


class _GatherSmall(_Exchange):
    def __init__(self, block):
        self.inputs = [block]
        self.out_shape = [jax.ShapeDtypeStruct((N_DEV, *block.shape), block.dtype)]
        self.sems = [DMA_SEMS((7,)), DMA_SEMS((7,)), DMA_SEMS(())]

    def _copies(self, ins, outs, sems):
        send, recv, lsem = sems
        (x, y, c), me, sibling, chips, ids = _place()
        slot = lambda px, py, pc: outs[0].at[4 * px + 2 * py + pc]

        def copy(k, block, to, src=None):
            return _remote(slot(*block) if src is None else src, slot(*block), send.at[k], recv.at[k], to)

        local = [pltpu.make_async_copy(ins[0], slot(x, y, c), lsem)]
        first = [copy(0, (x, y, c), sibling, src=ins[0])] + [copy(1 + j, (x, y, c), (*chip, c), src=ins[0]) for j, chip in enumerate(chips)]
        passed = [copy(4 + j, (*chip, c), sibling) for j, chip in enumerate(chips)]
        landed = [copy(1 + j, (*chip, c), (x, y, c)) for j, chip in enumerate(chips)]
        from_sibling = [copy(0, (x, y, 1 - c), (x, y, c))] + [copy(4 + j, (*chip, 1 - c), (x, y, c)) for j, chip in enumerate(chips)]
        return local, first, (passed, landed, from_sibling)

    def finish(self, ins, outs, sems):
        local, first, (passed, landed, from_sibling) = self._copies(ins, outs, sems)
        for j in range(3):
            landed[j].wait_recv()
            passed[j].start()
        for cp in from_sibling:
            cp.wait_recv()
        for cp in first + passed:
            cp.wait_send()
        for cp in local:
            cp.wait()


def _split_refs(refs, counts):
    out = []
    for n in counts:
        out.append(refs[:n])
        refs = refs[n:]
    return out


def _each_exchange(exchanges, method, x_in, x_out, x_sem):
    for ex in exchanges:
        ni, no, ns = len(ex.inputs), len(ex.out_shape), len(ex.sems)
        getattr(ex, method)(x_in[:ni], x_out[:no], x_sem[:ns])
        x_in, x_out, x_sem = x_in[ni:], x_out[no:], x_sem[ns:]


def _call(body, *, name, grid, in_specs, out_specs, out_shape, operands, semantics, scratch_shapes=(), exchanges=()):
    x_in = [a for ex in exchanges for a in ex.inputs]
    x_out = [s for ex in exchanges for s in ex.out_shape]
    x_sem = [s for ex in exchanges for s in ex.sems]
    counts = (len(in_specs), len(x_in), len(out_specs), len(x_out), len(scratch_shapes), len(x_sem))
    aliases, i0, o0 = {}, len(in_specs), len(out_specs)
    for ex in exchanges:
        aliases.update({i0 + i: o0 + o for i, o in ex.aliases.items()})
        i0, o0 = i0 + len(ex.inputs), o0 + len(ex.out_shape)

    def full_body(*refs):
        ins, xi, outs, xo, scr, xs = _split_refs(list(refs), counts)
        if exchanges:
            @pl.when(functools.reduce(jnp.logical_and, [pl.program_id(a) == 0 for a in range(len(grid))]))
            def _():
                _each_exchange(exchanges, "start", xi, xo, xs)

        body(*ins, *outs, *scr)
        if exchanges:
            @pl.when(functools.reduce(jnp.logical_and, [pl.program_id(a) == grid[a] - 1 for a in range(len(grid))]))
            def _():
                _each_exchange(exchanges, "finish", xi, xo, xs)

    return pl.pallas_call(
        full_body, name=name, grid=grid,
        in_specs=list(in_specs) + [HBM] * len(x_in), out_specs=list(out_specs) + [HBM] * len(x_out),
        out_shape=list(out_shape) + x_out, scratch_shapes=list(scratch_shapes) + x_sem,
        input_output_aliases=aliases, compiler_params=_params(semantics),
    )(*operands, *x_in)


def _run_exchanges(exchanges, name):
    x_in = [a for ex in exchanges for a in ex.inputs]
    x_out = [s for ex in exchanges for s in ex.out_shape]
    x_sem = [s for ex in exchanges for s in ex.sems]
    aliases, i0, o0 = {}, 0, 0
    for ex in exchanges:
        aliases.update({i0 + i: o0 + o for i, o in ex.aliases.items()})
        i0, o0 = i0 + len(ex.inputs), o0 + len(ex.out_shape)

    def body(*refs):
        xi, xo, xs = _split_refs(list(refs), (len(x_in), len(x_out), len(x_sem)))
        _each_exchange(exchanges, "start", xi, xo, xs)
        _each_exchange(exchanges, "finish", xi, xo, xs)

    return pl.pallas_call(
        body, name=name, in_specs=[HBM] * len(x_in), out_specs=[HBM] * len(x_out), out_shape=x_out,
        scratch_shapes=x_sem, input_output_aliases=aliases,
    )(*x_in)


def _inproj_fwd(x, g1, w_in_all, exchanges=()):
    L, D = x.shape
    ns, _, nc = w_in_all.shape
    tm = _tile(L, TM_PROJ)

    def body(x_ref, g_ref, w_ref, hn_ref, proj_ref, u_ref):
        hn = _rms_fwd(x_ref[...], g_ref[...]).astype(MXU_DTYPE)
        hn_ref[...] = hn
        for j in range(ns):
            proj_ref[:, j * nc:(j + 1) * nc] = _dot(hn, w_ref[j])
        _store_slabs(u_ref, proj_ref[:, 0:nc])

    return _call(
        body, name="inproj_fwd", grid=(L // tm,), exchanges=exchanges, semantics=("arbitrary",), operands=(x, g1, w_in_all),
        in_specs=[pl.BlockSpec((tm, D), lambda i: (i, 0)), _resident((1, D)), _resident(w_in_all.shape)],
        out_specs=[pl.BlockSpec((tm, D), lambda i: (i, 0)), pl.BlockSpec((tm, ns * nc), lambda i: (i, 0)), _slab_spec(nc, tm)],
        out_shape=[jax.ShapeDtypeStruct((L, D), MXU_DTYPE), jax.ShapeDtypeStruct((L, ns * nc), F32), _slab_shape(L, nc)],
    )


def _slab_shape(L, n):
    return jax.ShapeDtypeStruct((n // LANES, L, LANES), F32)


def _slab_spec(n, tm, index=lambda i: (0, i, 0)):
    return pl.BlockSpec((n // LANES, tm, LANES), index)


def _store_slabs(ref, value):
    for k in range(ref.shape[0]):
        ref[k] = value[:, k * LANES:(k + 1) * LANES]


def _load_slabs(ref):
    return jnp.concatenate([ref[k] for k in range(ref.shape[0])], axis=1)


SEG_ROWS = SUBLANES * SUBLANES


def _load_permuted(ref):
    tm = ref.shape[1]
    slabs = []
    for k in range(ref.shape[0]):
        tiles = [ref.at[k][pl.ds(b * SEG_ROWS + j, SUBLANES, stride=SUBLANES), :] for b in range(tm // SEG_ROWS) for j in range(SUBLANES)]
        slabs.append(jnp.concatenate(tiles, axis=0))
    return jnp.concatenate(slabs, axis=1)


def _store_permuted(ref, value):
    tm = ref.shape[1]
    for k in range(ref.shape[0]):
        for b in range(tm // SEG_ROWS):
            for j in range(SUBLANES):
                r = b * SEG_ROWS + j * SUBLANES
                ref.at[k][pl.ds(b * SEG_ROWS + j, SUBLANES, stride=SUBLANES), :] = value[r:r + SUBLANES, k * LANES:(k + 1) * LANES]


def _scan_tile(xr, xi, hr, hi, coef_ref, lanes, reverse):
    for k, j in ((1, 0), (2, 2), (4, 4)):
        ar = coef_ref[j, :, lanes]
        ai = coef_ref[j + 1, :, lanes]
        shift = SUBLANES - k if reverse else k
        sr = pltpu.roll(xr, shift, 0)
        si = pltpu.roll(xi, shift, 0)
        xr, xi = xr + (ar * sr - ai * si), xi + (ar * si + ai * sr)
    pr = coef_ref[6, :, lanes]
    pi = coef_ref[7, :, lanes]
    return xr + (pr * hr - pi * hi), xi + (pr * hi + pi * hr)


def _scan_block(read, write, hr, hi, coef_ref, lanes, reverse):
    order = list(range(SUBLANES - 1, -1, -1) if reverse else range(SUBLANES))
    near = 8 + 2 * order[0]
    ar = coef_ref[near, :, lanes]
    ai = coef_ref[near + 1, :, lanes]
    xr, xi = read(order[0])
    local = {order[0]: (xr, xi)}
    for j in order[1:]:
        br, bi = read(j)
        xr, xi = br + (ar * xr - ai * xi), bi + (ar * xi + ai * xr)
        local[j] = (xr, xi)
    er, ei = _scan_tile(xr, xi, hr, hi, coef_ref, lanes, reverse)
    edge = lax.broadcasted_iota(jnp.int32, er.shape, 0) == (SUBLANES - 1 if reverse else 0)
    shift = SUBLANES - 1 if reverse else 1
    pr = jnp.where(edge, hr, pltpu.roll(er, shift, 0))
    pi = jnp.where(edge, hi, pltpu.roll(ei, shift, 0))
    for j in range(SUBLANES):
        cr = coef_ref[8 + 2 * j, :, lanes]
        ci = coef_ref[9 + 2 * j, :, lanes]
        xr, xi = local[j]
        write(j, xr + (cr * pr - ci * pi), xi + (cr * pi + ci * pr))
    end = 0 if reverse else SUBLANES - 1
    return jnp.broadcast_to(er[end:end + 1, :], er.shape), jnp.broadcast_to(ei[end:end + 1, :], ei.shape)


def _s5_fwd(u4, bmat, cmat, coef, dskip, exchanges=()):
    L = u4.shape[1]
    tm = _tile(L, TM_S5)
    lc = min(LANE_CHUNK, WB)

    def body(u_ref, bm_ref, cm_ref, coef_ref, d_ref, sre_ref, sim_ref, ys_ref, hr_ref, hi_ref):
        @pl.when(pl.program_id(1) == 0)
        def _():
            hr_ref[...] = jnp.zeros_like(hr_ref)
            hi_ref[...] = jnp.zeros_like(hi_ref)

        u = _load_permuted(u_ref)
        bu = _dot(u.astype(MXU_DTYPE), bm_ref[0])
        sre_ref[...] = bu[:, :WB]
        sim_ref[...] = bu[:, WB:]
        for c in range(WB // lc):
            lanes = slice(c * lc, (c + 1) * lc)
            hr, hi = hr_ref[:, lanes], hi_ref[:, lanes]
            for b in range(tm // SEG_ROWS):
                rows = lambda j, b=b: slice(b * SEG_ROWS + j * SUBLANES, b * SEG_ROWS + (j + 1) * SUBLANES)

                def read(j, rows=rows, lanes=lanes):
                    return sre_ref[rows(j), lanes], sim_ref[rows(j), lanes]

                def write(j, xr, xi, rows=rows, lanes=lanes):
                    sre_ref[rows(j), lanes] = xr
                    sim_ref[rows(j), lanes] = xi

                hr, hi = _scan_block(read, write, hr, hi, coef_ref, lanes, False)
            hr_ref[:, lanes] = hr
            hi_ref[:, lanes] = hi
        ys = _dot_nt(sre_ref[...].astype(MXU_DTYPE), cm_ref[0, :, :WB]) + _dot_nt(sim_ref[...].astype(MXU_DTYPE), cm_ref[0, :, WB:])
        _store_permuted(ys_ref, ys + d_ref[0] * u)

    return _call(
        body, name="s5_fwd", grid=(N_GBLK, L // tm), exchanges=exchanges, semantics=("arbitrary", "arbitrary"),
        operands=(u4, bmat, cmat, coef, dskip),
        in_specs=[
            _slab_spec(UB, tm, lambda b, i: (b, i, 0)),
            pl.BlockSpec((1, UB, 2 * WB), lambda b, i: (b, 0, 0)),
            pl.BlockSpec((1, UB, 2 * WB), lambda b, i: (b, 0, 0)),
            pl.BlockSpec((N_TABLES, SUBLANES, WB), lambda b, i: (0, 0, b)),
            pl.BlockSpec((1, 1, UB), lambda b, i: (b, 0, 0)),
        ],
        out_specs=[
            pl.BlockSpec((tm, WB), lambda b, i: (i, b)),
            pl.BlockSpec((tm, WB), lambda b, i: (i, b)),
            _slab_spec(UB, tm, lambda b, i: (b, i, 0)),
        ],
        out_shape=[
            jax.ShapeDtypeStruct((L, N_GBLK * WB), F32),
            jax.ShapeDtypeStruct((L, N_GBLK * WB), F32),
            _slab_shape(L, D_SSM),
        ],
        scratch_shapes=[pltpu.VMEM((SUBLANES, WB), F32), pltpu.VMEM((SUBLANES, WB), F32)],
    )


def _tail_fwd(x, ys, proj, w_glu, conv_w, g_ssm, g_conv, w_out, g_post, exchanges=()):
    L, D = x.shape
    tm = _tile(L, TM_TAIL)

    def body(x_ref, ys_ref, h_ref, bg_ref, cg_ref, wglu_ref, cw_ref, gs_ref, gc_ref, wout_ref, gp_ref,
             ycat_ref, o_ref, x1_ref, zbuf):
        @pl.when(pl.program_id(0) == 0)
        def _():
            zbuf[0:SUBLANES, :] = jnp.zeros((SUBLANES, D_CONV), F32)

        y1, _ = _gelu(_load_slabs(ys_ref))
        y2 = y1 * jax.nn.sigmoid(_dot(y1.astype(MXU_DTYPE), wglu_ref[...]))
        ycat_ref[:, :D_SSM] = _rms_fwd(y2, gs_ref[...]).astype(MXU_DTYPE)
        z = cg_ref[...] * h_ref[...]
        zbuf[SUBLANES:, :] = z
        conv = cw_ref[0:1, :] * zbuf[SUBLANES - 2:SUBLANES - 2 + tm, :] + cw_ref[1:2, :] * zbuf[SUBLANES - 1:SUBLANES - 1 + tm, :] + cw_ref[2:3, :] * z
        zbuf[0:SUBLANES, :] = zbuf[tm:tm + SUBLANES, :]
        ycat_ref[:, D_SSM:] = _rms_fwd(bg_ref[...] * conv, gc_ref[...]).astype(MXU_DTYPE)
        o = _dot(ycat_ref[...], wout_ref[...])
        o_ref[...] = o
        x1_ref[...] = x_ref[...] + _rms_fwd(o, gp_ref[...])

    row = lambda i: (i, 0)
    return _call(
        body, name="tail_fwd", grid=(L // tm,), exchanges=exchanges, semantics=("arbitrary",),
        operands=(x, ys, proj, proj, proj, w_glu, conv_w, g_ssm, g_conv, w_out, g_post),
        in_specs=[
            pl.BlockSpec((tm, D), row), _slab_spec(D_SSM, tm),
            pl.BlockSpec((tm, D_CONV), lambda i: (i, 1)), pl.BlockSpec((tm, D_CONV), lambda i: (i, 2)),
            pl.BlockSpec((tm, D_CONV), lambda i: (i, 3)),
            _resident(w_glu.shape), _resident(conv_w.shape), _resident(g_ssm.shape), _resident(g_conv.shape),
            _resident(w_out.shape), _resident(g_post.shape),
        ],
        out_specs=[pl.BlockSpec((tm, D), row), pl.BlockSpec((tm, D), row), pl.BlockSpec((tm, D), row)],
        out_shape=[jax.ShapeDtypeStruct((L, D), MXU_DTYPE), jax.ShapeDtypeStruct((L, D), F32), jax.ShapeDtypeStruct((L, D), F32)],
        scratch_shapes=[pltpu.VMEM((tm + SUBLANES, D_CONV), F32)],
    )


def _mlp_fwd(x1, target, w_up_all, w_down_a, w_down_b, g_pre, g_post):
    L, D = x1.shape
    ns, _, fc = w_up_all.shape
    half = w_down_a.shape[1]
    tm = _tile(L, TM_MLP)

    def body(x1_ref, t_ref, wup_ref, wda_ref, wdb_ref, gpre_ref, gpost_ref, hn2_ref, up_ref, act_ref, m_ref, dx2_ref, loss_ref):
        @pl.when(pl.program_id(0) == 0)
        def _():
            loss_ref[...] = jnp.zeros_like(loss_ref)

        x1v = x1_ref[...]
        hn2 = _rms_fwd(x1v, gpre_ref[...]).astype(MXU_DTYPE)
        hn2_ref[...] = hn2
        m = jnp.zeros((tm, D), F32)
        for j in range(ns):
            up = _dot(hn2, wup_ref[j])
            up_ref[:, j * fc:(j + 1) * fc] = up.astype(MXU_DTYPE)
            act = jnp.square(jnp.maximum(up, 0.0)).astype(MXU_DTYPE)
            act_ref[:, j * fc:(j + 1) * fc] = act
            m = m + _dot(act[:, :half], wda_ref[j]) + _dot(act[:, half:], wdb_ref[j])
        m_ref[...] = m
        err = x1v + _rms_fwd(m, gpost_ref[...]) - t_ref[...]
        loss_ref[...] += 0.5 * jnp.sum(jnp.mean(err * err, axis=-1, keepdims=True))
        dx2_ref[...] = err * (1.0 / D)

    row = lambda i: (i, 0)
    return pl.pallas_call(
        body, name="mlp_fwd", grid=(L // tm,),
        in_specs=[pl.BlockSpec((tm, D), row), pl.BlockSpec((tm, D), row), _resident(w_up_all.shape), _resident(w_down_a.shape),
                  _resident(w_down_b.shape), _resident(g_pre.shape), _resident(g_post.shape)],
        out_specs=[pl.BlockSpec((tm, D), row), pl.BlockSpec((tm, ns * fc), row), pl.BlockSpec((tm, ns * fc), row), pl.BlockSpec((tm, D), row),
                   pl.BlockSpec((tm, D), row), pl.BlockSpec((SUBLANES, LANES), lambda i: (0, 0))],
        out_shape=[jax.ShapeDtypeStruct((L, D), MXU_DTYPE), jax.ShapeDtypeStruct((L, ns * fc), MXU_DTYPE), jax.ShapeDtypeStruct((L, ns * fc), MXU_DTYPE),
                   jax.ShapeDtypeStruct((L, D), F32), jax.ShapeDtypeStruct((L, D), F32), jax.ShapeDtypeStruct((SUBLANES, LANES), F32)],
        compiler_params=_params(("arbitrary",)),
    )(x1, target, w_up_all, w_down_a, w_down_b, g_pre, g_post)


def _mlp_bwd(dx2, m, up, x1, w_up_all, w_down_a, w_down_b, g_pre, g_post):
    L, D = x1.shape
    ns, _, fc = w_up_all.shape
    tm = _tile(L, TM_MLP)

    def body(dx2_ref, m_ref, up_ref, x1_ref, wup_ref, wda_ref, wdb_ref, gpre_ref, gpost_ref,
             dm_ref, dup_ref, dx1_ref, dgpost_ref, dgpre_ref):
        @pl.when(pl.program_id(0) == 0)
        def _():
            dgpost_ref[...] = jnp.zeros_like(dgpost_ref)
            dgpre_ref[...] = jnp.zeros_like(dgpre_ref)

        dx2v = dx2_ref[...]
        dm, dg = _rms_bwd(m_ref[...], gpost_ref[...], dx2v)
        dgpost_ref[...] += dg
        dm_b = dm.astype(MXU_DTYPE)
        dm_ref[...] = dm_b
        dhn2 = jnp.zeros((tm, D), F32)
        for j in range(ns):
            cols = slice(j * fc, (j + 1) * fc)
            relu = jnp.maximum(up_ref[:, cols].astype(F32), 0.0)
            dact = jnp.concatenate([_dot_nt(dm_b, wda_ref[j]), _dot_nt(dm_b, wdb_ref[j])], axis=1)
            dup = (dact * (2.0 * relu)).astype(MXU_DTYPE)
            dup_ref[:, cols] = dup
            dhn2 = dhn2 + _dot_nt(dup, wup_ref[j])
        dx, dg = _rms_bwd(x1_ref[...], gpre_ref[...], dhn2)
        dgpre_ref[...] += dg
        dx1_ref[...] = dx2v + dx

    row = lambda i: (i, 0)
    vec = pl.BlockSpec((1, D), lambda i: (0, 0))
    return pl.pallas_call(
        body, name="mlp_bwd", grid=(L // tm,),
        in_specs=[pl.BlockSpec((tm, D), row), pl.BlockSpec((tm, D), row), pl.BlockSpec((tm, ns * fc), row), pl.BlockSpec((tm, D), row),
                  _resident(w_up_all.shape), _resident(w_down_a.shape), _resident(w_down_b.shape), _resident(g_pre.shape), _resident(g_post.shape)],
        out_specs=[pl.BlockSpec((tm, D), row), pl.BlockSpec((tm, ns * fc), row), pl.BlockSpec((tm, D), row), vec, vec],
        out_shape=[jax.ShapeDtypeStruct((L, D), MXU_DTYPE), jax.ShapeDtypeStruct((L, ns * fc), MXU_DTYPE), jax.ShapeDtypeStruct((L, D), F32),
                   jax.ShapeDtypeStruct((1, D), F32), jax.ShapeDtypeStruct((1, D), F32)],
        compiler_params=_params(("arbitrary",)),
    )(dx2, m, up, x1, w_up_all, w_down_a, w_down_b, g_pre, g_post)


def _tail_bwd(dx1, o, ys, proj, w_glu, conv_w, g_ssm, g_conv, w_out, g_post, exchanges=()):
    L, D = dx1.shape
    tm = _tile(L, TM_TAIL)
    nt = L // tm
    hb = tm // SUBLANES

    def body(dx1_ref, o_ref, ys_ref, h_ref, bg_ref, cg_ref, hh_ref, hcg_ref, wglu_ref, cw_ref, gs_ref, gc_ref, wout_ref, gp_ref,
             do_ref, da_ref, y1_ref, dys_ref, dhbc_ref, dgp_ref, dgs_ref, dgc_ref, dcw_ref, zbuf, dcbuf):
        step = pl.program_id(0)

        @pl.when(step == 0)
        def _():
            dcbuf[tm:, :] = jnp.zeros((SUBLANES, D_CONV), F32)
            dgp_ref[...] = jnp.zeros_like(dgp_ref)
            dgs_ref[...] = jnp.zeros_like(dgs_ref)
            dgc_ref[...] = jnp.zeros_like(dgc_ref)
            dcw_ref[...] = jnp.zeros_like(dcw_ref)

        do, dg = _rms_bwd(o_ref[...], gp_ref[...], dx1_ref[...])
        dgp_ref[...] += dg
        do_b = do.astype(MXU_DTYPE)
        do_ref[...] = do_b
        dycat = _dot_nt(do_b, wout_ref[...])
        y1, dgelu = _gelu(_load_slabs(ys_ref))
        y1_b = y1.astype(MXU_DTYPE)
        y1_ref[...] = y1_b
        s = jax.nn.sigmoid(_dot(y1_b, wglu_ref[...]))
        dy2, dg = _rms_bwd(y1 * s, gs_ref[...], dycat[:, :D_SSM])
        dgs_ref[...] += dg
        da_b = (dy2 * y1 * s * (1.0 - s)).astype(MXU_DTYPE)
        da_ref[...] = da_b
        _store_slabs(dys_ref, (dy2 * s + _dot_nt(da_b, wglu_ref[...])) * dgelu)
        h = h_ref[...]
        cg = cg_ref[...]
        bg = bg_ref[...]
        z = cg * h
        first = step == nt - 1
        zbuf[0:SUBLANES, :] = jnp.where(first, 0.0, hcg_ref[...] * hh_ref[...])
        zbuf[SUBLANES:, :] = z
        z1 = zbuf[SUBLANES - 1:SUBLANES - 1 + tm, :]
        z2 = zbuf[SUBLANES - 2:SUBLANES - 2 + tm, :]
        conv = cw_ref[0:1, :] * z2 + cw_ref[1:2, :] * z1 + cw_ref[2:3, :] * z
        dyc, dg = _rms_bwd(bg * conv, gc_ref[...], dycat[:, D_SSM:])
        dgc_ref[...] += dg
        dconv = dyc * bg
        dcw_ref[0:1, :] += jnp.sum(dconv * z2, axis=0, keepdims=True)
        dcw_ref[1:2, :] += jnp.sum(dconv * z1, axis=0, keepdims=True)
        dcw_ref[2:3, :] += jnp.sum(dconv * z, axis=0, keepdims=True)
        dcbuf[0:tm, :] = dconv
        dz = cw_ref[2:3, :] * dconv + cw_ref[1:2, :] * dcbuf[1:1 + tm, :] + cw_ref[0:1, :] * dcbuf[2:2 + tm, :]
        dcbuf[tm:, :] = dcbuf[0:SUBLANES, :]
        dhbc_ref[:, 0:D_CONV] = (dz * cg).astype(MXU_DTYPE)
        dhbc_ref[:, D_CONV:2 * D_CONV] = (dyc * conv).astype(MXU_DTYPE)
        dhbc_ref[:, 2 * D_CONV:] = (dz * h).astype(MXU_DTYPE)

    rev = lambda i: (nt - 1 - i, 0)
    slab = lambda i: (0, nt - 1 - i, 0)
    col = lambda c: (lambda i: (nt - 1 - i, c))
    halo = lambda c: (lambda i: (jnp.maximum((nt - 1 - i) * hb - 1, 0), c))
    vec = lambda n: pl.BlockSpec((1, n), lambda i: (0, 0))
    return _call(
        body, name="tail_bwd", grid=(nt,), exchanges=exchanges, semantics=("arbitrary",),
        operands=(dx1, o, ys, proj, proj, proj, proj, proj, w_glu, conv_w, g_ssm, g_conv, w_out, g_post),
        in_specs=[
            pl.BlockSpec((tm, D), rev), pl.BlockSpec((tm, D), rev), _slab_spec(D_SSM, tm, slab),
            pl.BlockSpec((tm, D_CONV), col(1)), pl.BlockSpec((tm, D_CONV), col(2)), pl.BlockSpec((tm, D_CONV), col(3)),
            pl.BlockSpec((SUBLANES, D_CONV), halo(1)), pl.BlockSpec((SUBLANES, D_CONV), halo(3)),
            _resident(w_glu.shape), _resident(conv_w.shape), _resident(g_ssm.shape), _resident(g_conv.shape),
            _resident(w_out.shape), _resident(g_post.shape),
        ],
        out_specs=[
            pl.BlockSpec((tm, D), rev), pl.BlockSpec((tm, D_SSM), rev), pl.BlockSpec((tm, D_SSM), rev), _slab_spec(D_SSM, tm, slab),
            pl.BlockSpec((tm, 3 * D_CONV), rev), vec(D), vec(D_SSM), vec(D_CONV),
            pl.BlockSpec((SUBLANES, D_CONV), lambda i: (0, 0)),
        ],
        out_shape=[
            jax.ShapeDtypeStruct((L, D), MXU_DTYPE), jax.ShapeDtypeStruct((L, D_SSM), MXU_DTYPE), jax.ShapeDtypeStruct((L, D_SSM), MXU_DTYPE),
            _slab_shape(L, D_SSM), jax.ShapeDtypeStruct((L, 3 * D_CONV), MXU_DTYPE),
            jax.ShapeDtypeStruct((1, D), F32), jax.ShapeDtypeStruct((1, D_SSM), F32), jax.ShapeDtypeStruct((1, D_CONV), F32),
            jax.ShapeDtypeStruct((SUBLANES, D_CONV), F32),
        ],
        scratch_shapes=[pltpu.VMEM((tm + SUBLANES, D_CONV), F32), pltpu.VMEM((tm + SUBLANES, D_CONV), F32)],
    )


def _s5_bwd(dys, u4, s_re, s_im, bmat, cmat, coef_rev, dskip, exchanges=()):
    L = dys.shape[1]
    tm = _tile(L, TM_S5)
    nt = L // tm
    lc = min(LANE_CHUNK, WB)

    def body(dys_ref, u_ref, sre_ref, sim_ref, bm_ref, cm_ref, coef_ref, d_ref,
             du_ref, gb_ref, gc_ref, q_ref, gd_ref, dr_ref, di_ref, lr_ref, li_ref, hr_ref, hi_ref, qr_acc, qi_acc, gb_acc, gc_acc):
        step = pl.program_id(1)

        @pl.when(step == 0)
        def _():
            for ref in (hr_ref, hi_ref, qr_acc, qi_acc, gb_acc, gc_acc, gd_ref):
                ref[...] = jnp.zeros_like(ref)

        dys_v = _load_permuted(dys_ref)
        u = _load_permuted(u_ref)
        dys_b = dys_v.astype(MXU_DTYPE)
        u_b = u.astype(MXU_DTYPE)
        d = _dot(dys_b, cm_ref[0])
        dr_ref[...] = d[:, :WB]
        di_ref[...] = d[:, WB:]
        for c in range(WB // lc):
            lanes = slice(c * lc, (c + 1) * lc)
            hr, hi = hr_ref[:, lanes], hi_ref[:, lanes]
            q = [qr_acc[:, lanes], qi_acc[:, lanes]]
            for b in range(tm // SEG_ROWS - 1, -1, -1):
                rows = lambda j, b=b: slice(b * SEG_ROWS + j * SUBLANES, b * SEG_ROWS + (j + 1) * SUBLANES)

                def read(j, rows=rows, lanes=lanes):
                    return dr_ref[rows(j), lanes], di_ref[rows(j), lanes]

                def write(j, xr, xi, rows=rows, lanes=lanes, q=q):
                    lr_ref[rows(j), lanes] = xr
                    li_ref[rows(j), lanes] = xi
                    er = xr - dr_ref[rows(j), lanes]
                    ei = xi - di_ref[rows(j), lanes]
                    sr = sre_ref[rows(j), lanes]
                    si = sim_ref[rows(j), lanes]
                    q[0] = q[0] + (er * sr + ei * si)
                    q[1] = q[1] + (ei * sr - er * si)

                hr, hi = _scan_block(read, write, hr, hi, coef_ref, lanes, True)
            hr_ref[:, lanes] = hr
            hi_ref[:, lanes] = hi
            qr_acc[:, lanes] = q[0]
            qi_acc[:, lanes] = q[1]
        lr_b = lr_ref[...].astype(MXU_DTYPE)
        li_b = li_ref[...].astype(MXU_DTYPE)
        _store_permuted(du_ref, _dot_nt(lr_b, bm_ref[0, :, :WB]) + _dot_nt(li_b, bm_ref[0, :, WB:]) + d_ref[0] * dys_v)
        sre_b = sre_ref[...].astype(MXU_DTYPE)
        sim_b = sim_ref[...].astype(MXU_DTYPE)
        for t in range(WB // DIAG_COLS):
            ch = slice(t * DIAG_ROWS, (t + 1) * DIAG_ROWS)
            cols = slice(t * DIAG_COLS, (t + 1) * DIAG_COLS)
            icols = slice(WB + t * DIAG_COLS, WB + (t + 1) * DIAG_COLS)
            gb_acc[ch, cols] += _dot_tn(u_b[:, ch], lr_b[:, cols])
            gb_acc[ch, icols] += _dot_tn(u_b[:, ch], li_b[:, cols])
            gc_acc[ch, cols] += _dot_tn(dys_b[:, ch], sre_b[:, cols])
            gc_acc[ch, icols] += _dot_tn(dys_b[:, ch], sim_b[:, cols])
        gd_ref[0] += jnp.sum(dys_v * u, axis=0, keepdims=True)

        @pl.when(step == nt - 1)
        def _():
            q_ref[0, 0:1, :] = jnp.sum(qr_acc[...], axis=0, keepdims=True)
            q_ref[0, 1:2, :] = jnp.sum(qi_acc[...], axis=0, keepdims=True)
            mask = _group_mask(DIAG_ROWS, DIAG_COLS)
            fold = (lax.broadcasted_iota(jnp.int32, (DIAG_COLS, STATE), 0) % STATE == lax.broadcasted_iota(jnp.int32, (DIAG_COLS, STATE), 1)).astype(F32)
            for acc, out in ((gb_acc, gb_ref), (gc_acc, gc_ref)):
                for k in range(2):
                    for t in range(WB // DIAG_COLS):
                        ch = slice(t * DIAG_ROWS, (t + 1) * DIAG_ROWS)
                        own = jnp.where(mask, acc[ch, k * WB + t * DIAG_COLS:k * WB + (t + 1) * DIAG_COLS], 0.0)
                        out[0, k, ch, :] = jnp.dot(own, fold, precision=lax.Precision.HIGHEST, preferred_element_type=F32)

    rev = lambda b, i: (nt - 1 - i, b)
    slab = lambda b, i: (b, nt - 1 - i, 0)
    blk = lambda b, i: (b, 0, 0)
    blk4 = lambda b, i: (b, 0, 0, 0)
    return _call(
        body, name="s5_bwd", grid=(N_GBLK, nt), exchanges=exchanges, semantics=("arbitrary", "arbitrary"),
        operands=(dys, u4, s_re, s_im, bmat, cmat, coef_rev, dskip),
        in_specs=[
            _slab_spec(UB, tm, slab), _slab_spec(UB, tm, slab), pl.BlockSpec((tm, WB), rev), pl.BlockSpec((tm, WB), rev),
            pl.BlockSpec((1, UB, 2 * WB), blk), pl.BlockSpec((1, UB, 2 * WB), blk),
            pl.BlockSpec((N_TABLES, SUBLANES, WB), lambda b, i: (0, 0, b)), pl.BlockSpec((1, 1, UB), blk),
        ],
        out_specs=[
            _slab_spec(UB, tm, slab), pl.BlockSpec((1, 2, UB, STATE), blk4), pl.BlockSpec((1, 2, UB, STATE), blk4),
            pl.BlockSpec((1, 2, WB), blk), pl.BlockSpec((1, 1, UB), blk),
        ],
        out_shape=[
            _slab_shape(L, D_SSM), jax.ShapeDtypeStruct((N_GBLK, 2, UB, STATE), F32),
            jax.ShapeDtypeStruct((N_GBLK, 2, UB, STATE), F32), jax.ShapeDtypeStruct((N_GBLK, 2, WB), F32),
            jax.ShapeDtypeStruct((N_GBLK, 1, UB), F32),
        ],
        scratch_shapes=[pltpu.VMEM((tm, WB), F32)] * 4 + [pltpu.VMEM((SUBLANES, WB), F32)] * 4 + [pltpu.VMEM((UB, 2 * WB), F32)] * 2,
    )


def _inproj_bwd(du, dhbc, x, dx1, w_in_all, g1):
    L, D = x.shape
    ns, _, nc = w_in_all.shape
    tm = _tile(L, TM_PROJ)

    def body(du_ref, dhbc_ref, x_ref, dx1_ref, w_ref, g_ref, gx_ref, dproj_ref, dg_ref):
        @pl.when(pl.program_id(0) == 0)
        def _():
            dg_ref[...] = jnp.zeros_like(dg_ref)

        du_b = _load_slabs(du_ref).astype(MXU_DTYPE)
        dproj_ref[:, :nc] = du_b
        dproj_ref[:, nc:] = dhbc_ref[...]
        dhn = _dot_nt(du_b, w_ref[0])
        for j in range(1, ns):
            dhn = dhn + _dot_nt(dhbc_ref[:, (j - 1) * nc:j * nc], w_ref[j])
        dx, dg = _rms_bwd(x_ref[...], g_ref[...], dhn)
        dg_ref[...] += dg
        gx_ref[...] = dx1_ref[...] + dx

    row = lambda i: (i, 0)
    return pl.pallas_call(
        body, name="inproj_bwd", grid=(L // tm,),
        in_specs=[_slab_spec(nc, tm), pl.BlockSpec((tm, (ns - 1) * nc), row), pl.BlockSpec((tm, D), row), pl.BlockSpec((tm, D), row),
                  _resident(w_in_all.shape), _resident(g1.shape)],
        out_specs=[pl.BlockSpec((tm, D), row), pl.BlockSpec((tm, ns * nc), row), pl.BlockSpec((1, D), lambda i: (0, 0))],
        out_shape=[jax.ShapeDtypeStruct((L, D), F32), jax.ShapeDtypeStruct((L, ns * nc), MXU_DTYPE), jax.ShapeDtypeStruct((1, D), F32)],
        compiler_params=_params(("arbitrary",)),
    )(du, dhbc, x, dx1, w_in_all, g1)


def _matmul_tn(a, b, name, col_shards=1, exchanges=()):
    L, K = a.shape
    N = b.shape[1]
    tl = _tile(L, TL_TN)
    tk = _tile(K, TK_TN)
    nw = N // col_shards
    spb = max(1, min(col_shards, TN_TN // nw))
    tn = spb * nw if spb > 1 else _tile(nw, TK_TN)
    npb = nw // tn if spb == 1 else 1

    def body(a_ref, b_ref, o_ref):
        @pl.when(pl.program_id(2) == 0)
        def _():
            o_ref[...] = jnp.zeros_like(o_ref)

        res = _dot_tn(a_ref[...], b_ref[...])
        for s in range(spb):
            o_ref[s] += res[:, s * nw:(s + 1) * nw] if spb > 1 else res

    out_block = pl.BlockSpec((spb, tk, nw if spb > 1 else tn), (lambda k, n, l: (n, k, 0)) if spb > 1 else (lambda k, n, l: (n // npb, k, n % npb)))
    return _call(
        body, name=name, grid=(K // tk, N // tn, L // tl), exchanges=exchanges, semantics=("arbitrary", "arbitrary", "arbitrary"),
        operands=(a, b),
        in_specs=[pl.BlockSpec((tl, tk), lambda k, n, l: (l, k)), pl.BlockSpec((tl, tn), lambda k, n, l: (l, n))],
        out_specs=[out_block],
        out_shape=[jax.ShapeDtypeStruct((col_shards, K, nw), F32)],
    )


def _ssm_discretize(lam_re, lam_im, log_dt, bt_re, bt_im):
    dt = jnp.exp(log_dt)[:, None]
    zr = lam_re * dt
    zi = lam_im * dt
    mag = jnp.exp(zr)
    abr = mag * jnp.cos(zi)
    abi = mag * jnp.sin(zi)
    nr, ni = abr - 1.0, abi
    den = lam_re * lam_re + lam_im * lam_im
    coef_r = ((nr * lam_re + ni * lam_im) / den)[:, None, :]
    coef_i = ((ni * lam_re - nr * lam_im) / den)[:, None, :]
    return zr, zi, coef_r * bt_re - coef_i * bt_im, coef_r * bt_im + coef_i * bt_re


def _scan_tables(ar, ai, reverse):
    rows = np.arange(SUBLANES)
    exps = np.zeros((N_TABLES // 2, SUBLANES), np.int32)
    keep = np.ones((N_TABLES // 2, SUBLANES), bool)
    for t, k in enumerate((1, 2, 4)):
        exps[t] = SUBLANES * k
        keep[t] = (rows + k <= SUBLANES - 1) if reverse else (rows >= k)
    exps[3] = SUBLANES * (SUBLANES - rows) if reverse else SUBLANES * (rows + 1)
    for j in range(SUBLANES):
        exps[4 + j] = SUBLANES - j if reverse else j + 1
    pr, pi = ar, (-ai if reverse else ai)
    shape = (N_TABLES // 2, SUBLANES, ar.shape[0])
    xr, xi = jnp.ones(shape, F32), jnp.zeros(shape, F32)
    for bit in range(int(exps.max()).bit_length()):
        on = ((exps >> bit) & 1).astype(bool)[:, :, None]
        xr, xi = jnp.where(on, xr * pr - xi * pi, xr), jnp.where(on, xr * pi + xi * pr, xi)
        pr, pi = pr * pr - pi * pi, 2.0 * pr * pi
    xr = jnp.where(keep[:, :, None], xr, 0.0)
    xi = jnp.where(keep[:, :, None], xi, 0.0)
    return jnp.stack([xr, xi], axis=1).reshape(N_TABLES, SUBLANES, ar.shape[0])


def _group_mask(rows, cols):
    r = lax.broadcasted_iota(jnp.int32, (rows, cols), 0) // GROUP
    c = lax.broadcasted_iota(jnp.int32, (rows, cols), 1) // STATE
    return r == c


def _ssm_expand(bt_re, bt_im, c_re, c_im, exchanges=()):
    flat = lambda a: a.reshape(N_GROUPS * GROUP, STATE)

    def body(br_ref, bi_ref, cr_ref, ci_ref, bm_ref, cm_ref):
        spread = (lax.broadcasted_iota(jnp.int32, (STATE, WB), 1) % STATE == lax.broadcasted_iota(jnp.int32, (STATE, WB), 0)).astype(F32)
        mask = _group_mask(UB, WB)

        def expand(x):
            wide = jnp.dot(x, spread, precision=lax.Precision.HIGHEST, preferred_element_type=F32)
            return jnp.where(mask, wide, 0.0).astype(MXU_DTYPE)

        bm_ref[0, :, :WB] = expand(br_ref[...])
        bm_ref[0, :, WB:] = expand(bi_ref[...])
        cm_ref[0, :, :WB] = expand(cr_ref[...])
        cm_ref[0, :, WB:] = expand(-ci_ref[...])

    spec = pl.BlockSpec((UB, STATE), lambda b: (b, 0))
    out = pl.BlockSpec((1, UB, 2 * WB), lambda b: (b, 0, 0))
    return _call(
        body, name="ssm_expand", grid=(N_GBLK,), exchanges=exchanges, semantics=("arbitrary",),
        operands=(flat(bt_re), flat(bt_im), flat(c_re), flat(c_im)), in_specs=[spec] * 4, out_specs=[out, out],
        out_shape=[jax.ShapeDtypeStruct((N_GBLK, UB, 2 * WB), MXU_DTYPE)] * 2,
    )


def _ssm_matrices(lam_re, lam_im, log_dt, bt_re, bt_im, c_re, c_im, exchanges=()):
    zr, zi, bbar_r, bbar_i = _ssm_discretize(lam_re, lam_im, log_dt, bt_re, bt_im)
    mag = jnp.exp(zr)
    ar = (mag * jnp.cos(zi)).reshape(-1)
    ai = (mag * jnp.sin(zi)).reshape(-1)
    bmat, cmat, *rest = _ssm_expand(bbar_r, bbar_i, c_re, c_im, exchanges)
    return bmat, cmat, _scan_tables(ar, ai, False), _scan_tables(ar, ai, True), rest


def _ssm_param_grads(lam_re, lam_im, log_dt, bt_re, bt_im, gb, gc, q, gd):
    part = lambda g, k: g[:, k].reshape(N_GROUPS, GROUP, STATE)
    qr = q[:, 0, :].reshape(N_GROUPS, STATE)
    qi = q[:, 1, :].reshape(N_GROUPS, STATE)
    _, vjp = jax.vjp(_ssm_discretize, lam_re, lam_im, log_dt, bt_re, bt_im)
    d_lam_re, d_lam_im, d_log_dt, d_bt_re, d_bt_im = vjp((qr, qi, part(gb, 0), part(gb, 1)))
    return d_lam_re, d_lam_im, d_log_dt, d_bt_re, d_bt_im, part(gc, 0), -part(gc, 1), gd.reshape(N_GROUPS, GROUP)


def _row_tile(rows, n):
    return _tile(rows, max(SUBLANES, (2 * 1024 * 1024) // (4 * n)))


def _pair_add(grad, other, core, name):
    ns, h, n = other.shape
    tr = _row_tile(h, n)
    nb = h // tr

    def body(c_ref, g_ref, o_ref, out_ref):
        out_ref[...] = (g_ref[...] + o_ref[...]).astype(WIRE_DTYPE)

    return pl.pallas_call(
        body, name=name,
        grid_spec=pltpu.PrefetchScalarGridSpec(
            num_scalar_prefetch=1, grid=(ns, nb),
            in_specs=[pl.BlockSpec((1, tr, n), lambda s, i, c: (s, c[0] * nb + i, 0)), pl.BlockSpec((1, tr, n), lambda s, i, c: (s, i, 0))],
            out_specs=pl.BlockSpec((1, tr, n), lambda s, i, c: (s, i, 0))),
        out_shape=jax.ShapeDtypeStruct(other.shape, WIRE_DTYPE),
        compiler_params=_params(("arbitrary", "arbitrary")),
    )(core, grad, other)


def _quad_sum(parts, core, name):
    ns, h, n = parts.shape
    tr = _row_tile(h, n)
    nb = h // tr

    def body(c_ref, p_ref, out_ref):
        p = [p_ref[k].astype(F32) for k in range(ns)]
        out_ref[...] = ((p[0] + p[1]) + p[2]) + p[3]

    return pl.pallas_call(
        body, name=name,
        grid_spec=pltpu.PrefetchScalarGridSpec(
            num_scalar_prefetch=1, grid=(nb,),
            in_specs=[pl.BlockSpec((ns, tr, n), lambda i, c: (0, i, 0))],
            out_specs=pl.BlockSpec((tr, n), lambda i, c: (c[0] * nb + i, 0))),
        out_shape=jax.ShapeDtypeStruct((2 * h, n), F32),
        compiler_params=_params(("arbitrary",)),
    )(core, parts)


def _adamw_math(w, g, m, v):
    m = ADAM_B1 * m + (1.0 - ADAM_B1) * g
    v = ADAM_B2 * v + (1.0 - ADAM_B2) * jnp.square(g)
    m_hat = m / (1.0 - ADAM_B1 ** ADAM_STEP)
    v_hat = v / (1.0 - ADAM_B2 ** ADAM_STEP)
    delta = -ADAM_LR * (m_hat / (jnp.sqrt(v_hat) + ADAM_EPS) + ADAM_WD * w)
    return delta, m, v


def _adamw(w, g, m, v, name):
    r, n = w.shape
    tr = _row_tile(r, n)

    def body(w_ref, g_ref, m_ref, v_ref, d_ref, nm_ref, nv_ref):
        d_ref[...], nm_ref[...], nv_ref[...] = _adamw_math(w_ref[...], g_ref[...], m_ref[...], v_ref[...])

    spec = pl.BlockSpec((tr, n), lambda i: (i, 0))
    return pl.pallas_call(
        body, name=name, grid=(r // tr,), in_specs=[spec] * 4, out_specs=[spec] * 3,
        out_shape=[jax.ShapeDtypeStruct((r, n), F32)] * 3,
        compiler_params=_params(("arbitrary",)),
    )(w, g, m, v)


LANES = 128
SMALL = ["g_pre_mix", "lam_re", "lam_im", "log_dt", "b_re", "b_im", "c_re", "c_im", "d_skip", "conv_w", "g_ssm_out", "g_conv_out",
         "g_post_mix", "g_pre_mlp", "g_post_mlp"]
TILE_SLOTS = {"b_re": (0, N_GROUPS), "b_im": (N_GROUPS, N_GROUPS), "c_re": (2 * N_GROUPS, N_GROUPS), "c_im": (3 * N_GROUPS, N_GROUPS),
              "lam_re": (4 * N_GROUPS, 2), "lam_im": (4 * N_GROUPS + 2, 2)}
N_TILE_SLOTS = 4 * N_GROUPS + 4
VEC_ROWS = {"g_pre_mix": 0, "g_post_mix": 1, "g_pre_mlp": 2, "g_post_mlp": 3, "g_ssm_out": 4, "g_conv_out": 5, "log_dt": 6}
ROW_LOSS, ROW_DSKIP, ROW_CONV, N_PACK_ROWS = 7, 8, 24, 32


def _kernel_form(name, a):
    if name in ("b_re", "b_im"):
        return jnp.transpose(a, (0, 1, 3, 2)).reshape(N_GROUPS, GROUP, STATE)
    if name in ("c_re", "c_im"):
        return a.reshape(N_GROUPS, GROUP, STATE)
    if name in ("lam_re", "lam_im"):
        return a.reshape(2, GROUP, STATE)
    if name == "d_skip":
        return jnp.transpose(a, (0, 2, 1)).reshape(GROUP, N_GROUPS)
    if name == "conv_w":
        return jnp.transpose(a, (1, 0, 2))
    return a


def _param_form(name, k):
    if name in ("b_re", "b_im"):
        return jnp.transpose(k.reshape(1, N_GROUPS, GROUP, STATE), (0, 1, 3, 2))
    if name in ("c_re", "c_im"):
        return k.reshape(1, N_GROUPS, GROUP, STATE)
    if name in ("lam_re", "lam_im"):
        return k.reshape(1, N_GROUPS, STATE)
    if name == "d_skip":
        return jnp.transpose(k.reshape(1, GROUP, N_GROUPS), (0, 2, 1))
    if name == "conv_w":
        return jnp.transpose(k, (1, 0, 2))
    return k


def _pack_tiles(g):
    lam = lambda a: a.reshape(2, GROUP, STATE)
    tiles = jnp.concatenate([g["b_re"], g["b_im"], g["c_re"], g["c_im"], lam(g["lam_re"]), lam(g["lam_im"])], axis=0)
    return tiles.astype(WIRE_DTYPE)


def _pack_rows(g, loss):
    row = lambda a: jnp.pad(a, ((0, 0), (0, D_MODEL - a.shape[1])))
    rows = [row(g[k][None]) for k in VEC_ROWS] + [row(loss[0:1]), row(g["d_skip"].T), row(g["conv_w"])]
    rows.append(jnp.zeros((N_PACK_ROWS - ROW_CONV - 3, D_MODEL), F32))
    return jnp.concatenate(rows, axis=0)


def _adamw_small(tiles, rows, w, m, v):
    nn = len(SMALL)

    def body(*refs):
        t_ref, r_ref = refs[0], refs[1]
        w_refs, m_refs, v_refs = refs[2:2 + nn], refs[2 + nn:2 + 2 * nn], refs[2 + 2 * nn:2 + 3 * nn]
        loss_ref, outs = refs[2 + 3 * nn], refs[3 + 3 * nn:]

        def tile_sum(first, count):
            total = t_ref[0, first:first + count].astype(F32)
            for d in range(1, N_DEV):
                total = total + t_ref[d, first:first + count].astype(F32)
            return total

        def row_sum(first, count, lanes):
            total = r_ref[0, first:first + count, 0:lanes]
            for d in range(1, N_DEV):
                total = total + r_ref[d, first:first + count, 0:lanes]
            return total

        def step(j, g, at=lambda ref: ref):
            delta, nm, nv = _adamw_math(at(w_refs[j])[...], g, at(m_refs[j])[...], at(v_refs[j])[...])
            at(outs[j])[...] = g
            at(outs[nn + j])[...] = delta
            at(outs[2 * nn + j])[...] = nm
            at(outs[3 * nn + j])[...] = nv

        loss_ref[...] = row_sum(ROW_LOSS, 1, LANES)
        chip = 2 * lax.axis_index("x") + lax.axis_index("y")
        for j, name in enumerate(SMALL):
            if name in TILE_SLOTS:
                step(j, tile_sum(*TILE_SLOTS[name]))
            elif name == "d_skip":
                step(j, row_sum(ROW_DSKIP, GROUP, N_GROUPS))
            elif name == "conv_w":
                full = row_sum(ROW_CONV, 3, D_CONV)
                mine = full[:, 0:LANES]
                for s in range(1, N_CHIPS):
                    mine = jnp.where(chip == s, full[:, s * LANES:(s + 1) * LANES], mine)
                for k in range(3):
                    step(j, mine[k:k + 1, :], at=lambda ref, k=k: ref.at[k])
            else:
                step(j, row_sum(VEC_ROWS[name], 1, w_refs[j].shape[1]))

    args = [tiles, rows] + [w[k] for k in SMALL] + [m[k] for k in SMALL] + [v[k] for k in SMALL]
    res = pl.pallas_call(
        body, name="adamw_small", in_specs=[VMEM] * len(args), out_specs=[VMEM] * (1 + 4 * nn),
        out_shape=[jax.ShapeDtypeStruct((1, LANES), F32)] + [jax.ShapeDtypeStruct(w[k].shape, F32) for k in SMALL] * 4,
        compiler_params=pltpu.CompilerParams(vmem_limit_bytes=VMEM_LIMIT),
    )(*args)
    return res[0], [dict(zip(SMALL, res[1 + q * nn:1 + (q + 1) * nn])) for q in range(4)]


WEIGHTS = ["g_pre_mix", "w_in", "lam_re", "lam_im", "log_dt", "b_re", "b_im", "c_re", "c_im", "d_skip", "w_glu", "conv_w",
           "g_ssm_out", "g_conv_out", "w_out", "g_post_mix", "g_pre_mlp", "w_up", "w_down", "g_post_mlp"]
BIG = ["w_in", "w_glu", "w_out", "w_up", "w_down"]


def kernel(x, g_pre_mix, w_in, lam_re, lam_im, log_dt, b_re, b_im, c_re, c_im, d_skip, w_glu, conv_w, g_ssm_out, g_conv_out, w_out, g_post_mix, g_pre_mlp, w_up, w_down, g_post_mlp, loss_target, m_g_pre_mix, m_w_in, m_lam_re, m_lam_im, m_log_dt, m_b_re, m_b_im, m_c_re, m_c_im, m_d_skip, m_w_glu, m_conv_w, m_g_ssm_out, m_g_conv_out, m_w_out, m_g_post_mix, m_g_pre_mlp, m_w_up, m_w_down, m_g_post_mlp, v_g_pre_mix, v_w_in, v_lam_re, v_lam_im, v_log_dt, v_b_re, v_b_im, v_c_re, v_c_im, v_d_skip, v_w_glu, v_conv_w, v_g_ssm_out, v_g_conv_out, v_w_out, v_g_post_mix, v_g_pre_mlp, v_w_up, v_w_down, v_g_post_mlp):
    w = dict(g_pre_mix=g_pre_mix, w_in=w_in, lam_re=lam_re, lam_im=lam_im, log_dt=log_dt, b_re=b_re, b_im=b_im, c_re=c_re, c_im=c_im,
             d_skip=d_skip, w_glu=w_glu, conv_w=conv_w, g_ssm_out=g_ssm_out, g_conv_out=g_conv_out, w_out=w_out, g_post_mix=g_post_mix,
             g_pre_mlp=g_pre_mlp, w_up=w_up, w_down=w_down, g_post_mlp=g_post_mlp)
    m = dict(g_pre_mix=m_g_pre_mix, w_in=m_w_in, lam_re=m_lam_re, lam_im=m_lam_im, log_dt=m_log_dt, b_re=m_b_re, b_im=m_b_im, c_re=m_c_re,
             c_im=m_c_im, d_skip=m_d_skip, w_glu=m_w_glu, conv_w=m_conv_w, g_ssm_out=m_g_ssm_out, g_conv_out=m_g_conv_out, w_out=m_w_out,
             g_post_mix=m_g_post_mix, g_pre_mlp=m_g_pre_mlp, w_up=m_w_up, w_down=m_w_down, g_post_mlp=m_g_post_mlp)
    v = dict(g_pre_mix=v_g_pre_mix, w_in=v_w_in, lam_re=v_lam_re, lam_im=v_lam_im, log_dt=v_log_dt, b_re=v_b_re, b_im=v_b_im, c_re=v_c_re,
             c_im=v_c_im, d_skip=v_d_skip, w_glu=v_w_glu, conv_w=v_conv_w, g_ssm_out=v_g_ssm_out, g_conv_out=v_g_conv_out, w_out=v_w_out,
             g_post_mix=v_g_post_mix, g_pre_mlp=v_g_pre_mlp, w_up=v_w_up, w_down=v_w_down, g_post_mlp=v_g_post_mlp)
    w_dev, m_dev, v_dev = w, m, v
    w, m, v = ({k: a[0] for k, a in d.items()} for d in (w, m, v))
    core = lax.axis_index("c").astype(jnp.int32).reshape(1)

    xs, target = x[0], loss_target[0]
    g1 = w["g_pre_mix"][None]
    g_ssm, g_conv = w["g_ssm_out"][None], w["g_conv_out"][None]
    g_post_mix, g_pre_mlp, g_post_mlp = w["g_post_mix"][None], w["g_pre_mlp"][None], w["g_post_mlp"][None]
    bt_re, bt_im = (jnp.transpose(w[k], (0, 2, 1)) for k in ("b_re", "b_im"))
    dskip = w["d_skip"].reshape(N_GBLK, 1, UB)
    shard = {k: w[k].astype(MXU_DTYPE) for k in BIG}
    conv_pad = jnp.pad(w["conv_w"], ((0, SUBLANES - 3), (0, 0)))

    bmat, cmat, coef_f, coef_r, (w_in_all,) = _ssm_matrices(
        w["lam_re"], w["lam_im"], w["log_dt"], bt_re, bt_im, w["c_re"], w["c_im"], exchanges=[_GatherForward([shard["w_in"]])])
    hn, proj, u4, w_glu_all, w_out_all, conv_all = _inproj_fwd(
        xs, g1, w_in_all, exchanges=[_Gather([shard["w_glu"], shard["w_out"], conv_pad], [False, False, False])])
    wd_half = shard["w_down"].shape[0] // 2
    wda, wdb = shard["w_down"][:wd_half], shard["w_down"][wd_half:]
    s_re, s_im, ys, w_up_all, wda_all = _s5_fwd(u4, bmat, cmat, coef_f, dskip, exchanges=[_Gather([shard["w_up"], wda], [True, True])])
    w_glu_f, w_out_f = w_glu_all.reshape(D_SSM, D_SSM), w_out_all.reshape(D_MODEL, D_MODEL)
    conv_f = jnp.transpose(conv_all, (1, 0, 2)).reshape(SUBLANES, D_CONV)
    ycat, o, x1, w_up_all, wda_all, wdb_all = _tail_fwd(xs, ys, proj, w_glu_f, conv_f, g_ssm, g_conv, w_out_f, g_post_mix,
                                                        exchanges=[_Forward([w_up_all, wda_all]), _GatherForward([wdb])])
    hn2, up, act, m_act, dx2, loss = _mlp_fwd(x1, target, w_up_all, wda_all, wdb_all, g_pre_mlp, g_post_mlp)

    dm, dup, dx1, dg_post_mlp, dg_pre_mlp = _mlp_bwd(dx2, m_act, up, x1, w_up_all, wda_all, wdb_all, g_pre_mlp, g_post_mlp)
    gw_down = _matmul_tn(act, dm, "dw_down")[0].reshape(N_CHIPS, D_FF // N_CHIPS, D_MODEL)
    gw_up = _matmul_tn(hn2, dup, "dw_up", col_shards=N_CHIPS)[0]
    do, da, y1, dys, dhbc, dg_post_mix, dg_ssm, dg_conv, dconv_w, o_down, o_up = _tail_bwd(
        dx1, o, ys, proj, w_glu_f, conv_f, g_ssm, g_conv, w_out_f, g_post_mix, exchanges=[_Pair([gw_down, gw_up])])
    p_down = _pair_add(gw_down, o_down, core, "pair_add_w_down")
    p_up = _pair_add(gw_up, o_up, core, "pair_add_w_up")
    gw_out = _matmul_tn(ycat, do, "dw_out")[0].reshape(N_CHIPS, D_MODEL // N_CHIPS, D_MODEL)
    gw_glu = _matmul_tn(y1, da, "dw_glu")[0].reshape(N_CHIPS, D_SSM // N_CHIPS, D_SSM)
    du, gb, gc, q, gd, q_down, q_up, o_out, o_glu = _s5_bwd(
        dys, u4, s_re, s_im, bmat, cmat, coef_r, dskip, exchanges=[_Chip([p_down, p_up]), _Pair([gw_out, gw_glu])])
    h_down = _quad_sum(q_down, core, "quad_sum_w_down")
    h_up = _quad_sum(q_up, core, "quad_sum_w_up")
    p_out = _pair_add(gw_out, o_out, core, "pair_add_w_out")
    p_glu = _pair_add(gw_glu, o_glu, core, "pair_add_w_glu")
    grad_x, dproj, dg_pre_mix = _inproj_bwd(du, dhbc, xs, dx1, w_in_all, g1)
    d_lam_re, d_lam_im, d_log_dt, d_b_re, d_b_im, d_c_re, d_c_im, d_d_skip = _ssm_param_grads(
        w["lam_re"], w["lam_im"], w["log_dt"], bt_re, bt_im, gb, gc, q, gd)
    small = {
        "g_pre_mix": dg_pre_mix[0], "lam_re": d_lam_re, "lam_im": d_lam_im, "log_dt": d_log_dt, "b_re": d_b_re, "b_im": d_b_im,
        "c_re": d_c_re, "c_im": d_c_im, "d_skip": d_d_skip, "conv_w": dconv_w[:3], "g_ssm_out": dg_ssm[0], "g_conv_out": dg_conv[0],
        "g_post_mix": dg_post_mix[0], "g_pre_mlp": dg_pre_mlp[0], "g_post_mlp": dg_post_mlp[0],
    }
    gw_in, g_down, g_up, q_out, q_glu, tiles, rows = _matmul_tn(
        hn, dproj, "dw_in", col_shards=N_CHIPS,
        exchanges=[_Share([h_down, h_up]), _Chip([p_out, p_glu]), _GatherSmall(_pack_tiles(small)), _GatherSmall(_pack_rows(small, loss))])
    h_out = _quad_sum(q_out, core, "quad_sum_w_out")
    h_glu = _quad_sum(q_glu, core, "quad_sum_w_glu")
    (o_in,) = _run_exchanges([_Pair([gw_in])], "rs_pair_w_in")
    p_in = _pair_add(gw_in, o_in, core, "pair_add_w_in")
    q_in, g_out, g_glu = _run_exchanges([_Chip([p_in]), _Share([h_out, h_glu])], "rs_chip_w_in")
    h_in = _quad_sum(q_in, core, "quad_sum_w_in")
    (g_in,) = _run_exchanges([_Share([h_in])], "rs_share_w_in")
    shard_grads = {"w_in": g_in, "w_glu": g_glu, "w_out": g_out, "w_up": g_up, "w_down": g_down}

    out = {q: {} for q in ("grad", "delta", "new_m", "new_v")}
    for k in BIG:
        out["grad"][k] = shard_grads[k][None]
        delta, new_m, new_v = _adamw(w[k], shard_grads[k], m[k], v[k], "adamw_" + k)
        out["delta"][k], out["new_m"][k], out["new_v"][k] = delta[None], new_m[None], new_v[None]
    form = lambda d: {k: _kernel_form(k, d[k]) for k in SMALL}
    loss, res = _adamw_small(tiles, rows, form(w_dev), form(m_dev), form(v_dev))
    for q, d in zip(("grad", "delta", "new_m", "new_v"), res):
        out[q].update({k: _param_form(k, d[k]) for k in SMALL})
    flat = [loss[0, 0], grad_x[None]]
    for q in ("grad", "delta", "new_m", "new_v"):
        flat += [out[q][k] for k in WEIGHTS]
    return tuple(flat)
```

```python
import functools
import math

import jax
import jax.numpy as jnp
import numpy as np
from jax import lax
from jax.experimental import pallas as pl
from jax.experimental.pallas import tpu as pltpu

F32 = jnp.float32
MXU_DTYPE = jnp.bfloat16
WIRE_DTYPE = jnp.bfloat16

D_MODEL = 1024
D_SSM = 512
D_CONV = 512
N_GROUPS = 32
GROUP = 16
STATE = 64
D_FF = 4096
RMS_EPS = 1e-6
N_CHIPS = 4
N_DEV = 8

ADAM_LR = 0.001
ADAM_B1 = 0.9
ADAM_B2 = 0.999
ADAM_EPS = 1e-08
ADAM_WD = 0.01
ADAM_STEP = 10

N_GBLK = 2
G_PER_BLK = N_GROUPS // N_GBLK
UB = G_PER_BLK * GROUP
WB = G_PER_BLK * STATE
LANE_CHUNK = 256
SUBLANES = 8
N_TABLES = 24
DIAG_COLS = 256
DIAG_ROWS = DIAG_COLS // STATE * GROUP

TM_PROJ = 512
TM_S5 = 512
TM_TAIL = 512
TM_MLP = 512
TL_TN = 2048
TK_TN = 1024
TN_TN = 2048
VMEM_LIMIT = 56 * 1024 * 1024

MESH = pl.DeviceIdType.MESH


def _params(sem, vmem=VMEM_LIMIT):
    return pltpu.CompilerParams(dimension_semantics=sem, vmem_limit_bytes=vmem)


def _resident(shape):
    nd = len(shape)
    return pl.BlockSpec(shape, lambda *_: (0,) * nd, pipeline_mode=pl.Buffered(1))


def _dot(a, b):
    return jnp.dot(a, b, preferred_element_type=F32)


def _dot_nt(a, b):
    return lax.dot_general(a, b, (((1,), (1,)), ((), ())), preferred_element_type=F32)


def _dot_tn(a, b):
    return lax.dot_general(a, b, (((0,), (0,)), ((), ())), preferred_element_type=F32)


def _rms_fwd(x, g):
    r = lax.rsqrt(jnp.mean(x * x, axis=-1, keepdims=True) + RMS_EPS)
    return x * r * g


def _rms_bwd(x, g, dy):
    r = lax.rsqrt(jnp.mean(x * x, axis=-1, keepdims=True) + RMS_EPS)
    xn = x * r
    q = dy * g
    dx = r * (q - xn * jnp.mean(q * xn, axis=-1, keepdims=True))
    return dx, jnp.sum(dy * xn, axis=0, keepdims=True)


_GELU_C = math.sqrt(2.0 / math.pi)


def _gelu(x):
    t = jnp.tanh(_GELU_C * (x + 0.044715 * (x * x * x)))
    y = x * (0.5 * (1.0 + t))
    dy = 0.5 * (1.0 + t) + 0.5 * x * (1.0 - t * t) * (_GELU_C * (1.0 + 3 * 0.044715 * (x * x)))
    return y, dy


def _tile(n, pref):
    t = min(n, pref)
    assert n % t == 0, (n, t)
    return t


HBM = pl.BlockSpec(memory_space=pltpu.HBM)
VMEM = pl.BlockSpec(memory_space=pltpu.VMEM)
DMA_SEMS = pltpu.SemaphoreType.DMA


def _place():
    x, y, c = lax.axis_index("x"), lax.axis_index("y"), lax.axis_index("c")
    chips = [(1 - x, y), (x, 1 - y), (1 - x, 1 - y)]
    return (x, y, c), 2 * x + y, (x, y, 1 - c), chips, [2 * px + py for px, py in chips]


def _remote(src, dst, send_sem, recv_sem, device):
    return pltpu.make_async_remote_copy(src_ref=src, dst_ref=dst, send_sem=send_sem, recv_sem=recv_sem,
                                        device_id=device, device_id_type=MESH)


def _half(rows, c):
    return pl.ds(c * (rows // 2), rows // 2)


class _Exchange:
    aliases = {}

    def start(self, ins, outs, sems):
        local, outgoing, _ = self._copies(ins, outs, sems)
        for cp in local + outgoing:
            cp.start()

    def finish(self, ins, outs, sems):
        local, outgoing, incoming = self._copies(ins, outs, sems)
        for cp in incoming:
            cp.wait_recv()
        for cp in outgoing:
            cp.wait_send()
        for cp in local:
            cp.wait()


class _Gather(_Exchange):
    def __init__(self, shards, split):
        self.inputs, self.split = list(shards), split
        self.out_shape = [jax.ShapeDtypeStruct((N_CHIPS, *a.shape), a.dtype) for a in shards]
        self.sems = [DMA_SEMS((len(shards), 3)), DMA_SEMS((len(shards), 3)), DMA_SEMS((len(shards),))]

    def _copies(self, ins, outs, sems):
        send, recv, lsem = sems
        (x, y, c), me, sibling, chips, ids = _place()
        local = [pltpu.make_async_copy(ins[t], outs[t].at[me], lsem.at[t]) for t in range(len(ins))]
        outgoing, incoming = [], []
        for t, a in enumerate(self.inputs):
            rows = _half(a.shape[0], c) if self.split[t] else pl.ds(0, a.shape[0])
            for k in range(3):
                to = (*chips[k], c)
                outgoing.append(_remote(ins[t].at[rows, :], outs[t].at[me, rows, :], send.at[t, k], recv.at[t, k], to))
                incoming.append(_remote(ins[t].at[rows, :], outs[t].at[ids[k], rows, :], send.at[t, k], recv.at[t, k], to))
        return local, outgoing, incoming


class _Forward(_Exchange):
    def __init__(self, arrays):
        self.inputs = list(arrays)
        self.out_shape = [jax.ShapeDtypeStruct(a.shape, a.dtype) for a in arrays]
        self.aliases = {t: t for t in range(len(arrays))}
        self.sems = [DMA_SEMS((len(arrays), 3)), DMA_SEMS((len(arrays), 3))]

    def _copies(self, ins, outs, sems):
        send, recv = sems
        (x, y, c), me, sibling, chips, ids = _place()
        outgoing, incoming = [], []
        for t, a in enumerate(self.inputs):
            for k in range(3):
                mine = outs[t].at[ids[k], _half(a.shape[1], c), :]
                theirs = outs[t].at[ids[k], _half(a.shape[1], 1 - c), :]
                outgoing.append(_remote(mine, mine, send.at[t, k], recv.at[t, k], sibling))
                incoming.append(_remote(theirs, theirs, send.at[t, k], recv.at[t, k], sibling))
        return [], outgoing, incoming


class _GatherForward(_Exchange):
    def __init__(self, shards):
        self.gather = _Gather(shards, [True] * len(shards))
        self.forward = _Forward(self.gather.out_shape)
        self.inputs, self.out_shape = self.gather.inputs, self.gather.out_shape
        self.sems = self.gather.sems + self.forward.sems

    def start(self, ins, outs, sems):
        self.gather.start(ins, outs, sems[:3])

    def finish(self, ins, outs, sems):
        local, outgoing, incoming = self.gather._copies(ins, outs, sems[:3])
        _, passed, from_sibling = self.forward._copies(outs, outs, sems[3:])
        for landed, onward in zip(incoming, passed):
            landed.wait_recv()
            onward.start()
        for cp in from_sibling:
            cp.wait_recv()
        for cp in outgoing + passed:
            cp.wait_send()
        for cp in local:
            cp.wait()


class _Pair(_Exchange):
    def __init__(self, grads):
        self.inputs = list(grads)
        self.out_shape = [jax.ShapeDtypeStruct((g.shape[0], g.shape[1] // 2, g.shape[2]), g.dtype) for g in grads]
        self.sems = [DMA_SEMS((len(grads),)), DMA_SEMS((len(grads),))]

    def _copies(self, ins, outs, sems):
        send, recv = sems
        (x, y, c), me, sibling, chips, ids = _place()
        cps = [_remote(ins[t].at[:, _half(g.shape[1], 1 - c), :], outs[t], send.at[t], recv.at[t], sibling)
               for t, g in enumerate(self.inputs)]
        return [], cps, cps


class _Chip(_Exchange):
    def __init__(self, parts):
        self.inputs = list(parts)
        self.out_shape = [jax.ShapeDtypeStruct(p.shape, p.dtype) for p in parts]
        self.sems = [DMA_SEMS((len(parts), 3)), DMA_SEMS((len(parts), 3)), DMA_SEMS((len(parts),))]

    def _copies(self, ins, outs, sems):
        send, recv, lsem = sems
        (x, y, c), me, sibling, chips, ids = _place()
        local = [pltpu.make_async_copy(ins[t].at[me], outs[t].at[me], lsem.at[t]) for t in range(len(ins))]
        outgoing, incoming = [], []
        for t in range(len(ins)):
            for k in range(3):
                to = (*chips[k], c)
                outgoing.append(_remote(ins[t].at[ids[k]], outs[t].at[me], send.at[t, k], recv.at[t, k], to))
                incoming.append(_remote(ins[t].at[ids[k]], outs[t].at[ids[k]], send.at[t, k], recv.at[t, k], to))
        return local, outgoing, incoming


class _Share(_Exchange):
    def __init__(self, grads):
        self.inputs = list(grads)
        self.out_shape = [jax.ShapeDtypeStruct(g.shape, g.dtype) for g in grads]
        self.aliases = {t: t for t in range(len(grads))}
        self.sems = [DMA_SEMS((len(grads),)), DMA_SEMS((len(grads),))]

    def _copies(self, ins, outs, sems):
        send, recv = sems
        (x, y, c), me, sibling, chips, ids = _place()
        outgoing, incoming = [], []
        for t, g in enumerate(self.inputs):
            mine = outs[t].at[_half(g.shape[0], c), :]
            theirs = outs[t].at[_half(g.shape[0], 1 - c), :]
            outgoing.append(_remote(mine, mine, send.at[t], recv.at[t], sibling))
            incoming.append(_remote(theirs, theirs, send.at[t], recv.at[t], sibling))
        return [], outgoing, incoming


class _GatherSmall(_Exchange):
    def __init__(self, block):
        self.inputs = [block]
        self.out_shape = [jax.ShapeDtypeStruct((N_DEV, *block.shape), block.dtype)]
        self.sems = [DMA_SEMS((7,)), DMA_SEMS((7,)), DMA_SEMS(())]

    def _copies(self, ins, outs, sems):
        send, recv, lsem = sems
        (x, y, c), me, sibling, chips, ids = _place()
        slot = lambda px, py, pc: outs[0].at[4 * px + 2 * py + pc]

        def copy(k, block, to, src=None):
            return _remote(slot(*block) if src is None else src, slot(*block), send.at[k], recv.at[k], to)

        local = [pltpu.make_async_copy(ins[0], slot(x, y, c), lsem)]
        first = [copy(0, (x, y, c), sibling, src=ins[0])] + [copy(1 + j, (x, y, c), (*chip, c), src=ins[0]) for j, chip in enumerate(chips)]
        passed = [copy(4 + j, (*chip, c), sibling) for j, chip in enumerate(chips)]
        landed = [copy(1 + j, (*chip, c), (x, y, c)) for j, chip in enumerate(chips)]
        from_sibling = [copy(0, (x, y, 1 - c), (x, y, c))] + [copy(4 + j, (*chip, 1 - c), (x, y, c)) for j, chip in enumerate(chips)]
        return local, first, (passed, landed, from_sibling)

    def finish(self, ins, outs, sems):
        local, first, (passed, landed, from_sibling) = self._copies(ins, outs, sems)
        for j in range(3):
            landed[j].wait_recv()
            passed[j].start()
        for cp in from_sibling:
            cp.wait_recv()
        for cp in first + passed:
            cp.wait_send()
        for cp in local:
            cp.wait()


def _split_refs(refs, counts):
    out = []
    for n in counts:
        out.append(refs[:n])
        refs = refs[n:]
    return out


def _each_exchange(exchanges, method, x_in, x_out, x_sem):
    for ex in exchanges:
        ni, no, ns = len(ex.inputs), len(ex.out_shape), len(ex.sems)
        getattr(ex, method)(x_in[:ni], x_out[:no], x_sem[:ns])
        x_in, x_out, x_sem = x_in[ni:], x_out[no:], x_sem[ns:]


def _call(body, *, name, grid, in_specs, out_specs, out_shape, operands, semantics, scratch_shapes=(), exchanges=()):
    x_in = [a for ex in exchanges for a in ex.inputs]
    x_out = [s for ex in exchanges for s in ex.out_shape]
    x_sem = [s for ex in exchanges for s in ex.sems]
    counts = (len(in_specs), len(x_in), len(out_specs), len(x_out), len(scratch_shapes), len(x_sem))
    aliases, i0, o0 = {}, len(in_specs), len(out_specs)
    for ex in exchanges:
        aliases.update({i0 + i: o0 + o for i, o in ex.aliases.items()})
        i0, o0 = i0 + len(ex.inputs), o0 + len(ex.out_shape)

    def full_body(*refs):
        ins, xi, outs, xo, scr, xs = _split_refs(list(refs), counts)
        if exchanges:
            @pl.when(functools.reduce(jnp.logical_and, [pl.program_id(a) == 0 for a in range(len(grid))]))
            def _():
                _each_exchange(exchanges, "start", xi, xo, xs)

        body(*ins, *outs, *scr)
        if exchanges:
            @pl.when(functools.reduce(jnp.logical_and, [pl.program_id(a) == grid[a] - 1 for a in range(len(grid))]))
            def _():
                _each_exchange(exchanges, "finish", xi, xo, xs)

    return pl.pallas_call(
        full_body, name=name, grid=grid,
        in_specs=list(in_specs) + [HBM] * len(x_in), out_specs=list(out_specs) + [HBM] * len(x_out),
        out_shape=list(out_shape) + x_out, scratch_shapes=list(scratch_shapes) + x_sem,
        input_output_aliases=aliases, compiler_params=_params(semantics),
    )(*operands, *x_in)


def _run_exchanges(exchanges, name):
    x_in = [a for ex in exchanges for a in ex.inputs]
    x_out = [s for ex in exchanges for s in ex.out_shape]
    x_sem = [s for ex in exchanges for s in ex.sems]
    aliases, i0, o0 = {}, 0, 0
    for ex in exchanges:
        aliases.update({i0 + i: o0 + o for i, o in ex.aliases.items()})
        i0, o0 = i0 + len(ex.inputs), o0 + len(ex.out_shape)

    def body(*refs):
        xi, xo, xs = _split_refs(list(refs), (len(x_in), len(x_out), len(x_sem)))
        _each_exchange(exchanges, "start", xi, xo, xs)
        _each_exchange(exchanges, "finish", xi, xo, xs)

    return pl.pallas_call(
        body, name=name, in_specs=[HBM] * len(x_in), out_specs=[HBM] * len(x_out), out_shape=x_out,
        scratch_shapes=x_sem, input_output_aliases=aliases,
    )(*x_in)


def _inproj_fwd(x, g1, w_in_all, exchanges=()):
    L, D = x.shape
    ns, _, nc = w_in_all.shape
    tm = _tile(L, TM_PROJ)

    def body(x_ref, g_ref, w_ref, hn_ref, proj_ref, u_ref):
        hn = _rms_fwd(x_ref[...], g_ref[...]).astype(MXU_DTYPE)
        hn_ref[...] = hn
        for j in range(ns):
            proj_ref[:, j * nc:(j + 1) * nc] = _dot(hn, w_ref[j])
        _store_slabs(u_ref, proj_ref[:, 0:nc])

    return _call(
        body, name="inproj_fwd", grid=(L // tm,), exchanges=exchanges, semantics=("arbitrary",), operands=(x, g1, w_in_all),
        in_specs=[pl.BlockSpec((tm, D), lambda i: (i, 0)), _resident((1, D)), _resident(w_in_all.shape)],
        out_specs=[pl.BlockSpec((tm, D), lambda i: (i, 0)), pl.BlockSpec((tm, ns * nc), lambda i: (i, 0)), _slab_spec(nc, tm)],
        out_shape=[jax.ShapeDtypeStruct((L, D), MXU_DTYPE), jax.ShapeDtypeStruct((L, ns * nc), F32), _slab_shape(L, nc)],
    )


def _slab_shape(L, n):
    return jax.ShapeDtypeStruct((n // LANES, L, LANES), F32)


def _slab_spec(n, tm, index=lambda i: (0, i, 0)):
    return pl.BlockSpec((n // LANES, tm, LANES), index)


def _store_slabs(ref, value):
    for k in range(ref.shape[0]):
        ref[k] = value[:, k * LANES:(k + 1) * LANES]


def _load_slabs(ref):
    return jnp.concatenate([ref[k] for k in range(ref.shape[0])], axis=1)


SEG_ROWS = SUBLANES * SUBLANES


def _load_permuted(ref):
    tm = ref.shape[1]
    slabs = []
    for k in range(ref.shape[0]):
        tiles = [ref.at[k][pl.ds(b * SEG_ROWS + j, SUBLANES, stride=SUBLANES), :] for b in range(tm // SEG_ROWS) for j in range(SUBLANES)]
        slabs.append(jnp.concatenate(tiles, axis=0))
    return jnp.concatenate(slabs, axis=1)


def _store_permuted(ref, value):
    tm = ref.shape[1]
    for k in range(ref.shape[0]):
        for b in range(tm // SEG_ROWS):
            for j in range(SUBLANES):
                r = b * SEG_ROWS + j * SUBLANES
                ref.at[k][pl.ds(b * SEG_ROWS + j, SUBLANES, stride=SUBLANES), :] = value[r:r + SUBLANES, k * LANES:(k + 1) * LANES]


def _scan_tile(xr, xi, hr, hi, coef_ref, lanes, reverse):
    for k, j in ((1, 0), (2, 2), (4, 4)):
        ar = coef_ref[j, :, lanes]
        ai = coef_ref[j + 1, :, lanes]
        shift = SUBLANES - k if reverse else k
        sr = pltpu.roll(xr, shift, 0)
        si = pltpu.roll(xi, shift, 0)
        xr, xi = xr + (ar * sr - ai * si), xi + (ar * si + ai * sr)
    pr = coef_ref[6, :, lanes]
    pi = coef_ref[7, :, lanes]
    return xr + (pr * hr - pi * hi), xi + (pr * hi + pi * hr)


def _scan_block(read, write, hr, hi, coef_ref, lanes, reverse):
    order = list(range(SUBLANES - 1, -1, -1) if reverse else range(SUBLANES))
    near = 8 + 2 * order[0]
    ar = coef_ref[near, :, lanes]
    ai = coef_ref[near + 1, :, lanes]
    xr, xi = read(order[0])
    local = {order[0]: (xr, xi)}
    for j in order[1:]:
        br, bi = read(j)
        xr, xi = br + (ar * xr - ai * xi), bi + (ar * xi + ai * xr)
        local[j] = (xr, xi)
    er, ei = _scan_tile(xr, xi, hr, hi, coef_ref, lanes, reverse)
    edge = lax.broadcasted_iota(jnp.int32, er.shape, 0) == (SUBLANES - 1 if reverse else 0)
    shift = SUBLANES - 1 if reverse else 1
    pr = jnp.where(edge, hr, pltpu.roll(er, shift, 0))
    pi = jnp.where(edge, hi, pltpu.roll(ei, shift, 0))
    for j in range(SUBLANES):
        cr = coef_ref[8 + 2 * j, :, lanes]
        ci = coef_ref[9 + 2 * j, :, lanes]
        xr, xi = local[j]
        write(j, xr + (cr * pr - ci * pi), xi + (cr * pi + ci * pr))
    end = 0 if reverse else SUBLANES - 1
    return jnp.broadcast_to(er[end:end + 1, :], er.shape), jnp.broadcast_to(ei[end:end + 1, :], ei.shape)


def _s5_fwd(u4, bmat, cmat, coef, dskip, exchanges=()):
    L = u4.shape[1]
    tm = _tile(L, TM_S5)
    lc = min(LANE_CHUNK, WB)

    def body(u_ref, bm_ref, cm_ref, coef_ref, d_ref, sre_ref, sim_ref, ys_ref, hr_ref, hi_ref):
        @pl.when(pl.program_id(1) == 0)
        def _():
            hr_ref[...] = jnp.zeros_like(hr_ref)
            hi_ref[...] = jnp.zeros_like(hi_ref)

        u = _load_permuted(u_ref)
        bu = _dot(u.astype(MXU_DTYPE), bm_ref[0])
        sre_ref[...] = bu[:, :WB]
        sim_ref[...] = bu[:, WB:]
        for c in range(WB // lc):
            lanes = slice(c * lc, (c + 1) * lc)
            hr, hi = hr_ref[:, lanes], hi_ref[:, lanes]
            for b in range(tm // SEG_ROWS):
                rows = lambda j, b=b: slice(b * SEG_ROWS + j * SUBLANES, b * SEG_ROWS + (j + 1) * SUBLANES)

                def read(j, rows=rows, lanes=lanes):
                    return sre_ref[rows(j), lanes], sim_ref[rows(j), lanes]

                def write(j, xr, xi, rows=rows, lanes=lanes):
                    sre_ref[rows(j), lanes] = xr
                    sim_ref[rows(j), lanes] = xi

                hr, hi = _scan_block(read, write, hr, hi, coef_ref, lanes, False)
            hr_ref[:, lanes] = hr
            hi_ref[:, lanes] = hi
        ys = _dot_nt(sre_ref[...].astype(MXU_DTYPE), cm_ref[0, :, :WB]) + _dot_nt(sim_ref[...].astype(MXU_DTYPE), cm_ref[0, :, WB:])
        _store_permuted(ys_ref, ys + d_ref[0] * u)

    return _call(
        body, name="s5_fwd", grid=(N_GBLK, L // tm), exchanges=exchanges, semantics=("arbitrary", "arbitrary"),
        operands=(u4, bmat, cmat, coef, dskip),
        in_specs=[
            _slab_spec(UB, tm, lambda b, i: (b, i, 0)),
            pl.BlockSpec((1, UB, 2 * WB), lambda b, i: (b, 0, 0)),
            pl.BlockSpec((1, UB, 2 * WB), lambda b, i: (b, 0, 0)),
            pl.BlockSpec((N_TABLES, SUBLANES, WB), lambda b, i: (0, 0, b)),
            pl.BlockSpec((1, 1, UB), lambda b, i: (b, 0, 0)),
        ],
        out_specs=[
            pl.BlockSpec((tm, WB), lambda b, i: (i, b)),
            pl.BlockSpec((tm, WB), lambda b, i: (i, b)),
            _slab_spec(UB, tm, lambda b, i: (b, i, 0)),
        ],
        out_shape=[
            jax.ShapeDtypeStruct((L, N_GBLK * WB), F32),
            jax.ShapeDtypeStruct((L, N_GBLK * WB), F32),
            _slab_shape(L, D_SSM),
        ],
        scratch_shapes=[pltpu.VMEM((SUBLANES, WB), F32), pltpu.VMEM((SUBLANES, WB), F32)],
    )


def _tail_fwd(x, ys, proj, w_glu, conv_w, g_ssm, g_conv, w_out, g_post, exchanges=()):
    L, D = x.shape
    tm = _tile(L, TM_TAIL)

    def body(x_ref, ys_ref, h_ref, bg_ref, cg_ref, wglu_ref, cw_ref, gs_ref, gc_ref, wout_ref, gp_ref,
             ycat_ref, o_ref, x1_ref, zbuf):
        @pl.when(pl.program_id(0) == 0)
        def _():
            zbuf[0:SUBLANES, :] = jnp.zeros((SUBLANES, D_CONV), F32)

        y1, _ = _gelu(_load_slabs(ys_ref))
        y2 = y1 * jax.nn.sigmoid(_dot(y1.astype(MXU_DTYPE), wglu_ref[...]))
        ycat_ref[:, :D_SSM] = _rms_fwd(y2, gs_ref[...]).astype(MXU_DTYPE)
        z = cg_ref[...] * h_ref[...]
        zbuf[SUBLANES:, :] = z
        conv = cw_ref[0:1, :] * zbuf[SUBLANES - 2:SUBLANES - 2 + tm, :] + cw_ref[1:2, :] * zbuf[SUBLANES - 1:SUBLANES - 1 + tm, :] + cw_ref[2:3, :] * z
        zbuf[0:SUBLANES, :] = zbuf[tm:tm + SUBLANES, :]
        ycat_ref[:, D_SSM:] = _rms_fwd(bg_ref[...] * conv, gc_ref[...]).astype(MXU_DTYPE)
        o = _dot(ycat_ref[...], wout_ref[...])
        o_ref[...] = o
        x1_ref[...] = x_ref[...] + _rms_fwd(o, gp_ref[...])

    row = lambda i: (i, 0)
    return _call(
        body, name="tail_fwd", grid=(L // tm,), exchanges=exchanges, semantics=("arbitrary",),
        operands=(x, ys, proj, proj, proj, w_glu, conv_w, g_ssm, g_conv, w_out, g_post),
        in_specs=[
            pl.BlockSpec((tm, D), row), _slab_spec(D_SSM, tm),
            pl.BlockSpec((tm, D_CONV), lambda i: (i, 1)), pl.BlockSpec((tm, D_CONV), lambda i: (i, 2)),
            pl.BlockSpec((tm, D_CONV), lambda i: (i, 3)),
            _resident(w_glu.shape), _resident(conv_w.shape), _resident(g_ssm.shape), _resident(g_conv.shape),
            _resident(w_out.shape), _resident(g_post.shape),
        ],
        out_specs=[pl.BlockSpec((tm, D), row), pl.BlockSpec((tm, D), row), pl.BlockSpec((tm, D), row)],
        out_shape=[jax.ShapeDtypeStruct((L, D), MXU_DTYPE), jax.ShapeDtypeStruct((L, D), F32), jax.ShapeDtypeStruct((L, D), F32)],
        scratch_shapes=[pltpu.VMEM((tm + SUBLANES, D_CONV), F32)],
    )


def _mlp_fwd(x1, target, w_up_all, w_down_a, w_down_b, g_pre, g_post):
    L, D = x1.shape
    ns, _, fc = w_up_all.shape
    half = w_down_a.shape[1]
    tm = _tile(L, TM_MLP)

    def body(x1_ref, t_ref, wup_ref, wda_ref, wdb_ref, gpre_ref, gpost_ref, hn2_ref, up_ref, act_ref, m_ref, dx2_ref, loss_ref):
        @pl.when(pl.program_id(0) == 0)
        def _():
            loss_ref[...] = jnp.zeros_like(loss_ref)

        x1v = x1_ref[...]
        hn2 = _rms_fwd(x1v, gpre_ref[...]).astype(MXU_DTYPE)
        hn2_ref[...] = hn2
        m = jnp.zeros((tm, D), F32)
        for j in range(ns):
            up = _dot(hn2, wup_ref[j])
            up_ref[:, j * fc:(j + 1) * fc] = up.astype(MXU_DTYPE)
            act = jnp.square(jnp.maximum(up, 0.0)).astype(MXU_DTYPE)
            act_ref[:, j * fc:(j + 1) * fc] = act
            m = m + _dot(act[:, :half], wda_ref[j]) + _dot(act[:, half:], wdb_ref[j])
        m_ref[...] = m
        err = x1v + _rms_fwd(m, gpost_ref[...]) - t_ref[...]
        loss_ref[...] += 0.5 * jnp.sum(jnp.mean(err * err, axis=-1, keepdims=True))
        dx2_ref[...] = err * (1.0 / D)

    row = lambda i: (i, 0)
    return pl.pallas_call(
        body, name="mlp_fwd", grid=(L // tm,),
        in_specs=[pl.BlockSpec((tm, D), row), pl.BlockSpec((tm, D), row), _resident(w_up_all.shape), _resident(w_down_a.shape),
                  _resident(w_down_b.shape), _resident(g_pre.shape), _resident(g_post.shape)],
        out_specs=[pl.BlockSpec((tm, D), row), pl.BlockSpec((tm, ns * fc), row), pl.BlockSpec((tm, ns * fc), row), pl.BlockSpec((tm, D), row),
                   pl.BlockSpec((tm, D), row), pl.BlockSpec((SUBLANES, LANES), lambda i: (0, 0))],
        out_shape=[jax.ShapeDtypeStruct((L, D), MXU_DTYPE), jax.ShapeDtypeStruct((L, ns * fc), MXU_DTYPE), jax.ShapeDtypeStruct((L, ns * fc), MXU_DTYPE),
                   jax.ShapeDtypeStruct((L, D), F32), jax.ShapeDtypeStruct((L, D), F32), jax.ShapeDtypeStruct((SUBLANES, LANES), F32)],
        compiler_params=_params(("arbitrary",)),
    )(x1, target, w_up_all, w_down_a, w_down_b, g_pre, g_post)


def _mlp_bwd(dx2, m, up, x1, w_up_all, w_down_a, w_down_b, g_pre, g_post):
    L, D = x1.shape
    ns, _, fc = w_up_all.shape
    tm = _tile(L, TM_MLP)

    def body(dx2_ref, m_ref, up_ref, x1_ref, wup_ref, wda_ref, wdb_ref, gpre_ref, gpost_ref,
             dm_ref, dup_ref, dx1_ref, dgpost_ref, dgpre_ref):
        @pl.when(pl.program_id(0) == 0)
        def _():
            dgpost_ref[...] = jnp.zeros_like(dgpost_ref)
            dgpre_ref[...] = jnp.zeros_like(dgpre_ref)

        dx2v = dx2_ref[...]
        dm, dg = _rms_bwd(m_ref[...], gpost_ref[...], dx2v)
        dgpost_ref[...] += dg
        dm_b = dm.astype(MXU_DTYPE)
        dm_ref[...] = dm_b
        dhn2 = jnp.zeros((tm, D), F32)
        for j in range(ns):
            cols = slice(j * fc, (j + 1) * fc)
            relu = jnp.maximum(up_ref[:, cols].astype(F32), 0.0)
            dact = jnp.concatenate([_dot_nt(dm_b, wda_ref[j]), _dot_nt(dm_b, wdb_ref[j])], axis=1)
            dup = (dact * (2.0 * relu)).astype(MXU_DTYPE)
            dup_ref[:, cols] = dup
            dhn2 = dhn2 + _dot_nt(dup, wup_ref[j])
        dx, dg = _rms_bwd(x1_ref[...], gpre_ref[...], dhn2)
        dgpre_ref[...] += dg
        dx1_ref[...] = dx2v + dx

    row = lambda i: (i, 0)
    vec = pl.BlockSpec((1, D), lambda i: (0, 0))
    return pl.pallas_call(
        body, name="mlp_bwd", grid=(L // tm,),
        in_specs=[pl.BlockSpec((tm, D), row), pl.BlockSpec((tm, D), row), pl.BlockSpec((tm, ns * fc), row), pl.BlockSpec((tm, D), row),
                  _resident(w_up_all.shape), _resident(w_down_a.shape), _resident(w_down_b.shape), _resident(g_pre.shape), _resident(g_post.shape)],
        out_specs=[pl.BlockSpec((tm, D), row), pl.BlockSpec((tm, ns * fc), row), pl.BlockSpec((tm, D), row), vec, vec],
        out_shape=[jax.ShapeDtypeStruct((L, D), MXU_DTYPE), jax.ShapeDtypeStruct((L, ns * fc), MXU_DTYPE), jax.ShapeDtypeStruct((L, D), F32),
                   jax.ShapeDtypeStruct((1, D), F32), jax.ShapeDtypeStruct((1, D), F32)],
        compiler_params=_params(("arbitrary",)),
    )(dx2, m, up, x1, w_up_all, w_down_a, w_down_b, g_pre, g_post)


def _tail_bwd(dx1, o, ys, proj, w_glu, conv_w, g_ssm, g_conv, w_out, g_post, exchanges=()):
    L, D = dx1.shape
    tm = _tile(L, TM_TAIL)
    nt = L // tm
    hb = tm // SUBLANES

    def body(dx1_ref, o_ref, ys_ref, h_ref, bg_ref, cg_ref, hh_ref, hcg_ref, wglu_ref, cw_ref, gs_ref, gc_ref, wout_ref, gp_ref,
             do_ref, da_ref, y1_ref, dys_ref, dhbc_ref, dgp_ref, dgs_ref, dgc_ref, dcw_ref, zbuf, dcbuf):
        step = pl.program_id(0)

        @pl.when(step == 0)
        def _():
            dcbuf[tm:, :] = jnp.zeros((SUBLANES, D_CONV), F32)
            dgp_ref[...] = jnp.zeros_like(dgp_ref)
            dgs_ref[...] = jnp.zeros_like(dgs_ref)
            dgc_ref[...] = jnp.zeros_like(dgc_ref)
            dcw_ref[...] = jnp.zeros_like(dcw_ref)

        do, dg = _rms_bwd(o_ref[...], gp_ref[...], dx1_ref[...])
        dgp_ref[...] += dg
        do_b = do.astype(MXU_DTYPE)
        do_ref[...] = do_b
        dycat = _dot_nt(do_b, wout_ref[...])
        y1, dgelu = _gelu(_load_slabs(ys_ref))
        y1_b = y1.astype(MXU_DTYPE)
        y1_ref[...] = y1_b
        s = jax.nn.sigmoid(_dot(y1_b, wglu_ref[...]))
        dy2, dg = _rms_bwd(y1 * s, gs_ref[...], dycat[:, :D_SSM])
        dgs_ref[...] += dg
        da_b = (dy2 * y1 * s * (1.0 - s)).astype(MXU_DTYPE)
        da_ref[...] = da_b
        _store_slabs(dys_ref, (dy2 * s + _dot_nt(da_b, wglu_ref[...])) * dgelu)
        h = h_ref[...]
        cg = cg_ref[...]
        bg = bg_ref[...]
        z = cg * h
        first = step == nt - 1
        zbuf[0:SUBLANES, :] = jnp.where(first, 0.0, hcg_ref[...] * hh_ref[...])
        zbuf[SUBLANES:, :] = z
        z1 = zbuf[SUBLANES - 1:SUBLANES - 1 + tm, :]
        z2 = zbuf[SUBLANES - 2:SUBLANES - 2 + tm, :]
        conv = cw_ref[0:1, :] * z2 + cw_ref[1:2, :] * z1 + cw_ref[2:3, :] * z
        dyc, dg = _rms_bwd(bg * conv, gc_ref[...], dycat[:, D_SSM:])
        dgc_ref[...] += dg
        dconv = dyc * bg
        dcw_ref[0:1, :] += jnp.sum(dconv * z2, axis=0, keepdims=True)
        dcw_ref[1:2, :] += jnp.sum(dconv * z1, axis=0, keepdims=True)
        dcw_ref[2:3, :] += jnp.sum(dconv * z, axis=0, keepdims=True)
        dcbuf[0:tm, :] = dconv
        dz = cw_ref[2:3, :] * dconv + cw_ref[1:2, :] * dcbuf[1:1 + tm, :] + cw_ref[0:1, :] * dcbuf[2:2 + tm, :]
        dcbuf[tm:, :] = dcbuf[0:SUBLANES, :]
        dhbc_ref[:, 0:D_CONV] = (dz * cg).astype(MXU_DTYPE)
        dhbc_ref[:, D_CONV:2 * D_CONV] = (dyc * conv).astype(MXU_DTYPE)
        dhbc_ref[:, 2 * D_CONV:] = (dz * h).astype(MXU_DTYPE)

    rev = lambda i: (nt - 1 - i, 0)
    slab = lambda i: (0, nt - 1 - i, 0)
    col = lambda c: (lambda i: (nt - 1 - i, c))
    halo = lambda c: (lambda i: (jnp.maximum((nt - 1 - i) * hb - 1, 0), c))
    vec = lambda n: pl.BlockSpec((1, n), lambda i: (0, 0))
    return _call(
        body, name="tail_bwd", grid=(nt,), exchanges=exchanges, semantics=("arbitrary",),
        operands=(dx1, o, ys, proj, proj, proj, proj, proj, w_glu, conv_w, g_ssm, g_conv, w_out, g_post),
        in_specs=[
            pl.BlockSpec((tm, D), rev), pl.BlockSpec((tm, D), rev), _slab_spec(D_SSM, tm, slab),
            pl.BlockSpec((tm, D_CONV), col(1)), pl.BlockSpec((tm, D_CONV), col(2)), pl.BlockSpec((tm, D_CONV), col(3)),
            pl.BlockSpec((SUBLANES, D_CONV), halo(1)), pl.BlockSpec((SUBLANES, D_CONV), halo(3)),
            _resident(w_glu.shape), _resident(conv_w.shape), _resident(g_ssm.shape), _resident(g_conv.shape),
            _resident(w_out.shape), _resident(g_post.shape),
        ],
        out_specs=[
            pl.BlockSpec((tm, D), rev), pl.BlockSpec((tm, D_SSM), rev), pl.BlockSpec((tm, D_SSM), rev), _slab_spec(D_SSM, tm, slab),
            pl.BlockSpec((tm, 3 * D_CONV), rev), vec(D), vec(D_SSM), vec(D_CONV),
            pl.BlockSpec((SUBLANES, D_CONV), lambda i: (0, 0)),
        ],
        out_shape=[
            jax.ShapeDtypeStruct((L, D), MXU_DTYPE), jax.ShapeDtypeStruct((L, D_SSM), MXU_DTYPE), jax.ShapeDtypeStruct((L, D_SSM), MXU_DTYPE),
            _slab_shape(L, D_SSM), jax.ShapeDtypeStruct((L, 3 * D_CONV), MXU_DTYPE),
            jax.ShapeDtypeStruct((1, D), F32), jax.ShapeDtypeStruct((1, D_SSM), F32), jax.ShapeDtypeStruct((1, D_CONV), F32),
            jax.ShapeDtypeStruct((SUBLANES, D_CONV), F32),
        ],
        scratch_shapes=[pltpu.VMEM((tm + SUBLANES, D_CONV), F32), pltpu.VMEM((tm + SUBLANES, D_CONV), F32)],
    )


def _s5_bwd(dys, u4, s_re, s_im, bmat, cmat, coef_rev, dskip, exchanges=()):
    L = dys.shape[1]
    tm = _tile(L, TM_S5)
    nt = L // tm
    lc = min(LANE_CHUNK, WB)

    def body(dys_ref, u_ref, sre_ref, sim_ref, bm_ref, cm_ref, coef_ref, d_ref,
             du_ref, gb_ref, gc_ref, q_ref, gd_ref, dr_ref, di_ref, lr_ref, li_ref, hr_ref, hi_ref, qr_acc, qi_acc, gb_acc, gc_acc):
        step = pl.program_id(1)

        @pl.when(step == 0)
        def _():
            for ref in (hr_ref, hi_ref, qr_acc, qi_acc, gb_acc, gc_acc, gd_ref):
                ref[...] = jnp.zeros_like(ref)

        dys_v = _load_permuted(dys_ref)
        u = _load_permuted(u_ref)
        dys_b = dys_v.astype(MXU_DTYPE)
        u_b = u.astype(MXU_DTYPE)
        d = _dot(dys_b, cm_ref[0])
        dr_ref[...] = d[:, :WB]
        di_ref[...] = d[:, WB:]
        for c in range(WB // lc):
            lanes = slice(c * lc, (c + 1) * lc)
            hr, hi = hr_ref[:, lanes], hi_ref[:, lanes]
            q = [qr_acc[:, lanes], qi_acc[:, lanes]]
            for b in range(tm // SEG_ROWS - 1, -1, -1):
                rows = lambda j, b=b: slice(b * SEG_ROWS + j * SUBLANES, b * SEG_ROWS + (j + 1) * SUBLANES)

                def read(j, rows=rows, lanes=lanes):
                    return dr_ref[rows(j), lanes], di_ref[rows(j), lanes]

                def write(j, xr, xi, rows=rows, lanes=lanes, q=q):
                    lr_ref[rows(j), lanes] = xr
                    li_ref[rows(j), lanes] = xi
                    er = xr - dr_ref[rows(j), lanes]
                    ei = xi - di_ref[rows(j), lanes]
                    sr = sre_ref[rows(j), lanes]
                    si = sim_ref[rows(j), lanes]
                    q[0] = q[0] + (er * sr + ei * si)
                    q[1] = q[1] + (ei * sr - er * si)

                hr, hi = _scan_block(read, write, hr, hi, coef_ref, lanes, True)
            hr_ref[:, lanes] = hr
            hi_ref[:, lanes] = hi
            qr_acc[:, lanes] = q[0]
            qi_acc[:, lanes] = q[1]
        lr_b = lr_ref[...].astype(MXU_DTYPE)
        li_b = li_ref[...].astype(MXU_DTYPE)
        _store_permuted(du_ref, _dot_nt(lr_b, bm_ref[0, :, :WB]) + _dot_nt(li_b, bm_ref[0, :, WB:]) + d_ref[0] * dys_v)
        sre_b = sre_ref[...].astype(MXU_DTYPE)
        sim_b = sim_ref[...].astype(MXU_DTYPE)
        for t in range(WB // DIAG_COLS):
            ch = slice(t * DIAG_ROWS, (t + 1) * DIAG_ROWS)
            cols = slice(t * DIAG_COLS, (t + 1) * DIAG_COLS)
            icols = slice(WB + t * DIAG_COLS, WB + (t + 1) * DIAG_COLS)
            gb_acc[ch, cols] += _dot_tn(u_b[:, ch], lr_b[:, cols])
            gb_acc[ch, icols] += _dot_tn(u_b[:, ch], li_b[:, cols])
            gc_acc[ch, cols] += _dot_tn(dys_b[:, ch], sre_b[:, cols])
            gc_acc[ch, icols] += _dot_tn(dys_b[:, ch], sim_b[:, cols])
        gd_ref[0] += jnp.sum(dys_v * u, axis=0, keepdims=True)

        @pl.when(step == nt - 1)
        def _():
            q_ref[0, 0:1, :] = jnp.sum(qr_acc[...], axis=0, keepdims=True)
            q_ref[0, 1:2, :] = jnp.sum(qi_acc[...], axis=0, keepdims=True)
            mask = _group_mask(DIAG_ROWS, DIAG_COLS)
            fold = (lax.broadcasted_iota(jnp.int32, (DIAG_COLS, STATE), 0) % STATE == lax.broadcasted_iota(jnp.int32, (DIAG_COLS, STATE), 1)).astype(F32)
            for acc, out in ((gb_acc, gb_ref), (gc_acc, gc_ref)):
                for k in range(2):
                    for t in range(WB // DIAG_COLS):
                        ch = slice(t * DIAG_ROWS, (t + 1) * DIAG_ROWS)
                        own = jnp.where(mask, acc[ch, k * WB + t * DIAG_COLS:k * WB + (t + 1) * DIAG_COLS], 0.0)
                        out[0, k, ch, :] = jnp.dot(own, fold, precision=lax.Precision.HIGHEST, preferred_element_type=F32)

    rev = lambda b, i: (nt - 1 - i, b)
    slab = lambda b, i: (b, nt - 1 - i, 0)
    blk = lambda b, i: (b, 0, 0)
    blk4 = lambda b, i: (b, 0, 0, 0)
    return _call(
        body, name="s5_bwd", grid=(N_GBLK, nt), exchanges=exchanges, semantics=("arbitrary", "arbitrary"),
        operands=(dys, u4, s_re, s_im, bmat, cmat, coef_rev, dskip),
        in_specs=[
            _slab_spec(UB, tm, slab), _slab_spec(UB, tm, slab), pl.BlockSpec((tm, WB), rev), pl.BlockSpec((tm, WB), rev),
            pl.BlockSpec((1, UB, 2 * WB), blk), pl.BlockSpec((1, UB, 2 * WB), blk),
            pl.BlockSpec((N_TABLES, SUBLANES, WB), lambda b, i: (0, 0, b)), pl.BlockSpec((1, 1, UB), blk),
        ],
        out_specs=[
            _slab_spec(UB, tm, slab), pl.BlockSpec((1, 2, UB, STATE), blk4), pl.BlockSpec((1, 2, UB, STATE), blk4),
            pl.BlockSpec((1, 2, WB), blk), pl.BlockSpec((1, 1, UB), blk),
        ],
        out_shape=[
            _slab_shape(L, D_SSM), jax.ShapeDtypeStruct((N_GBLK, 2, UB, STATE), F32),
            jax.ShapeDtypeStruct((N_GBLK, 2, UB, STATE), F32), jax.ShapeDtypeStruct((N_GBLK, 2, WB), F32),
            jax.ShapeDtypeStruct((N_GBLK, 1, UB), F32),
        ],
        scratch_shapes=[pltpu.VMEM((tm, WB), F32)] * 4 + [pltpu.VMEM((SUBLANES, WB), F32)] * 4 + [pltpu.VMEM((UB, 2 * WB), F32)] * 2,
    )


def _inproj_bwd(du, dhbc, x, dx1, w_in_all, g1):
    L, D = x.shape
    ns, _, nc = w_in_all.shape
    tm = _tile(L, TM_PROJ)

    def body(du_ref, dhbc_ref, x_ref, dx1_ref, w_ref, g_ref, gx_ref, dproj_ref, dg_ref):
        @pl.when(pl.program_id(0) == 0)
        def _():
            dg_ref[...] = jnp.zeros_like(dg_ref)

        du_b = _load_slabs(du_ref).astype(MXU_DTYPE)
        dproj_ref[:, :nc] = du_b
        dproj_ref[:, nc:] = dhbc_ref[...]
        dhn = _dot_nt(du_b, w_ref[0])
        for j in range(1, ns):
            dhn = dhn + _dot_nt(dhbc_ref[:, (j - 1) * nc:j * nc], w_ref[j])
        dx, dg = _rms_bwd(x_ref[...], g_ref[...], dhn)
        dg_ref[...] += dg
        gx_ref[...] = dx1_ref[...] + dx

    row = lambda i: (i, 0)
    return pl.pallas_call(
        body, name="inproj_bwd", grid=(L // tm,),
        in_specs=[_slab_spec(nc, tm), pl.BlockSpec((tm, (ns - 1) * nc), row), pl.BlockSpec((tm, D), row), pl.BlockSpec((tm, D), row),
                  _resident(w_in_all.shape), _resident(g1.shape)],
        out_specs=[pl.BlockSpec((tm, D), row), pl.BlockSpec((tm, ns * nc), row), pl.BlockSpec((1, D), lambda i: (0, 0))],
        out_shape=[jax.ShapeDtypeStruct((L, D), F32), jax.ShapeDtypeStruct((L, ns * nc), MXU_DTYPE), jax.ShapeDtypeStruct((1, D), F32)],
        compiler_params=_params(("arbitrary",)),
    )(du, dhbc, x, dx1, w_in_all, g1)


def _matmul_tn(a, b, name, col_shards=1, exchanges=()):
    L, K = a.shape
    N = b.shape[1]
    tl = _tile(L, TL_TN)
    tk = _tile(K, TK_TN)
    nw = N // col_shards
    spb = max(1, min(col_shards, TN_TN // nw))
    tn = spb * nw if spb > 1 else _tile(nw, TK_TN)
    if tn <= TK_TN:
        tl = _tile(L, 2 * TL_TN)
    npb = nw // tn if spb == 1 else 1

    def body(a_ref, b_ref, o_ref):
        @pl.when(pl.program_id(2) == 0)
        def _():
            o_ref[...] = jnp.zeros_like(o_ref)

        res = _dot_tn(a_ref[...], b_ref[...])
        for s in range(spb):
            o_ref[s] += res[:, s * nw:(s + 1) * nw] if spb > 1 else res

    out_block = pl.BlockSpec((spb, tk, nw if spb > 1 else tn), (lambda k, n, l: (n, k, 0)) if spb > 1 else (lambda k, n, l: (n // npb, k, n % npb)))
    return _call(
        body, name=name, grid=(K // tk, N // tn, L // tl), exchanges=exchanges, semantics=("arbitrary", "arbitrary", "arbitrary"),
        operands=(a, b),
        in_specs=[pl.BlockSpec((tl, tk), lambda k, n, l: (l, k)), pl.BlockSpec((tl, tn), lambda k, n, l: (l, n))],
        out_specs=[out_block],
        out_shape=[jax.ShapeDtypeStruct((col_shards, K, nw), F32)],
    )


def _ssm_discretize(lam_re, lam_im, log_dt, bt_re, bt_im):
    dt = jnp.exp(log_dt)[:, None]
    zr = lam_re * dt
    zi = lam_im * dt
    mag = jnp.exp(zr)
    abr = mag * jnp.cos(zi)
    abi = mag * jnp.sin(zi)
    nr, ni = abr - 1.0, abi
    den = lam_re * lam_re + lam_im * lam_im
    coef_r = ((nr * lam_re + ni * lam_im) / den)[:, None, :]
    coef_i = ((ni * lam_re - nr * lam_im) / den)[:, None, :]
    return zr, zi, coef_r * bt_re - coef_i * bt_im, coef_r * bt_im + coef_i * bt_re


def _scan_tables(ar, ai, reverse):
    rows = np.arange(SUBLANES)
    exps = np.zeros((N_TABLES // 2, SUBLANES), np.int32)
    keep = np.ones((N_TABLES // 2, SUBLANES), bool)
    for t, k in enumerate((1, 2, 4)):
        exps[t] = SUBLANES * k
        keep[t] = (rows + k <= SUBLANES - 1) if reverse else (rows >= k)
    exps[3] = SUBLANES * (SUBLANES - rows) if reverse else SUBLANES * (rows + 1)
    for j in range(SUBLANES):
        exps[4 + j] = SUBLANES - j if reverse else j + 1
    pr, pi = ar, (-ai if reverse else ai)
    shape = (N_TABLES // 2, SUBLANES, ar.shape[0])
    xr, xi = jnp.ones(shape, F32), jnp.zeros(shape, F32)
    for bit in range(int(exps.max()).bit_length()):
        on = ((exps >> bit) & 1).astype(bool)[:, :, None]
        xr, xi = jnp.where(on, xr * pr - xi * pi, xr), jnp.where(on, xr * pi + xi * pr, xi)
        pr, pi = pr * pr - pi * pi, 2.0 * pr * pi
    xr = jnp.where(keep[:, :, None], xr, 0.0)
    xi = jnp.where(keep[:, :, None], xi, 0.0)
    return jnp.stack([xr, xi], axis=1).reshape(N_TABLES, SUBLANES, ar.shape[0])


def _group_mask(rows, cols):
    r = lax.broadcasted_iota(jnp.int32, (rows, cols), 0) // GROUP
    c = lax.broadcasted_iota(jnp.int32, (rows, cols), 1) // STATE
    return r == c


def _ssm_expand(bt_re, bt_im, c_re, c_im, exchanges=()):
    flat = lambda a: a.reshape(N_GROUPS * GROUP, STATE)

    def body(br_ref, bi_ref, cr_ref, ci_ref, bm_ref, cm_ref):
        spread = (lax.broadcasted_iota(jnp.int32, (STATE, WB), 1) % STATE == lax.broadcasted_iota(jnp.int32, (STATE, WB), 0)).astype(F32)
        mask = _group_mask(UB, WB)

        def expand(x):
            wide = jnp.dot(x, spread, precision=lax.Precision.HIGHEST, preferred_element_type=F32)
            return jnp.where(mask, wide, 0.0).astype(MXU_DTYPE)

        bm_ref[0, :, :WB] = expand(br_ref[...])
        bm_ref[0, :, WB:] = expand(bi_ref[...])
        cm_ref[0, :, :WB] = expand(cr_ref[...])
        cm_ref[0, :, WB:] = expand(-ci_ref[...])

    spec = pl.BlockSpec((UB, STATE), lambda b: (b, 0))
    out = pl.BlockSpec((1, UB, 2 * WB), lambda b: (b, 0, 0))
    return _call(
        body, name="ssm_expand", grid=(N_GBLK,), exchanges=exchanges, semantics=("arbitrary",),
        operands=(flat(bt_re), flat(bt_im), flat(c_re), flat(c_im)), in_specs=[spec] * 4, out_specs=[out, out],
        out_shape=[jax.ShapeDtypeStruct((N_GBLK, UB, 2 * WB), MXU_DTYPE)] * 2,
    )


def _ssm_matrices(lam_re, lam_im, log_dt, bt_re, bt_im, c_re, c_im, exchanges=()):
    zr, zi, bbar_r, bbar_i = _ssm_discretize(lam_re, lam_im, log_dt, bt_re, bt_im)
    mag = jnp.exp(zr)
    ar = (mag * jnp.cos(zi)).reshape(-1)
    ai = (mag * jnp.sin(zi)).reshape(-1)
    bmat, cmat, *rest = _ssm_expand(bbar_r, bbar_i, c_re, c_im, exchanges)
    return bmat, cmat, _scan_tables(ar, ai, False), _scan_tables(ar, ai, True), rest


def _ssm_param_grads(lam_re, lam_im, log_dt, bt_re, bt_im, gb, gc, q, gd):
    part = lambda g, k: g[:, k].reshape(N_GROUPS, GROUP, STATE)
    qr = q[:, 0, :].reshape(N_GROUPS, STATE)
    qi = q[:, 1, :].reshape(N_GROUPS, STATE)
    _, vjp = jax.vjp(_ssm_discretize, lam_re, lam_im, log_dt, bt_re, bt_im)
    d_lam_re, d_lam_im, d_log_dt, d_bt_re, d_bt_im = vjp((qr, qi, part(gb, 0), part(gb, 1)))
    return d_lam_re, d_lam_im, d_log_dt, d_bt_re, d_bt_im, part(gc, 0), -part(gc, 1), gd.reshape(N_GROUPS, GROUP)


def _row_tile(rows, n):
    return _tile(rows, max(SUBLANES, (2 * 1024 * 1024) // (4 * n)))


def _pair_add(grad, other, core, name):
    ns, h, n = other.shape
    tr = _row_tile(h, n)
    nb = h // tr

    def body(c_ref, g_ref, o_ref, out_ref):
        out_ref[...] = (g_ref[...] + o_ref[...]).astype(WIRE_DTYPE)

    return pl.pallas_call(
        body, name=name,
        grid_spec=pltpu.PrefetchScalarGridSpec(
            num_scalar_prefetch=1, grid=(ns, nb),
            in_specs=[pl.BlockSpec((1, tr, n), lambda s, i, c: (s, c[0] * nb + i, 0)), pl.BlockSpec((1, tr, n), lambda s, i, c: (s, i, 0))],
            out_specs=pl.BlockSpec((1, tr, n), lambda s, i, c: (s, i, 0))),
        out_shape=jax.ShapeDtypeStruct(other.shape, WIRE_DTYPE),
        compiler_params=_params(("arbitrary", "arbitrary")),
    )(core, grad, other)


def _quad_sum(parts, core, name):
    ns, h, n = parts.shape
    tr = _row_tile(h, n)
    nb = h // tr

    def body(c_ref, p_ref, out_ref):
        p = [p_ref[k].astype(F32) for k in range(ns)]
        out_ref[...] = ((p[0] + p[1]) + p[2]) + p[3]

    return pl.pallas_call(
        body, name=name,
        grid_spec=pltpu.PrefetchScalarGridSpec(
            num_scalar_prefetch=1, grid=(nb,),
            in_specs=[pl.BlockSpec((ns, tr, n), lambda i, c: (0, i, 0))],
            out_specs=pl.BlockSpec((tr, n), lambda i, c: (c[0] * nb + i, 0))),
        out_shape=jax.ShapeDtypeStruct((2 * h, n), F32),
        compiler_params=_params(("arbitrary",)),
    )(core, parts)


def _adamw_math(w, g, m, v):
    m = ADAM_B1 * m + (1.0 - ADAM_B1) * g
    v = ADAM_B2 * v + (1.0 - ADAM_B2) * jnp.square(g)
    m_hat = m / (1.0 - ADAM_B1 ** ADAM_STEP)
    v_hat = v / (1.0 - ADAM_B2 ** ADAM_STEP)
    delta = -ADAM_LR * (m_hat / (jnp.sqrt(v_hat) + ADAM_EPS) + ADAM_WD * w)
    return delta, m, v


def _adamw(w, g, m, v, name):
    r, n = w.shape
    tr = _row_tile(r, n)

    def body(w_ref, g_ref, m_ref, v_ref, d_ref, nm_ref, nv_ref):
        d_ref[...], nm_ref[...], nv_ref[...] = _adamw_math(w_ref[...], g_ref[...], m_ref[...], v_ref[...])

    spec = pl.BlockSpec((tr, n), lambda i: (i, 0))
    return pl.pallas_call(
        body, name=name, grid=(r // tr,), in_specs=[spec] * 4, out_specs=[spec] * 3,
        out_shape=[jax.ShapeDtypeStruct((r, n), F32)] * 3,
        compiler_params=_params(("arbitrary",)),
    )(w, g, m, v)


LANES = 128
SMALL = ["g_pre_mix", "lam_re", "lam_im", "log_dt", "b_re", "b_im", "c_re", "c_im", "d_skip", "conv_w", "g_ssm_out", "g_conv_out",
         "g_post_mix", "g_pre_mlp", "g_post_mlp"]
TILE_SLOTS = {"b_re": (0, N_GROUPS), "b_im": (N_GROUPS, N_GROUPS), "c_re": (2 * N_GROUPS, N_GROUPS), "c_im": (3 * N_GROUPS, N_GROUPS),
              "lam_re": (4 * N_GROUPS, 2), "lam_im": (4 * N_GROUPS + 2, 2)}
N_TILE_SLOTS = 4 * N_GROUPS + 4
VEC_ROWS = {"g_pre_mix": 0, "g_post_mix": 1, "g_pre_mlp": 2, "g_post_mlp": 3, "g_ssm_out": 4, "g_conv_out": 5, "log_dt": 6}
ROW_LOSS, ROW_DSKIP, ROW_CONV, N_PACK_ROWS = 7, 8, 24, 32


def _kernel_form(name, a):
    if name in ("b_re", "b_im"):
        return jnp.transpose(a, (0, 1, 3, 2)).reshape(N_GROUPS, GROUP, STATE)
    if name in ("c_re", "c_im"):
        return a.reshape(N_GROUPS, GROUP, STATE)
    if name in ("lam_re", "lam_im"):
        return a.reshape(2, GROUP, STATE)
    if name == "d_skip":
        return jnp.transpose(a, (0, 2, 1)).reshape(GROUP, N_GROUPS)
    if name == "conv_w":
        return jnp.transpose(a, (1, 0, 2))
    return a


def _param_form(name, k):
    if name in ("b_re", "b_im"):
        return jnp.transpose(k.reshape(1, N_GROUPS, GROUP, STATE), (0, 1, 3, 2))
    if name in ("c_re", "c_im"):
        return k.reshape(1, N_GROUPS, GROUP, STATE)
    if name in ("lam_re", "lam_im"):
        return k.reshape(1, N_GROUPS, STATE)
    if name == "d_skip":
        return jnp.transpose(k.reshape(1, GROUP, N_GROUPS), (0, 2, 1))
    if name == "conv_w":
        return jnp.transpose(k, (1, 0, 2))
    return k


def _pack_tiles(g):
    lam = lambda a: a.reshape(2, GROUP, STATE)
    tiles = jnp.concatenate([g["b_re"], g["b_im"], g["c_re"], g["c_im"], lam(g["lam_re"]), lam(g["lam_im"])], axis=0)
    return tiles.astype(WIRE_DTYPE)


def _pack_rows(g, loss):
    row = lambda a: jnp.pad(a, ((0, 0), (0, D_MODEL - a.shape[1])))
    rows = [row(g[k][None]) for k in VEC_ROWS] + [row(loss[0:1]), row(g["d_skip"].T), row(g["conv_w"])]
    rows.append(jnp.zeros((N_PACK_ROWS - ROW_CONV - 3, D_MODEL), F32))
    return jnp.concatenate(rows, axis=0)


def _adamw_small(tiles, rows, w, m, v):
    nn = len(SMALL)

    def body(*refs):
        t_ref, r_ref = refs[0], refs[1]
        w_refs, m_refs, v_refs = refs[2:2 + nn], refs[2 + nn:2 + 2 * nn], refs[2 + 2 * nn:2 + 3 * nn]
        loss_ref, outs = refs[2 + 3 * nn], refs[3 + 3 * nn:]

        def tile_sum(first, count):
            total = t_ref[0, first:first + count].astype(F32)
            for d in range(1, N_DEV):
                total = total + t_ref[d, first:first + count].astype(F32)
            return total

        def row_sum(first, count, lanes):
            total = r_ref[0, first:first + count, 0:lanes]
            for d in range(1, N_DEV):
                total = total + r_ref[d, first:first + count, 0:lanes]
            return total

        def step(j, g, at=lambda ref: ref):
            delta, nm, nv = _adamw_math(at(w_refs[j])[...], g, at(m_refs[j])[...], at(v_refs[j])[...])
            at(outs[j])[...] = g
            at(outs[nn + j])[...] = delta
            at(outs[2 * nn + j])[...] = nm
            at(outs[3 * nn + j])[...] = nv

        loss_ref[...] = row_sum(ROW_LOSS, 1, LANES)
        chip = 2 * lax.axis_index("x") + lax.axis_index("y")
        for j, name in enumerate(SMALL):
            if name in TILE_SLOTS:
                step(j, tile_sum(*TILE_SLOTS[name]))
            elif name == "d_skip":
                step(j, row_sum(ROW_DSKIP, GROUP, N_GROUPS))
            elif name == "conv_w":
                full = row_sum(ROW_CONV, 3, D_CONV)
                mine = full[:, 0:LANES]
                for s in range(1, N_CHIPS):
                    mine = jnp.where(chip == s, full[:, s * LANES:(s + 1) * LANES], mine)
                for k in range(3):
                    step(j, mine[k:k + 1, :], at=lambda ref, k=k: ref.at[k])
            else:
                step(j, row_sum(VEC_ROWS[name], 1, w_refs[j].shape[1]))

    args = [tiles, rows] + [w[k] for k in SMALL] + [m[k] for k in SMALL] + [v[k] for k in SMALL]
    res = pl.pallas_call(
        body, name="adamw_small", in_specs=[VMEM] * len(args), out_specs=[VMEM] * (1 + 4 * nn),
        out_shape=[jax.ShapeDtypeStruct((1, LANES), F32)] + [jax.ShapeDtypeStruct(w[k].shape, F32) for k in SMALL] * 4,
        compiler_params=pltpu.CompilerParams(vmem_limit_bytes=VMEM_LIMIT),
    )(*args)
    return res[0], [dict(zip(SMALL, res[1 + q * nn:1 + (q + 1) * nn])) for q in range(4)]


WEIGHTS = ["g_pre_mix", "w_in", "lam_re", "lam_im", "log_dt", "b_re", "b_im", "c_re", "c_im", "d_skip", "w_glu", "conv_w",
           "g_ssm_out", "g_conv_out", "w_out", "g_post_mix", "g_pre_mlp", "w_up", "w_down", "g_post_mlp"]
BIG = ["w_in", "w_glu", "w_out", "w_up", "w_down"]


def kernel(x, g_pre_mix, w_in, lam_re, lam_im, log_dt, b_re, b_im, c_re, c_im, d_skip, w_glu, conv_w, g_ssm_out, g_conv_out, w_out, g_post_mix, g_pre_mlp, w_up, w_down, g_post_mlp, loss_target, m_g_pre_mix, m_w_in, m_lam_re, m_lam_im, m_log_dt, m_b_re, m_b_im, m_c_re, m_c_im, m_d_skip, m_w_glu, m_conv_w, m_g_ssm_out, m_g_conv_out, m_w_out, m_g_post_mix, m_g_pre_mlp, m_w_up, m_w_down, m_g_post_mlp, v_g_pre_mix, v_w_in, v_lam_re, v_lam_im, v_log_dt, v_b_re, v_b_im, v_c_re, v_c_im, v_d_skip, v_w_glu, v_conv_w, v_g_ssm_out, v_g_conv_out, v_w_out, v_g_post_mix, v_g_pre_mlp, v_w_up, v_w_down, v_g_post_mlp):
    w = dict(g_pre_mix=g_pre_mix, w_in=w_in, lam_re=lam_re, lam_im=lam_im, log_dt=log_dt, b_re=b_re, b_im=b_im, c_re=c_re, c_im=c_im,
             d_skip=d_skip, w_glu=w_glu, conv_w=conv_w, g_ssm_out=g_ssm_out, g_conv_out=g_conv_out, w_out=w_out, g_post_mix=g_post_mix,
             g_pre_mlp=g_pre_mlp, w_up=w_up, w_down=w_down, g_post_mlp=g_post_mlp)
    m = dict(g_pre_mix=m_g_pre_mix, w_in=m_w_in, lam_re=m_lam_re, lam_im=m_lam_im, log_dt=m_log_dt, b_re=m_b_re, b_im=m_b_im, c_re=m_c_re,
             c_im=m_c_im, d_skip=m_d_skip, w_glu=m_w_glu, conv_w=m_conv_w, g_ssm_out=m_g_ssm_out, g_conv_out=m_g_conv_out, w_out=m_w_out,
             g_post_mix=m_g_post_mix, g_pre_mlp=m_g_pre_mlp, w_up=m_w_up, w_down=m_w_down, g_post_mlp=m_g_post_mlp)
    v = dict(g_pre_mix=v_g_pre_mix, w_in=v_w_in, lam_re=v_lam_re, lam_im=v_lam_im, log_dt=v_log_dt, b_re=v_b_re, b_im=v_b_im, c_re=v_c_re,
             c_im=v_c_im, d_skip=v_d_skip, w_glu=v_w_glu, conv_w=v_conv_w, g_ssm_out=v_g_ssm_out, g_conv_out=v_g_conv_out, w_out=v_w_out,
             g_post_mix=v_g_post_mix, g_pre_mlp=v_g_pre_mlp, w_up=v_w_up, w_down=v_w_down, g_post_mlp=v_g_post_mlp)
    w_dev, m_dev, v_dev = w, m, v
    w, m, v = ({k: a[0] for k, a in d.items()} for d in (w, m, v))
    core = lax.axis_index("c").astype(jnp.int32).reshape(1)

    xs, target = x[0], loss_target[0]
    g1 = w["g_pre_mix"][None]
    g_ssm, g_conv = w["g_ssm_out"][None], w["g_conv_out"][None]
    g_post_mix, g_pre_mlp, g_post_mlp = w["g_post_mix"][None], w["g_pre_mlp"][None], w["g_post_mlp"][None]
    bt_re, bt_im = (jnp.transpose(w[k], (0, 2, 1)) for k in ("b_re", "b_im"))
    dskip = w["d_skip"].reshape(N_GBLK, 1, UB)
    shard = {k: w[k].astype(MXU_DTYPE) for k in BIG}
    conv_pad = jnp.pad(w["conv_w"], ((0, SUBLANES - 3), (0, 0)))

    bmat, cmat, coef_f, coef_r, (w_in_all,) = _ssm_matrices(
        w["lam_re"], w["lam_im"], w["log_dt"], bt_re, bt_im, w["c_re"], w["c_im"], exchanges=[_GatherForward([shard["w_in"]])])
    hn, proj, u4, w_glu_all, w_out_all, conv_all = _inproj_fwd(
        xs, g1, w_in_all, exchanges=[_Gather([shard["w_glu"], shard["w_out"], conv_pad], [False, False, False])])
    wd_half = shard["w_down"].shape[0] // 2
    wda, wdb = shard["w_down"][:wd_half], shard["w_down"][wd_half:]
    s_re, s_im, ys, w_up_all, wda_all = _s5_fwd(u4, bmat, cmat, coef_f, dskip, exchanges=[_Gather([shard["w_up"], wda], [True, True])])
    w_glu_f, w_out_f = w_glu_all.reshape(D_SSM, D_SSM), w_out_all.reshape(D_MODEL, D_MODEL)
    conv_f = jnp.transpose(conv_all, (1, 0, 2)).reshape(SUBLANES, D_CONV)
    ycat, o, x1, w_up_all, wda_all, wdb_all = _tail_fwd(xs, ys, proj, w_glu_f, conv_f, g_ssm, g_conv, w_out_f, g_post_mix,
                                                        exchanges=[_Forward([w_up_all, wda_all]), _GatherForward([wdb])])
    hn2, up, act, m_act, dx2, loss = _mlp_fwd(x1, target, w_up_all, wda_all, wdb_all, g_pre_mlp, g_post_mlp)

    dm, dup, dx1, dg_post_mlp, dg_pre_mlp = _mlp_bwd(dx2, m_act, up, x1, w_up_all, wda_all, wdb_all, g_pre_mlp, g_post_mlp)
    gw_down = _matmul_tn(act, dm, "dw_down")[0].reshape(N_CHIPS, D_FF // N_CHIPS, D_MODEL)
    gw_up = _matmul_tn(hn2, dup, "dw_up", col_shards=N_CHIPS)[0]
    do, da, y1, dys, dhbc, dg_post_mix, dg_ssm, dg_conv, dconv_w, o_down, o_up = _tail_bwd(
        dx1, o, ys, proj, w_glu_f, conv_f, g_ssm, g_conv, w_out_f, g_post_mix, exchanges=[_Pair([gw_down, gw_up])])
    p_down = _pair_add(gw_down, o_down, core, "pair_add_w_down")
    p_up = _pair_add(gw_up, o_up, core, "pair_add_w_up")
    gw_out = _matmul_tn(ycat, do, "dw_out")[0].reshape(N_CHIPS, D_MODEL // N_CHIPS, D_MODEL)
    gw_glu = _matmul_tn(y1, da, "dw_glu")[0].reshape(N_CHIPS, D_SSM // N_CHIPS, D_SSM)
    du, gb, gc, q, gd, q_down, q_up, o_out, o_glu = _s5_bwd(
        dys, u4, s_re, s_im, bmat, cmat, coef_r, dskip, exchanges=[_Chip([p_down, p_up]), _Pair([gw_out, gw_glu])])
    h_down = _quad_sum(q_down, core, "quad_sum_w_down")
    h_up = _quad_sum(q_up, core, "quad_sum_w_up")
    p_out = _pair_add(gw_out, o_out, core, "pair_add_w_out")
    p_glu = _pair_add(gw_glu, o_glu, core, "pair_add_w_glu")
    grad_x, dproj, dg_pre_mix = _inproj_bwd(du, dhbc, xs, dx1, w_in_all, g1)
    d_lam_re, d_lam_im, d_log_dt, d_b_re, d_b_im, d_c_re, d_c_im, d_d_skip = _ssm_param_grads(
        w["lam_re"], w["lam_im"], w["log_dt"], bt_re, bt_im, gb, gc, q, gd)
    small = {
        "g_pre_mix": dg_pre_mix[0], "lam_re": d_lam_re, "lam_im": d_lam_im, "log_dt": d_log_dt, "b_re": d_b_re, "b_im": d_b_im,
        "c_re": d_c_re, "c_im": d_c_im, "d_skip": d_d_skip, "conv_w": dconv_w[:3], "g_ssm_out": dg_ssm[0], "g_conv_out": dg_conv[0],
        "g_post_mix": dg_post_mix[0], "g_pre_mlp": dg_pre_mlp[0], "g_post_mlp": dg_post_mlp[0],
    }
    gw_in, g_down, g_up, q_out, q_glu, tiles, rows = _matmul_tn(
        hn, dproj, "dw_in", col_shards=N_CHIPS,
        exchanges=[_Share([h_down, h_up]), _Chip([p_out, p_glu]), _GatherSmall(_pack_tiles(small)), _GatherSmall(_pack_rows(small, loss))])
    h_out = _quad_sum(q_out, core, "quad_sum_w_out")
    h_glu = _quad_sum(q_glu, core, "quad_sum_w_glu")
    (o_in,) = _run_exchanges([_Pair([gw_in])], "rs_pair_w_in")
    p_in = _pair_add(gw_in, o_in, core, "pair_add_w_in")
    q_in, g_out, g_glu = _run_exchanges([_Chip([p_in]), _Share([h_out, h_glu])], "rs_chip_w_in")
    h_in = _quad_sum(q_in, core, "quad_sum_w_in")
    (g_in,) = _run_exchanges([_Share([h_in])], "rs_share_w_in")
    shard_grads = {"w_in": g_in, "w_glu": g_glu, "w_out": g_out, "w_up": g_up, "w_down": g_down}

    out = {q: {} for q in ("grad", "delta", "new_m", "new_v")}
    for k in BIG:
        out["grad"][k] = shard_grads[k][None]
        delta, new_m, new_v = _adamw(w[k], shard_grads[k], m[k], v[k], "adamw_" + k)
        out["delta"][k], out["new_m"][k], out["new_v"][k] = delta[None], new_m[None], new_v[None]
    form = lambda d: {k: _kernel_form(k, d[k]) for k in SMALL}
    loss, res = _adamw_small(tiles, rows, form(w_dev), form(m_dev), form(v_dev))
    for q, d in zip(("grad", "delta", "new_m", "new_v"), res):
        out[q].update({k: _param_form(k, d[k]) for k in SMALL})
    flat = [loss[0, 0], grad_x[None]]
    for q in ("grad", "delta", "new_m", "new_v"):
        flat += [out[q][k] for k in WEIGHTS]
    return tuple(flat)
```

```python
import functools
import math

import jax
import jax.numpy as jnp
import numpy as np
from jax import lax
from jax.experimental import pallas as pl
from jax.experimental.pallas import tpu as pltpu

F32 = jnp.float32
MXU_DTYPE = jnp.bfloat16
WIRE_DTYPE = jnp.bfloat16

D_MODEL = 1024
D_SSM = 512
D_CONV = 512
N_GROUPS = 32
GROUP = 16
STATE = 64
D_FF = 4096
RMS_EPS = 1e-6
N_CHIPS = 4
N_DEV = 8

ADAM_LR = 0.001
ADAM_B1 = 0.9
ADAM_B2 = 0.999
ADAM_EPS = 1e-08
ADAM_WD = 0.01
ADAM_STEP = 10

N_GBLK = 2
G_PER_BLK = N_GROUPS // N_GBLK
UB = G_PER_BLK * GROUP
WB = G_PER_BLK * STATE
LANE_CHUNK = 256
SUBLANES = 8
N_TABLES = 24
DIAG_COLS = 256
DIAG_ROWS = DIAG_COLS // STATE * GROUP

TM_PROJ = 512
TM_S5 = 512
TM_TAIL = 512
TM_MLP = 512
TL_TN = 2048
TK_TN = 1024
TN_TN = 2048
VMEM_LIMIT = 56 * 1024 * 1024

MESH = pl.DeviceIdType.MESH


def _params(sem, vmem=VMEM_LIMIT):
    return pltpu.CompilerParams(dimension_semantics=sem, vmem_limit_bytes=vmem)


def _resident(shape):
    nd = len(shape)
    return pl.BlockSpec(shape, lambda *_: (0,) * nd, pipeline_mode=pl.Buffered(1))


def _dot(a, b):
    return jnp.dot(a, b, preferred_element_type=F32)


def _dot_nt(a, b):
    return lax.dot_general(a, b, (((1,), (1,)), ((), ())), preferred_element_type=F32)


def _dot_tn(a, b):
    return lax.dot_general(a, b, (((0,), (0,)), ((), ())), preferred_element_type=F32)


def _rms_fwd(x, g):
    r = lax.rsqrt(jnp.mean(x * x, axis=-1, keepdims=True) + RMS_EPS)
    return x * r * g


def _rms_bwd(x, g, dy):
    r = lax.rsqrt(jnp.mean(x * x, axis=-1, keepdims=True) + RMS_EPS)
    xn = x * r
    q = dy * g
    dx = r * (q - xn * jnp.mean(q * xn, axis=-1, keepdims=True))
    return dx, jnp.sum(dy * xn, axis=0, keepdims=True)


_GELU_C = math.sqrt(2.0 / math.pi)


def _gelu(x):
    t = jnp.tanh(_GELU_C * (x + 0.044715 * (x * x * x)))
    y = x * (0.5 * (1.0 + t))
    dy = 0.5 * (1.0 + t) + 0.5 * x * (1.0 - t * t) * (_GELU_C * (1.0 + 3 * 0.044715 * (x * x)))
    return y, dy


def _tile(n, pref):
    t = min(n, pref)
    assert n % t == 0, (n, t)
    return t


HBM = pl.BlockSpec(memory_space=pltpu.HBM)
VMEM = pl.BlockSpec(memory_space=pltpu.VMEM)
DMA_SEMS = pltpu.SemaphoreType.DMA


def _place():
    x, y, c = lax.axis_index("x"), lax.axis_index("y"), lax.axis_index("c")
    chips = [(1 - x, y), (x, 1 - y), (1 - x, 1 - y)]
    return (x, y, c), 2 * x + y, (x, y, 1 - c), chips, [2 * px + py for px, py in chips]


def _remote(src, dst, send_sem, recv_sem, device):
    return pltpu.make_async_remote_copy(src_ref=src, dst_ref=dst, send_sem=send_sem, recv_sem=recv_sem,
                                        device_id=device, device_id_type=MESH)


def _half(rows, c):
    return pl.ds(c * (rows // 2), rows // 2)


class _Exchange:
    aliases = {}

    def start(self, ins, outs, sems):
        local, outgoing, _ = self._copies(ins, outs, sems)
        for cp in local + outgoing:
            cp.start()

    def finish(self, ins, outs, sems):
        local, outgoing, incoming = self._copies(ins, outs, sems)
        for cp in incoming:
            cp.wait_recv()
        for cp in outgoing:
            cp.wait_send()
        for cp in local:
            cp.wait()


class _Gather(_Exchange):
    def __init__(self, shards, split):
        self.inputs, self.split = list(shards), split
        self.out_shape = [jax.ShapeDtypeStruct((N_CHIPS, *a.shape), a.dtype) for a in shards]
        self.sems = [DMA_SEMS((len(shards), 3)), DMA_SEMS((len(shards), 3)), DMA_SEMS((len(shards),))]

    def _copies(self, ins, outs, sems):
        send, recv, lsem = sems
        (x, y, c), me, sibling, chips, ids = _place()
        local = [pltpu.make_async_copy(ins[t], outs[t].at[me], lsem.at[t]) for t in range(len(ins))]
        outgoing, incoming = [], []
        for t, a in enumerate(self.inputs):
            rows = _half(a.shape[0], c) if self.split[t] else pl.ds(0, a.shape[0])
            for k in range(3):
                to = (*chips[k], c)
                outgoing.append(_remote(ins[t].at[rows, :], outs[t].at[me, rows, :], send.at[t, k], recv.at[t, k], to))
                incoming.append(_remote(ins[t].at[rows, :], outs[t].at[ids[k], rows, :], send.at[t, k], recv.at[t, k], to))
        return local, outgoing, incoming


class _Forward(_Exchange):
    def __init__(self, arrays):
        self.inputs = list(arrays)
        self.out_shape = [jax.ShapeDtypeStruct(a.shape, a.dtype) for a in arrays]
        self.aliases = {t: t for t in range(len(arrays))}
        self.sems = [DMA_SEMS((len(arrays), 3)), DMA_SEMS((len(arrays), 3))]

    def _copies(self, ins, outs, sems):
        send, recv = sems
        (x, y, c), me, sibling, chips, ids = _place()
        outgoing, incoming = [], []
        for t, a in enumerate(self.inputs):
            for k in range(3):
                mine = outs[t].at[ids[k], _half(a.shape[1], c), :]
                theirs = outs[t].at[ids[k], _half(a.shape[1], 1 - c), :]
                outgoing.append(_remote(mine, mine, send.at[t, k], recv.at[t, k], sibling))
                incoming.append(_remote(theirs, theirs, send.at[t, k], recv.at[t, k], sibling))
        return [], outgoing, incoming


class _GatherForward(_Exchange):
    def __init__(self, shards):
        self.gather = _Gather(shards, [True] * len(shards))
        self.forward = _Forward(self.gather.out_shape)
        self.inputs, self.out_shape = self.gather.inputs, self.gather.out_shape
        self.sems = self.gather.sems + self.forward.sems

    def start(self, ins, outs, sems):
        self.gather.start(ins, outs, sems[:3])

    def finish(self, ins, outs, sems):
        local, outgoing, incoming = self.gather._copies(ins, outs, sems[:3])
        _, passed, from_sibling = self.forward._copies(outs, outs, sems[3:])
        for landed, onward in zip(incoming, passed):
            landed.wait_recv()
            onward.start()
        for cp in from_sibling:
            cp.wait_recv()
        for cp in outgoing + passed:
            cp.wait_send()
        for cp in local:
            cp.wait()


class _Pair(_Exchange):
    def __init__(self, grads):
        self.inputs = list(grads)
        self.out_shape = [jax.ShapeDtypeStruct((g.shape[0], g.shape[1] // 2, g.shape[2]), g.dtype) for g in grads]
        self.sems = [DMA_SEMS((len(grads),)), DMA_SEMS((len(grads),))]

    def _copies(self, ins, outs, sems):
        send, recv = sems
        (x, y, c), me, sibling, chips, ids = _place()
        cps = [_remote(ins[t].at[:, _half(g.shape[1], 1 - c), :], outs[t], send.at[t], recv.at[t], sibling)
               for t, g in enumerate(self.inputs)]
        return [], cps, cps


class _Chip(_Exchange):
    def __init__(self, parts):
        self.inputs = list(parts)
        self.out_shape = [jax.ShapeDtypeStruct(p.shape, p.dtype) for p in parts]
        self.sems = [DMA_SEMS((len(parts), 3)), DMA_SEMS((len(parts), 3)), DMA_SEMS((len(parts),))]

    def _copies(self, ins, outs, sems):
        send, recv, lsem = sems
        (x, y, c), me, sibling, chips, ids = _place()
        local = [pltpu.make_async_copy(ins[t].at[me], outs[t].at[me], lsem.at[t]) for t in range(len(ins))]
        outgoing, incoming = [], []
        for t in range(len(ins)):
            for k in range(3):
                to = (*chips[k], c)
                outgoing.append(_remote(ins[t].at[ids[k]], outs[t].at[me], send.at[t, k], recv.at[t, k], to))
                incoming.append(_remote(ins[t].at[ids[k]], outs[t].at[ids[k]], send.at[t, k], recv.at[t, k], to))
        return local, outgoing, incoming


class _Share(_Exchange):
    def __init__(self, grads):
        self.inputs = list(grads)
        self.out_shape = [jax.ShapeDtypeStruct(g.shape, g.dtype) for g in grads]
        self.aliases = {t: t for t in range(len(grads))}
        self.sems = [DMA_SEMS((len(grads),)), DMA_SEMS((len(grads),))]

    def _copies(self, ins, outs, sems):
        send, recv = sems
        (x, y, c), me, sibling, chips, ids = _place()
        outgoing, incoming = [], []
        for t, g in enumerate(self.inputs):
            mine = outs[t].at[_half(g.shape[0], c), :]
            theirs = outs[t].at[_half(g.shape[0], 1 - c), :]
            outgoing.append(_remote(mine, mine, send.at[t], recv.at[t], sibling))
            incoming.append(_remote(theirs, theirs, send.at[t], recv.at[t], sibling))
        return [], outgoing, incoming


class _GatherSmall(_Exchange):
    def __init__(self, block):
        self.inputs = [block]
        self.out_shape = [jax.ShapeDtypeStruct((N_DEV, *block.shape), block.dtype)]
        self.sems = [DMA_SEMS((7,)), DMA_SEMS((7,)), DMA_SEMS(())]

    def _copies(self, ins, outs, sems):
        send, recv, lsem = sems
        (x, y, c), me, sibling, chips, ids = _place()
        slot = lambda px, py, pc: outs[0].at[4 * px + 2 * py + pc]

        def copy(k, block, to, src=None):
            return _remote(slot(*block) if src is None else src, slot(*block), send.at[k], recv.at[k], to)

        local = [pltpu.make_async_copy(ins[0], slot(x, y, c), lsem)]
        first = [copy(0, (x, y, c), sibling, src=ins[0])] + [copy(1 + j, (x, y, c), (*chip, c), src=ins[0]) for j, chip in enumerate(chips)]
        passed = [copy(4 + j, (*chip, c), sibling) for j, chip in enumerate(chips)]
        landed = [copy(1 + j, (*chip, c), (x, y, c)) for j, chip in enumerate(chips)]
        from_sibling = [copy(0, (x, y, 1 - c), (x, y, c))] + [copy(4 + j, (*chip, 1 - c), (x, y, c)) for j, chip in enumerate(chips)]
        return local, first, (passed, landed, from_sibling)

    def finish(self, ins, outs, sems):
        local, first, (passed, landed, from_sibling) = self._copies(ins, outs, sems)
        for j in range(3):
            landed[j].wait_recv()
            passed[j].start()
        for cp in from_sibling:
            cp.wait_recv()
        for cp in first + passed:
            cp.wait_send()
        for cp in local:
            cp.wait()


def _split_refs(refs, counts):
    out = []
    for n in counts:
        out.append(refs[:n])
        refs = refs[n:]
    return out


def _each_exchange(exchanges, method, x_in, x_out, x_sem):
    for ex in exchanges:
        ni, no, ns = len(ex.inputs), len(ex.out_shape), len(ex.sems)
        getattr(ex, method)(x_in[:ni], x_out[:no], x_sem[:ns])
        x_in, x_out, x_sem = x_in[ni:], x_out[no:], x_sem[ns:]


def _call(body, *, name, grid, in_specs, out_specs, out_shape, operands, semantics, scratch_shapes=(), exchanges=()):
    x_in = [a for ex in exchanges for a in ex.inputs]
    x_out = [s for ex in exchanges for s in ex.out_shape]
    x_sem = [s for ex in exchanges for s in ex.sems]
    counts = (len(in_specs), len(x_in), len(out_specs), len(x_out), len(scratch_shapes), len(x_sem))
    aliases, i0, o0 = {}, len(in_specs), len(out_specs)
    for ex in exchanges:
        aliases.update({i0 + i: o0 + o for i, o in ex.aliases.items()})
        i0, o0 = i0 + len(ex.inputs), o0 + len(ex.out_shape)

    def full_body(*refs):
        ins, xi, outs, xo, scr, xs = _split_refs(list(refs), counts)
        if exchanges:
            @pl.when(functools.reduce(jnp.logical_and, [pl.program_id(a) == 0 for a in range(len(grid))]))
            def _():
                _each_exchange(exchanges, "start", xi, xo, xs)

        body(*ins, *outs, *scr)
        if exchanges:
            @pl.when(functools.reduce(jnp.logical_and, [pl.program_id(a) == grid[a] - 1 for a in range(len(grid))]))
            def _():
                _each_exchange(exchanges, "finish", xi, xo, xs)

    return pl.pallas_call(
        full_body, name=name, grid=grid,
        in_specs=list(in_specs) + [HBM] * len(x_in), out_specs=list(out_specs) + [HBM] * len(x_out),
        out_shape=list(out_shape) + x_out, scratch_shapes=list(scratch_shapes) + x_sem,
        input_output_aliases=aliases, compiler_params=_params(semantics),
    )(*operands, *x_in)


def _run_exchanges(exchanges, name):
    x_in = [a for ex in exchanges for a in ex.inputs]
    x_out = [s for ex in exchanges for s in ex.out_shape]
    x_sem = [s for ex in exchanges for s in ex.sems]
    aliases, i0, o0 = {}, 0, 0
    for ex in exchanges:
        aliases.update({i0 + i: o0 + o for i, o in ex.aliases.items()})
        i0, o0 = i0 + len(ex.inputs), o0 + len(ex.out_shape)

    def body(*refs):
        xi, xo, xs = _split_refs(list(refs), (len(x_in), len(x_out), len(x_sem)))
        _each_exchange(exchanges, "start", xi, xo, xs)
        _each_exchange(exchanges, "finish", xi, xo, xs)

    return pl.pallas_call(
        body, name=name, in_specs=[HBM] * len(x_in), out_specs=[HBM] * len(x_out), out_shape=x_out,
        scratch_shapes=x_sem, input_output_aliases=aliases,
    )(*x_in)


def _inproj_fwd(x, g1, w_in_all, exchanges=()):
    L, D = x.shape
    ns, _, nc = w_in_all.shape
    tm = _tile(L, TM_PROJ)

    def body(x_ref, g_ref, w_ref, hn_ref, proj_ref, u_ref):
        hn = _rms_fwd(x_ref[...], g_ref[...]).astype(MXU_DTYPE)
        hn_ref[...] = hn
        for j in range(ns):
            proj_ref[:, j * nc:(j + 1) * nc] = _dot(hn, w_ref[j])
        _store_slabs(u_ref, proj_ref[:, 0:nc])

    return _call(
        body, name="inproj_fwd", grid=(L // tm,), exchanges=exchanges, semantics=("arbitrary",), operands=(x, g1, w_in_all),
        in_specs=[pl.BlockSpec((tm, D), lambda i: (i, 0)), _resident((1, D)), _resident(w_in_all.shape)],
        out_specs=[pl.BlockSpec((tm, D), lambda i: (i, 0)), pl.BlockSpec((tm, ns * nc), lambda i: (i, 0)), _slab_spec(nc, tm)],
        out_shape=[jax.ShapeDtypeStruct((L, D), MXU_DTYPE), jax.ShapeDtypeStruct((L, ns * nc), F32), _slab_shape(L, nc)],
    )


def _slab_shape(L, n):
    return jax.ShapeDtypeStruct((n // LANES, L, LANES), F32)


def _slab_spec(n, tm, index=lambda i: (0, i, 0)):
    return pl.BlockSpec((n // LANES, tm, LANES), index)


def _store_slabs(ref, value):
    for k in range(ref.shape[0]):
        ref[k] = value[:, k * LANES:(k + 1) * LANES]


def _load_slabs(ref):
    return jnp.concatenate([ref[k] for k in range(ref.shape[0])], axis=1)


SEG_ROWS = SUBLANES * SUBLANES


def _load_permuted(ref):
    tm = ref.shape[1]
    slabs = []
    for k in range(ref.shape[0]):
        tiles = [ref.at[k][pl.ds(b * SEG_ROWS + j, SUBLANES, stride=SUBLANES), :] for b in range(tm // SEG_ROWS) for j in range(SUBLANES)]
        slabs.append(jnp.concatenate(tiles, axis=0))
    return jnp.concatenate(slabs, axis=1)


def _store_permuted(ref, value):
    tm = ref.shape[1]
    for k in range(ref.shape[0]):
        for b in range(tm // SEG_ROWS):
            for j in range(SUBLANES):
                r = b * SEG_ROWS + j * SUBLANES
                ref.at[k][pl.ds(b * SEG_ROWS + j, SUBLANES, stride=SUBLANES), :] = value[r:r + SUBLANES, k * LANES:(k + 1) * LANES]


def _scan_tile(xr, xi, hr, hi, coef_ref, lanes, reverse):
    for k, j in ((1, 0), (2, 2), (4, 4)):
        ar = coef_ref[j, :, lanes]
        ai = coef_ref[j + 1, :, lanes]
        shift = SUBLANES - k if reverse else k
        sr = pltpu.roll(xr, shift, 0)
        si = pltpu.roll(xi, shift, 0)
        xr, xi = xr + (ar * sr - ai * si), xi + (ar * si + ai * sr)
    pr = coef_ref[6, :, lanes]
    pi = coef_ref[7, :, lanes]
    return xr + (pr * hr - pi * hi), xi + (pr * hi + pi * hr)


def _scan_block(read, write, hr, hi, coef_ref, lanes, reverse):
    order = list(range(SUBLANES - 1, -1, -1) if reverse else range(SUBLANES))
    near = 8 + 2 * order[0]
    ar = coef_ref[near, :, lanes]
    ai = coef_ref[near + 1, :, lanes]
    xr, xi = read(order[0])
    local = {order[0]: (xr, xi)}
    for j in order[1:]:
        br, bi = read(j)
        xr, xi = br + (ar * xr - ai * xi), bi + (ar * xi + ai * xr)
        local[j] = (xr, xi)
    er, ei = _scan_tile(xr, xi, hr, hi, coef_ref, lanes, reverse)
    edge = lax.broadcasted_iota(jnp.int32, er.shape, 0) == (SUBLANES - 1 if reverse else 0)
    shift = SUBLANES - 1 if reverse else 1
    pr = jnp.where(edge, hr, pltpu.roll(er, shift, 0))
    pi = jnp.where(edge, hi, pltpu.roll(ei, shift, 0))
    for j in range(SUBLANES):
        cr = coef_ref[8 + 2 * j, :, lanes]
        ci = coef_ref[9 + 2 * j, :, lanes]
        xr, xi = local[j]
        write(j, xr + (cr * pr - ci * pi), xi + (cr * pi + ci * pr))
    end = 0 if reverse else SUBLANES - 1
    return jnp.broadcast_to(er[end:end + 1, :], er.shape), jnp.broadcast_to(ei[end:end + 1, :], ei.shape)


def _s5_fwd(u4, bmat, cmat, coef, dskip, exchanges=()):
    L = u4.shape[1]
    tm = _tile(L, TM_S5)
    lc = min(LANE_CHUNK, WB)

    def body(u_ref, bm_ref, cm_ref, coef_ref, d_ref, sre_ref, sim_ref, ys_ref, hr_ref, hi_ref):
        @pl.when(pl.program_id(1) == 0)
        def _():
            hr_ref[...] = jnp.zeros_like(hr_ref)
            hi_ref[...] = jnp.zeros_like(hi_ref)

        u = _load_permuted(u_ref)
        bu = _dot(u.astype(MXU_DTYPE), bm_ref[0])
        sre_ref[...] = bu[:, :WB]
        sim_ref[...] = bu[:, WB:]
        for c in range(WB // lc):
            lanes = slice(c * lc, (c + 1) * lc)
            hr, hi = hr_ref[:, lanes], hi_ref[:, lanes]
            for b in range(tm // SEG_ROWS):
                rows = lambda j, b=b: slice(b * SEG_ROWS + j * SUBLANES, b * SEG_ROWS + (j + 1) * SUBLANES)

                def read(j, rows=rows, lanes=lanes):
                    return sre_ref[rows(j), lanes], sim_ref[rows(j), lanes]

                def write(j, xr, xi, rows=rows, lanes=lanes):
                    sre_ref[rows(j), lanes] = xr
                    sim_ref[rows(j), lanes] = xi

                hr, hi = _scan_block(read, write, hr, hi, coef_ref, lanes, False)
            hr_ref[:, lanes] = hr
            hi_ref[:, lanes] = hi
        ys = _dot_nt(sre_ref[...].astype(MXU_DTYPE), cm_ref[0, :, :WB]) + _dot_nt(sim_ref[...].astype(MXU_DTYPE), cm_ref[0, :, WB:])
        _store_permuted(ys_ref, ys + d_ref[0] * u)

    return _call(
        body, name="s5_fwd", grid=(N_GBLK, L // tm), exchanges=exchanges, semantics=("arbitrary", "arbitrary"),
        operands=(u4, bmat, cmat, coef, dskip),
        in_specs=[
            _slab_spec(UB, tm, lambda b, i: (b, i, 0)),
            pl.BlockSpec((1, UB, 2 * WB), lambda b, i: (b, 0, 0)),
            pl.BlockSpec((1, UB, 2 * WB), lambda b, i: (b, 0, 0)),
            pl.BlockSpec((N_TABLES, SUBLANES, WB), lambda b, i: (0, 0, b)),
            pl.BlockSpec((1, 1, UB), lambda b, i: (b, 0, 0)),
        ],
        out_specs=[
            pl.BlockSpec((tm, WB), lambda b, i: (i, b)),
            pl.BlockSpec((tm, WB), lambda b, i: (i, b)),
            _slab_spec(UB, tm, lambda b, i: (b, i, 0)),
        ],
        out_shape=[
            jax.ShapeDtypeStruct((L, N_GBLK * WB), F32),
            jax.ShapeDtypeStruct((L, N_GBLK * WB), F32),
            _slab_shape(L, D_SSM),
        ],
        scratch_shapes=[pltpu.VMEM((SUBLANES, WB), F32), pltpu.VMEM((SUBLANES, WB), F32)],
    )


def _tail_fwd(x, ys, proj, w_glu, conv_w, g_ssm, g_conv, w_out, g_post, exchanges=()):
    L, D = x.shape
    tm = _tile(L, TM_TAIL)

    def body(x_ref, ys_ref, h_ref, bg_ref, cg_ref, wglu_ref, cw_ref, gs_ref, gc_ref, wout_ref, gp_ref,
             ycat_ref, o_ref, x1_ref, zbuf):
        @pl.when(pl.program_id(0) == 0)
        def _():
            zbuf[0:SUBLANES, :] = jnp.zeros((SUBLANES, D_CONV), F32)

        y1, _ = _gelu(_load_slabs(ys_ref))
        y2 = y1 * jax.nn.sigmoid(_dot(y1.astype(MXU_DTYPE), wglu_ref[...]))
        ycat_ref[:, :D_SSM] = _rms_fwd(y2, gs_ref[...]).astype(MXU_DTYPE)
        z = cg_ref[...] * h_ref[...]
        zbuf[SUBLANES:, :] = z
        conv = cw_ref[0:1, :] * zbuf[SUBLANES - 2:SUBLANES - 2 + tm, :] + cw_ref[1:2, :] * zbuf[SUBLANES - 1:SUBLANES - 1 + tm, :] + cw_ref[2:3, :] * z
        zbuf[0:SUBLANES, :] = zbuf[tm:tm + SUBLANES, :]
        ycat_ref[:, D_SSM:] = _rms_fwd(bg_ref[...] * conv, gc_ref[...]).astype(MXU_DTYPE)
        o = _dot(ycat_ref[...], wout_ref[...])
        o_ref[...] = o
        x1_ref[...] = x_ref[...] + _rms_fwd(o, gp_ref[...])

    row = lambda i: (i, 0)
    return _call(
        body, name="tail_fwd", grid=(L // tm,), exchanges=exchanges, semantics=("arbitrary",),
        operands=(x, ys, proj, proj, proj, w_glu, conv_w, g_ssm, g_conv, w_out, g_post),
        in_specs=[
            pl.BlockSpec((tm, D), row), _slab_spec(D_SSM, tm),
            pl.BlockSpec((tm, D_CONV), lambda i: (i, 1)), pl.BlockSpec((tm, D_CONV), lambda i: (i, 2)),
            pl.BlockSpec((tm, D_CONV), lambda i: (i, 3)),
            _resident(w_glu.shape), _resident(conv_w.shape), _resident(g_ssm.shape), _resident(g_conv.shape),
            _resident(w_out.shape), _resident(g_post.shape),
        ],
        out_specs=[pl.BlockSpec((tm, D), row), pl.BlockSpec((tm, D), row), pl.BlockSpec((tm, D), row)],
        out_shape=[jax.ShapeDtypeStruct((L, D), MXU_DTYPE), jax.ShapeDtypeStruct((L, D), F32), jax.ShapeDtypeStruct((L, D), F32)],
        scratch_shapes=[pltpu.VMEM((tm + SUBLANES, D_CONV), F32)],
    )


def _mlp_fwd(x1, target, w_up_all, w_down_a, w_down_b, g_pre, g_post):
    L, D = x1.shape
    ns, _, fc = w_up_all.shape
    half = w_down_a.shape[1]
    tm = _tile(L, TM_MLP)

    def body(x1_ref, t_ref, wup_ref, wda_ref, wdb_ref, gpre_ref, gpost_ref, hn2_ref, up_ref, act_ref, m_ref, dx2_ref, loss_ref):
        @pl.when(pl.program_id(0) == 0)
        def _():
            loss_ref[...] = jnp.zeros_like(loss_ref)

        x1v = x1_ref[...]
        hn2 = _rms_fwd(x1v, gpre_ref[...]).astype(MXU_DTYPE)
        hn2_ref[...] = hn2
        m = jnp.zeros((tm, D), F32)
        for j in range(ns):
            up = _dot(hn2, wup_ref[j])
            up_ref[:, j * fc:(j + 1) * fc] = up.astype(MXU_DTYPE)
            act = jnp.square(jnp.maximum(up, 0.0)).astype(MXU_DTYPE)
            act_ref[:, j * fc:(j + 1) * fc] = act
            m = m + _dot(act[:, :half], wda_ref[j]) + _dot(act[:, half:], wdb_ref[j])
        m_ref[...] = m
        err = x1v + _rms_fwd(m, gpost_ref[...]) - t_ref[...]
        loss_ref[...] += 0.5 * jnp.sum(jnp.mean(err * err, axis=-1, keepdims=True))
        dx2_ref[...] = err * (1.0 / D)

    row = lambda i: (i, 0)
    return pl.pallas_call(
        body, name="mlp_fwd", grid=(L // tm,),
        in_specs=[pl.BlockSpec((tm, D), row), pl.BlockSpec((tm, D), row), _resident(w_up_all.shape), _resident(w_down_a.shape),
                  _resident(w_down_b.shape), _resident(g_pre.shape), _resident(g_post.shape)],
        out_specs=[pl.BlockSpec((tm, D), row), pl.BlockSpec((tm, ns * fc), row), pl.BlockSpec((tm, ns * fc), row), pl.BlockSpec((tm, D), row),
                   pl.BlockSpec((tm, D), row), pl.BlockSpec((SUBLANES, LANES), lambda i: (0, 0))],
        out_shape=[jax.ShapeDtypeStruct((L, D), MXU_DTYPE), jax.ShapeDtypeStruct((L, ns * fc), MXU_DTYPE), jax.ShapeDtypeStruct((L, ns * fc), MXU_DTYPE),
                   jax.ShapeDtypeStruct((L, D), F32), jax.ShapeDtypeStruct((L, D), F32), jax.ShapeDtypeStruct((SUBLANES, LANES), F32)],
        compiler_params=_params(("arbitrary",)),
    )(x1, target, w_up_all, w_down_a, w_down_b, g_pre, g_post)


def _mlp_bwd(dx2, m, up, x1, w_up_all, w_down_a, w_down_b, g_pre, g_post):
    L, D = x1.shape
    ns, _, fc = w_up_all.shape
    tm = _tile(L, TM_MLP)

    def body(dx2_ref, m_ref, up_ref, x1_ref, wup_ref, wda_ref, wdb_ref, gpre_ref, gpost_ref,
             dm_ref, dup_ref, dx1_ref, dgpost_ref, dgpre_ref):
        @pl.when(pl.program_id(0) == 0)
        def _():
            dgpost_ref[...] = jnp.zeros_like(dgpost_ref)
            dgpre_ref[...] = jnp.zeros_like(dgpre_ref)

        dx2v = dx2_ref[...]
        dm, dg = _rms_bwd(m_ref[...], gpost_ref[...], dx2v)
        dgpost_ref[...] += dg
        dm_b = dm.astype(MXU_DTYPE)
        dm_ref[...] = dm_b
        dhn2 = jnp.zeros((tm, D), F32)
        for j in range(ns):
            cols = slice(j * fc, (j + 1) * fc)
            relu = jnp.maximum(up_ref[:, cols].astype(F32), 0.0)
            dact = jnp.concatenate([_dot_nt(dm_b, wda_ref[j]), _dot_nt(dm_b, wdb_ref[j])], axis=1)
            dup = (dact * (2.0 * relu)).astype(MXU_DTYPE)
            dup_ref[:, cols] = dup
            dhn2 = dhn2 + _dot_nt(dup, wup_ref[j])
        dx, dg = _rms_bwd(x1_ref[...], gpre_ref[...], dhn2)
        dgpre_ref[...] += dg
        dx1_ref[...] = dx2v + dx

    row = lambda i: (i, 0)
    vec = pl.BlockSpec((1, D), lambda i: (0, 0))
    return pl.pallas_call(
        body, name="mlp_bwd", grid=(L // tm,),
        in_specs=[pl.BlockSpec((tm, D), row), pl.BlockSpec((tm, D), row), pl.BlockSpec((tm, ns * fc), row), pl.BlockSpec((tm, D), row),
                  _resident(w_up_all.shape), _resident(w_down_a.shape), _resident(w_down_b.shape), _resident(g_pre.shape), _resident(g_post.shape)],
        out_specs=[pl.BlockSpec((tm, D), row), pl.BlockSpec((tm, ns * fc), row), pl.BlockSpec((tm, D), row), vec, vec],
        out_shape=[jax.ShapeDtypeStruct((L, D), MXU_DTYPE), jax.ShapeDtypeStruct((L, ns * fc), MXU_DTYPE), jax.ShapeDtypeStruct((L, D), F32),
                   jax.ShapeDtypeStruct((1, D), F32), jax.ShapeDtypeStruct((1, D), F32)],
        compiler_params=_params(("arbitrary",)),
    )(dx2, m, up, x1, w_up_all, w_down_a, w_down_b, g_pre, g_post)


def _tail_bwd(dx1, o, ys, proj, w_glu, conv_w, g_ssm, g_conv, w_out, g_post, exchanges=()):
    L, D = dx1.shape
    tm = _tile(L, TM_TAIL)
    nt = L // tm
    hb = tm // SUBLANES

    def body(dx1_ref, o_ref, ys_ref, h_ref, bg_ref, cg_ref, hh_ref, hcg_ref, wglu_ref, cw_ref, gs_ref, gc_ref, wout_ref, gp_ref,
             do_ref, da_ref, y1_ref, dys_ref, dhbc_ref, dgp_ref, dgs_ref, dgc_ref, dcw_ref, zbuf, dcbuf):
        step = pl.program_id(0)

        @pl.when(step == 0)
        def _():
            dcbuf[tm:, :] = jnp.zeros((SUBLANES, D_CONV), F32)
            dgp_ref[...] = jnp.zeros_like(dgp_ref)
            dgs_ref[...] = jnp.zeros_like(dgs_ref)
            dgc_ref[...] = jnp.zeros_like(dgc_ref)
            dcw_ref[...] = jnp.zeros_like(dcw_ref)

        do, dg = _rms_bwd(o_ref[...], gp_ref[...], dx1_ref[...])
        dgp_ref[...] += dg
        do_b = do.astype(MXU_DTYPE)
        do_ref[...] = do_b
        dycat = _dot_nt(do_b, wout_ref[...])
        y1, dgelu = _gelu(_load_slabs(ys_ref))
        y1_b = y1.astype(MXU_DTYPE)
        y1_ref[...] = y1_b
        s = jax.nn.sigmoid(_dot(y1_b, wglu_ref[...]))
        dy2, dg = _rms_bwd(y1 * s, gs_ref[...], dycat[:, :D_SSM])
        dgs_ref[...] += dg
        da_b = (dy2 * y1 * s * (1.0 - s)).astype(MXU_DTYPE)
        da_ref[...] = da_b
        _store_slabs(dys_ref, (dy2 * s + _dot_nt(da_b, wglu_ref[...])) * dgelu)
        h = h_ref[...]
        cg = cg_ref[...]
        bg = bg_ref[...]
        z = cg * h
        first = step == nt - 1
        zbuf[0:SUBLANES, :] = jnp.where(first, 0.0, hcg_ref[...] * hh_ref[...])
        zbuf[SUBLANES:, :] = z
        z1 = zbuf[SUBLANES - 1:SUBLANES - 1 + tm, :]
        z2 = zbuf[SUBLANES - 2:SUBLANES - 2 + tm, :]
        conv = cw_ref[0:1, :] * z2 + cw_ref[1:2, :] * z1 + cw_ref[2:3, :] * z
        dyc, dg = _rms_bwd(bg * conv, gc_ref[...], dycat[:, D_SSM:])
        dgc_ref[...] += dg
        dconv = dyc * bg
        dcw_ref[0:1, :] += jnp.sum(dconv * z2, axis=0, keepdims=True)
        dcw_ref[1:2, :] += jnp.sum(dconv * z1, axis=0, keepdims=True)
        dcw_ref[2:3, :] += jnp.sum(dconv * z, axis=0, keepdims=True)
        dcbuf[0:tm, :] = dconv
        dz = cw_ref[2:3, :] * dconv + cw_ref[1:2, :] * dcbuf[1:1 + tm, :] + cw_ref[0:1, :] * dcbuf[2:2 + tm, :]
        dcbuf[tm:, :] = dcbuf[0:SUBLANES, :]
        dhbc_ref[:, 0:D_CONV] = (dz * cg).astype(MXU_DTYPE)
        dhbc_ref[:, D_CONV:2 * D_CONV] = (dyc * conv).astype(MXU_DTYPE)
        dhbc_ref[:, 2 * D_CONV:] = (dz * h).astype(MXU_DTYPE)

    rev = lambda i: (nt - 1 - i, 0)
    slab = lambda i: (0, nt - 1 - i, 0)
    col = lambda c: (lambda i: (nt - 1 - i, c))
    halo = lambda c: (lambda i: (jnp.maximum((nt - 1 - i) * hb - 1, 0), c))
    vec = lambda n: pl.BlockSpec((1, n), lambda i: (0, 0))
    return _call(
        body, name="tail_bwd", grid=(nt,), exchanges=exchanges, semantics=("arbitrary",),
        operands=(dx1, o, ys, proj, proj, proj, proj, proj, w_glu, conv_w, g_ssm, g_conv, w_out, g_post),
        in_specs=[
            pl.BlockSpec((tm, D), rev), pl.BlockSpec((tm, D), rev), _slab_spec(D_SSM, tm, slab),
            pl.BlockSpec((tm, D_CONV), col(1)), pl.BlockSpec((tm, D_CONV), col(2)), pl.BlockSpec((tm, D_CONV), col(3)),
            pl.BlockSpec((SUBLANES, D_CONV), halo(1)), pl.BlockSpec((SUBLANES, D_CONV), halo(3)),
            _resident(w_glu.shape), _resident(conv_w.shape), _resident(g_ssm.shape), _resident(g_conv.shape),
            _resident(w_out.shape), _resident(g_post.shape),
        ],
        out_specs=[
            pl.BlockSpec((tm, D), rev), pl.BlockSpec((tm, D_SSM), rev), pl.BlockSpec((tm, D_SSM), rev), _slab_spec(D_SSM, tm, slab),
            pl.BlockSpec((tm, 3 * D_CONV), rev), vec(D), vec(D_SSM), vec(D_CONV),
            pl.BlockSpec((SUBLANES, D_CONV), lambda i: (0, 0)),
        ],
        out_shape=[
            jax.ShapeDtypeStruct((L, D), MXU_DTYPE), jax.ShapeDtypeStruct((L, D_SSM), MXU_DTYPE), jax.ShapeDtypeStruct((L, D_SSM), MXU_DTYPE),
            _slab_shape(L, D_SSM), jax.ShapeDtypeStruct((L, 3 * D_CONV), MXU_DTYPE),
            jax.ShapeDtypeStruct((1, D), F32), jax.ShapeDtypeStruct((1, D_SSM), F32), jax.ShapeDtypeStruct((1, D_CONV), F32),
            jax.ShapeDtypeStruct((SUBLANES, D_CONV), F32),
        ],
        scratch_shapes=[pltpu.VMEM((tm + SUBLANES, D_CONV), F32), pltpu.VMEM((tm + SUBLANES, D_CONV), F32)],
    )


def _s5_bwd(dys, u4, s_re, s_im, bmat, cmat, coef_rev, dskip, exchanges=()):
    L = dys.shape[1]
    tm = _tile(L, TM_S5)
    nt = L // tm
    lc = min(LANE_CHUNK, WB)

    def body(dys_ref, u_ref, sre_ref, sim_ref, bm_ref, cm_ref, coef_ref, d_ref,
             du_ref, gb_ref, gc_ref, q_ref, gd_ref, dr_ref, di_ref, lr_ref, li_ref, hr_ref, hi_ref, qr_acc, qi_acc, gb_acc, gc_acc):
        step = pl.program_id(1)

        @pl.when(step == 0)
        def _():
            for ref in (hr_ref, hi_ref, qr_acc, qi_acc, gb_acc, gc_acc, gd_ref):
                ref[...] = jnp.zeros_like(ref)

        dys_v = _load_permuted(dys_ref)
        u = _load_permuted(u_ref)
        dys_b = dys_v.astype(MXU_DTYPE)
        u_b = u.astype(MXU_DTYPE)
        d = _dot(dys_b, cm_ref[0])
        dr_ref[...] = d[:, :WB]
        di_ref[...] = d[:, WB:]
        for c in range(WB // lc):
            lanes = slice(c * lc, (c + 1) * lc)
            hr, hi = hr_ref[:, lanes], hi_ref[:, lanes]
            q = [qr_acc[:, lanes], qi_acc[:, lanes]]
            for b in range(tm // SEG_ROWS - 1, -1, -1):
                rows = lambda j, b=b: slice(b * SEG_ROWS + j * SUBLANES, b * SEG_ROWS + (j + 1) * SUBLANES)

                def read(j, rows=rows, lanes=lanes):
                    return dr_ref[rows(j), lanes], di_ref[rows(j), lanes]

                def write(j, xr, xi, rows=rows, lanes=lanes, q=q):
                    lr_ref[rows(j), lanes] = xr
                    li_ref[rows(j), lanes] = xi
                    sr = sre_ref[rows(j), lanes]
                    si = sim_ref[rows(j), lanes]
                    q[0] = q[0] + (xr * sr + xi * si)
                    q[1] = q[1] + (xi * sr - xr * si)

                hr, hi = _scan_block(read, write, hr, hi, coef_ref, lanes, True)
            hr_ref[:, lanes] = hr
            hi_ref[:, lanes] = hi
            qr_acc[:, lanes] = q[0]
            qi_acc[:, lanes] = q[1]
        lr_b = lr_ref[...].astype(MXU_DTYPE)
        li_b = li_ref[...].astype(MXU_DTYPE)
        _store_permuted(du_ref, _dot_nt(lr_b, bm_ref[0, :, :WB]) + _dot_nt(li_b, bm_ref[0, :, WB:]) + d_ref[0] * dys_v)
        sre_b = sre_ref[...].astype(MXU_DTYPE)
        sim_b = sim_ref[...].astype(MXU_DTYPE)
        for t in range(WB // DIAG_COLS):
            ch = slice(t * DIAG_ROWS, (t + 1) * DIAG_ROWS)
            cols = slice(t * DIAG_COLS, (t + 1) * DIAG_COLS)
            icols = slice(WB + t * DIAG_COLS, WB + (t + 1) * DIAG_COLS)
            gb_acc[ch, cols] += _dot_tn(u_b[:, ch], lr_b[:, cols])
            gb_acc[ch, icols] += _dot_tn(u_b[:, ch], li_b[:, cols])
            gc_acc[ch, cols] += _dot_tn(dys_b[:, ch], sre_b[:, cols])
            gc_acc[ch, icols] += _dot_tn(dys_b[:, ch], sim_b[:, cols])
        gd_ref[0] += jnp.sum(dys_v * u, axis=0, keepdims=True)

        @pl.when(step == nt - 1)
        def _():
            q_ref[0, 0:1, :] = jnp.sum(qr_acc[...], axis=0, keepdims=True)
            q_ref[0, 1:2, :] = jnp.sum(qi_acc[...], axis=0, keepdims=True)
            mask = _group_mask(DIAG_ROWS, DIAG_COLS)
            fold = (lax.broadcasted_iota(jnp.int32, (DIAG_COLS, STATE), 0) % STATE == lax.broadcasted_iota(jnp.int32, (DIAG_COLS, STATE), 1)).astype(F32)
            for acc, out in ((gb_acc, gb_ref), (gc_acc, gc_ref)):
                for k in range(2):
                    for t in range(WB // DIAG_COLS):
                        ch = slice(t * DIAG_ROWS, (t + 1) * DIAG_ROWS)
                        own = jnp.where(mask, acc[ch, k * WB + t * DIAG_COLS:k * WB + (t + 1) * DIAG_COLS], 0.0)
                        out[0, k, ch, :] = jnp.dot(own, fold, precision=lax.Precision.HIGHEST, preferred_element_type=F32)

    rev = lambda b, i: (nt - 1 - i, b)
    slab = lambda b, i: (b, nt - 1 - i, 0)
    blk = lambda b, i: (b, 0, 0)
    blk4 = lambda b, i: (b, 0, 0, 0)
    return _call(
        body, name="s5_bwd", grid=(N_GBLK, nt), exchanges=exchanges, semantics=("arbitrary", "arbitrary"),
        operands=(dys, u4, s_re, s_im, bmat, cmat, coef_rev, dskip),
        in_specs=[
            _slab_spec(UB, tm, slab), _slab_spec(UB, tm, slab), pl.BlockSpec((tm, WB), rev), pl.BlockSpec((tm, WB), rev),
            pl.BlockSpec((1, UB, 2 * WB), blk), pl.BlockSpec((1, UB, 2 * WB), blk),
            pl.BlockSpec((N_TABLES, SUBLANES, WB), lambda b, i: (0, 0, b)), pl.BlockSpec((1, 1, UB), blk),
        ],
        out_specs=[
            _slab_spec(UB, tm, slab), pl.BlockSpec((1, 2, UB, STATE), blk4), pl.BlockSpec((1, 2, UB, STATE), blk4),
            pl.BlockSpec((1, 2, WB), blk), pl.BlockSpec((1, 1, UB), blk),
        ],
        out_shape=[
            _slab_shape(L, D_SSM), jax.ShapeDtypeStruct((N_GBLK, 2, UB, STATE), F32),
            jax.ShapeDtypeStruct((N_GBLK, 2, UB, STATE), F32), jax.ShapeDtypeStruct((N_GBLK, 2, WB), F32),
            jax.ShapeDtypeStruct((N_GBLK, 1, UB), F32),
        ],
        scratch_shapes=[pltpu.VMEM((tm, WB), F32)] * 4 + [pltpu.VMEM((SUBLANES, WB), F32)] * 4 + [pltpu.VMEM((UB, 2 * WB), F32)] * 2,
    )


def _inproj_bwd(du, dhbc, x, dx1, w_in_all, g1):
    L, D = x.shape
    ns, _, nc = w_in_all.shape
    tm = _tile(L, TM_PROJ)

    def body(du_ref, dhbc_ref, x_ref, dx1_ref, w_ref, g_ref, gx_ref, dproj_ref, dg_ref):
        @pl.when(pl.program_id(0) == 0)
        def _():
            dg_ref[...] = jnp.zeros_like(dg_ref)

        du_b = _load_slabs(du_ref).astype(MXU_DTYPE)
        dproj_ref[:, :nc] = du_b
        dproj_ref[:, nc:] = dhbc_ref[...]
        dhn = _dot_nt(du_b, w_ref[0])
        for j in range(1, ns):
            dhn = dhn + _dot_nt(dhbc_ref[:, (j - 1) * nc:j * nc], w_ref[j])
        dx, dg = _rms_bwd(x_ref[...], g_ref[...], dhn)
        dg_ref[...] += dg
        gx_ref[...] = dx1_ref[...] + dx

    row = lambda i: (i, 0)
    return pl.pallas_call(
        body, name="inproj_bwd", grid=(L // tm,),
        in_specs=[_slab_spec(nc, tm), pl.BlockSpec((tm, (ns - 1) * nc), row), pl.BlockSpec((tm, D), row), pl.BlockSpec((tm, D), row),
                  _resident(w_in_all.shape), _resident(g1.shape)],
        out_specs=[pl.BlockSpec((tm, D), row), pl.BlockSpec((tm, ns * nc), row), pl.BlockSpec((1, D), lambda i: (0, 0))],
        out_shape=[jax.ShapeDtypeStruct((L, D), F32), jax.ShapeDtypeStruct((L, ns * nc), MXU_DTYPE), jax.ShapeDtypeStruct((1, D), F32)],
        compiler_params=_params(("arbitrary",)),
    )(du, dhbc, x, dx1, w_in_all, g1)


def _matmul_tn(a, b, name, col_shards=1, exchanges=()):
    L, K = a.shape
    N = b.shape[1]
    tl = _tile(L, TL_TN)
    tk = _tile(K, TK_TN)
    nw = N // col_shards
    spb = max(1, min(col_shards, TN_TN // nw))
    tn = spb * nw if spb > 1 else _tile(nw, TK_TN)
    npb = nw // tn if spb == 1 else 1

    def body(a_ref, b_ref, o_ref):
        @pl.when(pl.program_id(2) == 0)
        def _():
            o_ref[...] = jnp.zeros_like(o_ref)

        res = _dot_tn(a_ref[...], b_ref[...])
        for s in range(spb):
            o_ref[s] += res[:, s * nw:(s + 1) * nw] if spb > 1 else res

    out_block = pl.BlockSpec((spb, tk, nw if spb > 1 else tn), (lambda k, n, l: (n, k, 0)) if spb > 1 else (lambda k, n, l: (n // npb, k, n % npb)))
    return _call(
        body, name=name, grid=(K // tk, N // tn, L // tl), exchanges=exchanges, semantics=("arbitrary", "arbitrary", "arbitrary"),
        operands=(a, b),
        in_specs=[pl.BlockSpec((tl, tk), lambda k, n, l: (l, k)), pl.BlockSpec((tl, tn), lambda k, n, l: (l, n))],
        out_specs=[out_block],
        out_shape=[jax.ShapeDtypeStruct((col_shards, K, nw), F32)],
    )


def _ssm_discretize(lam_re, lam_im, log_dt, bt_re, bt_im):
    dt = jnp.exp(log_dt)[:, None]
    zr = lam_re * dt
    zi = lam_im * dt
    mag = jnp.exp(zr)
    abr = mag * jnp.cos(zi)
    abi = mag * jnp.sin(zi)
    nr, ni = abr - 1.0, abi
    den = lam_re * lam_re + lam_im * lam_im
    coef_r = ((nr * lam_re + ni * lam_im) / den)[:, None, :]
    coef_i = ((ni * lam_re - nr * lam_im) / den)[:, None, :]
    return zr, zi, coef_r * bt_re - coef_i * bt_im, coef_r * bt_im + coef_i * bt_re


def _scan_tables(ar, ai, reverse):
    rows = np.arange(SUBLANES)
    exps = np.zeros((N_TABLES // 2, SUBLANES), np.int32)
    keep = np.ones((N_TABLES // 2, SUBLANES), bool)
    for t, k in enumerate((1, 2, 4)):
        exps[t] = SUBLANES * k
        keep[t] = (rows + k <= SUBLANES - 1) if reverse else (rows >= k)
    exps[3] = SUBLANES * (SUBLANES - rows) if reverse else SUBLANES * (rows + 1)
    for j in range(SUBLANES):
        exps[4 + j] = SUBLANES - j if reverse else j + 1
    pr, pi = ar, (-ai if reverse else ai)
    shape = (N_TABLES // 2, SUBLANES, ar.shape[0])
    xr, xi = jnp.ones(shape, F32), jnp.zeros(shape, F32)
    for bit in range(int(exps.max()).bit_length()):
        on = ((exps >> bit) & 1).astype(bool)[:, :, None]
        xr, xi = jnp.where(on, xr * pr - xi * pi, xr), jnp.where(on, xr * pi + xi * pr, xi)
        pr, pi = pr * pr - pi * pi, 2.0 * pr * pi
    xr = jnp.where(keep[:, :, None], xr, 0.0)
    xi = jnp.where(keep[:, :, None], xi, 0.0)
    return jnp.stack([xr, xi], axis=1).reshape(N_TABLES, SUBLANES, ar.shape[0])


def _group_mask(rows, cols):
    r = lax.broadcasted_iota(jnp.int32, (rows, cols), 0) // GROUP
    c = lax.broadcasted_iota(jnp.int32, (rows, cols), 1) // STATE
    return r == c


def _ssm_expand(bt_re, bt_im, c_re, c_im, exchanges=()):
    flat = lambda a: a.reshape(N_GROUPS * GROUP, STATE)

    def body(br_ref, bi_ref, cr_ref, ci_ref, bm_ref, cm_ref):
        spread = (lax.broadcasted_iota(jnp.int32, (STATE, WB), 1) % STATE == lax.broadcasted_iota(jnp.int32, (STATE, WB), 0)).astype(F32)
        mask = _group_mask(UB, WB)

        def expand(x):
            wide = jnp.dot(x, spread, precision=lax.Precision.HIGHEST, preferred_element_type=F32)
            return jnp.where(mask, wide, 0.0).astype(MXU_DTYPE)

        bm_ref[0, :, :WB] = expand(br_ref[...])
        bm_ref[0, :, WB:] = expand(bi_ref[...])
        cm_ref[0, :, :WB] = expand(cr_ref[...])
        cm_ref[0, :, WB:] = expand(-ci_ref[...])

    spec = pl.BlockSpec((UB, STATE), lambda b: (b, 0))
    out = pl.BlockSpec((1, UB, 2 * WB), lambda b: (b, 0, 0))
    return _call(
        body, name="ssm_expand", grid=(N_GBLK,), exchanges=exchanges, semantics=("arbitrary",),
        operands=(flat(bt_re), flat(bt_im), flat(c_re), flat(c_im)), in_specs=[spec] * 4, out_specs=[out, out],
        out_shape=[jax.ShapeDtypeStruct((N_GBLK, UB, 2 * WB), MXU_DTYPE)] * 2,
    )


def _ssm_matrices(lam_re, lam_im, log_dt, bt_re, bt_im, c_re, c_im, exchanges=()):
    zr, zi, bbar_r, bbar_i = _ssm_discretize(lam_re, lam_im, log_dt, bt_re, bt_im)
    mag = jnp.exp(zr)
    ar = (mag * jnp.cos(zi)).reshape(-1)
    ai = (mag * jnp.sin(zi)).reshape(-1)
    bmat, cmat, *rest = _ssm_expand(bbar_r, bbar_i, c_re, c_im, exchanges)
    return bmat, cmat, _scan_tables(ar, ai, False), _scan_tables(ar, ai, True), rest


def _ssm_param_grads(lam_re, lam_im, log_dt, bt_re, bt_im, c_re, c_im, gb, gc, q, gd):
    part = lambda g, k: g[:, k].reshape(N_GROUPS, GROUP, STATE)
    gcr, gci = part(gc, 0), part(gc, 1)
    qr = q[:, 0, :].reshape(N_GROUPS, STATE) - jnp.sum(c_re * gcr - c_im * gci, axis=1)
    qi = q[:, 1, :].reshape(N_GROUPS, STATE) + jnp.sum(c_im * gcr + c_re * gci, axis=1)
    _, vjp = jax.vjp(_ssm_discretize, lam_re, lam_im, log_dt, bt_re, bt_im)
    d_lam_re, d_lam_im, d_log_dt, d_bt_re, d_bt_im = vjp((qr, qi, part(gb, 0), part(gb, 1)))
    return d_lam_re, d_lam_im, d_log_dt, d_bt_re, d_bt_im, part(gc, 0), -part(gc, 1), gd.reshape(N_GROUPS, GROUP)


def _row_tile(rows, n):
    return _tile(rows, max(SUBLANES, (2 * 1024 * 1024) // (4 * n)))


def _pair_add(grad, other, core, name):
    ns, h, n = other.shape
    tr = _row_tile(h, n)
    nb = h // tr

    def body(c_ref, g_ref, o_ref, out_ref):
        out_ref[...] = (g_ref[...] + o_ref[...]).astype(WIRE_DTYPE)

    return pl.pallas_call(
        body, name=name,
        grid_spec=pltpu.PrefetchScalarGridSpec(
            num_scalar_prefetch=1, grid=(ns, nb),
            in_specs=[pl.BlockSpec((1, tr, n), lambda s, i, c: (s, c[0] * nb + i, 0)), pl.BlockSpec((1, tr, n), lambda s, i, c: (s, i, 0))],
            out_specs=pl.BlockSpec((1, tr, n), lambda s, i, c: (s, i, 0))),
        out_shape=jax.ShapeDtypeStruct(other.shape, WIRE_DTYPE),
        compiler_params=_params(("arbitrary", "arbitrary")),
    )(core, grad, other)


def _quad_sum(parts, core, name):
    ns, h, n = parts.shape
    tr = _row_tile(h, n)
    nb = h // tr

    def body(c_ref, p_ref, out_ref):
        p = [p_ref[k].astype(F32) for k in range(ns)]
        out_ref[...] = ((p[0] + p[1]) + p[2]) + p[3]

    return pl.pallas_call(
        body, name=name,
        grid_spec=pltpu.PrefetchScalarGridSpec(
            num_scalar_prefetch=1, grid=(nb,),
            in_specs=[pl.BlockSpec((ns, tr, n), lambda i, c: (0, i, 0))],
            out_specs=pl.BlockSpec((tr, n), lambda i, c: (c[0] * nb + i, 0))),
        out_shape=jax.ShapeDtypeStruct((2 * h, n), F32),
        compiler_params=_params(("arbitrary",)),
    )(core, parts)


def _adamw_math(w, g, m, v):
    m = ADAM_B1 * m + (1.0 - ADAM_B1) * g
    v = ADAM_B2 * v + (1.0 - ADAM_B2) * jnp.square(g)
    m_hat = m / (1.0 - ADAM_B1 ** ADAM_STEP)
    v_hat = v / (1.0 - ADAM_B2 ** ADAM_STEP)
    delta = -ADAM_LR * (m_hat / (jnp.sqrt(v_hat) + ADAM_EPS) + ADAM_WD * w)
    return delta, m, v


def _adamw(w, g, m, v, name):
    r, n = w.shape
    tr = _row_tile(r, n)

    def body(w_ref, g_ref, m_ref, v_ref, d_ref, nm_ref, nv_ref):
        d_ref[...], nm_ref[...], nv_ref[...] = _adamw_math(w_ref[...], g_ref[...], m_ref[...], v_ref[...])

    spec = pl.BlockSpec((tr, n), lambda i: (i, 0))
    return pl.pallas_call(
        body, name=name, grid=(r // tr,), in_specs=[spec] * 4, out_specs=[spec] * 3,
        out_shape=[jax.ShapeDtypeStruct((r, n), F32)] * 3,
        compiler_params=_params(("arbitrary",)),
    )(w, g, m, v)


LANES = 128
SMALL = ["g_pre_mix", "lam_re", "lam_im", "log_dt", "b_re", "b_im", "c_re", "c_im", "d_skip", "conv_w", "g_ssm_out", "g_conv_out",
         "g_post_mix", "g_pre_mlp", "g_post_mlp"]
TILE_SLOTS = {"b_re": (0, N_GROUPS), "b_im": (N_GROUPS, N_GROUPS), "c_re": (2 * N_GROUPS, N_GROUPS), "c_im": (3 * N_GROUPS, N_GROUPS),
              "lam_re": (4 * N_GROUPS, 2), "lam_im": (4 * N_GROUPS + 2, 2)}
N_TILE_SLOTS = 4 * N_GROUPS + 4
VEC_ROWS = {"g_pre_mix": 0, "g_post_mix": 1, "g_pre_mlp": 2, "g_post_mlp": 3, "g_ssm_out": 4, "g_conv_out": 5, "log_dt": 6}
ROW_LOSS, ROW_DSKIP, ROW_CONV, N_PACK_ROWS = 7, 8, 24, 32


def _kernel_form(name, a):
    if name in ("b_re", "b_im"):
        return jnp.transpose(a, (0, 1, 3, 2)).reshape(N_GROUPS, GROUP, STATE)
    if name in ("c_re", "c_im"):
        return a.reshape(N_GROUPS, GROUP, STATE)
    if name in ("lam_re", "lam_im"):
        return a.reshape(2, GROUP, STATE)
    if name == "d_skip":
        return jnp.transpose(a, (0, 2, 1)).reshape(GROUP, N_GROUPS)
    if name == "conv_w":
        return jnp.transpose(a, (1, 0, 2))
    return a


def _param_form(name, k):
    if name in ("b_re", "b_im"):
        return jnp.transpose(k.reshape(1, N_GROUPS, GROUP, STATE), (0, 1, 3, 2))
    if name in ("c_re", "c_im"):
        return k.reshape(1, N_GROUPS, GROUP, STATE)
    if name in ("lam_re", "lam_im"):
        return k.reshape(1, N_GROUPS, STATE)
    if name == "d_skip":
        return jnp.transpose(k.reshape(1, GROUP, N_GROUPS), (0, 2, 1))
    if name == "conv_w":
        return jnp.transpose(k, (1, 0, 2))
    return k


def _pack_tiles(g):
    lam = lambda a: a.reshape(2, GROUP, STATE)
    tiles = jnp.concatenate([g["b_re"], g["b_im"], g["c_re"], g["c_im"], lam(g["lam_re"]), lam(g["lam_im"])], axis=0)
    return tiles.astype(WIRE_DTYPE)


def _pack_rows(g, loss):
    row = lambda a: jnp.pad(a, ((0, 0), (0, D_MODEL - a.shape[1])))
    rows = [row(g[k][None]) for k in VEC_ROWS] + [row(loss[0:1]), row(g["d_skip"].T), row(g["conv_w"])]
    rows.append(jnp.zeros((N_PACK_ROWS - ROW_CONV - 3, D_MODEL), F32))
    return jnp.concatenate(rows, axis=0)


def _adamw_small(tiles, rows, w, m, v):
    nn = len(SMALL)

    def body(*refs):
        t_ref, r_ref = refs[0], refs[1]
        w_refs, m_refs, v_refs = refs[2:2 + nn], refs[2 + nn:2 + 2 * nn], refs[2 + 2 * nn:2 + 3 * nn]
        loss_ref, outs = refs[2 + 3 * nn], refs[3 + 3 * nn:]

        def tile_sum(first, count):
            total = t_ref[0, first:first + count].astype(F32)
            for d in range(1, N_DEV):
                total = total + t_ref[d, first:first + count].astype(F32)
            return total

        def row_sum(first, count, lanes):
            total = r_ref[0, first:first + count, 0:lanes]
            for d in range(1, N_DEV):
                total = total + r_ref[d, first:first + count, 0:lanes]
            return total

        def step(j, g, at=lambda ref: ref):
            delta, nm, nv = _adamw_math(at(w_refs[j])[...], g, at(m_refs[j])[...], at(v_refs[j])[...])
            at(outs[j])[...] = g
            at(outs[nn + j])[...] = delta
            at(outs[2 * nn + j])[...] = nm
            at(outs[3 * nn + j])[...] = nv

        loss_ref[...] = row_sum(ROW_LOSS, 1, LANES)
        chip = 2 * lax.axis_index("x") + lax.axis_index("y")
        for j, name in enumerate(SMALL):
            if name in TILE_SLOTS:
                step(j, tile_sum(*TILE_SLOTS[name]))
            elif name == "d_skip":
                step(j, row_sum(ROW_DSKIP, GROUP, N_GROUPS))
            elif name == "conv_w":
                full = row_sum(ROW_CONV, 3, D_CONV)
                mine = full[:, 0:LANES]
                for s in range(1, N_CHIPS):
                    mine = jnp.where(chip == s, full[:, s * LANES:(s + 1) * LANES], mine)
                for k in range(3):
                    step(j, mine[k:k + 1, :], at=lambda ref, k=k: ref.at[k])
            else:
                step(j, row_sum(VEC_ROWS[name], 1, w_refs[j].shape[1]))

    args = [tiles, rows] + [w[k] for k in SMALL] + [m[k] for k in SMALL] + [v[k] for k in SMALL]
    res = pl.pallas_call(
        body, name="adamw_small", in_specs=[VMEM] * len(args), out_specs=[VMEM] * (1 + 4 * nn),
        out_shape=[jax.ShapeDtypeStruct((1, LANES), F32)] + [jax.ShapeDtypeStruct(w[k].shape, F32) for k in SMALL] * 4,
        compiler_params=pltpu.CompilerParams(vmem_limit_bytes=VMEM_LIMIT),
    )(*args)
    return res[0], [dict(zip(SMALL, res[1 + q * nn:1 + (q + 1) * nn])) for q in range(4)]


WEIGHTS = ["g_pre_mix", "w_in", "lam_re", "lam_im", "log_dt", "b_re", "b_im", "c_re", "c_im", "d_skip", "w_glu", "conv_w",
           "g_ssm_out", "g_conv_out", "w_out", "g_post_mix", "g_pre_mlp", "w_up", "w_down", "g_post_mlp"]
BIG = ["w_in", "w_glu", "w_out", "w_up", "w_down"]


def kernel(x, g_pre_mix, w_in, lam_re, lam_im, log_dt, b_re, b_im, c_re, c_im, d_skip, w_glu, conv_w, g_ssm_out, g_conv_out, w_out, g_post_mix, g_pre_mlp, w_up, w_down, g_post_mlp, loss_target, m_g_pre_mix, m_w_in, m_lam_re, m_lam_im, m_log_dt, m_b_re, m_b_im, m_c_re, m_c_im, m_d_skip, m_w_glu, m_conv_w, m_g_ssm_out, m_g_conv_out, m_w_out, m_g_post_mix, m_g_pre_mlp, m_w_up, m_w_down, m_g_post_mlp, v_g_pre_mix, v_w_in, v_lam_re, v_lam_im, v_log_dt, v_b_re, v_b_im, v_c_re, v_c_im, v_d_skip, v_w_glu, v_conv_w, v_g_ssm_out, v_g_conv_out, v_w_out, v_g_post_mix, v_g_pre_mlp, v_w_up, v_w_down, v_g_post_mlp):
    w = dict(g_pre_mix=g_pre_mix, w_in=w_in, lam_re=lam_re, lam_im=lam_im, log_dt=log_dt, b_re=b_re, b_im=b_im, c_re=c_re, c_im=c_im,
             d_skip=d_skip, w_glu=w_glu, conv_w=conv_w, g_ssm_out=g_ssm_out, g_conv_out=g_conv_out, w_out=w_out, g_post_mix=g_post_mix,
             g_pre_mlp=g_pre_mlp, w_up=w_up, w_down=w_down, g_post_mlp=g_post_mlp)
    m = dict(g_pre_mix=m_g_pre_mix, w_in=m_w_in, lam_re=m_lam_re, lam_im=m_lam_im, log_dt=m_log_dt, b_re=m_b_re, b_im=m_b_im, c_re=m_c_re,
             c_im=m_c_im, d_skip=m_d_skip, w_glu=m_w_glu, conv_w=m_conv_w, g_ssm_out=m_g_ssm_out, g_conv_out=m_g_conv_out, w_out=m_w_out,
             g_post_mix=m_g_post_mix, g_pre_mlp=m_g_pre_mlp, w_up=m_w_up, w_down=m_w_down, g_post_mlp=m_g_post_mlp)
    v = dict(g_pre_mix=v_g_pre_mix, w_in=v_w_in, lam_re=v_lam_re, lam_im=v_lam_im, log_dt=v_log_dt, b_re=v_b_re, b_im=v_b_im, c_re=v_c_re,
             c_im=v_c_im, d_skip=v_d_skip, w_glu=v_w_glu, conv_w=v_conv_w, g_ssm_out=v_g_ssm_out, g_conv_out=v_g_conv_out, w_out=v_w_out,
             g_post_mix=v_g_post_mix, g_pre_mlp=v_g_pre_mlp, w_up=v_w_up, w_down=v_w_down, g_post_mlp=v_g_post_mlp)
    w_dev, m_dev, v_dev = w, m, v
    w, m, v = ({k: a[0] for k, a in d.items()} for d in (w, m, v))
    core = lax.axis_index("c").astype(jnp.int32).reshape(1)

    xs, target = x[0], loss_target[0]
    g1 = w["g_pre_mix"][None]
    g_ssm, g_conv = w["g_ssm_out"][None], w["g_conv_out"][None]
    g_post_mix, g_pre_mlp, g_post_mlp = w["g_post_mix"][None], w["g_pre_mlp"][None], w["g_post_mlp"][None]
    bt_re, bt_im = (jnp.transpose(w[k], (0, 2, 1)) for k in ("b_re", "b_im"))
    dskip = w["d_skip"].reshape(N_GBLK, 1, UB)
    shard = {k: w[k].astype(MXU_DTYPE) for k in BIG}
    conv_pad = jnp.pad(w["conv_w"], ((0, SUBLANES - 3), (0, 0)))

    bmat, cmat, coef_f, coef_r, (w_in_all,) = _ssm_matrices(
        w["lam_re"], w["lam_im"], w["log_dt"], bt_re, bt_im, w["c_re"], w["c_im"], exchanges=[_GatherForward([shard["w_in"]])])
    hn, proj, u4, w_glu_all, w_out_all, conv_all = _inproj_fwd(
        xs, g1, w_in_all, exchanges=[_Gather([shard["w_glu"], shard["w_out"], conv_pad], [False, False, False])])
    wd_half = shard["w_down"].shape[0] // 2
    wda, wdb = shard["w_down"][:wd_half], shard["w_down"][wd_half:]
    s_re, s_im, ys, w_up_all, wda_all = _s5_fwd(u4, bmat, cmat, coef_f, dskip, exchanges=[_Gather([shard["w_up"], wda], [True, True])])
    w_glu_f, w_out_f = w_glu_all.reshape(D_SSM, D_SSM), w_out_all.reshape(D_MODEL, D_MODEL)
    conv_f = jnp.transpose(conv_all, (1, 0, 2)).reshape(SUBLANES, D_CONV)
    ycat, o, x1, w_up_all, wda_all, wdb_all = _tail_fwd(xs, ys, proj, w_glu_f, conv_f, g_ssm, g_conv, w_out_f, g_post_mix,
                                                        exchanges=[_Forward([w_up_all, wda_all]), _GatherForward([wdb])])
    hn2, up, act, m_act, dx2, loss = _mlp_fwd(x1, target, w_up_all, wda_all, wdb_all, g_pre_mlp, g_post_mlp)

    dm, dup, dx1, dg_post_mlp, dg_pre_mlp = _mlp_bwd(dx2, m_act, up, x1, w_up_all, wda_all, wdb_all, g_pre_mlp, g_post_mlp)
    gw_down = _matmul_tn(act, dm, "dw_down")[0].reshape(N_CHIPS, D_FF // N_CHIPS, D_MODEL)
    gw_up = _matmul_tn(hn2, dup, "dw_up", col_shards=N_CHIPS)[0]
    do, da, y1, dys, dhbc, dg_post_mix, dg_ssm, dg_conv, dconv_w, o_down, o_up = _tail_bwd(
        dx1, o, ys, proj, w_glu_f, conv_f, g_ssm, g_conv, w_out_f, g_post_mix, exchanges=[_Pair([gw_down, gw_up])])
    p_down = _pair_add(gw_down, o_down, core, "pair_add_w_down")
    p_up = _pair_add(gw_up, o_up, core, "pair_add_w_up")
    gw_out = _matmul_tn(ycat, do, "dw_out")[0].reshape(N_CHIPS, D_MODEL // N_CHIPS, D_MODEL)
    gw_glu = _matmul_tn(y1, da, "dw_glu")[0].reshape(N_CHIPS, D_SSM // N_CHIPS, D_SSM)
    du, gb, gc, q, gd, q_down, q_up, o_out, o_glu = _s5_bwd(
        dys, u4, s_re, s_im, bmat, cmat, coef_r, dskip, exchanges=[_Chip([p_down, p_up]), _Pair([gw_out, gw_glu])])
    h_down = _quad_sum(q_down, core, "quad_sum_w_down")
    h_up = _quad_sum(q_up, core, "quad_sum_w_up")
    p_out = _pair_add(gw_out, o_out, core, "pair_add_w_out")
    p_glu = _pair_add(gw_glu, o_glu, core, "pair_add_w_glu")
    grad_x, dproj, dg_pre_mix = _inproj_bwd(du, dhbc, xs, dx1, w_in_all, g1)
    d_lam_re, d_lam_im, d_log_dt, d_b_re, d_b_im, d_c_re, d_c_im, d_d_skip = _ssm_param_grads(
        w["lam_re"], w["lam_im"], w["log_dt"], bt_re, bt_im, w["c_re"], w["c_im"], gb, gc, q, gd)
    small = {
        "g_pre_mix": dg_pre_mix[0], "lam_re": d_lam_re, "lam_im": d_lam_im, "log_dt": d_log_dt, "b_re": d_b_re, "b_im": d_b_im,
        "c_re": d_c_re, "c_im": d_c_im, "d_skip": d_d_skip, "conv_w": dconv_w[:3], "g_ssm_out": dg_ssm[0], "g_conv_out": dg_conv[0],
        "g_post_mix": dg_post_mix[0], "g_pre_mlp": dg_pre_mlp[0], "g_post_mlp": dg_post_mlp[0],
    }
    gw_in, g_down, g_up, q_out, q_glu, tiles, rows = _matmul_tn(
        hn, dproj, "dw_in", col_shards=N_CHIPS,
        exchanges=[_Share([h_down, h_up]), _Chip([p_out, p_glu]), _GatherSmall(_pack_tiles(small)), _GatherSmall(_pack_rows(small, loss))])
    h_out = _quad_sum(q_out, core, "quad_sum_w_out")
    h_glu = _quad_sum(q_glu, core, "quad_sum_w_glu")
    (o_in,) = _run_exchanges([_Pair([gw_in])], "rs_pair_w_in")
    p_in = _pair_add(gw_in, o_in, core, "pair_add_w_in")
    q_in, g_out, g_glu = _run_exchanges([_Chip([p_in]), _Share([h_out, h_glu])], "rs_chip_w_in")
    h_in = _quad_sum(q_in, core, "quad_sum_w_in")
    (g_in,) = _run_exchanges([_Share([h_in])], "rs_share_w_in")
    shard_grads = {"w_in": g_in, "w_glu": g_glu, "w_out": g_out, "w_up": g_up, "w_down": g_down}

    out = {q: {} for q in ("grad", "delta", "new_m", "new_v")}
    for k in BIG:
        out["grad"][k] = shard_grads[k][None]
        delta, new_m, new_v = _adamw(w[k], shard_grads[k], m[k], v[k], "adamw_" + k)
        out["delta"][k], out["new_m"][k], out["new_v"][k] = delta[None], new_m[None], new_v[None]
    form = lambda d: {k: _kernel_form(k, d[k]) for k in SMALL}
    loss, res = _adamw_small(tiles, rows, form(w_dev), form(m_dev), form(v_dev))
    for q, d in zip(("grad", "delta", "new_m", "new_v"), res):
        out[q].update({k: _param_form(k, d[k]) for k in SMALL})
    flat = [loss[0, 0], grad_x[None]]
    for q in ("grad", "delta", "new_m", "new_v"):
        flat += [out[q][k] for k in WEIGHTS]
    return tuple(flat)
```
